```python
import math
import jax
import jax.numpy as jnp
from jax import lax
import numpy as np

D_MODEL = 1024
BATCH = 32
SEQ = 256
DEPTH = 4
DEC_BATCH = 8
DEC_SEQ = 4096
PAST_LEN = 512

GRID_W = 64
HEAD_DIM = 64
ATTN_SCALE = HEAD_DIM ** -0.5
ATTN_BLOCK = 128
NEG_INF = -1e30
LN_EPS = 1e-5
ALPHA = (2 * DEPTH) ** 0.25
BETA = (8 * DEPTH) ** -0.25
ROPE_BASE = 10000.0

RNN_WIDTH = 512
RNN_BLOCKS = 8
RNN_BLOCK = RNN_WIDTH // RNN_BLOCKS
CONV_W = 4
CONV_PAD_LEFT = 1
RGLRU_C = 8.0
NAT_HEADS = 8
NAT_WIDTH = NAT_HEADS * HEAD_DIM
NAT_KR = 8
NAT_KC = 16
NAT_QCB = 16
NAT_KCB = 2 * NAT_KC
SWA_HEADS = 8
SWA_KV_HEADS = 2
SWA_GROUP = SWA_HEADS // SWA_KV_HEADS
SWA_WINDOW = 128
SWA_BLOCK = 128
SWA_SPAN = SWA_BLOCK + 2 * SWA_WINDOW
FNET_GROUPS = 4
FNET_WIDTH = 512
FNET_GC = FNET_WIDTH // FNET_GROUPS
N_BRANCH = 4
N_EXPERTS = 16
N_EXPERT_GROUPS = 4
EXPERTS_PER_GROUP = N_EXPERTS // N_EXPERT_GROUPS
TOP_K = 2
D_EXPERT = 512

IN_SPLITS = (RNN_WIDTH, RNN_WIDTH, NAT_WIDTH, NAT_WIDTH, NAT_WIDTH,
             SWA_HEADS * HEAD_DIM, SWA_KV_HEADS * HEAD_DIM, SWA_KV_HEADS * HEAD_DIM,
             FNET_WIDTH, N_BRANCH * D_MODEL)
IN_WIDTH = sum(IN_SPLITS)
IN_SPLIT_POINTS = [sum(IN_SPLITS[:i + 1]) for i in range(len(IN_SPLITS) - 1)]
BRANCH_W = (RNN_WIDTH, NAT_WIDTH, SWA_HEADS * HEAD_DIM, FNET_WIDTH)

kernel_name = 'hybrid_diffusion_backbone_step'

F32 = jnp.float32


def layer_norm(x):
    xf = x.astype(F32)
    mu = jnp.mean(xf, -1, keepdims=True)
    var = jnp.mean(jnp.square(xf - mu), -1, keepdims=True)
    return ((xf - mu) * lax.rsqrt(var + LN_EPS)).astype(x.dtype)


def post_norm(x, delta, g, b):
    return layer_norm(ALPHA * x + delta) * g + b


def softmax_with_sink(s, sink):
    m = jnp.maximum(jnp.max(s, -1, keepdims=True), sink)
    e = jnp.exp(s - m)
    return e / (jnp.sum(e, -1, keepdims=True) + jnp.exp(sink - m))


def centred_dwconv(x, w, b):
    T = x.shape[1]
    xp = jnp.pad(x, ((0, 0), (CONV_PAD_LEFT, CONV_W - 1 - CONV_PAD_LEFT), (0, 0)))
    y = b
    for i in range(CONV_W):
        y = y + w[i] * xp[:, i:i + T]
    return y


def block_diag_linear(x, w, b):
    xb = x.reshape(x.shape[:-1] + (RNN_BLOCKS, RNN_BLOCK))
    return jnp.einsum('btni,nio->btno', xb, w).reshape(x.shape) + b


def linear_recurrence_op(e1, e2):
    a1, b1 = e1
    a2, b2 = e2
    return a1 * a2, a2 * b1 + b2


def rglru_scan(x, w_a, b_a, w_x, b_x, lam, h0, reverse):
    r = jax.nn.sigmoid(block_diag_linear(x, w_a, b_a)).astype(F32)
    i = jax.nn.sigmoid(block_diag_linear(x, w_x, b_x)).astype(F32)
    log_a = -RGLRU_C * r * jax.nn.softplus(-lam.astype(F32))
    a = jnp.exp(log_a)
    u = jnp.sqrt(-jnp.expm1(2.0 * log_a)) * (i * x.astype(F32))
    edge = -1 if reverse else 0
    u = u.at[:, edge].add(a[:, edge] * h0.astype(F32))
    _, h = lax.associative_scan(linear_recurrence_op, (a, u), axis=1, reverse=reverse)
    return h


def rglru_mixer(xa, ga, p, l, h0_fwd, h0_bwd):
    xc = centred_dwconv(xa, p['rg_conv_w'][l], p['rg_conv_b'][l])
    hf = rglru_scan(xc, p['rg_wa'][l, 0], p['rg_ba'][l, 0], p['rg_wx'][l, 0], p['rg_bx'][l, 0],
                    p['rg_lambda'][l, 0], h0_fwd, False)
    hb = rglru_scan(xc, p['rg_wa'][l, 1], p['rg_ba'][l, 1], p['rg_wx'][l, 1], p['rg_bx'][l, 1],
                    p['rg_lambda'][l, 1], h0_bwd, True)
    y = ((hf + hb) * jax.nn.gelu(ga.astype(F32))).astype(xa.dtype)
    return y, hf[:, -1].astype(xa.dtype), hb[:, 0].astype(xa.dtype)


def context_attention(q, k, v, sink):
    B, T, KV, G, dh = q.shape
    nb = T // ATTN_BLOCK
    qb = jnp.moveaxis(q.reshape(B, nb, ATTN_BLOCK, KV, G, dh), 1, 0)

    def one_block(q_i):
        s = jnp.einsum('bqkgd,bskd->bkgqs', q_i, k).astype(F32) * ATTN_SCALE
        prob = jax.nn.softmax(s, axis=-1) if sink is None else softmax_with_sink(s, sink)
        return jnp.einsum('bkgqs,bskd->bqkgd', prob.astype(v.dtype), v)

    o = lax.map(one_block, qb)
    return jnp.moveaxis(o, 0, 1).reshape(B, T, KV * G * dh)


def nat_latent(q, k, v, ck, cv, rpb):
    B, T, H, dh = q.shape
    rows = T // GRID_W
    kr = min(NAT_KR, rows)
    ncb = GRID_W // NAT_QCB
    qcol = np.arange(GRID_W).reshape(ncb, NAT_QCB)
    cstart = np.clip(qcol - NAT_KC // 2, 0, GRID_W - NAT_KC)
    kcol = (np.clip(np.arange(ncb) * NAT_QCB - NAT_KC // 2, 0, GRID_W - NAT_KCB)[:, None]
            + np.arange(NAT_KCB)[None, :])
    col_ok = (kcol[:, None, :] >= cstart[:, :, None]) & (kcol[:, None, :] < cstart[:, :, None] + NAT_KC)
    col_idx = np.clip(kcol[:, None, :] - qcol[:, :, None] + NAT_KC - 1, 0, 2 * NAT_KC - 2)
    qrow = np.arange(rows)
    rstart = np.clip(qrow - NAT_KR // 2, 0, rows - kr)
    row_idx = rstart[:, None] + np.arange(kr)[None, :] - qrow[:, None] + NAT_KR - 1

    kblk = k.reshape(B, rows, GRID_W, H, dh)[:, :, kcol]
    vblk = v.reshape(B, rows, GRID_W, H, dh)[:, :, kcol]
    rpb_c = rpb[:, :, col_idx].astype(F32)
    mask = jnp.asarray(col_ok)[None, None, :, :, None, :]
    qg = jnp.moveaxis(q.reshape(B, rows, ncb, NAT_QCB, H, dh), 1, 0)
    n_loc = kr * NAT_KCB

    def one_row(args):
        q_r, rs, roff = args
        k_r = lax.dynamic_slice_in_dim(kblk, rs, kr, axis=1)
        v_r = lax.dynamic_slice_in_dim(vblk, rs, kr, axis=1)
        bias = jnp.transpose(rpb_c[:, roff], (0, 2, 3, 1, 4))
        s_loc = jnp.einsum('bjqhd,bijkhd->bhjqik', q_r, k_r).astype(F32) * ATTN_SCALE + bias
        s_loc = jnp.where(mask, s_loc, NEG_INF).reshape(B, H, ncb, NAT_QCB, n_loc)
        s_ctx = jnp.einsum('bjqhd,bshd->bhjqs', q_r, ck).astype(F32) * ATTN_SCALE
        prob = jax.nn.softmax(jnp.concatenate([s_loc, s_ctx], -1), axis=-1).astype(v.dtype)
        p_loc = prob[..., :n_loc].reshape(B, H, ncb, NAT_QCB, kr, NAT_KCB)
        return (jnp.einsum('bhjqik,bijkhd->bjqhd', p_loc, v_r)
                + jnp.einsum('bhjqs,bshd->bjqhd', prob[..., n_loc:], cv))

    o = lax.map(one_row, (qg, jnp.asarray(rstart, jnp.int32), jnp.asarray(row_idx, jnp.int32)))
    return jnp.moveaxis(o, 0, 1).reshape(B, T, H * dh)


def swa_latent(q, k, v, ck, cv, sink):
    B, T, KV, G, dh = q.shape
    nb = T // SWA_BLOCK
    pad = ((0, 0), (SWA_WINDOW, SWA_WINDOW), (0, 0), (0, 0))
    kp = jnp.pad(k, pad)
    vp = jnp.pad(v, pad)
    qb = jnp.moveaxis(q.reshape(B, nb, SWA_BLOCK, KV, G, dh), 1, 0)

    def one_block(args):
        q_i, blk = args
        start = blk * SWA_BLOCK
        k_i = lax.dynamic_slice_in_dim(kp, start, SWA_SPAN, axis=1)
        v_i = lax.dynamic_slice_in_dim(vp, start, SWA_SPAN, axis=1)
        kpos = start - SWA_WINDOW + jnp.arange(SWA_SPAN)
        qpos = start + jnp.arange(SWA_BLOCK)
        ok = ((jnp.abs(kpos[None, :] - qpos[:, None]) <= SWA_WINDOW)
              & (kpos >= 0)[None, :] & (kpos < T)[None, :])
        s_loc = jnp.where(ok, jnp.einsum('bqkgd,bskd->bkgqs', q_i, k_i).astype(F32) * ATTN_SCALE, NEG_INF)
        s_ctx = jnp.einsum('bqkgd,bskd->bkgqs', q_i, ck).astype(F32) * ATTN_SCALE
        prob = softmax_with_sink(jnp.concatenate([s_loc, s_ctx], -1), sink).astype(v.dtype)
        return (jnp.einsum('bkgqs,bskd->bqkgd', prob[..., :SWA_SPAN], v_i)
                + jnp.einsum('bkgqs,bskd->bqkgd', prob[..., SWA_SPAN:], cv))

    o = lax.map(one_block, (qb, jnp.arange(nb)))
    return jnp.moveaxis(o, 0, 1).reshape(B, T, KV * G * dh)


def rope_2d(x):
    T = x.shape[1]
    t = jnp.arange(T)
    nf = HEAD_DIM // 4
    half = HEAD_DIM // 2
    inv = ROPE_BASE ** (-jnp.arange(nf, dtype=F32) / nf)
    bshape = (T,) + (1,) * (x.ndim - 3) + (nf,)
    xf = x.astype(F32)

    def rot(xa, pos):
        ang = (pos.astype(F32)[:, None] * inv).reshape(bshape)
        cos, sin = jnp.cos(ang), jnp.sin(ang)
        x1, x2 = xa[..., :nf], xa[..., nf:]
        return jnp.concatenate([x1 * cos - x2 * sin, x1 * sin + x2 * cos], -1)

    out = jnp.concatenate([rot(xf[..., :half], t // GRID_W), rot(xf[..., half:], t % GRID_W)], -1)
    return out.astype(x.dtype)


def fourier_mix(xf):
    B, T, _ = xf.shape
    g = xf.reshape(B, T, FNET_GROUPS, FNET_GC).astype(F32)
    y = jnp.fft.fft2(g, axes=(1, 3), norm='ortho').real
    return y.reshape(B, T, FNET_WIDTH).astype(xf.dtype)


def mixing_sublayer(h, p, l, ctx):
    B, T, _ = h.shape
    u = jnp.einsum('btd,dn->btn', h, p['w_in'][l])
    xa, ga, nq, nk, nv, sq, sk, sv, xf, gl = jnp.split(u, IN_SPLIT_POINTS, axis=-1)
    nq = nq.reshape(B, T, NAT_HEADS, 1, HEAD_DIM)
    nk = nk.reshape(B, T, NAT_HEADS, HEAD_DIM)
    nv = nv.reshape(B, T, NAT_HEADS, HEAD_DIM)
    sq = sq.reshape(B, T, SWA_KV_HEADS, SWA_GROUP, HEAD_DIM)
    sk = sk.reshape(B, T, SWA_KV_HEADS, HEAD_DIM)
    sv = sv.reshape(B, T, SWA_KV_HEADS, HEAD_DIM)
    sink = p['swa_sink'][l].astype(F32).reshape(1, SWA_KV_HEADS, SWA_GROUP, 1, 1)
    if ctx is None:
        h0 = jnp.zeros((B, RNN_WIDTH), h.dtype)
        ya, hf, hb = rglru_mixer(xa, ga, p, l, h0, h0)
        yb = context_attention(nq, nk, nv, None)
        yc = context_attention(sq, sk, sv, sink)
        new_ctx = (nk, nv, sk, sv, jnp.stack([hf, hb], axis=1))
    else:
        ck_nat, cv_nat, ck_swa, cv_swa, st = ctx
        ya, _, _ = rglru_mixer(xa, ga, p, l, st[:, 0], st[:, 1])
        yb = nat_latent(nq[:, :, :, 0], nk, nv, ck_nat, cv_nat, p['nat_rpb'][l])
        yc = swa_latent(rope_2d(sq), rope_2d(sk), sv, ck_swa, cv_swa, sink)
        new_ctx = None
    yd = fourier_mix(xf)
    merged = None
    for b, y in enumerate((ya, yb, yc, yd)):
        term = (jax.nn.sigmoid(gl[..., b * D_MODEL:(b + 1) * D_MODEL])
                * jnp.einsum('btc,cd->btd', y, p['w_branch'][l, b]))
        merged = term if merged is None else merged + term
    return jnp.einsum('btd,de->bte', merged, p['w_out'][l]), new_ctx


def moe_ffn(h, p, l):
    B, T, _ = h.shape
    scores = jax.nn.softmax(jnp.einsum('btd,de->bte', h, p['w_router']).astype(F32), axis=-1)
    biased = scores + p['router_bias'].astype(F32)
    grp = biased.reshape(B, T, N_EXPERT_GROUPS, EXPERTS_PER_GROUP)
    grp_score = jnp.sum(lax.top_k(grp, TOP_K)[0], -1)
    sel_g = jnp.argmax(grp_score, -1)
    in_grp = jnp.take_along_axis(grp, sel_g[..., None, None], axis=2)[..., 0, :]
    _, local_idx = lax.top_k(in_grp, TOP_K)
    idx = sel_g[..., None] * EXPERTS_PER_GROUP + local_idx
    w = jnp.take_along_axis(scores, idx, -1)
    w = w / jnp.sum(w, -1, keepdims=True)
    comb = jnp.sum(jax.nn.one_hot(idx, N_EXPERTS, dtype=F32) * w[..., None], axis=2).astype(h.dtype)
    out = jnp.zeros_like(h)
    for e in range(N_EXPERTS):
        a = jnp.einsum('btd,df->btf', h, p['w1'][l, e])
        g = jnp.einsum('btd,df->btf', h, p['w3'][l, e])
        y = jnp.einsum('btf,fd->btd', jax.nn.silu(a) * g, p['w2'][l, e])
        out = out + comb[..., e:e + 1] * y
    return out


def trunk_layer(x, cvec, p, l, ctx):
    mod = jnp.einsum('...d,de->...e', jax.nn.silu(cvec), p['w_ada'][l]) + p['b_ada'][l]
    sh1, sc1, g1, sh2, sc2, g2 = jnp.split(mod, 6, axis=-1)
    h = layer_norm(x) * (1 + sc1) + sh1
    out, new_ctx = mixing_sublayer(h, p, l, ctx)
    x = post_norm(x, g1 * out, p['ln_g'][l, 0], p['ln_b'][l, 0])
    h = layer_norm(x) * (1 + sc2) + sh2
    x = post_norm(x, g2 * moe_ffn(h, p, l), p['ln_g'][l, 1], p['ln_b'][l, 1])
    return x, new_ctx


def setup_inputs(seed: int = 0) -> dict:
    key = jax.random.key(seed)
    ks = jax.random.split(key, 32)

    def nrm(k, shape, std):
        return std * jax.random.normal(k, shape, F32)

    a0 = jax.random.uniform(ks[18], (DEPTH, 2, RNN_WIDTH), F32, 0.9, 0.999)
    return {
        'x_prompt': nrm(ks[0], (BATCH, SEQ, D_MODEL), 1.0),
        'x_sample': nrm(ks[1], (DEC_BATCH, DEC_SEQ, D_MODEL), 1.0),
        'c': nrm(ks[2], (DEC_BATCH, D_MODEL), 1.0),
        'cache_nat_k': nrm(ks[3], (DEC_BATCH, DEPTH, PAST_LEN, NAT_HEADS, HEAD_DIM), 1.0),
        'cache_nat_v': nrm(ks[4], (DEC_BATCH, DEPTH, PAST_LEN, NAT_HEADS, HEAD_DIM), 1.0),
        'cache_swa_k': nrm(ks[5], (DEC_BATCH, DEPTH, PAST_LEN, SWA_KV_HEADS, HEAD_DIM), 1.0),
        'cache_swa_v': nrm(ks[6], (DEC_BATCH, DEPTH, PAST_LEN, SWA_KV_HEADS, HEAD_DIM), 1.0),
        'state_rglru': nrm(ks[7], (DEC_BATCH, DEPTH, 2, RNN_WIDTH), 0.5),
        'c_ctx': nrm(ks[8], (D_MODEL,), 1.0),
        'w_ada': nrm(ks[9], (DEPTH, D_MODEL, 6 * D_MODEL), D_MODEL ** -0.5),
        'b_ada': nrm(ks[10], (DEPTH, 6 * D_MODEL), 0.02),
        'w_in': nrm(ks[11], (DEPTH, D_MODEL, IN_WIDTH), D_MODEL ** -0.5),
        'rg_conv_w': nrm(ks[12], (DEPTH, CONV_W, RNN_WIDTH), CONV_W ** -0.5),
        'rg_conv_b': nrm(ks[13], (DEPTH, RNN_WIDTH), 0.02),
        'rg_wa': nrm(ks[14], (DEPTH, 2, RNN_BLOCKS, RNN_BLOCK, RNN_BLOCK), RNN_BLOCK ** -0.5),
        'rg_ba': nrm(ks[15], (DEPTH, 2, RNN_WIDTH), 0.02),
        'rg_wx': nrm(ks[16], (DEPTH, 2, RNN_BLOCKS, RNN_BLOCK, RNN_BLOCK), RNN_BLOCK ** -0.5),
        'rg_bx': nrm(ks[17], (DEPTH, 2, RNN_WIDTH), 0.02),
        'rg_lambda': jnp.log(a0) - jnp.log1p(-a0),
        'nat_rpb': nrm(ks[19], (DEPTH, NAT_HEADS, 2 * NAT_KR - 1, 2 * NAT_KC - 1), 0.1),
        'swa_sink': nrm(ks[20], (DEPTH, SWA_HEADS), 1.0),
        'w_branch': nrm(ks[21], (DEPTH, N_BRANCH, BRANCH_W[0], D_MODEL), BRANCH_W[0] ** -0.5),
        'w_out': nrm(ks[22], (DEPTH, D_MODEL, D_MODEL), BETA * D_MODEL ** -0.5),
        'ln_g': 1.0 + nrm(ks[23], (DEPTH, 2, D_MODEL), 0.02),
        'ln_b': nrm(ks[24], (DEPTH, 2, D_MODEL), 0.02),
        'w_router': nrm(ks[25], (D_MODEL, N_EXPERTS), D_MODEL ** -0.5),
        'router_bias': nrm(ks[26], (N_EXPERTS,), 0.01),
        'w1': nrm(ks[27], (DEPTH, N_EXPERTS, D_MODEL, D_EXPERT), D_MODEL ** -0.5),
        'w3': nrm(ks[28], (DEPTH, N_EXPERTS, D_MODEL, D_EXPERT), D_MODEL ** -0.5),
        'w2': nrm(ks[29], (DEPTH, N_EXPERTS, D_EXPERT, D_MODEL), BETA * D_EXPERT ** -0.5),
    }


def reference(x_prompt, x_sample, c, cache_nat_k, cache_nat_v, cache_swa_k, cache_swa_v,
              state_rglru, c_ctx, w_ada, b_ada, w_in, rg_conv_w, rg_conv_b, rg_wa, rg_ba,
              rg_wx, rg_bx, rg_lambda, nat_rpb, swa_sink, w_branch, w_out, ln_g, ln_b,
              w_router, router_bias, w1, w3, w2):
    p = {'w_ada': w_ada, 'b_ada': b_ada, 'w_in': w_in, 'rg_conv_w': rg_conv_w,
         'rg_conv_b': rg_conv_b, 'rg_wa': rg_wa, 'rg_ba': rg_ba, 'rg_wx': rg_wx, 'rg_bx': rg_bx,
         'rg_lambda': rg_lambda, 'nat_rpb': nat_rpb, 'swa_sink': swa_sink, 'w_branch': w_branch,
         'w_out': w_out, 'ln_g': ln_g, 'ln_b': ln_b, 'w_router': w_router,
         'router_bias': router_bias, 'w1': w1, 'w3': w3, 'w2': w2}

    y_prompt = x_prompt
    nk_l, nv_l, sk_l, sv_l, st_l = [], [], [], [], []
    for l in range(DEPTH):
        y_prompt, (nk, nv, sk, sv, st) = trunk_layer(y_prompt, c_ctx[None, None, :], p, l, None)
        nk_l.append(nk)
        nv_l.append(nv)
        sk_l.append(sk)
        sv_l.append(sv)
        st_l.append(st)
    new_nat_k = jnp.stack(nk_l, axis=1)
    new_nat_v = jnp.stack(nv_l, axis=1)
    new_swa_k = jnp.stack(sk_l, axis=1)
    new_swa_v = jnp.stack(sv_l, axis=1)
    new_state_rglru = jnp.stack(st_l, axis=1)

    y_sample = x_sample
    for l in range(DEPTH):
        ctx = (cache_nat_k[:, l], cache_nat_v[:, l], cache_swa_k[:, l], cache_swa_v[:, l], state_rglru[:, l])
        y_sample, _ = trunk_layer(y_sample, c[:, None, :], p, l, ctx)

    return (y_prompt, y_sample, new_nat_k, new_nat_v, new_swa_k, new_swa_v, new_state_rglru)
```

```python
import functools
import math

import jax
import jax.numpy as jnp
import numpy as np
from jax import lax
from jax.experimental import pallas as pl
from jax.experimental.pallas import tpu as pltpu

F32 = jnp.float32
BF16 = jnp.bfloat16

D_MODEL = 1024
DEPTH = 4
GRID_W = 64
HEAD_DIM = 64
ATTN_SCALE = HEAD_DIM ** -0.5
NEG_INF = -1e30
LN_EPS = 1e-5
ALPHA = (2 * DEPTH) ** 0.25
ROPE_BASE = 10000.0
RNN_WIDTH = 512
RNN_BLOCKS = 8
RNN_BLOCK = RNN_WIDTH // RNN_BLOCKS
CONV_W = 4
RGLRU_C = 8.0
NAT_HEADS = 8
NAT_KR = 8
NAT_KC = 16
SWA_HEADS = 8
SWA_KV_HEADS = 2
SWA_WINDOW = 128
SWA_BLOCK = 128
SWA_SPAN = SWA_BLOCK + 2 * SWA_WINDOW
FNET_GROUPS = 4
FNET_WIDTH = 512
FNET_GC = FNET_WIDTH // FNET_GROUPS
N_BRANCH = 4
N_EXPERTS = 16
N_EXPERT_GROUPS = 4
EXPERTS_PER_GROUP = N_EXPERTS // N_EXPERT_GROUPS
D_EXPERT = 512
W512 = 512
GATE_OFF = 3840

LANES = 128
SUBLANES = 8
VMEM_LIMIT = 56 * 1024 * 1024


def _cparams(*sem):
    return pltpu.CompilerParams(dimension_semantics=sem, vmem_limit_bytes=VMEM_LIMIT)


def _const_spec(shape):
    nd = len(shape)
    return pl.BlockSpec(shape, lambda *_: (0,) * nd, pipeline_mode=pl.Buffered(1))


def _ln(x):
    mu = jnp.mean(x, -1, keepdims=True)
    xc = x - mu
    var = jnp.mean(xc * xc, -1, keepdims=True)
    return xc * lax.rsqrt(var + LN_EPS)


def _dot(a, b):
    return jnp.dot(a, b, preferred_element_type=F32)


def _dot_nt(a, b):
    return lax.dot_general(a, b, (((1,), (1,)), ((), ())), preferred_element_type=F32)


ADA_ROWS = 16
ADA_TN = 1024


def _ada_kernel(c_ref, w_ref, b_ref, o_ref):
    cv = c_ref[...]
    s = (cv * jax.nn.sigmoid(cv)).astype(BF16)
    o_ref[0] = _dot(s, w_ref[0].astype(BF16)) + b_ref[0]


def _ada_call(cv, w_ada, b_ada):
    n = w_ada.shape[-1]
    return pl.pallas_call(
        _ada_kernel,
        out_shape=jax.ShapeDtypeStruct((DEPTH, ADA_ROWS, n), F32),
        grid=(DEPTH, n // ADA_TN),
        in_specs=[
            pl.BlockSpec((ADA_ROWS, D_MODEL), lambda l, j: (0, 0)),
            pl.BlockSpec((1, D_MODEL, ADA_TN), lambda l, j: (l, 0, j)),
            pl.BlockSpec((1, 1, ADA_TN), lambda l, j: (l, 0, j)),
        ],
        out_specs=pl.BlockSpec((1, ADA_ROWS, ADA_TN), lambda l, j: (l, 0, j)),
        compiler_params=_cparams("parallel", "parallel"),
        name="ada",
    )(cv, w_ada, b_ada.reshape(DEPTH, 1, n))


def _rope(u, cos, sin):
    lane = lax.broadcasted_iota(jnp.int32, cos.shape, 1)
    first = (lane & 31) < 16
    outs = []
    for j in range(u.shape[1] // LANES):
        s = u[:, j * LANES:(j + 1) * LANES]
        partner = jnp.where(first, pltpu.roll(s, LANES - 16, 1), pltpu.roll(s, 16, 1))
        outs.append(s * cos + partner * sin)
    return outs[0] if len(outs) == 1 else jnp.concatenate(outs, -1)


def _pre_kernel(*refs, plan, rope):
    if rope:
        x_ref, mod_ref, w_ref, cos_ref, sin_ref = refs[:5]
        outs = refs[5:]
        cos, sin = cos_ref[...], sin_ref[...]
    else:
        x_ref, mod_ref, w_ref = refs[:3]
        outs = refs[3:]
    m = mod_ref[0]
    h = (_ln(x_ref[...]) * (1.0 + m[1:2]) + m[0:1]).astype(BF16)
    off = 0
    for (width, _, scale, do_rope), o_ref in zip(plan, outs):
        u = _dot(h, w_ref[:, off:off + width])
        if do_rope:
            u = _rope(u, cos, sin)
        if scale != 1.0:
            u = u * scale
        o_ref[...] = u.astype(o_ref.dtype)
        off += width


def _pre_call(x, mod, w, plan, T, TM, rope_tabs):
    N = x.shape[0]
    tpb = T // TM
    rope = rope_tabs is not None
    wtot = sum(p[0] for p in plan)
    assert w.shape == (D_MODEL, wtot)
    in_specs = [
        pl.BlockSpec((TM, D_MODEL), lambda i: (i, 0)),
        pl.BlockSpec((1, SUBLANES, D_MODEL), lambda i: (i // tpb, 0, 0)),
        _const_spec((D_MODEL, wtot)),
    ]
    args = [x, mod, w]
    if rope:
        in_specs += [pl.BlockSpec((TM, LANES), lambda i: (i % tpb, 0))] * 2
        args += list(rope_tabs)
    return pl.pallas_call(
        functools.partial(_pre_kernel, plan=plan, rope=rope),
        out_shape=[jax.ShapeDtypeStruct((N, p[0]), p[1]) for p in plan],
        grid=(N // TM,),
        in_specs=in_specs,
        out_specs=[pl.BlockSpec((TM, p[0]), lambda i: (i, 0)) for p in plan],
        compiler_params=_cparams("parallel"),
        name="pre",
    )(*args)


RG_CW = 128
RG_TCH = 256


def _expm1(z):
    u = jnp.exp(z)
    um1 = u - 1.0
    near = jnp.where(u == 1.0, z, um1 * z / jnp.log(jnp.where(u == 1.0, 0.5, u)))
    return jnp.where(z < -1.0, um1, near)


def _rglru_kernel(xa_ref, ga_ref, cw_ref, cb_ref, wa_ref, wx_ref, ba_ref, bx_ref, lam_ref, h0_ref,
                  y_ref, hfin_ref, xpad, a_f, u_f, a_b, u_b, *, T):
    cw_ = RG_CW
    zeros8 = jnp.zeros((SUBLANES, cw_), F32)
    xpad[0:SUBLANES, :] = zeros8
    xpad[T + SUBLANES:T + 2 * SUBLANES, :] = zeros8
    xpad[SUBLANES:T + SUBLANES, :] = xa_ref[0]
    cw = cw_ref[...]
    cb = cb_ref[...]
    lam = lam_ref[...]
    sp = jnp.maximum(-lam, 0.0) + jnp.log1p(jnp.exp(-jnp.abs(lam)))
    ba = ba_ref[...]
    bx = bx_ref[...]
    h0 = h0_ref[0]

    def chunk(c, carry):
        base = pl.multiple_of(c * RG_TCH, RG_TCH)
        xw = xpad[pl.ds(base, RG_TCH + 2 * SUBLANES), :]
        xc = cb
        for i in range(CONV_W):
            xc = xc + cw[i:i + 1] * xw[SUBLANES - 1 + i:SUBLANES - 1 + i + RG_TCH]
        xcb = xc.astype(BF16)
        for d, (a_s, u_s) in enumerate(((a_f, u_f), (a_b, u_b))):
            r = jax.nn.sigmoid(_dot(xcb, wa_ref[d]) + ba[d:d + 1])
            i_g = jax.nn.sigmoid(_dot(xcb, wx_ref[d]) + bx[d:d + 1])
            log_a = -RGLRU_C * r * sp[d:d + 1]
            a_s[pl.ds(base, RG_TCH), :] = jnp.exp(log_a)
            u_s[pl.ds(base, RG_TCH), :] = jnp.sqrt(-_expm1(2.0 * log_a)) * (i_g * xc)
        return carry

    lax.fori_loop(0, T // RG_TCH, chunk, 0)

    row = lax.broadcasted_iota(jnp.int32, (SUBLANES, cw_), 0)

    def block_scan(a, u, reverse):
        for dd in (1, 2, 4):
            sh = SUBLANES - dd if reverse else dd
            a_n = pltpu.roll(a, sh, 0)
            u_n = pltpu.roll(u, sh, 0)
            ok = (row < SUBLANES - dd) if reverse else (row >= dd)
            u = jnp.where(ok, u + a * u_n, u)
            a = jnp.where(ok, a * a_n, a)
        return a, u

    nblk = T // SUBLANES

    def bwd(i, c):
        base = pl.multiple_of((nblk - 1 - i) * SUBLANES, SUBLANES)
        a, u = block_scan(a_b[pl.ds(base, SUBLANES), :], u_b[pl.ds(base, SUBLANES), :], True)
        h = u + a * c
        u_b[pl.ds(base, SUBLANES), :] = h
        return jnp.broadcast_to(h[0:1, :], (SUBLANES, cw_))

    c_b = lax.fori_loop(0, nblk, bwd, jnp.broadcast_to(h0[1:2, :], (SUBLANES, cw_)))

    def fwd(i, c):
        base = pl.multiple_of(i * 2 * SUBLANES, 2 * SUBLANES)
        hs = []
        for k in range(2):
            lo = base + k * SUBLANES
            a, u = block_scan(a_f[pl.ds(lo, SUBLANES), :], u_f[pl.ds(lo, SUBLANES), :], False)
            h = u + a * c
            c = jnp.broadcast_to(h[SUBLANES - 1:SUBLANES, :], (SUBLANES, cw_))
            hs.append(h)
        hf = jnp.concatenate(hs, 0)
        hb = u_b[pl.ds(base, 2 * SUBLANES), :]
        g = ga_ref[0, pl.ds(base, 2 * SUBLANES), :]
        y_ref[0, pl.ds(base, 2 * SUBLANES), :] = ((hf + hb) * jax.nn.gelu(g)).astype(y_ref.dtype)
        return c

    c_f = lax.fori_loop(0, nblk // 2, fwd, jnp.broadcast_to(h0[0:1, :], (SUBLANES, cw_)))
    hfin_ref[0] = jnp.where(row == 0, c_f, jnp.where(row == 1, c_b, 0.0))


def _rglru_call(xa, ga, cw, cb, wa, wx, ba, bx, lam, h0):
    B, T, _ = xa.shape
    nj = RNN_WIDTH // RG_CW
    seq = pl.BlockSpec((1, T, RG_CW), lambda b, j: (b, 0, j))
    vec8 = pl.BlockSpec((SUBLANES, RG_CW), lambda b, j: (0, j))
    wsp = pl.BlockSpec((2, RG_CW, RG_CW), lambda b, j: (0, j, j))
    st = pl.BlockSpec((1, SUBLANES, RG_CW), lambda b, j: (b, 0, j))
    return pl.pallas_call(
        functools.partial(_rglru_kernel, T=T),
        out_shape=[jax.ShapeDtypeStruct((B, T, RNN_WIDTH), BF16),
                   jax.ShapeDtypeStruct((B, SUBLANES, RNN_WIDTH), F32)],
        grid=(B, nj),
        in_specs=[seq, seq, vec8, pl.BlockSpec((1, RG_CW), lambda b, j: (0, j)),
                  wsp, wsp, vec8, vec8, vec8, st],
        out_specs=[seq, st],
        scratch_shapes=[pltpu.VMEM((T + 2 * SUBLANES, RG_CW), F32)]
        + [pltpu.VMEM((T, RG_CW), F32)] * 4,
        compiler_params=_cparams("parallel", "parallel"),
        name="rglru",
    )(xa, ga, cw, cb, wa, wx, ba, bx, lam, h0)


def _attend2(q2, srcs, sink_col):
    M = q2.shape[0]
    lo = lax.broadcasted_iota(jnp.int32, (M, LANES), 1) < HEAD_DIM
    zero = jnp.zeros_like(q2)
    qs = jnp.concatenate([jnp.where(lo, q2, zero), jnp.where(lo, zero, q2)], 0)
    ss = []
    for k, _, bias in srcs:
        s = _dot_nt(qs, k)
        ss.append(s if bias is None else s + bias)
    m = jnp.max(ss[0], -1, keepdims=True)
    for s in ss[1:]:
        m = jnp.maximum(m, jnp.max(s, -1, keepdims=True))
    if sink_col is not None:
        m = jnp.maximum(m, sink_col)
    den = None
    o = None
    for s, (_, v, _) in zip(ss, srcs):
        e = jnp.exp(s - m)
        d_ = jnp.sum(e, -1, keepdims=True)
        o_ = _dot(e.astype(BF16), v)
        den = d_ if den is None else den + d_
        o = o_ if o is None else o + o_
    if sink_col is not None:
        den = den + jnp.exp(sink_col - m)
    o = o / den
    return jnp.where(lo, o[:M], o[M:])


def _sink_col(sink_ref, j, M):
    r = lax.broadcasted_iota(jnp.int32, (2 * M, 1), 0)
    return jnp.where(r < M, sink_ref[2 * j], sink_ref[2 * j + 1])


def _slab(j):
    return slice(j * LANES, (j + 1) * LANES)


def _attn_ctx_kernel(sink_ref, nq_ref, nk_ref, nv_ref, sq_ref, skd_ref, svd_ref, yb_ref, yc_ref):
    M = nq_ref.shape[1]
    for j in range(W512 // LANES):
        k2 = nk_ref[0, :, _slab(j)].astype(BF16)
        v2 = nv_ref[0, :, _slab(j)].astype(BF16)
        yb_ref[0, :, _slab(j)] = _attend2(nq_ref[0, :, _slab(j)], [(k2, v2, None)], None).astype(yb_ref.dtype)
        g = j // 2
        yc_ref[0, :, _slab(j)] = _attend2(
            sq_ref[0, :, _slab(j)], [(skd_ref[0, :, _slab(g)], svd_ref[0, :, _slab(g)], None)],
            _sink_col(sink_ref, j, M)).astype(yc_ref.dtype)


def _attn_ctx_call(sink, nq, nk, nv, sq, skd, svd):
    B, T, _ = nq.shape
    s512 = pl.BlockSpec((1, T, W512), lambda b: (b, 0, 0))
    s256 = pl.BlockSpec((1, T, 2 * LANES), lambda b: (b, 0, 0))
    return pl.pallas_call(
        _attn_ctx_kernel,
        out_shape=[jax.ShapeDtypeStruct((B, T, W512), BF16)] * 2,
        grid=(B,),
        in_specs=[pl.BlockSpec(memory_space=pltpu.SMEM), s512, s512, s512, s512, s256, s256],
        out_specs=[s512, s512],
        compiler_params=_cparams("parallel"),
        name="attn_ctx",
    )(sink, nq, nk, nv, sq, skd, svd)


NAT_NLOC = NAT_KR * GRID_W


def _nat_lat_kernel(q_ref, k_ref, v_ref, ck_ref, cv_ref, bias_ref, y_ref, *, rows):
    r = pl.program_id(1)
    rstart = jnp.clip(r - NAT_KR // 2, 0, rows - NAT_KR)
    d = r - rstart
    kbase = pl.multiple_of(rstart * GRID_W, GRID_W)
    for j in range(W512 // LANES):
        k2 = k_ref[0, pl.ds(kbase, NAT_NLOC), _slab(j)]
        v2 = v_ref[0, pl.ds(kbase, NAT_NLOC), _slab(j)]
        bias = bias_ref[j, d]
        o = _attend2(q_ref[0, :, _slab(j)],
                     [(k2, v2, bias), (ck_ref[0, 0, :, _slab(j)], cv_ref[0, 0, :, _slab(j)], None)], None)
        y_ref[0, :, _slab(j)] = o.astype(y_ref.dtype)


def _nat_lat_call(q, k, v, ck, cv, bias, l):
    B, T, _ = q.shape
    rows = T // GRID_W
    P = ck.shape[2]
    qs = pl.BlockSpec((1, GRID_W, W512), lambda b, r: (b, r, 0))
    full = pl.BlockSpec((1, T, W512), lambda b, r: (b, 0, 0))
    cs = pl.BlockSpec((1, 1, P, W512), lambda b, r: (b, l, 0, 0))
    return pl.pallas_call(
        functools.partial(_nat_lat_kernel, rows=rows),
        out_shape=jax.ShapeDtypeStruct((B, T, W512), BF16),
        grid=(B, rows),
        in_specs=[qs, full, full, cs, cs, _const_spec(bias.shape)],
        out_specs=qs,
        compiler_params=_cparams("parallel", "arbitrary"),
        name="nat_lat",
    )(q, k, v, ck, cv, bias)


def _swa_lat_kernel(sink_ref, q_ref, kd_ref, vd_ref, ckd_ref, cvd_ref, y_ref, *, T):
    blk = pl.program_id(1)
    M = SWA_BLOCK
    start = blk * SWA_BLOCK
    ks = pl.multiple_of(jnp.clip(start - SWA_WINDOW, 0, T - SWA_SPAN), SWA_BLOCK)
    kpos = ks + lax.broadcasted_iota(jnp.int32, (2 * M, SWA_SPAN), 1)
    qpos = start + (lax.broadcasted_iota(jnp.int32, (2 * M, SWA_SPAN), 0) & (M - 1))
    dist = kpos - qpos
    bias = jnp.where(dist > SWA_WINDOW, NEG_INF, jnp.where(dist < -SWA_WINDOW, NEG_INF, 0.0))
    for j in range(W512 // LANES):
        g = j // 2
        o = _attend2(q_ref[0, :, _slab(j)],
                     [(kd_ref[0, pl.ds(ks, SWA_SPAN), _slab(g)], vd_ref[0, pl.ds(ks, SWA_SPAN), _slab(g)], bias),
                      (ckd_ref[0, 0, :, _slab(g)], cvd_ref[0, 0, :, _slab(g)], None)],
                     _sink_col(sink_ref, j, M))
        y_ref[0, :, _slab(j)] = o.astype(y_ref.dtype)


def _swa_lat_call(sink, q, kd, vd, ckd, cvd, l):
    B, T, _ = q.shape
    P = ckd.shape[2]
    qs = pl.BlockSpec((1, SWA_BLOCK, W512), lambda b, i: (b, i, 0))
    full = pl.BlockSpec((1, T, 2 * LANES), lambda b, i: (b, 0, 0))
    cs = pl.BlockSpec((1, 1, P, 2 * LANES), lambda b, i: (b, l, 0, 0))
    return pl.pallas_call(
        functools.partial(_swa_lat_kernel, T=T),
        out_shape=jax.ShapeDtypeStruct((B, T, W512), BF16),
        grid=(B, T // SWA_BLOCK),
        in_specs=[pl.BlockSpec(memory_space=pltpu.SMEM), qs, full, full, cs, cs],
        out_specs=qs,
        compiler_params=_cparams("parallel", "arbitrary"),
        name="swa_lat",
    )(sink, q, kd, vd, ckd, cvd)


FN_N2 = 256
FN_GP = 2


def _fourier_kernel(x_ref, cc_ref, m_ref, y_ref, *scratch, n1, coef):
    gw = FN_GP * FNET_GC
    cc = cc_ref[...]
    for gp in range(FNET_GROUPS // FN_GP):
        parts = []
        for t1 in range(n1):
            ws = []
            for g in range(FN_GP):
                lo = t1 * FNET_WIDTH + (gp * FN_GP + g) * FNET_GC
                ws.append(_dot(x_ref[0, :, lo:lo + FNET_GC], cc))
            wr = jnp.concatenate([w[:, :FNET_GC] for w in ws], -1)
            wi = jnp.concatenate([w[:, FNET_GC:] for w in ws], -1)
            v = jnp.concatenate([wr, wi], 0).astype(BF16)
            b = _dot(m_ref[t1], v)
            if n1 == 1:
                y_ref[0, :, gp * gw:(gp + 1) * gw] = b.astype(y_ref.dtype)
            else:
                scratch[0][t1] = b
        if n1 > 1:
            bs = scratch[0]
            for k1 in range(n1):
                acc = None
                for t1 in range(n1):
                    c, s = coef[k1][t1]
                    for cf, lo in ((c, 0), (s, FN_N2)):
                        if cf == 0.0:
                            continue
                        blk = bs[t1, lo:lo + FN_N2, :]
                        term = blk if cf == 1.0 else (-blk if cf == -1.0 else cf * blk)
                        acc = term if acc is None else acc + term
                y_ref[0, k1 * FN_N2:(k1 + 1) * FN_N2, gp * gw:(gp + 1) * gw] = acc.astype(y_ref.dtype)


def _dft_consts(T):
    n1 = T // FN_N2
    j = np.arange(FNET_GC)
    ang = 2 * np.pi * np.outer(j, j) / FNET_GC
    sc = 1.0 / math.sqrt(FNET_GC)
    cc = np.concatenate([np.cos(ang) * sc, -np.sin(ang) * sc], 1)
    k2 = np.arange(FN_N2)[:, None]
    t2 = np.arange(FN_N2)[None, :]
    st = 1.0 / math.sqrt(T)
    mats = []
    for t1 in range(n1):
        th = 2 * np.pi * ((k2 * (n1 * t2 + t1)) % T) / T
        c, s = np.cos(th) * st, np.sin(th) * st
        top = np.concatenate([c, s], 1)
        mats.append(top if n1 == 1 else np.concatenate([top, np.concatenate([-s, c], 1)], 0))
    coef = []
    for k1 in range(n1):
        rowc = []
        for t1 in range(n1):
            q = (k1 * t1) % n1
            c, s = math.cos(2 * math.pi * q / n1), math.sin(2 * math.pi * q / n1)
            c = 0.0 if abs(c) < 1e-12 else (1.0 if abs(c - 1) < 1e-12 else (-1.0 if abs(c + 1) < 1e-12 else c))
            s = 0.0 if abs(s) < 1e-12 else (1.0 if abs(s - 1) < 1e-12 else (-1.0 if abs(s + 1) < 1e-12 else s))
            rowc.append((c, s))
        coef.append(tuple(rowc))
    return n1, jnp.asarray(cc, BF16), jnp.asarray(np.stack(mats), BF16), tuple(coef)


def _fourier_call(xf):
    B, T, _ = xf.shape
    n1, cc, mats, coef = _dft_consts(T)
    xv = xf.reshape(B, FN_N2, n1 * FNET_WIDTH)
    gw = FN_GP * FNET_GC
    scratch = [pltpu.VMEM((n1, 2 * FN_N2, gw), F32)] if n1 > 1 else []
    return pl.pallas_call(
        functools.partial(_fourier_kernel, n1=n1, coef=coef),
        out_shape=jax.ShapeDtypeStruct((B, T, FNET_WIDTH), BF16),
        grid=(B,),
        in_specs=[pl.BlockSpec((1, FN_N2, n1 * FNET_WIDTH), lambda b: (b, 0, 0)),
                  _const_spec(cc.shape), _const_spec(mats.shape)],
        out_specs=pl.BlockSpec((1, T, FNET_WIDTH), lambda b: (b, 0, 0)),
        scratch_shapes=scratch,
        compiler_params=_cparams("parallel"),
        name="fourier",
    )(xv, cc, mats)


def _route(scores, biased):
    one, zero = jnp.float32(1.0), jnp.float32(0.0)
    in2, gscore = [], []
    for g in range(N_EXPERT_GROUPS):
        vs = [biased[g * EXPERTS_PER_GROUP + j:g * EXPERTS_PER_GROUP + j + 1] for j in range(EXPERTS_PER_GROUP)]
        gs = None
        for j in range(EXPERTS_PER_GROUP):
            rank = None
            for i in range(EXPERTS_PER_GROUP):
                if i == j:
                    continue
                beats = (vs[i] >= vs[j]) if i < j else (vs[i] > vs[j])
                t = jnp.where(beats, one, zero)
                rank = t if rank is None else rank + t
            keep = jnp.where(rank < 2.0, one, zero)
            in2.append(keep)
            t = keep * vs[j]
            gs = t if gs is None else gs + t
        gscore.append(gs)
    rows = []
    for g in range(N_EXPERT_GROUPS):
        lost = None
        for i in range(N_EXPERT_GROUPS):
            if i == g:
                continue
            beats = (gscore[i] >= gscore[g]) if i < g else (gscore[i] > gscore[g])
            t = jnp.where(beats, one, zero)
            lost = t if lost is None else lost + t
        gsel = jnp.where(lost < 1.0, one, zero)
        for j in range(EXPERTS_PER_GROUP):
            e = g * EXPERTS_PER_GROUP + j
            rows.append(gsel * in2[e] * scores[e:e + 1])
    tot = rows[0]
    for rr in rows[1:]:
        tot = tot + rr
    return jnp.concatenate(rows, 0) / tot


def _merge_kernel(x_ref, mod_ref, ya_ref, yb_ref, yc_ref, yd_ref, wg_ref, wb_ref, wo_ref,
                  lng_ref, lnb_ref, wr_ref, rb_ref, x1_ref, h2_ref, comb_ref):
    x = x_ref[...]
    m = mod_ref[0]
    h = (_ln(x) * (1.0 + m[1:2]) + m[0:1]).astype(BF16)
    merged = None
    for b, y_ref in enumerate((ya_ref, yb_ref, yc_ref, yd_ref)):
        gate = jax.nn.sigmoid(_dot(h, wg_ref[:, b * D_MODEL:(b + 1) * D_MODEL]))
        term = gate * _dot(y_ref[...], wb_ref[b])
        merged = term if merged is None else merged + term
    out = _dot(merged.astype(BF16), wo_ref[...])
    x1 = _ln(ALPHA * x + m[2:3] * out) * lng_ref[...] + lnb_ref[...]
    x1_ref[...] = x1
    h2 = (_ln(x1) * (1.0 + m[4:5]) + m[3:4]).astype(BF16)
    h2_ref[...] = h2
    logits = _dot_nt(wr_ref[...], h2)
    e = jnp.exp(logits - jnp.max(logits, 0, keepdims=True))
    scores = e / jnp.sum(e, 0, keepdims=True)
    comb = _route(scores, scores + rb_ref[...])
    tm = x.shape[0]
    comb_t = jnp.concatenate([comb, jnp.zeros((LANES - N_EXPERTS, tm), F32)], 0)
    comb_ref[...] = comb_t.T


def _merge_call(x, mod, ys, wg, wb, wo, lng, lnb, wr_t, rb, T, TM):
    N = x.shape[0]
    tpb = T // TM
    tok = lambda w: pl.BlockSpec((TM, w), lambda i: (i, 0))
    return pl.pallas_call(
        _merge_kernel,
        out_shape=[jax.ShapeDtypeStruct((N, D_MODEL), F32), jax.ShapeDtypeStruct((N, D_MODEL), BF16),
                   jax.ShapeDtypeStruct((N, LANES), F32)],
        grid=(N // TM,),
        in_specs=[tok(D_MODEL), pl.BlockSpec((1, SUBLANES, D_MODEL), lambda i: (i // tpb, 0, 0)),
                  tok(W512), tok(W512), tok(W512), tok(W512),
                  _const_spec(wg.shape), _const_spec(wb.shape), _const_spec(wo.shape),
                  _const_spec(lng.shape), _const_spec(lnb.shape), _const_spec(wr_t.shape), _const_spec(rb.shape)],
        out_specs=[tok(D_MODEL), tok(D_MODEL), tok(LANES)],
        compiler_params=_cparams("parallel"),
        name="merge",
    )(x, mod, *ys, wg, wb, wo, lng, lnb, wr_t, rb)


def _moe_kernel(h2_ref, comb_ref, w1_ref, w3_ref, w2_ref, x1_ref, mod_ref, lng_ref, lnb_ref, o_ref, acc_ref):
    e = pl.program_id(1)

    @pl.when(e == 0)
    def _():
        acc_ref[...] = jnp.zeros_like(acc_ref)

    h = h2_ref[...]
    a = _dot(h, w1_ref[0])
    g = _dot(h, w3_ref[0])
    lane = lax.broadcasted_iota(jnp.int32, comb_ref.shape, 1)
    wcol = jnp.sum(jnp.where(lane == e, comb_ref[...], 0.0), -1, keepdims=True)
    act = (a * jax.nn.sigmoid(a) * g).astype(BF16)
    acc_ref[...] += wcol * _dot(act, w2_ref[0])

    @pl.when(e == N_EXPERTS - 1)
    def _():
        m = mod_ref[0]
        o_ref[...] = _ln(ALPHA * x1_ref[...] + m[5:6] * acc_ref[...]) * lng_ref[...] + lnb_ref[...]


def _moe_call(h2, comb, w1, w3, w2, x1, mod, lng, lnb, T, TM):
    N = h2.shape[0]
    tpb = T // TM
    tok = lambda w: pl.BlockSpec((TM, w), lambda i, e: (i, 0))
    return pl.pallas_call(
        _moe_kernel,
        out_shape=jax.ShapeDtypeStruct((N, D_MODEL), F32),
        grid=(N // TM, N_EXPERTS),
        in_specs=[tok(D_MODEL), tok(LANES),
                  pl.BlockSpec((1, D_MODEL, D_EXPERT), lambda i, e: (e, 0, 0)),
                  pl.BlockSpec((1, D_MODEL, D_EXPERT), lambda i, e: (e, 0, 0)),
                  pl.BlockSpec((1, D_EXPERT, D_MODEL), lambda i, e: (e, 0, 0)),
                  tok(D_MODEL), pl.BlockSpec((1, SUBLANES, D_MODEL), lambda i, e: (i // tpb, 0, 0)),
                  _const_spec(lng.shape), _const_spec(lnb.shape)],
        out_specs=tok(D_MODEL),
        scratch_shapes=[pltpu.VMEM((TM, D_MODEL), F32)],
        compiler_params=_cparams("parallel", "arbitrary"),
        name="moe",
    )(h2, comb, w1, w3, w2, x1, mod, lng, lnb)


def _rope_tables(T):
    t = jnp.arange(T)
    nf = HEAD_DIM // 4
    inv = ROPE_BASE ** (-jnp.arange(nf, dtype=F32) / nf)
    ar = (t // GRID_W).astype(F32)[:, None] * inv
    ac = (t % GRID_W).astype(F32)[:, None] * inv
    cos = jnp.concatenate([jnp.cos(ar), jnp.cos(ar), jnp.cos(ac), jnp.cos(ac)], -1)
    sin = jnp.concatenate([-jnp.sin(ar), jnp.sin(ar), -jnp.sin(ac), jnp.sin(ac)], -1)
    return jnp.tile(cos, (1, LANES // HEAD_DIM)), jnp.tile(sin, (1, LANES // HEAD_DIM))


def _nat_bias_table(rpb):
    q = np.arange(GRID_W)
    kc = np.arange(GRID_W)
    cstart = np.clip(q - NAT_KC // 2, 0, GRID_W - NAT_KC)
    ok = (kc[None, :] >= cstart[:, None]) & (kc[None, :] < cstart[:, None] + NAT_KC)
    cidx = np.clip(kc[None, :] - q[:, None] + NAT_KC - 1, 0, 2 * NAT_KC - 2)
    d = np.arange(NAT_KR)
    i = np.arange(NAT_KR)
    ridx = i[None, :] - d[:, None] + NAT_KR - 1
    t = rpb[:, ridx][:, :, :, cidx]
    t = jnp.where(jnp.asarray(ok)[None, None, None], t.astype(F32), NEG_INF)
    t = jnp.transpose(t, (0, 1, 3, 2, 4)).reshape(NAT_HEADS, NAT_KR, GRID_W, NAT_KR * GRID_W)
    t = t.reshape(NAT_HEADS // 2, 2, NAT_KR, GRID_W, NAT_KR * GRID_W)
    return jnp.transpose(t, (0, 2, 1, 3, 4)).reshape(NAT_HEADS // 2, NAT_KR, 2 * GRID_W, NAT_KR * GRID_W)


def _dup_heads(a):
    a = jnp.broadcast_to(a[..., :, None, :], a.shape[:-1] + (2, a.shape[-1]))
    return a.reshape(a.shape[:-3] + (-1,))


def _block_diag(w):
    eye = jnp.eye(RNN_BLOCKS, dtype=w.dtype)
    return jnp.einsum('dnio,nm->dnimo', w, eye).reshape(2, RNN_WIDTH, RNN_WIDTH)


def _pad_rows(a, rows=SUBLANES):
    return jnp.pad(a, ((0, rows - a.shape[0]),) + ((0, 0),) * (a.ndim - 1))


def kernel(x_prompt, x_sample, c, cache_nat_k, cache_nat_v, cache_swa_k, cache_swa_v, state_rglru, c_ctx,
           w_ada, b_ada, w_in, rg_conv_w, rg_conv_b, rg_wa, rg_ba, rg_wx, rg_bx, rg_lambda, nat_rpb,
           swa_sink, w_branch, w_out, ln_g, ln_b, w_router, router_bias, w1, w3, w2):
    B_c, T_c, _ = x_prompt.shape
    B_l, T_l, _ = x_sample.shape
    P = cache_nat_k.shape[2]

    cv = jnp.concatenate([c_ctx[None], c, jnp.zeros((ADA_ROWS - 1 - B_l, D_MODEL), F32)], 0)
    mod_all = _ada_call(cv, w_ada, b_ada).reshape(DEPTH, ADA_ROWS, 6, D_MODEL)
    mod_all = jnp.pad(mod_all, ((0, 0), (0, 0), (0, SUBLANES - 6), (0, 0)))

    sk0 = 3072
    sv0 = sk0 + SWA_KV_HEADS * HEAD_DIM
    dup = np.concatenate([np.arange(HEAD_DIM) + (h // 2) * HEAD_DIM for h in range(2 * SWA_KV_HEADS)])
    cols_ctx = np.concatenate([np.arange(0, GATE_OFF), sk0 + dup, sv0 + dup])
    cols_lat = np.concatenate([np.arange(0, sk0), np.arange(sv0 + SWA_KV_HEADS * HEAD_DIM, GATE_OFF),
                               sk0 + dup, sv0 + dup])
    w_in_b = w_in.astype(BF16)
    w_pre_ctx = jnp.take(w_in_b, cols_ctx, axis=2)
    w_pre_lat = jnp.take(w_in_b, cols_lat, axis=2)
    w_gate = w_in_b[:, :, GATE_OFF:]
    w_branch_b = w_branch.astype(BF16)
    w_out_b = w_out.astype(BF16)
    w1_b, w3_b, w2_b = w1.astype(BF16), w3.astype(BF16), w2.astype(BF16)
    wr_t = w_router.T.astype(BF16)
    wa_bd = jnp.stack([_block_diag(rg_wa[l]) for l in range(DEPTH)]).astype(BF16)
    wx_bd = jnp.stack([_block_diag(rg_wx[l]) for l in range(DEPTH)]).astype(BF16)

    plan_ctx = ((512, F32, 1.0, False), (512, F32, 1.0, False), (512, BF16, ATTN_SCALE, False),
                (512, F32, 1.0, False), (512, F32, 1.0, False), (512, BF16, ATTN_SCALE, False),
                (128, F32, 1.0, False), (128, F32, 1.0, False), (512, BF16, 1.0, False),
                (256, BF16, 1.0, False), (256, BF16, 1.0, False))
    plan_lat = ((512, F32, 1.0, False), (512, F32, 1.0, False), (512, BF16, ATTN_SCALE, False),
                (512, BF16, 1.0, False), (512, BF16, 1.0, False), (512, BF16, ATTN_SCALE, True),
                (512, BF16, 1.0, False), (256, BF16, 1.0, True), (256, BF16, 1.0, False))

    rope_tabs = _rope_tables(T_l)
    ck_nat = cache_nat_k.reshape(B_l, DEPTH, P, W512).astype(BF16)
    cv_nat = cache_nat_v.reshape(B_l, DEPTH, P, W512).astype(BF16)
    ckd_swa = _dup_heads(cache_swa_k).astype(BF16)
    cvd_swa = _dup_heads(cache_swa_v).astype(BF16)
    state8 = jnp.pad(state_rglru, ((0, 0), (0, 0), (0, SUBLANES - 2), (0, 0)))
    zero_state = jnp.zeros((B_c, SUBLANES, RNN_WIDTH), F32)

    def layer(x, l, ctx_pass):
        B, T = (B_c, T_c) if ctx_pass else (B_l, T_l)
        TM = min(T, 512)
        modb = jnp.broadcast_to(mod_all[l, 0:1], (B, SUBLANES, D_MODEL)) if ctx_pass else mod_all[l, 1:1 + B_l]
        if ctx_pass:
            xa, ga, nq, nk, nv, sq, sk, sv, xf, skd, svd = _pre_call(
                x, modb, w_pre_ctx[l], plan_ctx, T, TM, None)
        else:
            xa, ga, nq, nk, nv, sq, xf, skd, svd = _pre_call(x, modb, w_pre_lat[l], plan_lat, T, TM, rope_tabs)
        r3 = lambda a: a.reshape(B, T, a.shape[-1])
        h0 = zero_state if ctx_pass else state8[:, l]
        ya, hfin = _rglru_call(
            r3(xa), r3(ga), _pad_rows(rg_conv_w[l]), rg_conv_b[l][None], wa_bd[l], wx_bd[l],
            _pad_rows(rg_ba[l]), _pad_rows(rg_bx[l]), _pad_rows(rg_lambda[l]), h0)
        if ctx_pass:
            yb, yc = _attn_ctx_call(swa_sink[l], r3(nq), r3(nk), r3(nv), r3(sq), r3(skd), r3(svd))
        else:
            yb = _nat_lat_call(r3(nq), r3(nk), r3(nv), ck_nat, cv_nat, _nat_bias_table(nat_rpb[l]), l)
            yc = _swa_lat_call(swa_sink[l], r3(sq), r3(skd), r3(svd), ckd_swa, cvd_swa, l)
        yd = _fourier_call(r3(xf))
        f2 = lambda a: a.reshape(B * T, a.shape[-1])
        x1, h2, comb = _merge_call(
            x, modb, (f2(ya), f2(yb), f2(yc), f2(yd)), w_gate[l], w_branch_b[l], w_out_b[l],
            ln_g[l, 0][None], ln_b[l, 0][None], wr_t,
            jnp.broadcast_to(router_bias[:, None], (N_EXPERTS, TM)), T, TM)
        x2 = _moe_call(h2, comb, w1_b[l], w3_b[l], w2_b[l], x1, modb, ln_g[l, 1][None], ln_b[l, 1][None],
                       T, TM)
        new = (nk, nv, sk, sv, hfin[:, :2]) if ctx_pass else None
        return x2, new

    y = x_prompt.reshape(B_c * T_c, D_MODEL)
    caches = []
    for l in range(DEPTH):
        y, new = layer(y, l, True)
        caches.append(new)
    y_prompt = y.reshape(B_c, T_c, D_MODEL)
    new_nat_k = jnp.stack([cc[0].reshape(B_c, T_c, NAT_HEADS, HEAD_DIM) for cc in caches], 1)
    new_nat_v = jnp.stack([cc[1].reshape(B_c, T_c, NAT_HEADS, HEAD_DIM) for cc in caches], 1)
    new_swa_k = jnp.stack([cc[2].reshape(B_c, T_c, SWA_KV_HEADS, HEAD_DIM) for cc in caches], 1)
    new_swa_v = jnp.stack([cc[3].reshape(B_c, T_c, SWA_KV_HEADS, HEAD_DIM) for cc in caches], 1)
    new_state = jnp.stack([cc[4] for cc in caches], 1)

    y = x_sample.reshape(B_l * T_l, D_MODEL)
    for l in range(DEPTH):
        y, _ = layer(y, l, False)
    y_sample = y.reshape(B_l, T_l, D_MODEL)
    return (y_prompt, y_sample, new_nat_k, new_nat_v, new_swa_k, new_swa_v, new_state)
```

```python
import functools
import math

import jax
import jax.numpy as jnp
import numpy as np
from jax import lax
from jax.experimental import pallas as pl
from jax.experimental.pallas import tpu as pltpu

F32 = jnp.float32
BF16 = jnp.bfloat16

D_MODEL = 1024
DEPTH = 4
GRID_W = 64
HEAD_DIM = 64
ATTN_SCALE = HEAD_DIM ** -0.5
NEG_INF = -1e30
LN_EPS = 1e-5
ALPHA = (2 * DEPTH) ** 0.25
ROPE_BASE = 10000.0
RNN_WIDTH = 512
RNN_BLOCKS = 8
RNN_BLOCK = RNN_WIDTH // RNN_BLOCKS
CONV_W = 4
RGLRU_C = 8.0
NAT_HEADS = 8
NAT_KR = 8
NAT_KC = 16
SWA_HEADS = 8
SWA_KV_HEADS = 2
SWA_WINDOW = 128
SWA_BLOCK = 128
SWA_SPAN = SWA_BLOCK + 2 * SWA_WINDOW
FNET_GROUPS = 4
FNET_WIDTH = 512
FNET_GC = FNET_WIDTH // FNET_GROUPS
N_BRANCH = 4
N_EXPERTS = 16
N_EXPERT_GROUPS = 4
EXPERTS_PER_GROUP = N_EXPERTS // N_EXPERT_GROUPS
D_EXPERT = 512
W512 = 512
GATE_OFF = 3840

LANES = 128
SUBLANES = 8
VMEM_LIMIT = 56 * 1024 * 1024


def _cparams(*sem):
    return pltpu.CompilerParams(dimension_semantics=sem, vmem_limit_bytes=VMEM_LIMIT)


def _const_spec(shape):
    nd = len(shape)
    return pl.BlockSpec(shape, lambda *_: (0,) * nd, pipeline_mode=pl.Buffered(1))


def _ln(x):
    mu = jnp.mean(x, -1, keepdims=True)
    xc = x - mu
    var = jnp.mean(xc * xc, -1, keepdims=True)
    return xc * lax.rsqrt(var + LN_EPS)


def _dot(a, b):
    return jnp.dot(a, b, preferred_element_type=F32)


def _dot_nt(a, b):
    return lax.dot_general(a, b, (((1,), (1,)), ((), ())), preferred_element_type=F32)


ADA_ROWS = 16
ADA_TN = 1024


def _ada_kernel(c_ref, w_ref, b_ref, o_ref):
    cv = c_ref[...]
    s = (cv * jax.nn.sigmoid(cv)).astype(BF16)
    o_ref[0] = _dot(s, w_ref[0].astype(BF16)) + b_ref[0]


def _ada_call(cv, w_ada, b_ada):
    n = w_ada.shape[-1]
    return pl.pallas_call(
        _ada_kernel,
        out_shape=jax.ShapeDtypeStruct((DEPTH, ADA_ROWS, n), F32),
        grid=(DEPTH, n // ADA_TN),
        in_specs=[
            pl.BlockSpec((ADA_ROWS, D_MODEL), lambda l, j: (0, 0)),
            pl.BlockSpec((1, D_MODEL, ADA_TN), lambda l, j: (l, 0, j)),
            pl.BlockSpec((1, 1, ADA_TN), lambda l, j: (l, 0, j)),
        ],
        out_specs=pl.BlockSpec((1, ADA_ROWS, ADA_TN), lambda l, j: (l, 0, j)),
        compiler_params=_cparams("parallel", "parallel"),
        name="ada",
    )(cv, w_ada, b_ada.reshape(DEPTH, 1, n))


def _rope(u, cos, sin):
    lane = lax.broadcasted_iota(jnp.int32, cos.shape, 1)
    first = (lane & 31) < 16
    outs = []
    for j in range(u.shape[1] // LANES):
        s = u[:, j * LANES:(j + 1) * LANES]
        partner = jnp.where(first, pltpu.roll(s, LANES - 16, 1), pltpu.roll(s, 16, 1))
        outs.append(s * cos + partner * sin)
    return outs[0] if len(outs) == 1 else jnp.concatenate(outs, -1)


def _pre_kernel(*refs, plan, rope):
    if rope:
        x_ref, mod_ref, w_ref, cos_ref, sin_ref = refs[:5]
        outs = refs[5:]
        cos, sin = cos_ref[...], sin_ref[...]
    else:
        x_ref, mod_ref, w_ref = refs[:3]
        outs = refs[3:]
    m = mod_ref[0]
    h = (_ln(x_ref[...]) * (1.0 + m[1:2]) + m[0:1]).astype(BF16)
    off = 0
    for (width, _, scale, do_rope), o_ref in zip(plan, outs):
        u = _dot(h, w_ref[:, off:off + width])
        if do_rope:
            u = _rope(u, cos, sin)
        if scale != 1.0:
            u = u * scale
        o_ref[...] = u.astype(o_ref.dtype)
        off += width


def _pre_call(x, mod, w, plan, T, TM, rope_tabs):
    N = x.shape[0]
    tpb = T // TM
    rope = rope_tabs is not None
    wtot = sum(p[0] for p in plan)
    assert w.shape == (D_MODEL, wtot)
    in_specs = [
        pl.BlockSpec((TM, D_MODEL), lambda i: (i, 0)),
        pl.BlockSpec((1, SUBLANES, D_MODEL), lambda i: (i // tpb, 0, 0)),
        _const_spec((D_MODEL, wtot)),
    ]
    args = [x, mod, w]
    if rope:
        in_specs += [pl.BlockSpec((TM, LANES), lambda i: (i % tpb, 0))] * 2
        args += list(rope_tabs)
    return pl.pallas_call(
        functools.partial(_pre_kernel, plan=plan, rope=rope),
        out_shape=[jax.ShapeDtypeStruct((N, p[0]), p[1]) for p in plan],
        grid=(N // TM,),
        in_specs=in_specs,
        out_specs=[pl.BlockSpec((TM, p[0]), lambda i: (i, 0)) for p in plan],
        compiler_params=_cparams("parallel"),
        name="pre",
    )(*args)


RG_CW = 256
RG_TCH = 256


def _sigmoid(x):
    return 0.5 * jnp.tanh(0.5 * x) + 0.5


def _rglru_kernel(xa_ref, ga_ref, cw_ref, cb_ref, wa_ref, wx_ref, ba_ref, bx_ref, lam_ref, h0_ref,
                  y_ref, hfin_ref, xpad, a_f, u_f, a_b, u_b, *, T):
    cw_ = RG_CW
    zeros8 = jnp.zeros((SUBLANES, cw_), F32)
    xpad[0:SUBLANES, :] = zeros8
    xpad[T + SUBLANES:T + 2 * SUBLANES, :] = zeros8
    xpad[SUBLANES:T + SUBLANES, :] = xa_ref[0]
    cw = cw_ref[...]
    cb = cb_ref[...]
    lam = lam_ref[...]
    sp = jnp.maximum(-lam, 0.0) + jnp.log1p(jnp.exp(-jnp.abs(lam)))
    ba = ba_ref[...]
    bx = bx_ref[...]
    h0 = h0_ref[0]

    def chunk(c, carry):
        base = pl.multiple_of(c * RG_TCH, RG_TCH)
        xw = xpad[pl.ds(base, RG_TCH + 2 * SUBLANES), :]
        xc = cb
        for i in range(CONV_W):
            xc = xc + cw[i:i + 1] * xw[SUBLANES - 1 + i:SUBLANES - 1 + i + RG_TCH]
        xcb = xc.astype(BF16)
        for d, (a_s, u_s) in enumerate(((a_f, u_f), (a_b, u_b))):
            r = _sigmoid(_dot(xcb, wa_ref[d]) + ba[d:d + 1])
            i_g = _sigmoid(_dot(xcb, wx_ref[d]) + bx[d:d + 1])
            log_a = -RGLRU_C * r * sp[d:d + 1]
            a = jnp.exp(log_a)
            gain = jnp.sqrt(jnp.tanh(-log_a) * (1.0 + a * a))
            a_s[pl.ds(base, RG_TCH), :] = a
            u_s[pl.ds(base, RG_TCH), :] = gain * (i_g * xc)
        return carry

    lax.fori_loop(0, T // RG_TCH, chunk, 0)

    row = lax.broadcasted_iota(jnp.int32, (SUBLANES, cw_), 0)

    def block_scan(a, u, reverse):
        for dd in (1, 2, 4):
            sh = SUBLANES - dd if reverse else dd
            a_n = pltpu.roll(a, sh, 0)
            u_n = pltpu.roll(u, sh, 0)
            ok = (row < SUBLANES - dd) if reverse else (row >= dd)
            u = jnp.where(ok, u + a * u_n, u)
            a = jnp.where(ok, a * a_n, a)
        return a, u

    nblk = T // SUBLANES

    def scan(i, carry):
        c_f, c_b = carry
        lo_f = pl.multiple_of(i * SUBLANES, SUBLANES)
        lo_b = pl.multiple_of((nblk - 1 - i) * SUBLANES, SUBLANES)
        a, u = block_scan(a_f[pl.ds(lo_f, SUBLANES), :], u_f[pl.ds(lo_f, SUBLANES), :], False)
        h_f = u + a * c_f
        u_f[pl.ds(lo_f, SUBLANES), :] = h_f
        a, u = block_scan(a_b[pl.ds(lo_b, SUBLANES), :], u_b[pl.ds(lo_b, SUBLANES), :], True)
        h_b = u + a * c_b
        u_b[pl.ds(lo_b, SUBLANES), :] = h_b
        return (jnp.broadcast_to(h_f[SUBLANES - 1:SUBLANES, :], (SUBLANES, cw_)),
                jnp.broadcast_to(h_b[0:1, :], (SUBLANES, cw_)))

    c_f, c_b = lax.fori_loop(
        0, nblk, scan,
        (jnp.broadcast_to(h0[0:1, :], (SUBLANES, cw_)), jnp.broadcast_to(h0[1:2, :], (SUBLANES, cw_))),
        unroll=2)
    hfin_ref[0] = jnp.where(row == 0, c_f, jnp.where(row == 1, c_b, 0.0))

    def emit(c, carry):
        base = pl.multiple_of(c * RG_TCH, RG_TCH)
        h = u_f[pl.ds(base, RG_TCH), :] + u_b[pl.ds(base, RG_TCH), :]
        y_ref[0, pl.ds(base, RG_TCH), :] = (h * jax.nn.gelu(ga_ref[0, pl.ds(base, RG_TCH), :])).astype(y_ref.dtype)
        return carry

    lax.fori_loop(0, T // RG_TCH, emit, 0)


def _rglru_call(xa, ga, cw, cb, wa, wx, ba, bx, lam, h0):
    B, T, _ = xa.shape
    nj = RNN_WIDTH // RG_CW
    seq = pl.BlockSpec((1, T, RG_CW), lambda b, j: (b, 0, j))
    vec8 = pl.BlockSpec((SUBLANES, RG_CW), lambda b, j: (0, j))
    wsp = pl.BlockSpec((2, RG_CW, RG_CW), lambda b, j: (0, j, j))
    st = pl.BlockSpec((1, SUBLANES, RG_CW), lambda b, j: (b, 0, j))
    return pl.pallas_call(
        functools.partial(_rglru_kernel, T=T),
        out_shape=[jax.ShapeDtypeStruct((B, T, RNN_WIDTH), BF16),
                   jax.ShapeDtypeStruct((B, SUBLANES, RNN_WIDTH), F32)],
        grid=(B, nj),
        in_specs=[seq, seq, vec8, pl.BlockSpec((1, RG_CW), lambda b, j: (0, j)),
                  wsp, wsp, vec8, vec8, vec8, st],
        out_specs=[seq, st],
        scratch_shapes=[pltpu.VMEM((T + 2 * SUBLANES, RG_CW), F32)]
        + [pltpu.VMEM((T, RG_CW), F32)] * 4,
        compiler_params=_cparams("parallel", "parallel"),
        name="rglru",
    )(xa, ga, cw, cb, wa, wx, ba, bx, lam, h0)


def _attend2(q2, srcs, sink_col):
    M = q2.shape[0]
    lo = lax.broadcasted_iota(jnp.int32, (M, LANES), 1) < HEAD_DIM
    zero = jnp.zeros_like(q2)
    qs = jnp.concatenate([jnp.where(lo, q2, zero), jnp.where(lo, zero, q2)], 0)
    ss = []
    for k, _, bias in srcs:
        s = _dot_nt(qs, k)
        ss.append(s if bias is None else s + bias)
    m = jnp.max(ss[0], -1, keepdims=True)
    for s in ss[1:]:
        m = jnp.maximum(m, jnp.max(s, -1, keepdims=True))
    if sink_col is not None:
        m = jnp.maximum(m, sink_col)
    den = None
    o = None
    for s, (_, v, _) in zip(ss, srcs):
        e = jnp.exp(s - m)
        d_ = jnp.sum(e, -1, keepdims=True)
        o_ = _dot(e.astype(BF16), v)
        den = d_ if den is None else den + d_
        o = o_ if o is None else o + o_
    if sink_col is not None:
        den = den + jnp.exp(sink_col - m)
    o = o / den
    return jnp.where(lo, o[:M], o[M:])


def _sink_col(sink_ref, j, M):
    r = lax.broadcasted_iota(jnp.int32, (2 * M, 1), 0)
    return jnp.where(r < M, sink_ref[2 * j], sink_ref[2 * j + 1])


def _slab(j):
    return slice(j * LANES, (j + 1) * LANES)


def _attn_ctx_kernel(sink_ref, nq_ref, nk_ref, nv_ref, sq_ref, skd_ref, svd_ref, yb_ref, yc_ref):
    M = nq_ref.shape[1]
    for j in range(W512 // LANES):
        k2 = nk_ref[0, :, _slab(j)].astype(BF16)
        v2 = nv_ref[0, :, _slab(j)].astype(BF16)
        yb_ref[0, :, _slab(j)] = _attend2(nq_ref[0, :, _slab(j)], [(k2, v2, None)], None).astype(yb_ref.dtype)
        g = j // 2
        yc_ref[0, :, _slab(j)] = _attend2(
            sq_ref[0, :, _slab(j)], [(skd_ref[0, :, _slab(g)], svd_ref[0, :, _slab(g)], None)],
            _sink_col(sink_ref, j, M)).astype(yc_ref.dtype)


def _attn_ctx_call(sink, nq, nk, nv, sq, skd, svd):
    B, T, _ = nq.shape
    s512 = pl.BlockSpec((1, T, W512), lambda b: (b, 0, 0))
    s256 = pl.BlockSpec((1, T, 2 * LANES), lambda b: (b, 0, 0))
    return pl.pallas_call(
        _attn_ctx_kernel,
        out_shape=[jax.ShapeDtypeStruct((B, T, W512), BF16)] * 2,
        grid=(B,),
        in_specs=[pl.BlockSpec(memory_space=pltpu.SMEM), s512, s512, s512, s512, s256, s256],
        out_specs=[s512, s512],
        compiler_params=_cparams("parallel"),
        name="attn_ctx",
    )(sink, nq, nk, nv, sq, skd, svd)


NAT_NLOC = NAT_KR * GRID_W


def _nat_lat_kernel(q_ref, k_ref, v_ref, ck_ref, cv_ref, bias_ref, y_ref, *, rows):
    r = pl.program_id(1)
    rstart = jnp.clip(r - NAT_KR // 2, 0, rows - NAT_KR)
    d = r - rstart
    kbase = pl.multiple_of(rstart * GRID_W, GRID_W)
    for j in range(W512 // LANES):
        k2 = k_ref[0, pl.ds(kbase, NAT_NLOC), _slab(j)]
        v2 = v_ref[0, pl.ds(kbase, NAT_NLOC), _slab(j)]
        bias = bias_ref[j, d]
        o = _attend2(q_ref[0, :, _slab(j)],
                     [(k2, v2, bias), (ck_ref[0, 0, :, _slab(j)], cv_ref[0, 0, :, _slab(j)], None)], None)
        y_ref[0, :, _slab(j)] = o.astype(y_ref.dtype)


def _nat_lat_call(q, k, v, ck, cv, bias, l):
    B, T, _ = q.shape
    rows = T // GRID_W
    P = ck.shape[2]
    qs = pl.BlockSpec((1, GRID_W, W512), lambda b, r: (b, r, 0))
    full = pl.BlockSpec((1, T, W512), lambda b, r: (b, 0, 0))
    cs = pl.BlockSpec((1, 1, P, W512), lambda b, r: (b, l, 0, 0))
    return pl.pallas_call(
        functools.partial(_nat_lat_kernel, rows=rows),
        out_shape=jax.ShapeDtypeStruct((B, T, W512), BF16),
        grid=(B, rows),
        in_specs=[qs, full, full, cs, cs, _const_spec(bias.shape)],
        out_specs=qs,
        compiler_params=_cparams("parallel", "arbitrary"),
        name="nat_lat",
    )(q, k, v, ck, cv, bias)


def _swa_lat_kernel(sink_ref, q_ref, kd_ref, vd_ref, ckd_ref, cvd_ref, y_ref, *, T):
    blk = pl.program_id(1)
    M = SWA_BLOCK
    start = blk * SWA_BLOCK
    ks = pl.multiple_of(jnp.clip(start - SWA_WINDOW, 0, T - SWA_SPAN), SWA_BLOCK)
    kpos = ks + lax.broadcasted_iota(jnp.int32, (2 * M, SWA_SPAN), 1)
    qpos = start + (lax.broadcasted_iota(jnp.int32, (2 * M, SWA_SPAN), 0) & (M - 1))
    dist = kpos - qpos
    bias = jnp.where(dist > SWA_WINDOW, NEG_INF, jnp.where(dist < -SWA_WINDOW, NEG_INF, 0.0))
    for j in range(W512 // LANES):
        g = j // 2
        o = _attend2(q_ref[0, :, _slab(j)],
                     [(kd_ref[0, pl.ds(ks, SWA_SPAN), _slab(g)], vd_ref[0, pl.ds(ks, SWA_SPAN), _slab(g)], bias),
                      (ckd_ref[0, 0, :, _slab(g)], cvd_ref[0, 0, :, _slab(g)], None)],
                     _sink_col(sink_ref, j, M))
        y_ref[0, :, _slab(j)] = o.astype(y_ref.dtype)


def _swa_lat_call(sink, q, kd, vd, ckd, cvd, l):
    B, T, _ = q.shape
    P = ckd.shape[2]
    qs = pl.BlockSpec((1, SWA_BLOCK, W512), lambda b, i: (b, i, 0))
    full = pl.BlockSpec((1, T, 2 * LANES), lambda b, i: (b, 0, 0))
    cs = pl.BlockSpec((1, 1, P, 2 * LANES), lambda b, i: (b, l, 0, 0))
    return pl.pallas_call(
        functools.partial(_swa_lat_kernel, T=T),
        out_shape=jax.ShapeDtypeStruct((B, T, W512), BF16),
        grid=(B, T // SWA_BLOCK),
        in_specs=[pl.BlockSpec(memory_space=pltpu.SMEM), qs, full, full, cs, cs],
        out_specs=qs,
        compiler_params=_cparams("parallel", "arbitrary"),
        name="swa_lat",
    )(sink, q, kd, vd, ckd, cvd)


FN_N2 = 256
FN_GP = 2


def _fourier_kernel(x_ref, cc_ref, m_ref, y_ref, *scratch, n1, coef):
    gw = FN_GP * FNET_GC
    cc = cc_ref[...]
    for gp in range(FNET_GROUPS // FN_GP):
        parts = []
        for t1 in range(n1):
            ws = []
            for g in range(FN_GP):
                lo = t1 * FNET_WIDTH + (gp * FN_GP + g) * FNET_GC
                ws.append(_dot(x_ref[0, :, lo:lo + FNET_GC], cc))
            wr = jnp.concatenate([w[:, :FNET_GC] for w in ws], -1)
            wi = jnp.concatenate([w[:, FNET_GC:] for w in ws], -1)
            v = jnp.concatenate([wr, wi], 0).astype(BF16)
            b = _dot(m_ref[t1], v)
            if n1 == 1:
                y_ref[0, :, gp * gw:(gp + 1) * gw] = b.astype(y_ref.dtype)
            else:
                scratch[0][t1] = b
        if n1 > 1:
            bs = scratch[0]
            for k1 in range(n1):
                acc = None
                for t1 in range(n1):
                    c, s = coef[k1][t1]
                    for cf, lo in ((c, 0), (s, FN_N2)):
                        if cf == 0.0:
                            continue
                        blk = bs[t1, lo:lo + FN_N2, :]
                        term = blk if cf == 1.0 else (-blk if cf == -1.0 else cf * blk)
                        acc = term if acc is None else acc + term
                y_ref[0, k1 * FN_N2:(k1 + 1) * FN_N2, gp * gw:(gp + 1) * gw] = acc.astype(y_ref.dtype)


def _dft_consts(T):
    n1 = T // FN_N2
    j = np.arange(FNET_GC)
    ang = 2 * np.pi * np.outer(j, j) / FNET_GC
    sc = 1.0 / math.sqrt(FNET_GC)
    cc = np.concatenate([np.cos(ang) * sc, -np.sin(ang) * sc], 1)
    k2 = np.arange(FN_N2)[:, None]
    t2 = np.arange(FN_N2)[None, :]
    st = 1.0 / math.sqrt(T)
    mats = []
    for t1 in range(n1):
        th = 2 * np.pi * ((k2 * (n1 * t2 + t1)) % T) / T
        c, s = np.cos(th) * st, np.sin(th) * st
        top = np.concatenate([c, s], 1)
        mats.append(top if n1 == 1 else np.concatenate([top, np.concatenate([-s, c], 1)], 0))
    coef = []
    for k1 in range(n1):
        rowc = []
        for t1 in range(n1):
            q = (k1 * t1) % n1
            c, s = math.cos(2 * math.pi * q / n1), math.sin(2 * math.pi * q / n1)
            c = 0.0 if abs(c) < 1e-12 else (1.0 if abs(c - 1) < 1e-12 else (-1.0 if abs(c + 1) < 1e-12 else c))
            s = 0.0 if abs(s) < 1e-12 else (1.0 if abs(s - 1) < 1e-12 else (-1.0 if abs(s + 1) < 1e-12 else s))
            rowc.append((c, s))
        coef.append(tuple(rowc))
    return n1, jnp.asarray(cc, BF16), jnp.asarray(np.stack(mats), BF16), tuple(coef)


def _fourier_call(xf):
    B, T, _ = xf.shape
    n1, cc, mats, coef = _dft_consts(T)
    xv = xf.reshape(B, FN_N2, n1 * FNET_WIDTH)
    gw = FN_GP * FNET_GC
    scratch = [pltpu.VMEM((n1, 2 * FN_N2, gw), F32)] if n1 > 1 else []
    return pl.pallas_call(
        functools.partial(_fourier_kernel, n1=n1, coef=coef),
        out_shape=jax.ShapeDtypeStruct((B, T, FNET_WIDTH), BF16),
        grid=(B,),
        in_specs=[pl.BlockSpec((1, FN_N2, n1 * FNET_WIDTH), lambda b: (b, 0, 0)),
                  _const_spec(cc.shape), _const_spec(mats.shape)],
        out_specs=pl.BlockSpec((1, T, FNET_WIDTH), lambda b: (b, 0, 0)),
        scratch_shapes=scratch,
        compiler_params=_cparams("parallel"),
        name="fourier",
    )(xv, cc, mats)


def _route(scores, biased):
    one, zero = jnp.float32(1.0), jnp.float32(0.0)
    in2, gscore = [], []
    for g in range(N_EXPERT_GROUPS):
        vs = [biased[g * EXPERTS_PER_GROUP + j:g * EXPERTS_PER_GROUP + j + 1] for j in range(EXPERTS_PER_GROUP)]
        gs = None
        for j in range(EXPERTS_PER_GROUP):
            rank = None
            for i in range(EXPERTS_PER_GROUP):
                if i == j:
                    continue
                beats = (vs[i] >= vs[j]) if i < j else (vs[i] > vs[j])
                t = jnp.where(beats, one, zero)
                rank = t if rank is None else rank + t
            keep = jnp.where(rank < 2.0, one, zero)
            in2.append(keep)
            t = keep * vs[j]
            gs = t if gs is None else gs + t
        gscore.append(gs)
    rows, sel = [], []
    for g in range(N_EXPERT_GROUPS):
        lost = None
        for i in range(N_EXPERT_GROUPS):
            if i == g:
                continue
            beats = (gscore[i] >= gscore[g]) if i < g else (gscore[i] > gscore[g])
            t = jnp.where(beats, one, zero)
            lost = t if lost is None else lost + t
        gsel = jnp.where(lost < 1.0, one, zero)
        for j in range(EXPERTS_PER_GROUP):
            e = g * EXPERTS_PER_GROUP + j
            sel.append(gsel * in2[e])
            rows.append(sel[-1] * scores[e:e + 1])
    tot = rows[0]
    for rr in rows[1:]:
        tot = tot + rr
    return jnp.concatenate(rows, 0) / tot, jnp.concatenate(sel, 0)


def _pack_halves(v):
    w = v.shape[1] // 2
    lo = lax.bitcast_convert_type(v[:, :w].astype(F32), jnp.uint32)
    hi = lax.bitcast_convert_type(v[:, w:].astype(F32), jnp.uint32)
    return (lo >> 16) | hi


def _unpack_halves(p):
    lo = lax.bitcast_convert_type(p << 16, F32).astype(BF16)
    hi = lax.bitcast_convert_type(p & jnp.uint32(0xFFFF0000), F32).astype(BF16)
    return lo, hi


def _merge_kernel(x_ref, mod_ref, ya_ref, yb_ref, yc_ref, yd_ref, wg_ref, wb_ref, wo_ref,
                  lng_ref, lnb_ref, wr_ref, rb_ref, x1_ref, hp_ref, rt_ref):
    x = x_ref[...]
    m = mod_ref[0]
    h = (_ln(x) * (1.0 + m[1:2]) + m[0:1]).astype(BF16)
    merged = None
    for b, y_ref in enumerate((ya_ref, yb_ref, yc_ref, yd_ref)):
        gate = jax.nn.sigmoid(_dot(h, wg_ref[:, b * D_MODEL:(b + 1) * D_MODEL]))
        term = gate * _dot(y_ref[...], wb_ref[b])
        merged = term if merged is None else merged + term
    out = _dot(merged.astype(BF16), wo_ref[...])
    x1 = _ln(ALPHA * x + m[2:3] * out) * lng_ref[...] + lnb_ref[...]
    x1_ref[...] = x1
    h2 = (_ln(x1) * (1.0 + m[4:5]) + m[3:4]).astype(BF16)
    hp_ref[...] = _pack_halves(h2)
    logits = _dot_nt(wr_ref[...], h2)
    e = jnp.exp(logits - jnp.max(logits, 0, keepdims=True))
    scores = e / jnp.sum(e, 0, keepdims=True)
    comb, sel = _route(scores, scores + rb_ref[...])
    rt_ref[...] = jnp.concatenate([comb, sel], 0)


def _merge_call(x, mod, ys, wg, wb, wo, lng, lnb, wr_t, rb, T, TM):
    N = x.shape[0]
    tpb = T // TM
    tok = lambda w: pl.BlockSpec((TM, w), lambda i: (i, 0))
    return pl.pallas_call(
        _merge_kernel,
        out_shape=[jax.ShapeDtypeStruct((N, D_MODEL), F32), jax.ShapeDtypeStruct((N, D_MODEL // 2), jnp.uint32),
                   jax.ShapeDtypeStruct((2 * N_EXPERTS, N), F32)],
        grid=(N // TM,),
        in_specs=[tok(D_MODEL), pl.BlockSpec((1, SUBLANES, D_MODEL), lambda i: (i // tpb, 0, 0)),
                  tok(W512), tok(W512), tok(W512), tok(W512),
                  _const_spec(wg.shape), _const_spec(wb.shape), _const_spec(wo.shape),
                  _const_spec(lng.shape), _const_spec(lnb.shape), _const_spec(wr_t.shape), _const_spec(rb.shape)],
        out_specs=[tok(D_MODEL), tok(D_MODEL // 2), pl.BlockSpec((2 * N_EXPERTS, TM), lambda i: (0, i))],
        compiler_params=_cparams("parallel"),
        name="merge",
    )(x, mod, *ys, wg, wb, wo, lng, lnb, wr_t, rb)


MOE_TM = 2048
MOE_RB = 256
MOE_KC = 4
MOE_CH = MOE_TM // MOE_KC
MOE_SLOTS = 2 * MOE_TM + N_EXPERTS * SUBLANES + MOE_RB


def _route_kernel(rt_ref, u_ref, pos_ref, wts_ref, seg_ref):
    comb = rt_ref[0:N_EXPERTS, :]
    sel = rt_ref[N_EXPERTS:2 * N_EXPERTS, :]
    nb = MOE_TM // LANES
    stacked = jnp.concatenate([sel[:, b * LANES:(b + 1) * LANES] for b in range(nb)], 0)
    within = _dot(stacked.astype(BF16), u_ref[...])
    tot = jnp.sum(stacked, -1, keepdims=True)
    base = jnp.zeros((N_EXPERTS, 1), F32)
    blocks = []
    for b in range(nb):
        blocks.append(within[b * N_EXPERTS:(b + 1) * N_EXPERTS] + base)
        base = base + tot[b * N_EXPERTS:(b + 1) * N_EXPERTS]
    running = jnp.concatenate(blocks, -1)
    cnt = base
    padded = jnp.floor((cnt + (SUBLANES - 1.0)) * (1.0 / SUBLANES)) * SUBLANES
    rowi = lax.broadcasted_iota(jnp.int32, (N_EXPERTS, 1), 0)
    off = jnp.zeros((N_EXPERTS, 1), F32)
    for e in range(N_EXPERTS - 1):
        off = off + jnp.where(rowi > e, padded[e:e + 1, :], 0.0)
    slot = off + running
    seen = jnp.zeros((1, MOE_TM), F32)
    acc = [jnp.zeros((1, MOE_TM), F32) for _ in range(4)]
    for e in range(N_EXPERTS):
        s_e = sel[e:e + 1]
        first = jnp.where(seen == 0.0, s_e, 0.0)
        second = jnp.where(seen == 1.0, s_e, 0.0)
        acc[0] = acc[0] + first * slot[e:e + 1]
        acc[1] = acc[1] + second * slot[e:e + 1]
        acc[2] = acc[2] + first * comb[e:e + 1]
        acc[3] = acc[3] + second * comb[e:e + 1]
        seen = seen + s_e
    pos_ref[0] = jnp.concatenate(acc[0:2], 0).astype(jnp.int32)
    wts_ref[0] = jnp.concatenate(acc[2:4], 0)
    lane = lax.broadcasted_iota(jnp.int32, (N_EXPERTS, LANES), 1)
    diag = lane == lax.broadcasted_iota(jnp.int32, (N_EXPERTS, LANES), 0)
    off_row = jnp.sum(jnp.where(diag, off, 0.0), 0, keepdims=True)
    cnt_row = jnp.sum(jnp.where(diag, cnt, 0.0), 0, keepdims=True)
    seg_ref[0] = jnp.concatenate([off_row, cnt_row, jnp.zeros((SUBLANES - 2, LANES), F32)], 0).astype(jnp.int32)


def _route_call(rt):
    N = rt.shape[1]
    nt = N // MOE_TM
    u = jnp.asarray(np.triu(np.ones((LANES, LANES), np.float32), 1), BF16)
    return pl.pallas_call(
        _route_kernel,
        out_shape=[jax.ShapeDtypeStruct((nt, 2, MOE_TM), jnp.int32), jax.ShapeDtypeStruct((nt, 2, MOE_TM), F32),
                   jax.ShapeDtypeStruct((nt, SUBLANES, LANES), jnp.int32)],
        grid=(nt,),
        in_specs=[pl.BlockSpec((2 * N_EXPERTS, MOE_TM), lambda i: (0, i)), _const_spec(u.shape)],
        out_specs=[pl.BlockSpec((1, 2, MOE_TM), lambda i: (i, 0, 0)), pl.BlockSpec((1, 2, MOE_TM), lambda i: (i, 0, 0)),
                   pl.BlockSpec((1, SUBLANES, LANES), lambda i: (i, 0, 0))],
        compiler_params=_cparams("parallel"),
        name="route",
    )(rt, u)


def _moe_kernel(pos_ref, wts_ref, seg_ref, hp_ref, w1_ref, w3_ref, w2_ref, x1_ref, mod_ref, lng_ref, lnb_ref,
                o_ref, xs, ys):
    s = pl.program_id(1)
    half = D_MODEL // 2

    @pl.when(s == 0)
    def _dispatch():
        xs[...] = jnp.zeros_like(xs)

        def body(t, carry):
            row = hp_ref[pl.ds(t, 1), :]
            xs[pl.ds(pos_ref[0, 0, t], 1), :] = row
            xs[pl.ds(pos_ref[0, 1, t], 1), :] = row
            return carry

        lax.fori_loop(0, MOE_TM, body, 0, unroll=8)

    @pl.when(s < N_EXPERTS)
    def _expert():
        off = seg_ref[0, 0, s]
        nblk = (seg_ref[0, 1, s] + (MOE_RB - 1)) // MOE_RB

        def blk(i, carry):
            r0 = pl.multiple_of(off + i * MOE_RB, SUBLANES)
            lo, hi = _unpack_halves(xs[pl.ds(r0, MOE_RB), :])
            a = _dot(lo, w1_ref[0, :half, :]) + _dot(hi, w1_ref[0, half:, :])
            g = _dot(lo, w3_ref[0, :half, :]) + _dot(hi, w3_ref[0, half:, :])
            act = (a * jax.nn.sigmoid(a) * g).astype(BF16)
            ys[pl.ds(r0, MOE_RB), :] = _dot(act, w2_ref[0])
            return carry

        lax.fori_loop(0, nblk, blk, 0)

    @pl.when(s >= N_EXPERTS)
    def _combine():
        t0 = (s - N_EXPERTS) * MOE_CH

        def body(i, carry):
            t = t0 + i
            o_ref[pl.ds(i, 1), :] = (wts_ref[0, 0, t] * ys[pl.ds(pos_ref[0, 0, t], 1), :]
                                     + wts_ref[0, 1, t] * ys[pl.ds(pos_ref[0, 1, t], 1), :])
            return carry

        lax.fori_loop(0, MOE_CH, body, 0, unroll=8)
        m = mod_ref[0]
        o_ref[...] = _ln(ALPHA * x1_ref[...] + m[5:6] * o_ref[...]) * lng_ref[...] + lnb_ref[...]


def _moe_call(pos, wts, seg, hp, w1, w3, w2, x1, mod, lng, lnb, T):
    N = hp.shape[0]
    nt = N // MOE_TM
    smem = lambda shp: pl.BlockSpec((1,) + shp, lambda i, s: (i, 0, 0), memory_space=pltpu.SMEM)
    wspec = lambda shp: pl.BlockSpec((1,) + shp, lambda i, s: (jnp.minimum(s, N_EXPERTS - 1), 0, 0))
    chunk = pl.BlockSpec((MOE_CH, D_MODEL),
                         lambda i, s: (i * MOE_KC + jnp.clip(s - N_EXPERTS, 0, MOE_KC - 1), 0))
    return pl.pallas_call(
        _moe_kernel,
        out_shape=jax.ShapeDtypeStruct((N, D_MODEL), F32),
        grid=(nt, N_EXPERTS + MOE_KC),
        in_specs=[smem((2, MOE_TM)), smem((2, MOE_TM)), smem((SUBLANES, LANES)),
                  pl.BlockSpec((MOE_TM, D_MODEL // 2), lambda i, s: (i, 0)),
                  wspec((D_MODEL, D_EXPERT)), wspec((D_MODEL, D_EXPERT)), wspec((D_EXPERT, D_MODEL)),
                  chunk, pl.BlockSpec((1, SUBLANES, D_MODEL), lambda i, s: ((i * MOE_TM) // T, 0, 0)),
                  _const_spec(lng.shape), _const_spec(lnb.shape)],
        out_specs=chunk,
        scratch_shapes=[pltpu.VMEM((MOE_SLOTS, D_MODEL // 2), jnp.uint32), pltpu.VMEM((MOE_SLOTS, D_MODEL), F32)],
        compiler_params=_cparams("parallel", "arbitrary"),
        name="moe",
    )(pos, wts, seg, hp, w1, w3, w2, x1, mod, lng, lnb)


def _rope_tables(T):
    t = jnp.arange(T)
    nf = HEAD_DIM // 4
    inv = ROPE_BASE ** (-jnp.arange(nf, dtype=F32) / nf)
    ar = (t // GRID_W).astype(F32)[:, None] * inv
    ac = (t % GRID_W).astype(F32)[:, None] * inv
    cos = jnp.concatenate([jnp.cos(ar), jnp.cos(ar), jnp.cos(ac), jnp.cos(ac)], -1)
    sin = jnp.concatenate([-jnp.sin(ar), jnp.sin(ar), -jnp.sin(ac), jnp.sin(ac)], -1)
    return jnp.tile(cos, (1, LANES // HEAD_DIM)), jnp.tile(sin, (1, LANES // HEAD_DIM))


def _nat_bias_table(rpb):
    q = np.arange(GRID_W)
    kc = np.arange(GRID_W)
    cstart = np.clip(q - NAT_KC // 2, 0, GRID_W - NAT_KC)
    ok = (kc[None, :] >= cstart[:, None]) & (kc[None, :] < cstart[:, None] + NAT_KC)
    cidx = np.clip(kc[None, :] - q[:, None] + NAT_KC - 1, 0, 2 * NAT_KC - 2)
    d = np.arange(NAT_KR)
    i = np.arange(NAT_KR)
    ridx = i[None, :] - d[:, None] + NAT_KR - 1
    t = rpb[:, ridx][:, :, :, cidx]
    t = jnp.where(jnp.asarray(ok)[None, None, None], t.astype(F32), NEG_INF)
    t = jnp.transpose(t, (0, 1, 3, 2, 4)).reshape(NAT_HEADS, NAT_KR, GRID_W, NAT_KR * GRID_W)
    t = t.reshape(NAT_HEADS // 2, 2, NAT_KR, GRID_W, NAT_KR * GRID_W)
    return jnp.transpose(t, (0, 2, 1, 3, 4)).reshape(NAT_HEADS // 2, NAT_KR, 2 * GRID_W, NAT_KR * GRID_W)


def _dup_heads(a):
    a = jnp.broadcast_to(a[..., :, None, :], a.shape[:-1] + (2, a.shape[-1]))
    return a.reshape(a.shape[:-3] + (-1,))


def _block_diag(w):
    eye = jnp.eye(RNN_BLOCKS, dtype=w.dtype)
    return jnp.einsum('dnio,nm->dnimo', w, eye).reshape(2, RNN_WIDTH, RNN_WIDTH)


def _pad_rows(a, rows=SUBLANES):
    return jnp.pad(a, ((0, rows - a.shape[0]),) + ((0, 0),) * (a.ndim - 1))


def kernel(x_prompt, x_sample, c, cache_nat_k, cache_nat_v, cache_swa_k, cache_swa_v, state_rglru, c_ctx,
           w_ada, b_ada, w_in, rg_conv_w, rg_conv_b, rg_wa, rg_ba, rg_wx, rg_bx, rg_lambda, nat_rpb,
           swa_sink, w_branch, w_out, ln_g, ln_b, w_router, router_bias, w1, w3, w2):
    B_c, T_c, _ = x_prompt.shape
    B_l, T_l, _ = x_sample.shape
    P = cache_nat_k.shape[2]

    cv = jnp.concatenate([c_ctx[None], c, jnp.zeros((ADA_ROWS - 1 - B_l, D_MODEL), F32)], 0)
    mod_all = _ada_call(cv, w_ada, b_ada).reshape(DEPTH, ADA_ROWS, 6, D_MODEL)
    mod_all = jnp.pad(mod_all, ((0, 0), (0, 0), (0, SUBLANES - 6), (0, 0)))

    sk0 = 3072
    sv0 = sk0 + SWA_KV_HEADS * HEAD_DIM
    dup = np.concatenate([np.arange(HEAD_DIM) + (h // 2) * HEAD_DIM for h in range(2 * SWA_KV_HEADS)])
    cols_ctx = np.concatenate([np.arange(0, GATE_OFF), sk0 + dup, sv0 + dup])
    cols_lat = np.concatenate([np.arange(0, sk0), np.arange(sv0 + SWA_KV_HEADS * HEAD_DIM, GATE_OFF),
                               sk0 + dup, sv0 + dup])
    w_in_b = w_in.astype(BF16)
    w_pre_ctx = jnp.take(w_in_b, cols_ctx, axis=2)
    w_pre_lat = jnp.take(w_in_b, cols_lat, axis=2)
    w_gate = w_in_b[:, :, GATE_OFF:]
    w_branch_b = w_branch.astype(BF16)
    w_out_b = w_out.astype(BF16)
    w1_b, w3_b, w2_b = w1.astype(BF16), w3.astype(BF16), w2.astype(BF16)
    wr_t = w_router.T.astype(BF16)
    wa_bd = jnp.stack([_block_diag(rg_wa[l]) for l in range(DEPTH)]).astype(BF16)
    wx_bd = jnp.stack([_block_diag(rg_wx[l]) for l in range(DEPTH)]).astype(BF16)

    plan_ctx = ((512, F32, 1.0, False), (512, F32, 1.0, False), (512, BF16, ATTN_SCALE, False),
                (512, F32, 1.0, False), (512, F32, 1.0, False), (512, BF16, ATTN_SCALE, False),
                (128, F32, 1.0, False), (128, F32, 1.0, False), (512, BF16, 1.0, False),
                (256, BF16, 1.0, False), (256, BF16, 1.0, False))
    plan_lat = ((512, F32, 1.0, False), (512, F32, 1.0, False), (512, BF16, ATTN_SCALE, False),
                (512, BF16, 1.0, False), (512, BF16, 1.0, False), (512, BF16, ATTN_SCALE, True),
                (512, BF16, 1.0, False), (256, BF16, 1.0, True), (256, BF16, 1.0, False))

    rope_tabs = _rope_tables(T_l)
    ck_nat = cache_nat_k.reshape(B_l, DEPTH, P, W512).astype(BF16)
    cv_nat = cache_nat_v.reshape(B_l, DEPTH, P, W512).astype(BF16)
    ckd_swa = _dup_heads(cache_swa_k).astype(BF16)
    cvd_swa = _dup_heads(cache_swa_v).astype(BF16)
    state8 = jnp.pad(state_rglru, ((0, 0), (0, 0), (0, SUBLANES - 2), (0, 0)))
    zero_state = jnp.zeros((B_c, SUBLANES, RNN_WIDTH), F32)

    def layer(x, l, ctx_pass):
        B, T = (B_c, T_c) if ctx_pass else (B_l, T_l)
        TM = min(T, 512)
        modb = jnp.broadcast_to(mod_all[l, 0:1], (B, SUBLANES, D_MODEL)) if ctx_pass else mod_all[l, 1:1 + B_l]
        if ctx_pass:
            xa, ga, nq, nk, nv, sq, sk, sv, xf, skd, svd = _pre_call(
                x, modb, w_pre_ctx[l], plan_ctx, T, TM, None)
        else:
            xa, ga, nq, nk, nv, sq, xf, skd, svd = _pre_call(x, modb, w_pre_lat[l], plan_lat, T, TM, rope_tabs)
        r3 = lambda a: a.reshape(B, T, a.shape[-1])
        h0 = zero_state if ctx_pass else state8[:, l]
        ya, hfin = _rglru_call(
            r3(xa), r3(ga), _pad_rows(rg_conv_w[l]), rg_conv_b[l][None], wa_bd[l], wx_bd[l],
            _pad_rows(rg_ba[l]), _pad_rows(rg_bx[l]), _pad_rows(rg_lambda[l]), h0)
        if ctx_pass:
            yb, yc = _attn_ctx_call(swa_sink[l], r3(nq), r3(nk), r3(nv), r3(sq), r3(skd), r3(svd))
        else:
            yb = _nat_lat_call(r3(nq), r3(nk), r3(nv), ck_nat, cv_nat, _nat_bias_table(nat_rpb[l]), l)
            yc = _swa_lat_call(swa_sink[l], r3(sq), r3(skd), r3(svd), ckd_swa, cvd_swa, l)
        yd = _fourier_call(r3(xf))
        f2 = lambda a: a.reshape(B * T, a.shape[-1])
        x1, hp, rt = _merge_call(
            x, modb, (f2(ya), f2(yb), f2(yc), f2(yd)), w_gate[l], w_branch_b[l], w_out_b[l],
            ln_g[l, 0][None], ln_b[l, 0][None], wr_t,
            jnp.broadcast_to(router_bias[:, None], (N_EXPERTS, TM)), T, TM)
        pos, wts, seg = _route_call(rt)
        x2 = _moe_call(pos, wts, seg, hp, w1_b[l], w3_b[l], w2_b[l], x1, modb,
                       ln_g[l, 1][None], ln_b[l, 1][None], T)
        new = (nk, nv, sk, sv, hfin[:, :2]) if ctx_pass else None
        return x2, new

    y = x_prompt.reshape(B_c * T_c, D_MODEL)
    caches = []
    for l in range(DEPTH):
        y, new = layer(y, l, True)
        caches.append(new)
    y_prompt = y.reshape(B_c, T_c, D_MODEL)
    new_nat_k = jnp.stack([cc[0].reshape(B_c, T_c, NAT_HEADS, HEAD_DIM) for cc in caches], 1)
    new_nat_v = jnp.stack([cc[1].reshape(B_c, T_c, NAT_HEADS, HEAD_DIM) for cc in caches], 1)
    new_swa_k = jnp.stack([cc[2].reshape(B_c, T_c, SWA_KV_HEADS, HEAD_DIM) for cc in caches], 1)
    new_swa_v = jnp.stack([cc[3].reshape(B_c, T_c, SWA_KV_HEADS, HEAD_DIM) for cc in caches], 1)
    new_state = jnp.stack([cc[4] for cc in caches], 1)

    y = x_sample.reshape(B_l * T_l, D_MODEL)
    for l in range(DEPTH):
        y, _ = layer(y, l, False)
    y_sample = y.reshape(B_l, T_l, D_MODEL)
    return (y_prompt, y_sample, new_nat_k, new_nat_v, new_swa_k, new_swa_v, new_state)
```

```python
import functools
import math

import jax
import jax.numpy as jnp
import numpy as np
from jax import lax
from jax.experimental import pallas as pl
from jax.experimental.pallas import tpu as pltpu

F32 = jnp.float32
BF16 = jnp.bfloat16

D_MODEL = 1024
DEPTH = 4
GRID_W = 64
HEAD_DIM = 64
ATTN_SCALE = HEAD_DIM ** -0.5
NEG_INF = -1e30
LN_EPS = 1e-5
ALPHA = (2 * DEPTH) ** 0.25
ROPE_BASE = 10000.0
RNN_WIDTH = 512
RNN_BLOCKS = 8
RNN_BLOCK = RNN_WIDTH // RNN_BLOCKS
CONV_W = 4
RGLRU_C = 8.0
NAT_HEADS = 8
NAT_KR = 8
NAT_KC = 16
SWA_HEADS = 8
SWA_KV_HEADS = 2
SWA_WINDOW = 128
SWA_BLOCK = 128
SWA_SPAN = SWA_BLOCK + 2 * SWA_WINDOW
FNET_GROUPS = 4
FNET_WIDTH = 512
FNET_GC = FNET_WIDTH // FNET_GROUPS
N_BRANCH = 4
N_EXPERTS = 16
N_EXPERT_GROUPS = 4
EXPERTS_PER_GROUP = N_EXPERTS // N_EXPERT_GROUPS
D_EXPERT = 512
W512 = 512
GATE_OFF = 3840

LANES = 128
SUBLANES = 8
VMEM_LIMIT = 56 * 1024 * 1024


def _cparams(*sem):
    return pltpu.CompilerParams(dimension_semantics=sem, vmem_limit_bytes=VMEM_LIMIT)


def _const_spec(shape):
    nd = len(shape)
    return pl.BlockSpec(shape, lambda *_: (0,) * nd, pipeline_mode=pl.Buffered(1))


def _ln(x):
    mu = jnp.mean(x, -1, keepdims=True)
    xc = x - mu
    var = jnp.mean(xc * xc, -1, keepdims=True)
    return xc * lax.rsqrt(var + LN_EPS)


def _dot(a, b):
    return jnp.dot(a, b, preferred_element_type=F32)


def _dot_nt(a, b):
    return lax.dot_general(a, b, (((1,), (1,)), ((), ())), preferred_element_type=F32)


ADA_ROWS = 16
ADA_TN = 1024


def _ada_kernel(c_ref, w_ref, b_ref, o_ref):
    cv = c_ref[...]
    s = (cv * jax.nn.sigmoid(cv)).astype(BF16)
    o_ref[0] = _dot(s, w_ref[0].astype(BF16)) + b_ref[0]


def _ada_call(cv, w_ada, b_ada):
    n = w_ada.shape[-1]
    return pl.pallas_call(
        _ada_kernel,
        out_shape=jax.ShapeDtypeStruct((DEPTH, ADA_ROWS, n), F32),
        grid=(DEPTH, n // ADA_TN),
        in_specs=[
            pl.BlockSpec((ADA_ROWS, D_MODEL), lambda l, j: (0, 0)),
            pl.BlockSpec((1, D_MODEL, ADA_TN), lambda l, j: (l, 0, j)),
            pl.BlockSpec((1, 1, ADA_TN), lambda l, j: (l, 0, j)),
        ],
        out_specs=pl.BlockSpec((1, ADA_ROWS, ADA_TN), lambda l, j: (l, 0, j)),
        compiler_params=_cparams("parallel", "parallel"),
        name="ada",
    )(cv, w_ada, b_ada.reshape(DEPTH, 1, n))


def _rope(u, cos, sin):
    lane = lax.broadcasted_iota(jnp.int32, cos.shape, 1)
    first = (lane & 31) < 16
    outs = []
    for j in range(u.shape[1] // LANES):
        s = u[:, j * LANES:(j + 1) * LANES]
        partner = jnp.where(first, pltpu.roll(s, LANES - 16, 1), pltpu.roll(s, 16, 1))
        outs.append(s * cos + partner * sin)
    return outs[0] if len(outs) == 1 else jnp.concatenate(outs, -1)


def _pre_kernel(*refs, plan, rope, n1):
    x_ref, mod_ref, wm_ref, wt_ref = refs[:4]
    refs = refs[4:]
    if rope:
        cos, sin = refs[0][...], refs[1][...]
        refs = refs[2:]
    outs = refs[:len(plan)]
    m = mod_ref[0]
    h = (_ln(x_ref[...]) * (1.0 + m[1:2]) + m[0:1]).astype(BF16)
    for (src, off, width, _, scale, do_rope, fourier), o_ref in zip(plan, outs):
        w_ref = wt_ref if src else wm_ref
        u = _dot(h, w_ref[0, :, off:off + width])
        if do_rope:
            u = _rope(u, cos, sin)
        if scale != 1.0:
            u = u * scale
        if fourier and n1 > 1:
            stage = refs[len(plan)]
            for c in range(width // LANES):
                stage[c] = u[:, c * LANES:(c + 1) * LANES]
            rows = u.shape[0] // n1
            for t1 in range(n1):
                for c in range(width // LANES):
                    lo = t1 * width + c * LANES
                    o_ref[0, :, lo:lo + LANES] = stage[c, pl.ds(t1, rows, stride=n1), :].astype(o_ref.dtype)
        elif fourier:
            o_ref[0] = u.astype(o_ref.dtype)
        else:
            o_ref[...] = u.astype(o_ref.dtype)


def _pre_call(x, mod, w_main, n_main, w_tail, l, plan, B, T, TM, rope_tabs):
    N = x.shape[0]
    tpb = T // TM
    n1 = T // FN_N2
    rope = rope_tabs is not None
    n_tail = w_tail.shape[-1]
    in_specs = [
        pl.BlockSpec((TM, D_MODEL), lambda i: (i, 0)),
        pl.BlockSpec((1, SUBLANES, D_MODEL), lambda i: (i // tpb, 0, 0)),
        pl.BlockSpec((1, D_MODEL, n_main), lambda i: (l, 0, 0), pipeline_mode=pl.Buffered(1)),
        pl.BlockSpec((1, D_MODEL, n_tail), lambda i: (l, 0, 0), pipeline_mode=pl.Buffered(1)),
    ]
    args = [x, mod, w_main, w_tail]
    if rope:
        in_specs += [pl.BlockSpec((TM, LANES), lambda i: (i % tpb, 0))] * 2
        args += list(rope_tabs)
    out_shape, out_specs, scratch = [], [], []
    for p in plan:
        if p[6]:
            out_shape.append(jax.ShapeDtypeStruct((B, FN_N2, n1 * p[2]), p[3]))
            out_specs.append(pl.BlockSpec((1, TM // n1, n1 * p[2]), lambda i: (i // tpb, i % tpb, 0)))
            if n1 > 1:
                scratch.append(pltpu.VMEM((p[2] // LANES, TM, LANES), F32))
        else:
            out_shape.append(jax.ShapeDtypeStruct((N, p[2]), p[3]))
            out_specs.append(pl.BlockSpec((TM, p[2]), lambda i: (i, 0)))
    return pl.pallas_call(
        functools.partial(_pre_kernel, plan=plan, rope=rope, n1=n1),
        out_shape=out_shape,
        grid=(N // TM,),
        in_specs=in_specs,
        out_specs=out_specs,
        scratch_shapes=scratch,
        compiler_params=_cparams("parallel"),
        name="pre",
    )(*args)


RG_CW = 256
RG_TCH = 256


def _sigmoid(x):
    return 0.5 * jnp.tanh(0.5 * x) + 0.5


def _rglru_kernel(xa_ref, ga_ref, cw_ref, cb_ref, wa_ref, wx_ref, ba_ref, bx_ref, lam_ref, h0_ref,
                  y_ref, hfin_ref, xpad, a_f, u_f, a_b, u_b, *, T):
    cw_ = RG_CW
    zeros8 = jnp.zeros((SUBLANES, cw_), F32)
    xpad[0:SUBLANES, :] = zeros8
    xpad[T + SUBLANES:T + 2 * SUBLANES, :] = zeros8
    xpad[SUBLANES:T + SUBLANES, :] = xa_ref[0]
    cw = cw_ref[...]
    cb = cb_ref[...]
    lam = lam_ref[...]
    sp = jnp.maximum(-lam, 0.0) + jnp.log1p(jnp.exp(-jnp.abs(lam)))
    ba = ba_ref[...]
    bx = bx_ref[...]
    h0 = h0_ref[0]

    def chunk(c, carry):
        base = pl.multiple_of(c * RG_TCH, RG_TCH)
        xw = xpad[pl.ds(base, RG_TCH + 2 * SUBLANES), :]
        xc = cb
        for i in range(CONV_W):
            xc = xc + cw[i:i + 1] * xw[SUBLANES - 1 + i:SUBLANES - 1 + i + RG_TCH]
        xcb = xc.astype(BF16)
        for d, (a_s, u_s) in enumerate(((a_f, u_f), (a_b, u_b))):
            r = _sigmoid(_dot(xcb, wa_ref[d]) + ba[d:d + 1])
            i_g = _sigmoid(_dot(xcb, wx_ref[d]) + bx[d:d + 1])
            log_a = -RGLRU_C * r * sp[d:d + 1]
            a = jnp.exp(log_a)
            gain = jnp.sqrt(jnp.tanh(-log_a) * (1.0 + a * a))
            a_s[pl.ds(base, RG_TCH), :] = a
            u_s[pl.ds(base, RG_TCH), :] = gain * (i_g * xc)
        return carry

    lax.fori_loop(0, T // RG_TCH, chunk, 0)

    row = lax.broadcasted_iota(jnp.int32, (SUBLANES, cw_), 0)

    def block_scan(a, u, reverse):
        for dd in (1, 2, 4):
            sh = SUBLANES - dd if reverse else dd
            a_n = pltpu.roll(a, sh, 0)
            u_n = pltpu.roll(u, sh, 0)
            ok = (row < SUBLANES - dd) if reverse else (row >= dd)
            u = jnp.where(ok, u + a * u_n, u)
            a = jnp.where(ok, a * a_n, a)
        return a, u

    nblk = T // SUBLANES

    def scan(i, carry):
        c_f, c_b = carry
        lo_f = pl.multiple_of(i * SUBLANES, SUBLANES)
        lo_b = pl.multiple_of((nblk - 1 - i) * SUBLANES, SUBLANES)
        a, u = block_scan(a_f[pl.ds(lo_f, SUBLANES), :], u_f[pl.ds(lo_f, SUBLANES), :], False)
        h_f = u + a * c_f
        u_f[pl.ds(lo_f, SUBLANES), :] = h_f
        a, u = block_scan(a_b[pl.ds(lo_b, SUBLANES), :], u_b[pl.ds(lo_b, SUBLANES), :], True)
        h_b = u + a * c_b
        u_b[pl.ds(lo_b, SUBLANES), :] = h_b
        return (jnp.broadcast_to(h_f[SUBLANES - 1:SUBLANES, :], (SUBLANES, cw_)),
                jnp.broadcast_to(h_b[0:1, :], (SUBLANES, cw_)))

    c_f, c_b = lax.fori_loop(
        0, nblk, scan,
        (jnp.broadcast_to(h0[0:1, :], (SUBLANES, cw_)), jnp.broadcast_to(h0[1:2, :], (SUBLANES, cw_))),
        unroll=2)
    hfin_ref[0] = jnp.where(row == 0, c_f, jnp.where(row == 1, c_b, 0.0))

    def emit(c, carry):
        base = pl.multiple_of(c * RG_TCH, RG_TCH)
        h = u_f[pl.ds(base, RG_TCH), :] + u_b[pl.ds(base, RG_TCH), :]
        y_ref[0, pl.ds(base, RG_TCH), :] = (h * jax.nn.gelu(ga_ref[0, pl.ds(base, RG_TCH), :])).astype(y_ref.dtype)
        return carry

    lax.fori_loop(0, T // RG_TCH, emit, 0)


def _rglru_call(xa, ga, cw, cb, wa, wx, ba, bx, lam, h0):
    B, T, _ = xa.shape
    nj = RNN_WIDTH // RG_CW
    seq = pl.BlockSpec((1, T, RG_CW), lambda b, j: (b, 0, j))
    vec8 = pl.BlockSpec((SUBLANES, RG_CW), lambda b, j: (0, j))
    wsp = pl.BlockSpec((2, RG_CW, RG_CW), lambda b, j: (0, j, j))
    st = pl.BlockSpec((1, SUBLANES, RG_CW), lambda b, j: (b, 0, j))
    return pl.pallas_call(
        functools.partial(_rglru_kernel, T=T),
        out_shape=[jax.ShapeDtypeStruct((B, T, RNN_WIDTH), BF16),
                   jax.ShapeDtypeStruct((B, SUBLANES, RNN_WIDTH), F32)],
        grid=(B, nj),
        in_specs=[seq, seq, vec8, pl.BlockSpec((1, RG_CW), lambda b, j: (0, j)),
                  wsp, wsp, vec8, vec8, vec8, st],
        out_specs=[seq, st],
        scratch_shapes=[pltpu.VMEM((T + 2 * SUBLANES, RG_CW), F32)]
        + [pltpu.VMEM((T, RG_CW), F32)] * 4,
        compiler_params=_cparams("parallel", "parallel"),
        name="rglru",
    )(xa, ga, cw, cb, wa, wx, ba, bx, lam, h0)


def _attend_slabs(jobs):
    M = jobs[0][0].shape[0]
    lo = lax.broadcasted_iota(jnp.int32, (M, LANES), 1) < HEAD_DIM
    scores = []
    for q2, srcs, _ in jobs:
        zero = jnp.zeros_like(q2)
        qs = jnp.concatenate([jnp.where(lo, q2, zero), jnp.where(lo, zero, q2)], 0)
        ss = []
        for k, _, bias in srcs:
            s = _dot_nt(qs, k)
            ss.append(s if bias is None else s + bias)
        scores.append(ss)
    maxima = []
    for (_, _, sink_col), ss in zip(jobs, scores):
        m = jnp.max(ss[0], -1, keepdims=True)
        for s in ss[1:]:
            m = jnp.maximum(m, jnp.max(s, -1, keepdims=True))
        maxima.append(m if sink_col is None else jnp.maximum(m, sink_col))
    outs = []
    for (_, srcs, sink_col), ss, m in zip(jobs, scores, maxima):
        den = None
        o = None
        for s, (_, v, _) in zip(ss, srcs):
            e = jnp.exp(s - m)
            d_ = jnp.sum(e, -1, keepdims=True)
            o_ = _dot(e.astype(BF16), v)
            den = d_ if den is None else den + d_
            o = o_ if o is None else o + o_
        if sink_col is not None:
            den = den + jnp.exp(sink_col - m)
        o = o / den
        outs.append(jnp.where(lo, o[:M], o[M:]))
    return outs


def _sink_col(sink_ref, j, M):
    r = lax.broadcasted_iota(jnp.int32, (2 * M, 1), 0)
    return jnp.where(r < M, sink_ref[2 * j], sink_ref[2 * j + 1])


def _slab(j):
    return slice(j * LANES, (j + 1) * LANES)


def _attn_ctx_kernel(sink_ref, nq_ref, nk_ref, nv_ref, sq_ref, skd_ref, svd_ref, yb_ref, yc_ref):
    M = nq_ref.shape[1]
    nslab = W512 // LANES
    jobs = []
    for j in range(nslab):
        k2 = nk_ref[0, :, _slab(j)].astype(BF16)
        v2 = nv_ref[0, :, _slab(j)].astype(BF16)
        jobs.append((nq_ref[0, :, _slab(j)], [(k2, v2, None)], None))
    for j in range(nslab):
        g = j // 2
        jobs.append((sq_ref[0, :, _slab(j)], [(skd_ref[0, :, _slab(g)], svd_ref[0, :, _slab(g)], None)],
                     _sink_col(sink_ref, j, M)))
    outs = _attend_slabs(jobs)
    for j in range(nslab):
        yb_ref[0, :, _slab(j)] = outs[j].astype(yb_ref.dtype)
        yc_ref[0, :, _slab(j)] = outs[nslab + j].astype(yc_ref.dtype)


def _attn_ctx_call(sink, nq, nk, nv, sq, skd, svd):
    B, T, _ = nq.shape
    s512 = pl.BlockSpec((1, T, W512), lambda b: (b, 0, 0))
    s256 = pl.BlockSpec((1, T, 2 * LANES), lambda b: (b, 0, 0))
    return pl.pallas_call(
        _attn_ctx_kernel,
        out_shape=[jax.ShapeDtypeStruct((B, T, W512), BF16)] * 2,
        grid=(B,),
        in_specs=[pl.BlockSpec(memory_space=pltpu.SMEM), s512, s512, s512, s512, s256, s256],
        out_specs=[s512, s512],
        compiler_params=_cparams("parallel"),
        name="attn_ctx",
    )(sink, nq, nk, nv, sq, skd, svd)


NAT_NLOC = NAT_KR * GRID_W


def _nat_lat_kernel(q_ref, k_ref, v_ref, ck_ref, cv_ref, bias_ref, y_ref, *, rows):
    r = pl.program_id(1)
    rstart = jnp.clip(r - NAT_KR // 2, 0, rows - NAT_KR)
    d = r - rstart
    kbase = pl.multiple_of(rstart * GRID_W, GRID_W)
    jobs = []
    for j in range(W512 // LANES):
        k2 = k_ref[0, pl.ds(kbase, NAT_NLOC), _slab(j)]
        v2 = v_ref[0, pl.ds(kbase, NAT_NLOC), _slab(j)]
        jobs.append((q_ref[0, :, _slab(j)],
                     [(k2, v2, bias_ref[j, d]), (ck_ref[0, 0, :, _slab(j)], cv_ref[0, 0, :, _slab(j)], None)], None))
    for j, o in enumerate(_attend_slabs(jobs)):
        y_ref[0, :, _slab(j)] = o.astype(y_ref.dtype)


def _nat_lat_call(q, k, v, ck, cv, bias, l):
    B, T, _ = q.shape
    rows = T // GRID_W
    P = ck.shape[2]
    qs = pl.BlockSpec((1, GRID_W, W512), lambda b, r: (b, r, 0))
    full = pl.BlockSpec((1, T, W512), lambda b, r: (b, 0, 0))
    cs = pl.BlockSpec((1, 1, P, W512), lambda b, r: (b, l, 0, 0))
    return pl.pallas_call(
        functools.partial(_nat_lat_kernel, rows=rows),
        out_shape=jax.ShapeDtypeStruct((B, T, W512), BF16),
        grid=(B, rows),
        in_specs=[qs, full, full, cs, cs, _const_spec(bias.shape)],
        out_specs=qs,
        compiler_params=_cparams("parallel", "arbitrary"),
        name="nat_lat",
    )(q, k, v, ck, cv, bias)


def _swa_lat_kernel(sink_ref, q_ref, kd_ref, vd_ref, ckd_ref, cvd_ref, y_ref, *, T):
    blk = pl.program_id(1)
    M = SWA_BLOCK
    start = blk * SWA_BLOCK
    ks = pl.multiple_of(jnp.clip(start - SWA_WINDOW, 0, T - SWA_SPAN), SWA_BLOCK)
    kpos = ks + lax.broadcasted_iota(jnp.int32, (2 * M, SWA_SPAN), 1)
    qpos = start + (lax.broadcasted_iota(jnp.int32, (2 * M, SWA_SPAN), 0) & (M - 1))
    dist = kpos - qpos
    bias = jnp.where(dist > SWA_WINDOW, NEG_INF, jnp.where(dist < -SWA_WINDOW, NEG_INF, 0.0))
    jobs = []
    for j in range(W512 // LANES):
        g = j // 2
        jobs.append((q_ref[0, :, _slab(j)],
                     [(kd_ref[0, pl.ds(ks, SWA_SPAN), _slab(g)], vd_ref[0, pl.ds(ks, SWA_SPAN), _slab(g)], bias),
                      (ckd_ref[0, 0, :, _slab(g)], cvd_ref[0, 0, :, _slab(g)], None)],
                     _sink_col(sink_ref, j, M)))
    for j, o in enumerate(_attend_slabs(jobs)):
        y_ref[0, :, _slab(j)] = o.astype(y_ref.dtype)


def _swa_lat_call(sink, q, kd, vd, ckd, cvd, l):
    B, T, _ = q.shape
    P = ckd.shape[2]
    qs = pl.BlockSpec((1, SWA_BLOCK, W512), lambda b, i: (b, i, 0))
    full = pl.BlockSpec((1, T, 2 * LANES), lambda b, i: (b, 0, 0))
    cs = pl.BlockSpec((1, 1, P, 2 * LANES), lambda b, i: (b, l, 0, 0))
    return pl.pallas_call(
        functools.partial(_swa_lat_kernel, T=T),
        out_shape=jax.ShapeDtypeStruct((B, T, W512), BF16),
        grid=(B, T // SWA_BLOCK),
        in_specs=[pl.BlockSpec(memory_space=pltpu.SMEM), qs, full, full, cs, cs],
        out_specs=qs,
        compiler_params=_cparams("parallel", "arbitrary"),
        name="swa_lat",
    )(sink, q, kd, vd, ckd, cvd)


FN_N2 = 256
FN_GP = 2


def _fourier_kernel(x_ref, cc_ref, m_ref, y_ref, *scratch, n1, coef):
    gw = FN_GP * FNET_GC
    cc = cc_ref[...]
    for gp in range(FNET_GROUPS // FN_GP):
        parts = []
        for t1 in range(n1):
            ws = []
            for g in range(FN_GP):
                lo = t1 * FNET_WIDTH + (gp * FN_GP + g) * FNET_GC
                ws.append(_dot(x_ref[0, :, lo:lo + FNET_GC], cc))
            wr = jnp.concatenate([w[:, :FNET_GC] for w in ws], -1)
            wi = jnp.concatenate([w[:, FNET_GC:] for w in ws], -1)
            v = jnp.concatenate([wr, wi], 0).astype(BF16)
            b = _dot(m_ref[t1], v)
            if n1 == 1:
                y_ref[0, :, gp * gw:(gp + 1) * gw] = b.astype(y_ref.dtype)
            else:
                scratch[0][t1] = b
        if n1 > 1:
            bs = scratch[0]
            for k1 in range(n1):
                acc = None
                for t1 in range(n1):
                    c, s = coef[k1][t1]
                    for cf, lo in ((c, 0), (s, FN_N2)):
                        if cf == 0.0:
                            continue
                        blk = bs[t1, lo:lo + FN_N2, :]
                        term = blk if cf == 1.0 else (-blk if cf == -1.0 else cf * blk)
                        acc = term if acc is None else acc + term
                y_ref[0, k1 * FN_N2:(k1 + 1) * FN_N2, gp * gw:(gp + 1) * gw] = acc.astype(y_ref.dtype)


def _dft_consts(T):
    n1 = T // FN_N2
    j = np.arange(FNET_GC)
    ang = 2 * np.pi * np.outer(j, j) / FNET_GC
    sc = 1.0 / math.sqrt(FNET_GC)
    cc = np.concatenate([np.cos(ang) * sc, -np.sin(ang) * sc], 1)
    k2 = np.arange(FN_N2)[:, None]
    t2 = np.arange(FN_N2)[None, :]
    st = 1.0 / math.sqrt(T)
    mats = []
    for t1 in range(n1):
        th = 2 * np.pi * ((k2 * (n1 * t2 + t1)) % T) / T
        c, s = np.cos(th) * st, np.sin(th) * st
        top = np.concatenate([c, s], 1)
        mats.append(top if n1 == 1 else np.concatenate([top, np.concatenate([-s, c], 1)], 0))
    coef = []
    for k1 in range(n1):
        rowc = []
        for t1 in range(n1):
            q = (k1 * t1) % n1
            c, s = math.cos(2 * math.pi * q / n1), math.sin(2 * math.pi * q / n1)
            c = 0.0 if abs(c) < 1e-12 else (1.0 if abs(c - 1) < 1e-12 else (-1.0 if abs(c + 1) < 1e-12 else c))
            s = 0.0 if abs(s) < 1e-12 else (1.0 if abs(s - 1) < 1e-12 else (-1.0 if abs(s + 1) < 1e-12 else s))
            rowc.append((c, s))
        coef.append(tuple(rowc))
    return n1, jnp.asarray(cc, BF16), jnp.asarray(np.stack(mats), BF16), tuple(coef)


def _fourier_call(xv, T):
    B = xv.shape[0]
    n1, cc, mats, coef = _dft_consts(T)
    gw = FN_GP * FNET_GC
    scratch = [pltpu.VMEM((n1, 2 * FN_N2, gw), F32)] if n1 > 1 else []
    return pl.pallas_call(
        functools.partial(_fourier_kernel, n1=n1, coef=coef),
        out_shape=jax.ShapeDtypeStruct((B, T, FNET_WIDTH), BF16),
        grid=(B,),
        in_specs=[pl.BlockSpec((1, FN_N2, n1 * FNET_WIDTH), lambda b: (b, 0, 0)),
                  _const_spec(cc.shape), _const_spec(mats.shape)],
        out_specs=pl.BlockSpec((1, T, FNET_WIDTH), lambda b: (b, 0, 0)),
        scratch_shapes=scratch,
        compiler_params=_cparams("parallel"),
        name="fourier",
    )(xv, cc, mats)


def _route(scores, biased):
    one, zero = jnp.float32(1.0), jnp.float32(0.0)
    in2, gscore = [], []
    for g in range(N_EXPERT_GROUPS):
        vs = [biased[g * EXPERTS_PER_GROUP + j:g * EXPERTS_PER_GROUP + j + 1] for j in range(EXPERTS_PER_GROUP)]
        gs = None
        for j in range(EXPERTS_PER_GROUP):
            rank = None
            for i in range(EXPERTS_PER_GROUP):
                if i == j:
                    continue
                beats = (vs[i] >= vs[j]) if i < j else (vs[i] > vs[j])
                t = jnp.where(beats, one, zero)
                rank = t if rank is None else rank + t
            keep = jnp.where(rank < 2.0, one, zero)
            in2.append(keep)
            t = keep * vs[j]
            gs = t if gs is None else gs + t
        gscore.append(gs)
    rows, sel = [], []
    for g in range(N_EXPERT_GROUPS):
        lost = None
        for i in range(N_EXPERT_GROUPS):
            if i == g:
                continue
            beats = (gscore[i] >= gscore[g]) if i < g else (gscore[i] > gscore[g])
            t = jnp.where(beats, one, zero)
            lost = t if lost is None else lost + t
        gsel = jnp.where(lost < 1.0, one, zero)
        for j in range(EXPERTS_PER_GROUP):
            e = g * EXPERTS_PER_GROUP + j
            sel.append(gsel * in2[e])
            rows.append(sel[-1] * scores[e:e + 1])
    tot = rows[0]
    for rr in rows[1:]:
        tot = tot + rr
    return jnp.concatenate(rows, 0) / tot, jnp.concatenate(sel, 0)


def _pack_halves(v):
    w = v.shape[1] // 2
    lo = lax.bitcast_convert_type(v[:, :w].astype(F32), jnp.uint32)
    hi = lax.bitcast_convert_type(v[:, w:].astype(F32), jnp.uint32)
    return (lo >> 16) | hi


def _unpack_halves(p):
    lo = lax.bitcast_convert_type(p << 16, F32).astype(BF16)
    hi = lax.bitcast_convert_type(p & jnp.uint32(0xFFFF0000), F32).astype(BF16)
    return lo, hi


def _merge_kernel(x_ref, mod_ref, ya_ref, yb_ref, yc_ref, yd_ref, wg_ref, wb_ref, wo_ref,
                  lng_ref, lnb_ref, wr_ref, rb_ref, x1_ref, hp_ref, rt_ref):
    x = x_ref[...]
    m = mod_ref[0]
    h = (_ln(x) * (1.0 + m[1:2]) + m[0:1]).astype(BF16)
    merged = None
    for b, y_ref in enumerate((ya_ref, yb_ref, yc_ref, yd_ref)):
        gate = jax.nn.sigmoid(_dot(h, wg_ref[:, b * D_MODEL:(b + 1) * D_MODEL]))
        term = gate * _dot(y_ref[...], wb_ref[b])
        merged = term if merged is None else merged + term
    out = _dot(merged.astype(BF16), wo_ref[...])
    x1 = _ln(ALPHA * x + m[2:3] * out) * lng_ref[...] + lnb_ref[...]
    x1_ref[...] = x1
    h2 = (_ln(x1) * (1.0 + m[4:5]) + m[3:4]).astype(BF16)
    hp_ref[...] = _pack_halves(h2)
    logits = _dot_nt(wr_ref[...], h2)
    e = jnp.exp(logits - jnp.max(logits, 0, keepdims=True))
    scores = e / jnp.sum(e, 0, keepdims=True)
    comb, sel = _route(scores, scores + rb_ref[...])
    rt_ref[...] = jnp.concatenate([comb, sel], 0)


def _merge_call(x, mod, ys, wg, wb, wo, lng, lnb, wr_t, rb, T, TM):
    N = x.shape[0]
    tpb = T // TM
    tok = lambda w: pl.BlockSpec((TM, w), lambda i: (i, 0))
    return pl.pallas_call(
        _merge_kernel,
        out_shape=[jax.ShapeDtypeStruct((N, D_MODEL), F32), jax.ShapeDtypeStruct((N, D_MODEL // 2), jnp.uint32),
                   jax.ShapeDtypeStruct((2 * N_EXPERTS, N), F32)],
        grid=(N // TM,),
        in_specs=[tok(D_MODEL), pl.BlockSpec((1, SUBLANES, D_MODEL), lambda i: (i // tpb, 0, 0)),
                  tok(W512), tok(W512), tok(W512), tok(W512),
                  _const_spec(wg.shape), _const_spec(wb.shape), _const_spec(wo.shape),
                  _const_spec(lng.shape), _const_spec(lnb.shape), _const_spec(wr_t.shape), _const_spec(rb.shape)],
        out_specs=[tok(D_MODEL), tok(D_MODEL // 2), pl.BlockSpec((2 * N_EXPERTS, TM), lambda i: (0, i))],
        compiler_params=_cparams("parallel"),
        name="merge",
    )(x, mod, *ys, wg, wb, wo, lng, lnb, wr_t, rb)


MOE_TM = 2048
MOE_RB = 256
MOE_KC = 4
MOE_CH = MOE_TM // MOE_KC
MOE_SLOTS = 2 * MOE_TM + N_EXPERTS * SUBLANES + MOE_RB
MOE_PR = 2 * MOE_TM // LANES
MOE_UNROLL = 8


def _route_kernel(rt_ref, u_ref, pos_ref, wts_ref, seg_ref):
    comb = rt_ref[0:N_EXPERTS, :]
    sel = rt_ref[N_EXPERTS:2 * N_EXPERTS, :]
    nb = MOE_TM // LANES
    stacked = jnp.concatenate([sel[:, b * LANES:(b + 1) * LANES] for b in range(nb)], 0)
    within = _dot(stacked.astype(BF16), u_ref[...])
    tot = jnp.sum(stacked, -1, keepdims=True)
    base = jnp.zeros((N_EXPERTS, 1), F32)
    bases = []
    for b in range(nb):
        bases.append(base)
        base = base + tot[b * N_EXPERTS:(b + 1) * N_EXPERTS]
    cnt = base
    padded = jnp.floor((cnt + (SUBLANES - 1.0)) * (1.0 / SUBLANES)) * SUBLANES
    rowi = lax.broadcasted_iota(jnp.int32, (N_EXPERTS, 1), 0)
    off = jnp.zeros((N_EXPERTS, 1), F32)
    for e in range(N_EXPERTS - 1):
        off = off + jnp.where(rowi > e, padded[e:e + 1, :], 0.0)
    pos_rows, wts_rows = [], []
    for b in range(nb):
        slot = within[b * N_EXPERTS:(b + 1) * N_EXPERTS] + (bases[b] + off)
        seen = jnp.zeros((1, LANES), F32)
        acc = [jnp.zeros((1, LANES), F32) for _ in range(4)]
        for e in range(N_EXPERTS):
            s_e = sel[e:e + 1, b * LANES:(b + 1) * LANES]
            c_e = comb[e:e + 1, b * LANES:(b + 1) * LANES]
            first = jnp.where(seen == 0.0, s_e, 0.0)
            second = jnp.where(seen == 1.0, s_e, 0.0)
            acc[0] = acc[0] + first * slot[e:e + 1]
            acc[1] = acc[1] + second * slot[e:e + 1]
            acc[2] = acc[2] + first * c_e
            acc[3] = acc[3] + second * c_e
            seen = seen + s_e
        pos_rows += acc[0:2]
        wts_rows += acc[2:4]
    pos_ref[0] = jnp.concatenate(pos_rows, 0).astype(jnp.int32)
    wts_ref[0] = jnp.concatenate(wts_rows, 0)
    lane = lax.broadcasted_iota(jnp.int32, (N_EXPERTS, LANES), 1)
    diag = lane == lax.broadcasted_iota(jnp.int32, (N_EXPERTS, LANES), 0)
    off_row = jnp.sum(jnp.where(diag, off, 0.0), 0, keepdims=True)
    cnt_row = jnp.sum(jnp.where(diag, cnt, 0.0), 0, keepdims=True)
    seg_ref[0] = jnp.concatenate([off_row, cnt_row, jnp.zeros((SUBLANES - 2, LANES), F32)], 0).astype(jnp.int32)


def _route_call(rt):
    N = rt.shape[1]
    nt = N // MOE_TM
    u = jnp.asarray(np.triu(np.ones((LANES, LANES), np.float32), 1), BF16)
    return pl.pallas_call(
        _route_kernel,
        out_shape=[jax.ShapeDtypeStruct((nt, MOE_PR, LANES), jnp.int32),
                   jax.ShapeDtypeStruct((nt, MOE_PR, LANES), F32),
                   jax.ShapeDtypeStruct((nt, SUBLANES, LANES), jnp.int32)],
        grid=(nt,),
        in_specs=[pl.BlockSpec((2 * N_EXPERTS, MOE_TM), lambda i: (0, i)), _const_spec(u.shape)],
        out_specs=[pl.BlockSpec((1, MOE_PR, LANES), lambda i: (i, 0, 0)),
                   pl.BlockSpec((1, MOE_PR, LANES), lambda i: (i, 0, 0)),
                   pl.BlockSpec((1, SUBLANES, LANES), lambda i: (i, 0, 0))],
        compiler_params=_cparams("parallel"),
        name="route",
    )(rt, u)


def _table_index(it):
    per_block = LANES // MOE_UNROLL
    return (it // per_block) * (2 * LANES) + (it % per_block) * MOE_UNROLL


def _moe_kernel(pos_ref, wts_ref, seg_ref, hp_ref, w1_ref, w3_ref, w2_ref, x1_ref, mod_ref, lng_ref, lnb_ref,
                o_ref, xs, ys):
    s = pl.program_id(1)
    half = D_MODEL // 2

    @pl.when(s == 0)
    def _dispatch():
        xs[...] = jnp.zeros_like(xs)

        def body(it, carry):
            t0 = pl.multiple_of(it * MOE_UNROLL, MOE_UNROLL)
            i0 = _table_index(it)
            for j in range(MOE_UNROLL):
                row = hp_ref[pl.ds(t0 + j, 1), :]
                xs[pl.ds(pos_ref[i0 + j], 1), :] = row
                xs[pl.ds(pos_ref[i0 + LANES + j], 1), :] = row
            return carry

        lax.fori_loop(0, MOE_TM // MOE_UNROLL, body, 0)

    @pl.when(s < N_EXPERTS)
    def _expert():
        off = seg_ref[0, 0, s]
        nblk = (seg_ref[0, 1, s] + (MOE_RB - 1)) // MOE_RB

        def blk(i, carry):
            r0 = pl.multiple_of(off + i * MOE_RB, SUBLANES)
            lo, hi = _unpack_halves(xs[pl.ds(r0, MOE_RB), :])
            a = _dot(lo, w1_ref[0, :half, :]) + _dot(hi, w1_ref[0, half:, :])
            g = _dot(lo, w3_ref[0, :half, :]) + _dot(hi, w3_ref[0, half:, :])
            act = (a * jax.nn.sigmoid(a) * g).astype(BF16)
            ys[pl.ds(r0, MOE_RB), :] = _dot(act, w2_ref[0])
            return carry

        lax.fori_loop(0, nblk, blk, 0)

    @pl.when(s >= N_EXPERTS)
    def _combine():
        it0 = (s - N_EXPERTS) * (MOE_CH // MOE_UNROLL)

        def body(it, carry):
            t0 = pl.multiple_of(it * MOE_UNROLL, MOE_UNROLL)
            i0 = _table_index(it0 + it)
            for j in range(MOE_UNROLL):
                o_ref[pl.ds(t0 + j, 1), :] = (
                    wts_ref[i0 + j] * ys[pl.ds(pos_ref[i0 + j], 1), :]
                    + wts_ref[i0 + LANES + j] * ys[pl.ds(pos_ref[i0 + LANES + j], 1), :])
            return carry

        lax.fori_loop(0, MOE_CH // MOE_UNROLL, body, 0)
        m = mod_ref[0]
        o_ref[...] = _ln(ALPHA * x1_ref[...] + m[5:6] * o_ref[...]) * lng_ref[...] + lnb_ref[...]


def _moe_call(pos, wts, seg, hp, w1, w3, w2, x1, mod, lng, lnb, T):
    N = hp.shape[0]
    nt = N // MOE_TM
    flat = pl.BlockSpec((2 * MOE_TM,), lambda i, s: (i,), memory_space=pltpu.SMEM)
    wspec = lambda shp: pl.BlockSpec((1,) + shp, lambda i, s: (jnp.minimum(s, N_EXPERTS - 1), 0, 0))
    chunk = pl.BlockSpec((MOE_CH, D_MODEL),
                         lambda i, s: (i * MOE_KC + jnp.clip(s - N_EXPERTS, 0, MOE_KC - 1), 0))
    return pl.pallas_call(
        _moe_kernel,
        out_shape=jax.ShapeDtypeStruct((N, D_MODEL), F32),
        grid=(nt, N_EXPERTS + MOE_KC),
        in_specs=[flat, flat,
                  pl.BlockSpec((1, SUBLANES, LANES), lambda i, s: (i, 0, 0), memory_space=pltpu.SMEM),
                  pl.BlockSpec((MOE_TM, D_MODEL // 2), lambda i, s: (i, 0)),
                  wspec((D_MODEL, D_EXPERT)), wspec((D_MODEL, D_EXPERT)), wspec((D_EXPERT, D_MODEL)),
                  chunk, pl.BlockSpec((1, SUBLANES, D_MODEL), lambda i, s: ((i * MOE_TM) // T, 0, 0)),
                  _const_spec(lng.shape), _const_spec(lnb.shape)],
        out_specs=chunk,
        scratch_shapes=[pltpu.VMEM((MOE_SLOTS, D_MODEL // 2), jnp.uint32), pltpu.VMEM((MOE_SLOTS, D_MODEL), F32)],
        compiler_params=_cparams("parallel", "arbitrary"),
        name="moe",
    )(pos.reshape(-1), wts.reshape(-1), seg, hp, w1, w3, w2, x1, mod, lng, lnb)


def _rope_tables(T):
    t = jnp.arange(T)
    nf = HEAD_DIM // 4
    inv = ROPE_BASE ** (-jnp.arange(nf, dtype=F32) / nf)
    ar = (t // GRID_W).astype(F32)[:, None] * inv
    ac = (t % GRID_W).astype(F32)[:, None] * inv
    cos = jnp.concatenate([jnp.cos(ar), jnp.cos(ar), jnp.cos(ac), jnp.cos(ac)], -1)
    sin = jnp.concatenate([-jnp.sin(ar), jnp.sin(ar), -jnp.sin(ac), jnp.sin(ac)], -1)
    return jnp.tile(cos, (1, LANES // HEAD_DIM)), jnp.tile(sin, (1, LANES // HEAD_DIM))


def _nat_bias_table(rpb):
    q = np.arange(GRID_W)
    kc = np.arange(GRID_W)
    cstart = np.clip(q - NAT_KC // 2, 0, GRID_W - NAT_KC)
    ok = (kc[None, :] >= cstart[:, None]) & (kc[None, :] < cstart[:, None] + NAT_KC)
    cidx = np.clip(kc[None, :] - q[:, None] + NAT_KC - 1, 0, 2 * NAT_KC - 2)
    onehot = (np.arange(2 * NAT_KC - 1)[:, None, None] == cidx[None]).astype(np.float32)
    t = jnp.einsum('hrc,cqk->hqrk', rpb.astype(F32), jnp.asarray(onehot), precision=lax.Precision.HIGHEST)
    t = jnp.where(jnp.asarray(ok)[None, :, None, :], t, NEG_INF)
    per_d = [t[:, :, NAT_KR - 1 - d:2 * NAT_KR - 1 - d, :].reshape(NAT_HEADS // 2, 2 * GRID_W, NAT_KR * GRID_W)
             for d in range(NAT_KR)]
    return jnp.stack(per_d, 1)


def _dup_heads(a):
    a = jnp.broadcast_to(a[..., :, None, :], a.shape[:-1] + (2, a.shape[-1]))
    return a.reshape(a.shape[:-3] + (-1,))


def _block_diag(w):
    eye = jnp.eye(RNN_BLOCKS, dtype=w.dtype)
    return jnp.einsum('dnio,nm->dnimo', w, eye).reshape(2, RNN_WIDTH, RNN_WIDTH)


def _pad_rows(a, rows=SUBLANES):
    return jnp.pad(a, ((0, rows - a.shape[0]),) + ((0, 0),) * (a.ndim - 1))


def kernel(x_prompt, x_sample, c, cache_nat_k, cache_nat_v, cache_swa_k, cache_swa_v, state_rglru, c_ctx,
           w_ada, b_ada, w_in, rg_conv_w, rg_conv_b, rg_wa, rg_ba, rg_wx, rg_bx, rg_lambda, nat_rpb,
           swa_sink, w_branch, w_out, ln_g, ln_b, w_router, router_bias, w1, w3, w2):
    B_c, T_c, _ = x_prompt.shape
    B_l, T_l, _ = x_sample.shape
    P = cache_nat_k.shape[2]

    cv = jnp.concatenate([c_ctx[None], c, jnp.zeros((ADA_ROWS - 1 - B_l, D_MODEL), F32)], 0)
    mod_all = _ada_call(cv, w_ada, b_ada).reshape(DEPTH, ADA_ROWS, 6, D_MODEL)
    mod_all = jnp.pad(mod_all, ((0, 0), (0, 0), (0, SUBLANES - 6), (0, 0)))

    sk0 = 3072
    sv0 = sk0 + SWA_KV_HEADS * HEAD_DIM
    xf0 = sv0 + SWA_KV_HEADS * HEAD_DIM
    w_in_b = w_in.astype(BF16)
    head = lambda c0, g: w_in_b[:, :, c0 + g * HEAD_DIM:c0 + (g + 1) * HEAD_DIM]
    w_tail_ctx = jnp.concatenate([head(c0, g) for c0 in (sk0, sv0) for g in (0, 0, 1, 1)], -1)
    w_tail_lat = jnp.concatenate([w_in_b[:, :, xf0:GATE_OFF], w_tail_ctx], -1)
    w_gate = w_in_b[:, :, GATE_OFF:]
    w_branch_b = w_branch.astype(BF16)
    w_out_b = w_out.astype(BF16)
    w1_b, w3_b, w2_b = w1.astype(BF16), w3.astype(BF16), w2.astype(BF16)
    wr_t = w_router.T.astype(BF16)
    wa_bd = jnp.stack([_block_diag(rg_wa[l]) for l in range(DEPTH)]).astype(BF16)
    wx_bd = jnp.stack([_block_diag(rg_wx[l]) for l in range(DEPTH)]).astype(BF16)

    plan_ctx = ((0, 0, 512, F32, 1.0, False, False), (0, 512, 512, F32, 1.0, False, False),
                (0, 1024, 512, BF16, ATTN_SCALE, False, False), (0, 1536, 512, F32, 1.0, False, False),
                (0, 2048, 512, F32, 1.0, False, False), (0, 2560, 512, BF16, ATTN_SCALE, False, False),
                (0, sk0, 128, F32, 1.0, False, False), (0, sv0, 128, F32, 1.0, False, False),
                (0, xf0, 512, BF16, 1.0, False, True),
                (1, 0, 256, BF16, 1.0, False, False), (1, 256, 256, BF16, 1.0, False, False))
    plan_lat = ((0, 0, 512, F32, 1.0, False, False), (0, 512, 512, F32, 1.0, False, False),
                (0, 1024, 512, BF16, ATTN_SCALE, False, False), (0, 1536, 512, BF16, 1.0, False, False),
                (0, 2048, 512, BF16, 1.0, False, False), (0, 2560, 512, BF16, ATTN_SCALE, True, False),
                (1, 0, 512, BF16, 1.0, False, True),
                (1, 512, 256, BF16, 1.0, True, False), (1, 768, 256, BF16, 1.0, False, False))

    rope_tabs = _rope_tables(T_l)
    ck_nat = cache_nat_k.reshape(B_l, DEPTH, P, W512).astype(BF16)
    cv_nat = cache_nat_v.reshape(B_l, DEPTH, P, W512).astype(BF16)
    ckd_swa = _dup_heads(cache_swa_k).astype(BF16)
    cvd_swa = _dup_heads(cache_swa_v).astype(BF16)
    state8 = jnp.pad(state_rglru, ((0, 0), (0, 0), (0, SUBLANES - 2), (0, 0)))
    zero_state = jnp.zeros((B_c, SUBLANES, RNN_WIDTH), F32)

    def layer(x, l, ctx_pass):
        B, T = (B_c, T_c) if ctx_pass else (B_l, T_l)
        TM = min(T, 512)
        modb = jnp.broadcast_to(mod_all[l, 0:1], (B, SUBLANES, D_MODEL)) if ctx_pass else mod_all[l, 1:1 + B_l]
        if ctx_pass:
            xa, ga, nq, nk, nv, sq, sk, sv, xf, skd, svd = _pre_call(
                x, modb, w_in_b, GATE_OFF, w_tail_ctx, l, plan_ctx, B, T, TM, None)
        else:
            xa, ga, nq, nk, nv, sq, xf, skd, svd = _pre_call(
                x, modb, w_in_b, sk0, w_tail_lat, l, plan_lat, B, T, TM, rope_tabs)
        r3 = lambda a: a.reshape(B, T, a.shape[-1])
        h0 = zero_state if ctx_pass else state8[:, l]
        ya, hfin = _rglru_call(
            r3(xa), r3(ga), _pad_rows(rg_conv_w[l]), rg_conv_b[l][None], wa_bd[l], wx_bd[l],
            _pad_rows(rg_ba[l]), _pad_rows(rg_bx[l]), _pad_rows(rg_lambda[l]), h0)
        if ctx_pass:
            yb, yc = _attn_ctx_call(swa_sink[l], r3(nq), r3(nk), r3(nv), r3(sq), r3(skd), r3(svd))
        else:
            yb = _nat_lat_call(r3(nq), r3(nk), r3(nv), ck_nat, cv_nat, _nat_bias_table(nat_rpb[l]), l)
            yc = _swa_lat_call(swa_sink[l], r3(sq), r3(skd), r3(svd), ckd_swa, cvd_swa, l)
        yd = _fourier_call(xf, T)
        f2 = lambda a: a.reshape(B * T, a.shape[-1])
        x1, hp, rt = _merge_call(
            x, modb, (f2(ya), f2(yb), f2(yc), f2(yd)), w_gate[l], w_branch_b[l], w_out_b[l],
            ln_g[l, 0][None], ln_b[l, 0][None], wr_t,
            jnp.broadcast_to(router_bias[:, None], (N_EXPERTS, TM)), T, TM)
        pos, wts, seg = _route_call(rt)
        x2 = _moe_call(pos, wts, seg, hp, w1_b[l], w3_b[l], w2_b[l], x1, modb,
                       ln_g[l, 1][None], ln_b[l, 1][None], T)
        new = (nk, nv, sk, sv, hfin[:, :2]) if ctx_pass else None
        return x2, new

    y = x_prompt.reshape(B_c * T_c, D_MODEL)
    caches = []
    for l in range(DEPTH):
        y, new = layer(y, l, True)
        caches.append(new)
    y_prompt = y.reshape(B_c, T_c, D_MODEL)
    new_nat_k = jnp.stack([cc[0].reshape(B_c, T_c, NAT_HEADS, HEAD_DIM) for cc in caches], 1)
    new_nat_v = jnp.stack([cc[1].reshape(B_c, T_c, NAT_HEADS, HEAD_DIM) for cc in caches], 1)
    new_swa_k = jnp.stack([cc[2].reshape(B_c, T_c, SWA_KV_HEADS, HEAD_DIM) for cc in caches], 1)
    new_swa_v = jnp.stack([cc[3].reshape(B_c, T_c, SWA_KV_HEADS, HEAD_DIM) for cc in caches], 1)
    new_state = jnp.stack([cc[4] for cc in caches], 1)

    y = x_sample.reshape(B_l * T_l, D_MODEL)
    for l in range(DEPTH):
        y, _ = layer(y, l, False)
    y_sample = y.reshape(B_l, T_l, D_MODEL)
    return (y_prompt, y_sample, new_nat_k, new_nat_v, new_swa_k, new_swa_v, new_state)
```

```python
import functools
import math

import jax
import jax.numpy as jnp
import numpy as np
from jax import lax
from jax.experimental import pallas as pl
from jax.experimental.pallas import tpu as pltpu

F32 = jnp.float32
BF16 = jnp.bfloat16

D_MODEL = 1024
DEPTH = 4
GRID_W = 64
HEAD_DIM = 64
ATTN_SCALE = HEAD_DIM ** -0.5
NEG_INF = -1e30
LN_EPS = 1e-5
ALPHA = (2 * DEPTH) ** 0.25
ROPE_BASE = 10000.0
RNN_WIDTH = 512
RNN_BLOCKS = 8
RNN_BLOCK = RNN_WIDTH // RNN_BLOCKS
CONV_W = 4
RGLRU_C = 8.0
NAT_HEADS = 8
NAT_KR = 8
NAT_KC = 16
SWA_HEADS = 8
SWA_KV_HEADS = 2
SWA_WINDOW = 128
SWA_BLOCK = 128
SWA_SPAN = SWA_BLOCK + 2 * SWA_WINDOW
FNET_GROUPS = 4
FNET_WIDTH = 512
FNET_GC = FNET_WIDTH // FNET_GROUPS
N_BRANCH = 4
N_EXPERTS = 16
N_EXPERT_GROUPS = 4
EXPERTS_PER_GROUP = N_EXPERTS // N_EXPERT_GROUPS
D_EXPERT = 512
W512 = 512
GATE_OFF = 3840

LANES = 128
SUBLANES = 8
VMEM_LIMIT = 56 * 1024 * 1024


def _cparams(*sem):
    return pltpu.CompilerParams(dimension_semantics=sem, vmem_limit_bytes=VMEM_LIMIT)


def _const_spec(shape):
    nd = len(shape)
    return pl.BlockSpec(shape, lambda *_: (0,) * nd, pipeline_mode=pl.Buffered(1))


def _ln(x):
    mu = jnp.mean(x, -1, keepdims=True)
    xc = x - mu
    var = jnp.mean(xc * xc, -1, keepdims=True)
    return xc * lax.rsqrt(var + LN_EPS)


def _dot(a, b):
    return jnp.dot(a, b, preferred_element_type=F32)


def _dot_nt(a, b):
    return lax.dot_general(a, b, (((1,), (1,)), ((), ())), preferred_element_type=F32)


ADA_ROWS = 16
ADA_TN = 1024


def _ada_kernel(c_ref, w_ref, b_ref, o_ref):
    cv = c_ref[...]
    s = (cv * jax.nn.sigmoid(cv)).astype(BF16)
    o_ref[0] = _dot(s, w_ref[0].astype(BF16)) + b_ref[0]


def _ada_call(cv, w_ada, b_ada):
    n = w_ada.shape[-1]
    return pl.pallas_call(
        _ada_kernel,
        out_shape=jax.ShapeDtypeStruct((DEPTH, ADA_ROWS, n), F32),
        grid=(DEPTH, n // ADA_TN),
        in_specs=[
            pl.BlockSpec((ADA_ROWS, D_MODEL), lambda l, j: (0, 0)),
            pl.BlockSpec((1, D_MODEL, ADA_TN), lambda l, j: (l, 0, j)),
            pl.BlockSpec((1, 1, ADA_TN), lambda l, j: (l, 0, j)),
        ],
        out_specs=pl.BlockSpec((1, ADA_ROWS, ADA_TN), lambda l, j: (l, 0, j)),
        compiler_params=_cparams("parallel", "parallel"),
        name="ada",
    )(cv, w_ada, b_ada.reshape(DEPTH, 1, n))


def _rope(u, cos, sin):
    lane = lax.broadcasted_iota(jnp.int32, cos.shape, 1)
    first = (lane & 31) < 16
    outs = []
    for j in range(u.shape[1] // LANES):
        s = u[:, j * LANES:(j + 1) * LANES]
        partner = jnp.where(first, pltpu.roll(s, LANES - 16, 1), pltpu.roll(s, 16, 1))
        outs.append(s * cos + partner * sin)
    return outs[0] if len(outs) == 1 else jnp.concatenate(outs, -1)


def _pre_kernel(*refs, plan, rope, n1):
    x_ref, mod_ref, wm_ref, wt_ref = refs[:4]
    refs = refs[4:]
    if rope:
        cos, sin = refs[0][...], refs[1][...]
        refs = refs[2:]
    outs = refs[:len(plan)]
    m = mod_ref[0]
    h = (_ln(x_ref[...]) * (1.0 + m[1:2]) + m[0:1]).astype(BF16)
    for (src, off, width, _, scale, do_rope, fourier), o_ref in zip(plan, outs):
        w_ref = wt_ref if src else wm_ref
        u = _dot(h, w_ref[0, :, off:off + width])
        if do_rope:
            u = _rope(u, cos, sin)
        if scale != 1.0:
            u = u * scale
        if fourier and n1 > 1:
            stage = refs[len(plan)]
            for c in range(width // LANES):
                stage[c] = u[:, c * LANES:(c + 1) * LANES]
            rows = u.shape[0] // n1
            for t1 in range(n1):
                for c in range(width // LANES):
                    lo = t1 * width + c * LANES
                    o_ref[0, :, lo:lo + LANES] = stage[c, pl.ds(t1, rows, stride=n1), :].astype(o_ref.dtype)
        elif fourier:
            o_ref[0] = u.astype(o_ref.dtype)
        else:
            o_ref[...] = u.astype(o_ref.dtype)


def _pre_call(x, mod, w_main, n_main, w_tail, l, plan, B, T, TM, rope_tabs):
    N = x.shape[0]
    tpb = T // TM
    n1 = T // FN_N2
    rope = rope_tabs is not None
    n_tail = w_tail.shape[-1]
    in_specs = [
        pl.BlockSpec((TM, D_MODEL), lambda i: (i, 0)),
        pl.BlockSpec((1, SUBLANES, D_MODEL), lambda i: (i // tpb, 0, 0)),
        pl.BlockSpec((1, D_MODEL, n_main), lambda i: (l, 0, 0), pipeline_mode=pl.Buffered(1)),
        pl.BlockSpec((1, D_MODEL, n_tail), lambda i: (l, 0, 0), pipeline_mode=pl.Buffered(1)),
    ]
    args = [x, mod, w_main, w_tail]
    if rope:
        in_specs += [pl.BlockSpec((TM, LANES), lambda i: (i % tpb, 0))] * 2
        args += list(rope_tabs)
    out_shape, out_specs, scratch = [], [], []
    for p in plan:
        if p[6]:
            out_shape.append(jax.ShapeDtypeStruct((B, FN_N2, n1 * p[2]), p[3]))
            out_specs.append(pl.BlockSpec((1, TM // n1, n1 * p[2]), lambda i: (i // tpb, i % tpb, 0)))
            if n1 > 1:
                scratch.append(pltpu.VMEM((p[2] // LANES, TM, LANES), F32))
        else:
            out_shape.append(jax.ShapeDtypeStruct((N, p[2]), p[3]))
            out_specs.append(pl.BlockSpec((TM, p[2]), lambda i: (i, 0)))
    return pl.pallas_call(
        functools.partial(_pre_kernel, plan=plan, rope=rope, n1=n1),
        out_shape=out_shape,
        grid=(N // TM,),
        in_specs=in_specs,
        out_specs=out_specs,
        scratch_shapes=scratch,
        compiler_params=_cparams("parallel"),
        name="pre",
    )(*args)


RG_CW = 256
RG_TCH = 256


def _rglru_kernel(xa_ref, ga_ref, cw_ref, cb_ref, wa_ref, wx_ref, ba_ref, bx_ref, lam_ref, h0_ref,
                  y_ref, hfin_ref, xpad, a_f, u_f, a_b, u_b, *, T):
    cw_ = RG_CW
    zeros8 = jnp.zeros((SUBLANES, cw_), F32)
    xpad[0:SUBLANES, :] = zeros8
    xpad[T + SUBLANES:T + 2 * SUBLANES, :] = zeros8
    xpad[SUBLANES:T + SUBLANES, :] = xa_ref[0]
    cw = cw_ref[...]
    cb = cb_ref[...]
    lam = lam_ref[...]
    sp = jnp.maximum(-lam, 0.0) + jnp.log1p(jnp.exp(-jnp.abs(lam)))
    sp4 = (0.5 * RGLRU_C) * sp
    nsp4_log2e = sp4 * (-math.log2(math.e))
    ba = ba_ref[...]
    bx = bx_ref[...]
    h0 = h0_ref[0]

    def chunk(c, carry):
        base = pl.multiple_of(c * RG_TCH, RG_TCH)
        xw = xpad[pl.ds(base, RG_TCH + 2 * SUBLANES), :]
        xc = cb
        for i in range(CONV_W):
            xc = xc + cw[i:i + 1] * xw[SUBLANES - 1 + i:SUBLANES - 1 + i + RG_TCH]
        xcb = xc.astype(BF16)
        half_xc = 0.5 * xc
        for d, (a_s, u_s) in enumerate(((a_f, u_f), (a_b, u_b))):
            r2 = jnp.tanh(_dot(xcb, wa_ref[d]) + ba[d:d + 1]) + 1.0
            i2 = jnp.tanh(_dot(xcb, wx_ref[d]) + bx[d:d + 1]) + 1.0
            a = jnp.exp2(r2 * nsp4_log2e[d:d + 1])
            gain = jnp.sqrt(jnp.tanh(r2 * sp4[d:d + 1]) * (1.0 + a * a))
            a_s[pl.ds(base, RG_TCH), :] = a
            u_s[pl.ds(base, RG_TCH), :] = gain * (i2 * half_xc)
        return carry

    lax.fori_loop(0, T // RG_TCH, chunk, 0)

    row = lax.broadcasted_iota(jnp.int32, (SUBLANES, cw_), 0)

    def block_scan(a, u, reverse):
        for dd in (1, 2, 4):
            sh = SUBLANES - dd if reverse else dd
            a_n = pltpu.roll(a, sh, 0)
            u_n = pltpu.roll(u, sh, 0)
            ok = (row < SUBLANES - dd) if reverse else (row >= dd)
            u = jnp.where(ok, u + a * u_n, u)
            a = jnp.where(ok, a * a_n, a)
        return a, u

    nblk = T // SUBLANES

    def scan(i, carry):
        c_f, c_b = carry
        lo_f = pl.multiple_of(i * SUBLANES, SUBLANES)
        lo_b = pl.multiple_of((nblk - 1 - i) * SUBLANES, SUBLANES)
        a, u = block_scan(a_f[pl.ds(lo_f, SUBLANES), :], u_f[pl.ds(lo_f, SUBLANES), :], False)
        h_f = u + a * c_f
        u_f[pl.ds(lo_f, SUBLANES), :] = h_f
        a, u = block_scan(a_b[pl.ds(lo_b, SUBLANES), :], u_b[pl.ds(lo_b, SUBLANES), :], True)
        h_b = u + a * c_b
        u_b[pl.ds(lo_b, SUBLANES), :] = h_b
        return (jnp.broadcast_to(h_f[SUBLANES - 1:SUBLANES, :], (SUBLANES, cw_)),
                jnp.broadcast_to(h_b[0:1, :], (SUBLANES, cw_)))

    c_f, c_b = lax.fori_loop(
        0, nblk, scan,
        (jnp.broadcast_to(h0[0:1, :], (SUBLANES, cw_)), jnp.broadcast_to(h0[1:2, :], (SUBLANES, cw_))),
        unroll=2)
    hfin_ref[0] = jnp.where(row == 0, c_f, jnp.where(row == 1, c_b, 0.0))

    def emit(c, carry):
        base = pl.multiple_of(c * RG_TCH, RG_TCH)
        h = u_f[pl.ds(base, RG_TCH), :] + u_b[pl.ds(base, RG_TCH), :]
        y_ref[0, pl.ds(base, RG_TCH), :] = (h * jax.nn.gelu(ga_ref[0, pl.ds(base, RG_TCH), :])).astype(y_ref.dtype)
        return carry

    lax.fori_loop(0, T // RG_TCH, emit, 0)


def _rglru_call(xa, ga, cw, cb, wa, wx, ba, bx, lam, h0):
    B, T, _ = xa.shape
    nj = RNN_WIDTH // RG_CW
    seq = pl.BlockSpec((1, T, RG_CW), lambda b, j: (b, 0, j))
    vec8 = pl.BlockSpec((SUBLANES, RG_CW), lambda b, j: (0, j))
    wsp = pl.BlockSpec((2, RG_CW, RG_CW), lambda b, j: (0, j, j))
    st = pl.BlockSpec((1, SUBLANES, RG_CW), lambda b, j: (b, 0, j))
    return pl.pallas_call(
        functools.partial(_rglru_kernel, T=T),
        out_shape=[jax.ShapeDtypeStruct((B, T, RNN_WIDTH), BF16),
                   jax.ShapeDtypeStruct((B, SUBLANES, RNN_WIDTH), F32)],
        grid=(B, nj),
        in_specs=[seq, seq, vec8, pl.BlockSpec((1, RG_CW), lambda b, j: (0, j)),
                  wsp, wsp, vec8, vec8, vec8, st],
        out_specs=[seq, st],
        scratch_shapes=[pltpu.VMEM((T + 2 * SUBLANES, RG_CW), F32)]
        + [pltpu.VMEM((T, RG_CW), F32)] * 4,
        compiler_params=_cparams("parallel", "parallel"),
        name="rglru",
    )(xa, ga, cw, cb, wa, wx, ba, bx, lam, h0)


def _attend_slabs(jobs):
    M = jobs[0][0].shape[0]
    lo = lax.broadcasted_iota(jnp.int32, (M, LANES), 1) < HEAD_DIM
    scores = []
    for q2, srcs, _ in jobs:
        zero = jnp.zeros_like(q2)
        qs = jnp.concatenate([jnp.where(lo, q2, zero), jnp.where(lo, zero, q2)], 0)
        ss = []
        for k, _, bias in srcs:
            s = _dot_nt(qs, k)
            ss.append(s if bias is None else s + bias)
        scores.append(ss)
    maxima = []
    for (_, _, sink_col), ss in zip(jobs, scores):
        m = jnp.max(ss[0], -1, keepdims=True)
        for s in ss[1:]:
            m = jnp.maximum(m, jnp.max(s, -1, keepdims=True))
        maxima.append(m if sink_col is None else jnp.maximum(m, sink_col))
    outs = []
    for (_, srcs, sink_col), ss, m in zip(jobs, scores, maxima):
        den = None
        o = None
        for s, (_, v, _) in zip(ss, srcs):
            e = jnp.exp(s - m)
            d_ = jnp.sum(e, -1, keepdims=True)
            o_ = _dot(e.astype(BF16), v)
            den = d_ if den is None else den + d_
            o = o_ if o is None else o + o_
        if sink_col is not None:
            den = den + jnp.exp(sink_col - m)
        o = o / den
        outs.append(jnp.where(lo, o[:M], o[M:]))
    return outs


def _sink_col(sink_ref, j, M):
    r = lax.broadcasted_iota(jnp.int32, (2 * M, 1), 0)
    return jnp.where(r < M, sink_ref[2 * j], sink_ref[2 * j + 1])


def _slab(j):
    return slice(j * LANES, (j + 1) * LANES)


def _attn_ctx_kernel(sink_ref, nq_ref, nk_ref, nv_ref, sq_ref, skd_ref, svd_ref, yb_ref, yc_ref):
    M = nq_ref.shape[1]
    nslab = W512 // LANES
    jobs = []
    for j in range(nslab):
        k2 = nk_ref[0, :, _slab(j)].astype(BF16)
        v2 = nv_ref[0, :, _slab(j)].astype(BF16)
        jobs.append((nq_ref[0, :, _slab(j)], [(k2, v2, None)], None))
    for j in range(nslab):
        g = j // 2
        jobs.append((sq_ref[0, :, _slab(j)], [(skd_ref[0, :, _slab(g)], svd_ref[0, :, _slab(g)], None)],
                     _sink_col(sink_ref, j, M)))
    outs = _attend_slabs(jobs)
    for j in range(nslab):
        yb_ref[0, :, _slab(j)] = outs[j].astype(yb_ref.dtype)
        yc_ref[0, :, _slab(j)] = outs[nslab + j].astype(yc_ref.dtype)


def _attn_ctx_call(sink, nq, nk, nv, sq, skd, svd):
    B, T, _ = nq.shape
    s512 = pl.BlockSpec((1, T, W512), lambda b: (b, 0, 0))
    s256 = pl.BlockSpec((1, T, 2 * LANES), lambda b: (b, 0, 0))
    return pl.pallas_call(
        _attn_ctx_kernel,
        out_shape=[jax.ShapeDtypeStruct((B, T, W512), BF16)] * 2,
        grid=(B,),
        in_specs=[pl.BlockSpec(memory_space=pltpu.SMEM), s512, s512, s512, s512, s256, s256],
        out_specs=[s512, s512],
        compiler_params=_cparams("parallel"),
        name="attn_ctx",
    )(sink, nq, nk, nv, sq, skd, svd)


NAT_NLOC = NAT_KR * GRID_W


def _nat_lat_kernel(q_ref, k_ref, v_ref, ck_ref, cv_ref, bias_ref, y_ref, *, rows):
    r = pl.program_id(1)
    rstart = jnp.clip(r - NAT_KR // 2, 0, rows - NAT_KR)
    d = r - rstart
    kbase = pl.multiple_of(rstart * GRID_W, GRID_W)
    jobs = []
    for j in range(W512 // LANES):
        k2 = k_ref[0, pl.ds(kbase, NAT_NLOC), _slab(j)]
        v2 = v_ref[0, pl.ds(kbase, NAT_NLOC), _slab(j)]
        jobs.append((q_ref[0, :, _slab(j)],
                     [(k2, v2, bias_ref[j, d]), (ck_ref[0, 0, :, _slab(j)], cv_ref[0, 0, :, _slab(j)], None)], None))
    for j, o in enumerate(_attend_slabs(jobs)):
        y_ref[0, :, _slab(j)] = o.astype(y_ref.dtype)


def _nat_lat_call(q, k, v, ck, cv, bias, l):
    B, T, _ = q.shape
    rows = T // GRID_W
    P = ck.shape[2]
    qs = pl.BlockSpec((1, GRID_W, W512), lambda b, r: (b, r, 0))
    full = pl.BlockSpec((1, T, W512), lambda b, r: (b, 0, 0))
    cs = pl.BlockSpec((1, 1, P, W512), lambda b, r: (b, l, 0, 0))
    return pl.pallas_call(
        functools.partial(_nat_lat_kernel, rows=rows),
        out_shape=jax.ShapeDtypeStruct((B, T, W512), BF16),
        grid=(B, rows),
        in_specs=[qs, full, full, cs, cs, _const_spec(bias.shape)],
        out_specs=qs,
        compiler_params=_cparams("parallel", "arbitrary"),
        name="nat_lat",
    )(q, k, v, ck, cv, bias)


def _swa_lat_kernel(sink_ref, q_ref, kd_ref, vd_ref, ckd_ref, cvd_ref, y_ref, *, T):
    blk = pl.program_id(1)
    M = SWA_BLOCK
    start = blk * SWA_BLOCK
    ks = pl.multiple_of(jnp.clip(start - SWA_WINDOW, 0, T - SWA_SPAN), SWA_BLOCK)
    kpos = ks + lax.broadcasted_iota(jnp.int32, (2 * M, SWA_SPAN), 1)
    qpos = start + (lax.broadcasted_iota(jnp.int32, (2 * M, SWA_SPAN), 0) & (M - 1))
    dist = kpos - qpos
    bias = jnp.where(dist > SWA_WINDOW, NEG_INF, jnp.where(dist < -SWA_WINDOW, NEG_INF, 0.0))
    jobs = []
    for j in range(W512 // LANES):
        g = j // 2
        jobs.append((q_ref[0, :, _slab(j)],
                     [(kd_ref[0, pl.ds(ks, SWA_SPAN), _slab(g)], vd_ref[0, pl.ds(ks, SWA_SPAN), _slab(g)], bias),
                      (ckd_ref[0, 0, :, _slab(g)], cvd_ref[0, 0, :, _slab(g)], None)],
                     _sink_col(sink_ref, j, M)))
    for j, o in enumerate(_attend_slabs(jobs)):
        y_ref[0, :, _slab(j)] = o.astype(y_ref.dtype)


def _swa_lat_call(sink, q, kd, vd, ckd, cvd, l):
    B, T, _ = q.shape
    P = ckd.shape[2]
    qs = pl.BlockSpec((1, SWA_BLOCK, W512), lambda b, i: (b, i, 0))
    full = pl.BlockSpec((1, T, 2 * LANES), lambda b, i: (b, 0, 0))
    cs = pl.BlockSpec((1, 1, P, 2 * LANES), lambda b, i: (b, l, 0, 0))
    return pl.pallas_call(
        functools.partial(_swa_lat_kernel, T=T),
        out_shape=jax.ShapeDtypeStruct((B, T, W512), BF16),
        grid=(B, T // SWA_BLOCK),
        in_specs=[pl.BlockSpec(memory_space=pltpu.SMEM), qs, full, full, cs, cs],
        out_specs=qs,
        compiler_params=_cparams("parallel", "arbitrary"),
        name="swa_lat",
    )(sink, q, kd, vd, ckd, cvd)


FN_N2 = 256
FN_GP = 2


def _fourier_kernel(x_ref, cc_ref, m_ref, y_ref, *scratch, n1, coef):
    gw = FN_GP * FNET_GC
    cc = cc_ref[...]
    for gp in range(FNET_GROUPS // FN_GP):
        parts = []
        for t1 in range(n1):
            ws = []
            for g in range(FN_GP):
                lo = t1 * FNET_WIDTH + (gp * FN_GP + g) * FNET_GC
                ws.append(_dot(x_ref[0, :, lo:lo + FNET_GC], cc))
            wr = jnp.concatenate([w[:, :FNET_GC] for w in ws], -1)
            wi = jnp.concatenate([w[:, FNET_GC:] for w in ws], -1)
            v = jnp.concatenate([wr, wi], 0).astype(BF16)
            b = _dot(m_ref[t1], v)
            if n1 == 1:
                y_ref[0, :, gp * gw:(gp + 1) * gw] = b.astype(y_ref.dtype)
            else:
                scratch[0][t1] = b
        if n1 > 1:
            bs = scratch[0]
            for k1 in range(n1):
                acc = None
                for t1 in range(n1):
                    c, s = coef[k1][t1]
                    for cf, lo in ((c, 0), (s, FN_N2)):
                        if cf == 0.0:
                            continue
                        blk = bs[t1, lo:lo + FN_N2, :]
                        term = blk if cf == 1.0 else (-blk if cf == -1.0 else cf * blk)
                        acc = term if acc is None else acc + term
                y_ref[0, k1 * FN_N2:(k1 + 1) * FN_N2, gp * gw:(gp + 1) * gw] = acc.astype(y_ref.dtype)


def _dft_consts(T):
    n1 = T // FN_N2
    j = np.arange(FNET_GC)
    ang = 2 * np.pi * np.outer(j, j) / FNET_GC
    sc = 1.0 / math.sqrt(FNET_GC)
    cc = np.concatenate([np.cos(ang) * sc, -np.sin(ang) * sc], 1)
    k2 = np.arange(FN_N2)[:, None]
    t2 = np.arange(FN_N2)[None, :]
    st = 1.0 / math.sqrt(T)
    mats = []
    for t1 in range(n1):
        th = 2 * np.pi * ((k2 * (n1 * t2 + t1)) % T) / T
        c, s = np.cos(th) * st, np.sin(th) * st
        top = np.concatenate([c, s], 1)
        mats.append(top if n1 == 1 else np.concatenate([top, np.concatenate([-s, c], 1)], 0))
    coef = []
    for k1 in range(n1):
        rowc = []
        for t1 in range(n1):
            q = (k1 * t1) % n1
            c, s = math.cos(2 * math.pi * q / n1), math.sin(2 * math.pi * q / n1)
            c = 0.0 if abs(c) < 1e-12 else (1.0 if abs(c - 1) < 1e-12 else (-1.0 if abs(c + 1) < 1e-12 else c))
            s = 0.0 if abs(s) < 1e-12 else (1.0 if abs(s - 1) < 1e-12 else (-1.0 if abs(s + 1) < 1e-12 else s))
            rowc.append((c, s))
        coef.append(tuple(rowc))
    return n1, jnp.asarray(cc, BF16), jnp.asarray(np.stack(mats), BF16), tuple(coef)


def _fourier_call(xv, T):
    B = xv.shape[0]
    n1, cc, mats, coef = _dft_consts(T)
    gw = FN_GP * FNET_GC
    scratch = [pltpu.VMEM((n1, 2 * FN_N2, gw), F32)] if n1 > 1 else []
    return pl.pallas_call(
        functools.partial(_fourier_kernel, n1=n1, coef=coef),
        out_shape=jax.ShapeDtypeStruct((B, T, FNET_WIDTH), BF16),
        grid=(B,),
        in_specs=[pl.BlockSpec((1, FN_N2, n1 * FNET_WIDTH), lambda b: (b, 0, 0)),
                  _const_spec(cc.shape), _const_spec(mats.shape)],
        out_specs=pl.BlockSpec((1, T, FNET_WIDTH), lambda b: (b, 0, 0)),
        scratch_shapes=scratch,
        compiler_params=_cparams("parallel"),
        name="fourier",
    )(xv, cc, mats)


def _route(scores, biased):
    one, zero = jnp.float32(1.0), jnp.float32(0.0)
    in2, gscore = [], []
    for g in range(N_EXPERT_GROUPS):
        vs = [biased[g * EXPERTS_PER_GROUP + j:g * EXPERTS_PER_GROUP + j + 1] for j in range(EXPERTS_PER_GROUP)]
        gs = None
        for j in range(EXPERTS_PER_GROUP):
            rank = None
            for i in range(EXPERTS_PER_GROUP):
                if i == j:
                    continue
                beats = (vs[i] >= vs[j]) if i < j else (vs[i] > vs[j])
                t = jnp.where(beats, one, zero)
                rank = t if rank is None else rank + t
            keep = jnp.where(rank < 2.0, one, zero)
            in2.append(keep)
            t = keep * vs[j]
            gs = t if gs is None else gs + t
        gscore.append(gs)
    rows, sel = [], []
    for g in range(N_EXPERT_GROUPS):
        lost = None
        for i in range(N_EXPERT_GROUPS):
            if i == g:
                continue
            beats = (gscore[i] >= gscore[g]) if i < g else (gscore[i] > gscore[g])
            t = jnp.where(beats, one, zero)
            lost = t if lost is None else lost + t
        gsel = jnp.where(lost < 1.0, one, zero)
        for j in range(EXPERTS_PER_GROUP):
            e = g * EXPERTS_PER_GROUP + j
            sel.append(gsel * in2[e])
            rows.append(sel[-1] * scores[e:e + 1])
    tot = rows[0]
    for rr in rows[1:]:
        tot = tot + rr
    return jnp.concatenate(rows, 0) / tot, jnp.concatenate(sel, 0)


HP_ROWS = D_MODEL // 2 // LANES
Y_ROWS = D_MODEL // LANES


def _pack_halves(v):
    w = v.shape[1] // 2
    lo = lax.bitcast_convert_type(v[:, :w].astype(F32), jnp.uint32)
    hi = lax.bitcast_convert_type(v[:, w:].astype(F32), jnp.uint32)
    return (lo >> 16) | hi


def _unpack_halves(p):
    lo = lax.bitcast_convert_type(p << 16, F32).astype(BF16)
    hi = lax.bitcast_convert_type(p & jnp.uint32(0xFFFF0000), F32).astype(BF16)
    return lo, hi


def _merge_kernel(x_ref, mod_ref, ya_ref, yb_ref, yc_ref, yd_ref, wg_ref, wb_ref, wo_ref,
                  lng_ref, lnb_ref, wr_ref, rb_ref, x1_ref, hp_ref, rt_ref):
    x = x_ref[...]
    m = mod_ref[0]
    h = (_ln(x) * (1.0 + m[1:2]) + m[0:1]).astype(BF16)
    merged = None
    for b, y_ref in enumerate((ya_ref, yb_ref, yc_ref, yd_ref)):
        gate = jax.nn.sigmoid(_dot(h, wg_ref[:, b * D_MODEL:(b + 1) * D_MODEL]))
        term = gate * _dot(y_ref[...], wb_ref[b])
        merged = term if merged is None else merged + term
    out = _dot(merged.astype(BF16), wo_ref[...])
    x1 = _ln(ALPHA * x + m[2:3] * out) * lng_ref[...] + lnb_ref[...]
    x1_ref[...] = x1
    h2 = (_ln(x1) * (1.0 + m[4:5]) + m[3:4]).astype(BF16)
    packed = _pack_halves(h2)
    tm = x.shape[0]
    for c in range(HP_ROWS):
        hp_ref[pl.ds(c, tm, stride=HP_ROWS), :] = packed[:, c * LANES:(c + 1) * LANES]
    logits = _dot_nt(wr_ref[...], h2)
    e = jnp.exp(logits - jnp.max(logits, 0, keepdims=True))
    scores = e / jnp.sum(e, 0, keepdims=True)
    comb, sel = _route(scores, scores + rb_ref[...])
    rt_ref[...] = jnp.concatenate([comb, sel], 0)


def _merge_call(x, mod, ys, wg, wb, wo, lng, lnb, wr_t, rb, T, TM):
    N = x.shape[0]
    tpb = T // TM
    tok = lambda w: pl.BlockSpec((TM, w), lambda i: (i, 0))
    return pl.pallas_call(
        _merge_kernel,
        out_shape=[jax.ShapeDtypeStruct((N, D_MODEL), F32), jax.ShapeDtypeStruct((N * HP_ROWS, LANES), jnp.uint32),
                   jax.ShapeDtypeStruct((2 * N_EXPERTS, N), F32)],
        grid=(N // TM,),
        in_specs=[tok(D_MODEL), pl.BlockSpec((1, SUBLANES, D_MODEL), lambda i: (i // tpb, 0, 0)),
                  tok(W512), tok(W512), tok(W512), tok(W512),
                  _const_spec(wg.shape), _const_spec(wb.shape), _const_spec(wo.shape),
                  _const_spec(lng.shape), _const_spec(lnb.shape), _const_spec(wr_t.shape), _const_spec(rb.shape)],
        out_specs=[tok(D_MODEL), pl.BlockSpec((TM * HP_ROWS, LANES), lambda i: (i, 0)),
                   pl.BlockSpec((2 * N_EXPERTS, TM), lambda i: (0, i))],
        compiler_params=_cparams("parallel"),
        name="merge",
    )(x, mod, *ys, wg, wb, wo, lng, lnb, wr_t, rb)


MOE_TM = 2048
MOE_RB = 256
MOE_KC = 4
MOE_CH = MOE_TM // MOE_KC
MOE_SLOTS = 2 * MOE_TM + N_EXPERTS * SUBLANES + MOE_RB
MOE_PR = 2 * MOE_TM // LANES
MOE_UNROLL = 8


def _route_kernel(rt_ref, u_ref, pos_ref, wts_ref, seg_ref):
    comb = rt_ref[0:N_EXPERTS, :]
    sel = rt_ref[N_EXPERTS:2 * N_EXPERTS, :]
    nb = MOE_TM // LANES
    stacked = jnp.concatenate([sel[:, b * LANES:(b + 1) * LANES] for b in range(nb)], 0)
    within = _dot(stacked.astype(BF16), u_ref[...])
    tot = jnp.sum(stacked, -1, keepdims=True)
    base = jnp.zeros((N_EXPERTS, 1), F32)
    bases = []
    for b in range(nb):
        bases.append(base)
        base = base + tot[b * N_EXPERTS:(b + 1) * N_EXPERTS]
    cnt = base
    padded = jnp.floor((cnt + (SUBLANES - 1.0)) * (1.0 / SUBLANES)) * SUBLANES
    rowi = lax.broadcasted_iota(jnp.int32, (N_EXPERTS, 1), 0)
    off = jnp.zeros((N_EXPERTS, 1), F32)
    for e in range(N_EXPERTS - 1):
        off = off + jnp.where(rowi > e, padded[e:e + 1, :], 0.0)
    pos_rows, wts_rows = [], []
    for b in range(nb):
        slot = within[b * N_EXPERTS:(b + 1) * N_EXPERTS] + (bases[b] + off)
        seen = jnp.zeros((1, LANES), F32)
        acc = [jnp.zeros((1, LANES), F32) for _ in range(4)]
        for e in range(N_EXPERTS):
            s_e = sel[e:e + 1, b * LANES:(b + 1) * LANES]
            c_e = comb[e:e + 1, b * LANES:(b + 1) * LANES]
            first = jnp.where(seen == 0.0, s_e, 0.0)
            second = jnp.where(seen == 1.0, s_e, 0.0)
            acc[0] = acc[0] + first * slot[e:e + 1]
            acc[1] = acc[1] + second * slot[e:e + 1]
            acc[2] = acc[2] + first * c_e
            acc[3] = acc[3] + second * c_e
            seen = seen + s_e
        pos_rows += acc[0:2]
        wts_rows += acc[2:4]
    pos_ref[0] = (jnp.concatenate(pos_rows, 0) * float(HP_ROWS)).astype(jnp.int32)
    wts_ref[0] = jnp.concatenate(wts_rows, 0)
    lane = lax.broadcasted_iota(jnp.int32, (N_EXPERTS, LANES), 1)
    diag = lane == lax.broadcasted_iota(jnp.int32, (N_EXPERTS, LANES), 0)
    off_row = jnp.sum(jnp.where(diag, off, 0.0), 0, keepdims=True)
    cnt_row = jnp.sum(jnp.where(diag, cnt, 0.0), 0, keepdims=True)
    seg_ref[0] = jnp.concatenate([off_row, cnt_row, jnp.zeros((SUBLANES - 2, LANES), F32)], 0).astype(jnp.int32)


def _route_call(rt):
    N = rt.shape[1]
    nt = N // MOE_TM
    u = jnp.asarray(np.triu(np.ones((LANES, LANES), np.float32), 1), BF16)
    return pl.pallas_call(
        _route_kernel,
        out_shape=[jax.ShapeDtypeStruct((nt, MOE_PR, LANES), jnp.int32),
                   jax.ShapeDtypeStruct((nt, MOE_PR, LANES), F32),
                   jax.ShapeDtypeStruct((nt, SUBLANES, LANES), jnp.int32)],
        grid=(nt,),
        in_specs=[pl.BlockSpec((2 * N_EXPERTS, MOE_TM), lambda i: (0, i)), _const_spec(u.shape)],
        out_specs=[pl.BlockSpec((1, MOE_PR, LANES), lambda i: (i, 0, 0)),
                   pl.BlockSpec((1, MOE_PR, LANES), lambda i: (i, 0, 0)),
                   pl.BlockSpec((1, SUBLANES, LANES), lambda i: (i, 0, 0))],
        compiler_params=_cparams("parallel"),
        name="route",
    )(rt, u)


def _table_index(it):
    per_block = LANES // MOE_UNROLL
    return (it // per_block) * (2 * LANES) + (it % per_block) * MOE_UNROLL


def _moe_kernel(pos_ref, wts_ref, seg_ref, hp_ref, w1_ref, w3_ref, w2_ref, x1_ref, mod_ref, lng_ref, lnb_ref,
                o_ref, xs, ys, oc):
    s = pl.program_id(1)
    half = D_MODEL // 2

    @pl.when(s == 0)
    def _dispatch():
        xs[...] = jnp.zeros_like(xs)

        def body(it, carry):
            t0 = pl.multiple_of(it * (MOE_UNROLL * HP_ROWS), MOE_UNROLL * HP_ROWS)
            i0 = _table_index(it)
            for j in range(MOE_UNROLL):
                row = hp_ref[pl.ds(t0 + HP_ROWS * j, HP_ROWS), :]
                xs[pl.ds(pl.multiple_of(pos_ref[i0 + j], HP_ROWS), HP_ROWS), :] = row
                xs[pl.ds(pl.multiple_of(pos_ref[i0 + LANES + j], HP_ROWS), HP_ROWS), :] = row
            return carry

        lax.fori_loop(0, MOE_TM // MOE_UNROLL, body, 0)

    @pl.when(s < N_EXPERTS)
    def _expert():
        off = seg_ref[0, 0, s]
        nblk = (seg_ref[0, 1, s] + (MOE_RB - 1)) // MOE_RB

        def blk(i, carry):
            r0 = pl.multiple_of(off + i * MOE_RB, SUBLANES)
            parts = [_unpack_halves(xs[pl.ds(pl.multiple_of(r0 * HP_ROWS, SUBLANES) + c, MOE_RB, stride=HP_ROWS), :])
                     for c in range(HP_ROWS)]
            lo = jnp.concatenate([p[0] for p in parts], -1)
            hi = jnp.concatenate([p[1] for p in parts], -1)
            a = _dot(lo, w1_ref[0, :half, :]) + _dot(hi, w1_ref[0, half:, :])
            g = _dot(lo, w3_ref[0, :half, :]) + _dot(hi, w3_ref[0, half:, :])
            act = (a * jax.nn.sigmoid(a) * g).astype(BF16)
            y = _dot(act, w2_ref[0])
            y0 = pl.multiple_of(r0 * Y_ROWS, SUBLANES)
            for c in range(Y_ROWS):
                ys[pl.ds(y0 + c, MOE_RB, stride=Y_ROWS), :] = y[:, c * LANES:(c + 1) * LANES]
            return carry

        lax.fori_loop(0, nblk, blk, 0)

    @pl.when(s >= N_EXPERTS)
    def _combine():
        it0 = (s - N_EXPERTS) * (MOE_CH // MOE_UNROLL)

        def body(it, carry):
            t0 = pl.multiple_of(it * (MOE_UNROLL * Y_ROWS), MOE_UNROLL * Y_ROWS)
            i0 = _table_index(it0 + it)
            for j in range(MOE_UNROLL):
                p0 = pl.multiple_of(pos_ref[i0 + j] * (Y_ROWS // HP_ROWS), Y_ROWS)
                p1 = pl.multiple_of(pos_ref[i0 + LANES + j] * (Y_ROWS // HP_ROWS), Y_ROWS)
                oc[pl.ds(t0 + Y_ROWS * j, Y_ROWS), :] = (wts_ref[i0 + j] * ys[pl.ds(p0, Y_ROWS), :]
                                                         + wts_ref[i0 + LANES + j] * ys[pl.ds(p1, Y_ROWS), :])
            return carry

        lax.fori_loop(0, MOE_CH // MOE_UNROLL, body, 0)
        moe = jnp.concatenate([oc[pl.ds(c, MOE_CH, stride=Y_ROWS), :] for c in range(Y_ROWS)], -1)
        m = mod_ref[0]
        o_ref[...] = _ln(ALPHA * x1_ref[...] + m[5:6] * moe) * lng_ref[...] + lnb_ref[...]


def _moe_call(pos, wts, seg, hp, w1, w3, w2, x1, mod, lng, lnb, T):
    N = x1.shape[0]
    nt = N // MOE_TM
    flat = pl.BlockSpec((2 * MOE_TM,), lambda i, s: (i,), memory_space=pltpu.SMEM)
    wspec = lambda shp: pl.BlockSpec((1,) + shp, lambda i, s: (jnp.minimum(s, N_EXPERTS - 1), 0, 0))
    chunk = pl.BlockSpec((MOE_CH, D_MODEL),
                         lambda i, s: (i * MOE_KC + jnp.clip(s - N_EXPERTS, 0, MOE_KC - 1), 0))
    return pl.pallas_call(
        _moe_kernel,
        out_shape=jax.ShapeDtypeStruct((N, D_MODEL), F32),
        grid=(nt, N_EXPERTS + MOE_KC),
        in_specs=[flat, flat,
                  pl.BlockSpec((1, SUBLANES, LANES), lambda i, s: (i, 0, 0), memory_space=pltpu.SMEM),
                  pl.BlockSpec((MOE_TM * HP_ROWS, LANES), lambda i, s: (i, 0)),
                  wspec((D_MODEL, D_EXPERT)), wspec((D_MODEL, D_EXPERT)), wspec((D_EXPERT, D_MODEL)),
                  chunk, pl.BlockSpec((1, SUBLANES, D_MODEL), lambda i, s: ((i * MOE_TM) // T, 0, 0)),
                  _const_spec(lng.shape), _const_spec(lnb.shape)],
        out_specs=chunk,
        scratch_shapes=[pltpu.VMEM((MOE_SLOTS * HP_ROWS, LANES), jnp.uint32),
                        pltpu.VMEM((MOE_SLOTS * Y_ROWS, LANES), F32),
                        pltpu.VMEM((MOE_CH * Y_ROWS, LANES), F32)],
        compiler_params=_cparams("parallel", "arbitrary"),
        name="moe",
    )(pos.reshape(-1), wts.reshape(-1), seg, hp, w1, w3, w2, x1, mod, lng, lnb)


def _rope_tables(T):
    t = jnp.arange(T)
    nf = HEAD_DIM // 4
    inv = ROPE_BASE ** (-jnp.arange(nf, dtype=F32) / nf)
    ar = (t // GRID_W).astype(F32)[:, None] * inv
    ac = (t % GRID_W).astype(F32)[:, None] * inv
    cos = jnp.concatenate([jnp.cos(ar), jnp.cos(ar), jnp.cos(ac), jnp.cos(ac)], -1)
    sin = jnp.concatenate([-jnp.sin(ar), jnp.sin(ar), -jnp.sin(ac), jnp.sin(ac)], -1)
    return jnp.tile(cos, (1, LANES // HEAD_DIM)), jnp.tile(sin, (1, LANES // HEAD_DIM))


def _nat_bias_table(rpb):
    q = np.arange(GRID_W)
    kc = np.arange(GRID_W)
    cstart = np.clip(q - NAT_KC // 2, 0, GRID_W - NAT_KC)
    ok = (kc[None, :] >= cstart[:, None]) & (kc[None, :] < cstart[:, None] + NAT_KC)
    cidx = np.clip(kc[None, :] - q[:, None] + NAT_KC - 1, 0, 2 * NAT_KC - 2)
    onehot = (np.arange(2 * NAT_KC - 1)[:, None, None] == cidx[None]).astype(np.float32)
    t = jnp.einsum('hrc,cqk->hqrk', rpb.astype(F32), jnp.asarray(onehot), precision=lax.Precision.HIGHEST)
    t = jnp.where(jnp.asarray(ok)[None, :, None, :], t, NEG_INF)
    per_d = [t[:, :, NAT_KR - 1 - d:2 * NAT_KR - 1 - d, :].reshape(NAT_HEADS // 2, 2 * GRID_W, NAT_KR * GRID_W)
             for d in range(NAT_KR)]
    return jnp.stack(per_d, 1)


def _dup_heads(a):
    a = jnp.broadcast_to(a[..., :, None, :], a.shape[:-1] + (2, a.shape[-1]))
    return a.reshape(a.shape[:-3] + (-1,))


def _block_diag(w):
    eye = jnp.eye(RNN_BLOCKS, dtype=w.dtype)
    return jnp.einsum('dnio,nm->dnimo', w, eye).reshape(2, RNN_WIDTH, RNN_WIDTH)


def _pad_rows(a, rows=SUBLANES):
    return jnp.pad(a, ((0, rows - a.shape[0]),) + ((0, 0),) * (a.ndim - 1))


def kernel(x_prompt, x_sample, c, cache_nat_k, cache_nat_v, cache_swa_k, cache_swa_v, state_rglru, c_ctx,
           w_ada, b_ada, w_in, rg_conv_w, rg_conv_b, rg_wa, rg_ba, rg_wx, rg_bx, rg_lambda, nat_rpb,
           swa_sink, w_branch, w_out, ln_g, ln_b, w_router, router_bias, w1, w3, w2):
    B_c, T_c, _ = x_prompt.shape
    B_l, T_l, _ = x_sample.shape
    P = cache_nat_k.shape[2]

    cv = jnp.concatenate([c_ctx[None], c, jnp.zeros((ADA_ROWS - 1 - B_l, D_MODEL), F32)], 0)
    mod_all = _ada_call(cv, w_ada, b_ada).reshape(DEPTH, ADA_ROWS, 6, D_MODEL)
    mod_all = jnp.pad(mod_all, ((0, 0), (0, 0), (0, SUBLANES - 6), (0, 0)))

    sk0 = 3072
    sv0 = sk0 + SWA_KV_HEADS * HEAD_DIM
    xf0 = sv0 + SWA_KV_HEADS * HEAD_DIM
    w_in_b = w_in[:, :, :GATE_OFF].astype(BF16)
    head = lambda c0, g: w_in_b[:, :, c0 + g * HEAD_DIM:c0 + (g + 1) * HEAD_DIM]
    w_tail_ctx = jnp.concatenate([head(c0, g) for c0 in (sk0, sv0) for g in (0, 0, 1, 1)], -1)
    w_tail_lat = jnp.concatenate([w_in_b[:, :, xf0:GATE_OFF], w_tail_ctx], -1)
    w_gate = w_in[:, :, GATE_OFF:].astype(BF16)
    w_branch_b = w_branch.astype(BF16)
    w_out_b = w_out.astype(BF16)
    w1_b, w3_b, w2_b = w1.astype(BF16), w3.astype(BF16), w2.astype(BF16)
    wr_t = w_router.T.astype(BF16)
    wa_bd = jnp.stack([_block_diag(0.5 * rg_wa[l]) for l in range(DEPTH)]).astype(BF16)
    wx_bd = jnp.stack([_block_diag(0.5 * rg_wx[l]) for l in range(DEPTH)]).astype(BF16)

    plan_ctx = ((0, 0, 512, F32, 1.0, False, False), (0, 512, 512, F32, 1.0, False, False),
                (0, 1024, 512, BF16, ATTN_SCALE, False, False), (0, 1536, 512, F32, 1.0, False, False),
                (0, 2048, 512, F32, 1.0, False, False), (0, 2560, 512, BF16, ATTN_SCALE, False, False),
                (0, sk0, 128, F32, 1.0, False, False), (0, sv0, 128, F32, 1.0, False, False),
                (0, xf0, 512, BF16, 1.0, False, True),
                (1, 0, 256, BF16, 1.0, False, False), (1, 256, 256, BF16, 1.0, False, False))
    plan_lat = ((0, 0, 512, F32, 1.0, False, False), (0, 512, 512, F32, 1.0, False, False),
                (0, 1024, 512, BF16, ATTN_SCALE, False, False), (0, 1536, 512, BF16, 1.0, False, False),
                (0, 2048, 512, BF16, 1.0, False, False), (0, 2560, 512, BF16, ATTN_SCALE, True, False),
                (1, 0, 512, BF16, 1.0, False, True),
                (1, 512, 256, BF16, 1.0, True, False), (1, 768, 256, BF16, 1.0, False, False))

    rope_tabs = _rope_tables(T_l)
    ck_nat = cache_nat_k.reshape(B_l, DEPTH, P, W512).astype(BF16)
    cv_nat = cache_nat_v.reshape(B_l, DEPTH, P, W512).astype(BF16)
    ckd_swa = _dup_heads(cache_swa_k).astype(BF16)
    cvd_swa = _dup_heads(cache_swa_v).astype(BF16)
    state8 = jnp.pad(state_rglru, ((0, 0), (0, 0), (0, SUBLANES - 2), (0, 0)))
    zero_state = jnp.zeros((B_c, SUBLANES, RNN_WIDTH), F32)

    def layer(x, l, ctx_pass):
        B, T = (B_c, T_c) if ctx_pass else (B_l, T_l)
        TM = min(T, 512)
        modb = jnp.broadcast_to(mod_all[l, 0:1], (B, SUBLANES, D_MODEL)) if ctx_pass else mod_all[l, 1:1 + B_l]
        if ctx_pass:
            xa, ga, nq, nk, nv, sq, sk, sv, xf, skd, svd = _pre_call(
                x, modb, w_in_b, GATE_OFF, w_tail_ctx, l, plan_ctx, B, T, TM, None)
        else:
            xa, ga, nq, nk, nv, sq, xf, skd, svd = _pre_call(
                x, modb, w_in_b, sk0, w_tail_lat, l, plan_lat, B, T, TM, rope_tabs)
        r3 = lambda a: a.reshape(B, T, a.shape[-1])
        h0 = zero_state if ctx_pass else state8[:, l]
        ya, hfin = _rglru_call(
            r3(xa), r3(ga), _pad_rows(rg_conv_w[l]), rg_conv_b[l][None], wa_bd[l], wx_bd[l],
            _pad_rows(0.5 * rg_ba[l]), _pad_rows(0.5 * rg_bx[l]), _pad_rows(rg_lambda[l]), h0)
        if ctx_pass:
            yb, yc = _attn_ctx_call(swa_sink[l], r3(nq), r3(nk), r3(nv), r3(sq), r3(skd), r3(svd))
        else:
            yb = _nat_lat_call(r3(nq), r3(nk), r3(nv), ck_nat, cv_nat, _nat_bias_table(nat_rpb[l]), l)
            yc = _swa_lat_call(swa_sink[l], r3(sq), r3(skd), r3(svd), ckd_swa, cvd_swa, l)
        yd = _fourier_call(xf, T)
        f2 = lambda a: a.reshape(B * T, a.shape[-1])
        x1, hp, rt = _merge_call(
            x, modb, (f2(ya), f2(yb), f2(yc), f2(yd)), w_gate[l], w_branch_b[l], w_out_b[l],
            ln_g[l, 0][None], ln_b[l, 0][None], wr_t,
            jnp.broadcast_to(router_bias[:, None], (N_EXPERTS, TM)), T, TM)
        pos, wts, seg = _route_call(rt)
        x2 = _moe_call(pos, wts, seg, hp, w1_b[l], w3_b[l], w2_b[l], x1, modb,
                       ln_g[l, 1][None], ln_b[l, 1][None], T)
        new = (nk, nv, sk, sv, hfin[:, :2]) if ctx_pass else None
        return x2, new

    y = x_prompt.reshape(B_c * T_c, D_MODEL)
    caches = []
    for l in range(DEPTH):
        y, new = layer(y, l, True)
        caches.append(new)
    y_prompt = y.reshape(B_c, T_c, D_MODEL)
    new_nat_k = jnp.stack([cc[0].reshape(B_c, T_c, NAT_HEADS, HEAD_DIM) for cc in caches], 1)
    new_nat_v = jnp.stack([cc[1].reshape(B_c, T_c, NAT_HEADS, HEAD_DIM) for cc in caches], 1)
    new_swa_k = jnp.stack([cc[2].reshape(B_c, T_c, SWA_KV_HEADS, HEAD_DIM) for cc in caches], 1)
    new_swa_v = jnp.stack([cc[3].reshape(B_c, T_c, SWA_KV_HEADS, HEAD_DIM) for cc in caches], 1)
    new_state = jnp.stack([cc[4] for cc in caches], 1)

    y = x_sample.reshape(B_l * T_l, D_MODEL)
    for l in range(DEPTH):
        y, _ = layer(y, l, False)
    y_sample = y.reshape(B_l, T_l, D_MODEL)
    return (y_prompt, y_sample, new_nat_k, new_nat_v, new_swa_k, new_swa_v, new_state)
```

```python
import functools
import math

import jax
import jax.numpy as jnp
import numpy as np
from jax import lax
from jax.experimental import pallas as pl
from jax.experimental.pallas import tpu as pltpu

F32 = jnp.float32
BF16 = jnp.bfloat16

D_MODEL = 1024
DEPTH = 4
GRID_W = 64
HEAD_DIM = 64
ATTN_SCALE = HEAD_DIM ** -0.5
NEG_INF = -1e30
LN_EPS = 1e-5
ALPHA = (2 * DEPTH) ** 0.25
ROPE_BASE = 10000.0
RNN_WIDTH = 512
RNN_BLOCKS = 8
RNN_BLOCK = RNN_WIDTH // RNN_BLOCKS
CONV_W = 4
RGLRU_C = 8.0
NAT_HEADS = 8
NAT_KR = 8
NAT_KC = 16
SWA_HEADS = 8
SWA_KV_HEADS = 2
SWA_WINDOW = 128
SWA_BLOCK = 128
SWA_SPAN = SWA_BLOCK + 2 * SWA_WINDOW
FNET_GROUPS = 4
FNET_WIDTH = 512
FNET_GC = FNET_WIDTH // FNET_GROUPS
N_BRANCH = 4
N_EXPERTS = 16
N_EXPERT_GROUPS = 4
EXPERTS_PER_GROUP = N_EXPERTS // N_EXPERT_GROUPS
D_EXPERT = 512
W512 = 512
GATE_OFF = 3840

LANES = 128
SUBLANES = 8
VMEM_LIMIT = 56 * 1024 * 1024
TOKEN_TM = 1024
SUB_TM = 512


def _cparams(*sem):
    return pltpu.CompilerParams(dimension_semantics=sem, vmem_limit_bytes=VMEM_LIMIT)


def _const_spec(shape):
    nd = len(shape)
    return pl.BlockSpec(shape, lambda *_: (0,) * nd, pipeline_mode=pl.Buffered(1))


def _ln(x):
    mu = jnp.mean(x, -1, keepdims=True)
    xc = x - mu
    var = jnp.mean(xc * xc, -1, keepdims=True)
    return xc * lax.rsqrt(var + LN_EPS)


def _dot(a, b):
    return jnp.dot(a, b, preferred_element_type=F32)


def _dot_nt(a, b):
    return lax.dot_general(a, b, (((1,), (1,)), ((), ())), preferred_element_type=F32)


ADA_ROWS = 16
ADA_TN = 1024


def _ada_kernel(c_ref, w_ref, b_ref, o_ref):
    cv = c_ref[...]
    s = (cv * jax.nn.sigmoid(cv)).astype(BF16)
    o_ref[0] = _dot(s, w_ref[0].astype(BF16)) + b_ref[0]


def _ada_call(cv, w_ada, b_ada):
    n = w_ada.shape[-1]
    return pl.pallas_call(
        _ada_kernel,
        out_shape=jax.ShapeDtypeStruct((DEPTH, ADA_ROWS, n), F32),
        grid=(DEPTH, n // ADA_TN),
        in_specs=[
            pl.BlockSpec((ADA_ROWS, D_MODEL), lambda l, j: (0, 0)),
            pl.BlockSpec((1, D_MODEL, ADA_TN), lambda l, j: (l, 0, j)),
            pl.BlockSpec((1, 1, ADA_TN), lambda l, j: (l, 0, j)),
        ],
        out_specs=pl.BlockSpec((1, ADA_ROWS, ADA_TN), lambda l, j: (l, 0, j)),
        compiler_params=_cparams("parallel", "parallel"),
        name="ada",
    )(cv, w_ada, b_ada.reshape(DEPTH, 1, n))


def _rope(u, cos, sin):
    lane = lax.broadcasted_iota(jnp.int32, cos.shape, 1)
    first = (lane & 31) < 16
    outs = []
    for j in range(u.shape[1] // LANES):
        s = u[:, j * LANES:(j + 1) * LANES]
        partner = jnp.where(first, pltpu.roll(s, LANES - 16, 1), pltpu.roll(s, 16, 1))
        outs.append(s * cos + partner * sin)
    return outs[0] if len(outs) == 1 else jnp.concatenate(outs, -1)


def _pre_kernel(*refs, plan, rope, n1, nsub, n_alias):
    x_ref, mod_ref, wm_ref, wt_ref = refs[:4]
    refs = refs[4:]
    if rope:
        cos_ref, sin_ref = refs[:2]
        refs = refs[2:]
    refs = refs[n_alias:]
    outs = refs[:len(plan)]
    m = mod_ref[0]
    sub = x_ref.shape[0] // nsub
    hs = [(_ln(x_ref[k * sub:(k + 1) * sub, :]) * (1.0 + m[1:2]) + m[0:1]).astype(BF16) for k in range(nsub)]
    for (src, off, width, _, scale, do_rope, fourier, slot), o_ref in zip(plan, outs):
        w_ref = wt_ref if src else wm_ref
        for k in range(nsub):
            rows = slice(k * sub, (k + 1) * sub)
            u = _dot(hs[k], w_ref[0, :, off:off + width])
            if do_rope:
                u = _rope(u, cos_ref[rows, :], sin_ref[rows, :])
            if scale != 1.0:
                u = u * scale
            if fourier and n1 > 1:
                stage = refs[len(plan)]
                for c in range(width // LANES):
                    stage[k, c] = u[:, c * LANES:(c + 1) * LANES]
                r = sub // n1
                for t1 in range(n1):
                    for c in range(width // LANES):
                        lo = t1 * width + c * LANES
                        o_ref[0, k * r:(k + 1) * r, lo:lo + LANES] = (
                            stage[k, c, pl.ds(t1, r, stride=n1), :].astype(o_ref.dtype))
            elif fourier:
                o_ref[0, rows, :] = u.astype(o_ref.dtype)
            elif slot is not None:
                o_ref[0, 0, rows, :] = u.astype(o_ref.dtype)
            else:
                o_ref[rows, :] = u.astype(o_ref.dtype)


def _pre_call(x, mod, w_main, n_main, w_tail, l, plan, B, T, TM, rope_tabs, caches=None):
    N = x.shape[0]
    tpb = T // TM
    nsub = max(1, TM // SUB_TM)
    n1 = T // FN_N2
    rope = rope_tabs is not None
    n_tail = w_tail.shape[-1]
    in_specs = [
        pl.BlockSpec((TM, D_MODEL), lambda i: (i, 0)),
        pl.BlockSpec((1, SUBLANES, D_MODEL), lambda i: (i // tpb, 0, 0)),
        pl.BlockSpec((1, D_MODEL, n_main), lambda i: (l, 0, 0), pipeline_mode=pl.Buffered(1)),
        pl.BlockSpec((1, D_MODEL, n_tail), lambda i: (l, 0, 0), pipeline_mode=pl.Buffered(1)),
    ]
    args = [x, mod, w_main, w_tail]
    if rope:
        in_specs += [pl.BlockSpec((TM, LANES), lambda i: (i % tpb, 0))] * 2
        args += list(rope_tabs)
    out_shape, out_specs, scratch, aliases = [], [], [], {}
    for k, p in enumerate(plan):
        if p[6]:
            out_shape.append(jax.ShapeDtypeStruct((B, FN_N2, n1 * p[2]), p[3]))
            out_specs.append(pl.BlockSpec((1, TM // n1, n1 * p[2]), lambda i: (i // tpb, i % tpb, 0)))
            if n1 > 1:
                scratch.append(pltpu.VMEM((nsub, p[2] // LANES, TM // nsub, LANES), F32))
        elif p[7] is not None:
            out_shape.append(jax.ShapeDtypeStruct((B, DEPTH, T, p[2]), p[3]))
            out_specs.append(pl.BlockSpec((1, 1, TM, p[2]), lambda i: (i // tpb, l, i % tpb, 0)))
            if caches is not None:
                aliases[len(args)] = k
                in_specs.append(pl.BlockSpec(memory_space=pl.ANY))
                args.append(caches[p[7]])
        else:
            out_shape.append(jax.ShapeDtypeStruct((N, p[2]), p[3]))
            out_specs.append(pl.BlockSpec((TM, p[2]), lambda i: (i, 0)))
    return pl.pallas_call(
        functools.partial(_pre_kernel, plan=plan, rope=rope, n1=n1, nsub=nsub, n_alias=len(aliases)),
        out_shape=out_shape,
        grid=(N // TM,),
        in_specs=in_specs,
        out_specs=out_specs,
        scratch_shapes=scratch,
        input_output_aliases=aliases,
        compiler_params=_cparams("parallel"),
        name="pre",
    )(*args)


RG_CW = 256
RG_TCH = 256


def _rglru_kernel(xa_ref, ga_ref, cw_ref, cb_ref, wa_ref, wx_ref, ba_ref, bx_ref, lam_ref, h0_ref,
                  y_ref, hfin_ref, xpad, a_f, u_f, a_b, u_b, *, T):
    cw_ = RG_CW
    zeros8 = jnp.zeros((SUBLANES, cw_), F32)
    xpad[0:SUBLANES, :] = zeros8
    xpad[T + SUBLANES:T + 2 * SUBLANES, :] = zeros8
    xpad[SUBLANES:T + SUBLANES, :] = xa_ref[0]
    cw = cw_ref[...]
    cb = cb_ref[...]
    lam = lam_ref[...]
    sp = jnp.maximum(-lam, 0.0) + jnp.log1p(jnp.exp(-jnp.abs(lam)))
    sp4 = (0.5 * RGLRU_C) * sp
    nsp4_log2e = sp4 * (-math.log2(math.e))
    ba = ba_ref[...]
    bx = bx_ref[...]
    h0 = h0_ref[0]

    def chunk(c, carry):
        base = pl.multiple_of(c * RG_TCH, RG_TCH)
        xw = xpad[pl.ds(base, RG_TCH + 2 * SUBLANES), :]
        xc = cb
        for i in range(CONV_W):
            xc = xc + cw[i:i + 1] * xw[SUBLANES - 1 + i:SUBLANES - 1 + i + RG_TCH]
        xcb = xc.astype(BF16)
        half_xc = 0.5 * xc
        for d, (a_s, u_s) in enumerate(((a_f, u_f), (a_b, u_b))):
            r2 = jnp.tanh(_dot(xcb, wa_ref[d]) + ba[d:d + 1]) + 1.0
            i2 = jnp.tanh(_dot(xcb, wx_ref[d]) + bx[d:d + 1]) + 1.0
            a = jnp.exp2(r2 * nsp4_log2e[d:d + 1])
            gain = jnp.sqrt(jnp.tanh(r2 * sp4[d:d + 1]) * (1.0 + a * a))
            a_s[pl.ds(base, RG_TCH), :] = a
            u_s[pl.ds(base, RG_TCH), :] = gain * (i2 * half_xc)
        return carry

    lax.fori_loop(0, T // RG_TCH, chunk, 0)

    row = lax.broadcasted_iota(jnp.int32, (SUBLANES, cw_), 0)

    def block_scan(a, u, reverse):
        for dd in (1, 2, 4):
            sh = SUBLANES - dd if reverse else dd
            a_n = pltpu.roll(a, sh, 0)
            u_n = pltpu.roll(u, sh, 0)
            ok = (row < SUBLANES - dd) if reverse else (row >= dd)
            u = jnp.where(ok, u + a * u_n, u)
            a = jnp.where(ok, a * a_n, a)
        return a, u

    nblk = T // SUBLANES

    def scan(i, carry):
        c_f, c_b = carry
        lo_f = pl.multiple_of(i * SUBLANES, SUBLANES)
        lo_b = pl.multiple_of((nblk - 1 - i) * SUBLANES, SUBLANES)
        a, u = block_scan(a_f[pl.ds(lo_f, SUBLANES), :], u_f[pl.ds(lo_f, SUBLANES), :], False)
        h_f = u + a * c_f
        u_f[pl.ds(lo_f, SUBLANES), :] = h_f
        a, u = block_scan(a_b[pl.ds(lo_b, SUBLANES), :], u_b[pl.ds(lo_b, SUBLANES), :], True)
        h_b = u + a * c_b
        u_b[pl.ds(lo_b, SUBLANES), :] = h_b
        return (jnp.broadcast_to(h_f[SUBLANES - 1:SUBLANES, :], (SUBLANES, cw_)),
                jnp.broadcast_to(h_b[0:1, :], (SUBLANES, cw_)))

    c_f, c_b = lax.fori_loop(
        0, nblk, scan,
        (jnp.broadcast_to(h0[0:1, :], (SUBLANES, cw_)), jnp.broadcast_to(h0[1:2, :], (SUBLANES, cw_))),
        unroll=2)
    hfin_ref[0] = jnp.where(row == 0, c_f, jnp.where(row == 1, c_b, 0.0))

    def emit(c, carry):
        base = pl.multiple_of(c * RG_TCH, RG_TCH)
        h = u_f[pl.ds(base, RG_TCH), :] + u_b[pl.ds(base, RG_TCH), :]
        y_ref[0, pl.ds(base, RG_TCH), :] = (h * jax.nn.gelu(ga_ref[0, pl.ds(base, RG_TCH), :])).astype(y_ref.dtype)
        return carry

    lax.fori_loop(0, T // RG_TCH, emit, 0)


def _rglru_call(xa, ga, cw, cb, wa, wx, ba, bx, lam, h0):
    B, T, _ = xa.shape
    nj = RNN_WIDTH // RG_CW
    seq = pl.BlockSpec((1, T, RG_CW), lambda b, j: (b, 0, j))
    vec8 = pl.BlockSpec((SUBLANES, RG_CW), lambda b, j: (0, j))
    wsp = pl.BlockSpec((2, RG_CW, RG_CW), lambda b, j: (0, j, j))
    st = pl.BlockSpec((1, SUBLANES, RG_CW), lambda b, j: (b, 0, j))
    return pl.pallas_call(
        functools.partial(_rglru_kernel, T=T),
        out_shape=[jax.ShapeDtypeStruct((B, T, RNN_WIDTH), BF16),
                   jax.ShapeDtypeStruct((B, SUBLANES, RNN_WIDTH), F32)],
        grid=(B, nj),
        in_specs=[seq, seq, vec8, pl.BlockSpec((1, RG_CW), lambda b, j: (0, j)),
                  wsp, wsp, vec8, vec8, vec8, st],
        out_specs=[seq, st],
        scratch_shapes=[pltpu.VMEM((T + 2 * SUBLANES, RG_CW), F32)]
        + [pltpu.VMEM((T, RG_CW), F32)] * 4,
        compiler_params=_cparams("parallel", "parallel"),
        name="rglru",
    )(xa, ga, cw, cb, wa, wx, ba, bx, lam, h0)


def _attend_slabs(jobs):
    M = jobs[0][0].shape[0]
    lo = lax.broadcasted_iota(jnp.int32, (M, LANES), 1) < HEAD_DIM
    scores = []
    for q2, srcs, _ in jobs:
        zero = jnp.zeros_like(q2)
        qs = jnp.concatenate([jnp.where(lo, q2, zero), jnp.where(lo, zero, q2)], 0)
        ss = []
        for k, _, bias in srcs:
            s = _dot_nt(qs, k)
            ss.append(s if bias is None else s + bias)
        scores.append(ss)
    maxima = []
    for (_, _, sink_col), ss in zip(jobs, scores):
        m = jnp.max(ss[0], -1, keepdims=True)
        for s in ss[1:]:
            m = jnp.maximum(m, jnp.max(s, -1, keepdims=True))
        maxima.append(m if sink_col is None else jnp.maximum(m, sink_col))
    outs = []
    for (_, srcs, sink_col), ss, m in zip(jobs, scores, maxima):
        den = None
        o = None
        for s, (_, v, _) in zip(ss, srcs):
            e = jnp.exp(s - m)
            d_ = jnp.sum(e, -1, keepdims=True)
            o_ = _dot(e.astype(BF16), v)
            den = d_ if den is None else den + d_
            o = o_ if o is None else o + o_
        if sink_col is not None:
            den = den + jnp.exp(sink_col - m)
        o = o / den
        outs.append(jnp.where(lo, o[:M], o[M:]))
    return outs


def _sink_col(sink_ref, j, M):
    r = lax.broadcasted_iota(jnp.int32, (2 * M, 1), 0)
    return jnp.where(r < M, sink_ref[2 * j], sink_ref[2 * j + 1])


def _slab(j):
    return slice(j * LANES, (j + 1) * LANES)


def _attn_ctx_kernel(sink_ref, nq_ref, nk_ref, nv_ref, sq_ref, skd_ref, svd_ref, yb_ref, yc_ref):
    M = nq_ref.shape[1]
    nslab = W512 // LANES
    jobs = []
    for j in range(nslab):
        k2 = nk_ref[0, 0, :, _slab(j)].astype(BF16)
        v2 = nv_ref[0, 0, :, _slab(j)].astype(BF16)
        jobs.append((nq_ref[0, :, _slab(j)], [(k2, v2, None)], None))
    for j in range(nslab):
        g = j // 2
        jobs.append((sq_ref[0, :, _slab(j)], [(skd_ref[0, :, _slab(g)], svd_ref[0, :, _slab(g)], None)],
                     _sink_col(sink_ref, j, M)))
    outs = _attend_slabs(jobs)
    for j in range(nslab):
        yb_ref[0, :, _slab(j)] = outs[j].astype(yb_ref.dtype)
        yc_ref[0, :, _slab(j)] = outs[nslab + j].astype(yc_ref.dtype)


def _attn_ctx_call(sink, nq, nk, nv, sq, skd, svd, l):
    B, T, _ = nq.shape
    s512 = pl.BlockSpec((1, T, W512), lambda b: (b, 0, 0))
    s256 = pl.BlockSpec((1, T, 2 * LANES), lambda b: (b, 0, 0))
    cache = pl.BlockSpec((1, 1, T, W512), lambda b: (b, l, 0, 0))
    return pl.pallas_call(
        _attn_ctx_kernel,
        out_shape=[jax.ShapeDtypeStruct((B, T, W512), BF16)] * 2,
        grid=(B,),
        in_specs=[pl.BlockSpec(memory_space=pltpu.SMEM), s512, cache, cache, s512, s256, s256],
        out_specs=[s512, s512],
        compiler_params=_cparams("parallel"),
        name="attn_ctx",
    )(sink, nq, nk, nv, sq, skd, svd)


NAT_NLOC = NAT_KR * GRID_W


def _nat_lat_kernel(q_ref, k_ref, v_ref, ck_ref, cv_ref, bias_ref, y_ref, *, rows):
    r = pl.program_id(1)
    rstart = jnp.clip(r - NAT_KR // 2, 0, rows - NAT_KR)
    d = r - rstart
    kbase = pl.multiple_of(rstart * GRID_W, GRID_W)
    jobs = []
    for j in range(W512 // LANES):
        k2 = k_ref[0, pl.ds(kbase, NAT_NLOC), _slab(j)]
        v2 = v_ref[0, pl.ds(kbase, NAT_NLOC), _slab(j)]
        jobs.append((q_ref[0, :, _slab(j)],
                     [(k2, v2, bias_ref[j, d]), (ck_ref[0, 0, :, _slab(j)], cv_ref[0, 0, :, _slab(j)], None)], None))
    for j, o in enumerate(_attend_slabs(jobs)):
        y_ref[0, :, _slab(j)] = o.astype(y_ref.dtype)


def _nat_lat_call(q, k, v, ck, cv, bias, l):
    B, T, _ = q.shape
    rows = T // GRID_W
    P = ck.shape[2]
    qs = pl.BlockSpec((1, GRID_W, W512), lambda b, r: (b, r, 0))
    full = pl.BlockSpec((1, T, W512), lambda b, r: (b, 0, 0))
    cs = pl.BlockSpec((1, 1, P, W512), lambda b, r: (b, l, 0, 0))
    return pl.pallas_call(
        functools.partial(_nat_lat_kernel, rows=rows),
        out_shape=jax.ShapeDtypeStruct((B, T, W512), BF16),
        grid=(B, rows),
        in_specs=[qs, full, full, cs, cs, _const_spec(bias.shape)],
        out_specs=qs,
        compiler_params=_cparams("parallel", "arbitrary"),
        name="nat_lat",
    )(q, k, v, ck, cv, bias)


def _swa_lat_kernel(sink_ref, q_ref, kd_ref, vd_ref, ckd_ref, cvd_ref, y_ref, *, T):
    blk = pl.program_id(1)
    M = SWA_BLOCK
    start = blk * SWA_BLOCK
    ks = pl.multiple_of(jnp.clip(start - SWA_WINDOW, 0, T - SWA_SPAN), SWA_BLOCK)
    kpos = ks + lax.broadcasted_iota(jnp.int32, (2 * M, SWA_SPAN), 1)
    qpos = start + (lax.broadcasted_iota(jnp.int32, (2 * M, SWA_SPAN), 0) & (M - 1))
    dist = kpos - qpos
    bias = jnp.where(dist > SWA_WINDOW, NEG_INF, jnp.where(dist < -SWA_WINDOW, NEG_INF, 0.0))
    jobs = []
    for j in range(W512 // LANES):
        g = j // 2
        jobs.append((q_ref[0, :, _slab(j)],
                     [(kd_ref[0, pl.ds(ks, SWA_SPAN), _slab(g)], vd_ref[0, pl.ds(ks, SWA_SPAN), _slab(g)], bias),
                      (ckd_ref[0, 0, :, _slab(g)], cvd_ref[0, 0, :, _slab(g)], None)],
                     _sink_col(sink_ref, j, M)))
    for j, o in enumerate(_attend_slabs(jobs)):
        y_ref[0, :, _slab(j)] = o.astype(y_ref.dtype)


def _swa_lat_call(sink, q, kd, vd, ckd, cvd, l):
    B, T, _ = q.shape
    P = ckd.shape[2]
    qs = pl.BlockSpec((1, SWA_BLOCK, W512), lambda b, i: (b, i, 0))
    full = pl.BlockSpec((1, T, 2 * LANES), lambda b, i: (b, 0, 0))
    cs = pl.BlockSpec((1, 1, P, 2 * LANES), lambda b, i: (b, l, 0, 0))
    return pl.pallas_call(
        functools.partial(_swa_lat_kernel, T=T),
        out_shape=jax.ShapeDtypeStruct((B, T, W512), BF16),
        grid=(B, T // SWA_BLOCK),
        in_specs=[pl.BlockSpec(memory_space=pltpu.SMEM), qs, full, full, cs, cs],
        out_specs=qs,
        compiler_params=_cparams("parallel", "arbitrary"),
        name="swa_lat",
    )(sink, q, kd, vd, ckd, cvd)


FN_N2 = 256
FN_GP = 2


def _fourier_kernel(x_ref, cc_ref, m_ref, y_ref, *scratch, n1, coef):
    gw = FN_GP * FNET_GC
    cc = cc_ref[...]
    for gp in range(FNET_GROUPS // FN_GP):
        parts = []
        for t1 in range(n1):
            ws = []
            for g in range(FN_GP):
                lo = t1 * FNET_WIDTH + (gp * FN_GP + g) * FNET_GC
                ws.append(_dot(x_ref[0, :, lo:lo + FNET_GC], cc))
            wr = jnp.concatenate([w[:, :FNET_GC] for w in ws], -1)
            wi = jnp.concatenate([w[:, FNET_GC:] for w in ws], -1)
            v = jnp.concatenate([wr, wi], 0).astype(BF16)
            b = _dot(m_ref[t1], v)
            if n1 == 1:
                y_ref[0, :, gp * gw:(gp + 1) * gw] = b.astype(y_ref.dtype)
            else:
                scratch[0][t1] = b
        if n1 > 1:
            bs = scratch[0]
            for k1 in range(n1):
                acc = None
                for t1 in range(n1):
                    c, s = coef[k1][t1]
                    for cf, lo in ((c, 0), (s, FN_N2)):
                        if cf == 0.0:
                            continue
                        blk = bs[t1, lo:lo + FN_N2, :]
                        term = blk if cf == 1.0 else (-blk if cf == -1.0 else cf * blk)
                        acc = term if acc is None else acc + term
                y_ref[0, k1 * FN_N2:(k1 + 1) * FN_N2, gp * gw:(gp + 1) * gw] = acc.astype(y_ref.dtype)


def _dft_consts(T):
    n1 = T // FN_N2
    j = np.arange(FNET_GC)
    ang = 2 * np.pi * np.outer(j, j) / FNET_GC
    sc = 1.0 / math.sqrt(FNET_GC)
    cc = np.concatenate([np.cos(ang) * sc, -np.sin(ang) * sc], 1)
    k2 = np.arange(FN_N2)[:, None]
    t2 = np.arange(FN_N2)[None, :]
    st = 1.0 / math.sqrt(T)
    mats = []
    for t1 in range(n1):
        th = 2 * np.pi * ((k2 * (n1 * t2 + t1)) % T) / T
        c, s = np.cos(th) * st, np.sin(th) * st
        top = np.concatenate([c, s], 1)
        mats.append(top if n1 == 1 else np.concatenate([top, np.concatenate([-s, c], 1)], 0))
    coef = []
    for k1 in range(n1):
        rowc = []
        for t1 in range(n1):
            q = (k1 * t1) % n1
            c, s = math.cos(2 * math.pi * q / n1), math.sin(2 * math.pi * q / n1)
            c = 0.0 if abs(c) < 1e-12 else (1.0 if abs(c - 1) < 1e-12 else (-1.0 if abs(c + 1) < 1e-12 else c))
            s = 0.0 if abs(s) < 1e-12 else (1.0 if abs(s - 1) < 1e-12 else (-1.0 if abs(s + 1) < 1e-12 else s))
            rowc.append((c, s))
        coef.append(tuple(rowc))
    return n1, jnp.asarray(cc, BF16), jnp.asarray(np.stack(mats), BF16), tuple(coef)


def _fourier_call(xv, T):
    B = xv.shape[0]
    n1, cc, mats, coef = _dft_consts(T)
    gw = FN_GP * FNET_GC
    scratch = [pltpu.VMEM((n1, 2 * FN_N2, gw), F32)] if n1 > 1 else []
    return pl.pallas_call(
        functools.partial(_fourier_kernel, n1=n1, coef=coef),
        out_shape=jax.ShapeDtypeStruct((B, T, FNET_WIDTH), BF16),
        grid=(B,),
        in_specs=[pl.BlockSpec((1, FN_N2, n1 * FNET_WIDTH), lambda b: (b, 0, 0)),
                  _const_spec(cc.shape), _const_spec(mats.shape)],
        out_specs=pl.BlockSpec((1, T, FNET_WIDTH), lambda b: (b, 0, 0)),
        scratch_shapes=scratch,
        compiler_params=_cparams("parallel"),
        name="fourier",
    )(xv, cc, mats)


def _route(scores, biased):
    one, zero = jnp.float32(1.0), jnp.float32(0.0)
    in2, gscore = [], []
    for g in range(N_EXPERT_GROUPS):
        vs = [biased[g * EXPERTS_PER_GROUP + j:g * EXPERTS_PER_GROUP + j + 1] for j in range(EXPERTS_PER_GROUP)]
        gs = None
        for j in range(EXPERTS_PER_GROUP):
            rank = None
            for i in range(EXPERTS_PER_GROUP):
                if i == j:
                    continue
                beats = (vs[i] >= vs[j]) if i < j else (vs[i] > vs[j])
                t = jnp.where(beats, one, zero)
                rank = t if rank is None else rank + t
            keep = jnp.where(rank < 2.0, one, zero)
            in2.append(keep)
            t = keep * vs[j]
            gs = t if gs is None else gs + t
        gscore.append(gs)
    rows, sel = [], []
    for g in range(N_EXPERT_GROUPS):
        lost = None
        for i in range(N_EXPERT_GROUPS):
            if i == g:
                continue
            beats = (gscore[i] >= gscore[g]) if i < g else (gscore[i] > gscore[g])
            t = jnp.where(beats, one, zero)
            lost = t if lost is None else lost + t
        gsel = jnp.where(lost < 1.0, one, zero)
        for j in range(EXPERTS_PER_GROUP):
            e = g * EXPERTS_PER_GROUP + j
            sel.append(gsel * in2[e])
            rows.append(sel[-1] * scores[e:e + 1])
    tot = rows[0]
    for rr in rows[1:]:
        tot = tot + rr
    return jnp.concatenate(rows, 0) / tot, jnp.concatenate(sel, 0)


HP_ROWS = D_MODEL // 2 // LANES
Y_ROWS = D_MODEL // LANES


def _pack_halves(v):
    w = v.shape[1] // 2
    lo = lax.bitcast_convert_type(v[:, :w].astype(F32), jnp.uint32)
    hi = lax.bitcast_convert_type(v[:, w:].astype(F32), jnp.uint32)
    return (lo >> 16) | hi


def _unpack_halves(p):
    lo = lax.bitcast_convert_type(p << 16, F32).astype(BF16)
    hi = lax.bitcast_convert_type(p & jnp.uint32(0xFFFF0000), F32).astype(BF16)
    return lo, hi


def _merge_kernel(x_ref, mod_ref, ya_ref, yb_ref, yc_ref, yd_ref, wg_ref, wb_ref, wo_ref,
                  lng_ref, lnb_ref, wr_ref, rb_ref, x1_ref, hp_ref, rt_ref, *, nsub):
    m = mod_ref[0]
    sub = x_ref.shape[0] // nsub
    for k in range(nsub):
        rows = slice(k * sub, (k + 1) * sub)
        x = x_ref[rows, :]
        h = (_ln(x) * (1.0 + m[1:2]) + m[0:1]).astype(BF16)
        merged = None
        for b, y_ref in enumerate((ya_ref, yb_ref, yc_ref, yd_ref)):
            gate = jax.nn.sigmoid(_dot(h, wg_ref[:, b * D_MODEL:(b + 1) * D_MODEL]))
            term = gate * _dot(y_ref[rows, :], wb_ref[b])
            merged = term if merged is None else merged + term
        out = _dot(merged.astype(BF16), wo_ref[...])
        x1 = _ln(ALPHA * x + m[2:3] * out) * lng_ref[...] + lnb_ref[...]
        x1_ref[rows, :] = x1
        h2 = (_ln(x1) * (1.0 + m[4:5]) + m[3:4]).astype(BF16)
        packed = _pack_halves(h2)
        for c in range(HP_ROWS):
            hp_ref[pl.ds(k * sub * HP_ROWS + c, sub, stride=HP_ROWS), :] = packed[:, c * LANES:(c + 1) * LANES]
        logits = _dot_nt(wr_ref[...], h2)
        e = jnp.exp(logits - jnp.max(logits, 0, keepdims=True))
        scores = e / jnp.sum(e, 0, keepdims=True)
        comb, sel = _route(scores, scores + rb_ref[...])
        rt_ref[:, rows] = jnp.concatenate([comb, sel], 0)


def _merge_call(x, mod, ys, wg, wb, wo, lng, lnb, wr_t, rb, T, TM):
    N = x.shape[0]
    tpb = T // TM
    tok = lambda w: pl.BlockSpec((TM, w), lambda i: (i, 0))
    return pl.pallas_call(
        functools.partial(_merge_kernel, nsub=max(1, TM // SUB_TM)),
        out_shape=[jax.ShapeDtypeStruct((N, D_MODEL), F32), jax.ShapeDtypeStruct((N * HP_ROWS, LANES), jnp.uint32),
                   jax.ShapeDtypeStruct((2 * N_EXPERTS, N), F32)],
        grid=(N // TM,),
        in_specs=[tok(D_MODEL), pl.BlockSpec((1, SUBLANES, D_MODEL), lambda i: (i // tpb, 0, 0)),
                  tok(W512), tok(W512), tok(W512), tok(W512),
                  _const_spec(wg.shape), _const_spec(wb.shape), _const_spec(wo.shape),
                  _const_spec(lng.shape), _const_spec(lnb.shape), _const_spec(wr_t.shape), _const_spec(rb.shape)],
        out_specs=[tok(D_MODEL), pl.BlockSpec((TM * HP_ROWS, LANES), lambda i: (i, 0)),
                   pl.BlockSpec((2 * N_EXPERTS, TM), lambda i: (0, i))],
        compiler_params=_cparams("parallel"),
        name="merge",
    )(x, mod, *ys, wg, wb, wo, lng, lnb, wr_t, rb)


MOE_TM = 2048
MOE_RB = 256
MOE_KC = 4
MOE_CH = MOE_TM // MOE_KC
MOE_SLOTS = 2 * MOE_TM + N_EXPERTS * SUBLANES + MOE_RB
MOE_PR = 2 * MOE_TM // LANES
MOE_UNROLL = 8


def _route_kernel(rt_ref, u_ref, pos_ref, wts_ref, seg_ref):
    comb = rt_ref[0:N_EXPERTS, :]
    sel = rt_ref[N_EXPERTS:2 * N_EXPERTS, :]
    nb = MOE_TM // LANES
    stacked = jnp.concatenate([sel[:, b * LANES:(b + 1) * LANES] for b in range(nb)], 0)
    within = _dot(stacked.astype(BF16), u_ref[...])
    tot = jnp.sum(stacked, -1, keepdims=True)
    base = jnp.zeros((N_EXPERTS, 1), F32)
    bases = []
    for b in range(nb):
        bases.append(base)
        base = base + tot[b * N_EXPERTS:(b + 1) * N_EXPERTS]
    cnt = base
    padded = jnp.floor((cnt + (SUBLANES - 1.0)) * (1.0 / SUBLANES)) * SUBLANES
    rowi = lax.broadcasted_iota(jnp.int32, (N_EXPERTS, 1), 0)
    off = jnp.zeros((N_EXPERTS, 1), F32)
    for e in range(N_EXPERTS - 1):
        off = off + jnp.where(rowi > e, padded[e:e + 1, :], 0.0)
    pos_rows, wts_rows = [], []
    for b in range(nb):
        slot = within[b * N_EXPERTS:(b + 1) * N_EXPERTS] + (bases[b] + off)
        seen = jnp.zeros((1, LANES), F32)
        acc = [jnp.zeros((1, LANES), F32) for _ in range(4)]
        for e in range(N_EXPERTS):
            s_e = sel[e:e + 1, b * LANES:(b + 1) * LANES]
            c_e = comb[e:e + 1, b * LANES:(b + 1) * LANES]
            first = jnp.where(seen == 0.0, s_e, 0.0)
            second = jnp.where(seen == 1.0, s_e, 0.0)
            acc[0] = acc[0] + first * slot[e:e + 1]
            acc[1] = acc[1] + second * slot[e:e + 1]
            acc[2] = acc[2] + first * c_e
            acc[3] = acc[3] + second * c_e
            seen = seen + s_e
        pos_rows += acc[0:2]
        wts_rows += acc[2:4]
    pos_ref[...] = (jnp.concatenate(pos_rows, 0) * float(HP_ROWS)).astype(jnp.int32).reshape(2 * MOE_TM)
    wts_ref[...] = jnp.concatenate(wts_rows, 0).reshape(2 * MOE_TM)
    lane = lax.broadcasted_iota(jnp.int32, (N_EXPERTS, LANES), 1)
    diag = lane == lax.broadcasted_iota(jnp.int32, (N_EXPERTS, LANES), 0)
    off_row = jnp.sum(jnp.where(diag, off, 0.0), 0, keepdims=True)
    cnt_row = jnp.sum(jnp.where(diag, cnt, 0.0), 0, keepdims=True)
    seg_ref[0] = jnp.concatenate([off_row, cnt_row, jnp.zeros((SUBLANES - 2, LANES), F32)], 0).astype(jnp.int32)


def _route_call(rt):
    N = rt.shape[1]
    nt = N // MOE_TM
    u = jnp.asarray(np.triu(np.ones((LANES, LANES), np.float32), 1), BF16)
    return pl.pallas_call(
        _route_kernel,
        out_shape=[jax.ShapeDtypeStruct((nt * 2 * MOE_TM,), jnp.int32),
                   jax.ShapeDtypeStruct((nt * 2 * MOE_TM,), F32),
                   jax.ShapeDtypeStruct((nt, SUBLANES, LANES), jnp.int32)],
        grid=(nt,),
        in_specs=[pl.BlockSpec((2 * N_EXPERTS, MOE_TM), lambda i: (0, i)), _const_spec(u.shape)],
        out_specs=[pl.BlockSpec((2 * MOE_TM,), lambda i: (i,)),
                   pl.BlockSpec((2 * MOE_TM,), lambda i: (i,)),
                   pl.BlockSpec((1, SUBLANES, LANES), lambda i: (i, 0, 0))],
        compiler_params=_cparams("parallel"),
        name="route",
    )(rt, u)


def _table_index(it):
    per_block = LANES // MOE_UNROLL
    return (it // per_block) * (2 * LANES) + (it % per_block) * MOE_UNROLL


def _moe_kernel(pos_ref, wts_ref, seg_ref, hp_ref, w1_ref, w3_ref, w2_ref, x1_ref, mod_ref, lng_ref, lnb_ref,
                o_ref, xs, ys, oc):
    s = pl.program_id(1)
    half = D_MODEL // 2

    @pl.when(s == 0)
    def _dispatch():
        xs[...] = jnp.zeros_like(xs)

        def body(it, carry):
            t0 = pl.multiple_of(it * (MOE_UNROLL * HP_ROWS), MOE_UNROLL * HP_ROWS)
            i0 = _table_index(it)
            for j in range(MOE_UNROLL):
                row = hp_ref[pl.ds(t0 + HP_ROWS * j, HP_ROWS), :]
                xs[pl.ds(pl.multiple_of(pos_ref[i0 + j], HP_ROWS), HP_ROWS), :] = row
                xs[pl.ds(pl.multiple_of(pos_ref[i0 + LANES + j], HP_ROWS), HP_ROWS), :] = row
            return carry

        lax.fori_loop(0, MOE_TM // MOE_UNROLL, body, 0)

    @pl.when(s < N_EXPERTS)
    def _expert():
        off = seg_ref[0, 0, s]
        nblk = (seg_ref[0, 1, s] + (MOE_RB - 1)) // MOE_RB

        def blk(i, carry):
            r0 = pl.multiple_of(off + i * MOE_RB, SUBLANES)
            parts = [_unpack_halves(xs[pl.ds(pl.multiple_of(r0 * HP_ROWS, SUBLANES) + c, MOE_RB, stride=HP_ROWS), :])
                     for c in range(HP_ROWS)]
            lo = jnp.concatenate([p[0] for p in parts], -1)
            hi = jnp.concatenate([p[1] for p in parts], -1)
            a = _dot(lo, w1_ref[0, :half, :]) + _dot(hi, w1_ref[0, half:, :])
            g = _dot(lo, w3_ref[0, :half, :]) + _dot(hi, w3_ref[0, half:, :])
            act = (a * jax.nn.sigmoid(a) * g).astype(BF16)
            y = _dot(act, w2_ref[0])
            y0 = pl.multiple_of(r0 * Y_ROWS, SUBLANES)
            for c in range(Y_ROWS):
                ys[pl.ds(y0 + c, MOE_RB, stride=Y_ROWS), :] = y[:, c * LANES:(c + 1) * LANES]
            return carry

        lax.fori_loop(0, nblk, blk, 0)

    @pl.when(s >= N_EXPERTS)
    def _combine():
        it0 = (s - N_EXPERTS) * (MOE_CH // MOE_UNROLL)

        def body(it, carry):
            t0 = pl.multiple_of(it * (MOE_UNROLL * Y_ROWS), MOE_UNROLL * Y_ROWS)
            i0 = _table_index(it0 + it)
            for j in range(MOE_UNROLL):
                p0 = pl.multiple_of(pos_ref[i0 + j] * (Y_ROWS // HP_ROWS), Y_ROWS)
                p1 = pl.multiple_of(pos_ref[i0 + LANES + j] * (Y_ROWS // HP_ROWS), Y_ROWS)
                oc[pl.ds(t0 + Y_ROWS * j, Y_ROWS), :] = (wts_ref[i0 + j] * ys[pl.ds(p0, Y_ROWS), :]
                                                         + wts_ref[i0 + LANES + j] * ys[pl.ds(p1, Y_ROWS), :])
            return carry

        lax.fori_loop(0, MOE_CH // MOE_UNROLL, body, 0)
        moe = jnp.concatenate([oc[pl.ds(c, MOE_CH, stride=Y_ROWS), :] for c in range(Y_ROWS)], -1)
        m = mod_ref[0]
        o_ref[...] = _ln(ALPHA * x1_ref[...] + m[5:6] * moe) * lng_ref[...] + lnb_ref[...]


def _moe_call(pos, wts, seg, hp, w1, w3, w2, x1, mod, lng, lnb, T):
    N = x1.shape[0]
    nt = N // MOE_TM
    flat = pl.BlockSpec((2 * MOE_TM,), lambda i, s: (i,), memory_space=pltpu.SMEM)
    wspec = lambda shp: pl.BlockSpec((1,) + shp, lambda i, s: (jnp.minimum(s, N_EXPERTS - 1), 0, 0))
    chunk = pl.BlockSpec((MOE_CH, D_MODEL),
                         lambda i, s: (i * MOE_KC + jnp.clip(s - N_EXPERTS, 0, MOE_KC - 1), 0))
    return pl.pallas_call(
        _moe_kernel,
        out_shape=jax.ShapeDtypeStruct((N, D_MODEL), F32),
        grid=(nt, N_EXPERTS + MOE_KC),
        in_specs=[flat, flat,
                  pl.BlockSpec((1, SUBLANES, LANES), lambda i, s: (i, 0, 0), memory_space=pltpu.SMEM),
                  pl.BlockSpec((MOE_TM * HP_ROWS, LANES), lambda i, s: (i, 0)),
                  wspec((D_MODEL, D_EXPERT)), wspec((D_MODEL, D_EXPERT)), wspec((D_EXPERT, D_MODEL)),
                  chunk, pl.BlockSpec((1, SUBLANES, D_MODEL), lambda i, s: ((i * MOE_TM) // T, 0, 0)),
                  _const_spec(lng.shape), _const_spec(lnb.shape)],
        out_specs=chunk,
        scratch_shapes=[pltpu.VMEM((MOE_SLOTS * HP_ROWS, LANES), jnp.uint32),
                        pltpu.VMEM((MOE_SLOTS * Y_ROWS, LANES), F32),
                        pltpu.VMEM((MOE_CH * Y_ROWS, LANES), F32)],
        compiler_params=_cparams("parallel", "arbitrary"),
        name="moe",
    )(pos, wts, seg, hp, w1, w3, w2, x1, mod, lng, lnb)


def _rope_tables(T):
    t = jnp.arange(T)
    nf = HEAD_DIM // 4
    inv = ROPE_BASE ** (-jnp.arange(nf, dtype=F32) / nf)
    ar = (t // GRID_W).astype(F32)[:, None] * inv
    ac = (t % GRID_W).astype(F32)[:, None] * inv
    cos = jnp.concatenate([jnp.cos(ar), jnp.cos(ar), jnp.cos(ac), jnp.cos(ac)], -1)
    sin = jnp.concatenate([-jnp.sin(ar), jnp.sin(ar), -jnp.sin(ac), jnp.sin(ac)], -1)
    return jnp.tile(cos, (1, LANES // HEAD_DIM)), jnp.tile(sin, (1, LANES // HEAD_DIM))


def _nat_bias_table(rpb):
    q = np.arange(GRID_W)
    kc = np.arange(GRID_W)
    cstart = np.clip(q - NAT_KC // 2, 0, GRID_W - NAT_KC)
    ok = (kc[None, :] >= cstart[:, None]) & (kc[None, :] < cstart[:, None] + NAT_KC)
    cidx = np.clip(kc[None, :] - q[:, None] + NAT_KC - 1, 0, 2 * NAT_KC - 2)
    onehot = (np.arange(2 * NAT_KC - 1)[:, None, None] == cidx[None]).astype(np.float32)
    t = jnp.einsum('hrc,cqk->hqrk', rpb.astype(F32), jnp.asarray(onehot), precision=lax.Precision.HIGHEST)
    t = jnp.where(jnp.asarray(ok)[None, :, None, :], t, NEG_INF)
    per_d = [t[:, :, NAT_KR - 1 - d:2 * NAT_KR - 1 - d, :].reshape(NAT_HEADS // 2, 2 * GRID_W, NAT_KR * GRID_W)
             for d in range(NAT_KR)]
    return jnp.stack(per_d, 1)


def _dup_heads(a):
    a = jnp.broadcast_to(a[..., :, None, :], a.shape[:-1] + (2, a.shape[-1]))
    return a.reshape(a.shape[:-3] + (-1,))


def _block_diag(w):
    eye = jnp.eye(RNN_BLOCKS, dtype=w.dtype)
    return jnp.einsum('dnio,nm->dnimo', w, eye).reshape(2, RNN_WIDTH, RNN_WIDTH)


def _pad_rows(a, rows=SUBLANES):
    return jnp.pad(a, ((0, rows - a.shape[0]),) + ((0, 0),) * (a.ndim - 1))


def kernel(x_prompt, x_sample, c, cache_nat_k, cache_nat_v, cache_swa_k, cache_swa_v, state_rglru, c_ctx,
           w_ada, b_ada, w_in, rg_conv_w, rg_conv_b, rg_wa, rg_ba, rg_wx, rg_bx, rg_lambda, nat_rpb,
           swa_sink, w_branch, w_out, ln_g, ln_b, w_router, router_bias, w1, w3, w2):
    B_c, T_c, _ = x_prompt.shape
    B_l, T_l, _ = x_sample.shape
    P = cache_nat_k.shape[2]

    cv = jnp.concatenate([c_ctx[None], c, jnp.zeros((ADA_ROWS - 1 - B_l, D_MODEL), F32)], 0)
    mod_all = _ada_call(cv, w_ada, b_ada).reshape(DEPTH, ADA_ROWS, 6, D_MODEL)
    mod_all = jnp.pad(mod_all, ((0, 0), (0, 0), (0, SUBLANES - 6), (0, 0)))

    sk0 = 3072
    sv0 = sk0 + SWA_KV_HEADS * HEAD_DIM
    xf0 = sv0 + SWA_KV_HEADS * HEAD_DIM
    w_in_b = w_in[:, :, :GATE_OFF].astype(BF16)
    head = lambda c0, g: w_in_b[:, :, c0 + g * HEAD_DIM:c0 + (g + 1) * HEAD_DIM]
    w_tail_ctx = jnp.concatenate([head(c0, g) for c0 in (sk0, sv0) for g in (0, 0, 1, 1)], -1)
    w_tail_lat = jnp.concatenate([w_in_b[:, :, xf0:GATE_OFF], w_tail_ctx], -1)
    w_gate = w_in[:, :, GATE_OFF:].astype(BF16)
    w_branch_b = w_branch.astype(BF16)
    w_out_b = w_out.astype(BF16)
    w1_b, w3_b, w2_b = w1.astype(BF16), w3.astype(BF16), w2.astype(BF16)
    wr_t = w_router.T.astype(BF16)
    wa_bd = jnp.stack([_block_diag(0.5 * rg_wa[l]) for l in range(DEPTH)]).astype(BF16)
    wx_bd = jnp.stack([_block_diag(0.5 * rg_wx[l]) for l in range(DEPTH)]).astype(BF16)

    plan_ctx = ((0, 0, 512, F32, 1.0, False, False, None), (0, 512, 512, F32, 1.0, False, False, None),
                (0, 1024, 512, BF16, ATTN_SCALE, False, False, None), (0, 1536, 512, F32, 1.0, False, False, 0),
                (0, 2048, 512, F32, 1.0, False, False, 1), (0, 2560, 512, BF16, ATTN_SCALE, False, False, None),
                (0, sk0, 128, F32, 1.0, False, False, 2), (0, sv0, 128, F32, 1.0, False, False, 3),
                (0, xf0, 512, BF16, 1.0, False, True, None),
                (1, 0, 256, BF16, 1.0, False, False, None), (1, 256, 256, BF16, 1.0, False, False, None))
    plan_lat = ((0, 0, 512, F32, 1.0, False, False, None), (0, 512, 512, F32, 1.0, False, False, None),
                (0, 1024, 512, BF16, ATTN_SCALE, False, False, None), (0, 1536, 512, BF16, 1.0, False, False, None),
                (0, 2048, 512, BF16, 1.0, False, False, None), (0, 2560, 512, BF16, ATTN_SCALE, True, False, None),
                (1, 0, 512, BF16, 1.0, False, True, None),
                (1, 512, 256, BF16, 1.0, True, False, None), (1, 768, 256, BF16, 1.0, False, False, None))

    rope_tabs = _rope_tables(T_l)
    ck_nat = cache_nat_k.reshape(B_l, DEPTH, P, W512).astype(BF16)
    cv_nat = cache_nat_v.reshape(B_l, DEPTH, P, W512).astype(BF16)
    ckd_swa = _dup_heads(cache_swa_k).astype(BF16)
    cvd_swa = _dup_heads(cache_swa_v).astype(BF16)
    state8 = jnp.pad(state_rglru, ((0, 0), (0, 0), (0, SUBLANES - 2), (0, 0)))
    zero_state = jnp.zeros((B_c, SUBLANES, RNN_WIDTH), F32)

    def layer(x, l, ctx_pass, caches=None):
        B, T = (B_c, T_c) if ctx_pass else (B_l, T_l)
        TM = min(T, TOKEN_TM)
        modb = jnp.broadcast_to(mod_all[l, 0:1], (B, SUBLANES, D_MODEL)) if ctx_pass else mod_all[l, 1:1 + B_l]
        if ctx_pass:
            xa, ga, nq, nk, nv, sq, sk, sv, xf, skd, svd = _pre_call(
                x, modb, w_in_b, GATE_OFF, w_tail_ctx, l, plan_ctx, B, T, TM, None, caches)
        else:
            xa, ga, nq, nk, nv, sq, xf, skd, svd = _pre_call(
                x, modb, w_in_b, sk0, w_tail_lat, l, plan_lat, B, T, TM, rope_tabs)
        r3 = lambda a: a.reshape(B, T, a.shape[-1])
        h0 = zero_state if ctx_pass else state8[:, l]
        ya, hfin = _rglru_call(
            r3(xa), r3(ga), _pad_rows(rg_conv_w[l]), rg_conv_b[l][None], wa_bd[l], wx_bd[l],
            _pad_rows(0.5 * rg_ba[l]), _pad_rows(0.5 * rg_bx[l]), _pad_rows(rg_lambda[l]), h0)
        if ctx_pass:
            yb, yc = _attn_ctx_call(swa_sink[l], r3(nq), nk, nv, r3(sq), r3(skd), r3(svd), l)
        else:
            yb = _nat_lat_call(r3(nq), r3(nk), r3(nv), ck_nat, cv_nat, _nat_bias_table(nat_rpb[l]), l)
            yc = _swa_lat_call(swa_sink[l], r3(sq), r3(skd), r3(svd), ckd_swa, cvd_swa, l)
        yd = _fourier_call(xf, T)
        f2 = lambda a: a.reshape(B * T, a.shape[-1])
        x1, hp, rt = _merge_call(
            x, modb, (f2(ya), f2(yb), f2(yc), f2(yd)), w_gate[l], w_branch_b[l], w_out_b[l],
            ln_g[l, 0][None], ln_b[l, 0][None], wr_t,
            jnp.broadcast_to(router_bias[:, None], (N_EXPERTS, min(T, SUB_TM))), T, min(T, SUB_TM))
        pos, wts, seg = _route_call(rt)
        x2 = _moe_call(pos, wts, seg, hp, w1_b[l], w3_b[l], w2_b[l], x1, modb,
                       ln_g[l, 1][None], ln_b[l, 1][None], T)
        new = ((nk, nv, sk, sv), hfin[:, :2]) if ctx_pass else None
        return x2, new

    y = x_prompt.reshape(B_c * T_c, D_MODEL)
    caches, states = None, []
    for l in range(DEPTH):
        y, (caches, st) = layer(y, l, True, caches)
        states.append(st)
    y_prompt = y.reshape(B_c, T_c, D_MODEL)
    new_nat_k = caches[0].reshape(B_c, DEPTH, T_c, NAT_HEADS, HEAD_DIM)
    new_nat_v = caches[1].reshape(B_c, DEPTH, T_c, NAT_HEADS, HEAD_DIM)
    new_swa_k = caches[2].reshape(B_c, DEPTH, T_c, SWA_KV_HEADS, HEAD_DIM)
    new_swa_v = caches[3].reshape(B_c, DEPTH, T_c, SWA_KV_HEADS, HEAD_DIM)
    new_state = jnp.stack(states, 1)

    y = x_sample.reshape(B_l * T_l, D_MODEL)
    for l in range(DEPTH):
        y, _ = layer(y, l, False)
    y_sample = y.reshape(B_l, T_l, D_MODEL)
    return (y_prompt, y_sample, new_nat_k, new_nat_v, new_swa_k, new_swa_v, new_state)
```

```python
import functools
import math

import jax
import jax.numpy as jnp
import numpy as np
from jax import lax
from jax.experimental import pallas as pl
from jax.experimental.pallas import tpu as pltpu

F32 = jnp.float32
BF16 = jnp.bfloat16

D_MODEL = 1024
DEPTH = 4
GRID_W = 64
HEAD_DIM = 64
ATTN_SCALE = HEAD_DIM ** -0.5
LOG2E = math.log2(math.e)
Q_SCALE = ATTN_SCALE * LOG2E
NEG_INF = -1e30
LN_EPS = 1e-5
ALPHA = (2 * DEPTH) ** 0.25
ROPE_BASE = 10000.0
RNN_WIDTH = 512
RNN_BLOCKS = 8
RNN_BLOCK = RNN_WIDTH // RNN_BLOCKS
CONV_W = 4
RGLRU_C = 8.0
NAT_HEADS = 8
NAT_KR = 8
NAT_KC = 16
SWA_HEADS = 8
SWA_KV_HEADS = 2
SWA_WINDOW = 128
SWA_BLOCK = 128
SWA_SPAN = SWA_BLOCK + 2 * SWA_WINDOW
FNET_GROUPS = 4
FNET_WIDTH = 512
FNET_GC = FNET_WIDTH // FNET_GROUPS
N_BRANCH = 4
N_EXPERTS = 16
N_EXPERT_GROUPS = 4
EXPERTS_PER_GROUP = N_EXPERTS // N_EXPERT_GROUPS
D_EXPERT = 512
W512 = 512
GATE_OFF = 3840

LANES = 128
SUBLANES = 8
VMEM_LIMIT = 56 * 1024 * 1024
TOKEN_TM = 1024
SUB_TM = 512


def _cparams(*sem):
    return pltpu.CompilerParams(dimension_semantics=sem, vmem_limit_bytes=VMEM_LIMIT)


def _const_spec(shape):
    nd = len(shape)
    return pl.BlockSpec(shape, lambda *_: (0,) * nd, pipeline_mode=pl.Buffered(1))


def _ln(x):
    mu = jnp.mean(x, -1, keepdims=True)
    xc = x - mu
    var = jnp.mean(xc * xc, -1, keepdims=True)
    return xc * lax.rsqrt(var + LN_EPS)


def _dot(a, b):
    return jnp.dot(a, b, preferred_element_type=F32)


def _dot_nt(a, b):
    return lax.dot_general(a, b, (((1,), (1,)), ((), ())), preferred_element_type=F32)


ADA_ROWS = 16
ADA_TN = 1024


def _ada_kernel(c_ref, w_ref, b_ref, o_ref):
    cv = c_ref[...]
    s = (cv * jax.nn.sigmoid(cv)).astype(BF16)
    o_ref[0] = _dot(s, w_ref[0].astype(BF16)) + b_ref[0]


def _ada_call(cv, w_ada, b_ada):
    n = w_ada.shape[-1]
    return pl.pallas_call(
        _ada_kernel,
        out_shape=jax.ShapeDtypeStruct((DEPTH, ADA_ROWS, n), F32),
        grid=(DEPTH, n // ADA_TN),
        in_specs=[
            pl.BlockSpec((ADA_ROWS, D_MODEL), lambda l, j: (0, 0)),
            pl.BlockSpec((1, D_MODEL, ADA_TN), lambda l, j: (l, 0, j)),
            pl.BlockSpec((1, 1, ADA_TN), lambda l, j: (l, 0, j)),
        ],
        out_specs=pl.BlockSpec((1, ADA_ROWS, ADA_TN), lambda l, j: (l, 0, j)),
        compiler_params=_cparams("parallel", "parallel"),
        name="ada",
    )(cv, w_ada, b_ada.reshape(DEPTH, 1, n))


def _rope(u, cos, sin):
    lane = lax.broadcasted_iota(jnp.int32, cos.shape, 1)
    first = (lane & 31) < 16
    outs = []
    for j in range(u.shape[1] // LANES):
        s = u[:, j * LANES:(j + 1) * LANES]
        partner = jnp.where(first, pltpu.roll(s, LANES - 16, 1), pltpu.roll(s, 16, 1))
        outs.append(s * cos + partner * sin)
    return outs[0] if len(outs) == 1 else jnp.concatenate(outs, -1)


def _pre_kernel(*refs, plan, rope, n1, nsub, n_alias):
    x_ref, mod_ref, wm_ref, wt_ref = refs[:4]
    refs = refs[4:]
    if rope:
        cos_ref, sin_ref = refs[:2]
        refs = refs[2:]
    refs = refs[n_alias:]
    outs = refs[:len(plan)]
    m = mod_ref[0]
    sub = x_ref.shape[0] // nsub
    hs = [(_ln(x_ref[k * sub:(k + 1) * sub, :]) * (1.0 + m[1:2]) + m[0:1]).astype(BF16) for k in range(nsub)]
    for (src, off, width, _, scale, do_rope, fourier, slot), o_ref in zip(plan, outs):
        w_ref = wt_ref if src else wm_ref
        for k in range(nsub):
            rows = slice(k * sub, (k + 1) * sub)
            u = _dot(hs[k], w_ref[0, :, off:off + width])
            if do_rope:
                u = _rope(u, cos_ref[rows, :], sin_ref[rows, :])
            if scale != 1.0:
                u = u * scale
            if fourier and n1 > 1:
                stage = refs[len(plan)]
                for c in range(width // LANES):
                    stage[k, c] = u[:, c * LANES:(c + 1) * LANES]
                r = sub // n1
                for t1 in range(n1):
                    for c in range(width // LANES):
                        lo = t1 * width + c * LANES
                        o_ref[0, k * r:(k + 1) * r, lo:lo + LANES] = (
                            stage[k, c, pl.ds(t1, r, stride=n1), :].astype(o_ref.dtype))
            elif fourier:
                o_ref[0, rows, :] = u.astype(o_ref.dtype)
            elif slot is not None:
                o_ref[0, 0, rows, :] = u.astype(o_ref.dtype)
            else:
                o_ref[rows, :] = u.astype(o_ref.dtype)


def _pre_call(x, mod, w_main, n_main, w_tail, l, plan, B, T, TM, rope_tabs, caches=None):
    N = x.shape[0]
    tpb = T // TM
    nsub = max(1, TM // SUB_TM)
    n1 = T // FN_N2
    rope = rope_tabs is not None
    n_tail = w_tail.shape[-1]
    in_specs = [
        pl.BlockSpec((TM, D_MODEL), lambda i: (i, 0)),
        pl.BlockSpec((1, SUBLANES, D_MODEL), lambda i: (i // tpb, 0, 0)),
        pl.BlockSpec((1, D_MODEL, n_main), lambda i: (l, 0, 0), pipeline_mode=pl.Buffered(1)),
        pl.BlockSpec((1, D_MODEL, n_tail), lambda i: (l, 0, 0), pipeline_mode=pl.Buffered(1)),
    ]
    args = [x, mod, w_main, w_tail]
    if rope:
        in_specs += [pl.BlockSpec((TM, LANES), lambda i: (i % tpb, 0))] * 2
        args += list(rope_tabs)
    out_shape, out_specs, scratch, aliases = [], [], [], {}
    for k, p in enumerate(plan):
        if p[6]:
            out_shape.append(jax.ShapeDtypeStruct((B, FN_N2, n1 * p[2]), p[3]))
            out_specs.append(pl.BlockSpec((1, TM // n1, n1 * p[2]), lambda i: (i // tpb, i % tpb, 0)))
            if n1 > 1:
                scratch.append(pltpu.VMEM((nsub, p[2] // LANES, TM // nsub, LANES), F32))
        elif p[7] is not None:
            out_shape.append(jax.ShapeDtypeStruct((B, DEPTH, T, p[2]), p[3]))
            out_specs.append(pl.BlockSpec((1, 1, TM, p[2]), lambda i: (i // tpb, l, i % tpb, 0)))
            if caches is not None:
                aliases[len(args)] = k
                in_specs.append(pl.BlockSpec(memory_space=pl.ANY))
                args.append(caches[p[7]])
        else:
            out_shape.append(jax.ShapeDtypeStruct((N, p[2]), p[3]))
            out_specs.append(pl.BlockSpec((TM, p[2]), lambda i: (i, 0)))
    return pl.pallas_call(
        functools.partial(_pre_kernel, plan=plan, rope=rope, n1=n1, nsub=nsub, n_alias=len(aliases)),
        out_shape=out_shape,
        grid=(N // TM,),
        in_specs=in_specs,
        out_specs=out_specs,
        scratch_shapes=scratch,
        input_output_aliases=aliases,
        compiler_params=_cparams("parallel"),
        name="pre",
    )(*args)


RG_CW = 256
RG_TCH = 256


def _rglru_kernel(xa_ref, ga_ref, cw_ref, cb_ref, wa_ref, wx_ref, ba_ref, bx_ref, lam_ref, h0_ref,
                  y_ref, hfin_ref, xpad, a_f, u_f, a_b, u_b, *, T):
    cw_ = RG_CW
    zeros8 = jnp.zeros((SUBLANES, cw_), F32)
    xpad[0:SUBLANES, :] = zeros8
    xpad[T + SUBLANES:T + 2 * SUBLANES, :] = zeros8
    xpad[SUBLANES:T + SUBLANES, :] = xa_ref[0]
    cw = cw_ref[...]
    cb = cb_ref[...]
    lam = lam_ref[...]
    sp = jnp.maximum(-lam, 0.0) + jnp.log1p(jnp.exp(-jnp.abs(lam)))
    sp4 = (0.5 * RGLRU_C) * sp
    nsp4_log2e = sp4 * (-math.log2(math.e))
    ba = ba_ref[...]
    bx = bx_ref[...]
    h0 = h0_ref[0]

    def chunk(c, carry):
        base = pl.multiple_of(c * RG_TCH, RG_TCH)
        xw = xpad[pl.ds(base, RG_TCH + 2 * SUBLANES), :]
        xc = cb
        for i in range(CONV_W):
            xc = xc + cw[i:i + 1] * xw[SUBLANES - 1 + i:SUBLANES - 1 + i + RG_TCH]
        xcb = xc.astype(BF16)
        half_xc = 0.5 * xc
        for d, (a_s, u_s) in enumerate(((a_f, u_f), (a_b, u_b))):
            r2 = jnp.tanh(_dot(xcb, wa_ref[d]) + ba[d:d + 1]) + 1.0
            i2 = jnp.tanh(_dot(xcb, wx_ref[d]) + bx[d:d + 1]) + 1.0
            a = jnp.exp2(r2 * nsp4_log2e[d:d + 1])
            v = jnp.tanh(r2 * sp4[d:d + 1]) * (1.0 + a * a)
            gain = jnp.where(v > 0.0, v * lax.rsqrt(v), 0.0)
            a_s[pl.ds(base, RG_TCH), :] = a
            u_s[pl.ds(base, RG_TCH), :] = gain * (i2 * half_xc)
        return carry

    lax.fori_loop(0, T // RG_TCH, chunk, 0)

    row = lax.broadcasted_iota(jnp.int32, (SUBLANES, cw_), 0)

    def block_scan(a, u, reverse):
        for dd in (1, 2, 4):
            sh = SUBLANES - dd if reverse else dd
            a_n = pltpu.roll(a, sh, 0)
            u_n = pltpu.roll(u, sh, 0)
            ok = (row < SUBLANES - dd) if reverse else (row >= dd)
            u = jnp.where(ok, u + a * u_n, u)
            a = jnp.where(ok, a * a_n, a)
        return a, u

    nblk = T // SUBLANES

    def scan(i, carry):
        c_f, c_b = carry
        lo_f = pl.multiple_of(i * SUBLANES, SUBLANES)
        lo_b = pl.multiple_of((nblk - 1 - i) * SUBLANES, SUBLANES)
        a, u = block_scan(a_f[pl.ds(lo_f, SUBLANES), :], u_f[pl.ds(lo_f, SUBLANES), :], False)
        h_f = u + a * c_f
        u_f[pl.ds(lo_f, SUBLANES), :] = h_f
        a, u = block_scan(a_b[pl.ds(lo_b, SUBLANES), :], u_b[pl.ds(lo_b, SUBLANES), :], True)
        h_b = u + a * c_b
        u_b[pl.ds(lo_b, SUBLANES), :] = h_b
        return (jnp.broadcast_to(h_f[SUBLANES - 1:SUBLANES, :], (SUBLANES, cw_)),
                jnp.broadcast_to(h_b[0:1, :], (SUBLANES, cw_)))

    c_f, c_b = lax.fori_loop(
        0, nblk, scan,
        (jnp.broadcast_to(h0[0:1, :], (SUBLANES, cw_)), jnp.broadcast_to(h0[1:2, :], (SUBLANES, cw_))),
        unroll=2)
    hfin_ref[0] = jnp.where(row == 0, c_f, jnp.where(row == 1, c_b, 0.0))

    def emit(c, carry):
        base = pl.multiple_of(c * RG_TCH, RG_TCH)
        h = u_f[pl.ds(base, RG_TCH), :] + u_b[pl.ds(base, RG_TCH), :]
        y_ref[0, pl.ds(base, RG_TCH), :] = (h * jax.nn.gelu(ga_ref[0, pl.ds(base, RG_TCH), :])).astype(y_ref.dtype)
        return carry

    lax.fori_loop(0, T // RG_TCH, emit, 0)


def _rglru_call(xa, ga, cw, cb, wa, wx, ba, bx, lam, h0):
    B, T, _ = xa.shape
    nj = RNN_WIDTH // RG_CW
    seq = pl.BlockSpec((1, T, RG_CW), lambda b, j: (b, 0, j))
    vec8 = pl.BlockSpec((SUBLANES, RG_CW), lambda b, j: (0, j))
    wsp = pl.BlockSpec((2, RG_CW, RG_CW), lambda b, j: (0, j, j))
    st = pl.BlockSpec((1, SUBLANES, RG_CW), lambda b, j: (b, 0, j))
    return pl.pallas_call(
        functools.partial(_rglru_kernel, T=T),
        out_shape=[jax.ShapeDtypeStruct((B, T, RNN_WIDTH), BF16),
                   jax.ShapeDtypeStruct((B, SUBLANES, RNN_WIDTH), F32)],
        grid=(B, nj),
        in_specs=[seq, seq, vec8, pl.BlockSpec((1, RG_CW), lambda b, j: (0, j)),
                  wsp, wsp, vec8, vec8, vec8, st],
        out_specs=[seq, st],
        scratch_shapes=[pltpu.VMEM((T + 2 * SUBLANES, RG_CW), F32)]
        + [pltpu.VMEM((T, RG_CW), F32)] * 4,
        compiler_params=_cparams("parallel", "parallel"),
        name="rglru",
    )(xa, ga, cw, cb, wa, wx, ba, bx, lam, h0)


def _attend_slabs(jobs):
    M = jobs[0][0].shape[0]
    lo = lax.broadcasted_iota(jnp.int32, (M, LANES), 1) < HEAD_DIM
    scores = []
    for q2, srcs, _ in jobs:
        zero = jnp.zeros_like(q2)
        qs = jnp.concatenate([jnp.where(lo, q2, zero), jnp.where(lo, zero, q2)], 0)
        ss = []
        for k, _, bias in srcs:
            s = _dot_nt(qs, k)
            ss.append(s if bias is None else s + bias)
        scores.append(ss)
    maxima = []
    for (_, _, sink_col), ss in zip(jobs, scores):
        m = jnp.max(ss[0], -1, keepdims=True)
        for s in ss[1:]:
            m = jnp.maximum(m, jnp.max(s, -1, keepdims=True))
        maxima.append(m if sink_col is None else jnp.maximum(m, sink_col))
    outs = []
    for (_, srcs, sink_col), ss, m in zip(jobs, scores, maxima):
        den = None
        o = None
        for s, (_, v, _) in zip(ss, srcs):
            e = jnp.exp2(s - m)
            d_ = jnp.sum(e, -1, keepdims=True)
            o_ = _dot(e.astype(BF16), v)
            den = d_ if den is None else den + d_
            o = o_ if o is None else o + o_
        if sink_col is not None:
            den = den + jnp.exp2(sink_col - m)
        o = o / den
        outs.append(jnp.where(lo, o[:M], o[M:]))
    return outs


def _sink_col(sink_ref, j, M):
    r = lax.broadcasted_iota(jnp.int32, (2 * M, 1), 0)
    return jnp.where(r < M, sink_ref[2 * j], sink_ref[2 * j + 1]) * LOG2E


def _slab(j):
    return slice(j * LANES, (j + 1) * LANES)


def _attn_ctx_kernel(sink_ref, nq_ref, nk_ref, nv_ref, sq_ref, skd_ref, svd_ref, yb_ref, yc_ref):
    M = nq_ref.shape[1]
    nslab = W512 // LANES
    jobs = []
    for j in range(nslab):
        k2 = nk_ref[0, 0, :, _slab(j)].astype(BF16)
        v2 = nv_ref[0, 0, :, _slab(j)].astype(BF16)
        jobs.append((nq_ref[0, :, _slab(j)], [(k2, v2, None)], None))
    for j in range(nslab):
        g = j // 2
        jobs.append((sq_ref[0, :, _slab(j)], [(skd_ref[0, :, _slab(g)], svd_ref[0, :, _slab(g)], None)],
                     _sink_col(sink_ref, j, M)))
    outs = _attend_slabs(jobs)
    for j in range(nslab):
        yb_ref[0, :, _slab(j)] = outs[j].astype(yb_ref.dtype)
        yc_ref[0, :, _slab(j)] = outs[nslab + j].astype(yc_ref.dtype)


def _attn_ctx_call(sink, nq, nk, nv, sq, skd, svd, l):
    B, T, _ = nq.shape
    s512 = pl.BlockSpec((1, T, W512), lambda b: (b, 0, 0))
    s256 = pl.BlockSpec((1, T, 2 * LANES), lambda b: (b, 0, 0))
    cache = pl.BlockSpec((1, 1, T, W512), lambda b: (b, l, 0, 0))
    return pl.pallas_call(
        _attn_ctx_kernel,
        out_shape=[jax.ShapeDtypeStruct((B, T, W512), BF16)] * 2,
        grid=(B,),
        in_specs=[pl.BlockSpec(memory_space=pltpu.SMEM), s512, cache, cache, s512, s256, s256],
        out_specs=[s512, s512],
        compiler_params=_cparams("parallel"),
        name="attn_ctx",
    )(sink, nq, nk, nv, sq, skd, svd)


NAT_NLOC = NAT_KR * GRID_W


def _nat_lat_kernel(q_ref, k_ref, v_ref, ck_ref, cv_ref, bias_ref, y_ref, *, rows):
    r = pl.program_id(1)
    rstart = jnp.clip(r - NAT_KR // 2, 0, rows - NAT_KR)
    d = r - rstart
    kbase = pl.multiple_of(rstart * GRID_W, GRID_W)
    jobs = []
    for j in range(W512 // LANES):
        k2 = k_ref[0, pl.ds(kbase, NAT_NLOC), _slab(j)]
        v2 = v_ref[0, pl.ds(kbase, NAT_NLOC), _slab(j)]
        jobs.append((q_ref[0, :, _slab(j)],
                     [(k2, v2, bias_ref[j, d]), (ck_ref[0, 0, :, _slab(j)], cv_ref[0, 0, :, _slab(j)], None)], None))
    for j, o in enumerate(_attend_slabs(jobs)):
        y_ref[0, :, _slab(j)] = o.astype(y_ref.dtype)


def _nat_lat_call(q, k, v, ck, cv, bias, l):
    B, T, _ = q.shape
    rows = T // GRID_W
    P = ck.shape[2]
    qs = pl.BlockSpec((1, GRID_W, W512), lambda b, r: (b, r, 0))
    full = pl.BlockSpec((1, T, W512), lambda b, r: (b, 0, 0))
    cs = pl.BlockSpec((1, 1, P, W512), lambda b, r: (b, l, 0, 0))
    return pl.pallas_call(
        functools.partial(_nat_lat_kernel, rows=rows),
        out_shape=jax.ShapeDtypeStruct((B, T, W512), BF16),
        grid=(B, rows),
        in_specs=[qs, full, full, cs, cs, _const_spec(bias.shape)],
        out_specs=qs,
        compiler_params=_cparams("parallel", "arbitrary"),
        name="nat_lat",
    )(q, k, v, ck, cv, bias)


def _swa_band_masks():
    i = np.arange(2 * SWA_BLOCK)[:, None] % SWA_BLOCK
    j = np.arange(SWA_SPAN)[None, :]
    tabs = []
    for span_start in (0, -SWA_WINDOW, -2 * SWA_WINDOW):
        dist = span_start + j - i
        tabs.append(np.where(np.abs(dist) <= SWA_WINDOW, 0.0, NEG_INF))
    return jnp.asarray(np.stack(tabs), F32)


def _swa_lat_kernel(sink_ref, q_ref, kd_ref, vd_ref, ckd_ref, cvd_ref, band_ref, y_ref, *, T):
    blk = pl.program_id(1)
    M = SWA_BLOCK
    start = blk * SWA_BLOCK
    ks = pl.multiple_of(jnp.clip(start - SWA_WINDOW, 0, T - SWA_SPAN), SWA_BLOCK)
    bias = band_ref[jnp.where(blk == 0, 0, jnp.where(blk == T // SWA_BLOCK - 1, 2, 1))]
    jobs = []
    for j in range(W512 // LANES):
        g = j // 2
        jobs.append((q_ref[0, :, _slab(j)],
                     [(kd_ref[0, pl.ds(ks, SWA_SPAN), _slab(g)], vd_ref[0, pl.ds(ks, SWA_SPAN), _slab(g)], bias),
                      (ckd_ref[0, 0, :, _slab(g)], cvd_ref[0, 0, :, _slab(g)], None)],
                     _sink_col(sink_ref, j, M)))
    for j, o in enumerate(_attend_slabs(jobs)):
        y_ref[0, :, _slab(j)] = o.astype(y_ref.dtype)


def _swa_lat_call(sink, q, kd, vd, ckd, cvd, l):
    B, T, _ = q.shape
    P = ckd.shape[2]
    qs = pl.BlockSpec((1, SWA_BLOCK, W512), lambda b, i: (b, i, 0))
    full = pl.BlockSpec((1, T, 2 * LANES), lambda b, i: (b, 0, 0))
    cs = pl.BlockSpec((1, 1, P, 2 * LANES), lambda b, i: (b, l, 0, 0))
    assert T // SWA_BLOCK >= 3
    band = _swa_band_masks()
    return pl.pallas_call(
        functools.partial(_swa_lat_kernel, T=T),
        out_shape=jax.ShapeDtypeStruct((B, T, W512), BF16),
        grid=(B, T // SWA_BLOCK),
        in_specs=[pl.BlockSpec(memory_space=pltpu.SMEM), qs, full, full, cs, cs, _const_spec(band.shape)],
        out_specs=qs,
        compiler_params=_cparams("parallel", "arbitrary"),
        name="swa_lat",
    )(sink, q, kd, vd, ckd, cvd, band)


FN_N2 = 256
FN_GP = 2


def _fourier_kernel(x_ref, cc_ref, m_ref, y_ref, *scratch, n1, coef):
    gw = FN_GP * FNET_GC
    cc = cc_ref[...]
    for gp in range(FNET_GROUPS // FN_GP):
        parts = []
        for t1 in range(n1):
            ws = []
            for g in range(FN_GP):
                lo = t1 * FNET_WIDTH + (gp * FN_GP + g) * FNET_GC
                ws.append(_dot(x_ref[0, :, lo:lo + FNET_GC], cc))
            wr = jnp.concatenate([w[:, :FNET_GC] for w in ws], -1)
            wi = jnp.concatenate([w[:, FNET_GC:] for w in ws], -1)
            v = jnp.concatenate([wr, wi], 0).astype(BF16)
            b = _dot(m_ref[t1], v)
            if n1 == 1:
                y_ref[0, :, gp * gw:(gp + 1) * gw] = b.astype(y_ref.dtype)
            else:
                scratch[0][t1] = b
        if n1 > 1:
            bs = scratch[0]
            for k1 in range(n1):
                acc = None
                for t1 in range(n1):
                    c, s = coef[k1][t1]
                    for cf, lo in ((c, 0), (s, FN_N2)):
                        if cf == 0.0:
                            continue
                        blk = bs[t1, lo:lo + FN_N2, :]
                        term = blk if cf == 1.0 else (-blk if cf == -1.0 else cf * blk)
                        acc = term if acc is None else acc + term
                y_ref[0, k1 * FN_N2:(k1 + 1) * FN_N2, gp * gw:(gp + 1) * gw] = acc.astype(y_ref.dtype)


def _dft_consts(T):
    n1 = T // FN_N2
    j = np.arange(FNET_GC)
    ang = 2 * np.pi * np.outer(j, j) / FNET_GC
    sc = 1.0 / math.sqrt(FNET_GC)
    cc = np.concatenate([np.cos(ang) * sc, -np.sin(ang) * sc], 1)
    k2 = np.arange(FN_N2)[:, None]
    t2 = np.arange(FN_N2)[None, :]
    st = 1.0 / math.sqrt(T)
    mats = []
    for t1 in range(n1):
        th = 2 * np.pi * ((k2 * (n1 * t2 + t1)) % T) / T
        c, s = np.cos(th) * st, np.sin(th) * st
        top = np.concatenate([c, s], 1)
        mats.append(top if n1 == 1 else np.concatenate([top, np.concatenate([-s, c], 1)], 0))
    coef = []
    for k1 in range(n1):
        rowc = []
        for t1 in range(n1):
            q = (k1 * t1) % n1
            c, s = math.cos(2 * math.pi * q / n1), math.sin(2 * math.pi * q / n1)
            c = 0.0 if abs(c) < 1e-12 else (1.0 if abs(c - 1) < 1e-12 else (-1.0 if abs(c + 1) < 1e-12 else c))
            s = 0.0 if abs(s) < 1e-12 else (1.0 if abs(s - 1) < 1e-12 else (-1.0 if abs(s + 1) < 1e-12 else s))
            rowc.append((c, s))
        coef.append(tuple(rowc))
    return n1, jnp.asarray(cc, BF16), jnp.asarray(np.stack(mats), BF16), tuple(coef)


def _fourier_call(xv, T):
    B = xv.shape[0]
    n1, cc, mats, coef = _dft_consts(T)
    gw = FN_GP * FNET_GC
    scratch = [pltpu.VMEM((n1, 2 * FN_N2, gw), F32)] if n1 > 1 else []
    return pl.pallas_call(
        functools.partial(_fourier_kernel, n1=n1, coef=coef),
        out_shape=jax.ShapeDtypeStruct((B, T, FNET_WIDTH), BF16),
        grid=(B,),
        in_specs=[pl.BlockSpec((1, FN_N2, n1 * FNET_WIDTH), lambda b: (b, 0, 0)),
                  _const_spec(cc.shape), _const_spec(mats.shape)],
        out_specs=pl.BlockSpec((1, T, FNET_WIDTH), lambda b: (b, 0, 0)),
        scratch_shapes=scratch,
        compiler_params=_cparams("parallel"),
        name="fourier",
    )(xv, cc, mats)


def _route(scores, biased):
    one, zero = jnp.float32(1.0), jnp.float32(0.0)
    in2, gscore = [], []
    for g in range(N_EXPERT_GROUPS):
        vs = [biased[g * EXPERTS_PER_GROUP + j:g * EXPERTS_PER_GROUP + j + 1] for j in range(EXPERTS_PER_GROUP)]
        gs = None
        for j in range(EXPERTS_PER_GROUP):
            rank = None
            for i in range(EXPERTS_PER_GROUP):
                if i == j:
                    continue
                beats = (vs[i] >= vs[j]) if i < j else (vs[i] > vs[j])
                t = jnp.where(beats, one, zero)
                rank = t if rank is None else rank + t
            keep = jnp.where(rank < 2.0, one, zero)
            in2.append(keep)
            t = keep * vs[j]
            gs = t if gs is None else gs + t
        gscore.append(gs)
    rows, sel = [], []
    for g in range(N_EXPERT_GROUPS):
        lost = None
        for i in range(N_EXPERT_GROUPS):
            if i == g:
                continue
            beats = (gscore[i] >= gscore[g]) if i < g else (gscore[i] > gscore[g])
            t = jnp.where(beats, one, zero)
            lost = t if lost is None else lost + t
        gsel = jnp.where(lost < 1.0, one, zero)
        for j in range(EXPERTS_PER_GROUP):
            e = g * EXPERTS_PER_GROUP + j
            sel.append(gsel * in2[e])
            rows.append(sel[-1] * scores[e:e + 1])
    tot = rows[0]
    for rr in rows[1:]:
        tot = tot + rr
    return jnp.concatenate(rows, 0) / tot, jnp.concatenate(sel, 0)


HP_ROWS = D_MODEL // 2 // LANES
Y_ROWS = D_MODEL // LANES


def _pack_halves(v):
    w = v.shape[1] // 2
    lo = lax.bitcast_convert_type(v[:, :w].astype(F32), jnp.uint32)
    hi = lax.bitcast_convert_type(v[:, w:].astype(F32), jnp.uint32)
    return (lo >> 16) | hi


def _unpack_halves(p):
    lo = lax.bitcast_convert_type(p << 16, F32).astype(BF16)
    hi = lax.bitcast_convert_type(p & jnp.uint32(0xFFFF0000), F32).astype(BF16)
    return lo, hi


def _merge_kernel(x_ref, mod_ref, ya_ref, yb_ref, yc_ref, yd_ref, wg_ref, wb_ref, wo_ref,
                  lng_ref, lnb_ref, wr_ref, rb_ref, x1_ref, hp_ref, rt_ref, *, nsub):
    m = mod_ref[0]
    sub = x_ref.shape[0] // nsub
    for k in range(nsub):
        rows = slice(k * sub, (k + 1) * sub)
        x = x_ref[rows, :]
        h = (_ln(x) * (1.0 + m[1:2]) + m[0:1]).astype(BF16)
        merged = None
        for b, y_ref in enumerate((ya_ref, yb_ref, yc_ref, yd_ref)):
            gate = 0.5 * jnp.tanh(_dot(h, wg_ref[:, b * D_MODEL:(b + 1) * D_MODEL])) + 0.5
            term = gate * _dot(y_ref[rows, :], wb_ref[b])
            merged = term if merged is None else merged + term
        out = _dot(merged.astype(BF16), wo_ref[...])
        x1 = _ln(ALPHA * x + m[2:3] * out) * lng_ref[...] + lnb_ref[...]
        x1_ref[rows, :] = x1
        h2 = (_ln(x1) * (1.0 + m[4:5]) + m[3:4]).astype(BF16)
        packed = _pack_halves(h2)
        for c in range(HP_ROWS):
            hp_ref[pl.ds(k * sub * HP_ROWS + c, sub, stride=HP_ROWS), :] = packed[:, c * LANES:(c + 1) * LANES]
        logits = _dot_nt(wr_ref[...], h2)
        e = jnp.exp(logits - jnp.max(logits, 0, keepdims=True))
        scores = e / jnp.sum(e, 0, keepdims=True)
        comb, sel = _route(scores, scores + rb_ref[...])
        rt_ref[:, rows] = jnp.concatenate([comb, sel], 0)


def _merge_call(x, mod, ys, wg, wb, wo, lng, lnb, wr_t, rb, T, TM):
    N = x.shape[0]
    tpb = T // TM
    tok = lambda w: pl.BlockSpec((TM, w), lambda i: (i, 0))
    return pl.pallas_call(
        functools.partial(_merge_kernel, nsub=max(1, TM // SUB_TM)),
        out_shape=[jax.ShapeDtypeStruct((N, D_MODEL), F32), jax.ShapeDtypeStruct((N * HP_ROWS, LANES), jnp.uint32),
                   jax.ShapeDtypeStruct((2 * N_EXPERTS, N), F32)],
        grid=(N // TM,),
        in_specs=[tok(D_MODEL), pl.BlockSpec((1, SUBLANES, D_MODEL), lambda i: (i // tpb, 0, 0)),
                  tok(W512), tok(W512), tok(W512), tok(W512),
                  _const_spec(wg.shape), _const_spec(wb.shape), _const_spec(wo.shape),
                  _const_spec(lng.shape), _const_spec(lnb.shape), _const_spec(wr_t.shape), _const_spec(rb.shape)],
        out_specs=[tok(D_MODEL), pl.BlockSpec((TM * HP_ROWS, LANES), lambda i: (i, 0)),
                   pl.BlockSpec((2 * N_EXPERTS, TM), lambda i: (0, i))],
        compiler_params=_cparams("parallel"),
        name="merge",
    )(x, mod, *ys, wg, wb, wo, lng, lnb, wr_t, rb)


MOE_TM = 2048
MOE_RB = 256
MOE_KC = 4
MOE_CH = MOE_TM // MOE_KC
MOE_SLOTS = 2 * MOE_TM + N_EXPERTS * SUBLANES + MOE_RB
MOE_PR = 2 * MOE_TM // LANES
MOE_UNROLL = 8


def _route_kernel(rt_ref, u_ref, pos_ref, wts_ref, seg_ref):
    comb = rt_ref[0:N_EXPERTS, :]
    sel = rt_ref[N_EXPERTS:2 * N_EXPERTS, :]
    nb = MOE_TM // LANES
    stacked = jnp.concatenate([sel[:, b * LANES:(b + 1) * LANES] for b in range(nb)], 0)
    within = _dot(stacked.astype(BF16), u_ref[...])
    tot = jnp.sum(stacked, -1, keepdims=True)
    base = jnp.zeros((N_EXPERTS, 1), F32)
    bases = []
    for b in range(nb):
        bases.append(base)
        base = base + tot[b * N_EXPERTS:(b + 1) * N_EXPERTS]
    cnt = base
    padded = jnp.floor((cnt + (SUBLANES - 1.0)) * (1.0 / SUBLANES)) * SUBLANES
    rowi = lax.broadcasted_iota(jnp.int32, (N_EXPERTS, 1), 0)
    off = jnp.zeros((N_EXPERTS, 1), F32)
    for e in range(N_EXPERTS - 1):
        off = off + jnp.where(rowi > e, padded[e:e + 1, :], 0.0)
    pos_rows, wts_rows = [], []
    for b in range(nb):
        slot = within[b * N_EXPERTS:(b + 1) * N_EXPERTS] + (bases[b] + off)
        seen = jnp.zeros((1, LANES), F32)
        acc = [jnp.zeros((1, LANES), F32) for _ in range(4)]
        for e in range(N_EXPERTS):
            s_e = sel[e:e + 1, b * LANES:(b + 1) * LANES]
            c_e = comb[e:e + 1, b * LANES:(b + 1) * LANES]
            first = jnp.where(seen == 0.0, s_e, 0.0)
            second = jnp.where(seen == 1.0, s_e, 0.0)
            acc[0] = acc[0] + first * slot[e:e + 1]
            acc[1] = acc[1] + second * slot[e:e + 1]
            acc[2] = acc[2] + first * c_e
            acc[3] = acc[3] + second * c_e
            seen = seen + s_e
        pos_rows += acc[0:2]
        wts_rows += acc[2:4]
    pos_ref[...] = (jnp.concatenate(pos_rows, 0) * float(HP_ROWS)).astype(jnp.int32).reshape(2 * MOE_TM)
    wts_ref[...] = jnp.concatenate(wts_rows, 0).reshape(2 * MOE_TM)
    lane = lax.broadcasted_iota(jnp.int32, (N_EXPERTS, LANES), 1)
    diag = lane == lax.broadcasted_iota(jnp.int32, (N_EXPERTS, LANES), 0)
    off_row = jnp.sum(jnp.where(diag, off, 0.0), 0, keepdims=True)
    cnt_row = jnp.sum(jnp.where(diag, cnt, 0.0), 0, keepdims=True)
    seg_ref[0] = jnp.concatenate([off_row, cnt_row, jnp.zeros((SUBLANES - 2, LANES), F32)], 0).astype(jnp.int32)


def _route_call(rt):
    N = rt.shape[1]
    nt = N // MOE_TM
    u = jnp.asarray(np.triu(np.ones((LANES, LANES), np.float32), 1), BF16)
    return pl.pallas_call(
        _route_kernel,
        out_shape=[jax.ShapeDtypeStruct((nt * 2 * MOE_TM,), jnp.int32),
                   jax.ShapeDtypeStruct((nt * 2 * MOE_TM,), F32),
                   jax.ShapeDtypeStruct((nt, SUBLANES, LANES), jnp.int32)],
        grid=(nt,),
        in_specs=[pl.BlockSpec((2 * N_EXPERTS, MOE_TM), lambda i: (0, i)), _const_spec(u.shape)],
        out_specs=[pl.BlockSpec((2 * MOE_TM,), lambda i: (i,)),
                   pl.BlockSpec((2 * MOE_TM,), lambda i: (i,)),
                   pl.BlockSpec((1, SUBLANES, LANES), lambda i: (i, 0, 0))],
        compiler_params=_cparams("parallel"),
        name="route",
    )(rt, u)


def _table_index(it):
    per_block = LANES // MOE_UNROLL
    return (it // per_block) * (2 * LANES) + (it % per_block) * MOE_UNROLL


def _moe_kernel(pos_ref, wts_ref, seg_ref, hp_ref, w1_ref, w3_ref, w2_ref, x1_ref, mod_ref, lng_ref, lnb_ref,
                o_ref, xs, ys, oc):
    s = pl.program_id(1)
    half = D_MODEL // 2

    @pl.when(s == 0)
    def _dispatch():
        xs[...] = jnp.zeros_like(xs)

        def body(it, carry):
            t0 = pl.multiple_of(it * (MOE_UNROLL * HP_ROWS), MOE_UNROLL * HP_ROWS)
            i0 = _table_index(it)
            for j in range(MOE_UNROLL):
                row = hp_ref[pl.ds(t0 + HP_ROWS * j, HP_ROWS), :]
                xs[pl.ds(pl.multiple_of(pos_ref[i0 + j], HP_ROWS), HP_ROWS), :] = row
                xs[pl.ds(pl.multiple_of(pos_ref[i0 + LANES + j], HP_ROWS), HP_ROWS), :] = row
            return carry

        lax.fori_loop(0, MOE_TM // MOE_UNROLL, body, 0)

    @pl.when(s < N_EXPERTS)
    def _expert():
        off = seg_ref[0, 0, s]
        nblk = (seg_ref[0, 1, s] + (MOE_RB - 1)) // MOE_RB

        def blk(i, carry):
            r0 = pl.multiple_of(off + i * MOE_RB, SUBLANES)
            parts = [_unpack_halves(xs[pl.ds(pl.multiple_of(r0 * HP_ROWS, SUBLANES) + c, MOE_RB, stride=HP_ROWS), :])
                     for c in range(HP_ROWS)]
            lo = jnp.concatenate([p[0] for p in parts], -1)
            hi = jnp.concatenate([p[1] for p in parts], -1)
            a = _dot(lo, w1_ref[0, :half, :]) + _dot(hi, w1_ref[0, half:, :])
            g = _dot(lo, w3_ref[0, :half, :]) + _dot(hi, w3_ref[0, half:, :])
            act = (a * jax.nn.sigmoid(a) * g).astype(BF16)
            y = _dot(act, w2_ref[0])
            y0 = pl.multiple_of(r0 * Y_ROWS, SUBLANES)
            for c in range(Y_ROWS):
                ys[pl.ds(y0 + c, MOE_RB, stride=Y_ROWS), :] = y[:, c * LANES:(c + 1) * LANES]
            return carry

        lax.fori_loop(0, nblk, blk, 0)

    @pl.when(s >= N_EXPERTS)
    def _combine():
        it0 = (s - N_EXPERTS) * (MOE_CH // MOE_UNROLL)

        def body(it, carry):
            t0 = pl.multiple_of(it * (MOE_UNROLL * Y_ROWS), MOE_UNROLL * Y_ROWS)
            i0 = _table_index(it0 + it)
            for j in range(MOE_UNROLL):
                p0 = pl.multiple_of(pos_ref[i0 + j] * (Y_ROWS // HP_ROWS), Y_ROWS)
                p1 = pl.multiple_of(pos_ref[i0 + LANES + j] * (Y_ROWS // HP_ROWS), Y_ROWS)
                oc[pl.ds(t0 + Y_ROWS * j, Y_ROWS), :] = (wts_ref[i0 + j] * ys[pl.ds(p0, Y_ROWS), :]
                                                         + wts_ref[i0 + LANES + j] * ys[pl.ds(p1, Y_ROWS), :])
            return carry

        lax.fori_loop(0, MOE_CH // MOE_UNROLL, body, 0)
        moe = jnp.concatenate([oc[pl.ds(c, MOE_CH, stride=Y_ROWS), :] for c in range(Y_ROWS)], -1)
        m = mod_ref[0]
        o_ref[...] = _ln(ALPHA * x1_ref[...] + m[5:6] * moe) * lng_ref[...] + lnb_ref[...]


def _moe_call(pos, wts, seg, hp, w1, w3, w2, x1, mod, lng, lnb, T):
    N = x1.shape[0]
    nt = N // MOE_TM
    flat = pl.BlockSpec((2 * MOE_TM,), lambda i, s: (i,), memory_space=pltpu.SMEM)
    wspec = lambda shp: pl.BlockSpec((1,) + shp, lambda i, s: (jnp.minimum(s, N_EXPERTS - 1), 0, 0))
    chunk = pl.BlockSpec((MOE_CH, D_MODEL),
                         lambda i, s: (i * MOE_KC + jnp.clip(s - N_EXPERTS, 0, MOE_KC - 1), 0))
    return pl.pallas_call(
        _moe_kernel,
        out_shape=jax.ShapeDtypeStruct((N, D_MODEL), F32),
        grid=(nt, N_EXPERTS + MOE_KC),
        in_specs=[flat, flat,
                  pl.BlockSpec((1, SUBLANES, LANES), lambda i, s: (i, 0, 0), memory_space=pltpu.SMEM),
                  pl.BlockSpec((MOE_TM * HP_ROWS, LANES), lambda i, s: (i, 0)),
                  wspec((D_MODEL, D_EXPERT)), wspec((D_MODEL, D_EXPERT)), wspec((D_EXPERT, D_MODEL)),
                  chunk, pl.BlockSpec((1, SUBLANES, D_MODEL), lambda i, s: ((i * MOE_TM) // T, 0, 0)),
                  _const_spec(lng.shape), _const_spec(lnb.shape)],
        out_specs=chunk,
        scratch_shapes=[pltpu.VMEM((MOE_SLOTS * HP_ROWS, LANES), jnp.uint32),
                        pltpu.VMEM((MOE_SLOTS * Y_ROWS, LANES), F32),
                        pltpu.VMEM((MOE_CH * Y_ROWS, LANES), F32)],
        compiler_params=_cparams("parallel", "arbitrary"),
        name="moe",
    )(pos, wts, seg, hp, w1, w3, w2, x1, mod, lng, lnb)


def _rope_tables(T):
    t = jnp.arange(T)
    nf = HEAD_DIM // 4
    inv = ROPE_BASE ** (-jnp.arange(nf, dtype=F32) / nf)
    ar = (t // GRID_W).astype(F32)[:, None] * inv
    ac = (t % GRID_W).astype(F32)[:, None] * inv
    cos = jnp.concatenate([jnp.cos(ar), jnp.cos(ar), jnp.cos(ac), jnp.cos(ac)], -1)
    sin = jnp.concatenate([-jnp.sin(ar), jnp.sin(ar), -jnp.sin(ac), jnp.sin(ac)], -1)
    return jnp.tile(cos, (1, LANES // HEAD_DIM)), jnp.tile(sin, (1, LANES // HEAD_DIM))


def _nat_bias_table(rpb):
    q = np.arange(GRID_W)
    kc = np.arange(GRID_W)
    cstart = np.clip(q - NAT_KC // 2, 0, GRID_W - NAT_KC)
    ok = (kc[None, :] >= cstart[:, None]) & (kc[None, :] < cstart[:, None] + NAT_KC)
    cidx = np.clip(kc[None, :] - q[:, None] + NAT_KC - 1, 0, 2 * NAT_KC - 2)
    onehot = (np.arange(2 * NAT_KC - 1)[:, None, None] == cidx[None]).astype(np.float32)
    t = jnp.einsum('hrc,cqk->hqrk', rpb.astype(F32), jnp.asarray(onehot), precision=lax.Precision.HIGHEST)
    t = jnp.where(jnp.asarray(ok)[None, :, None, :], t * LOG2E, NEG_INF)
    per_d = [t[:, :, NAT_KR - 1 - d:2 * NAT_KR - 1 - d, :].reshape(NAT_HEADS // 2, 2 * GRID_W, NAT_KR * GRID_W)
             for d in range(NAT_KR)]
    return jnp.stack(per_d, 1)


def _dup_heads(a):
    a = jnp.broadcast_to(a[..., :, None, :], a.shape[:-1] + (2, a.shape[-1]))
    return a.reshape(a.shape[:-3] + (-1,))


def _block_diag(w):
    eye = jnp.eye(RNN_BLOCKS, dtype=w.dtype)
    return jnp.einsum('dnio,nm->dnimo', w, eye).reshape(2, RNN_WIDTH, RNN_WIDTH)


def _pad_rows(a, rows=SUBLANES):
    return jnp.pad(a, ((0, rows - a.shape[0]),) + ((0, 0),) * (a.ndim - 1))


def kernel(x_prompt, x_sample, c, cache_nat_k, cache_nat_v, cache_swa_k, cache_swa_v, state_rglru, c_ctx,
           w_ada, b_ada, w_in, rg_conv_w, rg_conv_b, rg_wa, rg_ba, rg_wx, rg_bx, rg_lambda, nat_rpb,
           swa_sink, w_branch, w_out, ln_g, ln_b, w_router, router_bias, w1, w3, w2):
    B_c, T_c, _ = x_prompt.shape
    B_l, T_l, _ = x_sample.shape
    P = cache_nat_k.shape[2]

    cv = jnp.concatenate([c_ctx[None], c, jnp.zeros((ADA_ROWS - 1 - B_l, D_MODEL), F32)], 0)
    mod_all = _ada_call(cv, w_ada, b_ada).reshape(DEPTH, ADA_ROWS, 6, D_MODEL)
    mod_all = jnp.pad(mod_all, ((0, 0), (0, 0), (0, SUBLANES - 6), (0, 0)))

    sk0 = 3072
    sv0 = sk0 + SWA_KV_HEADS * HEAD_DIM
    xf0 = sv0 + SWA_KV_HEADS * HEAD_DIM
    w_in_b = w_in[:, :, :GATE_OFF].astype(BF16)
    head = lambda c0, g: w_in_b[:, :, c0 + g * HEAD_DIM:c0 + (g + 1) * HEAD_DIM]
    w_tail_ctx = jnp.concatenate([head(c0, g) for c0 in (sk0, sv0) for g in (0, 0, 1, 1)], -1)
    w_tail_lat = jnp.concatenate([w_in_b[:, :, xf0:GATE_OFF], w_tail_ctx], -1)
    w_gate = (0.5 * w_in[:, :, GATE_OFF:]).astype(BF16)
    w_branch_b = w_branch.astype(BF16)
    w_out_b = w_out.astype(BF16)
    w1_b, w3_b, w2_b = w1.astype(BF16), w3.astype(BF16), w2.astype(BF16)
    wr_t = w_router.T.astype(BF16)
    wa_bd = jnp.stack([_block_diag(0.5 * rg_wa[l]) for l in range(DEPTH)]).astype(BF16)
    wx_bd = jnp.stack([_block_diag(0.5 * rg_wx[l]) for l in range(DEPTH)]).astype(BF16)

    plan_ctx = ((0, 0, 512, F32, 1.0, False, False, None), (0, 512, 512, F32, 1.0, False, False, None),
                (0, 1024, 512, BF16, Q_SCALE, False, False, None), (0, 1536, 512, F32, 1.0, False, False, 0),
                (0, 2048, 512, F32, 1.0, False, False, 1), (0, 2560, 512, BF16, Q_SCALE, False, False, None),
                (0, sk0, 128, F32, 1.0, False, False, 2), (0, sv0, 128, F32, 1.0, False, False, 3),
                (0, xf0, 512, BF16, 1.0, False, True, None),
                (1, 0, 256, BF16, 1.0, False, False, None), (1, 256, 256, BF16, 1.0, False, False, None))
    plan_lat = ((0, 0, 512, F32, 1.0, False, False, None), (0, 512, 512, F32, 1.0, False, False, None),
                (0, 1024, 512, BF16, Q_SCALE, False, False, None), (0, 1536, 512, BF16, 1.0, False, False, None),
                (0, 2048, 512, BF16, 1.0, False, False, None), (0, 2560, 512, BF16, Q_SCALE, True, False, None),
                (1, 0, 512, BF16, 1.0, False, True, None),
                (1, 512, 256, BF16, 1.0, True, False, None), (1, 768, 256, BF16, 1.0, False, False, None))

    rope_tabs = _rope_tables(T_l)
    ck_nat = cache_nat_k.reshape(B_l, DEPTH, P, W512).astype(BF16)
    cv_nat = cache_nat_v.reshape(B_l, DEPTH, P, W512).astype(BF16)
    ckd_swa = _dup_heads(cache_swa_k).astype(BF16)
    cvd_swa = _dup_heads(cache_swa_v).astype(BF16)
    state8 = jnp.pad(state_rglru, ((0, 0), (0, 0), (0, SUBLANES - 2), (0, 0)))
    zero_state = jnp.zeros((B_c, SUBLANES, RNN_WIDTH), F32)

    def layer(x, l, ctx_pass, caches=None):
        B, T = (B_c, T_c) if ctx_pass else (B_l, T_l)
        TM = min(T, TOKEN_TM)
        modb = jnp.broadcast_to(mod_all[l, 0:1], (B, SUBLANES, D_MODEL)) if ctx_pass else mod_all[l, 1:1 + B_l]
        if ctx_pass:
            xa, ga, nq, nk, nv, sq, sk, sv, xf, skd, svd = _pre_call(
                x, modb, w_in_b, GATE_OFF, w_tail_ctx, l, plan_ctx, B, T, TM, None, caches)
        else:
            xa, ga, nq, nk, nv, sq, xf, skd, svd = _pre_call(
                x, modb, w_in_b, sk0, w_tail_lat, l, plan_lat, B, T, TM, rope_tabs)
        r3 = lambda a: a.reshape(B, T, a.shape[-1])
        h0 = zero_state if ctx_pass else state8[:, l]
        ya, hfin = _rglru_call(
            r3(xa), r3(ga), _pad_rows(rg_conv_w[l]), rg_conv_b[l][None], wa_bd[l], wx_bd[l],
            _pad_rows(0.5 * rg_ba[l]), _pad_rows(0.5 * rg_bx[l]), _pad_rows(rg_lambda[l]), h0)
        if ctx_pass:
            yb, yc = _attn_ctx_call(swa_sink[l], r3(nq), nk, nv, r3(sq), r3(skd), r3(svd), l)
        else:
            yb = _nat_lat_call(r3(nq), r3(nk), r3(nv), ck_nat, cv_nat, _nat_bias_table(nat_rpb[l]), l)
            yc = _swa_lat_call(swa_sink[l], r3(sq), r3(skd), r3(svd), ckd_swa, cvd_swa, l)
        yd = _fourier_call(xf, T)
        f2 = lambda a: a.reshape(B * T, a.shape[-1])
        x1, hp, rt = _merge_call(
            x, modb, (f2(ya), f2(yb), f2(yc), f2(yd)), w_gate[l], w_branch_b[l], w_out_b[l],
            ln_g[l, 0][None], ln_b[l, 0][None], wr_t,
            jnp.broadcast_to(router_bias[:, None], (N_EXPERTS, min(T, SUB_TM))), T, min(T, SUB_TM))
        pos, wts, seg = _route_call(rt)
        x2 = _moe_call(pos, wts, seg, hp, w1_b[l], w3_b[l], w2_b[l], x1, modb,
                       ln_g[l, 1][None], ln_b[l, 1][None], T)
        new = ((nk, nv, sk, sv), hfin[:, :2]) if ctx_pass else None
        return x2, new

    y = x_prompt.reshape(B_c * T_c, D_MODEL)
    caches, states = None, []
    for l in range(DEPTH):
        y, (caches, st) = layer(y, l, True, caches)
        states.append(st)
    y_prompt = y.reshape(B_c, T_c, D_MODEL)
    new_nat_k = caches[0].reshape(B_c, DEPTH, T_c, NAT_HEADS, HEAD_DIM)
    new_nat_v = caches[1].reshape(B_c, DEPTH, T_c, NAT_HEADS, HEAD_DIM)
    new_swa_k = caches[2].reshape(B_c, DEPTH, T_c, SWA_KV_HEADS, HEAD_DIM)
    new_swa_v = caches[3].reshape(B_c, DEPTH, T_c, SWA_KV_HEADS, HEAD_DIM)
    new_state = jnp.stack(states, 1)

    y = x_sample.reshape(B_l * T_l, D_MODEL)
    for l in range(DEPTH):
        y, _ = layer(y, l, False)
    y_sample = y.reshape(B_l, T_l, D_MODEL)
    return (y_prompt, y_sample, new_nat_k, new_nat_v, new_swa_k, new_swa_v, new_state)
```

```python
import functools
import math

import jax
import jax.numpy as jnp
import numpy as np
from jax import lax
from jax.experimental import pallas as pl
from jax.experimental.pallas import tpu as pltpu

F32 = jnp.float32
BF16 = jnp.bfloat16

D_MODEL = 1024
DEPTH = 4
GRID_W = 64
HEAD_DIM = 64
ATTN_SCALE = HEAD_DIM ** -0.5
LOG2E = math.log2(math.e)
Q_SCALE = ATTN_SCALE * LOG2E
NEG_INF = -1e30
LN_EPS = 1e-5
ALPHA = (2 * DEPTH) ** 0.25
ROPE_BASE = 10000.0
RNN_WIDTH = 512
RNN_BLOCKS = 8
RNN_BLOCK = RNN_WIDTH // RNN_BLOCKS
CONV_W = 4
RGLRU_C = 8.0
NAT_HEADS = 8
NAT_KR = 8
NAT_KC = 16
SWA_HEADS = 8
SWA_KV_HEADS = 2
SWA_WINDOW = 128
SWA_BLOCK = 128
SWA_SPAN = SWA_BLOCK + 2 * SWA_WINDOW
FNET_GROUPS = 4
FNET_WIDTH = 512
FNET_GC = FNET_WIDTH // FNET_GROUPS
N_BRANCH = 4
N_EXPERTS = 16
N_EXPERT_GROUPS = 4
EXPERTS_PER_GROUP = N_EXPERTS // N_EXPERT_GROUPS
D_EXPERT = 512
W512 = 512
GATE_OFF = 3840

LANES = 128
SUBLANES = 8
VMEM_LIMIT = 56 * 1024 * 1024
TOKEN_TM = 1024
SUB_TM = 512


def _cparams(*sem):
    return pltpu.CompilerParams(dimension_semantics=sem, vmem_limit_bytes=VMEM_LIMIT)


def _const_spec(shape):
    nd = len(shape)
    return pl.BlockSpec(shape, lambda *_: (0,) * nd, pipeline_mode=pl.Buffered(1))


def _layer_spec(shape, l):
    nd = len(shape)
    return pl.BlockSpec((1,) + tuple(shape[1:]), lambda *_: (l,) + (0,) * (nd - 1), pipeline_mode=pl.Buffered(1))


def _ln(x):
    mu = jnp.mean(x, -1, keepdims=True)
    xc = x - mu
    var = jnp.mean(xc * xc, -1, keepdims=True)
    return xc * lax.rsqrt(var + LN_EPS)


def _dot(a, b):
    return jnp.dot(a, b, preferred_element_type=F32)


def _dot_nt(a, b):
    return lax.dot_general(a, b, (((1,), (1,)), ((), ())), preferred_element_type=F32)


ADA_ROWS = 16
ADA_TN = 1024


def _ada_kernel(c_ref, w_ref, b_ref, o_ref):
    cv = c_ref[...]
    s = (cv * jax.nn.sigmoid(cv)).astype(BF16)
    o_ref[0] = _dot(s, w_ref[0].astype(BF16)) + b_ref[0]


def _ada_call(cv, w_ada, b_ada):
    n = w_ada.shape[-1]
    return pl.pallas_call(
        _ada_kernel,
        out_shape=jax.ShapeDtypeStruct((DEPTH, ADA_ROWS, n), F32),
        grid=(DEPTH, n // ADA_TN),
        in_specs=[
            pl.BlockSpec((ADA_ROWS, D_MODEL), lambda l, j: (0, 0)),
            pl.BlockSpec((1, D_MODEL, ADA_TN), lambda l, j: (l, 0, j)),
            pl.BlockSpec((1, 1, ADA_TN), lambda l, j: (l, 0, j)),
        ],
        out_specs=pl.BlockSpec((1, ADA_ROWS, ADA_TN), lambda l, j: (l, 0, j)),
        compiler_params=_cparams("parallel", "parallel"),
        name="ada",
    )(cv, w_ada, b_ada.reshape(DEPTH, 1, n))


def _rope(u, cos, sin):
    lane = lax.broadcasted_iota(jnp.int32, cos.shape, 1)
    first = (lane & 31) < 16
    outs = []
    for j in range(u.shape[1] // LANES):
        s = u[:, j * LANES:(j + 1) * LANES]
        partner = jnp.where(first, pltpu.roll(s, LANES - 16, 1), pltpu.roll(s, 16, 1))
        outs.append(s * cos + partner * sin)
    return outs[0] if len(outs) == 1 else jnp.concatenate(outs, -1)


def _pre_kernel(*refs, plan, rope, n1, nsub, n_alias):
    x_ref, mod_ref, wm_ref, wt_ref = refs[:4]
    refs = refs[4:]
    if rope:
        cos_ref, sin_ref = refs[:2]
        refs = refs[2:]
    refs = refs[n_alias:]
    outs = refs[:len(plan)]
    m = mod_ref[0]
    sub = x_ref.shape[0] // nsub
    hs = [(_ln(x_ref[k * sub:(k + 1) * sub, :]) * (1.0 + m[1:2]) + m[0:1]).astype(BF16) for k in range(nsub)]
    for (src, off, width, _, scale, do_rope, fourier, slot), o_ref in zip(plan, outs):
        w_ref = wt_ref if src else wm_ref
        for k in range(nsub):
            rows = slice(k * sub, (k + 1) * sub)
            u = _dot(hs[k], w_ref[0, :, off:off + width])
            if do_rope:
                u = _rope(u, cos_ref[rows, :], sin_ref[rows, :])
            if scale != 1.0:
                u = u * scale
            if fourier and n1 > 1:
                stage = refs[len(plan)]
                for c in range(width // LANES):
                    stage[k, c] = u[:, c * LANES:(c + 1) * LANES]
                r = sub // n1
                for t1 in range(n1):
                    for c in range(width // LANES):
                        lo = t1 * width + c * LANES
                        o_ref[0, k * r:(k + 1) * r, lo:lo + LANES] = (
                            stage[k, c, pl.ds(t1, r, stride=n1), :].astype(o_ref.dtype))
            elif fourier:
                o_ref[0, rows, :] = u.astype(o_ref.dtype)
            elif slot is not None:
                o_ref[0, 0, rows, :] = u.astype(o_ref.dtype)
            else:
                o_ref[rows, :] = u.astype(o_ref.dtype)


def _pre_call(x, mod, w_main, n_main, w_tail, l, plan, B, T, TM, rope_tabs, caches=None):
    N = x.shape[0]
    tpb = T // TM
    nsub = max(1, TM // SUB_TM)
    n1 = T // FN_N2
    rope = rope_tabs is not None
    n_tail = w_tail.shape[-1]
    in_specs = [
        pl.BlockSpec((TM, D_MODEL), lambda i: (i, 0)),
        pl.BlockSpec((1, SUBLANES, D_MODEL), lambda i: (i // tpb, 0, 0)),
        pl.BlockSpec((1, D_MODEL, n_main), lambda i: (l, 0, 0), pipeline_mode=pl.Buffered(1)),
        pl.BlockSpec((1, D_MODEL, n_tail), lambda i: (l, 0, 0), pipeline_mode=pl.Buffered(1)),
    ]
    args = [x, mod, w_main, w_tail]
    if rope:
        in_specs += [pl.BlockSpec((TM, LANES), lambda i: (i % tpb, 0))] * 2
        args += list(rope_tabs)
    out_shape, out_specs, scratch, aliases = [], [], [], {}
    for k, p in enumerate(plan):
        if p[6]:
            out_shape.append(jax.ShapeDtypeStruct((B, FN_N2, n1 * p[2]), p[3]))
            out_specs.append(pl.BlockSpec((1, TM // n1, n1 * p[2]), lambda i: (i // tpb, i % tpb, 0)))
            if n1 > 1:
                scratch.append(pltpu.VMEM((nsub, p[2] // LANES, TM // nsub, LANES), F32))
        elif p[7] is not None:
            out_shape.append(jax.ShapeDtypeStruct((B, DEPTH, T, p[2]), p[3]))
            out_specs.append(pl.BlockSpec((1, 1, TM, p[2]), lambda i: (i // tpb, l, i % tpb, 0)))
            if caches is not None:
                aliases[len(args)] = k
                in_specs.append(pl.BlockSpec(memory_space=pl.ANY))
                args.append(caches[p[7]])
        else:
            out_shape.append(jax.ShapeDtypeStruct((N, p[2]), p[3]))
            out_specs.append(pl.BlockSpec((TM, p[2]), lambda i: (i, 0)))
    return pl.pallas_call(
        functools.partial(_pre_kernel, plan=plan, rope=rope, n1=n1, nsub=nsub, n_alias=len(aliases)),
        out_shape=out_shape,
        grid=(N // TM,),
        in_specs=in_specs,
        out_specs=out_specs,
        scratch_shapes=scratch,
        input_output_aliases=aliases,
        compiler_params=_cparams("parallel"),
        name="pre",
    )(*args)


RG_CW = 256
RG_TCH = 256


def _rglru_kernel(xa_ref, ga_ref, cw_ref, cb_ref, wa_ref, wx_ref, ba_ref, bx_ref, lam_ref, h0_ref,
                  y_ref, hfin_ref, xpad, a_f, u_f, a_b, u_b, *, T):
    cw_ = RG_CW
    zeros8 = jnp.zeros((SUBLANES, cw_), F32)
    xpad[0:SUBLANES, :] = zeros8
    xpad[T + SUBLANES:T + 2 * SUBLANES, :] = zeros8
    xpad[SUBLANES:T + SUBLANES, :] = xa_ref[0]
    cw = cw_ref[...]
    cb = cb_ref[...]
    lam = lam_ref[...]
    sp = jnp.maximum(-lam, 0.0) + jnp.log1p(jnp.exp(-jnp.abs(lam)))
    sp4 = (0.5 * RGLRU_C) * sp
    nsp4_log2e = sp4 * (-math.log2(math.e))
    ba = ba_ref[...]
    bx = bx_ref[...]
    h0 = h0_ref[0]

    def chunk(c, carry):
        base = pl.multiple_of(c * RG_TCH, RG_TCH)
        xw = xpad[pl.ds(base, RG_TCH + 2 * SUBLANES), :]
        xc = cb
        for i in range(CONV_W):
            xc = xc + cw[i:i + 1] * xw[SUBLANES - 1 + i:SUBLANES - 1 + i + RG_TCH]
        xcb = xc.astype(BF16)
        half_xc = 0.5 * xc
        for d, (a_s, u_s) in enumerate(((a_f, u_f), (a_b, u_b))):
            r2 = jnp.tanh(_dot(xcb, wa_ref[d]) + ba[d:d + 1]) + 1.0
            i2 = jnp.tanh(_dot(xcb, wx_ref[d]) + bx[d:d + 1]) + 1.0
            a = jnp.exp2(r2 * nsp4_log2e[d:d + 1])
            v = jnp.tanh(r2 * sp4[d:d + 1]) * (1.0 + a * a)
            gain = jnp.where(v > 0.0, v * lax.rsqrt(v), 0.0)
            a_s[pl.ds(base, RG_TCH), :] = a
            u_s[pl.ds(base, RG_TCH), :] = gain * (i2 * half_xc)
        return carry

    lax.fori_loop(0, T // RG_TCH, chunk, 0)

    row = lax.broadcasted_iota(jnp.int32, (SUBLANES, cw_), 0)

    def block_scan(a, u, reverse):
        for dd in (1, 2, 4):
            sh = SUBLANES - dd if reverse else dd
            a_n = pltpu.roll(a, sh, 0)
            u_n = pltpu.roll(u, sh, 0)
            ok = (row < SUBLANES - dd) if reverse else (row >= dd)
            u = jnp.where(ok, u + a * u_n, u)
            a = jnp.where(ok, a * a_n, a)
        return a, u

    nblk = T // SUBLANES

    def scan(i, carry):
        c_f, c_b = carry
        lo_f = pl.multiple_of(i * SUBLANES, SUBLANES)
        lo_b = pl.multiple_of((nblk - 1 - i) * SUBLANES, SUBLANES)
        a, u = block_scan(a_f[pl.ds(lo_f, SUBLANES), :], u_f[pl.ds(lo_f, SUBLANES), :], False)
        h_f = u + a * c_f
        u_f[pl.ds(lo_f, SUBLANES), :] = h_f
        a, u = block_scan(a_b[pl.ds(lo_b, SUBLANES), :], u_b[pl.ds(lo_b, SUBLANES), :], True)
        h_b = u + a * c_b
        u_b[pl.ds(lo_b, SUBLANES), :] = h_b
        return (jnp.broadcast_to(h_f[SUBLANES - 1:SUBLANES, :], (SUBLANES, cw_)),
                jnp.broadcast_to(h_b[0:1, :], (SUBLANES, cw_)))

    c_f, c_b = lax.fori_loop(
        0, nblk, scan,
        (jnp.broadcast_to(h0[0:1, :], (SUBLANES, cw_)), jnp.broadcast_to(h0[1:2, :], (SUBLANES, cw_))),
        unroll=2)
    hfin_ref[0] = jnp.where(row == 0, c_f, jnp.where(row == 1, c_b, 0.0))

    def emit(c, carry):
        base = pl.multiple_of(c * RG_TCH, RG_TCH)
        h = u_f[pl.ds(base, RG_TCH), :] + u_b[pl.ds(base, RG_TCH), :]
        y_ref[0, pl.ds(base, RG_TCH), :] = (h * jax.nn.gelu(ga_ref[0, pl.ds(base, RG_TCH), :])).astype(y_ref.dtype)
        return carry

    lax.fori_loop(0, T // RG_TCH, emit, 0)


def _rglru_call(xa, ga, cw, cb, wa, wx, ba, bx, lam, h0):
    B, T, _ = xa.shape
    nj = RNN_WIDTH // RG_CW
    seq = pl.BlockSpec((1, T, RG_CW), lambda b, j: (b, 0, j))
    vec8 = pl.BlockSpec((SUBLANES, RG_CW), lambda b, j: (0, j))
    wsp = pl.BlockSpec((2, RG_CW, RG_CW), lambda b, j: (0, j, j))
    st = pl.BlockSpec((1, SUBLANES, RG_CW), lambda b, j: (b, 0, j))
    return pl.pallas_call(
        functools.partial(_rglru_kernel, T=T),
        out_shape=[jax.ShapeDtypeStruct((B, T, RNN_WIDTH), BF16),
                   jax.ShapeDtypeStruct((B, SUBLANES, RNN_WIDTH), F32)],
        grid=(B, nj),
        in_specs=[seq, seq, vec8, pl.BlockSpec((1, RG_CW), lambda b, j: (0, j)),
                  wsp, wsp, vec8, vec8, vec8, st],
        out_specs=[seq, st],
        scratch_shapes=[pltpu.VMEM((T + 2 * SUBLANES, RG_CW), F32)]
        + [pltpu.VMEM((T, RG_CW), F32)] * 4,
        compiler_params=_cparams("parallel", "parallel"),
        name="rglru",
    )(xa, ga, cw, cb, wa, wx, ba, bx, lam, h0)


def _attend_slabs(jobs):
    M = jobs[0][0].shape[0]
    lo = lax.broadcasted_iota(jnp.int32, (M, LANES), 1) < HEAD_DIM
    scores = []
    for q2, srcs, _ in jobs:
        zero = jnp.zeros_like(q2)
        qs = jnp.concatenate([jnp.where(lo, q2, zero), jnp.where(lo, zero, q2)], 0)
        ss = []
        for k, _, bias in srcs:
            s = _dot_nt(qs, k)
            ss.append(s if bias is None else s + bias)
        scores.append(ss)
    maxima = []
    for (_, _, sink_col), ss in zip(jobs, scores):
        m = jnp.max(ss[0], -1, keepdims=True)
        for s in ss[1:]:
            m = jnp.maximum(m, jnp.max(s, -1, keepdims=True))
        maxima.append(m if sink_col is None else jnp.maximum(m, sink_col))
    outs = []
    for (_, srcs, sink_col), ss, m in zip(jobs, scores, maxima):
        den = None
        o = None
        for s, (_, v, _) in zip(ss, srcs):
            e = jnp.exp2(s - m)
            d_ = jnp.sum(e, -1, keepdims=True)
            o_ = _dot(e.astype(BF16), v)
            den = d_ if den is None else den + d_
            o = o_ if o is None else o + o_
        if sink_col is not None:
            den = den + jnp.exp2(sink_col - m)
        o = o / den
        outs.append(jnp.where(lo, o[:M], o[M:]))
    return outs


def _sink_col(sink_ref, j, M):
    r = lax.broadcasted_iota(jnp.int32, (2 * M, 1), 0)
    return jnp.where(r < M, sink_ref[2 * j], sink_ref[2 * j + 1]) * LOG2E


def _slab(j):
    return slice(j * LANES, (j + 1) * LANES)


def _attn_ctx_kernel(sink_ref, nq_ref, nk_ref, nv_ref, sq_ref, skd_ref, svd_ref, yb_ref, yc_ref):
    M = nq_ref.shape[1]
    nslab = W512 // LANES
    jobs = []
    for j in range(nslab):
        k2 = nk_ref[0, 0, :, _slab(j)].astype(BF16)
        v2 = nv_ref[0, 0, :, _slab(j)].astype(BF16)
        jobs.append((nq_ref[0, :, _slab(j)], [(k2, v2, None)], None))
    for j in range(nslab):
        g = j // 2
        jobs.append((sq_ref[0, :, _slab(j)], [(skd_ref[0, :, _slab(g)], svd_ref[0, :, _slab(g)], None)],
                     _sink_col(sink_ref, j, M)))
    outs = _attend_slabs(jobs)
    for j in range(nslab):
        yb_ref[0, :, _slab(j)] = outs[j].astype(yb_ref.dtype)
        yc_ref[0, :, _slab(j)] = outs[nslab + j].astype(yc_ref.dtype)


def _attn_ctx_call(sink, nq, nk, nv, sq, skd, svd, l):
    B, T, _ = nq.shape
    s512 = pl.BlockSpec((1, T, W512), lambda b: (b, 0, 0))
    s256 = pl.BlockSpec((1, T, 2 * LANES), lambda b: (b, 0, 0))
    cache = pl.BlockSpec((1, 1, T, W512), lambda b: (b, l, 0, 0))
    return pl.pallas_call(
        _attn_ctx_kernel,
        out_shape=[jax.ShapeDtypeStruct((B, T, W512), BF16)] * 2,
        grid=(B,),
        in_specs=[pl.BlockSpec(memory_space=pltpu.SMEM), s512, cache, cache, s512, s256, s256],
        out_specs=[s512, s512],
        compiler_params=_cparams("parallel"),
        name="attn_ctx",
    )(sink, nq, nk, nv, sq, skd, svd)


NAT_NLOC = NAT_KR * GRID_W


def _nat_lat_kernel(q_ref, k_ref, v_ref, ck_ref, cv_ref, bias_ref, y_ref, *, rows):
    r = pl.program_id(1)
    rstart = jnp.clip(r - NAT_KR // 2, 0, rows - NAT_KR)
    d = r - rstart
    kbase = pl.multiple_of(rstart * GRID_W, GRID_W)
    jobs = []
    for j in range(W512 // LANES):
        k2 = k_ref[0, pl.ds(kbase, NAT_NLOC), _slab(j)]
        v2 = v_ref[0, pl.ds(kbase, NAT_NLOC), _slab(j)]
        jobs.append((q_ref[0, :, _slab(j)],
                     [(k2, v2, bias_ref[j, d]), (ck_ref[0, 0, :, _slab(j)], cv_ref[0, 0, :, _slab(j)], None)], None))
    for j, o in enumerate(_attend_slabs(jobs)):
        y_ref[0, :, _slab(j)] = o.astype(y_ref.dtype)


def _nat_lat_call(q, k, v, ck, cv, bias, l):
    B, T, _ = q.shape
    rows = T // GRID_W
    P = ck.shape[2]
    qs = pl.BlockSpec((1, GRID_W, W512), lambda b, r: (b, r, 0))
    full = pl.BlockSpec((1, T, W512), lambda b, r: (b, 0, 0))
    cs = pl.BlockSpec((1, 1, P, W512), lambda b, r: (b, l, 0, 0))
    return pl.pallas_call(
        functools.partial(_nat_lat_kernel, rows=rows),
        out_shape=jax.ShapeDtypeStruct((B, T, W512), BF16),
        grid=(B, rows),
        in_specs=[qs, full, full, cs, cs, _const_spec(bias.shape)],
        out_specs=qs,
        compiler_params=_cparams("parallel", "arbitrary"),
        name="nat_lat",
    )(q, k, v, ck, cv, bias)


def _swa_band_masks():
    i = np.arange(2 * SWA_BLOCK)[:, None] % SWA_BLOCK
    j = np.arange(SWA_SPAN)[None, :]
    tabs = []
    for span_start in (0, -SWA_WINDOW, -2 * SWA_WINDOW):
        dist = span_start + j - i
        tabs.append(np.where(np.abs(dist) <= SWA_WINDOW, 0.0, NEG_INF))
    return jnp.asarray(np.stack(tabs), F32)


def _swa_lat_kernel(sink_ref, q_ref, kd_ref, vd_ref, ckd_ref, cvd_ref, band_ref, y_ref, *, T):
    blk = pl.program_id(1)
    M = SWA_BLOCK
    start = blk * SWA_BLOCK
    ks = pl.multiple_of(jnp.clip(start - SWA_WINDOW, 0, T - SWA_SPAN), SWA_BLOCK)
    bias = band_ref[jnp.where(blk == 0, 0, jnp.where(blk == T // SWA_BLOCK - 1, 2, 1))]
    jobs = []
    for j in range(W512 // LANES):
        g = j // 2
        jobs.append((q_ref[0, :, _slab(j)],
                     [(kd_ref[0, pl.ds(ks, SWA_SPAN), _slab(g)], vd_ref[0, pl.ds(ks, SWA_SPAN), _slab(g)], bias),
                      (ckd_ref[0, 0, :, _slab(g)], cvd_ref[0, 0, :, _slab(g)], None)],
                     _sink_col(sink_ref, j, M)))
    for j, o in enumerate(_attend_slabs(jobs)):
        y_ref[0, :, _slab(j)] = o.astype(y_ref.dtype)


def _swa_lat_call(sink, q, kd, vd, ckd, cvd, l):
    B, T, _ = q.shape
    P = ckd.shape[2]
    qs = pl.BlockSpec((1, SWA_BLOCK, W512), lambda b, i: (b, i, 0))
    full = pl.BlockSpec((1, T, 2 * LANES), lambda b, i: (b, 0, 0))
    cs = pl.BlockSpec((1, 1, P, 2 * LANES), lambda b, i: (b, l, 0, 0))
    assert T // SWA_BLOCK >= 3
    band = _swa_band_masks()
    return pl.pallas_call(
        functools.partial(_swa_lat_kernel, T=T),
        out_shape=jax.ShapeDtypeStruct((B, T, W512), BF16),
        grid=(B, T // SWA_BLOCK),
        in_specs=[pl.BlockSpec(memory_space=pltpu.SMEM), qs, full, full, cs, cs, _const_spec(band.shape)],
        out_specs=qs,
        compiler_params=_cparams("parallel", "arbitrary"),
        name="swa_lat",
    )(sink, q, kd, vd, ckd, cvd, band)


FN_N2 = 256
FN_GP = 2


def _fourier_kernel(x_ref, cc_ref, m_ref, y_ref, *scratch, n1, coef):
    gw = FN_GP * FNET_GC
    cc = cc_ref[...]
    for gp in range(FNET_GROUPS // FN_GP):
        parts = []
        for t1 in range(n1):
            ws = []
            for g in range(FN_GP):
                lo = t1 * FNET_WIDTH + (gp * FN_GP + g) * FNET_GC
                ws.append(_dot(x_ref[0, :, lo:lo + FNET_GC], cc))
            wr = jnp.concatenate([w[:, :FNET_GC] for w in ws], -1)
            wi = jnp.concatenate([w[:, FNET_GC:] for w in ws], -1)
            v = jnp.concatenate([wr, wi], 0).astype(BF16)
            b = _dot(m_ref[t1], v)
            if n1 == 1:
                y_ref[0, :, gp * gw:(gp + 1) * gw] = b.astype(y_ref.dtype)
            else:
                scratch[0][t1] = b
        if n1 > 1:
            bs = scratch[0]
            for k1 in range(n1):
                acc = None
                for t1 in range(n1):
                    c, s = coef[k1][t1]
                    for cf, lo in ((c, 0), (s, FN_N2)):
                        if cf == 0.0:
                            continue
                        blk = bs[t1, lo:lo + FN_N2, :]
                        term = blk if cf == 1.0 else (-blk if cf == -1.0 else cf * blk)
                        acc = term if acc is None else acc + term
                y_ref[0, k1 * FN_N2:(k1 + 1) * FN_N2, gp * gw:(gp + 1) * gw] = acc.astype(y_ref.dtype)


def _dft_consts(T):
    n1 = T // FN_N2
    j = np.arange(FNET_GC)
    ang = 2 * np.pi * np.outer(j, j) / FNET_GC
    sc = 1.0 / math.sqrt(FNET_GC)
    cc = np.concatenate([np.cos(ang) * sc, -np.sin(ang) * sc], 1)
    k2 = np.arange(FN_N2)[:, None]
    t2 = np.arange(FN_N2)[None, :]
    st = 1.0 / math.sqrt(T)
    mats = []
    for t1 in range(n1):
        th = 2 * np.pi * ((k2 * (n1 * t2 + t1)) % T) / T
        c, s = np.cos(th) * st, np.sin(th) * st
        top = np.concatenate([c, s], 1)
        mats.append(top if n1 == 1 else np.concatenate([top, np.concatenate([-s, c], 1)], 0))
    coef = []
    for k1 in range(n1):
        rowc = []
        for t1 in range(n1):
            q = (k1 * t1) % n1
            c, s = math.cos(2 * math.pi * q / n1), math.sin(2 * math.pi * q / n1)
            c = 0.0 if abs(c) < 1e-12 else (1.0 if abs(c - 1) < 1e-12 else (-1.0 if abs(c + 1) < 1e-12 else c))
            s = 0.0 if abs(s) < 1e-12 else (1.0 if abs(s - 1) < 1e-12 else (-1.0 if abs(s + 1) < 1e-12 else s))
            rowc.append((c, s))
        coef.append(tuple(rowc))
    return n1, jnp.asarray(cc, BF16), jnp.asarray(np.stack(mats), BF16), tuple(coef)


def _fourier_call(xv, T):
    B = xv.shape[0]
    n1, cc, mats, coef = _dft_consts(T)
    gw = FN_GP * FNET_GC
    scratch = [pltpu.VMEM((n1, 2 * FN_N2, gw), F32)] if n1 > 1 else []
    return pl.pallas_call(
        functools.partial(_fourier_kernel, n1=n1, coef=coef),
        out_shape=jax.ShapeDtypeStruct((B, T, FNET_WIDTH), BF16),
        grid=(B,),
        in_specs=[pl.BlockSpec((1, FN_N2, n1 * FNET_WIDTH), lambda b: (b, 0, 0)),
                  _const_spec(cc.shape), _const_spec(mats.shape)],
        out_specs=pl.BlockSpec((1, T, FNET_WIDTH), lambda b: (b, 0, 0)),
        scratch_shapes=scratch,
        compiler_params=_cparams("parallel"),
        name="fourier",
    )(xv, cc, mats)


def _route(scores, biased):
    one, zero = jnp.float32(1.0), jnp.float32(0.0)
    in2, gscore = [], []
    for g in range(N_EXPERT_GROUPS):
        vs = [biased[g * EXPERTS_PER_GROUP + j:g * EXPERTS_PER_GROUP + j + 1] for j in range(EXPERTS_PER_GROUP)]
        gs = None
        for j in range(EXPERTS_PER_GROUP):
            rank = None
            for i in range(EXPERTS_PER_GROUP):
                if i == j:
                    continue
                beats = (vs[i] >= vs[j]) if i < j else (vs[i] > vs[j])
                t = jnp.where(beats, one, zero)
                rank = t if rank is None else rank + t
            keep = jnp.where(rank < 2.0, one, zero)
            in2.append(keep)
            t = keep * vs[j]
            gs = t if gs is None else gs + t
        gscore.append(gs)
    rows, sel = [], []
    for g in range(N_EXPERT_GROUPS):
        lost = None
        for i in range(N_EXPERT_GROUPS):
            if i == g:
                continue
            beats = (gscore[i] >= gscore[g]) if i < g else (gscore[i] > gscore[g])
            t = jnp.where(beats, one, zero)
            lost = t if lost is None else lost + t
        gsel = jnp.where(lost < 1.0, one, zero)
        for j in range(EXPERTS_PER_GROUP):
            e = g * EXPERTS_PER_GROUP + j
            sel.append(gsel * in2[e])
            rows.append(sel[-1] * scores[e:e + 1])
    tot = rows[0]
    for rr in rows[1:]:
        tot = tot + rr
    return jnp.concatenate(rows, 0) / tot, jnp.concatenate(sel, 0)


HP_ROWS = D_MODEL // 2 // LANES
Y_ROWS = D_MODEL // LANES


def _pack_halves(v):
    w = v.shape[1] // 2
    lo = lax.bitcast_convert_type(v[:, :w].astype(F32), jnp.uint32)
    hi = lax.bitcast_convert_type(v[:, w:].astype(F32), jnp.uint32)
    return (lo >> 16) | hi


def _unpack_halves(p):
    lo = lax.bitcast_convert_type(p << 16, F32).astype(BF16)
    hi = lax.bitcast_convert_type(p & jnp.uint32(0xFFFF0000), F32).astype(BF16)
    return lo, hi


def _merge_kernel(x_ref, mod_ref, ya_ref, yb_ref, yc_ref, yd_ref, wg_ref, wb_ref, wo_ref,
                  lng_ref, lnb_ref, wr_ref, rb_ref, x1_ref, hp_ref, rt_ref, *, nsub):
    m = mod_ref[0]
    sub = x_ref.shape[0] // nsub
    for k in range(nsub):
        rows = slice(k * sub, (k + 1) * sub)
        x = x_ref[rows, :]
        h = (_ln(x) * (1.0 + m[1:2]) + m[0:1]).astype(BF16)
        merged = None
        for b, y_ref in enumerate((ya_ref, yb_ref, yc_ref, yd_ref)):
            gate = 0.5 * jnp.tanh(_dot(h, wg_ref[0, :, b * D_MODEL:(b + 1) * D_MODEL])) + 0.5
            term = gate * _dot(y_ref[rows, :], wb_ref[0, b])
            merged = term if merged is None else merged + term
        out = _dot(merged.astype(BF16), wo_ref[0])
        x1 = _ln(ALPHA * x + m[2:3] * out) * lng_ref[...] + lnb_ref[...]
        x1_ref[rows, :] = x1
        h2 = (_ln(x1) * (1.0 + m[4:5]) + m[3:4]).astype(BF16)
        packed = _pack_halves(h2)
        for c in range(HP_ROWS):
            hp_ref[pl.ds(k * sub * HP_ROWS + c, sub, stride=HP_ROWS), :] = packed[:, c * LANES:(c + 1) * LANES]
        logits = _dot_nt(wr_ref[...], h2)
        e = jnp.exp(logits - jnp.max(logits, 0, keepdims=True))
        scores = e / jnp.sum(e, 0, keepdims=True)
        comb, sel = _route(scores, scores + rb_ref[...])
        rt_ref[:, rows] = jnp.concatenate([comb, sel], 0)


def _merge_call(x, mod, ys, wg, wb, wo, l, lng, lnb, wr_t, rb, T, TM):
    N = x.shape[0]
    tpb = T // TM
    tok = lambda w: pl.BlockSpec((TM, w), lambda i: (i, 0))
    return pl.pallas_call(
        functools.partial(_merge_kernel, nsub=max(1, TM // SUB_TM)),
        out_shape=[jax.ShapeDtypeStruct((N, D_MODEL), F32), jax.ShapeDtypeStruct((N * HP_ROWS, LANES), jnp.uint32),
                   jax.ShapeDtypeStruct((2 * N_EXPERTS, N), F32)],
        grid=(N // TM,),
        in_specs=[tok(D_MODEL), pl.BlockSpec((1, SUBLANES, D_MODEL), lambda i: (i // tpb, 0, 0)),
                  tok(W512), tok(W512), tok(W512), tok(W512),
                  _layer_spec(wg.shape, l), _layer_spec(wb.shape, l), _layer_spec(wo.shape, l),
                  _const_spec(lng.shape), _const_spec(lnb.shape), _const_spec(wr_t.shape), _const_spec(rb.shape)],
        out_specs=[tok(D_MODEL), pl.BlockSpec((TM * HP_ROWS, LANES), lambda i: (i, 0)),
                   pl.BlockSpec((2 * N_EXPERTS, TM), lambda i: (0, i))],
        compiler_params=_cparams("parallel"),
        name="merge",
    )(x, mod, *ys, wg, wb, wo, lng, lnb, wr_t, rb)


MOE_TM = 2048
MOE_RB = 256
MOE_KC = 4
MOE_CH = MOE_TM // MOE_KC
MOE_SLOTS = 2 * MOE_TM + N_EXPERTS * SUBLANES + MOE_RB
MOE_PR = 2 * MOE_TM // LANES
MOE_UNROLL = 8


def _route_kernel(rt_ref, u_ref, pos_ref, wts_ref, seg_ref):
    comb = rt_ref[0:N_EXPERTS, :]
    sel = rt_ref[N_EXPERTS:2 * N_EXPERTS, :]
    nb = MOE_TM // LANES
    stacked = jnp.concatenate([sel[:, b * LANES:(b + 1) * LANES] for b in range(nb)], 0)
    within = _dot(stacked.astype(BF16), u_ref[...])
    tot = jnp.sum(stacked, -1, keepdims=True)
    base = jnp.zeros((N_EXPERTS, 1), F32)
    bases = []
    for b in range(nb):
        bases.append(base)
        base = base + tot[b * N_EXPERTS:(b + 1) * N_EXPERTS]
    cnt = base
    padded = jnp.floor((cnt + (SUBLANES - 1.0)) * (1.0 / SUBLANES)) * SUBLANES
    rowi = lax.broadcasted_iota(jnp.int32, (N_EXPERTS, 1), 0)
    off = jnp.zeros((N_EXPERTS, 1), F32)
    for e in range(N_EXPERTS - 1):
        off = off + jnp.where(rowi > e, padded[e:e + 1, :], 0.0)
    pos_rows, wts_rows = [], []
    for b in range(nb):
        slot = within[b * N_EXPERTS:(b + 1) * N_EXPERTS] + (bases[b] + off)
        seen = jnp.zeros((1, LANES), F32)
        acc = [jnp.zeros((1, LANES), F32) for _ in range(4)]
        for e in range(N_EXPERTS):
            s_e = sel[e:e + 1, b * LANES:(b + 1) * LANES]
            c_e = comb[e:e + 1, b * LANES:(b + 1) * LANES]
            first = jnp.where(seen == 0.0, s_e, 0.0)
            second = jnp.where(seen == 1.0, s_e, 0.0)
            acc[0] = acc[0] + first * slot[e:e + 1]
            acc[1] = acc[1] + second * slot[e:e + 1]
            acc[2] = acc[2] + first * c_e
            acc[3] = acc[3] + second * c_e
            seen = seen + s_e
        pos_rows += acc[0:2]
        wts_rows += acc[2:4]
    pos_ref[...] = (jnp.concatenate(pos_rows, 0) * float(HP_ROWS)).astype(jnp.int32).reshape(2 * MOE_TM)
    wts_ref[...] = jnp.concatenate(wts_rows, 0).reshape(2 * MOE_TM)
    lane = lax.broadcasted_iota(jnp.int32, (N_EXPERTS, LANES), 1)
    diag = lane == lax.broadcasted_iota(jnp.int32, (N_EXPERTS, LANES), 0)
    off_row = jnp.sum(jnp.where(diag, off, 0.0), 0, keepdims=True)
    cnt_row = jnp.sum(jnp.where(diag, cnt, 0.0), 0, keepdims=True)
    seg_ref[0] = jnp.concatenate([off_row, cnt_row, jnp.zeros((SUBLANES - 2, LANES), F32)], 0).astype(jnp.int32)


def _route_call(rt):
    N = rt.shape[1]
    nt = N // MOE_TM
    u = jnp.asarray(np.triu(np.ones((LANES, LANES), np.float32), 1), BF16)
    return pl.pallas_call(
        _route_kernel,
        out_shape=[jax.ShapeDtypeStruct((nt * 2 * MOE_TM,), jnp.int32),
                   jax.ShapeDtypeStruct((nt * 2 * MOE_TM,), F32),
                   jax.ShapeDtypeStruct((nt, SUBLANES, LANES), jnp.int32)],
        grid=(nt,),
        in_specs=[pl.BlockSpec((2 * N_EXPERTS, MOE_TM), lambda i: (0, i)), _const_spec(u.shape)],
        out_specs=[pl.BlockSpec((2 * MOE_TM,), lambda i: (i,)),
                   pl.BlockSpec((2 * MOE_TM,), lambda i: (i,)),
                   pl.BlockSpec((1, SUBLANES, LANES), lambda i: (i, 0, 0))],
        compiler_params=_cparams("parallel"),
        name="route",
    )(rt, u)


def _table_index(it):
    per_block = LANES // MOE_UNROLL
    return (it // per_block) * (2 * LANES) + (it % per_block) * MOE_UNROLL


def _moe_kernel(pos_ref, wts_ref, seg_ref, hp_ref, w1_ref, w3_ref, w2_ref, x1_ref, mod_ref, lng_ref, lnb_ref,
                o_ref, xs, ys, oc):
    s = pl.program_id(1)
    half = D_MODEL // 2

    @pl.when(s == 0)
    def _dispatch():
        xs[...] = jnp.zeros_like(xs)

        def body(it, carry):
            t0 = pl.multiple_of(it * (MOE_UNROLL * HP_ROWS), MOE_UNROLL * HP_ROWS)
            i0 = _table_index(it)
            for j in range(MOE_UNROLL):
                row = hp_ref[pl.ds(t0 + HP_ROWS * j, HP_ROWS), :]
                xs[pl.ds(pl.multiple_of(pos_ref[i0 + j], HP_ROWS), HP_ROWS), :] = row
                xs[pl.ds(pl.multiple_of(pos_ref[i0 + LANES + j], HP_ROWS), HP_ROWS), :] = row
            return carry

        lax.fori_loop(0, MOE_TM // MOE_UNROLL, body, 0)

    @pl.when(s < N_EXPERTS)
    def _expert():
        off = seg_ref[0, 0, s]
        cnt = seg_ref[0, 1, s]
        nblk = (cnt + (MOE_RB // 2 - 1)) // MOE_RB

        def ffn_rows(r0, nrows):
            parts = [_unpack_halves(xs[pl.ds(pl.multiple_of(r0 * HP_ROWS, SUBLANES) + c, nrows, stride=HP_ROWS), :])
                     for c in range(HP_ROWS)]
            lo = jnp.concatenate([p[0] for p in parts], -1)
            hi = jnp.concatenate([p[1] for p in parts], -1)
            a = _dot(lo, w1_ref[0, 0, :half, :]) + _dot(hi, w1_ref[0, 0, half:, :])
            g = _dot(lo, w3_ref[0, 0, :half, :]) + _dot(hi, w3_ref[0, 0, half:, :])
            act = (a * jax.nn.sigmoid(a) * g).astype(BF16)
            y = _dot(act, w2_ref[0, 0])
            y0 = pl.multiple_of(r0 * Y_ROWS, SUBLANES)
            for c in range(Y_ROWS):
                ys[pl.ds(y0 + c, nrows, stride=Y_ROWS), :] = y[:, c * LANES:(c + 1) * LANES]

        def blk(i, carry):
            ffn_rows(pl.multiple_of(off + i * MOE_RB, SUBLANES), MOE_RB)
            return carry

        lax.fori_loop(0, nblk, blk, 0)

        @pl.when(cnt > nblk * MOE_RB)
        def _tail():
            ffn_rows(pl.multiple_of(off + nblk * MOE_RB, SUBLANES), MOE_RB // 2)

    @pl.when(s >= N_EXPERTS)
    def _combine():
        it0 = (s - N_EXPERTS) * (MOE_CH // MOE_UNROLL)

        def body(it, carry):
            t0 = pl.multiple_of(it * (MOE_UNROLL * Y_ROWS), MOE_UNROLL * Y_ROWS)
            i0 = _table_index(it0 + it)
            for j in range(MOE_UNROLL):
                p0 = pl.multiple_of(pos_ref[i0 + j] * (Y_ROWS // HP_ROWS), Y_ROWS)
                p1 = pl.multiple_of(pos_ref[i0 + LANES + j] * (Y_ROWS // HP_ROWS), Y_ROWS)
                oc[pl.ds(t0 + Y_ROWS * j, Y_ROWS), :] = (wts_ref[i0 + j] * ys[pl.ds(p0, Y_ROWS), :]
                                                         + wts_ref[i0 + LANES + j] * ys[pl.ds(p1, Y_ROWS), :])
            return carry

        lax.fori_loop(0, MOE_CH // MOE_UNROLL, body, 0)
        moe = jnp.concatenate([oc[pl.ds(c, MOE_CH, stride=Y_ROWS), :] for c in range(Y_ROWS)], -1)
        m = mod_ref[0]
        o_ref[...] = _ln(ALPHA * x1_ref[...] + m[5:6] * moe) * lng_ref[...] + lnb_ref[...]


def _moe_call(pos, wts, seg, hp, w1, w3, w2, l, x1, mod, lng, lnb, T):
    N = x1.shape[0]
    nt = N // MOE_TM
    flat = pl.BlockSpec((2 * MOE_TM,), lambda i, s: (i,), memory_space=pltpu.SMEM)
    wspec = lambda shp: pl.BlockSpec((1, 1) + shp, lambda i, s: (l, jnp.minimum(s, N_EXPERTS - 1), 0, 0))
    chunk = pl.BlockSpec((MOE_CH, D_MODEL),
                         lambda i, s: (i * MOE_KC + jnp.clip(s - N_EXPERTS, 0, MOE_KC - 1), 0))
    return pl.pallas_call(
        _moe_kernel,
        out_shape=jax.ShapeDtypeStruct((N, D_MODEL), F32),
        grid=(nt, N_EXPERTS + MOE_KC),
        in_specs=[flat, flat,
                  pl.BlockSpec((1, SUBLANES, LANES), lambda i, s: (i, 0, 0), memory_space=pltpu.SMEM),
                  pl.BlockSpec((MOE_TM * HP_ROWS, LANES), lambda i, s: (i, 0)),
                  wspec((D_MODEL, D_EXPERT)), wspec((D_MODEL, D_EXPERT)), wspec((D_EXPERT, D_MODEL)),
                  chunk, pl.BlockSpec((1, SUBLANES, D_MODEL), lambda i, s: ((i * MOE_TM) // T, 0, 0)),
                  _const_spec(lng.shape), _const_spec(lnb.shape)],
        out_specs=chunk,
        scratch_shapes=[pltpu.VMEM((MOE_SLOTS * HP_ROWS, LANES), jnp.uint32),
                        pltpu.VMEM((MOE_SLOTS * Y_ROWS, LANES), F32),
                        pltpu.VMEM((MOE_CH * Y_ROWS, LANES), F32)],
        compiler_params=_cparams("parallel", "arbitrary"),
        name="moe",
    )(pos, wts, seg, hp, w1, w3, w2, x1, mod, lng, lnb)


def _rope_tables(T):
    t = jnp.arange(T)
    nf = HEAD_DIM // 4
    inv = ROPE_BASE ** (-jnp.arange(nf, dtype=F32) / nf)
    ar = (t // GRID_W).astype(F32)[:, None] * inv
    ac = (t % GRID_W).astype(F32)[:, None] * inv
    cos = jnp.concatenate([jnp.cos(ar), jnp.cos(ar), jnp.cos(ac), jnp.cos(ac)], -1)
    sin = jnp.concatenate([-jnp.sin(ar), jnp.sin(ar), -jnp.sin(ac), jnp.sin(ac)], -1)
    return jnp.tile(cos, (1, LANES // HEAD_DIM)), jnp.tile(sin, (1, LANES // HEAD_DIM))


def _nat_bias_table(rpb):
    q = np.arange(GRID_W)
    kc = np.arange(GRID_W)
    cstart = np.clip(q - NAT_KC // 2, 0, GRID_W - NAT_KC)
    ok = (kc[None, :] >= cstart[:, None]) & (kc[None, :] < cstart[:, None] + NAT_KC)
    cidx = np.clip(kc[None, :] - q[:, None] + NAT_KC - 1, 0, 2 * NAT_KC - 2)
    onehot = (np.arange(2 * NAT_KC - 1)[:, None, None] == cidx[None]).astype(np.float32)
    t = jnp.einsum('hrc,cqk->hqrk', rpb.astype(F32), jnp.asarray(onehot), precision=lax.Precision.HIGHEST)
    t = jnp.where(jnp.asarray(ok)[None, :, None, :], t * LOG2E, NEG_INF)
    per_d = [t[:, :, NAT_KR - 1 - d:2 * NAT_KR - 1 - d, :].reshape(NAT_HEADS // 2, 2 * GRID_W, NAT_KR * GRID_W)
             for d in range(NAT_KR)]
    return jnp.stack(per_d, 1)


def _dup_heads(a):
    a = jnp.broadcast_to(a[..., :, None, :], a.shape[:-1] + (2, a.shape[-1]))
    return a.reshape(a.shape[:-3] + (-1,))


def _block_diag(w):
    eye = jnp.eye(RNN_BLOCKS, dtype=w.dtype)
    return jnp.einsum('dnio,nm->dnimo', w, eye).reshape(2, RNN_WIDTH, RNN_WIDTH)


def _pad_rows(a, rows=SUBLANES):
    return jnp.pad(a, ((0, rows - a.shape[0]),) + ((0, 0),) * (a.ndim - 1))


def kernel(x_prompt, x_sample, c, cache_nat_k, cache_nat_v, cache_swa_k, cache_swa_v, state_rglru, c_ctx,
           w_ada, b_ada, w_in, rg_conv_w, rg_conv_b, rg_wa, rg_ba, rg_wx, rg_bx, rg_lambda, nat_rpb,
           swa_sink, w_branch, w_out, ln_g, ln_b, w_router, router_bias, w1, w3, w2):
    B_c, T_c, _ = x_prompt.shape
    B_l, T_l, _ = x_sample.shape
    P = cache_nat_k.shape[2]

    cv = jnp.concatenate([c_ctx[None], c, jnp.zeros((ADA_ROWS - 1 - B_l, D_MODEL), F32)], 0)
    mod_all = _ada_call(cv, w_ada, b_ada).reshape(DEPTH, ADA_ROWS, 6, D_MODEL)
    mod_all = jnp.pad(mod_all, ((0, 0), (0, 0), (0, SUBLANES - 6), (0, 0)))

    sk0 = 3072
    sv0 = sk0 + SWA_KV_HEADS * HEAD_DIM
    xf0 = sv0 + SWA_KV_HEADS * HEAD_DIM
    w_in_b = w_in.astype(BF16)
    head = lambda c0, g: w_in_b[:, :, c0 + g * HEAD_DIM:c0 + (g + 1) * HEAD_DIM]
    w_tail_ctx = jnp.concatenate([head(c0, g) for c0 in (sk0, sv0) for g in (0, 0, 1, 1)], -1)
    w_tail_lat = jnp.concatenate([w_in_b[:, :, xf0:GATE_OFF], w_tail_ctx], -1)
    w_gate = (0.5 * w_in[:, :, GATE_OFF:]).astype(BF16)
    w_branch_b = w_branch.astype(BF16)
    w_out_b = w_out.astype(BF16)
    w1_b, w3_b, w2_b = w1.astype(BF16), w3.astype(BF16), w2.astype(BF16)
    wr_t = w_router.T.astype(BF16)
    wa_bd = jnp.stack([_block_diag(0.5 * rg_wa[l]) for l in range(DEPTH)]).astype(BF16)
    wx_bd = jnp.stack([_block_diag(0.5 * rg_wx[l]) for l in range(DEPTH)]).astype(BF16)

    plan_ctx = ((0, 0, 512, F32, 1.0, False, False, None), (0, 512, 512, F32, 1.0, False, False, None),
                (0, 1024, 512, BF16, Q_SCALE, False, False, None), (0, 1536, 512, F32, 1.0, False, False, 0),
                (0, 2048, 512, F32, 1.0, False, False, 1), (0, 2560, 512, BF16, Q_SCALE, False, False, None),
                (0, sk0, 128, F32, 1.0, False, False, 2), (0, sv0, 128, F32, 1.0, False, False, 3),
                (0, xf0, 512, BF16, 1.0, False, True, None),
                (1, 0, 256, BF16, 1.0, False, False, None), (1, 256, 256, BF16, 1.0, False, False, None))
    plan_lat = ((0, 0, 512, F32, 1.0, False, False, None), (0, 512, 512, F32, 1.0, False, False, None),
                (0, 1024, 512, BF16, Q_SCALE, False, False, None), (0, 1536, 512, BF16, 1.0, False, False, None),
                (0, 2048, 512, BF16, 1.0, False, False, None), (0, 2560, 512, BF16, Q_SCALE, True, False, None),
                (1, 0, 512, BF16, 1.0, False, True, None),
                (1, 512, 256, BF16, 1.0, True, False, None), (1, 768, 256, BF16, 1.0, False, False, None))

    rope_tabs = _rope_tables(T_l)
    ck_nat = cache_nat_k.reshape(B_l, DEPTH, P, W512).astype(BF16)
    cv_nat = cache_nat_v.reshape(B_l, DEPTH, P, W512).astype(BF16)
    ckd_swa = _dup_heads(cache_swa_k).astype(BF16)
    cvd_swa = _dup_heads(cache_swa_v).astype(BF16)
    state8 = jnp.pad(state_rglru, ((0, 0), (0, 0), (0, SUBLANES - 2), (0, 0)))
    zero_state = jnp.zeros((B_c, SUBLANES, RNN_WIDTH), F32)

    def layer(x, l, ctx_pass, caches=None):
        B, T = (B_c, T_c) if ctx_pass else (B_l, T_l)
        TM = min(T, TOKEN_TM)
        modb = jnp.broadcast_to(mod_all[l, 0:1], (B, SUBLANES, D_MODEL)) if ctx_pass else mod_all[l, 1:1 + B_l]
        if ctx_pass:
            xa, ga, nq, nk, nv, sq, sk, sv, xf, skd, svd = _pre_call(
                x, modb, w_in_b, GATE_OFF, w_tail_ctx, l, plan_ctx, B, T, TM, None, caches)
        else:
            xa, ga, nq, nk, nv, sq, xf, skd, svd = _pre_call(
                x, modb, w_in_b, sk0, w_tail_lat, l, plan_lat, B, T, TM, rope_tabs)
        r3 = lambda a: a.reshape(B, T, a.shape[-1])
        h0 = zero_state if ctx_pass else state8[:, l]
        ya, hfin = _rglru_call(
            r3(xa), r3(ga), _pad_rows(rg_conv_w[l]), rg_conv_b[l][None], wa_bd[l], wx_bd[l],
            _pad_rows(0.5 * rg_ba[l]), _pad_rows(0.5 * rg_bx[l]), _pad_rows(rg_lambda[l]), h0)
        if ctx_pass:
            yb, yc = _attn_ctx_call(swa_sink[l], r3(nq), nk, nv, r3(sq), r3(skd), r3(svd), l)
        else:
            yb = _nat_lat_call(r3(nq), r3(nk), r3(nv), ck_nat, cv_nat, _nat_bias_table(nat_rpb[l]), l)
            yc = _swa_lat_call(swa_sink[l], r3(sq), r3(skd), r3(svd), ckd_swa, cvd_swa, l)
        yd = _fourier_call(xf, T)
        f2 = lambda a: a.reshape(B * T, a.shape[-1])
        x1, hp, rt = _merge_call(
            x, modb, (f2(ya), f2(yb), f2(yc), f2(yd)), w_gate, w_branch_b, w_out_b, l,
            ln_g[l, 0][None], ln_b[l, 0][None], wr_t,
            jnp.broadcast_to(router_bias[:, None], (N_EXPERTS, min(T, SUB_TM))), T, min(T, SUB_TM))
        pos, wts, seg = _route_call(rt)
        x2 = _moe_call(pos, wts, seg, hp, w1_b, w3_b, w2_b, l, x1, modb,
                       ln_g[l, 1][None], ln_b[l, 1][None], T)
        new = ((nk, nv, sk, sv), hfin[:, :2]) if ctx_pass else None
        return x2, new

    y = x_prompt.reshape(B_c * T_c, D_MODEL)
    caches, states = None, []
    for l in range(DEPTH):
        y, (caches, st) = layer(y, l, True, caches)
        states.append(st)
    y_prompt = y.reshape(B_c, T_c, D_MODEL)
    new_nat_k = caches[0].reshape(B_c, DEPTH, T_c, NAT_HEADS, HEAD_DIM)
    new_nat_v = caches[1].reshape(B_c, DEPTH, T_c, NAT_HEADS, HEAD_DIM)
    new_swa_k = caches[2].reshape(B_c, DEPTH, T_c, SWA_KV_HEADS, HEAD_DIM)
    new_swa_v = caches[3].reshape(B_c, DEPTH, T_c, SWA_KV_HEADS, HEAD_DIM)
    new_state = jnp.stack(states, 1)

    y = x_sample.reshape(B_l * T_l, D_MODEL)
    for l in range(DEPTH):
        y, _ = layer(y, l, False)
    y_sample = y.reshape(B_l, T_l, D_MODEL)
    return (y_prompt, y_sample, new_nat_k, new_nat_v, new_swa_k, new_swa_v, new_state)
```

```python
import functools
import math

import jax
import jax.numpy as jnp
import numpy as np
from jax import lax
from jax.experimental import pallas as pl
from jax.experimental.pallas import tpu as pltpu

F32 = jnp.float32
BF16 = jnp.bfloat16

D_MODEL = 1024
DEPTH = 4
GRID_W = 64
HEAD_DIM = 64
ATTN_SCALE = HEAD_DIM ** -0.5
LOG2E = math.log2(math.e)
Q_SCALE = ATTN_SCALE * LOG2E
NEG_INF = -1e30
LN_EPS = 1e-5
ALPHA = (2 * DEPTH) ** 0.25
ROPE_BASE = 10000.0
RNN_WIDTH = 512
RNN_BLOCKS = 8
RNN_BLOCK = RNN_WIDTH // RNN_BLOCKS
CONV_W = 4
RGLRU_C = 8.0
NAT_HEADS = 8
NAT_KR = 8
NAT_KC = 16
SWA_HEADS = 8
SWA_KV_HEADS = 2
SWA_WINDOW = 128
SWA_BLOCK = 128
SWA_SPAN = SWA_BLOCK + 2 * SWA_WINDOW
FNET_GROUPS = 4
FNET_WIDTH = 512
FNET_GC = FNET_WIDTH // FNET_GROUPS
N_BRANCH = 4
N_EXPERTS = 16
N_EXPERT_GROUPS = 4
EXPERTS_PER_GROUP = N_EXPERTS // N_EXPERT_GROUPS
D_EXPERT = 512
W512 = 512
GATE_OFF = 3840

LANES = 128
SUBLANES = 8
VMEM_LIMIT = 56 * 1024 * 1024
TOKEN_TM = 1024
SUB_TM = 512


def _cparams(*sem):
    return pltpu.CompilerParams(dimension_semantics=sem, vmem_limit_bytes=VMEM_LIMIT)


def _const_spec(shape):
    nd = len(shape)
    return pl.BlockSpec(shape, lambda *_: (0,) * nd, pipeline_mode=pl.Buffered(1))


def _layer_spec(shape, l):
    nd = len(shape)
    return pl.BlockSpec((1,) + tuple(shape[1:]), lambda *_: (l,) + (0,) * (nd - 1), pipeline_mode=pl.Buffered(1))


def _ln(x):
    mu = jnp.mean(x, -1, keepdims=True)
    xc = x - mu
    var = jnp.mean(xc * xc, -1, keepdims=True)
    return xc * lax.rsqrt(var + LN_EPS)


def _dot(a, b):
    return jnp.dot(a, b, preferred_element_type=F32)


def _dot_nt(a, b):
    return lax.dot_general(a, b, (((1,), (1,)), ((), ())), preferred_element_type=F32)


ADA_ROWS = 16
ADA_TN = 1024


def _ada_kernel(c_ref, w_ref, b_ref, o_ref):
    cv = c_ref[...]
    s = (cv * jax.nn.sigmoid(cv)).astype(BF16)
    o_ref[0] = _dot(s, w_ref[0].astype(BF16)) + b_ref[0]


def _ada_call(cv, w_ada, b_ada):
    n = w_ada.shape[-1]
    return pl.pallas_call(
        _ada_kernel,
        out_shape=jax.ShapeDtypeStruct((DEPTH, ADA_ROWS, n), F32),
        grid=(DEPTH, n // ADA_TN),
        in_specs=[
            pl.BlockSpec((ADA_ROWS, D_MODEL), lambda l, j: (0, 0)),
            pl.BlockSpec((1, D_MODEL, ADA_TN), lambda l, j: (l, 0, j)),
            pl.BlockSpec((1, 1, ADA_TN), lambda l, j: (l, 0, j)),
        ],
        out_specs=pl.BlockSpec((1, ADA_ROWS, ADA_TN), lambda l, j: (l, 0, j)),
        compiler_params=_cparams("parallel", "parallel"),
        name="ada",
    )(cv, w_ada, b_ada.reshape(DEPTH, 1, n))


def _rope(u, cos, sin):
    lane = lax.broadcasted_iota(jnp.int32, cos.shape, 1)
    first = (lane & 31) < 16
    outs = []
    for j in range(u.shape[1] // LANES):
        s = u[:, j * LANES:(j + 1) * LANES]
        partner = jnp.where(first, pltpu.roll(s, LANES - 16, 1), pltpu.roll(s, 16, 1))
        outs.append(s * cos + partner * sin)
    return outs[0] if len(outs) == 1 else jnp.concatenate(outs, -1)


def _pre_kernel(*refs, plan, rope, n1, nsub, n_alias):
    x_ref, mod_ref, wm_ref, wt_ref = refs[:4]
    refs = refs[4:]
    if rope:
        cos_ref, sin_ref = refs[:2]
        refs = refs[2:]
    refs = refs[n_alias:]
    outs = refs[:len(plan)]
    m = mod_ref[0]
    sub = x_ref.shape[0] // nsub
    hs = [(_ln(x_ref[k * sub:(k + 1) * sub, :]) * (1.0 + m[1:2]) + m[0:1]).astype(BF16) for k in range(nsub)]
    for (src, off, width, _, scale, do_rope, fourier, slot), o_ref in zip(plan, outs):
        w_ref = wt_ref if src else wm_ref
        for k in range(nsub):
            rows = slice(k * sub, (k + 1) * sub)
            u = _dot(hs[k], w_ref[0, :, off:off + width])
            if do_rope:
                u = _rope(u, cos_ref[rows, :], sin_ref[rows, :])
            if scale != 1.0:
                u = u * scale
            if fourier and n1 > 1:
                stage = refs[len(plan)]
                for c in range(width // LANES):
                    stage[k, c] = u[:, c * LANES:(c + 1) * LANES]
                r = sub // n1
                for t1 in range(n1):
                    for c in range(width // LANES):
                        lo = t1 * width + c * LANES
                        o_ref[0, k * r:(k + 1) * r, lo:lo + LANES] = (
                            stage[k, c, pl.ds(t1, r, stride=n1), :].astype(o_ref.dtype))
            elif fourier:
                o_ref[0, rows, :] = u.astype(o_ref.dtype)
            elif slot is not None:
                o_ref[0, 0, rows, :] = u.astype(o_ref.dtype)
            else:
                o_ref[rows, :] = u.astype(o_ref.dtype)


def _pre_call(x, mod, w_main, n_main, w_tail, l, plan, B, T, TM, rope_tabs, caches=None):
    N = x.shape[0]
    tpb = T // TM
    nsub = max(1, TM // SUB_TM)
    n1 = T // FN_N2
    rope = rope_tabs is not None
    n_tail = w_tail.shape[-1]
    in_specs = [
        pl.BlockSpec((TM, D_MODEL), lambda i: (i, 0)),
        pl.BlockSpec((1, SUBLANES, D_MODEL), lambda i: (i // tpb, 0, 0)),
        pl.BlockSpec((1, D_MODEL, n_main), lambda i: (l, 0, 0), pipeline_mode=pl.Buffered(1)),
        pl.BlockSpec((1, D_MODEL, n_tail), lambda i: (l, 0, 0), pipeline_mode=pl.Buffered(1)),
    ]
    args = [x, mod, w_main, w_tail]
    if rope:
        in_specs += [pl.BlockSpec((TM, LANES), lambda i: (i % tpb, 0))] * 2
        args += list(rope_tabs)
    out_shape, out_specs, scratch, aliases = [], [], [], {}
    for k, p in enumerate(plan):
        if p[6]:
            out_shape.append(jax.ShapeDtypeStruct((B, FN_N2, n1 * p[2]), p[3]))
            out_specs.append(pl.BlockSpec((1, TM // n1, n1 * p[2]), lambda i: (i // tpb, i % tpb, 0)))
            if n1 > 1:
                scratch.append(pltpu.VMEM((nsub, p[2] // LANES, TM // nsub, LANES), F32))
        elif p[7] is not None:
            out_shape.append(jax.ShapeDtypeStruct((B, DEPTH, T, p[2]), p[3]))
            out_specs.append(pl.BlockSpec((1, 1, TM, p[2]), lambda i: (i // tpb, l, i % tpb, 0)))
            if caches is not None:
                aliases[len(args)] = k
                in_specs.append(pl.BlockSpec(memory_space=pl.ANY))
                args.append(caches[p[7]])
        else:
            out_shape.append(jax.ShapeDtypeStruct((N, p[2]), p[3]))
            out_specs.append(pl.BlockSpec((TM, p[2]), lambda i: (i, 0)))
    return pl.pallas_call(
        functools.partial(_pre_kernel, plan=plan, rope=rope, n1=n1, nsub=nsub, n_alias=len(aliases)),
        out_shape=out_shape,
        grid=(N // TM,),
        in_specs=in_specs,
        out_specs=out_specs,
        scratch_shapes=scratch,
        input_output_aliases=aliases,
        compiler_params=_cparams("parallel"),
        name="pre",
    )(*args)


RG_CW = 256
RG_TCH = 256


def _rglru_kernel(xa_ref, ga_ref, cw_ref, cb_ref, wa_ref, wx_ref, ba_ref, bx_ref, lam_ref, h0_ref,
                  y_ref, hfin_ref, xpad, a_f, u_f, a_b, u_b, *, T):
    cw_ = RG_CW
    zeros8 = jnp.zeros((SUBLANES, cw_), F32)
    xpad[0:SUBLANES, :] = zeros8
    xpad[T + SUBLANES:T + 2 * SUBLANES, :] = zeros8
    xpad[SUBLANES:T + SUBLANES, :] = xa_ref[0]
    cw = cw_ref[...]
    cb = cb_ref[...]
    lam = lam_ref[...]
    sp = jnp.maximum(-lam, 0.0) + jnp.log1p(jnp.exp(-jnp.abs(lam)))
    sp4 = (0.5 * RGLRU_C) * sp
    nsp4_log2e = sp4 * (-math.log2(math.e))
    ba = ba_ref[...]
    bx = bx_ref[...]
    h0 = h0_ref[0]

    def chunk(c, carry):
        base = pl.multiple_of(c * RG_TCH, RG_TCH)
        xw = xpad[pl.ds(base, RG_TCH + 2 * SUBLANES), :]
        xc = cb
        for i in range(CONV_W):
            xc = xc + cw[i:i + 1] * xw[SUBLANES - 1 + i:SUBLANES - 1 + i + RG_TCH]
        xcb = xc.astype(BF16)
        half_xc = 0.5 * xc
        for d, (a_s, u_s) in enumerate(((a_f, u_f), (a_b, u_b))):
            r2 = jnp.tanh(_dot(xcb, wa_ref[d]) + ba[d:d + 1]) + 1.0
            i2 = jnp.tanh(_dot(xcb, wx_ref[d]) + bx[d:d + 1]) + 1.0
            a = jnp.exp2(r2 * nsp4_log2e[d:d + 1])
            v = jnp.tanh(r2 * sp4[d:d + 1]) * (1.0 + a * a)
            gain = jnp.where(v > 0.0, v * lax.rsqrt(v), 0.0)
            a_s[pl.ds(base, RG_TCH), :] = a
            u_s[pl.ds(base, RG_TCH), :] = gain * (i2 * half_xc)
        return carry

    lax.fori_loop(0, T // RG_TCH, chunk, 0)

    row = lax.broadcasted_iota(jnp.int32, (SUBLANES, cw_), 0)

    def block_scan(a, u, reverse):
        for dd in (1, 2, 4):
            sh = SUBLANES - dd if reverse else dd
            a_n = pltpu.roll(a, sh, 0)
            u_n = pltpu.roll(u, sh, 0)
            ok = (row < SUBLANES - dd) if reverse else (row >= dd)
            u = jnp.where(ok, u + a * u_n, u)
            a = jnp.where(ok, a * a_n, a)
        return a, u

    nblk = T // SUBLANES

    def scan(i, carry):
        c_f, c_b = carry
        lo_f = pl.multiple_of(i * SUBLANES, SUBLANES)
        lo_b = pl.multiple_of((nblk - 1 - i) * SUBLANES, SUBLANES)
        a, u = block_scan(a_f[pl.ds(lo_f, SUBLANES), :], u_f[pl.ds(lo_f, SUBLANES), :], False)
        h_f = u + a * c_f
        u_f[pl.ds(lo_f, SUBLANES), :] = h_f
        a, u = block_scan(a_b[pl.ds(lo_b, SUBLANES), :], u_b[pl.ds(lo_b, SUBLANES), :], True)
        h_b = u + a * c_b
        u_b[pl.ds(lo_b, SUBLANES), :] = h_b
        return (jnp.broadcast_to(h_f[SUBLANES - 1:SUBLANES, :], (SUBLANES, cw_)),
                jnp.broadcast_to(h_b[0:1, :], (SUBLANES, cw_)))

    c_f, c_b = lax.fori_loop(
        0, nblk, scan,
        (jnp.broadcast_to(h0[0:1, :], (SUBLANES, cw_)), jnp.broadcast_to(h0[1:2, :], (SUBLANES, cw_))),
        unroll=2)
    hfin_ref[0] = jnp.where(row == 0, c_f, jnp.where(row == 1, c_b, 0.0))

    def emit(c, carry):
        base = pl.multiple_of(c * RG_TCH, RG_TCH)
        h = u_f[pl.ds(base, RG_TCH), :] + u_b[pl.ds(base, RG_TCH), :]
        y_ref[0, pl.ds(base, RG_TCH), :] = (h * jax.nn.gelu(ga_ref[0, pl.ds(base, RG_TCH), :])).astype(y_ref.dtype)
        return carry

    lax.fori_loop(0, T // RG_TCH, emit, 0)


def _rglru_call(xa, ga, cw, cb, wa, wx, ba, bx, lam, h0):
    B, T, _ = xa.shape
    nj = RNN_WIDTH // RG_CW
    seq = pl.BlockSpec((1, T, RG_CW), lambda b, j: (b, 0, j))
    vec8 = pl.BlockSpec((SUBLANES, RG_CW), lambda b, j: (0, j))
    wsp = pl.BlockSpec((2, RG_CW, RG_CW), lambda b, j: (0, j, j))
    st = pl.BlockSpec((1, SUBLANES, RG_CW), lambda b, j: (b, 0, j))
    return pl.pallas_call(
        functools.partial(_rglru_kernel, T=T),
        out_shape=[jax.ShapeDtypeStruct((B, T, RNN_WIDTH), BF16),
                   jax.ShapeDtypeStruct((B, SUBLANES, RNN_WIDTH), F32)],
        grid=(B, nj),
        in_specs=[seq, seq, vec8, pl.BlockSpec((1, RG_CW), lambda b, j: (0, j)),
                  wsp, wsp, vec8, vec8, vec8, st],
        out_specs=[seq, st],
        scratch_shapes=[pltpu.VMEM((T + 2 * SUBLANES, RG_CW), F32)]
        + [pltpu.VMEM((T, RG_CW), F32)] * 4,
        compiler_params=_cparams("parallel", "parallel"),
        name="rglru",
    )(xa, ga, cw, cb, wa, wx, ba, bx, lam, h0)


def _attend_slabs(jobs):
    M = jobs[0][0].shape[0]
    lo = lax.broadcasted_iota(jnp.int32, (M, LANES), 1) < HEAD_DIM
    scores = []
    for q2, srcs, _ in jobs:
        zero = jnp.zeros_like(q2)
        qs = jnp.concatenate([jnp.where(lo, q2, zero), jnp.where(lo, zero, q2)], 0)
        ss = []
        for k, _, bias in srcs:
            s = _dot_nt(qs, k)
            ss.append(s if bias is None else s + bias)
        scores.append(ss)
    maxima = []
    for (_, _, sink_col), ss in zip(jobs, scores):
        m = jnp.max(ss[0], -1, keepdims=True)
        for s in ss[1:]:
            m = jnp.maximum(m, jnp.max(s, -1, keepdims=True))
        maxima.append(m if sink_col is None else jnp.maximum(m, sink_col))
    outs = []
    for (_, srcs, sink_col), ss, m in zip(jobs, scores, maxima):
        den = None
        o = None
        for s, (_, v, _) in zip(ss, srcs):
            e = jnp.exp2(s - m)
            d_ = jnp.sum(e, -1, keepdims=True)
            o_ = _dot(e.astype(BF16), v)
            den = d_ if den is None else den + d_
            o = o_ if o is None else o + o_
        if sink_col is not None:
            den = den + jnp.exp2(sink_col - m)
        o = o / den
        outs.append(jnp.where(lo, o[:M], o[M:]))
    return outs


def _sink_col(sink_ref, j, M):
    r = lax.broadcasted_iota(jnp.int32, (2 * M, 1), 0)
    return jnp.where(r < M, sink_ref[2 * j], sink_ref[2 * j + 1]) * LOG2E


def _slab(j):
    return slice(j * LANES, (j + 1) * LANES)


def _attn_ctx_kernel(sink_ref, nq_ref, nk_ref, nv_ref, sq_ref, skd_ref, svd_ref, yb_ref, yc_ref):
    M = nq_ref.shape[1]
    nslab = W512 // LANES
    jobs = []
    for j in range(nslab):
        k2 = nk_ref[0, 0, :, _slab(j)].astype(BF16)
        v2 = nv_ref[0, 0, :, _slab(j)].astype(BF16)
        jobs.append((nq_ref[0, :, _slab(j)], [(k2, v2, None)], None))
    for j in range(nslab):
        g = j // 2
        jobs.append((sq_ref[0, :, _slab(j)], [(skd_ref[0, :, _slab(g)], svd_ref[0, :, _slab(g)], None)],
                     _sink_col(sink_ref, j, M)))
    outs = _attend_slabs(jobs)
    for j in range(nslab):
        yb_ref[0, :, _slab(j)] = outs[j].astype(yb_ref.dtype)
        yc_ref[0, :, _slab(j)] = outs[nslab + j].astype(yc_ref.dtype)


def _attn_ctx_call(sink, nq, nk, nv, sq, skd, svd, l):
    B, T, _ = nq.shape
    s512 = pl.BlockSpec((1, T, W512), lambda b: (b, 0, 0))
    s256 = pl.BlockSpec((1, T, 2 * LANES), lambda b: (b, 0, 0))
    cache = pl.BlockSpec((1, 1, T, W512), lambda b: (b, l, 0, 0))
    return pl.pallas_call(
        _attn_ctx_kernel,
        out_shape=[jax.ShapeDtypeStruct((B, T, W512), BF16)] * 2,
        grid=(B,),
        in_specs=[pl.BlockSpec(memory_space=pltpu.SMEM), s512, cache, cache, s512, s256, s256],
        out_specs=[s512, s512],
        compiler_params=_cparams("parallel"),
        name="attn_ctx",
    )(sink, nq, nk, nv, sq, skd, svd)


NAT_NLOC = NAT_KR * GRID_W


def _nat_lat_kernel(q_ref, k_ref, v_ref, ck_ref, cv_ref, bias_ref, y_ref, *, rows):
    r = pl.program_id(1)
    rstart = jnp.clip(r - NAT_KR // 2, 0, rows - NAT_KR)
    d = r - rstart
    kbase = pl.multiple_of(rstart * GRID_W, GRID_W)
    jobs = []
    for j in range(W512 // LANES):
        k2 = k_ref[0, pl.ds(kbase, NAT_NLOC), _slab(j)]
        v2 = v_ref[0, pl.ds(kbase, NAT_NLOC), _slab(j)]
        jobs.append((q_ref[0, :, _slab(j)],
                     [(k2, v2, bias_ref[j, d]), (ck_ref[0, 0, :, _slab(j)], cv_ref[0, 0, :, _slab(j)], None)], None))
    for j, o in enumerate(_attend_slabs(jobs)):
        y_ref[0, :, _slab(j)] = o.astype(y_ref.dtype)


def _nat_lat_call(q, k, v, ck, cv, bias, l):
    B, T, _ = q.shape
    rows = T // GRID_W
    P = ck.shape[2]
    qs = pl.BlockSpec((1, GRID_W, W512), lambda b, r: (b, r, 0))
    full = pl.BlockSpec((1, T, W512), lambda b, r: (b, 0, 0))
    cs = pl.BlockSpec((1, 1, P, W512), lambda b, r: (b, l, 0, 0))
    return pl.pallas_call(
        functools.partial(_nat_lat_kernel, rows=rows),
        out_shape=jax.ShapeDtypeStruct((B, T, W512), BF16),
        grid=(B, rows),
        in_specs=[qs, full, full, cs, cs, _const_spec(bias.shape)],
        out_specs=qs,
        compiler_params=_cparams("parallel", "arbitrary"),
        name="nat_lat",
    )(q, k, v, ck, cv, bias)


def _swa_band_masks():
    i = np.arange(2 * SWA_BLOCK)[:, None] % SWA_BLOCK
    j = np.arange(SWA_SPAN)[None, :]
    tabs = []
    for span_start in (0, -SWA_WINDOW, -2 * SWA_WINDOW):
        dist = span_start + j - i
        tabs.append(np.where(np.abs(dist) <= SWA_WINDOW, 0.0, NEG_INF))
    return jnp.asarray(np.stack(tabs), F32)


def _swa_lat_kernel(sink_ref, q_ref, kd_ref, vd_ref, ckd_ref, cvd_ref, band_ref, y_ref, *, T):
    blk = pl.program_id(1)
    M = SWA_BLOCK
    start = blk * SWA_BLOCK
    ks = pl.multiple_of(jnp.clip(start - SWA_WINDOW, 0, T - SWA_SPAN), SWA_BLOCK)
    bias = band_ref[jnp.where(blk == 0, 0, jnp.where(blk == T // SWA_BLOCK - 1, 2, 1))]
    jobs = []
    for j in range(W512 // LANES):
        g = j // 2
        jobs.append((q_ref[0, :, _slab(j)],
                     [(kd_ref[0, pl.ds(ks, SWA_SPAN), _slab(g)], vd_ref[0, pl.ds(ks, SWA_SPAN), _slab(g)], bias),
                      (ckd_ref[0, 0, :, _slab(g)], cvd_ref[0, 0, :, _slab(g)], None)],
                     _sink_col(sink_ref, j, M)))
    for j, o in enumerate(_attend_slabs(jobs)):
        y_ref[0, :, _slab(j)] = o.astype(y_ref.dtype)


def _swa_lat_call(sink, q, kd, vd, ckd, cvd, l):
    B, T, _ = q.shape
    P = ckd.shape[2]
    qs = pl.BlockSpec((1, SWA_BLOCK, W512), lambda b, i: (b, i, 0))
    full = pl.BlockSpec((1, T, 2 * LANES), lambda b, i: (b, 0, 0))
    cs = pl.BlockSpec((1, 1, P, 2 * LANES), lambda b, i: (b, l, 0, 0))
    assert T // SWA_BLOCK >= 3
    band = _swa_band_masks()
    return pl.pallas_call(
        functools.partial(_swa_lat_kernel, T=T),
        out_shape=jax.ShapeDtypeStruct((B, T, W512), BF16),
        grid=(B, T // SWA_BLOCK),
        in_specs=[pl.BlockSpec(memory_space=pltpu.SMEM), qs, full, full, cs, cs, _const_spec(band.shape)],
        out_specs=qs,
        compiler_params=_cparams("parallel", "arbitrary"),
        name="swa_lat",
    )(sink, q, kd, vd, ckd, cvd, band)


FN_N2 = 256
FN_GP = 2


def _fourier_kernel(x_ref, cc_ref, m_ref, y_ref, *scratch, n1):
    gw = FN_GP * FNET_GC
    cc = cc_ref[...]
    for gp in range(FNET_GROUPS // FN_GP):
        parts = []
        for t1 in range(n1):
            ws = []
            for g in range(FN_GP):
                lo = t1 * FNET_WIDTH + (gp * FN_GP + g) * FNET_GC
                ws.append(_dot(x_ref[0, :, lo:lo + FNET_GC], cc))
            wr = jnp.concatenate([w[:, :FNET_GC] for w in ws], -1)
            wi = jnp.concatenate([w[:, FNET_GC:] for w in ws], -1)
            v = jnp.concatenate([wr, wi], 0).astype(BF16)
            b = _dot(m_ref[t1], v)
            if n1 == 1:
                y_ref[0, :, gp * gw:(gp + 1) * gw] = b.astype(y_ref.dtype)
            else:
                scratch[0][t1] = b
        if n1 > 1:
            _dft16_real(scratch[0], y_ref, gp * gw, gw)


FN_RCH = 16


def _dft16_real(bs, y_ref, col0, gw):
    tw = {m: (math.cos(2 * math.pi * m / 16), math.sin(2 * math.pi * m / 16)) for m in (1, 2, 3, 6, 9)}

    def body(rc, carry):
        r0 = pl.multiple_of(rc * FN_RCH, FN_RCH)
        re = [bs[t1, pl.ds(r0, FN_RCH), :] for t1 in range(16)]
        im = [bs[t1, pl.ds(FN_N2 + r0, FN_RCH), :] for t1 in range(16)]
        h_re = [[None] * 4 for _ in range(4)]
        h_im = [[None] * 4 for _ in range(4)]
        for b in range(4):
            s02r, s02i = re[b] + re[8 + b], im[b] + im[8 + b]
            d02r, d02i = re[b] - re[8 + b], im[b] - im[8 + b]
            s13r, s13i = re[4 + b] + re[12 + b], im[4 + b] + im[12 + b]
            d13r, d13i = re[4 + b] - re[12 + b], im[4 + b] - im[12 + b]
            g = [(s02r + s13r, s02i + s13i), (d02r + d13i, d02i - d13r),
                 (s02r - s13r, s02i - s13i), (d02r - d13i, d02i + d13r)]
            for c in range(4):
                gr, gi = g[c]
                m = b * c
                if m == 0:
                    hr, hi = gr, gi
                elif m == 4:
                    hr, hi = gi, -gr
                else:
                    cs, sn = tw[m]
                    hr = cs * gr + sn * gi
                    hi = (cs * gi - sn * gr) if b % 2 else None
                h_re[b][c], h_im[b][c] = hr, hi
        for c in range(4):
            p = h_re[0][c] + h_re[2][c]
            q = h_re[0][c] - h_re[2][c]
            r = h_re[1][c] + h_re[3][c]
            s = h_im[1][c] - h_im[3][c]
            for d, val in enumerate((p + r, q + s, p - r, q - s)):
                k1 = c + 4 * d
                y_ref[0, pl.ds(k1 * FN_N2 + r0, FN_RCH), col0:col0 + gw] = val.astype(y_ref.dtype)
        return carry

    lax.fori_loop(0, FN_N2 // FN_RCH, body, 0)


def _dft_consts(T):
    n1 = T // FN_N2
    j = np.arange(FNET_GC)
    ang = 2 * np.pi * np.outer(j, j) / FNET_GC
    sc = 1.0 / math.sqrt(FNET_GC)
    cc = np.concatenate([np.cos(ang) * sc, -np.sin(ang) * sc], 1)
    k2 = np.arange(FN_N2)[:, None]
    t2 = np.arange(FN_N2)[None, :]
    st = 1.0 / math.sqrt(T)
    mats = []
    for t1 in range(n1):
        th = 2 * np.pi * ((k2 * (n1 * t2 + t1)) % T) / T
        c, s = np.cos(th) * st, np.sin(th) * st
        top = np.concatenate([c, s], 1)
        mats.append(top if n1 == 1 else np.concatenate([top, np.concatenate([-s, c], 1)], 0))
    return n1, jnp.asarray(cc, BF16), jnp.asarray(np.stack(mats), BF16)


def _fourier_call(xv, T):
    B = xv.shape[0]
    n1, cc, mats = _dft_consts(T)
    assert n1 in (1, 16)
    gw = FN_GP * FNET_GC
    scratch = [pltpu.VMEM((n1, 2 * FN_N2, gw), F32)] if n1 > 1 else []
    return pl.pallas_call(
        functools.partial(_fourier_kernel, n1=n1),
        out_shape=jax.ShapeDtypeStruct((B, T, FNET_WIDTH), BF16),
        grid=(B,),
        in_specs=[pl.BlockSpec((1, FN_N2, n1 * FNET_WIDTH), lambda b: (b, 0, 0)),
                  _const_spec(cc.shape), _const_spec(mats.shape)],
        out_specs=pl.BlockSpec((1, T, FNET_WIDTH), lambda b: (b, 0, 0)),
        scratch_shapes=scratch,
        compiler_params=_cparams("parallel"),
        name="fourier",
    )(xv, cc, mats)


def _route(scores, biased):
    one, zero = jnp.float32(1.0), jnp.float32(0.0)
    in2, gscore = [], []
    for g in range(N_EXPERT_GROUPS):
        vs = [biased[g * EXPERTS_PER_GROUP + j:g * EXPERTS_PER_GROUP + j + 1] for j in range(EXPERTS_PER_GROUP)]
        gs = None
        for j in range(EXPERTS_PER_GROUP):
            rank = None
            for i in range(EXPERTS_PER_GROUP):
                if i == j:
                    continue
                beats = (vs[i] >= vs[j]) if i < j else (vs[i] > vs[j])
                t = jnp.where(beats, one, zero)
                rank = t if rank is None else rank + t
            keep = jnp.where(rank < 2.0, one, zero)
            in2.append(keep)
            t = keep * vs[j]
            gs = t if gs is None else gs + t
        gscore.append(gs)
    rows, sel = [], []
    for g in range(N_EXPERT_GROUPS):
        lost = None
        for i in range(N_EXPERT_GROUPS):
            if i == g:
                continue
            beats = (gscore[i] >= gscore[g]) if i < g else (gscore[i] > gscore[g])
            t = jnp.where(beats, one, zero)
            lost = t if lost is None else lost + t
        gsel = jnp.where(lost < 1.0, one, zero)
        for j in range(EXPERTS_PER_GROUP):
            e = g * EXPERTS_PER_GROUP + j
            sel.append(gsel * in2[e])
            rows.append(sel[-1] * scores[e:e + 1])
    tot = rows[0]
    for rr in rows[1:]:
        tot = tot + rr
    return jnp.concatenate(rows, 0) / tot, jnp.concatenate(sel, 0)


HP_ROWS = D_MODEL // 2 // LANES
Y_ROWS = D_MODEL // LANES


def _pack_halves(v):
    w = v.shape[1] // 2
    lo = lax.bitcast_convert_type(v[:, :w].astype(F32), jnp.uint32)
    hi = lax.bitcast_convert_type(v[:, w:].astype(F32), jnp.uint32)
    return (lo >> 16) | hi


def _unpack_halves(p):
    lo = lax.bitcast_convert_type(p << 16, F32).astype(BF16)
    hi = lax.bitcast_convert_type(p & jnp.uint32(0xFFFF0000), F32).astype(BF16)
    return lo, hi


def _merge_kernel(x_ref, mod_ref, ya_ref, yb_ref, yc_ref, yd_ref, wg_ref, wb_ref, wo_ref,
                  lng_ref, lnb_ref, wr_ref, rb_ref, x1_ref, hp_ref, rt_ref, *, nsub):
    m = mod_ref[0]
    sub = x_ref.shape[0] // nsub
    for k in range(nsub):
        rows = slice(k * sub, (k + 1) * sub)
        x = x_ref[rows, :]
        h = (_ln(x) * (1.0 + m[1:2]) + m[0:1]).astype(BF16)
        merged = None
        for b, y_ref in enumerate((ya_ref, yb_ref, yc_ref, yd_ref)):
            gate = 0.5 * jnp.tanh(_dot(h, wg_ref[0, :, b * D_MODEL:(b + 1) * D_MODEL])) + 0.5
            term = gate * _dot(y_ref[rows, :], wb_ref[0, b])
            merged = term if merged is None else merged + term
        out = _dot(merged.astype(BF16), wo_ref[0])
        x1 = _ln(ALPHA * x + m[2:3] * out) * lng_ref[...] + lnb_ref[...]
        x1_ref[rows, :] = x1
        h2 = (_ln(x1) * (1.0 + m[4:5]) + m[3:4]).astype(BF16)
        packed = _pack_halves(h2)
        for c in range(HP_ROWS):
            hp_ref[pl.ds(k * sub * HP_ROWS + c, sub, stride=HP_ROWS), :] = packed[:, c * LANES:(c + 1) * LANES]
        logits = _dot_nt(wr_ref[...], h2)
        e = jnp.exp(logits - jnp.max(logits, 0, keepdims=True))
        scores = e / jnp.sum(e, 0, keepdims=True)
        comb, sel = _route(scores, scores + rb_ref[...])
        rt_ref[:, rows] = jnp.concatenate([comb, sel], 0)


def _merge_call(x, mod, ys, wg, wb, wo, l, lng, lnb, wr_t, rb, T, TM):
    N = x.shape[0]
    tpb = T // TM
    tok = lambda w: pl.BlockSpec((TM, w), lambda i: (i, 0))
    return pl.pallas_call(
        functools.partial(_merge_kernel, nsub=max(1, TM // SUB_TM)),
        out_shape=[jax.ShapeDtypeStruct((N, D_MODEL), F32), jax.ShapeDtypeStruct((N * HP_ROWS, LANES), jnp.uint32),
                   jax.ShapeDtypeStruct((2 * N_EXPERTS, N), F32)],
        grid=(N // TM,),
        in_specs=[tok(D_MODEL), pl.BlockSpec((1, SUBLANES, D_MODEL), lambda i: (i // tpb, 0, 0)),
                  tok(W512), tok(W512), tok(W512), tok(W512),
                  _layer_spec(wg.shape, l), _layer_spec(wb.shape, l), _layer_spec(wo.shape, l),
                  _const_spec(lng.shape), _const_spec(lnb.shape), _const_spec(wr_t.shape), _const_spec(rb.shape)],
        out_specs=[tok(D_MODEL), pl.BlockSpec((TM * HP_ROWS, LANES), lambda i: (i, 0)),
                   pl.BlockSpec((2 * N_EXPERTS, TM), lambda i: (0, i))],
        compiler_params=_cparams("parallel"),
        name="merge",
    )(x, mod, *ys, wg, wb, wo, lng, lnb, wr_t, rb)


MOE_TM = 2048
MOE_RB = 256
MOE_KC = 4
MOE_CH = MOE_TM // MOE_KC
MOE_SLOTS = 2 * MOE_TM + N_EXPERTS * SUBLANES + MOE_RB
MOE_PR = 2 * MOE_TM // LANES
MOE_UNROLL = 8


def _route_kernel(rt_ref, u_ref, pos_ref, wts_ref, seg_ref):
    comb = rt_ref[0:N_EXPERTS, :]
    sel = rt_ref[N_EXPERTS:2 * N_EXPERTS, :]
    nb = MOE_TM // LANES
    stacked = jnp.concatenate([sel[:, b * LANES:(b + 1) * LANES] for b in range(nb)], 0)
    within = _dot(stacked.astype(BF16), u_ref[...])
    tot = jnp.sum(stacked, -1, keepdims=True)
    base = jnp.zeros((N_EXPERTS, 1), F32)
    bases = []
    for b in range(nb):
        bases.append(base)
        base = base + tot[b * N_EXPERTS:(b + 1) * N_EXPERTS]
    cnt = base
    padded = jnp.floor((cnt + (SUBLANES - 1.0)) * (1.0 / SUBLANES)) * SUBLANES
    rowi = lax.broadcasted_iota(jnp.int32, (N_EXPERTS, 1), 0)
    off = jnp.zeros((N_EXPERTS, 1), F32)
    for e in range(N_EXPERTS - 1):
        off = off + jnp.where(rowi > e, padded[e:e + 1, :], 0.0)
    pos_rows, wts_rows = [], []
    for b in range(nb):
        slot = within[b * N_EXPERTS:(b + 1) * N_EXPERTS] + (bases[b] + off)
        seen = jnp.zeros((1, LANES), F32)
        acc = [jnp.zeros((1, LANES), F32) for _ in range(4)]
        for e in range(N_EXPERTS):
            s_e = sel[e:e + 1, b * LANES:(b + 1) * LANES]
            c_e = comb[e:e + 1, b * LANES:(b + 1) * LANES]
            first = jnp.where(seen == 0.0, s_e, 0.0)
            second = jnp.where(seen == 1.0, s_e, 0.0)
            acc[0] = acc[0] + first * slot[e:e + 1]
            acc[1] = acc[1] + second * slot[e:e + 1]
            acc[2] = acc[2] + first * c_e
            acc[3] = acc[3] + second * c_e
            seen = seen + s_e
        pos_rows += acc[0:2]
        wts_rows += acc[2:4]
    pos_ref[...] = (jnp.concatenate(pos_rows, 0) * float(HP_ROWS)).astype(jnp.int32).reshape(2 * MOE_TM)
    wts_ref[...] = jnp.concatenate(wts_rows, 0).reshape(2 * MOE_TM)
    lane = lax.broadcasted_iota(jnp.int32, (N_EXPERTS, LANES), 1)
    diag = lane == lax.broadcasted_iota(jnp.int32, (N_EXPERTS, LANES), 0)
    off_row = jnp.sum(jnp.where(diag, off, 0.0), 0, keepdims=True)
    cnt_row = jnp.sum(jnp.where(diag, cnt, 0.0), 0, keepdims=True)
    seg_ref[0] = jnp.concatenate([off_row, cnt_row, jnp.zeros((SUBLANES - 2, LANES), F32)], 0).astype(jnp.int32)


def _route_call(rt):
    N = rt.shape[1]
    nt = N // MOE_TM
    u = jnp.asarray(np.triu(np.ones((LANES, LANES), np.float32), 1), BF16)
    return pl.pallas_call(
        _route_kernel,
        out_shape=[jax.ShapeDtypeStruct((nt * 2 * MOE_TM,), jnp.int32),
                   jax.ShapeDtypeStruct((nt * 2 * MOE_TM,), F32),
                   jax.ShapeDtypeStruct((nt, SUBLANES, LANES), jnp.int32)],
        grid=(nt,),
        in_specs=[pl.BlockSpec((2 * N_EXPERTS, MOE_TM), lambda i: (0, i)), _const_spec(u.shape)],
        out_specs=[pl.BlockSpec((2 * MOE_TM,), lambda i: (i,)),
                   pl.BlockSpec((2 * MOE_TM,), lambda i: (i,)),
                   pl.BlockSpec((1, SUBLANES, LANES), lambda i: (i, 0, 0))],
        compiler_params=_cparams("parallel"),
        name="route",
    )(rt, u)


def _table_index(it):
    per_block = LANES // MOE_UNROLL
    return (it // per_block) * (2 * LANES) + (it % per_block) * MOE_UNROLL


def _moe_kernel(pos_ref, wts_ref, seg_ref, hp_ref, w1_ref, w3_ref, w2_ref, x1_ref, mod_ref, lng_ref, lnb_ref,
                o_ref, xs, ys, oc):
    s = pl.program_id(1)
    half = D_MODEL // 2

    @pl.when(s == 0)
    def _dispatch():
        xs[...] = jnp.zeros_like(xs)

        def body(it, carry):
            t0 = pl.multiple_of(it * (MOE_UNROLL * HP_ROWS), MOE_UNROLL * HP_ROWS)
            i0 = _table_index(it)
            for j in range(MOE_UNROLL):
                row = hp_ref[pl.ds(t0 + HP_ROWS * j, HP_ROWS), :]
                xs[pl.ds(pl.multiple_of(pos_ref[i0 + j], HP_ROWS), HP_ROWS), :] = row
                xs[pl.ds(pl.multiple_of(pos_ref[i0 + LANES + j], HP_ROWS), HP_ROWS), :] = row
            return carry

        lax.fori_loop(0, MOE_TM // MOE_UNROLL, body, 0)

    @pl.when(s < N_EXPERTS)
    def _expert():
        off = seg_ref[0, 0, s]
        cnt = seg_ref[0, 1, s]
        nblk = (cnt + (MOE_RB // 2 - 1)) // MOE_RB

        def ffn_rows(r0, nrows):
            parts = [_unpack_halves(xs[pl.ds(pl.multiple_of(r0 * HP_ROWS, SUBLANES) + c, nrows, stride=HP_ROWS), :])
                     for c in range(HP_ROWS)]
            lo = jnp.concatenate([p[0] for p in parts], -1)
            hi = jnp.concatenate([p[1] for p in parts], -1)
            a = _dot(lo, w1_ref[0, 0, :half, :]) + _dot(hi, w1_ref[0, 0, half:, :])
            g = _dot(lo, w3_ref[0, 0, :half, :]) + _dot(hi, w3_ref[0, 0, half:, :])
            act = (a * jax.nn.sigmoid(a) * g).astype(BF16)
            y = _dot(act, w2_ref[0, 0])
            y0 = pl.multiple_of(r0 * Y_ROWS, SUBLANES)
            for c in range(Y_ROWS):
                ys[pl.ds(y0 + c, nrows, stride=Y_ROWS), :] = y[:, c * LANES:(c + 1) * LANES]

        def blk(i, carry):
            ffn_rows(pl.multiple_of(off + i * MOE_RB, SUBLANES), MOE_RB)
            return carry

        lax.fori_loop(0, nblk, blk, 0)

        @pl.when(cnt > nblk * MOE_RB)
        def _tail():
            ffn_rows(pl.multiple_of(off + nblk * MOE_RB, SUBLANES), MOE_RB // 2)

    @pl.when(s >= N_EXPERTS)
    def _combine():
        it0 = (s - N_EXPERTS) * (MOE_CH // MOE_UNROLL)

        def body(it, carry):
            t0 = pl.multiple_of(it * (MOE_UNROLL * Y_ROWS), MOE_UNROLL * Y_ROWS)
            i0 = _table_index(it0 + it)
            for j in range(MOE_UNROLL):
                p0 = pl.multiple_of(pos_ref[i0 + j] * (Y_ROWS // HP_ROWS), Y_ROWS)
                p1 = pl.multiple_of(pos_ref[i0 + LANES + j] * (Y_ROWS // HP_ROWS), Y_ROWS)
                oc[pl.ds(t0 + Y_ROWS * j, Y_ROWS), :] = (wts_ref[i0 + j] * ys[pl.ds(p0, Y_ROWS), :]
                                                         + wts_ref[i0 + LANES + j] * ys[pl.ds(p1, Y_ROWS), :])
            return carry

        lax.fori_loop(0, MOE_CH // MOE_UNROLL, body, 0)
        moe = jnp.concatenate([oc[pl.ds(c, MOE_CH, stride=Y_ROWS), :] for c in range(Y_ROWS)], -1)
        m = mod_ref[0]
        o_ref[...] = _ln(ALPHA * x1_ref[...] + m[5:6] * moe) * lng_ref[...] + lnb_ref[...]


def _moe_call(pos, wts, seg, hp, w1, w3, w2, l, x1, mod, lng, lnb, T):
    N = x1.shape[0]
    nt = N // MOE_TM
    flat = pl.BlockSpec((2 * MOE_TM,), lambda i, s: (i,), memory_space=pltpu.SMEM)
    wspec = lambda shp: pl.BlockSpec((1, 1) + shp, lambda i, s: (l, jnp.minimum(s, N_EXPERTS - 1), 0, 0))
    chunk = pl.BlockSpec((MOE_CH, D_MODEL),
                         lambda i, s: (i * MOE_KC + jnp.clip(s - N_EXPERTS, 0, MOE_KC - 1), 0))
    return pl.pallas_call(
        _moe_kernel,
        out_shape=jax.ShapeDtypeStruct((N, D_MODEL), F32),
        grid=(nt, N_EXPERTS + MOE_KC),
        in_specs=[flat, flat,
                  pl.BlockSpec((1, SUBLANES, LANES), lambda i, s: (i, 0, 0), memory_space=pltpu.SMEM),
                  pl.BlockSpec((MOE_TM * HP_ROWS, LANES), lambda i, s: (i, 0)),
                  wspec((D_MODEL, D_EXPERT)), wspec((D_MODEL, D_EXPERT)), wspec((D_EXPERT, D_MODEL)),
                  chunk, pl.BlockSpec((1, SUBLANES, D_MODEL), lambda i, s: ((i * MOE_TM) // T, 0, 0)),
                  _const_spec(lng.shape), _const_spec(lnb.shape)],
        out_specs=chunk,
        scratch_shapes=[pltpu.VMEM((MOE_SLOTS * HP_ROWS, LANES), jnp.uint32),
                        pltpu.VMEM((MOE_SLOTS * Y_ROWS, LANES), F32),
                        pltpu.VMEM((MOE_CH * Y_ROWS, LANES), F32)],
        compiler_params=_cparams("parallel", "arbitrary"),
        name="moe",
    )(pos, wts, seg, hp, w1, w3, w2, x1, mod, lng, lnb)


def _rope_tables(T):
    t = jnp.arange(T)
    nf = HEAD_DIM // 4
    inv = ROPE_BASE ** (-jnp.arange(nf, dtype=F32) / nf)
    ar = (t // GRID_W).astype(F32)[:, None] * inv
    ac = (t % GRID_W).astype(F32)[:, None] * inv
    cos = jnp.concatenate([jnp.cos(ar), jnp.cos(ar), jnp.cos(ac), jnp.cos(ac)], -1)
    sin = jnp.concatenate([-jnp.sin(ar), jnp.sin(ar), -jnp.sin(ac), jnp.sin(ac)], -1)
    return jnp.tile(cos, (1, LANES // HEAD_DIM)), jnp.tile(sin, (1, LANES // HEAD_DIM))


def _nat_bias_table(rpb):
    q = np.arange(GRID_W)
    kc = np.arange(GRID_W)
    cstart = np.clip(q - NAT_KC // 2, 0, GRID_W - NAT_KC)
    ok = (kc[None, :] >= cstart[:, None]) & (kc[None, :] < cstart[:, None] + NAT_KC)
    cidx = np.clip(kc[None, :] - q[:, None] + NAT_KC - 1, 0, 2 * NAT_KC - 2)
    onehot = (np.arange(2 * NAT_KC - 1)[:, None, None] == cidx[None]).astype(np.float32)
    t = jnp.einsum('hrc,cqk->hqrk', rpb.astype(F32), jnp.asarray(onehot), precision=lax.Precision.HIGHEST)
    t = jnp.where(jnp.asarray(ok)[None, :, None, :], t * LOG2E, NEG_INF)
    per_d = [t[:, :, NAT_KR - 1 - d:2 * NAT_KR - 1 - d, :].reshape(NAT_HEADS // 2, 2 * GRID_W, NAT_KR * GRID_W)
             for d in range(NAT_KR)]
    return jnp.stack(per_d, 1)


def _dup_heads(a):
    a = jnp.broadcast_to(a[..., :, None, :], a.shape[:-1] + (2, a.shape[-1]))
    return a.reshape(a.shape[:-3] + (-1,))


def _block_diag(w):
    eye = jnp.eye(RNN_BLOCKS, dtype=w.dtype)
    return jnp.einsum('dnio,nm->dnimo', w, eye).reshape(2, RNN_WIDTH, RNN_WIDTH)


def _pad_rows(a, rows=SUBLANES):
    return jnp.pad(a, ((0, rows - a.shape[0]),) + ((0, 0),) * (a.ndim - 1))


def kernel(x_prompt, x_sample, c, cache_nat_k, cache_nat_v, cache_swa_k, cache_swa_v, state_rglru, c_ctx,
           w_ada, b_ada, w_in, rg_conv_w, rg_conv_b, rg_wa, rg_ba, rg_wx, rg_bx, rg_lambda, nat_rpb,
           swa_sink, w_branch, w_out, ln_g, ln_b, w_router, router_bias, w1, w3, w2):
    B_c, T_c, _ = x_prompt.shape
    B_l, T_l, _ = x_sample.shape
    P = cache_nat_k.shape[2]

    cv = jnp.concatenate([c_ctx[None], c, jnp.zeros((ADA_ROWS - 1 - B_l, D_MODEL), F32)], 0)
    mod_all = _ada_call(cv, w_ada, b_ada).reshape(DEPTH, ADA_ROWS, 6, D_MODEL)
    mod_all = jnp.pad(mod_all, ((0, 0), (0, 0), (0, SUBLANES - 6), (0, 0)))

    sk0 = 3072
    sv0 = sk0 + SWA_KV_HEADS * HEAD_DIM
    xf0 = sv0 + SWA_KV_HEADS * HEAD_DIM
    w_in_b = w_in.astype(BF16)
    head = lambda c0, g: w_in_b[:, :, c0 + g * HEAD_DIM:c0 + (g + 1) * HEAD_DIM]
    w_tail_ctx = jnp.concatenate([head(c0, g) for c0 in (sk0, sv0) for g in (0, 0, 1, 1)], -1)
    w_tail_lat = jnp.concatenate([w_in_b[:, :, xf0:GATE_OFF], w_tail_ctx], -1)
    w_gate = (0.5 * w_in[:, :, GATE_OFF:]).astype(BF16)
    w_branch_b = w_branch.astype(BF16)
    w_out_b = w_out.astype(BF16)
    w1_b, w3_b, w2_b = w1.astype(BF16), w3.astype(BF16), w2.astype(BF16)
    wr_t = w_router.T.astype(BF16)
    wa_bd = jnp.stack([_block_diag(0.5 * rg_wa[l]) for l in range(DEPTH)]).astype(BF16)
    wx_bd = jnp.stack([_block_diag(0.5 * rg_wx[l]) for l in range(DEPTH)]).astype(BF16)

    plan_ctx = ((0, 0, 512, F32, 1.0, False, False, None), (0, 512, 512, F32, 1.0, False, False, None),
                (0, 1024, 512, BF16, Q_SCALE, False, False, None), (0, 1536, 512, F32, 1.0, False, False, 0),
                (0, 2048, 512, F32, 1.0, False, False, 1), (0, 2560, 512, BF16, Q_SCALE, False, False, None),
                (0, sk0, 128, F32, 1.0, False, False, 2), (0, sv0, 128, F32, 1.0, False, False, 3),
                (0, xf0, 512, BF16, 1.0, False, True, None),
                (1, 0, 256, BF16, 1.0, False, False, None), (1, 256, 256, BF16, 1.0, False, False, None))
    plan_lat = ((0, 0, 512, F32, 1.0, False, False, None), (0, 512, 512, F32, 1.0, False, False, None),
                (0, 1024, 512, BF16, Q_SCALE, False, False, None), (0, 1536, 512, BF16, 1.0, False, False, None),
                (0, 2048, 512, BF16, 1.0, False, False, None), (0, 2560, 512, BF16, Q_SCALE, True, False, None),
                (1, 0, 512, BF16, 1.0, False, True, None),
                (1, 512, 256, BF16, 1.0, True, False, None), (1, 768, 256, BF16, 1.0, False, False, None))

    rope_tabs = _rope_tables(T_l)
    ck_nat = cache_nat_k.reshape(B_l, DEPTH, P, W512).astype(BF16)
    cv_nat = cache_nat_v.reshape(B_l, DEPTH, P, W512).astype(BF16)
    ckd_swa = _dup_heads(cache_swa_k).astype(BF16)
    cvd_swa = _dup_heads(cache_swa_v).astype(BF16)
    state8 = jnp.pad(state_rglru, ((0, 0), (0, 0), (0, SUBLANES - 2), (0, 0)))
    zero_state = jnp.zeros((B_c, SUBLANES, RNN_WIDTH), F32)

    def layer(x, l, ctx_pass, caches=None):
        B, T = (B_c, T_c) if ctx_pass else (B_l, T_l)
        TM = min(T, TOKEN_TM)
        modb = jnp.broadcast_to(mod_all[l, 0:1], (B, SUBLANES, D_MODEL)) if ctx_pass else mod_all[l, 1:1 + B_l]
        if ctx_pass:
            xa, ga, nq, nk, nv, sq, sk, sv, xf, skd, svd = _pre_call(
                x, modb, w_in_b, GATE_OFF, w_tail_ctx, l, plan_ctx, B, T, TM, None, caches)
        else:
            xa, ga, nq, nk, nv, sq, xf, skd, svd = _pre_call(
                x, modb, w_in_b, sk0, w_tail_lat, l, plan_lat, B, T, TM, rope_tabs)
        r3 = lambda a: a.reshape(B, T, a.shape[-1])
        h0 = zero_state if ctx_pass else state8[:, l]
        ya, hfin = _rglru_call(
            r3(xa), r3(ga), _pad_rows(rg_conv_w[l]), rg_conv_b[l][None], wa_bd[l], wx_bd[l],
            _pad_rows(0.5 * rg_ba[l]), _pad_rows(0.5 * rg_bx[l]), _pad_rows(rg_lambda[l]), h0)
        if ctx_pass:
            yb, yc = _attn_ctx_call(swa_sink[l], r3(nq), nk, nv, r3(sq), r3(skd), r3(svd), l)
        else:
            yb = _nat_lat_call(r3(nq), r3(nk), r3(nv), ck_nat, cv_nat, _nat_bias_table(nat_rpb[l]), l)
            yc = _swa_lat_call(swa_sink[l], r3(sq), r3(skd), r3(svd), ckd_swa, cvd_swa, l)
        yd = _fourier_call(xf, T)
        f2 = lambda a: a.reshape(B * T, a.shape[-1])
        x1, hp, rt = _merge_call(
            x, modb, (f2(ya), f2(yb), f2(yc), f2(yd)), w_gate, w_branch_b, w_out_b, l,
            ln_g[l, 0][None], ln_b[l, 0][None], wr_t,
            jnp.broadcast_to(router_bias[:, None], (N_EXPERTS, min(T, SUB_TM))), T, min(T, SUB_TM))
        pos, wts, seg = _route_call(rt)
        x2 = _moe_call(pos, wts, seg, hp, w1_b, w3_b, w2_b, l, x1, modb,
                       ln_g[l, 1][None], ln_b[l, 1][None], T)
        new = ((nk, nv, sk, sv), hfin[:, :2]) if ctx_pass else None
        return x2, new

    y = x_prompt.reshape(B_c * T_c, D_MODEL)
    caches = tuple(jnp.zeros((B_c, DEPTH, T_c, w), F32)
                   for w in (W512, W512, SWA_KV_HEADS * HEAD_DIM, SWA_KV_HEADS * HEAD_DIM))
    states = []
    for l in range(DEPTH):
        y, (caches, st) = layer(y, l, True, caches)
        states.append(st)
    y_prompt = y.reshape(B_c, T_c, D_MODEL)
    new_nat_k = caches[0].reshape(B_c, DEPTH, T_c, NAT_HEADS, HEAD_DIM)
    new_nat_v = caches[1].reshape(B_c, DEPTH, T_c, NAT_HEADS, HEAD_DIM)
    new_swa_k = caches[2].reshape(B_c, DEPTH, T_c, SWA_KV_HEADS, HEAD_DIM)
    new_swa_v = caches[3].reshape(B_c, DEPTH, T_c, SWA_KV_HEADS, HEAD_DIM)
    new_state = jnp.stack(states, 1)

    y = x_sample.reshape(B_l * T_l, D_MODEL)
    for l in range(DEPTH):
        y, _ = layer(y, l, False)
    y_sample = y.reshape(B_l, T_l, D_MODEL)
    return (y_prompt, y_sample, new_nat_k, new_nat_v, new_swa_k, new_swa_v, new_state)
```

```python
import functools
import math

import jax
import jax.numpy as jnp
import numpy as np
from jax import lax
from jax.experimental import pallas as pl
from jax.experimental.pallas import tpu as pltpu

F32 = jnp.float32
BF16 = jnp.bfloat16

D_MODEL = 1024
DEPTH = 4
GRID_W = 64
HEAD_DIM = 64
ATTN_SCALE = HEAD_DIM ** -0.5
LOG2E = math.log2(math.e)
Q_SCALE = ATTN_SCALE * LOG2E
NEG_INF = -1e30
LN_EPS = 1e-5
ALPHA = (2 * DEPTH) ** 0.25
ROPE_BASE = 10000.0
RNN_WIDTH = 512
RNN_BLOCKS = 8
RNN_BLOCK = RNN_WIDTH // RNN_BLOCKS
CONV_W = 4
RGLRU_C = 8.0
NAT_HEADS = 8
NAT_KR = 8
NAT_KC = 16
SWA_HEADS = 8
SWA_KV_HEADS = 2
SWA_WINDOW = 128
SWA_BLOCK = 128
SWA_SPAN = SWA_BLOCK + 2 * SWA_WINDOW
FNET_GROUPS = 4
FNET_WIDTH = 512
FNET_GC = FNET_WIDTH // FNET_GROUPS
N_BRANCH = 4
N_EXPERTS = 16
N_EXPERT_GROUPS = 4
EXPERTS_PER_GROUP = N_EXPERTS // N_EXPERT_GROUPS
D_EXPERT = 512
W512 = 512
GATE_OFF = 3840

LANES = 128
SUBLANES = 8
VMEM_LIMIT = 56 * 1024 * 1024
TOKEN_TM = 1024
SUB_TM = 512


def _cparams(*sem):
    return pltpu.CompilerParams(dimension_semantics=sem, vmem_limit_bytes=VMEM_LIMIT)


def _const_spec(shape):
    nd = len(shape)
    return pl.BlockSpec(shape, lambda *_: (0,) * nd, pipeline_mode=pl.Buffered(1))


def _layer_spec(shape, l):
    nd = len(shape)
    return pl.BlockSpec((1,) + tuple(shape[1:]), lambda *_: (l,) + (0,) * (nd - 1), pipeline_mode=pl.Buffered(1))


def _ln(x):
    mu = jnp.mean(x, -1, keepdims=True)
    xc = x - mu
    var = jnp.mean(xc * xc, -1, keepdims=True)
    return xc * lax.rsqrt(var + LN_EPS)


def _dot(a, b):
    return jnp.dot(a, b, preferred_element_type=F32)


def _dot_nt(a, b):
    return lax.dot_general(a, b, (((1,), (1,)), ((), ())), preferred_element_type=F32)


ADA_ROWS = 16
ADA_TN = 1024


def _ada_kernel(c_ref, w_ref, b_ref, o_ref):
    cv = c_ref[...]
    s = (cv * jax.nn.sigmoid(cv)).astype(BF16)
    o_ref[0] = _dot(s, w_ref[0].astype(BF16)) + b_ref[0]


def _ada_call(cv, w_ada, b_ada):
    n = w_ada.shape[-1]
    return pl.pallas_call(
        _ada_kernel,
        out_shape=jax.ShapeDtypeStruct((DEPTH, ADA_ROWS, n), F32),
        grid=(DEPTH, n // ADA_TN),
        in_specs=[
            pl.BlockSpec((ADA_ROWS, D_MODEL), lambda l, j: (0, 0)),
            pl.BlockSpec((1, D_MODEL, ADA_TN), lambda l, j: (l, 0, j)),
            pl.BlockSpec((1, 1, ADA_TN), lambda l, j: (l, 0, j)),
        ],
        out_specs=pl.BlockSpec((1, ADA_ROWS, ADA_TN), lambda l, j: (l, 0, j)),
        compiler_params=_cparams("parallel", "parallel"),
        name="ada",
    )(cv, w_ada, b_ada.reshape(DEPTH, 1, n))


def _rope(u, cos, sin):
    lane = lax.broadcasted_iota(jnp.int32, cos.shape, 1)
    first = (lane & 31) < 16
    outs = []
    for j in range(u.shape[1] // LANES):
        s = u[:, j * LANES:(j + 1) * LANES]
        partner = jnp.where(first, pltpu.roll(s, LANES - 16, 1), pltpu.roll(s, 16, 1))
        outs.append(s * cos + partner * sin)
    return outs[0] if len(outs) == 1 else jnp.concatenate(outs, -1)


def _pre_kernel(*refs, plan, rope, n1, nsub, n_alias):
    x_ref, mod_ref, wm_ref, wt_ref = refs[:4]
    refs = refs[4:]
    if rope:
        cos_ref, sin_ref = refs[:2]
        refs = refs[2:]
    refs = refs[n_alias:]
    outs = refs[:len(plan)]
    m = mod_ref[0]
    sub = x_ref.shape[0] // nsub
    hs = [(_ln(x_ref[k * sub:(k + 1) * sub, :]) * (1.0 + m[1:2]) + m[0:1]).astype(BF16) for k in range(nsub)]
    for (src, off, width, _, scale, do_rope, fourier, slot), o_ref in zip(plan, outs):
        w_ref = wt_ref if src else wm_ref
        for k in range(nsub):
            rows = slice(k * sub, (k + 1) * sub)
            u = _dot(hs[k], w_ref[0, :, off:off + width])
            if do_rope:
                u = _rope(u, cos_ref[rows, :], sin_ref[rows, :])
            if scale != 1.0:
                u = u * scale
            if fourier and n1 > 1:
                stage = refs[len(plan)]
                for c in range(width // LANES):
                    stage[k, c] = u[:, c * LANES:(c + 1) * LANES]
                r = sub // n1
                for t1 in range(n1):
                    for c in range(width // LANES):
                        lo = t1 * width + c * LANES
                        o_ref[0, k * r:(k + 1) * r, lo:lo + LANES] = (
                            stage[k, c, pl.ds(t1, r, stride=n1), :].astype(o_ref.dtype))
            elif fourier:
                o_ref[0, rows, :] = u.astype(o_ref.dtype)
            elif slot is not None:
                o_ref[0, 0, rows, :] = u.astype(o_ref.dtype)
            else:
                o_ref[rows, :] = u.astype(o_ref.dtype)


def _pre_call(x, mod, w_main, n_main, w_tail, l, plan, B, T, TM, rope_tabs, caches=None):
    N = x.shape[0]
    tpb = T // TM
    nsub = max(1, TM // SUB_TM)
    n1 = T // FN_N2
    rope = rope_tabs is not None
    n_tail = w_tail.shape[-1]
    in_specs = [
        pl.BlockSpec((TM, D_MODEL), lambda i: (i, 0)),
        pl.BlockSpec((1, SUBLANES, D_MODEL), lambda i: (i // tpb, 0, 0)),
        pl.BlockSpec((1, D_MODEL, n_main), lambda i: (l, 0, 0), pipeline_mode=pl.Buffered(1)),
        pl.BlockSpec((1, D_MODEL, n_tail), lambda i: (l, 0, 0), pipeline_mode=pl.Buffered(1)),
    ]
    args = [x, mod, w_main, w_tail]
    if rope:
        in_specs += [pl.BlockSpec((TM, LANES), lambda i: (i % tpb, 0))] * 2
        args += list(rope_tabs)
    out_shape, out_specs, scratch, aliases = [], [], [], {}
    for k, p in enumerate(plan):
        if p[6]:
            out_shape.append(jax.ShapeDtypeStruct((B, FN_N2, n1 * p[2]), p[3]))
            out_specs.append(pl.BlockSpec((1, TM // n1, n1 * p[2]), lambda i: (i // tpb, i % tpb, 0)))
            if n1 > 1:
                scratch.append(pltpu.VMEM((nsub, p[2] // LANES, TM // nsub, LANES), F32))
        elif p[7] is not None:
            out_shape.append(jax.ShapeDtypeStruct((B, DEPTH, T, p[2]), p[3]))
            out_specs.append(pl.BlockSpec((1, 1, TM, p[2]), lambda i: (i // tpb, l, i % tpb, 0)))
            if caches is not None:
                aliases[len(args)] = k
                in_specs.append(pl.BlockSpec(memory_space=pl.ANY))
                args.append(caches[p[7]])
        else:
            out_shape.append(jax.ShapeDtypeStruct((N, p[2]), p[3]))
            out_specs.append(pl.BlockSpec((TM, p[2]), lambda i: (i, 0)))
    return pl.pallas_call(
        functools.partial(_pre_kernel, plan=plan, rope=rope, n1=n1, nsub=nsub, n_alias=len(aliases)),
        out_shape=out_shape,
        grid=(N // TM,),
        in_specs=in_specs,
        out_specs=out_specs,
        scratch_shapes=scratch,
        input_output_aliases=aliases,
        compiler_params=_cparams("parallel"),
        name="pre",
    )(*args)


RG_CW = 256
RG_TCH = 256
RG_SEGS = SUBLANES


def _rglru_kernel(xa_ref, ga_ref, cw_ref, cb_ref, wa_ref, wx_ref, ba_ref, bx_ref, lam_ref, h0_ref,
                  y_ref, hfin_ref, xpad, a_f, u_f, a_b, u_b, carry_f, carry_b, *, T):
    cw_ = RG_CW
    zeros8 = jnp.zeros((SUBLANES, cw_), F32)
    xpad[0:SUBLANES, :] = zeros8
    xpad[T + SUBLANES:T + 2 * SUBLANES, :] = zeros8
    xpad[SUBLANES:T + SUBLANES, :] = xa_ref[0]
    cw = cw_ref[...]
    cb = cb_ref[...]
    lam = lam_ref[...]
    sp = jnp.maximum(-lam, 0.0) + jnp.log1p(jnp.exp(-jnp.abs(lam)))
    sp4 = (0.5 * RGLRU_C) * sp
    nsp4_log2e = sp4 * (-math.log2(math.e))
    ba = ba_ref[...]
    bx = bx_ref[...]
    h0 = h0_ref[0]

    def chunk(c, carry):
        base = pl.multiple_of(c * RG_TCH, RG_TCH)
        xw = xpad[pl.ds(base, RG_TCH + 2 * SUBLANES), :]
        xc = cb
        for i in range(CONV_W):
            xc = xc + cw[i:i + 1] * xw[SUBLANES - 1 + i:SUBLANES - 1 + i + RG_TCH]
        xcb = xc.astype(BF16)
        half_xc = 0.5 * xc
        for d, (a_s, u_s) in enumerate(((a_f, u_f), (a_b, u_b))):
            r2 = jnp.tanh(_dot(xcb, wa_ref[d]) + ba[d:d + 1]) + 1.0
            i2 = jnp.tanh(_dot(xcb, wx_ref[d]) + bx[d:d + 1]) + 1.0
            a = jnp.exp2(r2 * nsp4_log2e[d:d + 1])
            v = jnp.tanh(r2 * sp4[d:d + 1]) * (1.0 + a * a)
            gain = jnp.where(v > 0.0, v * lax.rsqrt(v), 0.0)
            put(a_s, pl.ds(store_row(c), RG_TCH), a)
            put(u_s, pl.ds(store_row(c), RG_TCH), gain * (i2 * half_xc))
        return carry

    def get(ref, rows):
        return jnp.concatenate([ref[k, rows, :] for k in range(cw_ // LANES)], -1)

    def put(ref, rows, val):
        for k in range(cw_ // LANES):
            ref[k, rows, :] = val[:, k * LANES:(k + 1) * LANES]

    seg_len = T // RG_SEGS
    segmented = seg_len % RG_TCH == 0
    seg_pitch = seg_len + SUBLANES
    per_seg = seg_len // RG_TCH if segmented else 1

    def store_row(c):
        base = c * RG_TCH
        return pl.multiple_of(base + SUBLANES * (c // per_seg), SUBLANES) if segmented else pl.multiple_of(base, RG_TCH)

    lax.fori_loop(0, T // RG_TCH, chunk, 0)

    row = lax.broadcasted_iota(jnp.int32, (SUBLANES, cw_), 0)

    if segmented:
        def seg_step(j, carry):
            h_f, p_f, h_b, p_b = carry
            jb = seg_len - 1 - j
            rows_f = pl.ds(j, RG_SEGS, stride=seg_pitch)
            rows_b = pl.ds(jb, RG_SEGS, stride=seg_pitch)
            a = get(a_f, rows_f)
            h_f = a * h_f + get(u_f, rows_f)
            p_f = p_f * a
            put(u_f, rows_f, h_f)
            put(a_f, rows_f, p_f)
            a = get(a_b, rows_b)
            h_b = a * h_b + get(u_b, rows_b)
            p_b = p_b * a
            put(u_b, rows_b, h_b)
            put(a_b, rows_b, p_b)
            return h_f, p_f, h_b, p_b

        zero = jnp.zeros((RG_SEGS, cw_), F32)
        one = jnp.ones((RG_SEGS, cw_), F32)
        h_f, p_f, h_b, p_b = lax.fori_loop(0, seg_len, seg_step, (zero, one, zero, one), unroll=2)
        cf = [h0[0:1, :]]
        for s_ in range(RG_SEGS):
            cf.append(h_f[s_:s_ + 1, :] + p_f[s_:s_ + 1, :] * cf[s_])
        cb_in = [None] * RG_SEGS + [h0[1:2, :]]
        for s_ in range(RG_SEGS - 1, -1, -1):
            cb_in[s_] = h_b[s_:s_ + 1, :] + p_b[s_:s_ + 1, :] * cb_in[s_ + 1]
        hfin_ref[0] = jnp.where(row == 0, cf[RG_SEGS], jnp.where(row == 1, cb_in[0], 0.0))
        carry_f[...] = jnp.concatenate(cf[:RG_SEGS], 0)
        carry_b[...] = jnp.concatenate(cb_in[1:], 0)

        def emit_seg(c, carry):
            base = pl.multiple_of(c * RG_TCH, RG_TCH)
            s_ = c // per_seg
            rows = pl.ds(store_row(c), RG_TCH)
            h = (get(u_f, rows) + get(a_f, rows) * carry_f[pl.ds(s_, 1), :]
                 + get(u_b, rows) + get(a_b, rows) * carry_b[pl.ds(s_, 1), :])
            y_ref[0, pl.ds(base, RG_TCH), :] = (
                h * jax.nn.gelu(ga_ref[0, pl.ds(base, RG_TCH), :])).astype(y_ref.dtype)
            return carry

        lax.fori_loop(0, T // RG_TCH, emit_seg, 0)
        return

    def block_scan(a, u, reverse):
        for dd in (1, 2, 4):
            sh = SUBLANES - dd if reverse else dd
            a_n = pltpu.roll(a, sh, 0)
            u_n = pltpu.roll(u, sh, 0)
            ok = (row < SUBLANES - dd) if reverse else (row >= dd)
            u = jnp.where(ok, u + a * u_n, u)
            a = jnp.where(ok, a * a_n, a)
        return a, u

    nblk = T // SUBLANES

    def scan(i, carry):
        c_f, c_b = carry
        lo_f = pl.multiple_of(i * SUBLANES, SUBLANES)
        lo_b = pl.multiple_of((nblk - 1 - i) * SUBLANES, SUBLANES)
        a, u = block_scan(get(a_f, pl.ds(lo_f, SUBLANES)), get(u_f, pl.ds(lo_f, SUBLANES)), False)
        h_f = u + a * c_f
        put(u_f, pl.ds(lo_f, SUBLANES), h_f)
        a, u = block_scan(get(a_b, pl.ds(lo_b, SUBLANES)), get(u_b, pl.ds(lo_b, SUBLANES)), True)
        h_b = u + a * c_b
        put(u_b, pl.ds(lo_b, SUBLANES), h_b)
        return (jnp.broadcast_to(h_f[SUBLANES - 1:SUBLANES, :], (SUBLANES, cw_)),
                jnp.broadcast_to(h_b[0:1, :], (SUBLANES, cw_)))

    c_f, c_b = lax.fori_loop(
        0, nblk, scan,
        (jnp.broadcast_to(h0[0:1, :], (SUBLANES, cw_)), jnp.broadcast_to(h0[1:2, :], (SUBLANES, cw_))),
        unroll=2)
    hfin_ref[0] = jnp.where(row == 0, c_f, jnp.where(row == 1, c_b, 0.0))

    def emit(c, carry):
        base = pl.multiple_of(c * RG_TCH, RG_TCH)
        h = get(u_f, pl.ds(base, RG_TCH)) + get(u_b, pl.ds(base, RG_TCH))
        y_ref[0, pl.ds(base, RG_TCH), :] = (h * jax.nn.gelu(ga_ref[0, pl.ds(base, RG_TCH), :])).astype(y_ref.dtype)
        return carry

    lax.fori_loop(0, T // RG_TCH, emit, 0)


def _rglru_call(xa, ga, cw, cb, wa, wx, ba, bx, lam, h0):
    B, T, _ = xa.shape
    nj = RNN_WIDTH // RG_CW
    seq = pl.BlockSpec((1, T, RG_CW), lambda b, j: (b, 0, j))
    vec8 = pl.BlockSpec((SUBLANES, RG_CW), lambda b, j: (0, j))
    wsp = pl.BlockSpec((2, RG_CW, RG_CW), lambda b, j: (0, j, j))
    st = pl.BlockSpec((1, SUBLANES, RG_CW), lambda b, j: (b, 0, j))
    return pl.pallas_call(
        functools.partial(_rglru_kernel, T=T),
        out_shape=[jax.ShapeDtypeStruct((B, T, RNN_WIDTH), BF16),
                   jax.ShapeDtypeStruct((B, SUBLANES, RNN_WIDTH), F32)],
        grid=(B, nj),
        in_specs=[seq, seq, vec8, pl.BlockSpec((1, RG_CW), lambda b, j: (0, j)),
                  wsp, wsp, vec8, vec8, vec8, st],
        out_specs=[seq, st],
        scratch_shapes=[pltpu.VMEM((T + 2 * SUBLANES, RG_CW), F32)]
        + [pltpu.VMEM((RG_CW // LANES, T + RG_SEGS * SUBLANES, LANES), F32)] * 4
        + [pltpu.VMEM((RG_SEGS, RG_CW), F32)] * 2,
        compiler_params=_cparams("parallel", "parallel"),
        name="rglru",
    )(xa, ga, cw, cb, wa, wx, ba, bx, lam, h0)


def _attend_slabs(jobs):
    M = jobs[0][0].shape[0]
    lo = lax.broadcasted_iota(jnp.int32, (M, LANES), 1) < HEAD_DIM
    scores = []
    for q2, srcs, _ in jobs:
        zero = jnp.zeros_like(q2)
        qs = jnp.concatenate([jnp.where(lo, q2, zero), jnp.where(lo, zero, q2)], 0)
        ss = []
        for k, _, bias in srcs:
            s = _dot_nt(qs, k)
            ss.append(s if bias is None else s + bias)
        scores.append(ss)
    maxima = []
    for (_, _, sink_col), ss in zip(jobs, scores):
        m = jnp.max(ss[0], -1, keepdims=True)
        for s in ss[1:]:
            m = jnp.maximum(m, jnp.max(s, -1, keepdims=True))
        maxima.append(m if sink_col is None else jnp.maximum(m, sink_col))
    outs = []
    for (_, srcs, sink_col), ss, m in zip(jobs, scores, maxima):
        den = None
        o = None
        for s, (_, v, _) in zip(ss, srcs):
            e = jnp.exp2(s - m)
            d_ = jnp.sum(e, -1, keepdims=True)
            o_ = _dot(e.astype(BF16), v)
            den = d_ if den is None else den + d_
            o = o_ if o is None else o + o_
        if sink_col is not None:
            den = den + jnp.exp2(sink_col - m)
        o = o / den
        outs.append(jnp.where(lo, o[:M], o[M:]))
    return outs


def _sink_col(sink_ref, j, M):
    r = lax.broadcasted_iota(jnp.int32, (2 * M, 1), 0)
    return jnp.where(r < M, sink_ref[2 * j], sink_ref[2 * j + 1]) * LOG2E


def _slab(j):
    return slice(j * LANES, (j + 1) * LANES)


def _attn_ctx_kernel(sink_ref, nq_ref, nk_ref, nv_ref, sq_ref, skd_ref, svd_ref, yb_ref, yc_ref):
    M = nq_ref.shape[1]
    nslab = W512 // LANES
    jobs = []
    for j in range(nslab):
        k2 = nk_ref[0, 0, :, _slab(j)].astype(BF16)
        v2 = nv_ref[0, 0, :, _slab(j)].astype(BF16)
        jobs.append((nq_ref[0, :, _slab(j)], [(k2, v2, None)], None))
    for j in range(nslab):
        g = j // 2
        jobs.append((sq_ref[0, :, _slab(j)], [(skd_ref[0, :, _slab(g)], svd_ref[0, :, _slab(g)], None)],
                     _sink_col(sink_ref, j, M)))
    outs = _attend_slabs(jobs)
    for j in range(nslab):
        yb_ref[0, :, _slab(j)] = outs[j].astype(yb_ref.dtype)
        yc_ref[0, :, _slab(j)] = outs[nslab + j].astype(yc_ref.dtype)


def _attn_ctx_call(sink, nq, nk, nv, sq, skd, svd, l):
    B, T, _ = nq.shape
    s512 = pl.BlockSpec((1, T, W512), lambda b: (b, 0, 0))
    s256 = pl.BlockSpec((1, T, 2 * LANES), lambda b: (b, 0, 0))
    cache = pl.BlockSpec((1, 1, T, W512), lambda b: (b, l, 0, 0))
    return pl.pallas_call(
        _attn_ctx_kernel,
        out_shape=[jax.ShapeDtypeStruct((B, T, W512), BF16)] * 2,
        grid=(B,),
        in_specs=[pl.BlockSpec(memory_space=pltpu.SMEM), s512, cache, cache, s512, s256, s256],
        out_specs=[s512, s512],
        compiler_params=_cparams("parallel"),
        name="attn_ctx",
    )(sink, nq, nk, nv, sq, skd, svd)


NAT_NLOC = NAT_KR * GRID_W


def _nat_lat_kernel(q_ref, k_ref, v_ref, ck_ref, cv_ref, bias_ref, y_ref, *, rows):
    r = pl.program_id(1)
    rstart = jnp.clip(r - NAT_KR // 2, 0, rows - NAT_KR)
    d = r - rstart
    kbase = pl.multiple_of(rstart * GRID_W, GRID_W)
    jobs = []
    for j in range(W512 // LANES):
        k2 = k_ref[0, pl.ds(kbase, NAT_NLOC), _slab(j)]
        v2 = v_ref[0, pl.ds(kbase, NAT_NLOC), _slab(j)]
        jobs.append((q_ref[0, :, _slab(j)],
                     [(k2, v2, bias_ref[j, d]), (ck_ref[0, 0, :, _slab(j)], cv_ref[0, 0, :, _slab(j)], None)], None))
    for j, o in enumerate(_attend_slabs(jobs)):
        y_ref[0, :, _slab(j)] = o.astype(y_ref.dtype)


def _nat_lat_call(q, k, v, ck, cv, bias, l):
    B, T, _ = q.shape
    rows = T // GRID_W
    P = ck.shape[2]
    qs = pl.BlockSpec((1, GRID_W, W512), lambda b, r: (b, r, 0))
    full = pl.BlockSpec((1, T, W512), lambda b, r: (b, 0, 0))
    cs = pl.BlockSpec((1, 1, P, W512), lambda b, r: (b, l, 0, 0))
    return pl.pallas_call(
        functools.partial(_nat_lat_kernel, rows=rows),
        out_shape=jax.ShapeDtypeStruct((B, T, W512), BF16),
        grid=(B, rows),
        in_specs=[qs, full, full, cs, cs, _const_spec(bias.shape)],
        out_specs=qs,
        compiler_params=_cparams("parallel", "arbitrary"),
        name="nat_lat",
    )(q, k, v, ck, cv, bias)


def _swa_band_masks():
    i = np.arange(2 * SWA_BLOCK)[:, None] % SWA_BLOCK
    j = np.arange(SWA_SPAN)[None, :]
    tabs = []
    for span_start in (0, -SWA_WINDOW, -2 * SWA_WINDOW):
        dist = span_start + j - i
        tabs.append(np.where(np.abs(dist) <= SWA_WINDOW, 0.0, NEG_INF))
    return jnp.asarray(np.stack(tabs), F32)


def _swa_lat_kernel(sink_ref, q_ref, kd_ref, vd_ref, ckd_ref, cvd_ref, band_ref, y_ref, *, T):
    blk = pl.program_id(1)
    M = SWA_BLOCK
    start = blk * SWA_BLOCK
    ks = pl.multiple_of(jnp.clip(start - SWA_WINDOW, 0, T - SWA_SPAN), SWA_BLOCK)
    bias = band_ref[jnp.where(blk == 0, 0, jnp.where(blk == T // SWA_BLOCK - 1, 2, 1))]
    jobs = []
    for j in range(W512 // LANES):
        g = j // 2
        jobs.append((q_ref[0, :, _slab(j)],
                     [(kd_ref[0, pl.ds(ks, SWA_SPAN), _slab(g)], vd_ref[0, pl.ds(ks, SWA_SPAN), _slab(g)], bias),
                      (ckd_ref[0, 0, :, _slab(g)], cvd_ref[0, 0, :, _slab(g)], None)],
                     _sink_col(sink_ref, j, M)))
    for j, o in enumerate(_attend_slabs(jobs)):
        y_ref[0, :, _slab(j)] = o.astype(y_ref.dtype)


def _swa_lat_call(sink, q, kd, vd, ckd, cvd, l):
    B, T, _ = q.shape
    P = ckd.shape[2]
    qs = pl.BlockSpec((1, SWA_BLOCK, W512), lambda b, i: (b, i, 0))
    full = pl.BlockSpec((1, T, 2 * LANES), lambda b, i: (b, 0, 0))
    cs = pl.BlockSpec((1, 1, P, 2 * LANES), lambda b, i: (b, l, 0, 0))
    assert T // SWA_BLOCK >= 3
    band = _swa_band_masks()
    return pl.pallas_call(
        functools.partial(_swa_lat_kernel, T=T),
        out_shape=jax.ShapeDtypeStruct((B, T, W512), BF16),
        grid=(B, T // SWA_BLOCK),
        in_specs=[pl.BlockSpec(memory_space=pltpu.SMEM), qs, full, full, cs, cs, _const_spec(band.shape)],
        out_specs=qs,
        compiler_params=_cparams("parallel", "arbitrary"),
        name="swa_lat",
    )(sink, q, kd, vd, ckd, cvd, band)


FN_N2 = 256
FN_GP = 2


def _fourier_kernel(x_ref, cc_ref, m_ref, y_ref, *scratch, n1):
    gw = FN_GP * FNET_GC
    cc = cc_ref[...]
    for gp in range(FNET_GROUPS // FN_GP):
        parts = []
        for t1 in range(n1):
            ws = []
            for g in range(FN_GP):
                lo = t1 * FNET_WIDTH + (gp * FN_GP + g) * FNET_GC
                ws.append(_dot(x_ref[0, :, lo:lo + FNET_GC], cc))
            wr = jnp.concatenate([w[:, :FNET_GC] for w in ws], -1)
            wi = jnp.concatenate([w[:, FNET_GC:] for w in ws], -1)
            v = jnp.concatenate([wr, wi], 0).astype(BF16)
            b = _dot(m_ref[t1], v)
            if n1 == 1:
                y_ref[0, :, gp * gw:(gp + 1) * gw] = b.astype(y_ref.dtype)
            else:
                scratch[0][t1] = b
        if n1 > 1:
            _dft16_real(scratch[0], y_ref, gp * gw, gw)


FN_RCH = 16


def _dft16_real(bs, y_ref, col0, gw):
    tw = {m: (math.cos(2 * math.pi * m / 16), math.sin(2 * math.pi * m / 16)) for m in (1, 2, 3, 6, 9)}

    def body(rc, carry):
        r0 = pl.multiple_of(rc * FN_RCH, FN_RCH)
        re = [bs[t1, pl.ds(r0, FN_RCH), :] for t1 in range(16)]
        im = [bs[t1, pl.ds(FN_N2 + r0, FN_RCH), :] for t1 in range(16)]
        h_re = [[None] * 4 for _ in range(4)]
        h_im = [[None] * 4 for _ in range(4)]
        for b in range(4):
            s02r, s02i = re[b] + re[8 + b], im[b] + im[8 + b]
            d02r, d02i = re[b] - re[8 + b], im[b] - im[8 + b]
            s13r, s13i = re[4 + b] + re[12 + b], im[4 + b] + im[12 + b]
            d13r, d13i = re[4 + b] - re[12 + b], im[4 + b] - im[12 + b]
            g = [(s02r + s13r, s02i + s13i), (d02r + d13i, d02i - d13r),
                 (s02r - s13r, s02i - s13i), (d02r - d13i, d02i + d13r)]
            for c in range(4):
                gr, gi = g[c]
                m = b * c
                if m == 0:
                    hr, hi = gr, gi
                elif m == 4:
                    hr, hi = gi, -gr
                else:
                    cs, sn = tw[m]
                    hr = cs * gr + sn * gi
                    hi = (cs * gi - sn * gr) if b % 2 else None
                h_re[b][c], h_im[b][c] = hr, hi
        for c in range(4):
            p = h_re[0][c] + h_re[2][c]
            q = h_re[0][c] - h_re[2][c]
            r = h_re[1][c] + h_re[3][c]
            s = h_im[1][c] - h_im[3][c]
            for d, val in enumerate((p + r, q + s, p - r, q - s)):
                k1 = c + 4 * d
                y_ref[0, pl.ds(k1 * FN_N2 + r0, FN_RCH), col0:col0 + gw] = val.astype(y_ref.dtype)
        return carry

    lax.fori_loop(0, FN_N2 // FN_RCH, body, 0)


def _dft_consts(T):
    n1 = T // FN_N2
    j = np.arange(FNET_GC)
    ang = 2 * np.pi * np.outer(j, j) / FNET_GC
    sc = 1.0 / math.sqrt(FNET_GC)
    cc = np.concatenate([np.cos(ang) * sc, -np.sin(ang) * sc], 1)
    k2 = np.arange(FN_N2)[:, None]
    t2 = np.arange(FN_N2)[None, :]
    st = 1.0 / math.sqrt(T)
    mats = []
    for t1 in range(n1):
        th = 2 * np.pi * ((k2 * (n1 * t2 + t1)) % T) / T
        c, s = np.cos(th) * st, np.sin(th) * st
        top = np.concatenate([c, s], 1)
        mats.append(top if n1 == 1 else np.concatenate([top, np.concatenate([-s, c], 1)], 0))
    return n1, jnp.asarray(cc, BF16), jnp.asarray(np.stack(mats), BF16)


def _fourier_call(xv, T):
    B = xv.shape[0]
    n1, cc, mats = _dft_consts(T)
    assert n1 in (1, 16)
    gw = FN_GP * FNET_GC
    scratch = [pltpu.VMEM((n1, 2 * FN_N2, gw), F32)] if n1 > 1 else []
    return pl.pallas_call(
        functools.partial(_fourier_kernel, n1=n1),
        out_shape=jax.ShapeDtypeStruct((B, T, FNET_WIDTH), BF16),
        grid=(B,),
        in_specs=[pl.BlockSpec((1, FN_N2, n1 * FNET_WIDTH), lambda b: (b, 0, 0)),
                  _const_spec(cc.shape), _const_spec(mats.shape)],
        out_specs=pl.BlockSpec((1, T, FNET_WIDTH), lambda b: (b, 0, 0)),
        scratch_shapes=scratch,
        compiler_params=_cparams("parallel"),
        name="fourier",
    )(xv, cc, mats)


def _route(scores, biased):
    one, zero = jnp.float32(1.0), jnp.float32(0.0)
    in2, gscore = [], []
    for g in range(N_EXPERT_GROUPS):
        vs = [biased[g * EXPERTS_PER_GROUP + j:g * EXPERTS_PER_GROUP + j + 1] for j in range(EXPERTS_PER_GROUP)]
        gs = None
        for j in range(EXPERTS_PER_GROUP):
            rank = None
            for i in range(EXPERTS_PER_GROUP):
                if i == j:
                    continue
                beats = (vs[i] >= vs[j]) if i < j else (vs[i] > vs[j])
                t = jnp.where(beats, one, zero)
                rank = t if rank is None else rank + t
            keep = jnp.where(rank < 2.0, one, zero)
            in2.append(keep)
            t = keep * vs[j]
            gs = t if gs is None else gs + t
        gscore.append(gs)
    rows, sel = [], []
    for g in range(N_EXPERT_GROUPS):
        lost = None
        for i in range(N_EXPERT_GROUPS):
            if i == g:
                continue
            beats = (gscore[i] >= gscore[g]) if i < g else (gscore[i] > gscore[g])
            t = jnp.where(beats, one, zero)
            lost = t if lost is None else lost + t
        gsel = jnp.where(lost < 1.0, one, zero)
        for j in range(EXPERTS_PER_GROUP):
            e = g * EXPERTS_PER_GROUP + j
            sel.append(gsel * in2[e])
            rows.append(sel[-1] * scores[e:e + 1])
    tot = rows[0]
    for rr in rows[1:]:
        tot = tot + rr
    return jnp.concatenate(rows, 0) / tot, jnp.concatenate(sel, 0)


HP_ROWS = D_MODEL // 2 // LANES
Y_ROWS = D_MODEL // LANES


def _pack_halves(v):
    w = v.shape[1] // 2
    lo = lax.bitcast_convert_type(v[:, :w].astype(F32), jnp.uint32)
    hi = lax.bitcast_convert_type(v[:, w:].astype(F32), jnp.uint32)
    return (lo >> 16) | hi


def _unpack_halves(p):
    lo = lax.bitcast_convert_type(p << 16, F32).astype(BF16)
    hi = lax.bitcast_convert_type(p & jnp.uint32(0xFFFF0000), F32).astype(BF16)
    return lo, hi


def _merge_kernel(x_ref, mod_ref, ya_ref, yb_ref, yc_ref, yd_ref, wg_ref, wb_ref, wo_ref,
                  lng_ref, lnb_ref, wr_ref, rb_ref, x1_ref, hp_ref, rt_ref, *, nsub):
    m = mod_ref[0]
    sub = x_ref.shape[0] // nsub
    for k in range(nsub):
        rows = slice(k * sub, (k + 1) * sub)
        x = x_ref[rows, :]
        h = (_ln(x) * (1.0 + m[1:2]) + m[0:1]).astype(BF16)
        merged = None
        for b, y_ref in enumerate((ya_ref, yb_ref, yc_ref, yd_ref)):
            gate = 0.5 * jnp.tanh(_dot(h, wg_ref[0, :, b * D_MODEL:(b + 1) * D_MODEL])) + 0.5
            term = gate * _dot(y_ref[rows, :], wb_ref[0, b])
            merged = term if merged is None else merged + term
        out = _dot(merged.astype(BF16), wo_ref[0])
        x1 = _ln(ALPHA * x + m[2:3] * out) * lng_ref[...] + lnb_ref[...]
        x1_ref[rows, :] = x1
        h2 = (_ln(x1) * (1.0 + m[4:5]) + m[3:4]).astype(BF16)
        packed = _pack_halves(h2)
        for c in range(HP_ROWS):
            hp_ref[pl.ds(k * sub * HP_ROWS + c, sub, stride=HP_ROWS), :] = packed[:, c * LANES:(c + 1) * LANES]
        logits = _dot_nt(wr_ref[...], h2)
        e = jnp.exp(logits - jnp.max(logits, 0, keepdims=True))
        scores = e / jnp.sum(e, 0, keepdims=True)
        comb, sel = _route(scores, scores + rb_ref[...])
        rt_ref[:, rows] = jnp.concatenate([comb, sel], 0)


def _merge_call(x, mod, ys, wg, wb, wo, l, lng, lnb, wr_t, rb, T, TM):
    N = x.shape[0]
    tpb = T // TM
    tok = lambda w: pl.BlockSpec((TM, w), lambda i: (i, 0))
    return pl.pallas_call(
        functools.partial(_merge_kernel, nsub=max(1, TM // SUB_TM)),
        out_shape=[jax.ShapeDtypeStruct((N, D_MODEL), F32), jax.ShapeDtypeStruct((N * HP_ROWS, LANES), jnp.uint32),
                   jax.ShapeDtypeStruct((2 * N_EXPERTS, N), F32)],
        grid=(N // TM,),
        in_specs=[tok(D_MODEL), pl.BlockSpec((1, SUBLANES, D_MODEL), lambda i: (i // tpb, 0, 0)),
                  tok(W512), tok(W512), tok(W512), tok(W512),
                  _layer_spec(wg.shape, l), _layer_spec(wb.shape, l), _layer_spec(wo.shape, l),
                  _const_spec(lng.shape), _const_spec(lnb.shape), _const_spec(wr_t.shape), _const_spec(rb.shape)],
        out_specs=[tok(D_MODEL), pl.BlockSpec((TM * HP_ROWS, LANES), lambda i: (i, 0)),
                   pl.BlockSpec((2 * N_EXPERTS, TM), lambda i: (0, i))],
        compiler_params=_cparams("parallel"),
        name="merge",
    )(x, mod, *ys, wg, wb, wo, lng, lnb, wr_t, rb)


MOE_TM = 2048
MOE_RB = 256
MOE_KC = 4
MOE_CH = MOE_TM // MOE_KC
MOE_SLOTS = 2 * MOE_TM + N_EXPERTS * SUBLANES + MOE_RB
MOE_PR = 2 * MOE_TM // LANES
MOE_UNROLL = 8


def _route_kernel(rt_ref, u_ref, pos_ref, wts_ref, seg_ref):
    comb = rt_ref[0:N_EXPERTS, :]
    sel = rt_ref[N_EXPERTS:2 * N_EXPERTS, :]
    nb = MOE_TM // LANES
    stacked = jnp.concatenate([sel[:, b * LANES:(b + 1) * LANES] for b in range(nb)], 0)
    within = _dot(stacked.astype(BF16), u_ref[...])
    tot = jnp.sum(stacked, -1, keepdims=True)
    base = jnp.zeros((N_EXPERTS, 1), F32)
    bases = []
    for b in range(nb):
        bases.append(base)
        base = base + tot[b * N_EXPERTS:(b + 1) * N_EXPERTS]
    cnt = base
    padded = jnp.floor((cnt + (SUBLANES - 1.0)) * (1.0 / SUBLANES)) * SUBLANES
    rowi = lax.broadcasted_iota(jnp.int32, (N_EXPERTS, 1), 0)
    off = jnp.zeros((N_EXPERTS, 1), F32)
    for e in range(N_EXPERTS - 1):
        off = off + jnp.where(rowi > e, padded[e:e + 1, :], 0.0)
    pos_rows, wts_rows = [], []
    for b in range(nb):
        slot = within[b * N_EXPERTS:(b + 1) * N_EXPERTS] + (bases[b] + off)
        seen = jnp.zeros((1, LANES), F32)
        acc = [jnp.zeros((1, LANES), F32) for _ in range(4)]
        for e in range(N_EXPERTS):
            s_e = sel[e:e + 1, b * LANES:(b + 1) * LANES]
            c_e = comb[e:e + 1, b * LANES:(b + 1) * LANES]
            first = jnp.where(seen == 0.0, s_e, 0.0)
            second = jnp.where(seen == 1.0, s_e, 0.0)
            acc[0] = acc[0] + first * slot[e:e + 1]
            acc[1] = acc[1] + second * slot[e:e + 1]
            acc[2] = acc[2] + first * c_e
            acc[3] = acc[3] + second * c_e
            seen = seen + s_e
        pos_rows += acc[0:2]
        wts_rows += acc[2:4]
    pos_ref[...] = (jnp.concatenate(pos_rows, 0) * float(HP_ROWS)).astype(jnp.int32).reshape(2 * MOE_TM)
    wts_ref[...] = jnp.concatenate(wts_rows, 0).reshape(2 * MOE_TM)
    lane = lax.broadcasted_iota(jnp.int32, (N_EXPERTS, LANES), 1)
    diag = lane == lax.broadcasted_iota(jnp.int32, (N_EXPERTS, LANES), 0)
    off_row = jnp.sum(jnp.where(diag, off, 0.0), 0, keepdims=True)
    cnt_row = jnp.sum(jnp.where(diag, cnt, 0.0), 0, keepdims=True)
    seg_ref[0] = jnp.concatenate([off_row, cnt_row, jnp.zeros((SUBLANES - 2, LANES), F32)], 0).astype(jnp.int32)


def _route_call(rt):
    N = rt.shape[1]
    nt = N // MOE_TM
    u = jnp.asarray(np.triu(np.ones((LANES, LANES), np.float32), 1), BF16)
    return pl.pallas_call(
        _route_kernel,
        out_shape=[jax.ShapeDtypeStruct((nt * 2 * MOE_TM,), jnp.int32),
                   jax.ShapeDtypeStruct((nt * 2 * MOE_TM,), F32),
                   jax.ShapeDtypeStruct((nt, SUBLANES, LANES), jnp.int32)],
        grid=(nt,),
        in_specs=[pl.BlockSpec((2 * N_EXPERTS, MOE_TM), lambda i: (0, i)), _const_spec(u.shape)],
        out_specs=[pl.BlockSpec((2 * MOE_TM,), lambda i: (i,)),
                   pl.BlockSpec((2 * MOE_TM,), lambda i: (i,)),
                   pl.BlockSpec((1, SUBLANES, LANES), lambda i: (i, 0, 0))],
        compiler_params=_cparams("parallel"),
        name="route",
    )(rt, u)


def _table_index(it):
    per_block = LANES // MOE_UNROLL
    return (it // per_block) * (2 * LANES) + (it % per_block) * MOE_UNROLL


def _moe_kernel(pos_ref, wts_ref, seg_ref, hp_ref, w1_ref, w3_ref, w2_ref, x1_ref, mod_ref, lng_ref, lnb_ref,
                o_ref, xs, ys, oc):
    s = pl.program_id(1)
    half = D_MODEL // 2

    @pl.when(s == 0)
    def _dispatch():
        xs[...] = jnp.zeros_like(xs)

        def body(it, carry):
            t0 = pl.multiple_of(it * (MOE_UNROLL * HP_ROWS), MOE_UNROLL * HP_ROWS)
            i0 = _table_index(it)
            for j in range(MOE_UNROLL):
                row = hp_ref[pl.ds(t0 + HP_ROWS * j, HP_ROWS), :]
                xs[pl.ds(pl.multiple_of(pos_ref[i0 + j], HP_ROWS), HP_ROWS), :] = row
                xs[pl.ds(pl.multiple_of(pos_ref[i0 + LANES + j], HP_ROWS), HP_ROWS), :] = row
            return carry

        lax.fori_loop(0, MOE_TM // MOE_UNROLL, body, 0)

    @pl.when(s < N_EXPERTS)
    def _expert():
        off = seg_ref[0, 0, s]
        cnt = seg_ref[0, 1, s]
        nblk = (cnt + (MOE_RB // 2 - 1)) // MOE_RB

        def ffn_rows(r0, nrows):
            parts = [_unpack_halves(xs[pl.ds(pl.multiple_of(r0 * HP_ROWS, SUBLANES) + c, nrows, stride=HP_ROWS), :])
                     for c in range(HP_ROWS)]
            lo = jnp.concatenate([p[0] for p in parts], -1)
            hi = jnp.concatenate([p[1] for p in parts], -1)
            a = _dot(lo, w1_ref[0, 0, :half, :]) + _dot(hi, w1_ref[0, 0, half:, :])
            g = _dot(lo, w3_ref[0, 0, :half, :]) + _dot(hi, w3_ref[0, 0, half:, :])
            act = (a * jax.nn.sigmoid(a) * g).astype(BF16)
            y = _dot(act, w2_ref[0, 0])
            y0 = pl.multiple_of(r0 * Y_ROWS, SUBLANES)
            for c in range(Y_ROWS):
                ys[pl.ds(y0 + c, nrows, stride=Y_ROWS), :] = y[:, c * LANES:(c + 1) * LANES]

        def blk(i, carry):
            ffn_rows(pl.multiple_of(off + i * MOE_RB, SUBLANES), MOE_RB)
            return carry

        lax.fori_loop(0, nblk, blk, 0)

        @pl.when(cnt > nblk * MOE_RB)
        def _tail():
            ffn_rows(pl.multiple_of(off + nblk * MOE_RB, SUBLANES), MOE_RB // 2)

    @pl.when(s >= N_EXPERTS)
    def _combine():
        it0 = (s - N_EXPERTS) * (MOE_CH // MOE_UNROLL)

        def body(it, carry):
            t0 = pl.multiple_of(it * (MOE_UNROLL * Y_ROWS), MOE_UNROLL * Y_ROWS)
            i0 = _table_index(it0 + it)
            for j in range(MOE_UNROLL):
                p0 = pl.multiple_of(pos_ref[i0 + j] * (Y_ROWS // HP_ROWS), Y_ROWS)
                p1 = pl.multiple_of(pos_ref[i0 + LANES + j] * (Y_ROWS // HP_ROWS), Y_ROWS)
                oc[pl.ds(t0 + Y_ROWS * j, Y_ROWS), :] = (wts_ref[i0 + j] * ys[pl.ds(p0, Y_ROWS), :]
                                                         + wts_ref[i0 + LANES + j] * ys[pl.ds(p1, Y_ROWS), :])
            return carry

        lax.fori_loop(0, MOE_CH // MOE_UNROLL, body, 0)
        moe = jnp.concatenate([oc[pl.ds(c, MOE_CH, stride=Y_ROWS), :] for c in range(Y_ROWS)], -1)
        m = mod_ref[0]
        o_ref[...] = _ln(ALPHA * x1_ref[...] + m[5:6] * moe) * lng_ref[...] + lnb_ref[...]


def _moe_call(pos, wts, seg, hp, w1, w3, w2, l, x1, mod, lng, lnb, T):
    N = x1.shape[0]
    nt = N // MOE_TM
    flat = pl.BlockSpec((2 * MOE_TM,), lambda i, s: (i,), memory_space=pltpu.SMEM)
    wspec = lambda shp: pl.BlockSpec((1, 1) + shp, lambda i, s: (l, jnp.minimum(s, N_EXPERTS - 1), 0, 0))
    chunk = pl.BlockSpec((MOE_CH, D_MODEL),
                         lambda i, s: (i * MOE_KC + jnp.clip(s - N_EXPERTS, 0, MOE_KC - 1), 0))
    return pl.pallas_call(
        _moe_kernel,
        out_shape=jax.ShapeDtypeStruct((N, D_MODEL), F32),
        grid=(nt, N_EXPERTS + MOE_KC),
        in_specs=[flat, flat,
                  pl.BlockSpec((1, SUBLANES, LANES), lambda i, s: (i, 0, 0), memory_space=pltpu.SMEM),
                  pl.BlockSpec((MOE_TM * HP_ROWS, LANES), lambda i, s: (i, 0)),
                  wspec((D_MODEL, D_EXPERT)), wspec((D_MODEL, D_EXPERT)), wspec((D_EXPERT, D_MODEL)),
                  chunk, pl.BlockSpec((1, SUBLANES, D_MODEL), lambda i, s: ((i * MOE_TM) // T, 0, 0)),
                  _const_spec(lng.shape), _const_spec(lnb.shape)],
        out_specs=chunk,
        scratch_shapes=[pltpu.VMEM((MOE_SLOTS * HP_ROWS, LANES), jnp.uint32),
                        pltpu.VMEM((MOE_SLOTS * Y_ROWS, LANES), F32),
                        pltpu.VMEM((MOE_CH * Y_ROWS, LANES), F32)],
        compiler_params=_cparams("parallel", "arbitrary"),
        name="moe",
    )(pos, wts, seg, hp, w1, w3, w2, x1, mod, lng, lnb)


def _rope_tables(T):
    t = jnp.arange(T)
    nf = HEAD_DIM // 4
    inv = ROPE_BASE ** (-jnp.arange(nf, dtype=F32) / nf)
    ar = (t // GRID_W).astype(F32)[:, None] * inv
    ac = (t % GRID_W).astype(F32)[:, None] * inv
    cos = jnp.concatenate([jnp.cos(ar), jnp.cos(ar), jnp.cos(ac), jnp.cos(ac)], -1)
    sin = jnp.concatenate([-jnp.sin(ar), jnp.sin(ar), -jnp.sin(ac), jnp.sin(ac)], -1)
    return jnp.tile(cos, (1, LANES // HEAD_DIM)), jnp.tile(sin, (1, LANES // HEAD_DIM))


def _nat_bias_table(rpb):
    q = np.arange(GRID_W)
    kc = np.arange(GRID_W)
    cstart = np.clip(q - NAT_KC // 2, 0, GRID_W - NAT_KC)
    ok = (kc[None, :] >= cstart[:, None]) & (kc[None, :] < cstart[:, None] + NAT_KC)
    cidx = np.clip(kc[None, :] - q[:, None] + NAT_KC - 1, 0, 2 * NAT_KC - 2)
    onehot = (np.arange(2 * NAT_KC - 1)[:, None, None] == cidx[None]).astype(np.float32)
    t = jnp.einsum('hrc,cqk->hqrk', rpb.astype(F32), jnp.asarray(onehot), precision=lax.Precision.HIGHEST)
    t = jnp.where(jnp.asarray(ok)[None, :, None, :], t * LOG2E, NEG_INF)
    per_d = [t[:, :, NAT_KR - 1 - d:2 * NAT_KR - 1 - d, :].reshape(NAT_HEADS // 2, 2 * GRID_W, NAT_KR * GRID_W)
             for d in range(NAT_KR)]
    return jnp.stack(per_d, 1)


def _dup_heads(a):
    a = jnp.broadcast_to(a[..., :, None, :], a.shape[:-1] + (2, a.shape[-1]))
    return a.reshape(a.shape[:-3] + (-1,))


def _block_diag(w):
    eye = jnp.eye(RNN_BLOCKS, dtype=w.dtype)
    return jnp.einsum('dnio,nm->dnimo', w, eye).reshape(2, RNN_WIDTH, RNN_WIDTH)


def _pad_rows(a, rows=SUBLANES):
    return jnp.pad(a, ((0, rows - a.shape[0]),) + ((0, 0),) * (a.ndim - 1))


def kernel(x_prompt, x_sample, c, cache_nat_k, cache_nat_v, cache_swa_k, cache_swa_v, state_rglru, c_ctx,
           w_ada, b_ada, w_in, rg_conv_w, rg_conv_b, rg_wa, rg_ba, rg_wx, rg_bx, rg_lambda, nat_rpb,
           swa_sink, w_branch, w_out, ln_g, ln_b, w_router, router_bias, w1, w3, w2):
    B_c, T_c, _ = x_prompt.shape
    B_l, T_l, _ = x_sample.shape
    P = cache_nat_k.shape[2]

    cv = jnp.concatenate([c_ctx[None], c, jnp.zeros((ADA_ROWS - 1 - B_l, D_MODEL), F32)], 0)
    mod_all = _ada_call(cv, w_ada, b_ada).reshape(DEPTH, ADA_ROWS, 6, D_MODEL)
    mod_all = jnp.pad(mod_all, ((0, 0), (0, 0), (0, SUBLANES - 6), (0, 0)))

    sk0 = 3072
    sv0 = sk0 + SWA_KV_HEADS * HEAD_DIM
    xf0 = sv0 + SWA_KV_HEADS * HEAD_DIM
    w_in_b = w_in.astype(BF16)
    head = lambda c0, g: w_in_b[:, :, c0 + g * HEAD_DIM:c0 + (g + 1) * HEAD_DIM]
    w_tail_ctx = jnp.concatenate([head(c0, g) for c0 in (sk0, sv0) for g in (0, 0, 1, 1)], -1)
    w_tail_lat = jnp.concatenate([w_in_b[:, :, xf0:GATE_OFF], w_tail_ctx], -1)
    w_gate = (0.5 * w_in[:, :, GATE_OFF:]).astype(BF16)
    w_branch_b = w_branch.astype(BF16)
    w_out_b = w_out.astype(BF16)
    w1_b, w3_b, w2_b = w1.astype(BF16), w3.astype(BF16), w2.astype(BF16)
    wr_t = w_router.T.astype(BF16)
    wa_bd = jnp.stack([_block_diag(0.5 * rg_wa[l]) for l in range(DEPTH)]).astype(BF16)
    wx_bd = jnp.stack([_block_diag(0.5 * rg_wx[l]) for l in range(DEPTH)]).astype(BF16)

    plan_ctx = ((0, 0, 512, F32, 1.0, False, False, None), (0, 512, 512, F32, 1.0, False, False, None),
                (0, 1024, 512, BF16, Q_SCALE, False, False, None), (0, 1536, 512, F32, 1.0, False, False, 0),
                (0, 2048, 512, F32, 1.0, False, False, 1), (0, 2560, 512, BF16, Q_SCALE, False, False, None),
                (0, sk0, 128, F32, 1.0, False, False, 2), (0, sv0, 128, F32, 1.0, False, False, 3),
                (0, xf0, 512, BF16, 1.0, False, True, None),
                (1, 0, 256, BF16, 1.0, False, False, None), (1, 256, 256, BF16, 1.0, False, False, None))
    plan_lat = ((0, 0, 512, F32, 1.0, False, False, None), (0, 512, 512, F32, 1.0, False, False, None),
                (0, 1024, 512, BF16, Q_SCALE, False, False, None), (0, 1536, 512, BF16, 1.0, False, False, None),
                (0, 2048, 512, BF16, 1.0, False, False, None), (0, 2560, 512, BF16, Q_SCALE, True, False, None),
                (1, 0, 512, BF16, 1.0, False, True, None),
                (1, 512, 256, BF16, 1.0, True, False, None), (1, 768, 256, BF16, 1.0, False, False, None))

    rope_tabs = _rope_tables(T_l)
    ck_nat = cache_nat_k.reshape(B_l, DEPTH, P, W512).astype(BF16)
    cv_nat = cache_nat_v.reshape(B_l, DEPTH, P, W512).astype(BF16)
    ckd_swa = _dup_heads(cache_swa_k).astype(BF16)
    cvd_swa = _dup_heads(cache_swa_v).astype(BF16)
    state8 = jnp.pad(state_rglru, ((0, 0), (0, 0), (0, SUBLANES - 2), (0, 0)))
    zero_state = jnp.zeros((B_c, SUBLANES, RNN_WIDTH), F32)

    def layer(x, l, ctx_pass, caches=None):
        B, T = (B_c, T_c) if ctx_pass else (B_l, T_l)
        TM = min(T, TOKEN_TM)
        modb = jnp.broadcast_to(mod_all[l, 0:1], (B, SUBLANES, D_MODEL)) if ctx_pass else mod_all[l, 1:1 + B_l]
        if ctx_pass:
            xa, ga, nq, nk, nv, sq, sk, sv, xf, skd, svd = _pre_call(
                x, modb, w_in_b, GATE_OFF, w_tail_ctx, l, plan_ctx, B, T, TM, None, caches)
        else:
            xa, ga, nq, nk, nv, sq, xf, skd, svd = _pre_call(
                x, modb, w_in_b, sk0, w_tail_lat, l, plan_lat, B, T, TM, rope_tabs)
        r3 = lambda a: a.reshape(B, T, a.shape[-1])
        h0 = zero_state if ctx_pass else state8[:, l]
        ya, hfin = _rglru_call(
            r3(xa), r3(ga), _pad_rows(rg_conv_w[l]), rg_conv_b[l][None], wa_bd[l], wx_bd[l],
            _pad_rows(0.5 * rg_ba[l]), _pad_rows(0.5 * rg_bx[l]), _pad_rows(rg_lambda[l]), h0)
        if ctx_pass:
            yb, yc = _attn_ctx_call(swa_sink[l], r3(nq), nk, nv, r3(sq), r3(skd), r3(svd), l)
        else:
            yb = _nat_lat_call(r3(nq), r3(nk), r3(nv), ck_nat, cv_nat, _nat_bias_table(nat_rpb[l]), l)
            yc = _swa_lat_call(swa_sink[l], r3(sq), r3(skd), r3(svd), ckd_swa, cvd_swa, l)
        yd = _fourier_call(xf, T)
        f2 = lambda a: a.reshape(B * T, a.shape[-1])
        x1, hp, rt = _merge_call(
            x, modb, (f2(ya), f2(yb), f2(yc), f2(yd)), w_gate, w_branch_b, w_out_b, l,
            ln_g[l, 0][None], ln_b[l, 0][None], wr_t,
            jnp.broadcast_to(router_bias[:, None], (N_EXPERTS, min(T, SUB_TM))), T, min(T, SUB_TM))
        pos, wts, seg = _route_call(rt)
        x2 = _moe_call(pos, wts, seg, hp, w1_b, w3_b, w2_b, l, x1, modb,
                       ln_g[l, 1][None], ln_b[l, 1][None], T)
        new = ((nk, nv, sk, sv), hfin[:, :2]) if ctx_pass else None
        return x2, new

    y = x_prompt.reshape(B_c * T_c, D_MODEL)
    caches = tuple(jnp.zeros((B_c, DEPTH, T_c, w), F32)
                   for w in (W512, W512, SWA_KV_HEADS * HEAD_DIM, SWA_KV_HEADS * HEAD_DIM))
    states = []
    for l in range(DEPTH):
        y, (caches, st) = layer(y, l, True, caches)
        states.append(st)
    y_prompt = y.reshape(B_c, T_c, D_MODEL)
    new_nat_k = caches[0].reshape(B_c, DEPTH, T_c, NAT_HEADS, HEAD_DIM)
    new_nat_v = caches[1].reshape(B_c, DEPTH, T_c, NAT_HEADS, HEAD_DIM)
    new_swa_k = caches[2].reshape(B_c, DEPTH, T_c, SWA_KV_HEADS, HEAD_DIM)
    new_swa_v = caches[3].reshape(B_c, DEPTH, T_c, SWA_KV_HEADS, HEAD_DIM)
    new_state = jnp.stack(states, 1)

    y = x_sample.reshape(B_l * T_l, D_MODEL)
    for l in range(DEPTH):
        y, _ = layer(y, l, False)
    y_sample = y.reshape(B_l, T_l, D_MODEL)
    return (y_prompt, y_sample, new_nat_k, new_nat_v, new_swa_k, new_swa_v, new_state)
```

```python
import functools
import math

import jax
import jax.numpy as jnp
import numpy as np
from jax import lax
from jax.experimental import pallas as pl
from jax.experimental.pallas import tpu as pltpu

F32 = jnp.float32
BF16 = jnp.bfloat16

D_MODEL = 1024
DEPTH = 4
GRID_W = 64
HEAD_DIM = 64
ATTN_SCALE = HEAD_DIM ** -0.5
LOG2E = math.log2(math.e)
Q_SCALE = ATTN_SCALE * LOG2E
NEG_INF = -1e30
LN_EPS = 1e-5
ALPHA = (2 * DEPTH) ** 0.25
ROPE_BASE = 10000.0
RNN_WIDTH = 512
RNN_BLOCKS = 8
RNN_BLOCK = RNN_WIDTH // RNN_BLOCKS
CONV_W = 4
RGLRU_C = 8.0
NAT_HEADS = 8
NAT_KR = 8
NAT_KC = 16
SWA_HEADS = 8
SWA_KV_HEADS = 2
SWA_WINDOW = 128
SWA_BLOCK = 128
SWA_SPAN = SWA_BLOCK + 2 * SWA_WINDOW
FNET_GROUPS = 4
FNET_WIDTH = 512
FNET_GC = FNET_WIDTH // FNET_GROUPS
N_BRANCH = 4
N_EXPERTS = 16
N_EXPERT_GROUPS = 4
EXPERTS_PER_GROUP = N_EXPERTS // N_EXPERT_GROUPS
D_EXPERT = 512
W512 = 512
GATE_OFF = 3840

LANES = 128
SUBLANES = 8
VMEM_LIMIT = 56 * 1024 * 1024
TOKEN_TM = 1024
SUB_TM = 512


def _cparams(*sem):
    return pltpu.CompilerParams(dimension_semantics=sem, vmem_limit_bytes=VMEM_LIMIT)


def _const_spec(shape):
    nd = len(shape)
    return pl.BlockSpec(shape, lambda *_: (0,) * nd, pipeline_mode=pl.Buffered(1))


def _layer_spec(shape, l):
    nd = len(shape)
    return pl.BlockSpec((1,) + tuple(shape[1:]), lambda *_: (l,) + (0,) * (nd - 1), pipeline_mode=pl.Buffered(1))


def _ln(x):
    mu = jnp.mean(x, -1, keepdims=True)
    xc = x - mu
    var = jnp.mean(xc * xc, -1, keepdims=True)
    return xc * lax.rsqrt(var + LN_EPS)


def _dot(a, b):
    return jnp.dot(a, b, preferred_element_type=F32)


def _dot_nt(a, b):
    return lax.dot_general(a, b, (((1,), (1,)), ((), ())), preferred_element_type=F32)


ADA_ROWS = 16
ADA_TN = 1024


def _ada_kernel(c_ref, w_ref, b_ref, o_ref):
    cv = c_ref[...]
    s = (cv * jax.nn.sigmoid(cv)).astype(BF16)
    o_ref[0] = _dot(s, w_ref[0].astype(BF16)) + b_ref[0]


def _ada_call(cv, w_ada, b_ada):
    n = w_ada.shape[-1]
    return pl.pallas_call(
        _ada_kernel,
        out_shape=jax.ShapeDtypeStruct((DEPTH, ADA_ROWS, n), F32),
        grid=(DEPTH, n // ADA_TN),
        in_specs=[
            pl.BlockSpec((ADA_ROWS, D_MODEL), lambda l, j: (0, 0)),
            pl.BlockSpec((1, D_MODEL, ADA_TN), lambda l, j: (l, 0, j)),
            pl.BlockSpec((1, 1, ADA_TN), lambda l, j: (l, 0, j)),
        ],
        out_specs=pl.BlockSpec((1, ADA_ROWS, ADA_TN), lambda l, j: (l, 0, j)),
        compiler_params=_cparams("parallel", "parallel"),
        name="ada",
    )(cv, w_ada, b_ada.reshape(DEPTH, 1, n))


def _rope(u, cos, sin):
    lane = lax.broadcasted_iota(jnp.int32, cos.shape, 1)
    first = (lane & 31) < 16
    outs = []
    for j in range(u.shape[1] // LANES):
        s = u[:, j * LANES:(j + 1) * LANES]
        partner = jnp.where(first, pltpu.roll(s, LANES - 16, 1), pltpu.roll(s, 16, 1))
        outs.append(s * cos + partner * sin)
    return outs[0] if len(outs) == 1 else jnp.concatenate(outs, -1)


def _pre_kernel(*refs, plan, rope, n1, nsub, n_alias):
    x_ref, mod_ref, wm_ref, wt_ref = refs[:4]
    refs = refs[4:]
    if rope:
        cos_ref, sin_ref = refs[:2]
        refs = refs[2:]
    refs = refs[n_alias:]
    outs = refs[:len(plan)]
    m = mod_ref[0]
    sub = x_ref.shape[0] // nsub
    hs = [(_ln(x_ref[k * sub:(k + 1) * sub, :]) * (1.0 + m[1:2]) + m[0:1]).astype(BF16) for k in range(nsub)]
    for (src, off, width, _, scale, do_rope, fourier, slot), o_ref in zip(plan, outs):
        w_ref = wt_ref if src else wm_ref
        for k in range(nsub):
            rows = slice(k * sub, (k + 1) * sub)
            u = _dot(hs[k], w_ref[0, :, off:off + width])
            if do_rope:
                u = _rope(u, cos_ref[rows, :], sin_ref[rows, :])
            if scale != 1.0:
                u = u * scale
            if fourier and n1 > 1:
                stage = refs[len(plan)]
                for c in range(width // LANES):
                    stage[k, c] = u[:, c * LANES:(c + 1) * LANES]
                r = sub // n1
                for t1 in range(n1):
                    for c in range(width // LANES):
                        lo = t1 * width + c * LANES
                        o_ref[0, k * r:(k + 1) * r, lo:lo + LANES] = (
                            stage[k, c, pl.ds(t1, r, stride=n1), :].astype(o_ref.dtype))
            elif fourier:
                o_ref[0, rows, :] = u.astype(o_ref.dtype)
            elif slot is not None:
                o_ref[0, 0, rows, :] = u.astype(o_ref.dtype)
            else:
                o_ref[rows, :] = u.astype(o_ref.dtype)


def _pre_call(x, mod, w_main, n_main, w_tail, l, plan, B, T, TM, rope_tabs, caches=None):
    N = x.shape[0]
    tpb = T // TM
    nsub = max(1, TM // SUB_TM)
    n1 = T // FN_N2
    rope = rope_tabs is not None
    n_tail = w_tail.shape[-1]
    in_specs = [
        pl.BlockSpec((TM, D_MODEL), lambda i: (i, 0)),
        pl.BlockSpec((1, SUBLANES, D_MODEL), lambda i: (i // tpb, 0, 0)),
        pl.BlockSpec((1, D_MODEL, n_main), lambda i: (l, 0, 0), pipeline_mode=pl.Buffered(1)),
        pl.BlockSpec((1, D_MODEL, n_tail), lambda i: (l, 0, 0), pipeline_mode=pl.Buffered(1)),
    ]
    args = [x, mod, w_main, w_tail]
    if rope:
        in_specs += [pl.BlockSpec((TM, LANES), lambda i: (i % tpb, 0))] * 2
        args += list(rope_tabs)
    out_shape, out_specs, scratch, aliases = [], [], [], {}
    for k, p in enumerate(plan):
        if p[6]:
            out_shape.append(jax.ShapeDtypeStruct((B, FN_N2, n1 * p[2]), p[3]))
            out_specs.append(pl.BlockSpec((1, TM // n1, n1 * p[2]), lambda i: (i // tpb, i % tpb, 0)))
            if n1 > 1:
                scratch.append(pltpu.VMEM((nsub, p[2] // LANES, TM // nsub, LANES), F32))
        elif p[7] is not None:
            out_shape.append(jax.ShapeDtypeStruct((B, DEPTH, T, p[2]), p[3]))
            out_specs.append(pl.BlockSpec((1, 1, TM, p[2]), lambda i: (i // tpb, l, i % tpb, 0)))
            if caches is not None:
                aliases[len(args)] = k
                in_specs.append(pl.BlockSpec(memory_space=pl.ANY))
                args.append(caches[p[7]])
        else:
            out_shape.append(jax.ShapeDtypeStruct((N, p[2]), p[3]))
            out_specs.append(pl.BlockSpec((TM, p[2]), lambda i: (i, 0)))
    return pl.pallas_call(
        functools.partial(_pre_kernel, plan=plan, rope=rope, n1=n1, nsub=nsub, n_alias=len(aliases)),
        out_shape=out_shape,
        grid=(N // TM,),
        in_specs=in_specs,
        out_specs=out_specs,
        scratch_shapes=scratch,
        input_output_aliases=aliases,
        compiler_params=_cparams("parallel"),
        name="pre",
    )(*args)


RG_CW = 256
RG_TCH = 256
RG_SEGS = SUBLANES


def _rglru_kernel(xa_ref, ga_ref, cw_ref, cb_ref, wa_ref, wx_ref, ba_ref, bx_ref, lam_ref, h0_ref,
                  y_ref, hfin_ref, xpad, a_f, u_f, a_b, u_b, carry_f, carry_b, *, T):
    cw_ = RG_CW
    zeros8 = jnp.zeros((SUBLANES, cw_), F32)
    for k in range(cw_ // LANES):
        xpad[k, 0:SUBLANES, :] = zeros8[:, :LANES]
        xpad[k, T + SUBLANES:T + 2 * SUBLANES, :] = zeros8[:, :LANES]
        xpad[k, SUBLANES:T + SUBLANES, :] = xa_ref[0, :, k * LANES:(k + 1) * LANES]
    cw = cw_ref[...]
    cb = cb_ref[...]
    lam = lam_ref[...]
    sp = jnp.maximum(-lam, 0.0) + jnp.log1p(jnp.exp(-jnp.abs(lam)))
    sp4 = (0.5 * RGLRU_C) * sp
    nsp4_log2e = sp4 * (-math.log2(math.e))
    ba = ba_ref[...]
    bx = bx_ref[...]
    h0 = h0_ref[0]

    def chunk(c, carry):
        base = pl.multiple_of(c * RG_TCH, RG_TCH)
        xc = cb
        for i in range(CONV_W):
            xc = xc + cw[i:i + 1] * get(xpad, pl.ds(base + (SUBLANES - 1 + i), RG_TCH, stride=1))
        xcb = xc.astype(BF16)
        half_xc = 0.5 * xc
        for d, (a_s, u_s) in enumerate(((a_f, u_f), (a_b, u_b))):
            r2 = jnp.tanh(_dot(xcb, wa_ref[d]) + ba[d:d + 1]) + 1.0
            i2 = jnp.tanh(_dot(xcb, wx_ref[d]) + bx[d:d + 1]) + 1.0
            a = jnp.exp2(r2 * nsp4_log2e[d:d + 1])
            v = jnp.tanh(r2 * sp4[d:d + 1]) * (1.0 + a * a)
            gain = jnp.where(v > 0.0, v * lax.rsqrt(v), 0.0)
            put(a_s, pl.ds(store_row(c), RG_TCH), a)
            put(u_s, pl.ds(store_row(c), RG_TCH), gain * (i2 * half_xc))
        return carry

    def get(ref, rows):
        return jnp.concatenate([ref[k, rows, :] for k in range(cw_ // LANES)], -1)

    def put(ref, rows, val):
        for k in range(cw_ // LANES):
            ref[k, rows, :] = val[:, k * LANES:(k + 1) * LANES]

    seg_len = T // RG_SEGS
    segmented = seg_len % RG_TCH == 0
    seg_pitch = seg_len + SUBLANES
    per_seg = seg_len // RG_TCH if segmented else 1

    def store_row(c):
        base = c * RG_TCH
        return pl.multiple_of(base + SUBLANES * (c // per_seg), SUBLANES) if segmented else pl.multiple_of(base, RG_TCH)

    lax.fori_loop(0, T // RG_TCH, chunk, 0)

    row = lax.broadcasted_iota(jnp.int32, (SUBLANES, cw_), 0)

    if segmented:
        def seg_step(j, carry):
            h_f, p_f, h_b, p_b = carry
            jb = seg_len - 1 - j
            rows_f = pl.ds(j, RG_SEGS, stride=seg_pitch)
            rows_b = pl.ds(jb, RG_SEGS, stride=seg_pitch)
            a = get(a_f, rows_f)
            h_f = a * h_f + get(u_f, rows_f)
            p_f = p_f * a
            put(u_f, rows_f, h_f)
            put(a_f, rows_f, p_f)
            a = get(a_b, rows_b)
            h_b = a * h_b + get(u_b, rows_b)
            p_b = p_b * a
            put(u_b, rows_b, h_b)
            put(a_b, rows_b, p_b)
            return h_f, p_f, h_b, p_b

        zero = jnp.zeros((RG_SEGS, cw_), F32)
        one = jnp.ones((RG_SEGS, cw_), F32)
        h_f, p_f, h_b, p_b = lax.fori_loop(0, seg_len, seg_step, (zero, one, zero, one), unroll=2)
        cf = [h0[0:1, :]]
        for s_ in range(RG_SEGS):
            cf.append(h_f[s_:s_ + 1, :] + p_f[s_:s_ + 1, :] * cf[s_])
        cb_in = [None] * RG_SEGS + [h0[1:2, :]]
        for s_ in range(RG_SEGS - 1, -1, -1):
            cb_in[s_] = h_b[s_:s_ + 1, :] + p_b[s_:s_ + 1, :] * cb_in[s_ + 1]
        hfin_ref[0] = jnp.where(row == 0, cf[RG_SEGS], jnp.where(row == 1, cb_in[0], 0.0))
        carry_f[...] = jnp.concatenate(cf[:RG_SEGS], 0)
        carry_b[...] = jnp.concatenate(cb_in[1:], 0)

        def emit_seg(c, carry):
            base = pl.multiple_of(c * RG_TCH, RG_TCH)
            s_ = c // per_seg
            rows = pl.ds(store_row(c), RG_TCH)
            h = (get(u_f, rows) + get(a_f, rows) * carry_f[pl.ds(s_, 1), :]
                 + get(u_b, rows) + get(a_b, rows) * carry_b[pl.ds(s_, 1), :])
            y_ref[0, pl.ds(base, RG_TCH), :] = (
                h * jax.nn.gelu(ga_ref[0, pl.ds(base, RG_TCH), :])).astype(y_ref.dtype)
            return carry

        lax.fori_loop(0, T // RG_TCH, emit_seg, 0)
        return

    def block_scan(a, u, reverse):
        for dd in (1, 2, 4):
            sh = SUBLANES - dd if reverse else dd
            a_n = pltpu.roll(a, sh, 0)
            u_n = pltpu.roll(u, sh, 0)
            ok = (row < SUBLANES - dd) if reverse else (row >= dd)
            u = jnp.where(ok, u + a * u_n, u)
            a = jnp.where(ok, a * a_n, a)
        return a, u

    nblk = T // SUBLANES

    def scan(i, carry):
        c_f, c_b = carry
        lo_f = pl.multiple_of(i * SUBLANES, SUBLANES)
        lo_b = pl.multiple_of((nblk - 1 - i) * SUBLANES, SUBLANES)
        a, u = block_scan(get(a_f, pl.ds(lo_f, SUBLANES)), get(u_f, pl.ds(lo_f, SUBLANES)), False)
        h_f = u + a * c_f
        put(u_f, pl.ds(lo_f, SUBLANES), h_f)
        a, u = block_scan(get(a_b, pl.ds(lo_b, SUBLANES)), get(u_b, pl.ds(lo_b, SUBLANES)), True)
        h_b = u + a * c_b
        put(u_b, pl.ds(lo_b, SUBLANES), h_b)
        return (jnp.broadcast_to(h_f[SUBLANES - 1:SUBLANES, :], (SUBLANES, cw_)),
                jnp.broadcast_to(h_b[0:1, :], (SUBLANES, cw_)))

    c_f, c_b = lax.fori_loop(
        0, nblk, scan,
        (jnp.broadcast_to(h0[0:1, :], (SUBLANES, cw_)), jnp.broadcast_to(h0[1:2, :], (SUBLANES, cw_))),
        unroll=2)
    hfin_ref[0] = jnp.where(row == 0, c_f, jnp.where(row == 1, c_b, 0.0))

    def emit(c, carry):
        base = pl.multiple_of(c * RG_TCH, RG_TCH)
        h = get(u_f, pl.ds(base, RG_TCH)) + get(u_b, pl.ds(base, RG_TCH))
        y_ref[0, pl.ds(base, RG_TCH), :] = (h * jax.nn.gelu(ga_ref[0, pl.ds(base, RG_TCH), :])).astype(y_ref.dtype)
        return carry

    lax.fori_loop(0, T // RG_TCH, emit, 0)


def _rglru_call(xa, ga, cw, cb, wa, wx, ba, bx, lam, h0):
    B, T, _ = xa.shape
    nj = RNN_WIDTH // RG_CW
    seq = pl.BlockSpec((1, T, RG_CW), lambda b, j: (b, 0, j))
    vec8 = pl.BlockSpec((SUBLANES, RG_CW), lambda b, j: (0, j))
    wsp = pl.BlockSpec((2, RG_CW, RG_CW), lambda b, j: (0, j, j))
    st = pl.BlockSpec((1, SUBLANES, RG_CW), lambda b, j: (b, 0, j))
    return pl.pallas_call(
        functools.partial(_rglru_kernel, T=T),
        out_shape=[jax.ShapeDtypeStruct((B, T, RNN_WIDTH), BF16),
                   jax.ShapeDtypeStruct((B, SUBLANES, RNN_WIDTH), F32)],
        grid=(B, nj),
        in_specs=[seq, seq, vec8, pl.BlockSpec((1, RG_CW), lambda b, j: (0, j)),
                  wsp, wsp, vec8, vec8, vec8, st],
        out_specs=[seq, st],
        scratch_shapes=[pltpu.VMEM((RG_CW // LANES, T + 2 * SUBLANES, LANES), F32)]
        + [pltpu.VMEM((RG_CW // LANES, T + RG_SEGS * SUBLANES, LANES), F32)] * 4
        + [pltpu.VMEM((RG_SEGS, RG_CW), F32)] * 2,
        compiler_params=_cparams("parallel", "parallel"),
        name="rglru",
    )(xa, ga, cw, cb, wa, wx, ba, bx, lam, h0)


def _attend_slabs(jobs):
    M = jobs[0][0].shape[0]
    lo = lax.broadcasted_iota(jnp.int32, (M, LANES), 1) < HEAD_DIM
    scores = []
    for q2, srcs, _ in jobs:
        zero = jnp.zeros_like(q2)
        qs = jnp.concatenate([jnp.where(lo, q2, zero), jnp.where(lo, zero, q2)], 0)
        ss = []
        for k, _, bias in srcs:
            s = _dot_nt(qs, k)
            ss.append(s if bias is None else s + bias)
        scores.append(ss)
    maxima = []
    for (_, _, sink_col), ss in zip(jobs, scores):
        m = jnp.max(ss[0], -1, keepdims=True)
        for s in ss[1:]:
            m = jnp.maximum(m, jnp.max(s, -1, keepdims=True))
        maxima.append(m if sink_col is None else jnp.maximum(m, sink_col))
    outs = []
    for (_, srcs, sink_col), ss, m in zip(jobs, scores, maxima):
        den = None
        o = None
        for s, (_, v, _) in zip(ss, srcs):
            e = jnp.exp2(s - m)
            d_ = jnp.sum(e, -1, keepdims=True)
            o_ = _dot(e.astype(BF16), v)
            den = d_ if den is None else den + d_
            o = o_ if o is None else o + o_
        if sink_col is not None:
            den = den + jnp.exp2(sink_col - m)
        o = o / den
        outs.append(jnp.where(lo, o[:M], o[M:]))
    return outs


def _sink_col(sink_ref, j, M):
    r = lax.broadcasted_iota(jnp.int32, (2 * M, 1), 0)
    return jnp.where(r < M, sink_ref[2 * j], sink_ref[2 * j + 1]) * LOG2E


def _slab(j):
    return slice(j * LANES, (j + 1) * LANES)


def _attn_ctx_kernel(sink_ref, nq_ref, nk_ref, nv_ref, sq_ref, skd_ref, svd_ref, yb_ref, yc_ref):
    M = nq_ref.shape[1]
    nslab = W512 // LANES
    jobs = []
    for j in range(nslab):
        k2 = nk_ref[0, 0, :, _slab(j)].astype(BF16)
        v2 = nv_ref[0, 0, :, _slab(j)].astype(BF16)
        jobs.append((nq_ref[0, :, _slab(j)], [(k2, v2, None)], None))
    for j in range(nslab):
        g = j // 2
        jobs.append((sq_ref[0, :, _slab(j)], [(skd_ref[0, :, _slab(g)], svd_ref[0, :, _slab(g)], None)],
                     _sink_col(sink_ref, j, M)))
    outs = _attend_slabs(jobs)
    for j in range(nslab):
        yb_ref[0, :, _slab(j)] = outs[j].astype(yb_ref.dtype)
        yc_ref[0, :, _slab(j)] = outs[nslab + j].astype(yc_ref.dtype)


def _attn_ctx_call(sink, nq, nk, nv, sq, skd, svd, l):
    B, T, _ = nq.shape
    s512 = pl.BlockSpec((1, T, W512), lambda b: (b, 0, 0))
    s256 = pl.BlockSpec((1, T, 2 * LANES), lambda b: (b, 0, 0))
    cache = pl.BlockSpec((1, 1, T, W512), lambda b: (b, l, 0, 0))
    return pl.pallas_call(
        _attn_ctx_kernel,
        out_shape=[jax.ShapeDtypeStruct((B, T, W512), BF16)] * 2,
        grid=(B,),
        in_specs=[pl.BlockSpec(memory_space=pltpu.SMEM), s512, cache, cache, s512, s256, s256],
        out_specs=[s512, s512],
        compiler_params=_cparams("parallel"),
        name="attn_ctx",
    )(sink, nq, nk, nv, sq, skd, svd)


NAT_NLOC = NAT_KR * GRID_W


def _nat_lat_kernel(q_ref, k_ref, v_ref, ck_ref, cv_ref, bias_ref, y_ref, *, rows):
    r = pl.program_id(1)
    rstart = jnp.clip(r - NAT_KR // 2, 0, rows - NAT_KR)
    d = r - rstart
    kbase = pl.multiple_of(rstart * GRID_W, GRID_W)
    jobs = []
    for j in range(W512 // LANES):
        k2 = k_ref[0, pl.ds(kbase, NAT_NLOC), _slab(j)]
        v2 = v_ref[0, pl.ds(kbase, NAT_NLOC), _slab(j)]
        jobs.append((q_ref[0, :, _slab(j)],
                     [(k2, v2, bias_ref[j, d]), (ck_ref[0, 0, :, _slab(j)], cv_ref[0, 0, :, _slab(j)], None)], None))
    for j, o in enumerate(_attend_slabs(jobs)):
        y_ref[0, :, _slab(j)] = o.astype(y_ref.dtype)


def _nat_lat_call(q, k, v, ck, cv, bias, l):
    B, T, _ = q.shape
    rows = T // GRID_W
    P = ck.shape[2]
    qs = pl.BlockSpec((1, GRID_W, W512), lambda b, r: (b, r, 0))
    full = pl.BlockSpec((1, T, W512), lambda b, r: (b, 0, 0))
    cs = pl.BlockSpec((1, 1, P, W512), lambda b, r: (b, l, 0, 0))
    return pl.pallas_call(
        functools.partial(_nat_lat_kernel, rows=rows),
        out_shape=jax.ShapeDtypeStruct((B, T, W512), BF16),
        grid=(B, rows),
        in_specs=[qs, full, full, cs, cs, _const_spec(bias.shape)],
        out_specs=qs,
        compiler_params=_cparams("parallel", "arbitrary"),
        name="nat_lat",
    )(q, k, v, ck, cv, bias)


def _swa_band_masks():
    i = np.arange(2 * SWA_BLOCK)[:, None] % SWA_BLOCK
    j = np.arange(SWA_SPAN)[None, :]
    tabs = []
    for span_start in (0, -SWA_WINDOW, -2 * SWA_WINDOW):
        dist = span_start + j - i
        tabs.append(np.where(np.abs(dist) <= SWA_WINDOW, 0.0, NEG_INF))
    return jnp.asarray(np.stack(tabs), F32)


def _swa_lat_kernel(sink_ref, q_ref, kd_ref, vd_ref, ckd_ref, cvd_ref, band_ref, y_ref, *, T):
    blk = pl.program_id(1)
    M = SWA_BLOCK
    start = blk * SWA_BLOCK
    ks = pl.multiple_of(jnp.clip(start - SWA_WINDOW, 0, T - SWA_SPAN), SWA_BLOCK)
    bias = band_ref[jnp.where(blk == 0, 0, jnp.where(blk == T // SWA_BLOCK - 1, 2, 1))]
    jobs = []
    for j in range(W512 // LANES):
        g = j // 2
        jobs.append((q_ref[0, :, _slab(j)],
                     [(kd_ref[0, pl.ds(ks, SWA_SPAN), _slab(g)], vd_ref[0, pl.ds(ks, SWA_SPAN), _slab(g)], bias),
                      (ckd_ref[0, 0, :, _slab(g)], cvd_ref[0, 0, :, _slab(g)], None)],
                     _sink_col(sink_ref, j, M)))
    for j, o in enumerate(_attend_slabs(jobs)):
        y_ref[0, :, _slab(j)] = o.astype(y_ref.dtype)


def _swa_lat_call(sink, q, kd, vd, ckd, cvd, l):
    B, T, _ = q.shape
    P = ckd.shape[2]
    qs = pl.BlockSpec((1, SWA_BLOCK, W512), lambda b, i: (b, i, 0))
    full = pl.BlockSpec((1, T, 2 * LANES), lambda b, i: (b, 0, 0))
    cs = pl.BlockSpec((1, 1, P, 2 * LANES), lambda b, i: (b, l, 0, 0))
    assert T // SWA_BLOCK >= 3
    band = _swa_band_masks()
    return pl.pallas_call(
        functools.partial(_swa_lat_kernel, T=T),
        out_shape=jax.ShapeDtypeStruct((B, T, W512), BF16),
        grid=(B, T // SWA_BLOCK),
        in_specs=[pl.BlockSpec(memory_space=pltpu.SMEM), qs, full, full, cs, cs, _const_spec(band.shape)],
        out_specs=qs,
        compiler_params=_cparams("parallel", "arbitrary"),
        name="swa_lat",
    )(sink, q, kd, vd, ckd, cvd, band)


FN_N2 = 256
FN_GP = 2


def _fourier_kernel(x_ref, cc_ref, m_ref, y_ref, *scratch, n1):
    gw = FN_GP * FNET_GC
    cc = cc_ref[...]
    for gp in range(FNET_GROUPS // FN_GP):
        parts = []
        for t1 in range(n1):
            ws = []
            for g in range(FN_GP):
                lo = t1 * FNET_WIDTH + (gp * FN_GP + g) * FNET_GC
                ws.append(_dot(x_ref[0, :, lo:lo + FNET_GC], cc))
            wr = jnp.concatenate([w[:, :FNET_GC] for w in ws], -1)
            wi = jnp.concatenate([w[:, FNET_GC:] for w in ws], -1)
            v = jnp.concatenate([wr, wi], 0).astype(BF16)
            b = _dot(m_ref[t1], v)
            if n1 == 1:
                y_ref[0, :, gp * gw:(gp + 1) * gw] = b.astype(y_ref.dtype)
            else:
                scratch[0][t1] = b
        if n1 > 1:
            _dft16_real(scratch[0], y_ref, gp * gw, gw)


FN_RCH = 16


def _dft16_real(bs, y_ref, col0, gw):
    tw = {m: (math.cos(2 * math.pi * m / 16), math.sin(2 * math.pi * m / 16)) for m in (1, 2, 3, 6, 9)}

    def body(rc, carry):
        r0 = pl.multiple_of(rc * FN_RCH, FN_RCH)
        re = [bs[t1, pl.ds(r0, FN_RCH), :] for t1 in range(16)]
        im = [bs[t1, pl.ds(FN_N2 + r0, FN_RCH), :] for t1 in range(16)]
        h_re = [[None] * 4 for _ in range(4)]
        h_im = [[None] * 4 for _ in range(4)]
        for b in range(4):
            s02r, s02i = re[b] + re[8 + b], im[b] + im[8 + b]
            d02r, d02i = re[b] - re[8 + b], im[b] - im[8 + b]
            s13r, s13i = re[4 + b] + re[12 + b], im[4 + b] + im[12 + b]
            d13r, d13i = re[4 + b] - re[12 + b], im[4 + b] - im[12 + b]
            g = [(s02r + s13r, s02i + s13i), (d02r + d13i, d02i - d13r),
                 (s02r - s13r, s02i - s13i), (d02r - d13i, d02i + d13r)]
            for c in range(4):
                gr, gi = g[c]
                m = b * c
                if m == 0:
                    hr, hi = gr, gi
                elif m == 4:
                    hr, hi = gi, -gr
                else:
                    cs, sn = tw[m]
                    hr = cs * gr + sn * gi
                    hi = (cs * gi - sn * gr) if b % 2 else None
                h_re[b][c], h_im[b][c] = hr, hi
        for c in range(4):
            p = h_re[0][c] + h_re[2][c]
            q = h_re[0][c] - h_re[2][c]
            r = h_re[1][c] + h_re[3][c]
            s = h_im[1][c] - h_im[3][c]
            for d, val in enumerate((p + r, q + s, p - r, q - s)):
                k1 = c + 4 * d
                y_ref[0, pl.ds(k1 * FN_N2 + r0, FN_RCH), col0:col0 + gw] = val.astype(y_ref.dtype)
        return carry

    lax.fori_loop(0, FN_N2 // FN_RCH, body, 0)


def _dft_consts(T):
    n1 = T // FN_N2
    j = np.arange(FNET_GC)
    ang = 2 * np.pi * np.outer(j, j) / FNET_GC
    sc = 1.0 / math.sqrt(FNET_GC)
    cc = np.concatenate([np.cos(ang) * sc, -np.sin(ang) * sc], 1)
    k2 = np.arange(FN_N2)[:, None]
    t2 = np.arange(FN_N2)[None, :]
    st = 1.0 / math.sqrt(T)
    mats = []
    for t1 in range(n1):
        th = 2 * np.pi * ((k2 * (n1 * t2 + t1)) % T) / T
        c, s = np.cos(th) * st, np.sin(th) * st
        top = np.concatenate([c, s], 1)
        mats.append(top if n1 == 1 else np.concatenate([top, np.concatenate([-s, c], 1)], 0))
    return n1, jnp.asarray(cc, BF16), jnp.asarray(np.stack(mats), BF16)


def _fourier_call(xv, T):
    B = xv.shape[0]
    n1, cc, mats = _dft_consts(T)
    assert n1 in (1, 16)
    gw = FN_GP * FNET_GC
    scratch = [pltpu.VMEM((n1, 2 * FN_N2, gw), F32)] if n1 > 1 else []
    return pl.pallas_call(
        functools.partial(_fourier_kernel, n1=n1),
        out_shape=jax.ShapeDtypeStruct((B, T, FNET_WIDTH), BF16),
        grid=(B,),
        in_specs=[pl.BlockSpec((1, FN_N2, n1 * FNET_WIDTH), lambda b: (b, 0, 0)),
                  _const_spec(cc.shape), _const_spec(mats.shape)],
        out_specs=pl.BlockSpec((1, T, FNET_WIDTH), lambda b: (b, 0, 0)),
        scratch_shapes=scratch,
        compiler_params=_cparams("parallel"),
        name="fourier",
    )(xv, cc, mats)


def _route(scores, biased):
    one, zero = jnp.float32(1.0), jnp.float32(0.0)
    in2, gscore = [], []
    for g in range(N_EXPERT_GROUPS):
        vs = [biased[g * EXPERTS_PER_GROUP + j:g * EXPERTS_PER_GROUP + j + 1] for j in range(EXPERTS_PER_GROUP)]
        gs = None
        for j in range(EXPERTS_PER_GROUP):
            rank = None
            for i in range(EXPERTS_PER_GROUP):
                if i == j:
                    continue
                beats = (vs[i] >= vs[j]) if i < j else (vs[i] > vs[j])
                t = jnp.where(beats, one, zero)
                rank = t if rank is None else rank + t
            keep = jnp.where(rank < 2.0, one, zero)
            in2.append(keep)
            t = keep * vs[j]
            gs = t if gs is None else gs + t
        gscore.append(gs)
    rows, sel = [], []
    for g in range(N_EXPERT_GROUPS):
        lost = None
        for i in range(N_EXPERT_GROUPS):
            if i == g:
                continue
            beats = (gscore[i] >= gscore[g]) if i < g else (gscore[i] > gscore[g])
            t = jnp.where(beats, one, zero)
            lost = t if lost is None else lost + t
        gsel = jnp.where(lost < 1.0, one, zero)
        for j in range(EXPERTS_PER_GROUP):
            e = g * EXPERTS_PER_GROUP + j
            sel.append(gsel * in2[e])
            rows.append(sel[-1] * scores[e:e + 1])
    tot = rows[0]
    for rr in rows[1:]:
        tot = tot + rr
    return jnp.concatenate(rows, 0) / tot, jnp.concatenate(sel, 0)


HP_ROWS = D_MODEL // 2 // LANES
Y_ROWS = D_MODEL // LANES


def _pack_halves(v):
    w = v.shape[1] // 2
    lo = lax.bitcast_convert_type(v[:, :w].astype(F32), jnp.uint32)
    hi = lax.bitcast_convert_type(v[:, w:].astype(F32), jnp.uint32)
    return (lo >> 16) | hi


def _unpack_halves(p):
    lo = lax.bitcast_convert_type(p << 16, F32).astype(BF16)
    hi = lax.bitcast_convert_type(p & jnp.uint32(0xFFFF0000), F32).astype(BF16)
    return lo, hi


def _merge_kernel(x_ref, mod_ref, ya_ref, yb_ref, yc_ref, yd_ref, wg_ref, wb_ref, wo_ref,
                  lng_ref, lnb_ref, wr_ref, rb_ref, x1_ref, hp_ref, rt_ref, *, nsub):
    m = mod_ref[0]
    sub = x_ref.shape[0] // nsub
    y_refs = (ya_ref, yb_ref, yc_ref, yd_ref)
    st = [dict() for _ in range(nsub)]

    def rows(k):
        return slice(k * sub, (k + 1) * sub)

    def norm_in(k):
        st[k]['x'] = x_ref[rows(k), :]
        st[k]['h'] = (_ln(st[k]['x']) * (1.0 + m[1:2]) + m[0:1]).astype(BF16)
        st[k]['merged'] = None

    def branch(k, b):
        gate = 0.5 * jnp.tanh(_dot(st[k]['h'], wg_ref[0, :, b * D_MODEL:(b + 1) * D_MODEL])) + 0.5
        term = gate * _dot(y_refs[b][rows(k), :], wb_ref[0, b])
        st[k]['merged'] = term if st[k]['merged'] is None else st[k]['merged'] + term

    def project(k):
        st[k]['out'] = _dot(st[k]['merged'].astype(BF16), wo_ref[0])

    def norm_out(k):
        x1 = _ln(ALPHA * st[k]['x'] + m[2:3] * st[k]['out']) * lng_ref[...] + lnb_ref[...]
        x1_ref[rows(k), :] = x1
        st[k]['h2'] = (_ln(x1) * (1.0 + m[4:5]) + m[3:4]).astype(BF16)

    def pack(k):
        packed = _pack_halves(st[k]['h2'])
        for c in range(HP_ROWS):
            hp_ref[pl.ds(k * sub * HP_ROWS + c, sub, stride=HP_ROWS), :] = packed[:, c * LANES:(c + 1) * LANES]

    def route(k):
        logits = _dot_nt(wr_ref[...], st[k]['h2'])
        e = jnp.exp(logits - jnp.max(logits, 0, keepdims=True))
        scores = e / jnp.sum(e, 0, keepdims=True)
        comb, sel = _route(scores, scores + rb_ref[...])
        rt_ref[:, rows(k)] = jnp.concatenate([comb, sel], 0)

    tail = (norm_out, pack, route)
    for k in range(nsub):
        norm_in(k)
    for k in range(nsub):
        for b in range(N_BRANCH):
            branch(k, b)
            if k > 0 and b < len(tail):
                tail[b](k - 1)
        project(k)
    for stage in tail:
        stage(nsub - 1)


def _merge_call(x, mod, ys, wg, wb, wo, l, lng, lnb, wr_t, rb, T, TM):
    N = x.shape[0]
    tpb = T // TM
    tok = lambda w: pl.BlockSpec((TM, w), lambda i: (i, 0))
    return pl.pallas_call(
        functools.partial(_merge_kernel, nsub=max(1, TM // SUB_TM)),
        out_shape=[jax.ShapeDtypeStruct((N, D_MODEL), F32), jax.ShapeDtypeStruct((N * HP_ROWS, LANES), jnp.uint32),
                   jax.ShapeDtypeStruct((2 * N_EXPERTS, N), F32)],
        grid=(N // TM,),
        in_specs=[tok(D_MODEL), pl.BlockSpec((1, SUBLANES, D_MODEL), lambda i: (i // tpb, 0, 0)),
                  tok(W512), tok(W512), tok(W512), tok(W512),
                  _layer_spec(wg.shape, l), _layer_spec(wb.shape, l), _layer_spec(wo.shape, l),
                  _const_spec(lng.shape), _const_spec(lnb.shape), _const_spec(wr_t.shape), _const_spec(rb.shape)],
        out_specs=[tok(D_MODEL), pl.BlockSpec((TM * HP_ROWS, LANES), lambda i: (i, 0)),
                   pl.BlockSpec((2 * N_EXPERTS, TM), lambda i: (0, i))],
        compiler_params=_cparams("parallel"),
        name="merge",
    )(x, mod, *ys, wg, wb, wo, lng, lnb, wr_t, rb)


MOE_TM = 2048
MOE_RB = 256
MOE_KC = 4
MOE_CH = MOE_TM // MOE_KC
MOE_SLOTS = 2 * MOE_TM + N_EXPERTS * SUBLANES + MOE_RB
MOE_PR = 2 * MOE_TM // LANES
MOE_UNROLL = 8


def _route_kernel(rt_ref, u_ref, pos_ref, wts_ref, seg_ref):
    comb = rt_ref[0:N_EXPERTS, :]
    sel = rt_ref[N_EXPERTS:2 * N_EXPERTS, :]
    nb = MOE_TM // LANES
    stacked = jnp.concatenate([sel[:, b * LANES:(b + 1) * LANES] for b in range(nb)], 0)
    within = _dot(stacked.astype(BF16), u_ref[...])
    tot = jnp.sum(stacked, -1, keepdims=True)
    base = jnp.zeros((N_EXPERTS, 1), F32)
    bases = []
    for b in range(nb):
        bases.append(base)
        base = base + tot[b * N_EXPERTS:(b + 1) * N_EXPERTS]
    cnt = base
    padded = jnp.floor((cnt + (SUBLANES - 1.0)) * (1.0 / SUBLANES)) * SUBLANES
    rowi = lax.broadcasted_iota(jnp.int32, (N_EXPERTS, 1), 0)
    off = jnp.zeros((N_EXPERTS, 1), F32)
    for e in range(N_EXPERTS - 1):
        off = off + jnp.where(rowi > e, padded[e:e + 1, :], 0.0)
    pos_rows, wts_rows = [], []
    for b in range(nb):
        slot = within[b * N_EXPERTS:(b + 1) * N_EXPERTS] + (bases[b] + off)
        seen = jnp.zeros((1, LANES), F32)
        acc = [jnp.zeros((1, LANES), F32) for _ in range(4)]
        for e in range(N_EXPERTS):
            s_e = sel[e:e + 1, b * LANES:(b + 1) * LANES]
            c_e = comb[e:e + 1, b * LANES:(b + 1) * LANES]
            first = jnp.where(seen == 0.0, s_e, 0.0)
            second = jnp.where(seen == 1.0, s_e, 0.0)
            acc[0] = acc[0] + first * slot[e:e + 1]
            acc[1] = acc[1] + second * slot[e:e + 1]
            acc[2] = acc[2] + first * c_e
            acc[3] = acc[3] + second * c_e
            seen = seen + s_e
        pos_rows += acc[0:2]
        wts_rows += acc[2:4]
    pos_ref[...] = (jnp.concatenate(pos_rows, 0) * float(HP_ROWS)).astype(jnp.int32).reshape(2 * MOE_TM)
    wts_ref[...] = jnp.concatenate(wts_rows, 0).reshape(2 * MOE_TM)
    lane = lax.broadcasted_iota(jnp.int32, (N_EXPERTS, LANES), 1)
    diag = lane == lax.broadcasted_iota(jnp.int32, (N_EXPERTS, LANES), 0)
    off_row = jnp.sum(jnp.where(diag, off, 0.0), 0, keepdims=True)
    cnt_row = jnp.sum(jnp.where(diag, cnt, 0.0), 0, keepdims=True)
    seg_ref[0] = jnp.concatenate([off_row, cnt_row, jnp.zeros((SUBLANES - 2, LANES), F32)], 0).astype(jnp.int32)


def _route_call(rt):
    N = rt.shape[1]
    nt = N // MOE_TM
    u = jnp.asarray(np.triu(np.ones((LANES, LANES), np.float32), 1), BF16)
    return pl.pallas_call(
        _route_kernel,
        out_shape=[jax.ShapeDtypeStruct((nt * 2 * MOE_TM,), jnp.int32),
                   jax.ShapeDtypeStruct((nt * 2 * MOE_TM,), F32),
                   jax.ShapeDtypeStruct((nt, SUBLANES, LANES), jnp.int32)],
        grid=(nt,),
        in_specs=[pl.BlockSpec((2 * N_EXPERTS, MOE_TM), lambda i: (0, i)), _const_spec(u.shape)],
        out_specs=[pl.BlockSpec((2 * MOE_TM,), lambda i: (i,)),
                   pl.BlockSpec((2 * MOE_TM,), lambda i: (i,)),
                   pl.BlockSpec((1, SUBLANES, LANES), lambda i: (i, 0, 0))],
        compiler_params=_cparams("parallel"),
        name="route",
    )(rt, u)


def _table_index(it):
    per_block = LANES // MOE_UNROLL
    return (it // per_block) * (2 * LANES) + (it % per_block) * MOE_UNROLL


def _moe_kernel(pos_ref, wts_ref, seg_ref, hp_ref, w1_ref, w3_ref, w2_ref, x1_ref, mod_ref, lng_ref, lnb_ref,
                o_ref, xs, ys, oc):
    s = pl.program_id(1)
    half = D_MODEL // 2

    @pl.when(s == 0)
    def _dispatch():
        xs[...] = jnp.zeros_like(xs)

        def body(it, carry):
            t0 = pl.multiple_of(it * (MOE_UNROLL * HP_ROWS), MOE_UNROLL * HP_ROWS)
            i0 = _table_index(it)
            for j in range(MOE_UNROLL):
                row = hp_ref[pl.ds(t0 + HP_ROWS * j, HP_ROWS), :]
                xs[pl.ds(pl.multiple_of(pos_ref[i0 + j], HP_ROWS), HP_ROWS), :] = row
                xs[pl.ds(pl.multiple_of(pos_ref[i0 + LANES + j], HP_ROWS), HP_ROWS), :] = row
            return carry

        lax.fori_loop(0, MOE_TM // MOE_UNROLL, body, 0)

    @pl.when(s < N_EXPERTS)
    def _expert():
        off = seg_ref[0, 0, s]
        cnt = seg_ref[0, 1, s]
        nblk = (cnt + (MOE_RB // 2 - 1)) // MOE_RB

        def ffn_rows(r0, nrows):
            parts = [_unpack_halves(xs[pl.ds(pl.multiple_of(r0 * HP_ROWS, SUBLANES) + c, nrows, stride=HP_ROWS), :])
                     for c in range(HP_ROWS)]
            lo = jnp.concatenate([p[0] for p in parts], -1)
            hi = jnp.concatenate([p[1] for p in parts], -1)
            a = _dot(lo, w1_ref[0, 0, :half, :]) + _dot(hi, w1_ref[0, 0, half:, :])
            g = _dot(lo, w3_ref[0, 0, :half, :]) + _dot(hi, w3_ref[0, 0, half:, :])
            act = (a * jax.nn.sigmoid(a) * g).astype(BF16)
            y = _dot(act, w2_ref[0, 0])
            y0 = pl.multiple_of(r0 * Y_ROWS, SUBLANES)
            for c in range(Y_ROWS):
                ys[pl.ds(y0 + c, nrows, stride=Y_ROWS), :] = y[:, c * LANES:(c + 1) * LANES]

        def blk(i, carry):
            ffn_rows(pl.multiple_of(off + i * MOE_RB, SUBLANES), MOE_RB)
            return carry

        lax.fori_loop(0, nblk, blk, 0)

        @pl.when(cnt > nblk * MOE_RB)
        def _tail():
            ffn_rows(pl.multiple_of(off + nblk * MOE_RB, SUBLANES), MOE_RB // 2)

    @pl.when(s >= N_EXPERTS)
    def _combine():
        it0 = (s - N_EXPERTS) * (MOE_CH // MOE_UNROLL)

        def body(it, carry):
            t0 = pl.multiple_of(it * (MOE_UNROLL * Y_ROWS), MOE_UNROLL * Y_ROWS)
            i0 = _table_index(it0 + it)
            for j in range(MOE_UNROLL):
                p0 = pl.multiple_of(pos_ref[i0 + j] * (Y_ROWS // HP_ROWS), Y_ROWS)
                p1 = pl.multiple_of(pos_ref[i0 + LANES + j] * (Y_ROWS // HP_ROWS), Y_ROWS)
                oc[pl.ds(t0 + Y_ROWS * j, Y_ROWS), :] = (wts_ref[i0 + j] * ys[pl.ds(p0, Y_ROWS), :]
                                                         + wts_ref[i0 + LANES + j] * ys[pl.ds(p1, Y_ROWS), :])
            return carry

        lax.fori_loop(0, MOE_CH // MOE_UNROLL, body, 0)
        moe = jnp.concatenate([oc[pl.ds(c, MOE_CH, stride=Y_ROWS), :] for c in range(Y_ROWS)], -1)
        m = mod_ref[0]
        o_ref[...] = _ln(ALPHA * x1_ref[...] + m[5:6] * moe) * lng_ref[...] + lnb_ref[...]


def _moe_call(pos, wts, seg, hp, w1, w3, w2, l, x1, mod, lng, lnb, T):
    N = x1.shape[0]
    nt = N // MOE_TM
    flat = pl.BlockSpec((2 * MOE_TM,), lambda i, s: (i,), memory_space=pltpu.SMEM)
    wspec = lambda shp: pl.BlockSpec((1, 1) + shp, lambda i, s: (l, jnp.minimum(s, N_EXPERTS - 1), 0, 0))
    chunk = pl.BlockSpec((MOE_CH, D_MODEL),
                         lambda i, s: (i * MOE_KC + jnp.clip(s - N_EXPERTS, 0, MOE_KC - 1), 0))
    return pl.pallas_call(
        _moe_kernel,
        out_shape=jax.ShapeDtypeStruct((N, D_MODEL), F32),
        grid=(nt, N_EXPERTS + MOE_KC),
        in_specs=[flat, flat,
                  pl.BlockSpec((1, SUBLANES, LANES), lambda i, s: (i, 0, 0), memory_space=pltpu.SMEM),
                  pl.BlockSpec((MOE_TM * HP_ROWS, LANES), lambda i, s: (i, 0)),
                  wspec((D_MODEL, D_EXPERT)), wspec((D_MODEL, D_EXPERT)), wspec((D_EXPERT, D_MODEL)),
                  chunk, pl.BlockSpec((1, SUBLANES, D_MODEL), lambda i, s: ((i * MOE_TM) // T, 0, 0)),
                  _const_spec(lng.shape), _const_spec(lnb.shape)],
        out_specs=chunk,
        scratch_shapes=[pltpu.VMEM((MOE_SLOTS * HP_ROWS, LANES), jnp.uint32),
                        pltpu.VMEM((MOE_SLOTS * Y_ROWS, LANES), F32),
                        pltpu.VMEM((MOE_CH * Y_ROWS, LANES), F32)],
        compiler_params=_cparams("parallel", "arbitrary"),
        name="moe",
    )(pos, wts, seg, hp, w1, w3, w2, x1, mod, lng, lnb)


def _rope_tables(T):
    t = jnp.arange(T)
    nf = HEAD_DIM // 4
    inv = ROPE_BASE ** (-jnp.arange(nf, dtype=F32) / nf)
    ar = (t // GRID_W).astype(F32)[:, None] * inv
    ac = (t % GRID_W).astype(F32)[:, None] * inv
    cos = jnp.concatenate([jnp.cos(ar), jnp.cos(ar), jnp.cos(ac), jnp.cos(ac)], -1)
    sin = jnp.concatenate([-jnp.sin(ar), jnp.sin(ar), -jnp.sin(ac), jnp.sin(ac)], -1)
    return jnp.tile(cos, (1, LANES // HEAD_DIM)), jnp.tile(sin, (1, LANES // HEAD_DIM))


def _nat_bias_table(rpb):
    q = np.arange(GRID_W)
    kc = np.arange(GRID_W)
    cstart = np.clip(q - NAT_KC // 2, 0, GRID_W - NAT_KC)
    ok = (kc[None, :] >= cstart[:, None]) & (kc[None, :] < cstart[:, None] + NAT_KC)
    cidx = np.clip(kc[None, :] - q[:, None] + NAT_KC - 1, 0, 2 * NAT_KC - 2)
    onehot = (np.arange(2 * NAT_KC - 1)[:, None, None] == cidx[None]).astype(np.float32)
    t = jnp.einsum('hrc,cqk->hqrk', rpb.astype(F32), jnp.asarray(onehot), precision=lax.Precision.HIGHEST)
    t = jnp.where(jnp.asarray(ok)[None, :, None, :], t * LOG2E, NEG_INF)
    per_d = [t[:, :, NAT_KR - 1 - d:2 * NAT_KR - 1 - d, :].reshape(NAT_HEADS // 2, 2 * GRID_W, NAT_KR * GRID_W)
             for d in range(NAT_KR)]
    return jnp.stack(per_d, 1)


def _dup_heads(a):
    a = jnp.broadcast_to(a[..., :, None, :], a.shape[:-1] + (2, a.shape[-1]))
    return a.reshape(a.shape[:-3] + (-1,))


def _block_diag(w):
    eye = jnp.eye(RNN_BLOCKS, dtype=w.dtype)
    return jnp.einsum('dnio,nm->dnimo', w, eye).reshape(2, RNN_WIDTH, RNN_WIDTH)


def _pad_rows(a, rows=SUBLANES):
    return jnp.pad(a, ((0, rows - a.shape[0]),) + ((0, 0),) * (a.ndim - 1))


def kernel(x_prompt, x_sample, c, cache_nat_k, cache_nat_v, cache_swa_k, cache_swa_v, state_rglru, c_ctx,
           w_ada, b_ada, w_in, rg_conv_w, rg_conv_b, rg_wa, rg_ba, rg_wx, rg_bx, rg_lambda, nat_rpb,
           swa_sink, w_branch, w_out, ln_g, ln_b, w_router, router_bias, w1, w3, w2):
    B_c, T_c, _ = x_prompt.shape
    B_l, T_l, _ = x_sample.shape
    P = cache_nat_k.shape[2]

    cv = jnp.concatenate([c_ctx[None], c, jnp.zeros((ADA_ROWS - 1 - B_l, D_MODEL), F32)], 0)
    mod_all = _ada_call(cv, w_ada, b_ada).reshape(DEPTH, ADA_ROWS, 6, D_MODEL)
    mod_all = jnp.pad(mod_all, ((0, 0), (0, 0), (0, SUBLANES - 6), (0, 0)))

    sk0 = 3072
    sv0 = sk0 + SWA_KV_HEADS * HEAD_DIM
    xf0 = sv0 + SWA_KV_HEADS * HEAD_DIM
    w_in_b = w_in.astype(BF16)
    head = lambda c0, g: w_in_b[:, :, c0 + g * HEAD_DIM:c0 + (g + 1) * HEAD_DIM]
    w_tail_ctx = jnp.concatenate([head(c0, g) for c0 in (sk0, sv0) for g in (0, 0, 1, 1)], -1)
    w_tail_lat = jnp.concatenate([w_in_b[:, :, xf0:GATE_OFF], w_tail_ctx], -1)
    w_gate = (0.5 * w_in[:, :, GATE_OFF:]).astype(BF16)
    w_branch_b = w_branch.astype(BF16)
    w_out_b = w_out.astype(BF16)
    w1_b, w3_b, w2_b = w1.astype(BF16), w3.astype(BF16), w2.astype(BF16)
    wr_t = w_router.T.astype(BF16)
    wa_bd = jnp.stack([_block_diag(0.5 * rg_wa[l]) for l in range(DEPTH)]).astype(BF16)
    wx_bd = jnp.stack([_block_diag(0.5 * rg_wx[l]) for l in range(DEPTH)]).astype(BF16)

    plan_ctx = ((0, 0, 512, F32, 1.0, False, False, None), (0, 512, 512, F32, 1.0, False, False, None),
                (0, 1024, 512, BF16, Q_SCALE, False, False, None), (0, 1536, 512, F32, 1.0, False, False, 0),
                (0, 2048, 512, F32, 1.0, False, False, 1), (0, 2560, 512, BF16, Q_SCALE, False, False, None),
                (0, sk0, 128, F32, 1.0, False, False, 2), (0, sv0, 128, F32, 1.0, False, False, 3),
                (0, xf0, 512, BF16, 1.0, False, True, None),
                (1, 0, 256, BF16, 1.0, False, False, None), (1, 256, 256, BF16, 1.0, False, False, None))
    plan_lat = ((0, 0, 512, F32, 1.0, False, False, None), (0, 512, 512, F32, 1.0, False, False, None),
                (0, 1024, 512, BF16, Q_SCALE, False, False, None), (0, 1536, 512, BF16, 1.0, False, False, None),
                (0, 2048, 512, BF16, 1.0, False, False, None), (0, 2560, 512, BF16, Q_SCALE, True, False, None),
                (1, 0, 512, BF16, 1.0, False, True, None),
                (1, 512, 256, BF16, 1.0, True, False, None), (1, 768, 256, BF16, 1.0, False, False, None))

    rope_tabs = _rope_tables(T_l)
    ck_nat = cache_nat_k.reshape(B_l, DEPTH, P, W512).astype(BF16)
    cv_nat = cache_nat_v.reshape(B_l, DEPTH, P, W512).astype(BF16)
    ckd_swa = _dup_heads(cache_swa_k).astype(BF16)
    cvd_swa = _dup_heads(cache_swa_v).astype(BF16)
    state8 = jnp.pad(state_rglru, ((0, 0), (0, 0), (0, SUBLANES - 2), (0, 0)))
    zero_state = jnp.zeros((B_c, SUBLANES, RNN_WIDTH), F32)

    def layer(x, l, ctx_pass, caches=None):
        B, T = (B_c, T_c) if ctx_pass else (B_l, T_l)
        TM = min(T, TOKEN_TM)
        modb = jnp.broadcast_to(mod_all[l, 0:1], (B, SUBLANES, D_MODEL)) if ctx_pass else mod_all[l, 1:1 + B_l]
        if ctx_pass:
            xa, ga, nq, nk, nv, sq, sk, sv, xf, skd, svd = _pre_call(
                x, modb, w_in_b, GATE_OFF, w_tail_ctx, l, plan_ctx, B, T, TM, None, caches)
        else:
            xa, ga, nq, nk, nv, sq, xf, skd, svd = _pre_call(
                x, modb, w_in_b, sk0, w_tail_lat, l, plan_lat, B, T, TM, rope_tabs)
        r3 = lambda a: a.reshape(B, T, a.shape[-1])
        h0 = zero_state if ctx_pass else state8[:, l]
        ya, hfin = _rglru_call(
            r3(xa), r3(ga), _pad_rows(rg_conv_w[l]), rg_conv_b[l][None], wa_bd[l], wx_bd[l],
            _pad_rows(0.5 * rg_ba[l]), _pad_rows(0.5 * rg_bx[l]), _pad_rows(rg_lambda[l]), h0)
        if ctx_pass:
            yb, yc = _attn_ctx_call(swa_sink[l], r3(nq), nk, nv, r3(sq), r3(skd), r3(svd), l)
        else:
            yb = _nat_lat_call(r3(nq), r3(nk), r3(nv), ck_nat, cv_nat, _nat_bias_table(nat_rpb[l]), l)
            yc = _swa_lat_call(swa_sink[l], r3(sq), r3(skd), r3(svd), ckd_swa, cvd_swa, l)
        yd = _fourier_call(xf, T)
        f2 = lambda a: a.reshape(B * T, a.shape[-1])
        x1, hp, rt = _merge_call(
            x, modb, (f2(ya), f2(yb), f2(yc), f2(yd)), w_gate, w_branch_b, w_out_b, l,
            ln_g[l, 0][None], ln_b[l, 0][None], wr_t,
            jnp.broadcast_to(router_bias[:, None], (N_EXPERTS, min(T, SUB_TM))), T, TM)
        pos, wts, seg = _route_call(rt)
        x2 = _moe_call(pos, wts, seg, hp, w1_b, w3_b, w2_b, l, x1, modb,
                       ln_g[l, 1][None], ln_b[l, 1][None], T)
        new = ((nk, nv, sk, sv), hfin[:, :2]) if ctx_pass else None
        return x2, new

    y = x_prompt.reshape(B_c * T_c, D_MODEL)
    caches = tuple(jnp.zeros((B_c, DEPTH, T_c, w), F32)
                   for w in (W512, W512, SWA_KV_HEADS * HEAD_DIM, SWA_KV_HEADS * HEAD_DIM))
    states = []
    for l in range(DEPTH):
        y, (caches, st) = layer(y, l, True, caches)
        states.append(st)
    y_prompt = y.reshape(B_c, T_c, D_MODEL)
    new_nat_k = caches[0].reshape(B_c, DEPTH, T_c, NAT_HEADS, HEAD_DIM)
    new_nat_v = caches[1].reshape(B_c, DEPTH, T_c, NAT_HEADS, HEAD_DIM)
    new_swa_k = caches[2].reshape(B_c, DEPTH, T_c, SWA_KV_HEADS, HEAD_DIM)
    new_swa_v = caches[3].reshape(B_c, DEPTH, T_c, SWA_KV_HEADS, HEAD_DIM)
    new_state = jnp.stack(states, 1)

    y = x_sample.reshape(B_l * T_l, D_MODEL)
    for l in range(DEPTH):
        y, _ = layer(y, l, False)
    y_sample = y.reshape(B_l, T_l, D_MODEL)
    return (y_prompt, y_sample, new_nat_k, new_nat_v, new_swa_k, new_swa_v, new_state)
```

```python
import functools
import math

import jax
import jax.numpy as jnp
import numpy as np
from jax import lax
from jax.experimental import pallas as pl
from jax.experimental.pallas import tpu as pltpu

F32 = jnp.float32
BF16 = jnp.bfloat16

D_MODEL = 1024
DEPTH = 4
GRID_W = 64
HEAD_DIM = 64
ATTN_SCALE = HEAD_DIM ** -0.5
LOG2E = math.log2(math.e)
Q_SCALE = ATTN_SCALE * LOG2E
NEG_INF = -1e30
LN_EPS = 1e-5
ALPHA = (2 * DEPTH) ** 0.25
ROPE_BASE = 10000.0
RNN_WIDTH = 512
RNN_BLOCKS = 8
CONV_W = 4
RGLRU_C = 8.0
NAT_HEADS = 8
NAT_KR = 8
NAT_KC = 16
SWA_HEADS = 8
SWA_KV_HEADS = 2
SWA_WINDOW = 128
SWA_BLOCK = 128
SWA_SPAN = SWA_BLOCK + 2 * SWA_WINDOW
FNET_GROUPS = 4
FNET_WIDTH = 512
FNET_GC = FNET_WIDTH // FNET_GROUPS
N_BRANCH = 4
N_EXPERTS = 16
N_EXPERT_GROUPS = 4
EXPERTS_PER_GROUP = N_EXPERTS // N_EXPERT_GROUPS
D_EXPERT = 512
W512 = 512
GATE_OFF = 3840

LANES = 128
SUBLANES = 8
VMEM_LIMIT = 56 * 1024 * 1024
TOKEN_TM = 1024
SUB_TM = 512


def _cparams(*sem):
    return pltpu.CompilerParams(dimension_semantics=sem, vmem_limit_bytes=VMEM_LIMIT)


def _const_spec(shape):
    nd = len(shape)
    return pl.BlockSpec(shape, lambda *_: (0,) * nd, pipeline_mode=pl.Buffered(1))


def _layer_spec(shape, l):
    nd = len(shape)
    return pl.BlockSpec((1,) + tuple(shape[1:]), lambda *_: (l,) + (0,) * (nd - 1), pipeline_mode=pl.Buffered(1))


def _ln(x):
    mu = jnp.mean(x, -1, keepdims=True)
    xc = x - mu
    var = jnp.mean(xc * xc, -1, keepdims=True)
    return xc * lax.rsqrt(var + LN_EPS)


def _dot(a, b):
    return jnp.dot(a, b, preferred_element_type=F32)


def _dot_nt(a, b):
    return lax.dot_general(a, b, (((1,), (1,)), ((), ())), preferred_element_type=F32)


ADA_ROWS = 2 * SUBLANES
ADA_TN = D_MODEL


def _ada_kernel(c_ref, w_ref, b_ref, o_ref):
    cv = c_ref[...]
    s = (cv * jax.nn.sigmoid(cv)).astype(BF16)
    o_ref[0] = _dot(s, w_ref[0].astype(BF16)) + b_ref[0]


def _ada_call(cv, w_ada, b_ada):
    n = w_ada.shape[-1]
    return pl.pallas_call(
        _ada_kernel,
        out_shape=jax.ShapeDtypeStruct((DEPTH, ADA_ROWS, n), F32),
        grid=(DEPTH, n // ADA_TN),
        in_specs=[
            pl.BlockSpec((ADA_ROWS, D_MODEL), lambda l, j: (0, 0)),
            pl.BlockSpec((1, D_MODEL, ADA_TN), lambda l, j: (l, 0, j)),
            pl.BlockSpec((1, 1, ADA_TN), lambda l, j: (l, 0, j)),
        ],
        out_specs=pl.BlockSpec((1, ADA_ROWS, ADA_TN), lambda l, j: (l, 0, j)),
        compiler_params=_cparams("parallel", "parallel"),
        name="ada",
    )(cv, w_ada, b_ada.reshape(DEPTH, 1, n))


def _rope(u, cos, sin):
    lane = lax.broadcasted_iota(jnp.int32, cos.shape, 1)
    first = (lane & 31) < 16
    outs = []
    for j in range(u.shape[1] // LANES):
        s = u[:, j * LANES:(j + 1) * LANES]
        partner = jnp.where(first, pltpu.roll(s, LANES - 16, 1), pltpu.roll(s, 16, 1))
        outs.append(s * cos + partner * sin)
    return outs[0] if len(outs) == 1 else jnp.concatenate(outs, -1)


def _pre_kernel(*refs, plan, rope, n1, nsub, n_alias):
    x_ref, mod_ref, wm_ref, wt_ref = refs[:4]
    refs = refs[4:]
    if rope:
        cos_ref, sin_ref = refs[:2]
        refs = refs[2:]
    refs = refs[n_alias:]
    outs = refs[:len(plan)]
    m = mod_ref[0]
    sub = x_ref.shape[0] // nsub
    hs = [(_ln(x_ref[k * sub:(k + 1) * sub, :]) * (1.0 + m[1:2]) + m[0:1]).astype(BF16) for k in range(nsub)]
    for (src, off, width, _, scale, do_rope, fourier, slot), o_ref in zip(plan, outs):
        w_ref = wt_ref if src else wm_ref
        for k in range(nsub):
            rows = slice(k * sub, (k + 1) * sub)
            u = _dot(hs[k], w_ref[0, :, off:off + width])
            if do_rope:
                u = _rope(u, cos_ref[rows, :], sin_ref[rows, :])
            if scale != 1.0:
                u = u * scale
            if fourier and n1 > 1:
                stage = refs[len(plan)]
                for c in range(width // LANES):
                    stage[k, c] = u[:, c * LANES:(c + 1) * LANES]
                r = sub // n1
                for t1 in range(n1):
                    for c in range(width // LANES):
                        lo = t1 * width + c * LANES
                        o_ref[0, k * r:(k + 1) * r, lo:lo + LANES] = (
                            stage[k, c, pl.ds(t1, r, stride=n1), :].astype(o_ref.dtype))
            elif fourier:
                o_ref[0, rows, :] = u.astype(o_ref.dtype)
            elif slot is not None:
                o_ref[0, 0, rows, :] = u.astype(o_ref.dtype)
            else:
                o_ref[rows, :] = u.astype(o_ref.dtype)


def _pre_call(x, mod, w_main, n_main, w_tail, l, plan, B, T, TM, rope_tabs, caches=None):
    N = x.shape[0]
    tpb = T // TM
    nsub = max(1, TM // SUB_TM)
    n1 = T // FN_N2
    rope = rope_tabs is not None
    n_tail = w_tail.shape[-1]
    in_specs = [
        pl.BlockSpec((TM, D_MODEL), lambda i: (i, 0)),
        pl.BlockSpec((1, SUBLANES, D_MODEL), lambda i: (i // tpb, 0, 0)),
        pl.BlockSpec((1, D_MODEL, n_main), lambda i: (l, 0, 0), pipeline_mode=pl.Buffered(1)),
        pl.BlockSpec((1, D_MODEL, n_tail), lambda i: (l, 0, 0), pipeline_mode=pl.Buffered(1)),
    ]
    args = [x, mod, w_main, w_tail]
    if rope:
        in_specs += [pl.BlockSpec((TM, LANES), lambda i: (i % tpb, 0))] * 2
        args += list(rope_tabs)
    out_shape, out_specs, scratch, aliases = [], [], [], {}
    for k, p in enumerate(plan):
        if p[6]:
            out_shape.append(jax.ShapeDtypeStruct((B, FN_N2, n1 * p[2]), p[3]))
            out_specs.append(pl.BlockSpec((1, TM // n1, n1 * p[2]), lambda i: (i // tpb, i % tpb, 0)))
            if n1 > 1:
                scratch.append(pltpu.VMEM((nsub, p[2] // LANES, TM // nsub, LANES), F32))
        elif p[7] is not None:
            out_shape.append(jax.ShapeDtypeStruct((B, DEPTH, T, p[2]), p[3]))
            out_specs.append(pl.BlockSpec((1, 1, TM, p[2]), lambda i: (i // tpb, l, i % tpb, 0)))
            if caches is not None:
                aliases[len(args)] = k
                in_specs.append(pl.BlockSpec(memory_space=pl.ANY))
                args.append(caches[p[7]])
        else:
            out_shape.append(jax.ShapeDtypeStruct((N, p[2]), p[3]))
            out_specs.append(pl.BlockSpec((TM, p[2]), lambda i: (i, 0)))
    return pl.pallas_call(
        functools.partial(_pre_kernel, plan=plan, rope=rope, n1=n1, nsub=nsub, n_alias=len(aliases)),
        out_shape=out_shape,
        grid=(N // TM,),
        in_specs=in_specs,
        out_specs=out_specs,
        scratch_shapes=scratch,
        input_output_aliases=aliases,
        compiler_params=_cparams("parallel"),
        name="pre",
    )(*args)


RG_CW = 256
RG_TCH = 256
RG_SEGS = SUBLANES


def _rglru_kernel(xa_ref, ga_ref, cw_ref, cb_ref, wa_ref, wx_ref, ba_ref, bx_ref, lam_ref, h0_ref,
                  y_ref, hfin_ref, xpad, a_f, u_f, a_b, u_b, carry_f, carry_b, *, T):
    cw_ = RG_CW
    zeros8 = jnp.zeros((SUBLANES, cw_), F32)
    for k in range(cw_ // LANES):
        xpad[k, 0:SUBLANES, :] = zeros8[:, :LANES]
        xpad[k, T + SUBLANES:T + 2 * SUBLANES, :] = zeros8[:, :LANES]
        xpad[k, SUBLANES:T + SUBLANES, :] = xa_ref[0, :, k * LANES:(k + 1) * LANES]
    cw = cw_ref[...]
    cb = cb_ref[...]
    lam = lam_ref[...]
    sp = jnp.maximum(-lam, 0.0) + jnp.log1p(jnp.exp(-jnp.abs(lam)))
    sp4 = (0.5 * RGLRU_C) * sp
    nsp4_log2e = sp4 * (-LOG2E)
    ba = ba_ref[...]
    bx = bx_ref[...]
    h0 = h0_ref[0]

    def chunk(c, carry):
        base = pl.multiple_of(c * RG_TCH, RG_TCH)
        xc = cb
        for i in range(CONV_W):
            xc = xc + cw[i:i + 1] * get(xpad, pl.ds(base + (SUBLANES - 1 + i), RG_TCH, stride=1))
        xcb = xc.astype(BF16)
        half_xc = 0.5 * xc
        for d, (a_s, u_s) in enumerate(((a_f, u_f), (a_b, u_b))):
            r2 = jnp.tanh(_dot(xcb, wa_ref[d]) + ba[d:d + 1]) + 1.0
            i2 = jnp.tanh(_dot(xcb, wx_ref[d]) + bx[d:d + 1]) + 1.0
            a = jnp.exp2(r2 * nsp4_log2e[d:d + 1])
            v = jnp.tanh(r2 * sp4[d:d + 1]) * (1.0 + a * a)
            gain = jnp.where(v > 0.0, v * lax.rsqrt(v), 0.0)
            put(a_s, pl.ds(store_row(c), RG_TCH), a)
            put(u_s, pl.ds(store_row(c), RG_TCH), gain * (i2 * half_xc))
        return carry

    def get(ref, rows):
        return jnp.concatenate([ref[k, rows, :] for k in range(cw_ // LANES)], -1)

    def put(ref, rows, val):
        for k in range(cw_ // LANES):
            ref[k, rows, :] = val[:, k * LANES:(k + 1) * LANES]

    seg_len = T // RG_SEGS
    segmented = seg_len % RG_TCH == 0
    seg_pitch = seg_len + SUBLANES
    per_seg = seg_len // RG_TCH if segmented else 1

    def store_row(c):
        base = c * RG_TCH
        return pl.multiple_of(base + SUBLANES * (c // per_seg), SUBLANES) if segmented else pl.multiple_of(base, RG_TCH)

    lax.fori_loop(0, T // RG_TCH, chunk, 0)

    row = lax.broadcasted_iota(jnp.int32, (SUBLANES, cw_), 0)

    if segmented:
        def seg_step(j, carry):
            h_f, p_f, h_b, p_b = carry
            jb = seg_len - 1 - j
            rows_f = pl.ds(j, RG_SEGS, stride=seg_pitch)
            rows_b = pl.ds(jb, RG_SEGS, stride=seg_pitch)
            a = get(a_f, rows_f)
            h_f = a * h_f + get(u_f, rows_f)
            p_f = p_f * a
            put(u_f, rows_f, h_f)
            put(a_f, rows_f, p_f)
            a = get(a_b, rows_b)
            h_b = a * h_b + get(u_b, rows_b)
            p_b = p_b * a
            put(u_b, rows_b, h_b)
            put(a_b, rows_b, p_b)
            return h_f, p_f, h_b, p_b

        zero = jnp.zeros((RG_SEGS, cw_), F32)
        one = jnp.ones((RG_SEGS, cw_), F32)
        h_f, p_f, h_b, p_b = lax.fori_loop(0, seg_len, seg_step, (zero, one, zero, one), unroll=2)
        cf = [h0[0:1, :]]
        for s_ in range(RG_SEGS):
            cf.append(h_f[s_:s_ + 1, :] + p_f[s_:s_ + 1, :] * cf[s_])
        cb_in = [None] * RG_SEGS + [h0[1:2, :]]
        for s_ in range(RG_SEGS - 1, -1, -1):
            cb_in[s_] = h_b[s_:s_ + 1, :] + p_b[s_:s_ + 1, :] * cb_in[s_ + 1]
        hfin_ref[0] = jnp.where(row == 0, cf[RG_SEGS], jnp.where(row == 1, cb_in[0], 0.0))
        carry_f[...] = jnp.concatenate(cf[:RG_SEGS], 0)
        carry_b[...] = jnp.concatenate(cb_in[1:], 0)

        def emit_seg(c, carry):
            base = pl.multiple_of(c * RG_TCH, RG_TCH)
            s_ = c // per_seg
            rows = pl.ds(store_row(c), RG_TCH)
            h = (get(u_f, rows) + get(a_f, rows) * carry_f[pl.ds(s_, 1), :]
                 + get(u_b, rows) + get(a_b, rows) * carry_b[pl.ds(s_, 1), :])
            y_ref[0, pl.ds(base, RG_TCH), :] = (
                h * jax.nn.gelu(ga_ref[0, pl.ds(base, RG_TCH), :])).astype(y_ref.dtype)
            return carry

        lax.fori_loop(0, T // RG_TCH, emit_seg, 0)
        return

    def block_scan(a, u, reverse):
        for dd in (1, 2, 4):
            sh = SUBLANES - dd if reverse else dd
            a_n = pltpu.roll(a, sh, 0)
            u_n = pltpu.roll(u, sh, 0)
            ok = (row < SUBLANES - dd) if reverse else (row >= dd)
            u = jnp.where(ok, u + a * u_n, u)
            a = jnp.where(ok, a * a_n, a)
        return a, u

    nblk = T // SUBLANES

    def scan(i, carry):
        c_f, c_b = carry
        lo_f = pl.multiple_of(i * SUBLANES, SUBLANES)
        lo_b = pl.multiple_of((nblk - 1 - i) * SUBLANES, SUBLANES)
        a, u = block_scan(get(a_f, pl.ds(lo_f, SUBLANES)), get(u_f, pl.ds(lo_f, SUBLANES)), False)
        h_f = u + a * c_f
        put(u_f, pl.ds(lo_f, SUBLANES), h_f)
        a, u = block_scan(get(a_b, pl.ds(lo_b, SUBLANES)), get(u_b, pl.ds(lo_b, SUBLANES)), True)
        h_b = u + a * c_b
        put(u_b, pl.ds(lo_b, SUBLANES), h_b)
        return (jnp.broadcast_to(h_f[SUBLANES - 1:SUBLANES, :], (SUBLANES, cw_)),
                jnp.broadcast_to(h_b[0:1, :], (SUBLANES, cw_)))

    c_f, c_b = lax.fori_loop(
        0, nblk, scan,
        (jnp.broadcast_to(h0[0:1, :], (SUBLANES, cw_)), jnp.broadcast_to(h0[1:2, :], (SUBLANES, cw_))),
        unroll=2)
    hfin_ref[0] = jnp.where(row == 0, c_f, jnp.where(row == 1, c_b, 0.0))

    def emit(c, carry):
        base = pl.multiple_of(c * RG_TCH, RG_TCH)
        h = get(u_f, pl.ds(base, RG_TCH)) + get(u_b, pl.ds(base, RG_TCH))
        y_ref[0, pl.ds(base, RG_TCH), :] = (h * jax.nn.gelu(ga_ref[0, pl.ds(base, RG_TCH), :])).astype(y_ref.dtype)
        return carry

    lax.fori_loop(0, T // RG_TCH, emit, 0)


def _rglru_call(xa, ga, cw, cb, wa, wx, ba, bx, lam, h0):
    B, T, _ = xa.shape
    nj = RNN_WIDTH // RG_CW
    seq = pl.BlockSpec((1, T, RG_CW), lambda b, j: (b, 0, j))
    vec8 = pl.BlockSpec((SUBLANES, RG_CW), lambda b, j: (0, j))
    wsp = pl.BlockSpec((2, RG_CW, RG_CW), lambda b, j: (0, j, j))
    st = pl.BlockSpec((1, SUBLANES, RG_CW), lambda b, j: (b, 0, j))
    return pl.pallas_call(
        functools.partial(_rglru_kernel, T=T),
        out_shape=[jax.ShapeDtypeStruct((B, T, RNN_WIDTH), BF16),
                   jax.ShapeDtypeStruct((B, SUBLANES, RNN_WIDTH), F32)],
        grid=(B, nj),
        in_specs=[seq, seq, vec8, pl.BlockSpec((1, RG_CW), lambda b, j: (0, j)),
                  wsp, wsp, vec8, vec8, vec8, st],
        out_specs=[seq, st],
        scratch_shapes=[pltpu.VMEM((RG_CW // LANES, T + 2 * SUBLANES, LANES), F32)]
        + [pltpu.VMEM((RG_CW // LANES, T + RG_SEGS * SUBLANES, LANES), F32)] * 4
        + [pltpu.VMEM((RG_SEGS, RG_CW), F32)] * 2,
        compiler_params=_cparams("parallel", "parallel"),
        name="rglru",
    )(xa, ga, cw, cb, wa, wx, ba, bx, lam, h0)


def _attend_slabs(jobs):
    M = jobs[0][0].shape[0]
    lo = lax.broadcasted_iota(jnp.int32, (M, LANES), 1) < HEAD_DIM
    scores = []
    for q2, srcs, _ in jobs:
        zero = jnp.zeros_like(q2)
        qs = jnp.concatenate([jnp.where(lo, q2, zero), jnp.where(lo, zero, q2)], 0)
        ss = []
        for k, _, bias in srcs:
            s = _dot_nt(qs, k)
            ss.append(s if bias is None else s + bias)
        scores.append(ss)
    maxima = []
    for (_, _, sink_col), ss in zip(jobs, scores):
        m = jnp.max(ss[0], -1, keepdims=True)
        for s in ss[1:]:
            m = jnp.maximum(m, jnp.max(s, -1, keepdims=True))
        maxima.append(m if sink_col is None else jnp.maximum(m, sink_col))
    outs = []
    for (_, srcs, sink_col), ss, m in zip(jobs, scores, maxima):
        den = None
        o = None
        for s, (_, v, _) in zip(ss, srcs):
            e = jnp.exp2(s - m)
            d_ = jnp.sum(e, -1, keepdims=True)
            o_ = _dot(e.astype(BF16), v)
            den = d_ if den is None else den + d_
            o = o_ if o is None else o + o_
        if sink_col is not None:
            den = den + jnp.exp2(sink_col - m)
        o = o / den
        outs.append(jnp.where(lo, o[:M], o[M:]))
    return outs


def _sink_col(sink_ref, j, M):
    r = lax.broadcasted_iota(jnp.int32, (2 * M, 1), 0)
    return jnp.where(r < M, sink_ref[2 * j], sink_ref[2 * j + 1]) * LOG2E


def _slab(j):
    return slice(j * LANES, (j + 1) * LANES)


def _attn_ctx_kernel(sink_ref, nq_ref, nk_ref, nv_ref, sq_ref, skd_ref, svd_ref, yb_ref, yc_ref):
    M = nq_ref.shape[1]
    nslab = W512 // LANES
    jobs = []
    for j in range(nslab):
        k2 = nk_ref[0, 0, :, _slab(j)].astype(BF16)
        v2 = nv_ref[0, 0, :, _slab(j)].astype(BF16)
        jobs.append((nq_ref[0, :, _slab(j)], [(k2, v2, None)], None))
    for j in range(nslab):
        g = j // 2
        jobs.append((sq_ref[0, :, _slab(j)], [(skd_ref[0, :, _slab(g)], svd_ref[0, :, _slab(g)], None)],
                     _sink_col(sink_ref, j, M)))
    outs = _attend_slabs(jobs)
    for j in range(nslab):
        yb_ref[0, :, _slab(j)] = outs[j].astype(yb_ref.dtype)
        yc_ref[0, :, _slab(j)] = outs[nslab + j].astype(yc_ref.dtype)


def _attn_ctx_call(sink, nq, nk, nv, sq, skd, svd, l):
    B, T, _ = nq.shape
    s512 = pl.BlockSpec((1, T, W512), lambda b: (b, 0, 0))
    s256 = pl.BlockSpec((1, T, 2 * LANES), lambda b: (b, 0, 0))
    cache = pl.BlockSpec((1, 1, T, W512), lambda b: (b, l, 0, 0))
    return pl.pallas_call(
        _attn_ctx_kernel,
        out_shape=[jax.ShapeDtypeStruct((B, T, W512), BF16)] * 2,
        grid=(B,),
        in_specs=[pl.BlockSpec(memory_space=pltpu.SMEM), s512, cache, cache, s512, s256, s256],
        out_specs=[s512, s512],
        compiler_params=_cparams("parallel"),
        name="attn_ctx",
    )(sink, nq, nk, nv, sq, skd, svd)


NAT_NLOC = NAT_KR * GRID_W


def _nat_lat_kernel(q_ref, k_ref, v_ref, ck_ref, cv_ref, bias_ref, y_ref, *, rows):
    r = pl.program_id(1)
    rstart = jnp.clip(r - NAT_KR // 2, 0, rows - NAT_KR)
    d = r - rstart
    kbase = pl.multiple_of(rstart * GRID_W, GRID_W)
    jobs = []
    for j in range(W512 // LANES):
        k2 = k_ref[0, pl.ds(kbase, NAT_NLOC), _slab(j)]
        v2 = v_ref[0, pl.ds(kbase, NAT_NLOC), _slab(j)]
        jobs.append((q_ref[0, :, _slab(j)],
                     [(k2, v2, bias_ref[j, d]), (ck_ref[0, 0, :, _slab(j)], cv_ref[0, 0, :, _slab(j)], None)], None))
    for j, o in enumerate(_attend_slabs(jobs)):
        y_ref[0, :, _slab(j)] = o.astype(y_ref.dtype)


def _nat_lat_call(q, k, v, ck, cv, bias, l):
    B, T, _ = q.shape
    rows = T // GRID_W
    P = ck.shape[2]
    qs = pl.BlockSpec((1, GRID_W, W512), lambda b, r: (b, r, 0))
    full = pl.BlockSpec((1, T, W512), lambda b, r: (b, 0, 0))
    cs = pl.BlockSpec((1, 1, P, W512), lambda b, r: (b, l, 0, 0))
    return pl.pallas_call(
        functools.partial(_nat_lat_kernel, rows=rows),
        out_shape=jax.ShapeDtypeStruct((B, T, W512), BF16),
        grid=(B, rows),
        in_specs=[qs, full, full, cs, cs, _const_spec(bias.shape)],
        out_specs=qs,
        compiler_params=_cparams("parallel", "arbitrary"),
        name="nat_lat",
    )(q, k, v, ck, cv, bias)


def _swa_band_masks():
    i = np.arange(2 * SWA_BLOCK)[:, None] % SWA_BLOCK
    j = np.arange(SWA_SPAN)[None, :]
    tabs = []
    for span_start in (0, -SWA_WINDOW, -2 * SWA_WINDOW):
        dist = span_start + j - i
        tabs.append(np.where(np.abs(dist) <= SWA_WINDOW, 0.0, NEG_INF))
    return jnp.asarray(np.stack(tabs), F32)


def _swa_lat_kernel(sink_ref, q_ref, kd_ref, vd_ref, ckd_ref, cvd_ref, band_ref, y_ref, *, T):
    blk = pl.program_id(1)
    M = SWA_BLOCK
    start = blk * SWA_BLOCK
    ks = pl.multiple_of(jnp.clip(start - SWA_WINDOW, 0, T - SWA_SPAN), SWA_BLOCK)
    bias = band_ref[jnp.where(blk == 0, 0, jnp.where(blk == T // SWA_BLOCK - 1, 2, 1))]
    jobs = []
    for j in range(W512 // LANES):
        g = j // 2
        jobs.append((q_ref[0, :, _slab(j)],
                     [(kd_ref[0, pl.ds(ks, SWA_SPAN), _slab(g)], vd_ref[0, pl.ds(ks, SWA_SPAN), _slab(g)], bias),
                      (ckd_ref[0, 0, :, _slab(g)], cvd_ref[0, 0, :, _slab(g)], None)],
                     _sink_col(sink_ref, j, M)))
    for j, o in enumerate(_attend_slabs(jobs)):
        y_ref[0, :, _slab(j)] = o.astype(y_ref.dtype)


def _swa_lat_call(sink, q, kd, vd, ckd, cvd, l):
    B, T, _ = q.shape
    P = ckd.shape[2]
    qs = pl.BlockSpec((1, SWA_BLOCK, W512), lambda b, i: (b, i, 0))
    full = pl.BlockSpec((1, T, 2 * LANES), lambda b, i: (b, 0, 0))
    cs = pl.BlockSpec((1, 1, P, 2 * LANES), lambda b, i: (b, l, 0, 0))
    assert T // SWA_BLOCK >= 3
    band = _swa_band_masks()
    return pl.pallas_call(
        functools.partial(_swa_lat_kernel, T=T),
        out_shape=jax.ShapeDtypeStruct((B, T, W512), BF16),
        grid=(B, T // SWA_BLOCK),
        in_specs=[pl.BlockSpec(memory_space=pltpu.SMEM), qs, full, full, cs, cs, _const_spec(band.shape)],
        out_specs=qs,
        compiler_params=_cparams("parallel", "arbitrary"),
        name="swa_lat",
    )(sink, q, kd, vd, ckd, cvd, band)


FN_N2 = 256
FN_GP = 2


def _fourier_kernel(x_ref, cc_ref, m_ref, y_ref, *scratch, n1):
    gw = FN_GP * FNET_GC
    cc = cc_ref[...]
    for gp in range(FNET_GROUPS // FN_GP):
        for t1 in range(n1):
            ws = []
            for g in range(FN_GP):
                lo = t1 * FNET_WIDTH + (gp * FN_GP + g) * FNET_GC
                ws.append(_dot(x_ref[0, :, lo:lo + FNET_GC], cc))
            wr = jnp.concatenate([w[:, :FNET_GC] for w in ws], -1)
            wi = jnp.concatenate([w[:, FNET_GC:] for w in ws], -1)
            v = jnp.concatenate([wr, wi], 0).astype(BF16)
            b = _dot(m_ref[t1], v)
            if n1 == 1:
                y_ref[0, :, gp * gw:(gp + 1) * gw] = b.astype(y_ref.dtype)
            else:
                scratch[0][t1] = b
        if n1 > 1:
            _dft16_real(scratch[0], y_ref, gp * gw, gw)


FN_RCH = 16


def _dft16_real(bs, y_ref, col0, gw):
    tw = {m: (math.cos(2 * math.pi * m / 16), math.sin(2 * math.pi * m / 16)) for m in (1, 2, 3, 6, 9)}

    def body(rc, carry):
        r0 = pl.multiple_of(rc * FN_RCH, FN_RCH)
        re = [bs[t1, pl.ds(r0, FN_RCH), :] for t1 in range(16)]
        im = [bs[t1, pl.ds(FN_N2 + r0, FN_RCH), :] for t1 in range(16)]
        h_re = [[None] * 4 for _ in range(4)]
        h_im = [[None] * 4 for _ in range(4)]
        for b in range(4):
            s02r, s02i = re[b] + re[8 + b], im[b] + im[8 + b]
            d02r, d02i = re[b] - re[8 + b], im[b] - im[8 + b]
            s13r, s13i = re[4 + b] + re[12 + b], im[4 + b] + im[12 + b]
            d13r, d13i = re[4 + b] - re[12 + b], im[4 + b] - im[12 + b]
            g = [(s02r + s13r, s02i + s13i), (d02r + d13i, d02i - d13r),
                 (s02r - s13r, s02i - s13i), (d02r - d13i, d02i + d13r)]
            for c in range(4):
                gr, gi = g[c]
                m = b * c
                if m == 0:
                    hr, hi = gr, gi
                elif m == 4:
                    hr, hi = gi, -gr
                else:
                    cs, sn = tw[m]
                    hr = cs * gr + sn * gi
                    hi = (cs * gi - sn * gr) if b % 2 else None
                h_re[b][c], h_im[b][c] = hr, hi
        for c in range(4):
            p = h_re[0][c] + h_re[2][c]
            q = h_re[0][c] - h_re[2][c]
            r = h_re[1][c] + h_re[3][c]
            s = h_im[1][c] - h_im[3][c]
            for d, val in enumerate((p + r, q + s, p - r, q - s)):
                k1 = c + 4 * d
                y_ref[0, pl.ds(k1 * FN_N2 + r0, FN_RCH), col0:col0 + gw] = val.astype(y_ref.dtype)
        return carry

    lax.fori_loop(0, FN_N2 // FN_RCH, body, 0)


def _dft_consts(T):
    n1 = T // FN_N2
    j = np.arange(FNET_GC)
    ang = 2 * np.pi * np.outer(j, j) / FNET_GC
    sc = 1.0 / math.sqrt(FNET_GC)
    cc = np.concatenate([np.cos(ang) * sc, -np.sin(ang) * sc], 1)
    k2 = np.arange(FN_N2)[:, None]
    t2 = np.arange(FN_N2)[None, :]
    st = 1.0 / math.sqrt(T)
    mats = []
    for t1 in range(n1):
        th = 2 * np.pi * ((k2 * (n1 * t2 + t1)) % T) / T
        c, s = np.cos(th) * st, np.sin(th) * st
        top = np.concatenate([c, s], 1)
        mats.append(top if n1 == 1 else np.concatenate([top, np.concatenate([-s, c], 1)], 0))
    return n1, jnp.asarray(cc, BF16), jnp.asarray(np.stack(mats), BF16)


def _fourier_call(xv, T):
    B = xv.shape[0]
    n1, cc, mats = _dft_consts(T)
    assert n1 in (1, 16)
    gw = FN_GP * FNET_GC
    scratch = [pltpu.VMEM((n1, 2 * FN_N2, gw), F32)] if n1 > 1 else []
    return pl.pallas_call(
        functools.partial(_fourier_kernel, n1=n1),
        out_shape=jax.ShapeDtypeStruct((B, T, FNET_WIDTH), BF16),
        grid=(B,),
        in_specs=[pl.BlockSpec((1, FN_N2, n1 * FNET_WIDTH), lambda b: (b, 0, 0)),
                  _const_spec(cc.shape), _const_spec(mats.shape)],
        out_specs=pl.BlockSpec((1, T, FNET_WIDTH), lambda b: (b, 0, 0)),
        scratch_shapes=scratch,
        compiler_params=_cparams("parallel"),
        name="fourier",
    )(xv, cc, mats)


def _route(scores, biased):
    one, zero = jnp.float32(1.0), jnp.float32(0.0)
    in2, gscore = [], []
    for g in range(N_EXPERT_GROUPS):
        vs = [biased[g * EXPERTS_PER_GROUP + j:g * EXPERTS_PER_GROUP + j + 1] for j in range(EXPERTS_PER_GROUP)]
        gs = None
        for j in range(EXPERTS_PER_GROUP):
            rank = None
            for i in range(EXPERTS_PER_GROUP):
                if i == j:
                    continue
                beats = (vs[i] >= vs[j]) if i < j else (vs[i] > vs[j])
                t = jnp.where(beats, one, zero)
                rank = t if rank is None else rank + t
            keep = jnp.where(rank < 2.0, one, zero)
            in2.append(keep)
            t = keep * vs[j]
            gs = t if gs is None else gs + t
        gscore.append(gs)
    rows, sel = [], []
    for g in range(N_EXPERT_GROUPS):
        lost = None
        for i in range(N_EXPERT_GROUPS):
            if i == g:
                continue
            beats = (gscore[i] >= gscore[g]) if i < g else (gscore[i] > gscore[g])
            t = jnp.where(beats, one, zero)
            lost = t if lost is None else lost + t
        gsel = jnp.where(lost < 1.0, one, zero)
        for j in range(EXPERTS_PER_GROUP):
            e = g * EXPERTS_PER_GROUP + j
            sel.append(gsel * in2[e])
            rows.append(sel[-1] * scores[e:e + 1])
    tot = rows[0]
    for rr in rows[1:]:
        tot = tot + rr
    return jnp.concatenate(rows, 0) / tot, jnp.concatenate(sel, 0)


HP_ROWS = D_MODEL // 2 // LANES
Y_ROWS = D_MODEL // LANES


def _pack_halves(v):
    w = v.shape[1] // 2
    lo = lax.bitcast_convert_type(v[:, :w].astype(F32), jnp.uint32)
    hi = lax.bitcast_convert_type(v[:, w:].astype(F32), jnp.uint32)
    return (lo >> 16) | hi


def _unpack_halves(p):
    lo = lax.bitcast_convert_type(p << 16, F32).astype(BF16)
    hi = lax.bitcast_convert_type(p & jnp.uint32(0xFFFF0000), F32).astype(BF16)
    return lo, hi


def _merge_kernel(x_ref, mod_ref, ya_ref, yb_ref, yc_ref, yd_ref, wg_ref, wb_ref, wo_ref,
                  lng_ref, lnb_ref, wr_ref, rb_ref, x1_ref, hp_ref, rt_ref, *, nsub):
    m = mod_ref[0]
    sub = x_ref.shape[0] // nsub
    y_refs = (ya_ref, yb_ref, yc_ref, yd_ref)
    st = [dict() for _ in range(nsub)]

    def rows(k):
        return slice(k * sub, (k + 1) * sub)

    def norm_in(k):
        st[k]['x'] = x_ref[rows(k), :]
        st[k]['h'] = (_ln(st[k]['x']) * (1.0 + m[1:2]) + m[0:1]).astype(BF16)
        st[k]['merged'] = None

    def branch(k, b):
        gate = 0.5 * jnp.tanh(_dot(st[k]['h'], wg_ref[0, :, b * D_MODEL:(b + 1) * D_MODEL])) + 0.5
        term = gate * _dot(y_refs[b][rows(k), :], wb_ref[0, b])
        st[k]['merged'] = term if st[k]['merged'] is None else st[k]['merged'] + term

    def project(k):
        st[k]['out'] = _dot(st[k]['merged'].astype(BF16), wo_ref[0])

    def norm_out(k):
        x1 = _ln(ALPHA * st[k]['x'] + m[2:3] * st[k]['out']) * lng_ref[...] + lnb_ref[...]
        x1_ref[rows(k), :] = x1
        st[k]['h2'] = (_ln(x1) * (1.0 + m[4:5]) + m[3:4]).astype(BF16)

    def pack(k):
        packed = _pack_halves(st[k]['h2'])
        for c in range(HP_ROWS):
            hp_ref[pl.ds(k * sub * HP_ROWS + c, sub, stride=HP_ROWS), :] = packed[:, c * LANES:(c + 1) * LANES]

    def route(k):
        logits = _dot_nt(wr_ref[...], st[k]['h2'])
        e = jnp.exp(logits - jnp.max(logits, 0, keepdims=True))
        scores = e / jnp.sum(e, 0, keepdims=True)
        comb, sel = _route(scores, scores + rb_ref[...])
        rt_ref[:, rows(k)] = jnp.concatenate([comb, sel], 0)

    tail = (norm_out, pack, route)
    for k in range(nsub):
        norm_in(k)
    for k in range(nsub):
        for b in range(N_BRANCH):
            branch(k, b)
            if k > 0 and b < len(tail):
                tail[b](k - 1)
        project(k)
    for stage in tail:
        stage(nsub - 1)


def _merge_call(x, mod, ys, wg, wb, wo, l, lng, lnb, wr_t, rb, T, TM):
    N = x.shape[0]
    tpb = T // TM
    tok = lambda w: pl.BlockSpec((TM, w), lambda i: (i, 0))
    return pl.pallas_call(
        functools.partial(_merge_kernel, nsub=max(1, TM // SUB_TM)),
        out_shape=[jax.ShapeDtypeStruct((N, D_MODEL), F32), jax.ShapeDtypeStruct((N * HP_ROWS, LANES), jnp.uint32),
                   jax.ShapeDtypeStruct((2 * N_EXPERTS, N), F32)],
        grid=(N // TM,),
        in_specs=[tok(D_MODEL), pl.BlockSpec((1, SUBLANES, D_MODEL), lambda i: (i // tpb, 0, 0)),
                  tok(W512), tok(W512), tok(W512), tok(W512),
                  _layer_spec(wg.shape, l), _layer_spec(wb.shape, l), _layer_spec(wo.shape, l),
                  _const_spec(lng.shape), _const_spec(lnb.shape), _const_spec(wr_t.shape), _const_spec(rb.shape)],
        out_specs=[tok(D_MODEL), pl.BlockSpec((TM * HP_ROWS, LANES), lambda i: (i, 0)),
                   pl.BlockSpec((2 * N_EXPERTS, TM), lambda i: (0, i))],
        compiler_params=_cparams("parallel"),
        name="merge",
    )(x, mod, *ys, wg, wb, wo, lng, lnb, wr_t, rb)


MOE_TM = 2048
MOE_RB = 256
MOE_KC = 4
MOE_CH = MOE_TM // MOE_KC
MOE_SLOTS = 2 * MOE_TM + N_EXPERTS * SUBLANES + MOE_RB
MOE_UNROLL = 8


def _route_kernel(rt_ref, u_ref, pos_ref, wts_ref, seg_ref):
    comb = rt_ref[0:N_EXPERTS, :]
    sel = rt_ref[N_EXPERTS:2 * N_EXPERTS, :]
    nb = MOE_TM // LANES
    stacked = jnp.concatenate([sel[:, b * LANES:(b + 1) * LANES] for b in range(nb)], 0)
    within = _dot(stacked.astype(BF16), u_ref[...])
    tot = jnp.sum(stacked, -1, keepdims=True)
    base = jnp.zeros((N_EXPERTS, 1), F32)
    bases = []
    for b in range(nb):
        bases.append(base)
        base = base + tot[b * N_EXPERTS:(b + 1) * N_EXPERTS]
    cnt = base
    padded = jnp.floor((cnt + (SUBLANES - 1.0)) * (1.0 / SUBLANES)) * SUBLANES
    rowi = lax.broadcasted_iota(jnp.int32, (N_EXPERTS, 1), 0)
    off = jnp.zeros((N_EXPERTS, 1), F32)
    for e in range(N_EXPERTS - 1):
        off = off + jnp.where(rowi > e, padded[e:e + 1, :], 0.0)
    pos_rows, wts_rows = [], []
    for b in range(nb):
        slot = within[b * N_EXPERTS:(b + 1) * N_EXPERTS] + (bases[b] + off)
        seen = jnp.zeros((1, LANES), F32)
        acc = [jnp.zeros((1, LANES), F32) for _ in range(4)]
        for e in range(N_EXPERTS):
            s_e = sel[e:e + 1, b * LANES:(b + 1) * LANES]
            c_e = comb[e:e + 1, b * LANES:(b + 1) * LANES]
            first = jnp.where(seen == 0.0, s_e, 0.0)
            second = jnp.where(seen == 1.0, s_e, 0.0)
            acc[0] = acc[0] + first * slot[e:e + 1]
            acc[1] = acc[1] + second * slot[e:e + 1]
            acc[2] = acc[2] + first * c_e
            acc[3] = acc[3] + second * c_e
            seen = seen + s_e
        pos_rows += acc[0:2]
        wts_rows += acc[2:4]
    pos_ref[...] = (jnp.concatenate(pos_rows, 0) * float(HP_ROWS)).astype(jnp.int32).reshape(2 * MOE_TM)
    wts_ref[...] = jnp.concatenate(wts_rows, 0).reshape(2 * MOE_TM)
    lane = lax.broadcasted_iota(jnp.int32, (N_EXPERTS, LANES), 1)
    diag = lane == lax.broadcasted_iota(jnp.int32, (N_EXPERTS, LANES), 0)
    off_row = jnp.sum(jnp.where(diag, off, 0.0), 0, keepdims=True)
    cnt_row = jnp.sum(jnp.where(diag, cnt, 0.0), 0, keepdims=True)
    seg_ref[0] = jnp.concatenate([off_row, cnt_row, jnp.zeros((SUBLANES - 2, LANES), F32)], 0).astype(jnp.int32)


def _route_call(rt):
    N = rt.shape[1]
    nt = N // MOE_TM
    u = jnp.asarray(np.triu(np.ones((LANES, LANES), np.float32), 1), BF16)
    return pl.pallas_call(
        _route_kernel,
        out_shape=[jax.ShapeDtypeStruct((nt * 2 * MOE_TM,), jnp.int32),
                   jax.ShapeDtypeStruct((nt * 2 * MOE_TM,), F32),
                   jax.ShapeDtypeStruct((nt, SUBLANES, LANES), jnp.int32)],
        grid=(nt,),
        in_specs=[pl.BlockSpec((2 * N_EXPERTS, MOE_TM), lambda i: (0, i)), _const_spec(u.shape)],
        out_specs=[pl.BlockSpec((2 * MOE_TM,), lambda i: (i,)),
                   pl.BlockSpec((2 * MOE_TM,), lambda i: (i,)),
                   pl.BlockSpec((1, SUBLANES, LANES), lambda i: (i, 0, 0))],
        compiler_params=_cparams("parallel"),
        name="route",
    )(rt, u)


def _table_index(it):
    per_block = LANES // MOE_UNROLL
    return (it // per_block) * (2 * LANES) + (it % per_block) * MOE_UNROLL


def _moe_kernel(pos_ref, wts_ref, seg_ref, hp_ref, w1_ref, w3_ref, w2_ref, x1_ref, mod_ref, lng_ref, lnb_ref,
                o_ref, xs, ys, oc):
    s = pl.program_id(1)
    half = D_MODEL // 2

    @pl.when(s == 0)
    def _dispatch():
        xs[...] = jnp.zeros_like(xs)

        def body(it, carry):
            t0 = pl.multiple_of(it * (MOE_UNROLL * HP_ROWS), MOE_UNROLL * HP_ROWS)
            i0 = _table_index(it)
            for j in range(MOE_UNROLL):
                row = hp_ref[pl.ds(t0 + HP_ROWS * j, HP_ROWS), :]
                xs[pl.ds(pl.multiple_of(pos_ref[i0 + j], HP_ROWS), HP_ROWS), :] = row
                xs[pl.ds(pl.multiple_of(pos_ref[i0 + LANES + j], HP_ROWS), HP_ROWS), :] = row
            return carry

        lax.fori_loop(0, MOE_TM // MOE_UNROLL, body, 0)

    @pl.when(s < N_EXPERTS)
    def _expert():
        off = seg_ref[0, 0, s]
        cnt = seg_ref[0, 1, s]
        nblk = (cnt + (MOE_RB // 2 - 1)) // MOE_RB

        def ffn_rows(r0, nrows):
            parts = [_unpack_halves(xs[pl.ds(pl.multiple_of(r0 * HP_ROWS, SUBLANES) + c, nrows, stride=HP_ROWS), :])
                     for c in range(HP_ROWS)]
            lo = jnp.concatenate([p[0] for p in parts], -1)
            hi = jnp.concatenate([p[1] for p in parts], -1)
            a = _dot(lo, w1_ref[0, 0, :half, :]) + _dot(hi, w1_ref[0, 0, half:, :])
            g = _dot(lo, w3_ref[0, 0, :half, :]) + _dot(hi, w3_ref[0, 0, half:, :])
            act = (a * jax.nn.sigmoid(a) * g).astype(BF16)
            y = _dot(act, w2_ref[0, 0])
            y0 = pl.multiple_of(r0 * Y_ROWS, SUBLANES)
            for c in range(Y_ROWS):
                ys[pl.ds(y0 + c, nrows, stride=Y_ROWS), :] = y[:, c * LANES:(c + 1) * LANES]

        def blk(i, carry):
            ffn_rows(pl.multiple_of(off + i * MOE_RB, SUBLANES), MOE_RB)
            return carry

        lax.fori_loop(0, nblk, blk, 0)

        @pl.when(cnt > nblk * MOE_RB)
        def _tail():
            ffn_rows(pl.multiple_of(off + nblk * MOE_RB, SUBLANES), MOE_RB // 2)

    @pl.when(s >= N_EXPERTS)
    def _combine():
        it0 = (s - N_EXPERTS) * (MOE_CH // MOE_UNROLL)

        def body(it, carry):
            t0 = pl.multiple_of(it * (MOE_UNROLL * Y_ROWS), MOE_UNROLL * Y_ROWS)
            i0 = _table_index(it0 + it)
            for j in range(MOE_UNROLL):
                p0 = pl.multiple_of(pos_ref[i0 + j] * (Y_ROWS // HP_ROWS), Y_ROWS)
                p1 = pl.multiple_of(pos_ref[i0 + LANES + j] * (Y_ROWS // HP_ROWS), Y_ROWS)
                oc[pl.ds(t0 + Y_ROWS * j, Y_ROWS), :] = (wts_ref[i0 + j] * ys[pl.ds(p0, Y_ROWS), :]
                                                         + wts_ref[i0 + LANES + j] * ys[pl.ds(p1, Y_ROWS), :])
            return carry

        lax.fori_loop(0, MOE_CH // MOE_UNROLL, body, 0)
        moe = jnp.concatenate([oc[pl.ds(c, MOE_CH, stride=Y_ROWS), :] for c in range(Y_ROWS)], -1)
        m = mod_ref[0]
        o_ref[...] = _ln(ALPHA * x1_ref[...] + m[5:6] * moe) * lng_ref[...] + lnb_ref[...]


def _moe_call(pos, wts, seg, hp, w1, w3, w2, l, x1, mod, lng, lnb, T):
    N = x1.shape[0]
    nt = N // MOE_TM
    flat = pl.BlockSpec((2 * MOE_TM,), lambda i, s: (i,), memory_space=pltpu.SMEM)
    wspec = lambda shp: pl.BlockSpec((1, 1) + shp, lambda i, s: (l, jnp.minimum(s, N_EXPERTS - 1), 0, 0))
    chunk = pl.BlockSpec((MOE_CH, D_MODEL),
                         lambda i, s: (i * MOE_KC + jnp.clip(s - N_EXPERTS, 0, MOE_KC - 1), 0))
    return pl.pallas_call(
        _moe_kernel,
        out_shape=jax.ShapeDtypeStruct((N, D_MODEL), F32),
        grid=(nt, N_EXPERTS + MOE_KC),
        in_specs=[flat, flat,
                  pl.BlockSpec((1, SUBLANES, LANES), lambda i, s: (i, 0, 0), memory_space=pltpu.SMEM),
                  pl.BlockSpec((MOE_TM * HP_ROWS, LANES), lambda i, s: (i, 0)),
                  wspec((D_MODEL, D_EXPERT)), wspec((D_MODEL, D_EXPERT)), wspec((D_EXPERT, D_MODEL)),
                  chunk, pl.BlockSpec((1, SUBLANES, D_MODEL), lambda i, s: ((i * MOE_TM) // T, 0, 0)),
                  _const_spec(lng.shape), _const_spec(lnb.shape)],
        out_specs=chunk,
        scratch_shapes=[pltpu.VMEM((MOE_SLOTS * HP_ROWS, LANES), jnp.uint32),
                        pltpu.VMEM((MOE_SLOTS * Y_ROWS, LANES), F32),
                        pltpu.VMEM((MOE_CH * Y_ROWS, LANES), F32)],
        compiler_params=_cparams("parallel", "arbitrary"),
        name="moe",
    )(pos, wts, seg, hp, w1, w3, w2, x1, mod, lng, lnb)


def _rope_tables(T):
    t = jnp.arange(T)
    nf = HEAD_DIM // 4
    inv = ROPE_BASE ** (-jnp.arange(nf, dtype=F32) / nf)
    ar = (t // GRID_W).astype(F32)[:, None] * inv
    ac = (t % GRID_W).astype(F32)[:, None] * inv
    cos = jnp.concatenate([jnp.cos(ar), jnp.cos(ar), jnp.cos(ac), jnp.cos(ac)], -1)
    sin = jnp.concatenate([-jnp.sin(ar), jnp.sin(ar), -jnp.sin(ac), jnp.sin(ac)], -1)
    return jnp.tile(cos, (1, LANES // HEAD_DIM)), jnp.tile(sin, (1, LANES // HEAD_DIM))


def _nat_bias_table(rpb):
    q = np.arange(GRID_W)
    kc = np.arange(GRID_W)
    cstart = np.clip(q - NAT_KC // 2, 0, GRID_W - NAT_KC)
    ok = (kc[None, :] >= cstart[:, None]) & (kc[None, :] < cstart[:, None] + NAT_KC)
    cidx = np.clip(kc[None, :] - q[:, None] + NAT_KC - 1, 0, 2 * NAT_KC - 2)
    onehot = (np.arange(2 * NAT_KC - 1)[:, None, None] == cidx[None]).astype(np.float32)
    t = jnp.einsum('hrc,cqk->hqrk', rpb.astype(F32), jnp.asarray(onehot), precision=lax.Precision.HIGHEST)
    t = jnp.where(jnp.asarray(ok)[None, :, None, :], t * LOG2E, NEG_INF)
    per_d = [t[:, :, NAT_KR - 1 - d:2 * NAT_KR - 1 - d, :].reshape(NAT_HEADS // 2, 2 * GRID_W, NAT_KR * GRID_W)
             for d in range(NAT_KR)]
    return jnp.stack(per_d, 1)


def _dup_heads(a):
    a = jnp.broadcast_to(a[..., :, None, :], a.shape[:-1] + (2, a.shape[-1]))
    return a.reshape(a.shape[:-3] + (-1,))


def _block_diag(w):
    eye = jnp.eye(RNN_BLOCKS, dtype=w.dtype)
    return jnp.einsum('dnio,nm->dnimo', w, eye).reshape(2, RNN_WIDTH, RNN_WIDTH)


def _pad_rows(a, rows=SUBLANES):
    return jnp.pad(a, ((0, rows - a.shape[0]),) + ((0, 0),) * (a.ndim - 1))


def kernel(x_prompt, x_sample, c, cache_nat_k, cache_nat_v, cache_swa_k, cache_swa_v, state_rglru, c_ctx,
           w_ada, b_ada, w_in, rg_conv_w, rg_conv_b, rg_wa, rg_ba, rg_wx, rg_bx, rg_lambda, nat_rpb,
           swa_sink, w_branch, w_out, ln_g, ln_b, w_router, router_bias, w1, w3, w2):
    B_c, T_c, _ = x_prompt.shape
    B_l, T_l, _ = x_sample.shape
    P = cache_nat_k.shape[2]

    cv = jnp.concatenate([c_ctx[None], c, jnp.zeros((ADA_ROWS - 1 - B_l, D_MODEL), F32)], 0)
    mod_all = _ada_call(cv, w_ada, b_ada).reshape(DEPTH, ADA_ROWS, 6, D_MODEL)
    mod_all = jnp.pad(mod_all, ((0, 0), (0, 0), (0, SUBLANES - 6), (0, 0)))

    kv_w = SWA_KV_HEADS * HEAD_DIM
    dup_w = 2 * kv_w
    xa0, ga0, nq0, nk0, nv0, sq0, sk0 = (i * W512 for i in range(7))
    sv0 = sk0 + kv_w
    xf0 = sv0 + kv_w
    assert xf0 + W512 == GATE_OFF
    w_in_b = w_in.astype(BF16)
    head = lambda c0, g: w_in_b[:, :, c0 + g * HEAD_DIM:c0 + (g + 1) * HEAD_DIM]
    w_tail_ctx = jnp.concatenate([head(c0, g) for c0 in (sk0, sv0) for g in (0, 0, 1, 1)], -1)
    w_tail_lat = jnp.concatenate([w_in_b[:, :, xf0:GATE_OFF], w_tail_ctx], -1)
    w_gate = (0.5 * w_in[:, :, GATE_OFF:]).astype(BF16)
    w_branch_b = w_branch.astype(BF16)
    w_out_b = w_out.astype(BF16)
    w1_b, w3_b, w2_b = w1.astype(BF16), w3.astype(BF16), w2.astype(BF16)
    wr_t = w_router.T.astype(BF16)
    wa_bd = jnp.stack([_block_diag(0.5 * rg_wa[l]) for l in range(DEPTH)]).astype(BF16)
    wx_bd = jnp.stack([_block_diag(0.5 * rg_wx[l]) for l in range(DEPTH)]).astype(BF16)

    def out(src, off, width, dtype, scale=1.0, rope=False, fourier=False, cache=None):
        return (src, off, width, dtype, scale, rope, fourier, cache)

    plan_ctx = (out(0, xa0, W512, F32), out(0, ga0, W512, F32), out(0, nq0, W512, BF16, Q_SCALE),
                out(0, nk0, W512, F32, cache=0), out(0, nv0, W512, F32, cache=1), out(0, sq0, W512, BF16, Q_SCALE),
                out(0, sk0, kv_w, F32, cache=2), out(0, sv0, kv_w, F32, cache=3),
                out(0, xf0, W512, BF16, fourier=True),
                out(1, 0, dup_w, BF16), out(1, dup_w, dup_w, BF16))
    plan_lat = (out(0, xa0, W512, F32), out(0, ga0, W512, F32), out(0, nq0, W512, BF16, Q_SCALE),
                out(0, nk0, W512, BF16), out(0, nv0, W512, BF16), out(0, sq0, W512, BF16, Q_SCALE, rope=True),
                out(1, 0, W512, BF16, fourier=True),
                out(1, W512, dup_w, BF16, rope=True), out(1, W512 + dup_w, dup_w, BF16))

    rope_tabs = _rope_tables(T_l)
    ck_nat = cache_nat_k.reshape(B_l, DEPTH, P, W512).astype(BF16)
    cv_nat = cache_nat_v.reshape(B_l, DEPTH, P, W512).astype(BF16)
    ckd_swa = _dup_heads(cache_swa_k).astype(BF16)
    cvd_swa = _dup_heads(cache_swa_v).astype(BF16)
    state8 = jnp.pad(state_rglru, ((0, 0), (0, 0), (0, SUBLANES - 2), (0, 0)))
    zero_state = jnp.zeros((B_c, SUBLANES, RNN_WIDTH), F32)

    def layer(x, l, ctx_pass, caches=None):
        B, T = (B_c, T_c) if ctx_pass else (B_l, T_l)
        TM = min(T, TOKEN_TM)
        modb = jnp.broadcast_to(mod_all[l, 0:1], (B, SUBLANES, D_MODEL)) if ctx_pass else mod_all[l, 1:1 + B_l]
        if ctx_pass:
            xa, ga, nq, nk, nv, sq, sk, sv, xf, skd, svd = _pre_call(
                x, modb, w_in_b, GATE_OFF, w_tail_ctx, l, plan_ctx, B, T, TM, None, caches)
        else:
            xa, ga, nq, nk, nv, sq, xf, skd, svd = _pre_call(
                x, modb, w_in_b, sk0, w_tail_lat, l, plan_lat, B, T, TM, rope_tabs)
        r3 = lambda a: a.reshape(B, T, a.shape[-1])
        h0 = zero_state if ctx_pass else state8[:, l]
        ya, hfin = _rglru_call(
            r3(xa), r3(ga), _pad_rows(rg_conv_w[l]), rg_conv_b[l][None], wa_bd[l], wx_bd[l],
            _pad_rows(0.5 * rg_ba[l]), _pad_rows(0.5 * rg_bx[l]), _pad_rows(rg_lambda[l]), h0)
        if ctx_pass:
            yb, yc = _attn_ctx_call(swa_sink[l], r3(nq), nk, nv, r3(sq), r3(skd), r3(svd), l)
        else:
            yb = _nat_lat_call(r3(nq), r3(nk), r3(nv), ck_nat, cv_nat, _nat_bias_table(nat_rpb[l]), l)
            yc = _swa_lat_call(swa_sink[l], r3(sq), r3(skd), r3(svd), ckd_swa, cvd_swa, l)
        yd = _fourier_call(xf, T)
        f2 = lambda a: a.reshape(B * T, a.shape[-1])
        x1, hp, rt = _merge_call(
            x, modb, (f2(ya), f2(yb), f2(yc), f2(yd)), w_gate, w_branch_b, w_out_b, l,
            ln_g[l, 0][None], ln_b[l, 0][None], wr_t,
            jnp.broadcast_to(router_bias[:, None], (N_EXPERTS, min(T, SUB_TM))), T, TM)
        pos, wts, seg = _route_call(rt)
        x2 = _moe_call(pos, wts, seg, hp, w1_b, w3_b, w2_b, l, x1, modb,
                       ln_g[l, 1][None], ln_b[l, 1][None], T)
        new = ((nk, nv, sk, sv), hfin[:, :2]) if ctx_pass else None
        return x2, new

    y = x_prompt.reshape(B_c * T_c, D_MODEL)
    caches = tuple(jnp.zeros((B_c, DEPTH, T_c, w), F32)
                   for w in (W512, W512, SWA_KV_HEADS * HEAD_DIM, SWA_KV_HEADS * HEAD_DIM))
    states = []
    for l in range(DEPTH):
        y, (caches, st) = layer(y, l, True, caches)
        states.append(st)
    y_prompt = y.reshape(B_c, T_c, D_MODEL)
    new_nat_k = caches[0].reshape(B_c, DEPTH, T_c, NAT_HEADS, HEAD_DIM)
    new_nat_v = caches[1].reshape(B_c, DEPTH, T_c, NAT_HEADS, HEAD_DIM)
    new_swa_k = caches[2].reshape(B_c, DEPTH, T_c, SWA_KV_HEADS, HEAD_DIM)
    new_swa_v = caches[3].reshape(B_c, DEPTH, T_c, SWA_KV_HEADS, HEAD_DIM)
    new_state = jnp.stack(states, 1)

    y = x_sample.reshape(B_l * T_l, D_MODEL)
    for l in range(DEPTH):
        y, _ = layer(y, l, False)
    y_sample = y.reshape(B_l, T_l, D_MODEL)
    return (y_prompt, y_sample, new_nat_k, new_nat_v, new_swa_k, new_swa_v, new_state)
```

```python
import functools
import math

import jax
import jax.numpy as jnp
import numpy as np
from jax import lax
from jax.experimental import pallas as pl
from jax.experimental.pallas import tpu as pltpu

F32 = jnp.float32
BF16 = jnp.bfloat16

D_MODEL = 1024
DEPTH = 4
GRID_W = 64
HEAD_DIM = 64
ATTN_SCALE = HEAD_DIM ** -0.5
LOG2E = math.log2(math.e)
Q_SCALE = ATTN_SCALE * LOG2E
NEG_INF = -1e30
LN_EPS = 1e-5
ALPHA = (2 * DEPTH) ** 0.25
ROPE_BASE = 10000.0
RNN_WIDTH = 512
RNN_BLOCKS = 8
CONV_W = 4
RGLRU_C = 8.0
NAT_HEADS = 8
NAT_KR = 8
NAT_KC = 16
SWA_HEADS = 8
SWA_KV_HEADS = 2
SWA_WINDOW = 128
SWA_BLOCK = 128
SWA_SPAN = SWA_BLOCK + 2 * SWA_WINDOW
FNET_GROUPS = 4
FNET_WIDTH = 512
FNET_GC = FNET_WIDTH // FNET_GROUPS
N_BRANCH = 4
N_EXPERTS = 16
N_EXPERT_GROUPS = 4
EXPERTS_PER_GROUP = N_EXPERTS // N_EXPERT_GROUPS
D_EXPERT = 512
W512 = 512
GATE_OFF = 3840

LANES = 128
SUBLANES = 8
VMEM_LIMIT = 56 * 1024 * 1024
TOKEN_TM = 1024
SUB_TM = 512


def _cparams(*sem):
    return pltpu.CompilerParams(dimension_semantics=sem, vmem_limit_bytes=VMEM_LIMIT)


def _const_spec(shape):
    nd = len(shape)
    return pl.BlockSpec(shape, lambda *_: (0,) * nd, pipeline_mode=pl.Buffered(1))


def _layer_spec(shape, l):
    nd = len(shape)
    return pl.BlockSpec((1,) + tuple(shape[1:]), lambda *_: (l,) + (0,) * (nd - 1), pipeline_mode=pl.Buffered(1))


def _ln(x):
    mu = jnp.mean(x, -1, keepdims=True)
    xc = x - mu
    var = jnp.mean(xc * xc, -1, keepdims=True)
    return xc * lax.rsqrt(var + LN_EPS)


def _dot(a, b):
    return jnp.dot(a, b, preferred_element_type=F32)


def _dot_nt(a, b):
    return lax.dot_general(a, b, (((1,), (1,)), ((), ())), preferred_element_type=F32)


ADA_ROWS = 2 * SUBLANES
ADA_TN = D_MODEL


def _ada_kernel(c_ref, w_ref, b_ref, o_ref):
    cv = c_ref[...]
    s = (cv * jax.nn.sigmoid(cv)).astype(BF16)
    o_ref[0] = _dot(s, w_ref[0].astype(BF16)) + b_ref[0]


def _ada_call(cv, w_ada, b_ada):
    n = w_ada.shape[-1]
    return pl.pallas_call(
        _ada_kernel,
        out_shape=jax.ShapeDtypeStruct((DEPTH, ADA_ROWS, n), F32),
        grid=(DEPTH, n // ADA_TN),
        in_specs=[
            pl.BlockSpec((ADA_ROWS, D_MODEL), lambda l, j: (0, 0)),
            pl.BlockSpec((1, D_MODEL, ADA_TN), lambda l, j: (l, 0, j)),
            pl.BlockSpec((1, 1, ADA_TN), lambda l, j: (l, 0, j)),
        ],
        out_specs=pl.BlockSpec((1, ADA_ROWS, ADA_TN), lambda l, j: (l, 0, j)),
        compiler_params=_cparams("parallel", "parallel"),
        name="ada",
    )(cv, w_ada, b_ada.reshape(DEPTH, 1, n))


def _rope(u, cos, sin):
    lane = lax.broadcasted_iota(jnp.int32, cos.shape, 1)
    first = (lane & 31) < 16
    outs = []
    for j in range(u.shape[1] // LANES):
        s = u[:, j * LANES:(j + 1) * LANES]
        partner = jnp.where(first, pltpu.roll(s, LANES - 16, 1), pltpu.roll(s, 16, 1))
        outs.append(s * cos + partner * sin)
    return outs[0] if len(outs) == 1 else jnp.concatenate(outs, -1)


def _pre_kernel(*refs, plan, rope, n1, nsub, n_alias, seq_t):
    x_ref, mod_ref, wm_ref, wt_ref = refs[:4]
    refs = refs[4:]
    if rope:
        cos_ref, sin_ref = refs[:2]
        refs = refs[2:]
    refs = refs[n_alias:]
    outs = refs[:len(plan)]
    m = mod_ref[0]
    sub = x_ref.shape[0] // nsub
    hs = [(_ln(x_ref[k * sub:(k + 1) * sub, :]) * (1.0 + m[1:2]) + m[0:1]).astype(BF16) for k in range(nsub)]
    for (src, off, width, _, scale, do_rope, fourier, slot), o_ref in zip(plan, outs):
        w_ref = wt_ref if src else wm_ref
        for k in range(nsub):
            rows = slice(k * sub, (k + 1) * sub)
            u = _dot(hs[k], w_ref[0, :, off:off + width])
            if do_rope:
                u = _rope(u, cos_ref[rows, :], sin_ref[rows, :])
            if scale != 1.0:
                u = u * scale
            if fourier and n1 > 1:
                stage = refs[len(plan)]
                for c in range(width // LANES):
                    stage[k, c] = u[:, c * LANES:(c + 1) * LANES]
                r = sub // n1
                for t1 in range(n1):
                    for c in range(width // LANES):
                        lo = t1 * width + c * LANES
                        o_ref[0, k * r:(k + 1) * r, lo:lo + LANES] = (
                            stage[k, c, pl.ds(t1, r, stride=n1), :].astype(o_ref.dtype))
            elif fourier or slot is not None:
                if sub <= seq_t:
                    pos = slice((k * sub) % seq_t, (k * sub) % seq_t + sub)
                    pieces = [((k * sub) // seq_t, pos, u)]
                else:
                    pieces = [(k * (sub // seq_t) + j, slice(None), u[j * seq_t:(j + 1) * seq_t])
                              for j in range(sub // seq_t)]
                for b_, pos, val in pieces:
                    if fourier:
                        o_ref[b_, pos, :] = val.astype(o_ref.dtype)
                    else:
                        o_ref[b_, 0, pos, :] = val.astype(o_ref.dtype)
            else:
                o_ref[rows, :] = u.astype(o_ref.dtype)


def _pre_call(x, mod, w_main, n_main, w_tail, l, plan, B, T, TM, rope_tabs, caches=None):
    N = x.shape[0]
    tpb = max(1, T // TM)
    spt = max(1, TM // T)
    seq_t = min(T, TM)
    nsub = max(1, TM // SUB_TM)
    n1 = T // FN_N2
    rope = rope_tabs is not None
    n_tail = w_tail.shape[-1]
    in_specs = [
        pl.BlockSpec((TM, D_MODEL), lambda i: (i, 0)),
        pl.BlockSpec((1, SUBLANES, D_MODEL), lambda i: ((i * TM) // T, 0, 0)),
        pl.BlockSpec((1, D_MODEL, n_main), lambda i: (l, 0, 0), pipeline_mode=pl.Buffered(1)),
        pl.BlockSpec((1, D_MODEL, n_tail), lambda i: (l, 0, 0), pipeline_mode=pl.Buffered(1)),
    ]
    args = [x, mod, w_main, w_tail]
    if rope:
        in_specs += [pl.BlockSpec((TM, LANES), lambda i: (i % tpb, 0))] * 2
        args += list(rope_tabs)
    out_shape, out_specs, scratch, aliases = [], [], [], {}
    for k, p in enumerate(plan):
        if p[6]:
            out_shape.append(jax.ShapeDtypeStruct((B, FN_N2, n1 * p[2]), p[3]))
            if n1 > 1:
                out_specs.append(pl.BlockSpec((1, TM // n1, n1 * p[2]), lambda i: (i // tpb, i % tpb, 0)))
                scratch.append(pltpu.VMEM((nsub, p[2] // LANES, TM // nsub, LANES), F32))
            else:
                out_specs.append(pl.BlockSpec((spt, seq_t, p[2]), lambda i: (i // tpb, i % tpb, 0)))
        elif p[7] is not None:
            out_shape.append(jax.ShapeDtypeStruct((B, DEPTH, T, p[2]), p[3]))
            out_specs.append(pl.BlockSpec((spt, 1, seq_t, p[2]), lambda i: (i // tpb, l, i % tpb, 0)))
            if caches is not None:
                aliases[len(args)] = k
                in_specs.append(pl.BlockSpec(memory_space=pl.ANY))
                args.append(caches[p[7]])
        else:
            out_shape.append(jax.ShapeDtypeStruct((N, p[2]), p[3]))
            out_specs.append(pl.BlockSpec((TM, p[2]), lambda i: (i, 0)))
    return pl.pallas_call(
        functools.partial(_pre_kernel, plan=plan, rope=rope, n1=n1, nsub=nsub, n_alias=len(aliases), seq_t=seq_t),
        out_shape=out_shape,
        grid=(N // TM,),
        in_specs=in_specs,
        out_specs=out_specs,
        scratch_shapes=scratch,
        input_output_aliases=aliases,
        compiler_params=_cparams("parallel"),
        name="pre",
    )(*args)


RG_CW = 256
RG_TCH = 256
RG_SEGS = SUBLANES


def _rglru_kernel(xa_ref, ga_ref, cw_ref, cb_ref, wa_ref, wx_ref, ba_ref, bx_ref, lam_ref, h0_ref,
                  y_ref, hfin_ref, xpad, a_f, u_f, a_b, u_b, carry_f, carry_b, *, T):
    cw_ = RG_CW
    zeros8 = jnp.zeros((SUBLANES, cw_), F32)
    for k in range(cw_ // LANES):
        xpad[k, 0:SUBLANES, :] = zeros8[:, :LANES]
        xpad[k, T + SUBLANES:T + 2 * SUBLANES, :] = zeros8[:, :LANES]
        xpad[k, SUBLANES:T + SUBLANES, :] = xa_ref[0, :, k * LANES:(k + 1) * LANES]
    cw = cw_ref[...]
    cb = cb_ref[...]
    lam = lam_ref[...]
    sp = jnp.maximum(-lam, 0.0) + jnp.log1p(jnp.exp(-jnp.abs(lam)))
    sp4 = (0.5 * RGLRU_C) * sp
    nsp4_log2e = sp4 * (-LOG2E)
    ba = ba_ref[...]
    bx = bx_ref[...]
    h0 = h0_ref[0]

    def chunk(c, carry):
        base = pl.multiple_of(c * RG_TCH, RG_TCH)
        xc = cb
        for i in range(CONV_W):
            xc = xc + cw[i:i + 1] * get(xpad, pl.ds(base + (SUBLANES - 1 + i), RG_TCH, stride=1))
        xcb = xc.astype(BF16)
        half_xc = 0.5 * xc
        for d, (a_s, u_s) in enumerate(((a_f, u_f), (a_b, u_b))):
            r2 = jnp.tanh(_dot(xcb, wa_ref[d]) + ba[d:d + 1]) + 1.0
            i2 = jnp.tanh(_dot(xcb, wx_ref[d]) + bx[d:d + 1]) + 1.0
            a = jnp.exp2(r2 * nsp4_log2e[d:d + 1])
            v = jnp.tanh(r2 * sp4[d:d + 1]) * (1.0 + a * a)
            gain = jnp.where(v > 0.0, v * lax.rsqrt(v), 0.0)
            put(a_s, pl.ds(store_row(c), RG_TCH), a)
            put(u_s, pl.ds(store_row(c), RG_TCH), gain * (i2 * half_xc))
        return carry

    def get(ref, rows):
        return jnp.concatenate([ref[k, rows, :] for k in range(cw_ // LANES)], -1)

    def put(ref, rows, val):
        for k in range(cw_ // LANES):
            ref[k, rows, :] = val[:, k * LANES:(k + 1) * LANES]

    seg_len = T // RG_SEGS
    segmented = seg_len % RG_TCH == 0
    seg_pitch = seg_len + SUBLANES
    per_seg = seg_len // RG_TCH if segmented else 1

    def store_row(c):
        base = c * RG_TCH
        return pl.multiple_of(base + SUBLANES * (c // per_seg), SUBLANES) if segmented else pl.multiple_of(base, RG_TCH)

    lax.fori_loop(0, T // RG_TCH, chunk, 0)

    row = lax.broadcasted_iota(jnp.int32, (SUBLANES, cw_), 0)

    if segmented:
        def seg_step(j, carry):
            h_f, p_f, h_b, p_b = carry
            jb = seg_len - 1 - j
            rows_f = pl.ds(j, RG_SEGS, stride=seg_pitch)
            rows_b = pl.ds(jb, RG_SEGS, stride=seg_pitch)
            a = get(a_f, rows_f)
            h_f = a * h_f + get(u_f, rows_f)
            p_f = p_f * a
            put(u_f, rows_f, h_f)
            put(a_f, rows_f, p_f)
            a = get(a_b, rows_b)
            h_b = a * h_b + get(u_b, rows_b)
            p_b = p_b * a
            put(u_b, rows_b, h_b)
            put(a_b, rows_b, p_b)
            return h_f, p_f, h_b, p_b

        zero = jnp.zeros((RG_SEGS, cw_), F32)
        one = jnp.ones((RG_SEGS, cw_), F32)
        h_f, p_f, h_b, p_b = lax.fori_loop(0, seg_len, seg_step, (zero, one, zero, one), unroll=2)
        cf = [h0[0:1, :]]
        for s_ in range(RG_SEGS):
            cf.append(h_f[s_:s_ + 1, :] + p_f[s_:s_ + 1, :] * cf[s_])
        cb_in = [None] * RG_SEGS + [h0[1:2, :]]
        for s_ in range(RG_SEGS - 1, -1, -1):
            cb_in[s_] = h_b[s_:s_ + 1, :] + p_b[s_:s_ + 1, :] * cb_in[s_ + 1]
        hfin_ref[0] = jnp.where(row == 0, cf[RG_SEGS], jnp.where(row == 1, cb_in[0], 0.0))
        carry_f[...] = jnp.concatenate(cf[:RG_SEGS], 0)
        carry_b[...] = jnp.concatenate(cb_in[1:], 0)

        def emit_seg(c, carry):
            base = pl.multiple_of(c * RG_TCH, RG_TCH)
            s_ = c // per_seg
            rows = pl.ds(store_row(c), RG_TCH)
            h = (get(u_f, rows) + get(a_f, rows) * carry_f[pl.ds(s_, 1), :]
                 + get(u_b, rows) + get(a_b, rows) * carry_b[pl.ds(s_, 1), :])
            y_ref[0, pl.ds(base, RG_TCH), :] = (
                h * jax.nn.gelu(ga_ref[0, pl.ds(base, RG_TCH), :])).astype(y_ref.dtype)
            return carry

        lax.fori_loop(0, T // RG_TCH, emit_seg, 0)
        return

    def block_scan(a, u, reverse):
        for dd in (1, 2, 4):
            sh = SUBLANES - dd if reverse else dd
            a_n = pltpu.roll(a, sh, 0)
            u_n = pltpu.roll(u, sh, 0)
            ok = (row < SUBLANES - dd) if reverse else (row >= dd)
            u = jnp.where(ok, u + a * u_n, u)
            a = jnp.where(ok, a * a_n, a)
        return a, u

    nblk = T // SUBLANES

    def scan(i, carry):
        c_f, c_b = carry
        lo_f = pl.multiple_of(i * SUBLANES, SUBLANES)
        lo_b = pl.multiple_of((nblk - 1 - i) * SUBLANES, SUBLANES)
        a, u = block_scan(get(a_f, pl.ds(lo_f, SUBLANES)), get(u_f, pl.ds(lo_f, SUBLANES)), False)
        h_f = u + a * c_f
        put(u_f, pl.ds(lo_f, SUBLANES), h_f)
        a, u = block_scan(get(a_b, pl.ds(lo_b, SUBLANES)), get(u_b, pl.ds(lo_b, SUBLANES)), True)
        h_b = u + a * c_b
        put(u_b, pl.ds(lo_b, SUBLANES), h_b)
        return (jnp.broadcast_to(h_f[SUBLANES - 1:SUBLANES, :], (SUBLANES, cw_)),
                jnp.broadcast_to(h_b[0:1, :], (SUBLANES, cw_)))

    c_f, c_b = lax.fori_loop(
        0, nblk, scan,
        (jnp.broadcast_to(h0[0:1, :], (SUBLANES, cw_)), jnp.broadcast_to(h0[1:2, :], (SUBLANES, cw_))),
        unroll=2)
    hfin_ref[0] = jnp.where(row == 0, c_f, jnp.where(row == 1, c_b, 0.0))

    def emit(c, carry):
        base = pl.multiple_of(c * RG_TCH, RG_TCH)
        h = get(u_f, pl.ds(base, RG_TCH)) + get(u_b, pl.ds(base, RG_TCH))
        y_ref[0, pl.ds(base, RG_TCH), :] = (h * jax.nn.gelu(ga_ref[0, pl.ds(base, RG_TCH), :])).astype(y_ref.dtype)
        return carry

    lax.fori_loop(0, T // RG_TCH, emit, 0)


def _rglru_call(xa, ga, cw, cb, wa, wx, ba, bx, lam, h0):
    B, T, _ = xa.shape
    nj = RNN_WIDTH // RG_CW
    seq = pl.BlockSpec((1, T, RG_CW), lambda b, j: (b, 0, j))
    vec8 = pl.BlockSpec((SUBLANES, RG_CW), lambda b, j: (0, j))
    wsp = pl.BlockSpec((2, RG_CW, RG_CW), lambda b, j: (0, j, j))
    st = pl.BlockSpec((1, SUBLANES, RG_CW), lambda b, j: (b, 0, j))
    return pl.pallas_call(
        functools.partial(_rglru_kernel, T=T),
        out_shape=[jax.ShapeDtypeStruct((B, T, RNN_WIDTH), BF16),
                   jax.ShapeDtypeStruct((B, SUBLANES, RNN_WIDTH), F32)],
        grid=(B, nj),
        in_specs=[seq, seq, vec8, pl.BlockSpec((1, RG_CW), lambda b, j: (0, j)),
                  wsp, wsp, vec8, vec8, vec8, st],
        out_specs=[seq, st],
        scratch_shapes=[pltpu.VMEM((RG_CW // LANES, T + 2 * SUBLANES, LANES), F32)]
        + [pltpu.VMEM((RG_CW // LANES, T + RG_SEGS * SUBLANES, LANES), F32)] * 4
        + [pltpu.VMEM((RG_SEGS, RG_CW), F32)] * 2,
        compiler_params=_cparams("parallel", "parallel"),
        name="rglru",
    )(xa, ga, cw, cb, wa, wx, ba, bx, lam, h0)


def _attend_slabs(jobs):
    M = jobs[0][0].shape[0]
    lo = lax.broadcasted_iota(jnp.int32, (M, LANES), 1) < HEAD_DIM
    scores = []
    for q2, srcs, _ in jobs:
        zero = jnp.zeros_like(q2)
        qs = jnp.concatenate([jnp.where(lo, q2, zero), jnp.where(lo, zero, q2)], 0)
        ss = []
        for k, _, bias in srcs:
            s = _dot_nt(qs, k)
            ss.append(s if bias is None else s + bias)
        scores.append(ss)
    maxima = []
    for (_, _, sink_col), ss in zip(jobs, scores):
        m = jnp.max(ss[0], -1, keepdims=True)
        for s in ss[1:]:
            m = jnp.maximum(m, jnp.max(s, -1, keepdims=True))
        maxima.append(m if sink_col is None else jnp.maximum(m, sink_col))
    outs = []
    for (_, srcs, sink_col), ss, m in zip(jobs, scores, maxima):
        den = None
        o = None
        for s, (_, v, _) in zip(ss, srcs):
            e = jnp.exp2(s - m)
            d_ = jnp.sum(e, -1, keepdims=True)
            o_ = _dot(e.astype(BF16), v)
            den = d_ if den is None else den + d_
            o = o_ if o is None else o + o_
        if sink_col is not None:
            den = den + jnp.exp2(sink_col - m)
        o = o / den
        outs.append(jnp.where(lo, o[:M], o[M:]))
    return outs


def _sink_col(sink_ref, j, M):
    r = lax.broadcasted_iota(jnp.int32, (2 * M, 1), 0)
    return jnp.where(r < M, sink_ref[2 * j], sink_ref[2 * j + 1]) * LOG2E


def _slab(j):
    return slice(j * LANES, (j + 1) * LANES)


def _attn_ctx_kernel(sink_ref, nq_ref, nk_ref, nv_ref, sq_ref, skd_ref, svd_ref, yb_ref, yc_ref):
    M = nq_ref.shape[1]
    nslab = W512 // LANES
    jobs = []
    for j in range(nslab):
        k2 = nk_ref[0, 0, :, _slab(j)].astype(BF16)
        v2 = nv_ref[0, 0, :, _slab(j)].astype(BF16)
        jobs.append((nq_ref[0, :, _slab(j)], [(k2, v2, None)], None))
    for j in range(nslab):
        g = j // 2
        jobs.append((sq_ref[0, :, _slab(j)], [(skd_ref[0, :, _slab(g)], svd_ref[0, :, _slab(g)], None)],
                     _sink_col(sink_ref, j, M)))
    outs = _attend_slabs(jobs)
    for j in range(nslab):
        yb_ref[0, :, _slab(j)] = outs[j].astype(yb_ref.dtype)
        yc_ref[0, :, _slab(j)] = outs[nslab + j].astype(yc_ref.dtype)


def _attn_ctx_call(sink, nq, nk, nv, sq, skd, svd, l):
    B, T, _ = nq.shape
    s512 = pl.BlockSpec((1, T, W512), lambda b: (b, 0, 0))
    s256 = pl.BlockSpec((1, T, 2 * LANES), lambda b: (b, 0, 0))
    cache = pl.BlockSpec((1, 1, T, W512), lambda b: (b, l, 0, 0))
    return pl.pallas_call(
        _attn_ctx_kernel,
        out_shape=[jax.ShapeDtypeStruct((B, T, W512), BF16)] * 2,
        grid=(B,),
        in_specs=[pl.BlockSpec(memory_space=pltpu.SMEM), s512, cache, cache, s512, s256, s256],
        out_specs=[s512, s512],
        compiler_params=_cparams("parallel"),
        name="attn_ctx",
    )(sink, nq, nk, nv, sq, skd, svd)


NAT_NLOC = NAT_KR * GRID_W


def _nat_lat_kernel(q_ref, k_ref, v_ref, ck_ref, cv_ref, bias_ref, y_ref, *, rows):
    r = pl.program_id(1)
    rstart = jnp.clip(r - NAT_KR // 2, 0, rows - NAT_KR)
    d = r - rstart
    kbase = pl.multiple_of(rstart * GRID_W, GRID_W)
    jobs = []
    for j in range(W512 // LANES):
        k2 = k_ref[0, pl.ds(kbase, NAT_NLOC), _slab(j)]
        v2 = v_ref[0, pl.ds(kbase, NAT_NLOC), _slab(j)]
        jobs.append((q_ref[0, :, _slab(j)],
                     [(k2, v2, bias_ref[j, d]), (ck_ref[0, 0, :, _slab(j)], cv_ref[0, 0, :, _slab(j)], None)], None))
    for j, o in enumerate(_attend_slabs(jobs)):
        y_ref[0, :, _slab(j)] = o.astype(y_ref.dtype)


def _nat_lat_call(q, k, v, ck, cv, bias, l):
    B, T, _ = q.shape
    rows = T // GRID_W
    P = ck.shape[2]
    qs = pl.BlockSpec((1, GRID_W, W512), lambda b, r: (b, r, 0))
    full = pl.BlockSpec((1, T, W512), lambda b, r: (b, 0, 0))
    cs = pl.BlockSpec((1, 1, P, W512), lambda b, r: (b, l, 0, 0))
    return pl.pallas_call(
        functools.partial(_nat_lat_kernel, rows=rows),
        out_shape=jax.ShapeDtypeStruct((B, T, W512), BF16),
        grid=(B, rows),
        in_specs=[qs, full, full, cs, cs, _const_spec(bias.shape)],
        out_specs=qs,
        compiler_params=_cparams("parallel", "arbitrary"),
        name="nat_lat",
    )(q, k, v, ck, cv, bias)


def _swa_band_masks():
    i = np.arange(2 * SWA_BLOCK)[:, None] % SWA_BLOCK
    j = np.arange(SWA_SPAN)[None, :]
    tabs = []
    for span_start in (0, -SWA_WINDOW, -2 * SWA_WINDOW):
        dist = span_start + j - i
        tabs.append(np.where(np.abs(dist) <= SWA_WINDOW, 0.0, NEG_INF))
    return jnp.asarray(np.stack(tabs), F32)


def _swa_lat_kernel(sink_ref, q_ref, kd_ref, vd_ref, ckd_ref, cvd_ref, band_ref, y_ref, *, T):
    blk = pl.program_id(1)
    M = SWA_BLOCK
    start = blk * SWA_BLOCK
    ks = pl.multiple_of(jnp.clip(start - SWA_WINDOW, 0, T - SWA_SPAN), SWA_BLOCK)
    bias = band_ref[jnp.where(blk == 0, 0, jnp.where(blk == T // SWA_BLOCK - 1, 2, 1))]
    jobs = []
    for j in range(W512 // LANES):
        g = j // 2
        jobs.append((q_ref[0, :, _slab(j)],
                     [(kd_ref[0, pl.ds(ks, SWA_SPAN), _slab(g)], vd_ref[0, pl.ds(ks, SWA_SPAN), _slab(g)], bias),
                      (ckd_ref[0, 0, :, _slab(g)], cvd_ref[0, 0, :, _slab(g)], None)],
                     _sink_col(sink_ref, j, M)))
    for j, o in enumerate(_attend_slabs(jobs)):
        y_ref[0, :, _slab(j)] = o.astype(y_ref.dtype)


def _swa_lat_call(sink, q, kd, vd, ckd, cvd, l):
    B, T, _ = q.shape
    P = ckd.shape[2]
    qs = pl.BlockSpec((1, SWA_BLOCK, W512), lambda b, i: (b, i, 0))
    full = pl.BlockSpec((1, T, 2 * LANES), lambda b, i: (b, 0, 0))
    cs = pl.BlockSpec((1, 1, P, 2 * LANES), lambda b, i: (b, l, 0, 0))
    assert T // SWA_BLOCK >= 3
    band = _swa_band_masks()
    return pl.pallas_call(
        functools.partial(_swa_lat_kernel, T=T),
        out_shape=jax.ShapeDtypeStruct((B, T, W512), BF16),
        grid=(B, T // SWA_BLOCK),
        in_specs=[pl.BlockSpec(memory_space=pltpu.SMEM), qs, full, full, cs, cs, _const_spec(band.shape)],
        out_specs=qs,
        compiler_params=_cparams("parallel", "arbitrary"),
        name="swa_lat",
    )(sink, q, kd, vd, ckd, cvd, band)


FN_N2 = 256
FN_GP = 2


def _fourier_kernel(x_ref, cc_ref, m_ref, y_ref, *scratch, n1):
    gw = FN_GP * FNET_GC
    cc = cc_ref[...]
    for gp in range(FNET_GROUPS // FN_GP):
        for t1 in range(n1):
            ws = []
            for g in range(FN_GP):
                lo = t1 * FNET_WIDTH + (gp * FN_GP + g) * FNET_GC
                ws.append(_dot(x_ref[0, :, lo:lo + FNET_GC], cc))
            wr = jnp.concatenate([w[:, :FNET_GC] for w in ws], -1)
            wi = jnp.concatenate([w[:, FNET_GC:] for w in ws], -1)
            v = jnp.concatenate([wr, wi], 0).astype(BF16)
            b = _dot(m_ref[t1], v)
            if n1 == 1:
                y_ref[0, :, gp * gw:(gp + 1) * gw] = b.astype(y_ref.dtype)
            else:
                scratch[0][t1] = b
        if n1 > 1:
            _dft16_real(scratch[0], y_ref, gp * gw, gw)


FN_RCH = 16


def _dft16_real(bs, y_ref, col0, gw):
    tw = {m: (math.cos(2 * math.pi * m / 16), math.sin(2 * math.pi * m / 16)) for m in (1, 2, 3, 6, 9)}

    def body(rc, carry):
        r0 = pl.multiple_of(rc * FN_RCH, FN_RCH)
        re = [bs[t1, pl.ds(r0, FN_RCH), :] for t1 in range(16)]
        im = [bs[t1, pl.ds(FN_N2 + r0, FN_RCH), :] for t1 in range(16)]
        h_re = [[None] * 4 for _ in range(4)]
        h_im = [[None] * 4 for _ in range(4)]
        for b in range(4):
            s02r, s02i = re[b] + re[8 + b], im[b] + im[8 + b]
            d02r, d02i = re[b] - re[8 + b], im[b] - im[8 + b]
            s13r, s13i = re[4 + b] + re[12 + b], im[4 + b] + im[12 + b]
            d13r, d13i = re[4 + b] - re[12 + b], im[4 + b] - im[12 + b]
            g = [(s02r + s13r, s02i + s13i), (d02r + d13i, d02i - d13r),
                 (s02r - s13r, s02i - s13i), (d02r - d13i, d02i + d13r)]
            for c in range(4):
                gr, gi = g[c]
                m = b * c
                if m == 0:
                    hr, hi = gr, gi
                elif m == 4:
                    hr, hi = gi, -gr
                else:
                    cs, sn = tw[m]
                    hr = cs * gr + sn * gi
                    hi = (cs * gi - sn * gr) if b % 2 else None
                h_re[b][c], h_im[b][c] = hr, hi
        for c in range(4):
            p = h_re[0][c] + h_re[2][c]
            q = h_re[0][c] - h_re[2][c]
            r = h_re[1][c] + h_re[3][c]
            s = h_im[1][c] - h_im[3][c]
            for d, val in enumerate((p + r, q + s, p - r, q - s)):
                k1 = c + 4 * d
                y_ref[0, pl.ds(k1 * FN_N2 + r0, FN_RCH), col0:col0 + gw] = val.astype(y_ref.dtype)
        return carry

    lax.fori_loop(0, FN_N2 // FN_RCH, body, 0)


def _dft_consts(T):
    n1 = T // FN_N2
    j = np.arange(FNET_GC)
    ang = 2 * np.pi * np.outer(j, j) / FNET_GC
    sc = 1.0 / math.sqrt(FNET_GC)
    cc = np.concatenate([np.cos(ang) * sc, -np.sin(ang) * sc], 1)
    k2 = np.arange(FN_N2)[:, None]
    t2 = np.arange(FN_N2)[None, :]
    st = 1.0 / math.sqrt(T)
    mats = []
    for t1 in range(n1):
        th = 2 * np.pi * ((k2 * (n1 * t2 + t1)) % T) / T
        c, s = np.cos(th) * st, np.sin(th) * st
        top = np.concatenate([c, s], 1)
        mats.append(top if n1 == 1 else np.concatenate([top, np.concatenate([-s, c], 1)], 0))
    return n1, jnp.asarray(cc, BF16), jnp.asarray(np.stack(mats), BF16)


def _fourier_call(xv, T):
    B = xv.shape[0]
    n1, cc, mats = _dft_consts(T)
    assert n1 in (1, 16)
    gw = FN_GP * FNET_GC
    scratch = [pltpu.VMEM((n1, 2 * FN_N2, gw), F32)] if n1 > 1 else []
    return pl.pallas_call(
        functools.partial(_fourier_kernel, n1=n1),
        out_shape=jax.ShapeDtypeStruct((B, T, FNET_WIDTH), BF16),
        grid=(B,),
        in_specs=[pl.BlockSpec((1, FN_N2, n1 * FNET_WIDTH), lambda b: (b, 0, 0)),
                  _const_spec(cc.shape), _const_spec(mats.shape)],
        out_specs=pl.BlockSpec((1, T, FNET_WIDTH), lambda b: (b, 0, 0)),
        scratch_shapes=scratch,
        compiler_params=_cparams("parallel"),
        name="fourier",
    )(xv, cc, mats)


def _route(scores, biased):
    one, zero = jnp.float32(1.0), jnp.float32(0.0)
    in2, gscore = [], []
    for g in range(N_EXPERT_GROUPS):
        vs = [biased[g * EXPERTS_PER_GROUP + j:g * EXPERTS_PER_GROUP + j + 1] for j in range(EXPERTS_PER_GROUP)]
        gs = None
        for j in range(EXPERTS_PER_GROUP):
            rank = None
            for i in range(EXPERTS_PER_GROUP):
                if i == j:
                    continue
                beats = (vs[i] >= vs[j]) if i < j else (vs[i] > vs[j])
                t = jnp.where(beats, one, zero)
                rank = t if rank is None else rank + t
            keep = jnp.where(rank < 2.0, one, zero)
            in2.append(keep)
            t = keep * vs[j]
            gs = t if gs is None else gs + t
        gscore.append(gs)
    rows, sel = [], []
    for g in range(N_EXPERT_GROUPS):
        lost = None
        for i in range(N_EXPERT_GROUPS):
            if i == g:
                continue
            beats = (gscore[i] >= gscore[g]) if i < g else (gscore[i] > gscore[g])
            t = jnp.where(beats, one, zero)
            lost = t if lost is None else lost + t
        gsel = jnp.where(lost < 1.0, one, zero)
        for j in range(EXPERTS_PER_GROUP):
            e = g * EXPERTS_PER_GROUP + j
            sel.append(gsel * in2[e])
            rows.append(sel[-1] * scores[e:e + 1])
    tot = rows[0]
    for rr in rows[1:]:
        tot = tot + rr
    return jnp.concatenate(rows, 0) / tot, jnp.concatenate(sel, 0)


HP_ROWS = D_MODEL // 2 // LANES
Y_ROWS = D_MODEL // LANES


def _pack_halves(v):
    w = v.shape[1] // 2
    lo = lax.bitcast_convert_type(v[:, :w].astype(F32), jnp.uint32)
    hi = lax.bitcast_convert_type(v[:, w:].astype(F32), jnp.uint32)
    return (lo >> 16) | hi


def _unpack_halves(p):
    lo = lax.bitcast_convert_type(p << 16, F32).astype(BF16)
    hi = lax.bitcast_convert_type(p & jnp.uint32(0xFFFF0000), F32).astype(BF16)
    return lo, hi


def _merge_kernel(x_ref, mod_ref, ya_ref, yb_ref, yc_ref, yd_ref, wg_ref, wb_ref, wo_ref,
                  lng_ref, lnb_ref, wr_ref, rb_ref, x1_ref, hp_ref, rt_ref, *, nsub):
    m = mod_ref[0]
    sub = x_ref.shape[0] // nsub
    y_refs = (ya_ref, yb_ref, yc_ref, yd_ref)
    st = [dict() for _ in range(nsub)]

    def rows(k):
        return slice(k * sub, (k + 1) * sub)

    def norm_in(k):
        st[k]['x'] = x_ref[rows(k), :]
        st[k]['h'] = (_ln(st[k]['x']) * (1.0 + m[1:2]) + m[0:1]).astype(BF16)
        st[k]['merged'] = None

    def branch(k, b):
        gate = 0.5 * jnp.tanh(_dot(st[k]['h'], wg_ref[0, :, b * D_MODEL:(b + 1) * D_MODEL])) + 0.5
        term = gate * _dot(y_refs[b][rows(k), :], wb_ref[0, b])
        st[k]['merged'] = term if st[k]['merged'] is None else st[k]['merged'] + term

    def project(k):
        st[k]['out'] = _dot(st[k]['merged'].astype(BF16), wo_ref[0])

    def norm_out(k):
        x1 = _ln(ALPHA * st[k]['x'] + m[2:3] * st[k]['out']) * lng_ref[...] + lnb_ref[...]
        x1_ref[rows(k), :] = x1
        st[k]['h2'] = (_ln(x1) * (1.0 + m[4:5]) + m[3:4]).astype(BF16)

    def pack(k):
        packed = _pack_halves(st[k]['h2'])
        for c in range(HP_ROWS):
            hp_ref[pl.ds(k * sub * HP_ROWS + c, sub, stride=HP_ROWS), :] = packed[:, c * LANES:(c + 1) * LANES]

    def route(k):
        logits = _dot_nt(wr_ref[...], st[k]['h2'])
        e = jnp.exp(logits - jnp.max(logits, 0, keepdims=True))
        scores = e / jnp.sum(e, 0, keepdims=True)
        comb, sel = _route(scores, scores + rb_ref[...])
        rt_ref[:, rows(k)] = jnp.concatenate([comb, sel], 0)

    tail = (norm_out, pack, route)
    for k in range(nsub):
        norm_in(k)
    for k in range(nsub):
        for b in range(N_BRANCH):
            branch(k, b)
            if k > 0 and b < len(tail):
                tail[b](k - 1)
        project(k)
    for stage in tail:
        stage(nsub - 1)


def _merge_call(x, mod, ys, wg, wb, wo, l, lng, lnb, wr_t, rb, T, TM):
    N = x.shape[0]
    tok = lambda w: pl.BlockSpec((TM, w), lambda i: (i, 0))
    return pl.pallas_call(
        functools.partial(_merge_kernel, nsub=max(1, TM // SUB_TM)),
        out_shape=[jax.ShapeDtypeStruct((N, D_MODEL), F32), jax.ShapeDtypeStruct((N * HP_ROWS, LANES), jnp.uint32),
                   jax.ShapeDtypeStruct((2 * N_EXPERTS, N), F32)],
        grid=(N // TM,),
        in_specs=[tok(D_MODEL), pl.BlockSpec((1, SUBLANES, D_MODEL), lambda i: ((i * TM) // T, 0, 0)),
                  tok(W512), tok(W512), tok(W512), tok(W512),
                  _layer_spec(wg.shape, l), _layer_spec(wb.shape, l), _layer_spec(wo.shape, l),
                  _const_spec(lng.shape), _const_spec(lnb.shape), _const_spec(wr_t.shape), _const_spec(rb.shape)],
        out_specs=[tok(D_MODEL), pl.BlockSpec((TM * HP_ROWS, LANES), lambda i: (i, 0)),
                   pl.BlockSpec((2 * N_EXPERTS, TM), lambda i: (0, i))],
        compiler_params=_cparams("parallel"),
        name="merge",
    )(x, mod, *ys, wg, wb, wo, lng, lnb, wr_t, rb)


MOE_TM = 2048
MOE_RB = 256
MOE_KC = 4
MOE_CH = MOE_TM // MOE_KC
MOE_SLOTS = 2 * MOE_TM + N_EXPERTS * SUBLANES + MOE_RB
MOE_UNROLL = 8


def _route_kernel(rt_ref, u_ref, pos_ref, wts_ref, seg_ref):
    comb = rt_ref[0:N_EXPERTS, :]
    sel = rt_ref[N_EXPERTS:2 * N_EXPERTS, :]
    nb = MOE_TM // LANES
    stacked = jnp.concatenate([sel[:, b * LANES:(b + 1) * LANES] for b in range(nb)], 0)
    within = _dot(stacked.astype(BF16), u_ref[...])
    tot = jnp.sum(stacked, -1, keepdims=True)
    base = jnp.zeros((N_EXPERTS, 1), F32)
    bases = []
    for b in range(nb):
        bases.append(base)
        base = base + tot[b * N_EXPERTS:(b + 1) * N_EXPERTS]
    cnt = base
    padded = jnp.floor((cnt + (SUBLANES - 1.0)) * (1.0 / SUBLANES)) * SUBLANES
    rowi = lax.broadcasted_iota(jnp.int32, (N_EXPERTS, 1), 0)
    off = jnp.zeros((N_EXPERTS, 1), F32)
    for e in range(N_EXPERTS - 1):
        off = off + jnp.where(rowi > e, padded[e:e + 1, :], 0.0)
    pos_rows, wts_rows = [], []
    for b in range(nb):
        slot = within[b * N_EXPERTS:(b + 1) * N_EXPERTS] + (bases[b] + off)
        seen = jnp.zeros((1, LANES), F32)
        acc = [jnp.zeros((1, LANES), F32) for _ in range(4)]
        for e in range(N_EXPERTS):
            s_e = sel[e:e + 1, b * LANES:(b + 1) * LANES]
            c_e = comb[e:e + 1, b * LANES:(b + 1) * LANES]
            first = jnp.where(seen == 0.0, s_e, 0.0)
            second = jnp.where(seen == 1.0, s_e, 0.0)
            acc[0] = acc[0] + first * slot[e:e + 1]
            acc[1] = acc[1] + second * slot[e:e + 1]
            acc[2] = acc[2] + first * c_e
            acc[3] = acc[3] + second * c_e
            seen = seen + s_e
        pos_rows += acc[0:2]
        wts_rows += acc[2:4]
    pos_ref[...] = (jnp.concatenate(pos_rows, 0) * float(HP_ROWS)).astype(jnp.int32).reshape(2 * MOE_TM)
    wts_ref[...] = jnp.concatenate(wts_rows, 0).reshape(2 * MOE_TM)
    lane = lax.broadcasted_iota(jnp.int32, (N_EXPERTS, LANES), 1)
    diag = lane == lax.broadcasted_iota(jnp.int32, (N_EXPERTS, LANES), 0)
    off_row = jnp.sum(jnp.where(diag, off, 0.0), 0, keepdims=True)
    cnt_row = jnp.sum(jnp.where(diag, cnt, 0.0), 0, keepdims=True)
    seg_ref[0] = jnp.concatenate([off_row, cnt_row, jnp.zeros((SUBLANES - 2, LANES), F32)], 0).astype(jnp.int32)


def _route_call(rt):
    N = rt.shape[1]
    nt = N // MOE_TM
    u = jnp.asarray(np.triu(np.ones((LANES, LANES), np.float32), 1), BF16)
    return pl.pallas_call(
        _route_kernel,
        out_shape=[jax.ShapeDtypeStruct((nt * 2 * MOE_TM,), jnp.int32),
                   jax.ShapeDtypeStruct((nt * 2 * MOE_TM,), F32),
                   jax.ShapeDtypeStruct((nt, SUBLANES, LANES), jnp.int32)],
        grid=(nt,),
        in_specs=[pl.BlockSpec((2 * N_EXPERTS, MOE_TM), lambda i: (0, i)), _const_spec(u.shape)],
        out_specs=[pl.BlockSpec((2 * MOE_TM,), lambda i: (i,)),
                   pl.BlockSpec((2 * MOE_TM,), lambda i: (i,)),
                   pl.BlockSpec((1, SUBLANES, LANES), lambda i: (i, 0, 0))],
        compiler_params=_cparams("parallel"),
        name="route",
    )(rt, u)


def _table_index(it):
    per_block = LANES // MOE_UNROLL
    return (it // per_block) * (2 * LANES) + (it % per_block) * MOE_UNROLL


def _moe_kernel(pos_ref, wts_ref, seg_ref, hp_ref, w1_ref, w3_ref, w2_ref, x1_ref, mod_ref, lng_ref, lnb_ref,
                o_ref, xs, ys, oc):
    s = pl.program_id(1)
    half = D_MODEL // 2

    @pl.when(s == 0)
    def _dispatch():
        xs[...] = jnp.zeros_like(xs)

        def body(it, carry):
            t0 = pl.multiple_of(it * (MOE_UNROLL * HP_ROWS), MOE_UNROLL * HP_ROWS)
            i0 = _table_index(it)
            for j in range(MOE_UNROLL):
                row = hp_ref[pl.ds(t0 + HP_ROWS * j, HP_ROWS), :]
                xs[pl.ds(pl.multiple_of(pos_ref[i0 + j], HP_ROWS), HP_ROWS), :] = row
                xs[pl.ds(pl.multiple_of(pos_ref[i0 + LANES + j], HP_ROWS), HP_ROWS), :] = row
            return carry

        lax.fori_loop(0, MOE_TM // MOE_UNROLL, body, 0)

    @pl.when(s < N_EXPERTS)
    def _expert():
        off = seg_ref[0, 0, s]
        cnt = seg_ref[0, 1, s]
        nblk = (cnt + (MOE_RB // 2 - 1)) // MOE_RB

        def ffn_rows(r0, nrows):
            parts = [_unpack_halves(xs[pl.ds(pl.multiple_of(r0 * HP_ROWS, SUBLANES) + c, nrows, stride=HP_ROWS), :])
                     for c in range(HP_ROWS)]
            lo = jnp.concatenate([p[0] for p in parts], -1)
            hi = jnp.concatenate([p[1] for p in parts], -1)
            a = _dot(lo, w1_ref[0, 0, :half, :]) + _dot(hi, w1_ref[0, 0, half:, :])
            g = _dot(lo, w3_ref[0, 0, :half, :]) + _dot(hi, w3_ref[0, 0, half:, :])
            act = (a * jax.nn.sigmoid(a) * g).astype(BF16)
            y = _dot(act, w2_ref[0, 0])
            y0 = pl.multiple_of(r0 * Y_ROWS, SUBLANES)
            for c in range(Y_ROWS):
                ys[pl.ds(y0 + c, nrows, stride=Y_ROWS), :] = y[:, c * LANES:(c + 1) * LANES]

        def blk(i, carry):
            ffn_rows(pl.multiple_of(off + i * MOE_RB, SUBLANES), MOE_RB)
            return carry

        lax.fori_loop(0, nblk, blk, 0)

        @pl.when(cnt > nblk * MOE_RB)
        def _tail():
            ffn_rows(pl.multiple_of(off + nblk * MOE_RB, SUBLANES), MOE_RB // 2)

    @pl.when(s >= N_EXPERTS)
    def _combine():
        it0 = (s - N_EXPERTS) * (MOE_CH // MOE_UNROLL)

        def body(it, carry):
            t0 = pl.multiple_of(it * (MOE_UNROLL * Y_ROWS), MOE_UNROLL * Y_ROWS)
            i0 = _table_index(it0 + it)
            for j in range(MOE_UNROLL):
                p0 = pl.multiple_of(pos_ref[i0 + j] * (Y_ROWS // HP_ROWS), Y_ROWS)
                p1 = pl.multiple_of(pos_ref[i0 + LANES + j] * (Y_ROWS // HP_ROWS), Y_ROWS)
                oc[pl.ds(t0 + Y_ROWS * j, Y_ROWS), :] = (wts_ref[i0 + j] * ys[pl.ds(p0, Y_ROWS), :]
                                                         + wts_ref[i0 + LANES + j] * ys[pl.ds(p1, Y_ROWS), :])
            return carry

        lax.fori_loop(0, MOE_CH // MOE_UNROLL, body, 0)
        moe = jnp.concatenate([oc[pl.ds(c, MOE_CH, stride=Y_ROWS), :] for c in range(Y_ROWS)], -1)
        m = mod_ref[0]
        o_ref[...] = _ln(ALPHA * x1_ref[...] + m[5:6] * moe) * lng_ref[...] + lnb_ref[...]


def _moe_call(pos, wts, seg, hp, w1, w3, w2, l, x1, mod, lng, lnb, T):
    N = x1.shape[0]
    nt = N // MOE_TM
    flat = pl.BlockSpec((2 * MOE_TM,), lambda i, s: (i,), memory_space=pltpu.SMEM)
    wspec = lambda shp: pl.BlockSpec((1, 1) + shp, lambda i, s: (l, jnp.minimum(s, N_EXPERTS - 1), 0, 0))
    chunk = pl.BlockSpec((MOE_CH, D_MODEL),
                         lambda i, s: (i * MOE_KC + jnp.clip(s - N_EXPERTS, 0, MOE_KC - 1), 0))
    return pl.pallas_call(
        _moe_kernel,
        out_shape=jax.ShapeDtypeStruct((N, D_MODEL), F32),
        grid=(nt, N_EXPERTS + MOE_KC),
        in_specs=[flat, flat,
                  pl.BlockSpec((1, SUBLANES, LANES), lambda i, s: (i, 0, 0), memory_space=pltpu.SMEM),
                  pl.BlockSpec((MOE_TM * HP_ROWS, LANES), lambda i, s: (i, 0)),
                  wspec((D_MODEL, D_EXPERT)), wspec((D_MODEL, D_EXPERT)), wspec((D_EXPERT, D_MODEL)),
                  chunk, pl.BlockSpec((1, SUBLANES, D_MODEL), lambda i, s: ((i * MOE_TM) // T, 0, 0)),
                  _const_spec(lng.shape), _const_spec(lnb.shape)],
        out_specs=chunk,
        scratch_shapes=[pltpu.VMEM((MOE_SLOTS * HP_ROWS, LANES), jnp.uint32),
                        pltpu.VMEM((MOE_SLOTS * Y_ROWS, LANES), F32),
                        pltpu.VMEM((MOE_CH * Y_ROWS, LANES), F32)],
        compiler_params=_cparams("parallel", "arbitrary"),
        name="moe",
    )(pos, wts, seg, hp, w1, w3, w2, x1, mod, lng, lnb)


def _rope_tables(T):
    t = jnp.arange(T)
    nf = HEAD_DIM // 4
    inv = ROPE_BASE ** (-jnp.arange(nf, dtype=F32) / nf)
    ar = (t // GRID_W).astype(F32)[:, None] * inv
    ac = (t % GRID_W).astype(F32)[:, None] * inv
    cos = jnp.concatenate([jnp.cos(ar), jnp.cos(ar), jnp.cos(ac), jnp.cos(ac)], -1)
    sin = jnp.concatenate([-jnp.sin(ar), jnp.sin(ar), -jnp.sin(ac), jnp.sin(ac)], -1)
    return jnp.tile(cos, (1, LANES // HEAD_DIM)), jnp.tile(sin, (1, LANES // HEAD_DIM))


def _nat_bias_table(rpb):
    q = np.arange(GRID_W)
    kc = np.arange(GRID_W)
    cstart = np.clip(q - NAT_KC // 2, 0, GRID_W - NAT_KC)
    ok = (kc[None, :] >= cstart[:, None]) & (kc[None, :] < cstart[:, None] + NAT_KC)
    cidx = np.clip(kc[None, :] - q[:, None] + NAT_KC - 1, 0, 2 * NAT_KC - 2)
    onehot = (np.arange(2 * NAT_KC - 1)[:, None, None] == cidx[None]).astype(np.float32)
    t = jnp.einsum('hrc,cqk->hqrk', rpb.astype(F32), jnp.asarray(onehot), precision=lax.Precision.HIGHEST)
    t = jnp.where(jnp.asarray(ok)[None, :, None, :], t * LOG2E, NEG_INF)
    per_d = [t[:, :, NAT_KR - 1 - d:2 * NAT_KR - 1 - d, :].reshape(NAT_HEADS // 2, 2 * GRID_W, NAT_KR * GRID_W)
             for d in range(NAT_KR)]
    return jnp.stack(per_d, 1)


def _dup_heads(a):
    a = jnp.broadcast_to(a[..., :, None, :], a.shape[:-1] + (2, a.shape[-1]))
    return a.reshape(a.shape[:-3] + (-1,))


def _block_diag(w):
    eye = jnp.eye(RNN_BLOCKS, dtype=w.dtype)
    return jnp.einsum('dnio,nm->dnimo', w, eye).reshape(2, RNN_WIDTH, RNN_WIDTH)


def _pad_rows(a, rows=SUBLANES):
    return jnp.pad(a, ((0, rows - a.shape[0]),) + ((0, 0),) * (a.ndim - 1))


def kernel(x_prompt, x_sample, c, cache_nat_k, cache_nat_v, cache_swa_k, cache_swa_v, state_rglru, c_ctx,
           w_ada, b_ada, w_in, rg_conv_w, rg_conv_b, rg_wa, rg_ba, rg_wx, rg_bx, rg_lambda, nat_rpb,
           swa_sink, w_branch, w_out, ln_g, ln_b, w_router, router_bias, w1, w3, w2):
    B_c, T_c, _ = x_prompt.shape
    B_l, T_l, _ = x_sample.shape
    P = cache_nat_k.shape[2]

    cv = jnp.concatenate([c_ctx[None], c, jnp.zeros((ADA_ROWS - 1 - B_l, D_MODEL), F32)], 0)
    mod_all = _ada_call(cv, w_ada, b_ada).reshape(DEPTH, ADA_ROWS, 6, D_MODEL)
    mod_all = jnp.pad(mod_all, ((0, 0), (0, 0), (0, SUBLANES - 6), (0, 0)))

    kv_w = SWA_KV_HEADS * HEAD_DIM
    dup_w = 2 * kv_w
    xa0, ga0, nq0, nk0, nv0, sq0, sk0 = (i * W512 for i in range(7))
    sv0 = sk0 + kv_w
    xf0 = sv0 + kv_w
    assert xf0 + W512 == GATE_OFF
    w_in_b = w_in.astype(BF16)
    head = lambda c0, g: w_in_b[:, :, c0 + g * HEAD_DIM:c0 + (g + 1) * HEAD_DIM]
    w_tail_ctx = jnp.concatenate([head(c0, g) for c0 in (sk0, sv0) for g in (0, 0, 1, 1)], -1)
    w_tail_lat = jnp.concatenate([w_in_b[:, :, xf0:GATE_OFF], w_tail_ctx], -1)
    w_gate = (0.5 * w_in[:, :, GATE_OFF:]).astype(BF16)
    w_branch_b = w_branch.astype(BF16)
    w_out_b = w_out.astype(BF16)
    w1_b, w3_b, w2_b = w1.astype(BF16), w3.astype(BF16), w2.astype(BF16)
    wr_t = w_router.T.astype(BF16)
    wa_bd = jnp.stack([_block_diag(0.5 * rg_wa[l]) for l in range(DEPTH)]).astype(BF16)
    wx_bd = jnp.stack([_block_diag(0.5 * rg_wx[l]) for l in range(DEPTH)]).astype(BF16)

    def out(src, off, width, dtype, scale=1.0, rope=False, fourier=False, cache=None):
        return (src, off, width, dtype, scale, rope, fourier, cache)

    plan_ctx = (out(0, xa0, W512, F32), out(0, ga0, W512, F32), out(0, nq0, W512, BF16, Q_SCALE),
                out(0, nk0, W512, F32, cache=0), out(0, nv0, W512, F32, cache=1), out(0, sq0, W512, BF16, Q_SCALE),
                out(0, sk0, kv_w, F32, cache=2), out(0, sv0, kv_w, F32, cache=3),
                out(0, xf0, W512, BF16, fourier=True),
                out(1, 0, dup_w, BF16), out(1, dup_w, dup_w, BF16))
    plan_lat = (out(0, xa0, W512, F32), out(0, ga0, W512, F32), out(0, nq0, W512, BF16, Q_SCALE),
                out(0, nk0, W512, BF16), out(0, nv0, W512, BF16), out(0, sq0, W512, BF16, Q_SCALE, rope=True),
                out(1, 0, W512, BF16, fourier=True),
                out(1, W512, dup_w, BF16, rope=True), out(1, W512 + dup_w, dup_w, BF16))

    rope_tabs = _rope_tables(T_l)
    ck_nat = cache_nat_k.reshape(B_l, DEPTH, P, W512).astype(BF16)
    cv_nat = cache_nat_v.reshape(B_l, DEPTH, P, W512).astype(BF16)
    ckd_swa = _dup_heads(cache_swa_k).astype(BF16)
    cvd_swa = _dup_heads(cache_swa_v).astype(BF16)
    state8 = jnp.pad(state_rglru, ((0, 0), (0, 0), (0, SUBLANES - 2), (0, 0)))
    zero_state = jnp.zeros((B_c, SUBLANES, RNN_WIDTH), F32)

    def layer(x, l, ctx_pass, caches=None):
        B, T = (B_c, T_c) if ctx_pass else (B_l, T_l)
        whole = (B * T) % TOKEN_TM == 0 and (T % TOKEN_TM == 0 or TOKEN_TM % T == 0)
        TM = TOKEN_TM if whole else min(T, TOKEN_TM)
        sub_tm = TM // max(1, TM // SUB_TM)
        modb = jnp.broadcast_to(mod_all[l, 0:1], (B, SUBLANES, D_MODEL)) if ctx_pass else mod_all[l, 1:1 + B_l]
        if ctx_pass:
            xa, ga, nq, nk, nv, sq, sk, sv, xf, skd, svd = _pre_call(
                x, modb, w_in_b, GATE_OFF, w_tail_ctx, l, plan_ctx, B, T, TM, None, caches)
        else:
            xa, ga, nq, nk, nv, sq, xf, skd, svd = _pre_call(
                x, modb, w_in_b, sk0, w_tail_lat, l, plan_lat, B, T, TM, rope_tabs)
        r3 = lambda a: a.reshape(B, T, a.shape[-1])
        h0 = zero_state if ctx_pass else state8[:, l]
        ya, hfin = _rglru_call(
            r3(xa), r3(ga), _pad_rows(rg_conv_w[l]), rg_conv_b[l][None], wa_bd[l], wx_bd[l],
            _pad_rows(0.5 * rg_ba[l]), _pad_rows(0.5 * rg_bx[l]), _pad_rows(rg_lambda[l]), h0)
        if ctx_pass:
            yb, yc = _attn_ctx_call(swa_sink[l], r3(nq), nk, nv, r3(sq), r3(skd), r3(svd), l)
        else:
            yb = _nat_lat_call(r3(nq), r3(nk), r3(nv), ck_nat, cv_nat, _nat_bias_table(nat_rpb[l]), l)
            yc = _swa_lat_call(swa_sink[l], r3(sq), r3(skd), r3(svd), ckd_swa, cvd_swa, l)
        yd = _fourier_call(xf, T)
        f2 = lambda a: a.reshape(B * T, a.shape[-1])
        x1, hp, rt = _merge_call(
            x, modb, (f2(ya), f2(yb), f2(yc), f2(yd)), w_gate, w_branch_b, w_out_b, l,
            ln_g[l, 0][None], ln_b[l, 0][None], wr_t,
            jnp.broadcast_to(router_bias[:, None], (N_EXPERTS, sub_tm)), T, TM)
        pos, wts, seg = _route_call(rt)
        x2 = _moe_call(pos, wts, seg, hp, w1_b, w3_b, w2_b, l, x1, modb,
                       ln_g[l, 1][None], ln_b[l, 1][None], T)
        new = ((nk, nv, sk, sv), hfin[:, :2]) if ctx_pass else None
        return x2, new

    y = x_prompt.reshape(B_c * T_c, D_MODEL)
    caches = tuple(jnp.zeros((B_c, DEPTH, T_c, w), F32)
                   for w in (W512, W512, SWA_KV_HEADS * HEAD_DIM, SWA_KV_HEADS * HEAD_DIM))
    states = []
    for l in range(DEPTH):
        y, (caches, st) = layer(y, l, True, caches)
        states.append(st)
    y_prompt = y.reshape(B_c, T_c, D_MODEL)
    new_nat_k = caches[0].reshape(B_c, DEPTH, T_c, NAT_HEADS, HEAD_DIM)
    new_nat_v = caches[1].reshape(B_c, DEPTH, T_c, NAT_HEADS, HEAD_DIM)
    new_swa_k = caches[2].reshape(B_c, DEPTH, T_c, SWA_KV_HEADS, HEAD_DIM)
    new_swa_v = caches[3].reshape(B_c, DEPTH, T_c, SWA_KV_HEADS, HEAD_DIM)
    new_state = jnp.stack(states, 1)

    y = x_sample.reshape(B_l * T_l, D_MODEL)
    for l in range(DEPTH):
        y, _ = layer(y, l, False)
    y_sample = y.reshape(B_l, T_l, D_MODEL)
    return (y_prompt, y_sample, new_nat_k, new_nat_v, new_swa_k, new_swa_v, new_state)
```

```python
import functools
import math

import jax
import jax.numpy as jnp
import numpy as np
from jax import lax
from jax.experimental import pallas as pl
from jax.experimental.pallas import tpu as pltpu

F32 = jnp.float32
BF16 = jnp.bfloat16

D_MODEL = 1024
DEPTH = 4
GRID_W = 64
HEAD_DIM = 64
ATTN_SCALE = HEAD_DIM ** -0.5
LOG2E = math.log2(math.e)
Q_SCALE = ATTN_SCALE * LOG2E
NEG_INF = -1e30
LN_EPS = 1e-5
ALPHA = (2 * DEPTH) ** 0.25
ROPE_BASE = 10000.0
RNN_WIDTH = 512
RNN_BLOCKS = 8
CONV_W = 4
RGLRU_C = 8.0
NAT_HEADS = 8
NAT_KR = 8
NAT_KC = 16
SWA_HEADS = 8
SWA_KV_HEADS = 2
SWA_WINDOW = 128
SWA_BLOCK = 128
SWA_SPAN = SWA_BLOCK + 2 * SWA_WINDOW
FNET_GROUPS = 4
FNET_WIDTH = 512
FNET_GC = FNET_WIDTH // FNET_GROUPS
N_BRANCH = 4
N_EXPERTS = 16
N_EXPERT_GROUPS = 4
EXPERTS_PER_GROUP = N_EXPERTS // N_EXPERT_GROUPS
D_EXPERT = 512
W512 = 512
GATE_OFF = 3840

LANES = 128
SUBLANES = 8
VMEM_LIMIT = 56 * 1024 * 1024
TOKEN_TM = 1024
SUB_TM = 512


def _cparams(*sem):
    return pltpu.CompilerParams(dimension_semantics=sem, vmem_limit_bytes=VMEM_LIMIT)


def _const_spec(shape):
    nd = len(shape)
    return pl.BlockSpec(shape, lambda *_: (0,) * nd, pipeline_mode=pl.Buffered(1))


def _layer_spec(shape, l):
    nd = len(shape)
    return pl.BlockSpec((1,) + tuple(shape[1:]), lambda *_: (l,) + (0,) * (nd - 1), pipeline_mode=pl.Buffered(1))


def _mod_spec(l, per_seq, tm, t):
    return pl.BlockSpec((1, 1, SUBLANES, D_MODEL),
                        lambda i, *_: (l, 1 + (i * tm) // t if per_seq else 0, 0, 0))


def _ln(x):
    mu = jnp.mean(x, -1, keepdims=True)
    xc = x - mu
    var = jnp.mean(xc * xc, -1, keepdims=True)
    return xc * lax.rsqrt(var + LN_EPS)


def _dot(a, b):
    return jnp.dot(a, b, preferred_element_type=F32)


def _dot_nt(a, b):
    return lax.dot_general(a, b, (((1,), (1,)), ((), ())), preferred_element_type=F32)


ADA_ROWS = 2 * SUBLANES
ADA_TN = D_MODEL


def _ada_kernel(c_ref, w_ref, b_ref, o_ref):
    cv = c_ref[...]
    s = (cv * jax.nn.sigmoid(cv)).astype(BF16)
    o_ref[0] = _dot(s, w_ref[0].astype(BF16)) + b_ref[0]


def _ada_call(cv, w_ada, b_ada):
    n = w_ada.shape[-1]
    return pl.pallas_call(
        _ada_kernel,
        out_shape=jax.ShapeDtypeStruct((DEPTH, ADA_ROWS, n), F32),
        grid=(DEPTH, n // ADA_TN),
        in_specs=[
            pl.BlockSpec((ADA_ROWS, D_MODEL), lambda l, j: (0, 0)),
            pl.BlockSpec((1, D_MODEL, ADA_TN), lambda l, j: (l, 0, j)),
            pl.BlockSpec((1, 1, ADA_TN), lambda l, j: (l, 0, j)),
        ],
        out_specs=pl.BlockSpec((1, ADA_ROWS, ADA_TN), lambda l, j: (l, 0, j)),
        compiler_params=_cparams("parallel", "parallel"),
        name="ada",
    )(cv, w_ada, b_ada.reshape(DEPTH, 1, n))


def _rope(u, cos, sin):
    lane = lax.broadcasted_iota(jnp.int32, cos.shape, 1)
    first = (lane & 31) < 16
    outs = []
    for j in range(u.shape[1] // LANES):
        s = u[:, j * LANES:(j + 1) * LANES]
        partner = jnp.where(first, pltpu.roll(s, LANES - 16, 1), pltpu.roll(s, 16, 1))
        outs.append(s * cos + partner * sin)
    return outs[0] if len(outs) == 1 else jnp.concatenate(outs, -1)


def _pre_kernel(*refs, plan, rope, n1, nsub, n_alias, seq_t):
    x_ref, mod_ref, wm_ref, wt_ref = refs[:4]
    refs = refs[4:]
    if rope:
        cos_ref, sin_ref = refs[:2]
        refs = refs[2:]
    refs = refs[n_alias:]
    outs = refs[:len(plan)]
    m = mod_ref[0, 0]
    sub = x_ref.shape[0] // nsub
    hs = [(_ln(x_ref[k * sub:(k + 1) * sub, :]) * (1.0 + m[1:2]) + m[0:1]).astype(BF16) for k in range(nsub)]
    for (src, off, width, _, scale, do_rope, fourier, slot), o_ref in zip(plan, outs):
        w_ref = wt_ref if src else wm_ref
        for k in range(nsub):
            rows = slice(k * sub, (k + 1) * sub)
            u = _dot(hs[k], w_ref[0, :, off:off + width])
            if do_rope:
                u = _rope(u, cos_ref[rows, :], sin_ref[rows, :])
            if scale != 1.0:
                u = u * scale
            if fourier and n1 > 1:
                stage = refs[len(plan)]
                for c in range(width // LANES):
                    stage[k, c] = u[:, c * LANES:(c + 1) * LANES]
                r = sub // n1
                for t1 in range(n1):
                    for c in range(width // LANES):
                        lo = t1 * width + c * LANES
                        o_ref[0, k * r:(k + 1) * r, lo:lo + LANES] = (
                            stage[k, c, pl.ds(t1, r, stride=n1), :].astype(o_ref.dtype))
            elif fourier or slot is not None:
                if sub <= seq_t:
                    pos = slice((k * sub) % seq_t, (k * sub) % seq_t + sub)
                    pieces = [((k * sub) // seq_t, pos, u)]
                else:
                    pieces = [(k * (sub // seq_t) + j, slice(None), u[j * seq_t:(j + 1) * seq_t])
                              for j in range(sub // seq_t)]
                for b_, pos, val in pieces:
                    if fourier:
                        o_ref[b_, pos, :] = val.astype(o_ref.dtype)
                    else:
                        o_ref[b_, 0, pos, :] = val.astype(o_ref.dtype)
            else:
                o_ref[rows, :] = u.astype(o_ref.dtype)


def _pre_call(x, mod, per_seq, w_main, n_main, w_tail, l, plan, B, T, TM, rope_tabs, caches=None):
    N = x.shape[0]
    tpb = max(1, T // TM)
    spt = max(1, TM // T)
    seq_t = min(T, TM)
    nsub = max(1, TM // SUB_TM)
    n1 = T // FN_N2
    rope = rope_tabs is not None
    n_tail = w_tail.shape[-1]
    in_specs = [
        pl.BlockSpec((TM, D_MODEL), lambda i: (i, 0)),
        _mod_spec(l, per_seq, TM, T),
        pl.BlockSpec((1, D_MODEL, n_main), lambda i: (l, 0, 0), pipeline_mode=pl.Buffered(1)),
        pl.BlockSpec((1, D_MODEL, n_tail), lambda i: (l, 0, 0), pipeline_mode=pl.Buffered(1)),
    ]
    args = [x, mod, w_main, w_tail]
    if rope:
        in_specs += [pl.BlockSpec((TM, LANES), lambda i: (i % tpb, 0))] * 2
        args += list(rope_tabs)
    out_shape, out_specs, scratch, aliases = [], [], [], {}
    for k, p in enumerate(plan):
        if p[6]:
            out_shape.append(jax.ShapeDtypeStruct((B, FN_N2, n1 * p[2]), p[3]))
            if n1 > 1:
                out_specs.append(pl.BlockSpec((1, TM // n1, n1 * p[2]), lambda i: (i // tpb, i % tpb, 0)))
                scratch.append(pltpu.VMEM((nsub, p[2] // LANES, TM // nsub, LANES), F32))
            else:
                out_specs.append(pl.BlockSpec((spt, seq_t, p[2]), lambda i: (i // tpb, i % tpb, 0)))
        elif p[7] is not None:
            out_shape.append(jax.ShapeDtypeStruct((B, DEPTH, T, p[2]), p[3]))
            out_specs.append(pl.BlockSpec((spt, 1, seq_t, p[2]), lambda i: (i // tpb, l, i % tpb, 0)))
            if caches is not None:
                aliases[len(args)] = k
                in_specs.append(pl.BlockSpec(memory_space=pl.ANY))
                args.append(caches[p[7]])
        else:
            out_shape.append(jax.ShapeDtypeStruct((N, p[2]), p[3]))
            out_specs.append(pl.BlockSpec((TM, p[2]), lambda i: (i, 0)))
    return pl.pallas_call(
        functools.partial(_pre_kernel, plan=plan, rope=rope, n1=n1, nsub=nsub, n_alias=len(aliases), seq_t=seq_t),
        out_shape=out_shape,
        grid=(N // TM,),
        in_specs=in_specs,
        out_specs=out_specs,
        scratch_shapes=scratch,
        input_output_aliases=aliases,
        compiler_params=_cparams("parallel"),
        name="pre",
    )(*args)


RG_CW = 256
RG_TCH = 256
RG_SEGS = SUBLANES


def _rglru_kernel(xa_ref, ga_ref, cw_ref, cb_ref, wa_ref, wx_ref, ba_ref, bx_ref, lam_ref, h0_ref,
                  y_ref, hfin_ref, xpad, a_f, u_f, a_b, u_b, carry_f, carry_b, *, T):
    cw_ = RG_CW
    zeros8 = jnp.zeros((SUBLANES, cw_), F32)
    for k in range(cw_ // LANES):
        xpad[k, 0:SUBLANES, :] = zeros8[:, :LANES]
        xpad[k, T + SUBLANES:T + 2 * SUBLANES, :] = zeros8[:, :LANES]
        xpad[k, SUBLANES:T + SUBLANES, :] = xa_ref[0, :, k * LANES:(k + 1) * LANES]
    cw = cw_ref[0]
    cb = cb_ref[0]
    lam = lam_ref[0]
    sp = jnp.maximum(-lam, 0.0) + jnp.log1p(jnp.exp(-jnp.abs(lam)))
    sp4 = (0.5 * RGLRU_C) * sp
    nsp4_log2e = sp4 * (-LOG2E)
    ba = ba_ref[0]
    bx = bx_ref[0]
    h0 = h0_ref[0, 0]

    def chunk(c, carry):
        base = pl.multiple_of(c * RG_TCH, RG_TCH)
        xc = cb
        for i in range(CONV_W):
            xc = xc + cw[i:i + 1] * get(xpad, pl.ds(base + (SUBLANES - 1 + i), RG_TCH, stride=1))
        xcb = xc.astype(BF16)
        half_xc = 0.5 * xc
        for d, (a_s, u_s) in enumerate(((a_f, u_f), (a_b, u_b))):
            r2 = jnp.tanh(_dot(xcb, wa_ref[0, d]) + ba[d:d + 1]) + 1.0
            i2 = jnp.tanh(_dot(xcb, wx_ref[0, d]) + bx[d:d + 1]) + 1.0
            a = jnp.exp2(r2 * nsp4_log2e[d:d + 1])
            v = jnp.tanh(r2 * sp4[d:d + 1]) * (1.0 + a * a)
            gain = jnp.where(v > 0.0, v * lax.rsqrt(v), 0.0)
            put(a_s, pl.ds(store_row(c), RG_TCH), a)
            put(u_s, pl.ds(store_row(c), RG_TCH), gain * (i2 * half_xc))
        return carry

    def get(ref, rows):
        return jnp.concatenate([ref[k, rows, :] for k in range(cw_ // LANES)], -1)

    def put(ref, rows, val):
        for k in range(cw_ // LANES):
            ref[k, rows, :] = val[:, k * LANES:(k + 1) * LANES]

    seg_len = T // RG_SEGS
    segmented = seg_len % RG_TCH == 0
    seg_pitch = seg_len + SUBLANES
    per_seg = seg_len // RG_TCH if segmented else 1

    def store_row(c):
        base = c * RG_TCH
        return pl.multiple_of(base + SUBLANES * (c // per_seg), SUBLANES) if segmented else pl.multiple_of(base, RG_TCH)

    lax.fori_loop(0, T // RG_TCH, chunk, 0)

    row = lax.broadcasted_iota(jnp.int32, (SUBLANES, cw_), 0)

    if segmented:
        def seg_step(j, carry):
            h_f, p_f, h_b, p_b = carry
            jb = seg_len - 1 - j
            rows_f = pl.ds(j, RG_SEGS, stride=seg_pitch)
            rows_b = pl.ds(jb, RG_SEGS, stride=seg_pitch)
            a = get(a_f, rows_f)
            h_f = a * h_f + get(u_f, rows_f)
            p_f = p_f * a
            put(u_f, rows_f, h_f)
            put(a_f, rows_f, p_f)
            a = get(a_b, rows_b)
            h_b = a * h_b + get(u_b, rows_b)
            p_b = p_b * a
            put(u_b, rows_b, h_b)
            put(a_b, rows_b, p_b)
            return h_f, p_f, h_b, p_b

        zero = jnp.zeros((RG_SEGS, cw_), F32)
        one = jnp.ones((RG_SEGS, cw_), F32)
        h_f, p_f, h_b, p_b = lax.fori_loop(0, seg_len, seg_step, (zero, one, zero, one), unroll=2)
        cf = [h0[0:1, :]]
        for s_ in range(RG_SEGS):
            cf.append(h_f[s_:s_ + 1, :] + p_f[s_:s_ + 1, :] * cf[s_])
        cb_in = [None] * RG_SEGS + [h0[1:2, :]]
        for s_ in range(RG_SEGS - 1, -1, -1):
            cb_in[s_] = h_b[s_:s_ + 1, :] + p_b[s_:s_ + 1, :] * cb_in[s_ + 1]
        hfin_ref[0] = jnp.where(row == 0, cf[RG_SEGS], jnp.where(row == 1, cb_in[0], 0.0))
        carry_f[...] = jnp.concatenate(cf[:RG_SEGS], 0)
        carry_b[...] = jnp.concatenate(cb_in[1:], 0)

        def emit_seg(c, carry):
            base = pl.multiple_of(c * RG_TCH, RG_TCH)
            s_ = c // per_seg
            rows = pl.ds(store_row(c), RG_TCH)
            h = (get(u_f, rows) + get(a_f, rows) * carry_f[pl.ds(s_, 1), :]
                 + get(u_b, rows) + get(a_b, rows) * carry_b[pl.ds(s_, 1), :])
            y_ref[0, pl.ds(base, RG_TCH), :] = (
                h * jax.nn.gelu(ga_ref[0, pl.ds(base, RG_TCH), :])).astype(y_ref.dtype)
            return carry

        lax.fori_loop(0, T // RG_TCH, emit_seg, 0)
        return

    def block_scan(a, u, reverse):
        for dd in (1, 2, 4):
            sh = SUBLANES - dd if reverse else dd
            a_n = pltpu.roll(a, sh, 0)
            u_n = pltpu.roll(u, sh, 0)
            ok = (row < SUBLANES - dd) if reverse else (row >= dd)
            u = jnp.where(ok, u + a * u_n, u)
            a = jnp.where(ok, a * a_n, a)
        return a, u

    nblk = T // SUBLANES

    def scan(i, carry):
        c_f, c_b = carry
        lo_f = pl.multiple_of(i * SUBLANES, SUBLANES)
        lo_b = pl.multiple_of((nblk - 1 - i) * SUBLANES, SUBLANES)
        a, u = block_scan(get(a_f, pl.ds(lo_f, SUBLANES)), get(u_f, pl.ds(lo_f, SUBLANES)), False)
        h_f = u + a * c_f
        put(u_f, pl.ds(lo_f, SUBLANES), h_f)
        a, u = block_scan(get(a_b, pl.ds(lo_b, SUBLANES)), get(u_b, pl.ds(lo_b, SUBLANES)), True)
        h_b = u + a * c_b
        put(u_b, pl.ds(lo_b, SUBLANES), h_b)
        return (jnp.broadcast_to(h_f[SUBLANES - 1:SUBLANES, :], (SUBLANES, cw_)),
                jnp.broadcast_to(h_b[0:1, :], (SUBLANES, cw_)))

    c_f, c_b = lax.fori_loop(
        0, nblk, scan,
        (jnp.broadcast_to(h0[0:1, :], (SUBLANES, cw_)), jnp.broadcast_to(h0[1:2, :], (SUBLANES, cw_))),
        unroll=2)
    hfin_ref[0] = jnp.where(row == 0, c_f, jnp.where(row == 1, c_b, 0.0))

    def emit(c, carry):
        base = pl.multiple_of(c * RG_TCH, RG_TCH)
        h = get(u_f, pl.ds(base, RG_TCH)) + get(u_b, pl.ds(base, RG_TCH))
        y_ref[0, pl.ds(base, RG_TCH), :] = (h * jax.nn.gelu(ga_ref[0, pl.ds(base, RG_TCH), :])).astype(y_ref.dtype)
        return carry

    lax.fori_loop(0, T // RG_TCH, emit, 0)


def _rglru_call(xa, ga, cw, cb, wa, wx, ba, bx, lam, h0, l, l_state):
    B, T, _ = xa.shape
    nj = RNN_WIDTH // RG_CW
    seq = pl.BlockSpec((1, T, RG_CW), lambda b, j: (b, 0, j))
    vec8 = pl.BlockSpec((1, SUBLANES, RG_CW), lambda b, j: (l, 0, j))
    wsp = pl.BlockSpec((1, 2, RG_CW, RG_CW), lambda b, j: (l, 0, j, j))
    st = pl.BlockSpec((1, SUBLANES, RG_CW), lambda b, j: (b, 0, j))
    st_in = pl.BlockSpec((1, 1, SUBLANES, RG_CW), lambda b, j: (b, l_state, 0, j))
    return pl.pallas_call(
        functools.partial(_rglru_kernel, T=T),
        out_shape=[jax.ShapeDtypeStruct((B, T, RNN_WIDTH), BF16),
                   jax.ShapeDtypeStruct((B, SUBLANES, RNN_WIDTH), F32)],
        grid=(B, nj),
        in_specs=[seq, seq, vec8, pl.BlockSpec((1, 1, RG_CW), lambda b, j: (l, 0, j)),
                  wsp, wsp, vec8, vec8, vec8, st_in],
        out_specs=[seq, st],
        scratch_shapes=[pltpu.VMEM((RG_CW // LANES, T + 2 * SUBLANES, LANES), F32)]
        + [pltpu.VMEM((RG_CW // LANES, T + RG_SEGS * SUBLANES, LANES), F32)] * 4
        + [pltpu.VMEM((RG_SEGS, RG_CW), F32)] * 2,
        compiler_params=_cparams("parallel", "parallel"),
        name="rglru",
    )(xa, ga, cw, cb, wa, wx, ba, bx, lam, h0)


def _attend_slabs(jobs):
    M = jobs[0][0].shape[0]
    lo = lax.broadcasted_iota(jnp.int32, (M, LANES), 1) < HEAD_DIM
    scores = []
    for q2, srcs, _ in jobs:
        zero = jnp.zeros_like(q2)
        qs = jnp.concatenate([jnp.where(lo, q2, zero), jnp.where(lo, zero, q2)], 0)
        ss = []
        for k, _, bias in srcs:
            s = _dot_nt(qs, k)
            ss.append(s if bias is None else s + bias)
        scores.append(ss)
    maxima = []
    for (_, _, sink_col), ss in zip(jobs, scores):
        m = jnp.max(ss[0], -1, keepdims=True)
        for s in ss[1:]:
            m = jnp.maximum(m, jnp.max(s, -1, keepdims=True))
        maxima.append(m if sink_col is None else jnp.maximum(m, sink_col))
    outs = []
    for (_, srcs, sink_col), ss, m in zip(jobs, scores, maxima):
        den = None
        o = None
        for s, (_, v, _) in zip(ss, srcs):
            e = jnp.exp2(s - m)
            d_ = jnp.sum(e, -1, keepdims=True)
            o_ = _dot(e.astype(BF16), v)
            den = d_ if den is None else den + d_
            o = o_ if o is None else o + o_
        if sink_col is not None:
            den = den + jnp.exp2(sink_col - m)
        o = o / den
        outs.append(jnp.where(lo, o[:M], o[M:]))
    return outs


def _sink_col(sink_ref, l, j, M):
    r = lax.broadcasted_iota(jnp.int32, (2 * M, 1), 0)
    return jnp.where(r < M, sink_ref[l, 2 * j], sink_ref[l, 2 * j + 1]) * LOG2E


def _slab(j):
    return slice(j * LANES, (j + 1) * LANES)


def _attn_ctx_kernel(sink_ref, nq_ref, nk_ref, nv_ref, sq_ref, skd_ref, svd_ref, yb_ref, yc_ref, *, l):
    M = nq_ref.shape[1]
    nslab = W512 // LANES
    jobs = []
    for j in range(nslab):
        k2 = nk_ref[0, 0, :, _slab(j)].astype(BF16)
        v2 = nv_ref[0, 0, :, _slab(j)].astype(BF16)
        jobs.append((nq_ref[0, :, _slab(j)], [(k2, v2, None)], None))
    for j in range(nslab):
        g = j // 2
        jobs.append((sq_ref[0, :, _slab(j)], [(skd_ref[0, :, _slab(g)], svd_ref[0, :, _slab(g)], None)],
                     _sink_col(sink_ref, l, j, M)))
    outs = _attend_slabs(jobs)
    for j in range(nslab):
        yb_ref[0, :, _slab(j)] = outs[j].astype(yb_ref.dtype)
        yc_ref[0, :, _slab(j)] = outs[nslab + j].astype(yc_ref.dtype)


def _attn_ctx_call(sink, nq, nk, nv, sq, skd, svd, l):
    B, T, _ = nq.shape
    s512 = pl.BlockSpec((1, T, W512), lambda b: (b, 0, 0))
    s256 = pl.BlockSpec((1, T, 2 * LANES), lambda b: (b, 0, 0))
    cache = pl.BlockSpec((1, 1, T, W512), lambda b: (b, l, 0, 0))
    return pl.pallas_call(
        functools.partial(_attn_ctx_kernel, l=l),
        out_shape=[jax.ShapeDtypeStruct((B, T, W512), BF16)] * 2,
        grid=(B,),
        in_specs=[pl.BlockSpec(memory_space=pltpu.SMEM), s512, cache, cache, s512, s256, s256],
        out_specs=[s512, s512],
        compiler_params=_cparams("parallel"),
        name="attn_ctx",
    )(sink, nq, nk, nv, sq, skd, svd)


NAT_NLOC = NAT_KR * GRID_W


def _nat_lat_kernel(q_ref, k_ref, v_ref, ck_ref, cv_ref, bias_ref, y_ref, *, rows):
    r = pl.program_id(1)
    rstart = jnp.clip(r - NAT_KR // 2, 0, rows - NAT_KR)
    d = r - rstart
    kbase = pl.multiple_of(rstart * GRID_W, GRID_W)
    jobs = []
    for j in range(W512 // LANES):
        k2 = k_ref[0, pl.ds(kbase, NAT_NLOC), _slab(j)]
        v2 = v_ref[0, pl.ds(kbase, NAT_NLOC), _slab(j)]
        jobs.append((q_ref[0, :, _slab(j)],
                     [(k2, v2, bias_ref[0, j, d]), (ck_ref[0, 0, :, _slab(j)], cv_ref[0, 0, :, _slab(j)], None)], None))
    for j, o in enumerate(_attend_slabs(jobs)):
        y_ref[0, :, _slab(j)] = o.astype(y_ref.dtype)


def _nat_lat_call(q, k, v, ck, cv, bias, l):
    B, T, _ = q.shape
    rows = T // GRID_W
    P = ck.shape[2]
    qs = pl.BlockSpec((1, GRID_W, W512), lambda b, r: (b, r, 0))
    full = pl.BlockSpec((1, T, W512), lambda b, r: (b, 0, 0))
    cs = pl.BlockSpec((1, 1, P, W512), lambda b, r: (b, l, 0, 0))
    return pl.pallas_call(
        functools.partial(_nat_lat_kernel, rows=rows),
        out_shape=jax.ShapeDtypeStruct((B, T, W512), BF16),
        grid=(B, rows),
        in_specs=[qs, full, full, cs, cs, _layer_spec(bias.shape, l)],
        out_specs=qs,
        compiler_params=_cparams("parallel", "arbitrary"),
        name="nat_lat",
    )(q, k, v, ck, cv, bias)


def _swa_band_masks():
    i = np.arange(2 * SWA_BLOCK)[:, None] % SWA_BLOCK
    j = np.arange(SWA_SPAN)[None, :]
    tabs = []
    for span_start in (0, -SWA_WINDOW, -2 * SWA_WINDOW):
        dist = span_start + j - i
        tabs.append(np.where(np.abs(dist) <= SWA_WINDOW, 0.0, NEG_INF))
    return jnp.asarray(np.stack(tabs), F32)


def _swa_lat_kernel(sink_ref, q_ref, kd_ref, vd_ref, ckd_ref, cvd_ref, band_ref, y_ref, *, T, l):
    blk = pl.program_id(1)
    M = SWA_BLOCK
    start = blk * SWA_BLOCK
    ks = pl.multiple_of(jnp.clip(start - SWA_WINDOW, 0, T - SWA_SPAN), SWA_BLOCK)
    bias = band_ref[jnp.where(blk == 0, 0, jnp.where(blk == T // SWA_BLOCK - 1, 2, 1))]
    jobs = []
    for j in range(W512 // LANES):
        g = j // 2
        jobs.append((q_ref[0, :, _slab(j)],
                     [(kd_ref[0, pl.ds(ks, SWA_SPAN), _slab(g)], vd_ref[0, pl.ds(ks, SWA_SPAN), _slab(g)], bias),
                      (ckd_ref[0, 0, :, _slab(g)], cvd_ref[0, 0, :, _slab(g)], None)],
                     _sink_col(sink_ref, l, j, M)))
    for j, o in enumerate(_attend_slabs(jobs)):
        y_ref[0, :, _slab(j)] = o.astype(y_ref.dtype)


def _swa_lat_call(sink, q, kd, vd, ckd, cvd, l):
    B, T, _ = q.shape
    P = ckd.shape[2]
    qs = pl.BlockSpec((1, SWA_BLOCK, W512), lambda b, i: (b, i, 0))
    full = pl.BlockSpec((1, T, 2 * LANES), lambda b, i: (b, 0, 0))
    cs = pl.BlockSpec((1, 1, P, 2 * LANES), lambda b, i: (b, l, 0, 0))
    assert T // SWA_BLOCK >= 3
    band = _swa_band_masks()
    return pl.pallas_call(
        functools.partial(_swa_lat_kernel, T=T, l=l),
        out_shape=jax.ShapeDtypeStruct((B, T, W512), BF16),
        grid=(B, T // SWA_BLOCK),
        in_specs=[pl.BlockSpec(memory_space=pltpu.SMEM), qs, full, full, cs, cs, _const_spec(band.shape)],
        out_specs=qs,
        compiler_params=_cparams("parallel", "arbitrary"),
        name="swa_lat",
    )(sink, q, kd, vd, ckd, cvd, band)


FN_N2 = 256
FN_GP = 2


def _fourier_kernel(x_ref, cc_ref, m_ref, y_ref, *scratch, n1):
    gw = FN_GP * FNET_GC
    cc = cc_ref[...]
    for gp in range(FNET_GROUPS // FN_GP):
        for t1 in range(n1):
            ws = []
            for g in range(FN_GP):
                lo = t1 * FNET_WIDTH + (gp * FN_GP + g) * FNET_GC
                ws.append(_dot(x_ref[0, :, lo:lo + FNET_GC], cc))
            wr = jnp.concatenate([w[:, :FNET_GC] for w in ws], -1)
            wi = jnp.concatenate([w[:, FNET_GC:] for w in ws], -1)
            v = jnp.concatenate([wr, wi], 0).astype(BF16)
            b = _dot(m_ref[t1], v)
            if n1 == 1:
                y_ref[0, :, gp * gw:(gp + 1) * gw] = b.astype(y_ref.dtype)
            else:
                scratch[0][t1] = b
        if n1 > 1:
            _dft16_real(scratch[0], y_ref, gp * gw, gw)


FN_RCH = 16


def _dft16_real(bs, y_ref, col0, gw):
    tw = {m: (math.cos(2 * math.pi * m / 16), math.sin(2 * math.pi * m / 16)) for m in (1, 2, 3, 6, 9)}

    def body(rc, carry):
        r0 = pl.multiple_of(rc * FN_RCH, FN_RCH)
        re = [bs[t1, pl.ds(r0, FN_RCH), :] for t1 in range(16)]
        im = [bs[t1, pl.ds(FN_N2 + r0, FN_RCH), :] for t1 in range(16)]
        h_re = [[None] * 4 for _ in range(4)]
        h_im = [[None] * 4 for _ in range(4)]
        for b in range(4):
            s02r, s02i = re[b] + re[8 + b], im[b] + im[8 + b]
            d02r, d02i = re[b] - re[8 + b], im[b] - im[8 + b]
            s13r, s13i = re[4 + b] + re[12 + b], im[4 + b] + im[12 + b]
            d13r, d13i = re[4 + b] - re[12 + b], im[4 + b] - im[12 + b]
            g = [(s02r + s13r, s02i + s13i), (d02r + d13i, d02i - d13r),
                 (s02r - s13r, s02i - s13i), (d02r - d13i, d02i + d13r)]
            for c in range(4):
                gr, gi = g[c]
                m = b * c
                if m == 0:
                    hr, hi = gr, gi
                elif m == 4:
                    hr, hi = gi, -gr
                else:
                    cs, sn = tw[m]
                    hr = cs * gr + sn * gi
                    hi = (cs * gi - sn * gr) if b % 2 else None
                h_re[b][c], h_im[b][c] = hr, hi
        for c in range(4):
            p = h_re[0][c] + h_re[2][c]
            q = h_re[0][c] - h_re[2][c]
            r = h_re[1][c] + h_re[3][c]
            s = h_im[1][c] - h_im[3][c]
            for d, val in enumerate((p + r, q + s, p - r, q - s)):
                k1 = c + 4 * d
                y_ref[0, pl.ds(k1 * FN_N2 + r0, FN_RCH), col0:col0 + gw] = val.astype(y_ref.dtype)
        return carry

    lax.fori_loop(0, FN_N2 // FN_RCH, body, 0)


def _dft_consts(T):
    n1 = T // FN_N2
    j = np.arange(FNET_GC)
    ang = 2 * np.pi * np.outer(j, j) / FNET_GC
    sc = 1.0 / math.sqrt(FNET_GC)
    cc = np.concatenate([np.cos(ang) * sc, -np.sin(ang) * sc], 1)
    k2 = np.arange(FN_N2)[:, None]
    t2 = np.arange(FN_N2)[None, :]
    st = 1.0 / math.sqrt(T)
    mats = []
    for t1 in range(n1):
        th = 2 * np.pi * ((k2 * (n1 * t2 + t1)) % T) / T
        c, s = np.cos(th) * st, np.sin(th) * st
        top = np.concatenate([c, s], 1)
        mats.append(top if n1 == 1 else np.concatenate([top, np.concatenate([-s, c], 1)], 0))
    return n1, jnp.asarray(cc, BF16), jnp.asarray(np.stack(mats), BF16)


def _fourier_call(xv, T):
    B = xv.shape[0]
    n1, cc, mats = _dft_consts(T)
    assert n1 in (1, 16)
    gw = FN_GP * FNET_GC
    scratch = [pltpu.VMEM((n1, 2 * FN_N2, gw), F32)] if n1 > 1 else []
    return pl.pallas_call(
        functools.partial(_fourier_kernel, n1=n1),
        out_shape=jax.ShapeDtypeStruct((B, T, FNET_WIDTH), BF16),
        grid=(B,),
        in_specs=[pl.BlockSpec((1, FN_N2, n1 * FNET_WIDTH), lambda b: (b, 0, 0)),
                  _const_spec(cc.shape), _const_spec(mats.shape)],
        out_specs=pl.BlockSpec((1, T, FNET_WIDTH), lambda b: (b, 0, 0)),
        scratch_shapes=scratch,
        compiler_params=_cparams("parallel"),
        name="fourier",
    )(xv, cc, mats)


def _route(scores, biased):
    one, zero = jnp.float32(1.0), jnp.float32(0.0)
    in2, gscore = [], []
    for g in range(N_EXPERT_GROUPS):
        vs = [biased[g * EXPERTS_PER_GROUP + j:g * EXPERTS_PER_GROUP + j + 1] for j in range(EXPERTS_PER_GROUP)]
        gs = None
        for j in range(EXPERTS_PER_GROUP):
            rank = None
            for i in range(EXPERTS_PER_GROUP):
                if i == j:
                    continue
                beats = (vs[i] >= vs[j]) if i < j else (vs[i] > vs[j])
                t = jnp.where(beats, one, zero)
                rank = t if rank is None else rank + t
            keep = jnp.where(rank < 2.0, one, zero)
            in2.append(keep)
            t = keep * vs[j]
            gs = t if gs is None else gs + t
        gscore.append(gs)
    rows, sel = [], []
    for g in range(N_EXPERT_GROUPS):
        lost = None
        for i in range(N_EXPERT_GROUPS):
            if i == g:
                continue
            beats = (gscore[i] >= gscore[g]) if i < g else (gscore[i] > gscore[g])
            t = jnp.where(beats, one, zero)
            lost = t if lost is None else lost + t
        gsel = jnp.where(lost < 1.0, one, zero)
        for j in range(EXPERTS_PER_GROUP):
            e = g * EXPERTS_PER_GROUP + j
            sel.append(gsel * in2[e])
            rows.append(sel[-1] * scores[e:e + 1])
    tot = rows[0]
    for rr in rows[1:]:
        tot = tot + rr
    return jnp.concatenate(rows, 0) / tot, jnp.concatenate(sel, 0)


HP_ROWS = D_MODEL // 2 // LANES
Y_ROWS = D_MODEL // LANES


def _pack_halves(v):
    w = v.shape[1] // 2
    lo = lax.bitcast_convert_type(v[:, :w].astype(F32), jnp.uint32)
    hi = lax.bitcast_convert_type(v[:, w:].astype(F32), jnp.uint32)
    return (lo >> 16) | hi


def _unpack_halves(p):
    lo = lax.bitcast_convert_type(p << 16, F32).astype(BF16)
    hi = lax.bitcast_convert_type(p & jnp.uint32(0xFFFF0000), F32).astype(BF16)
    return lo, hi


def _merge_kernel(x_ref, mod_ref, ya_ref, yb_ref, yc_ref, yd_ref, wg_ref, wb_ref, wo_ref,
                  lng_ref, lnb_ref, wr_ref, rb_ref, x1_ref, hp_ref, rt_ref, *, nsub, l):
    m = mod_ref[0, 0]
    sub = x_ref.shape[0] // nsub
    y_refs = (ya_ref, yb_ref, yc_ref, yd_ref)
    st = [dict() for _ in range(nsub)]

    def rows(k):
        return slice(k * sub, (k + 1) * sub)

    def norm_in(k):
        st[k]['x'] = x_ref[rows(k), :]
        st[k]['h'] = (_ln(st[k]['x']) * (1.0 + m[1:2]) + m[0:1]).astype(BF16)
        st[k]['merged'] = None

    def branch(k, b):
        gate = 0.5 * jnp.tanh(_dot(st[k]['h'], wg_ref[0, :, b * D_MODEL:(b + 1) * D_MODEL])) + 0.5
        term = gate * _dot(y_refs[b][rows(k), :], wb_ref[0, b])
        st[k]['merged'] = term if st[k]['merged'] is None else st[k]['merged'] + term

    def project(k):
        st[k]['out'] = _dot(st[k]['merged'].astype(BF16), wo_ref[0])

    def norm_out(k):
        x1 = _ln(ALPHA * st[k]['x'] + m[2:3] * st[k]['out']) * lng_ref[l, 0:1, :] + lnb_ref[l, 0:1, :]
        x1_ref[rows(k), :] = x1
        st[k]['h2'] = (_ln(x1) * (1.0 + m[4:5]) + m[3:4]).astype(BF16)

    def pack(k):
        packed = _pack_halves(st[k]['h2'])
        for c in range(HP_ROWS):
            hp_ref[pl.ds(k * sub * HP_ROWS + c, sub, stride=HP_ROWS), :] = packed[:, c * LANES:(c + 1) * LANES]

    def route(k):
        logits = _dot_nt(wr_ref[...], st[k]['h2'])
        e = jnp.exp(logits - jnp.max(logits, 0, keepdims=True))
        scores = e / jnp.sum(e, 0, keepdims=True)
        comb, sel = _route(scores, scores + rb_ref[...])
        rt_ref[:, rows(k)] = jnp.concatenate([comb, sel], 0)

    tail = (norm_out, pack, route)
    for k in range(nsub):
        norm_in(k)
    for k in range(nsub):
        for b in range(N_BRANCH):
            branch(k, b)
            if k > 0 and b < len(tail):
                tail[b](k - 1)
        project(k)
    for stage in tail:
        stage(nsub - 1)


def _merge_call(x, mod, per_seq, ys, wg, wb, wo, l, lng, lnb, wr_t, rb, T, TM):
    N = x.shape[0]
    tok = lambda w: pl.BlockSpec((TM, w), lambda i: (i, 0))
    return pl.pallas_call(
        functools.partial(_merge_kernel, nsub=max(1, TM // SUB_TM), l=l),
        out_shape=[jax.ShapeDtypeStruct((N, D_MODEL), F32), jax.ShapeDtypeStruct((N * HP_ROWS, LANES), jnp.uint32),
                   jax.ShapeDtypeStruct((2 * N_EXPERTS, N), F32)],
        grid=(N // TM,),
        in_specs=[tok(D_MODEL), _mod_spec(l, per_seq, TM, T),
                  tok(W512), tok(W512), tok(W512), tok(W512),
                  _layer_spec(wg.shape, l), _layer_spec(wb.shape, l), _layer_spec(wo.shape, l),
                  _const_spec(lng.shape), _const_spec(lnb.shape), _const_spec(wr_t.shape), _const_spec(rb.shape)],
        out_specs=[tok(D_MODEL), pl.BlockSpec((TM * HP_ROWS, LANES), lambda i: (i, 0)),
                   pl.BlockSpec((2 * N_EXPERTS, TM), lambda i: (0, i))],
        compiler_params=_cparams("parallel"),
        name="merge",
    )(x, mod, *ys, wg, wb, wo, lng, lnb, wr_t, rb)


MOE_TM = 2048
MOE_RB = 256
MOE_KC = 4
MOE_CH = MOE_TM // MOE_KC
MOE_SLOTS = 2 * MOE_TM + N_EXPERTS * SUBLANES + MOE_RB
MOE_UNROLL = 8


def _route_kernel(rt_ref, u_ref, pos_ref, wts_ref, seg_ref):
    comb = rt_ref[0:N_EXPERTS, :]
    sel = rt_ref[N_EXPERTS:2 * N_EXPERTS, :]
    nb = MOE_TM // LANES
    stacked = jnp.concatenate([sel[:, b * LANES:(b + 1) * LANES] for b in range(nb)], 0)
    within = _dot(stacked.astype(BF16), u_ref[...])
    tot = jnp.sum(stacked, -1, keepdims=True)
    base = jnp.zeros((N_EXPERTS, 1), F32)
    bases = []
    for b in range(nb):
        bases.append(base)
        base = base + tot[b * N_EXPERTS:(b + 1) * N_EXPERTS]
    cnt = base
    padded = jnp.floor((cnt + (SUBLANES - 1.0)) * (1.0 / SUBLANES)) * SUBLANES
    rowi = lax.broadcasted_iota(jnp.int32, (N_EXPERTS, 1), 0)
    off = jnp.zeros((N_EXPERTS, 1), F32)
    for e in range(N_EXPERTS - 1):
        off = off + jnp.where(rowi > e, padded[e:e + 1, :], 0.0)
    pos_rows, wts_rows = [], []
    for b in range(nb):
        slot = within[b * N_EXPERTS:(b + 1) * N_EXPERTS] + (bases[b] + off)
        seen = jnp.zeros((1, LANES), F32)
        acc = [jnp.zeros((1, LANES), F32) for _ in range(4)]
        for e in range(N_EXPERTS):
            s_e = sel[e:e + 1, b * LANES:(b + 1) * LANES]
            c_e = comb[e:e + 1, b * LANES:(b + 1) * LANES]
            first = jnp.where(seen == 0.0, s_e, 0.0)
            second = jnp.where(seen == 1.0, s_e, 0.0)
            acc[0] = acc[0] + first * slot[e:e + 1]
            acc[1] = acc[1] + second * slot[e:e + 1]
            acc[2] = acc[2] + first * c_e
            acc[3] = acc[3] + second * c_e
            seen = seen + s_e
        pos_rows += acc[0:2]
        wts_rows += acc[2:4]
    pos_ref[...] = (jnp.concatenate(pos_rows, 0) * float(HP_ROWS)).astype(jnp.int32).reshape(2 * MOE_TM)
    wts_ref[...] = jnp.concatenate(wts_rows, 0).reshape(2 * MOE_TM)
    lane = lax.broadcasted_iota(jnp.int32, (N_EXPERTS, LANES), 1)
    diag = lane == lax.broadcasted_iota(jnp.int32, (N_EXPERTS, LANES), 0)
    off_row = jnp.sum(jnp.where(diag, off, 0.0), 0, keepdims=True)
    cnt_row = jnp.sum(jnp.where(diag, cnt, 0.0), 0, keepdims=True)
    seg_ref[0] = jnp.concatenate([off_row, cnt_row, jnp.zeros((SUBLANES - 2, LANES), F32)], 0).astype(jnp.int32)


def _route_call(rt):
    N = rt.shape[1]
    nt = N // MOE_TM
    u = jnp.asarray(np.triu(np.ones((LANES, LANES), np.float32), 1), BF16)
    return pl.pallas_call(
        _route_kernel,
        out_shape=[jax.ShapeDtypeStruct((nt * 2 * MOE_TM,), jnp.int32),
                   jax.ShapeDtypeStruct((nt * 2 * MOE_TM,), F32),
                   jax.ShapeDtypeStruct((nt, SUBLANES, LANES), jnp.int32)],
        grid=(nt,),
        in_specs=[pl.BlockSpec((2 * N_EXPERTS, MOE_TM), lambda i: (0, i)), _const_spec(u.shape)],
        out_specs=[pl.BlockSpec((2 * MOE_TM,), lambda i: (i,)),
                   pl.BlockSpec((2 * MOE_TM,), lambda i: (i,)),
                   pl.BlockSpec((1, SUBLANES, LANES), lambda i: (i, 0, 0))],
        compiler_params=_cparams("parallel"),
        name="route",
    )(rt, u)


def _table_index(it):
    per_block = LANES // MOE_UNROLL
    return (it // per_block) * (2 * LANES) + (it % per_block) * MOE_UNROLL


def _moe_kernel(pos_ref, wts_ref, seg_ref, hp_ref, w1_ref, w3_ref, w2_ref, x1_ref, mod_ref, lng_ref, lnb_ref,
                o_ref, xs, ys, oc, *, l):
    s = pl.program_id(1)
    half = D_MODEL // 2

    @pl.when(s == 0)
    def _dispatch():
        xs[...] = jnp.zeros_like(xs)

        def body(it, carry):
            t0 = pl.multiple_of(it * (MOE_UNROLL * HP_ROWS), MOE_UNROLL * HP_ROWS)
            i0 = _table_index(it)
            for j in range(MOE_UNROLL):
                row = hp_ref[pl.ds(t0 + HP_ROWS * j, HP_ROWS), :]
                xs[pl.ds(pl.multiple_of(pos_ref[i0 + j], HP_ROWS), HP_ROWS), :] = row
                xs[pl.ds(pl.multiple_of(pos_ref[i0 + LANES + j], HP_ROWS), HP_ROWS), :] = row
            return carry

        lax.fori_loop(0, MOE_TM // MOE_UNROLL, body, 0)

    @pl.when(s < N_EXPERTS)
    def _expert():
        off = seg_ref[0, 0, s]
        cnt = seg_ref[0, 1, s]
        nblk = (cnt + (MOE_RB // 2 - 1)) // MOE_RB

        def ffn_rows(r0, nrows):
            parts = [_unpack_halves(xs[pl.ds(pl.multiple_of(r0 * HP_ROWS, SUBLANES) + c, nrows, stride=HP_ROWS), :])
                     for c in range(HP_ROWS)]
            lo = jnp.concatenate([p[0] for p in parts], -1)
            hi = jnp.concatenate([p[1] for p in parts], -1)
            a = _dot(lo, w1_ref[0, 0, :half, :]) + _dot(hi, w1_ref[0, 0, half:, :])
            g = _dot(lo, w3_ref[0, 0, :half, :]) + _dot(hi, w3_ref[0, 0, half:, :])
            act = (a * jax.nn.sigmoid(a) * g).astype(BF16)
            y = _dot(act, w2_ref[0, 0])
            y0 = pl.multiple_of(r0 * Y_ROWS, SUBLANES)
            for c in range(Y_ROWS):
                ys[pl.ds(y0 + c, nrows, stride=Y_ROWS), :] = y[:, c * LANES:(c + 1) * LANES]

        def blk(i, carry):
            ffn_rows(pl.multiple_of(off + i * MOE_RB, SUBLANES), MOE_RB)
            return carry

        lax.fori_loop(0, nblk, blk, 0)

        @pl.when(cnt > nblk * MOE_RB)
        def _tail():
            ffn_rows(pl.multiple_of(off + nblk * MOE_RB, SUBLANES), MOE_RB // 2)

    @pl.when(s >= N_EXPERTS)
    def _combine():
        it0 = (s - N_EXPERTS) * (MOE_CH // MOE_UNROLL)

        def body(it, carry):
            t0 = pl.multiple_of(it * (MOE_UNROLL * Y_ROWS), MOE_UNROLL * Y_ROWS)
            i0 = _table_index(it0 + it)
            for j in range(MOE_UNROLL):
                p0 = pl.multiple_of(pos_ref[i0 + j] * (Y_ROWS // HP_ROWS), Y_ROWS)
                p1 = pl.multiple_of(pos_ref[i0 + LANES + j] * (Y_ROWS // HP_ROWS), Y_ROWS)
                oc[pl.ds(t0 + Y_ROWS * j, Y_ROWS), :] = (wts_ref[i0 + j] * ys[pl.ds(p0, Y_ROWS), :]
                                                         + wts_ref[i0 + LANES + j] * ys[pl.ds(p1, Y_ROWS), :])
            return carry

        lax.fori_loop(0, MOE_CH // MOE_UNROLL, body, 0)
        moe = jnp.concatenate([oc[pl.ds(c, MOE_CH, stride=Y_ROWS), :] for c in range(Y_ROWS)], -1)
        m = mod_ref[0, 0]
        o_ref[...] = _ln(ALPHA * x1_ref[...] + m[5:6] * moe) * lng_ref[l, 1:2, :] + lnb_ref[l, 1:2, :]


def _moe_call(pos, wts, seg, hp, w1, w3, w2, l, x1, mod, per_seq, lng, lnb, T):
    N = x1.shape[0]
    nt = N // MOE_TM
    flat = pl.BlockSpec((2 * MOE_TM,), lambda i, s: (i,), memory_space=pltpu.SMEM)
    wspec = lambda shp: pl.BlockSpec((1, 1) + shp, lambda i, s: (l, jnp.minimum(s, N_EXPERTS - 1), 0, 0))
    chunk = pl.BlockSpec((MOE_CH, D_MODEL),
                         lambda i, s: (i * MOE_KC + jnp.clip(s - N_EXPERTS, 0, MOE_KC - 1), 0))
    return pl.pallas_call(
        functools.partial(_moe_kernel, l=l),
        out_shape=jax.ShapeDtypeStruct((N, D_MODEL), F32),
        grid=(nt, N_EXPERTS + MOE_KC),
        in_specs=[flat, flat,
                  pl.BlockSpec((1, SUBLANES, LANES), lambda i, s: (i, 0, 0), memory_space=pltpu.SMEM),
                  pl.BlockSpec((MOE_TM * HP_ROWS, LANES), lambda i, s: (i, 0)),
                  wspec((D_MODEL, D_EXPERT)), wspec((D_MODEL, D_EXPERT)), wspec((D_EXPERT, D_MODEL)),
                  chunk, _mod_spec(l, per_seq, MOE_TM, T),
                  _const_spec(lng.shape), _const_spec(lnb.shape)],
        out_specs=chunk,
        scratch_shapes=[pltpu.VMEM((MOE_SLOTS * HP_ROWS, LANES), jnp.uint32),
                        pltpu.VMEM((MOE_SLOTS * Y_ROWS, LANES), F32),
                        pltpu.VMEM((MOE_CH * Y_ROWS, LANES), F32)],
        compiler_params=_cparams("parallel", "arbitrary"),
        name="moe",
    )(pos, wts, seg, hp, w1, w3, w2, x1, mod, lng, lnb)


def _rope_tables(T):
    t = jnp.arange(T)
    nf = HEAD_DIM // 4
    inv = ROPE_BASE ** (-jnp.arange(nf, dtype=F32) / nf)
    ar = (t // GRID_W).astype(F32)[:, None] * inv
    ac = (t % GRID_W).astype(F32)[:, None] * inv
    cos = jnp.concatenate([jnp.cos(ar), jnp.cos(ar), jnp.cos(ac), jnp.cos(ac)], -1)
    sin = jnp.concatenate([-jnp.sin(ar), jnp.sin(ar), -jnp.sin(ac), jnp.sin(ac)], -1)
    return jnp.tile(cos, (1, LANES // HEAD_DIM)), jnp.tile(sin, (1, LANES // HEAD_DIM))


def _nat_bias_table(rpb):
    q = np.arange(GRID_W)
    kc = np.arange(GRID_W)
    cstart = np.clip(q - NAT_KC // 2, 0, GRID_W - NAT_KC)
    ok = (kc[None, :] >= cstart[:, None]) & (kc[None, :] < cstart[:, None] + NAT_KC)
    cidx = np.clip(kc[None, :] - q[:, None] + NAT_KC - 1, 0, 2 * NAT_KC - 2)
    onehot = (np.arange(2 * NAT_KC - 1)[:, None, None] == cidx[None]).astype(np.float32)
    t = jnp.einsum('lhrc,cqk->lhqrk', rpb.astype(F32), jnp.asarray(onehot), precision=lax.Precision.HIGHEST)
    t = jnp.where(jnp.asarray(ok)[None, None, :, None, :], t * LOG2E, NEG_INF)
    nl = rpb.shape[0]
    per_d = [t[:, :, :, NAT_KR - 1 - d:2 * NAT_KR - 1 - d, :].reshape(nl, NAT_HEADS // 2, 2 * GRID_W, NAT_KR * GRID_W)
             for d in range(NAT_KR)]
    return jnp.stack(per_d, 2)


def _dup_heads(a):
    a = jnp.broadcast_to(a[..., :, None, :], a.shape[:-1] + (2, a.shape[-1]))
    return a.reshape(a.shape[:-3] + (-1,))


def _block_diag(w):
    eye = jnp.eye(RNN_BLOCKS, dtype=w.dtype)
    return jnp.einsum('dnio,nm->dnimo', w, eye).reshape(2, RNN_WIDTH, RNN_WIDTH)


def kernel(x_prompt, x_sample, c, cache_nat_k, cache_nat_v, cache_swa_k, cache_swa_v, state_rglru, c_ctx,
           w_ada, b_ada, w_in, rg_conv_w, rg_conv_b, rg_wa, rg_ba, rg_wx, rg_bx, rg_lambda, nat_rpb,
           swa_sink, w_branch, w_out, ln_g, ln_b, w_router, router_bias, w1, w3, w2):
    B_c, T_c, _ = x_prompt.shape
    B_l, T_l, _ = x_sample.shape
    P = cache_nat_k.shape[2]

    cv = jnp.concatenate([c_ctx[None], c, jnp.zeros((ADA_ROWS - 1 - B_l, D_MODEL), F32)], 0)
    mod_all = _ada_call(cv, w_ada, b_ada).reshape(DEPTH, ADA_ROWS, 6, D_MODEL)
    mod_all = jnp.pad(mod_all, ((0, 0), (0, 0), (0, SUBLANES - 6), (0, 0)))

    kv_w = SWA_KV_HEADS * HEAD_DIM
    dup_w = 2 * kv_w
    xa0, ga0, nq0, nk0, nv0, sq0, sk0 = (i * W512 for i in range(7))
    sv0 = sk0 + kv_w
    xf0 = sv0 + kv_w
    assert xf0 + W512 == GATE_OFF
    w_in_b = w_in.astype(BF16)
    head = lambda c0, g: w_in_b[:, :, c0 + g * HEAD_DIM:c0 + (g + 1) * HEAD_DIM]
    w_tail_ctx = jnp.concatenate([head(c0, g) for c0 in (sk0, sv0) for g in (0, 0, 1, 1)], -1)
    w_tail_lat = jnp.concatenate([w_in_b[:, :, xf0:GATE_OFF], w_tail_ctx], -1)
    w_gate = (0.5 * w_in[:, :, GATE_OFF:]).astype(BF16)
    w_branch_b = w_branch.astype(BF16)
    w_out_b = w_out.astype(BF16)
    w1_b, w3_b, w2_b = w1.astype(BF16), w3.astype(BF16), w2.astype(BF16)
    wr_t = w_router.T.astype(BF16)
    wa_bd = jnp.stack([_block_diag(0.5 * rg_wa[l]) for l in range(DEPTH)]).astype(BF16)
    wx_bd = jnp.stack([_block_diag(0.5 * rg_wx[l]) for l in range(DEPTH)]).astype(BF16)

    def out(src, off, width, dtype, scale=1.0, rope=False, fourier=False, cache=None):
        return (src, off, width, dtype, scale, rope, fourier, cache)

    plan_ctx = (out(0, xa0, W512, F32), out(0, ga0, W512, F32), out(0, nq0, W512, BF16, Q_SCALE),
                out(0, nk0, W512, F32, cache=0), out(0, nv0, W512, F32, cache=1), out(0, sq0, W512, BF16, Q_SCALE),
                out(0, sk0, kv_w, F32, cache=2), out(0, sv0, kv_w, F32, cache=3),
                out(0, xf0, W512, BF16, fourier=True),
                out(1, 0, dup_w, BF16), out(1, dup_w, dup_w, BF16))
    plan_lat = (out(0, xa0, W512, F32), out(0, ga0, W512, F32), out(0, nq0, W512, BF16, Q_SCALE),
                out(0, nk0, W512, BF16), out(0, nv0, W512, BF16), out(0, sq0, W512, BF16, Q_SCALE, rope=True),
                out(1, 0, W512, BF16, fourier=True),
                out(1, W512, dup_w, BF16, rope=True), out(1, W512 + dup_w, dup_w, BF16))

    rope_tabs = _rope_tables(T_l)
    ck_nat = cache_nat_k.reshape(B_l, DEPTH, P, W512).astype(BF16)
    cv_nat = cache_nat_v.reshape(B_l, DEPTH, P, W512).astype(BF16)
    ckd_swa = _dup_heads(cache_swa_k).astype(BF16)
    cvd_swa = _dup_heads(cache_swa_v).astype(BF16)
    state8 = jnp.pad(state_rglru, ((0, 0), (0, 0), (0, SUBLANES - 2), (0, 0)))
    zero_state = jnp.zeros((B_c, 1, SUBLANES, RNN_WIDTH), F32)
    pad8 = lambda a: jnp.pad(a, ((0, 0), (0, SUBLANES - a.shape[1]), (0, 0)))
    rg_cw8, rg_cb3, rg_lam8 = pad8(rg_conv_w), rg_conv_b[:, None, :], pad8(rg_lambda)
    rg_ba8, rg_bx8 = pad8(0.5 * rg_ba), pad8(0.5 * rg_bx)
    nat_bias = _nat_bias_table(nat_rpb)

    def layer(x, l, ctx_pass, caches=None):
        B, T = (B_c, T_c) if ctx_pass else (B_l, T_l)
        whole = (B * T) % TOKEN_TM == 0 and (T % TOKEN_TM == 0 or TOKEN_TM % T == 0)
        TM = TOKEN_TM if whole else min(T, TOKEN_TM)
        sub_tm = TM // max(1, TM // SUB_TM)
        per_seq = not ctx_pass
        if ctx_pass:
            xa, ga, nq, nk, nv, sq, sk, sv, xf, skd, svd = _pre_call(
                x, mod_all, per_seq, w_in_b, GATE_OFF, w_tail_ctx, l, plan_ctx, B, T, TM, None, caches)
        else:
            xa, ga, nq, nk, nv, sq, xf, skd, svd = _pre_call(
                x, mod_all, per_seq, w_in_b, sk0, w_tail_lat, l, plan_lat, B, T, TM, rope_tabs)
        r3 = lambda a: a.reshape(B, T, a.shape[-1])
        ya, hfin = _rglru_call(r3(xa), r3(ga), rg_cw8, rg_cb3, wa_bd, wx_bd, rg_ba8, rg_bx8, rg_lam8,
                               zero_state if ctx_pass else state8, l, 0 if ctx_pass else l)
        if ctx_pass:
            yb, yc = _attn_ctx_call(swa_sink, r3(nq), nk, nv, r3(sq), r3(skd), r3(svd), l)
        else:
            yb = _nat_lat_call(r3(nq), r3(nk), r3(nv), ck_nat, cv_nat, nat_bias, l)
            yc = _swa_lat_call(swa_sink, r3(sq), r3(skd), r3(svd), ckd_swa, cvd_swa, l)
        yd = _fourier_call(xf, T)
        f2 = lambda a: a.reshape(B * T, a.shape[-1])
        x1, hp, rt = _merge_call(
            x, mod_all, per_seq, (f2(ya), f2(yb), f2(yc), f2(yd)), w_gate, w_branch_b, w_out_b, l,
            ln_g, ln_b, wr_t, jnp.broadcast_to(router_bias[:, None], (N_EXPERTS, sub_tm)), T, TM)
        pos, wts, seg = _route_call(rt)
        x2 = _moe_call(pos, wts, seg, hp, w1_b, w3_b, w2_b, l, x1, mod_all, per_seq, ln_g, ln_b, T)
        new = ((nk, nv, sk, sv), hfin[:, :2]) if ctx_pass else None
        return x2, new

    y = x_prompt.reshape(B_c * T_c, D_MODEL)
    caches = tuple(jnp.zeros((B_c, DEPTH, T_c, w), F32)
                   for w in (W512, W512, SWA_KV_HEADS * HEAD_DIM, SWA_KV_HEADS * HEAD_DIM))
    states = []
    for l in range(DEPTH):
        y, (caches, st) = layer(y, l, True, caches)
        states.append(st)
    y_prompt = y.reshape(B_c, T_c, D_MODEL)
    new_nat_k = caches[0].reshape(B_c, DEPTH, T_c, NAT_HEADS, HEAD_DIM)
    new_nat_v = caches[1].reshape(B_c, DEPTH, T_c, NAT_HEADS, HEAD_DIM)
    new_swa_k = caches[2].reshape(B_c, DEPTH, T_c, SWA_KV_HEADS, HEAD_DIM)
    new_swa_v = caches[3].reshape(B_c, DEPTH, T_c, SWA_KV_HEADS, HEAD_DIM)
    new_state = jnp.stack(states, 1)

    y = x_sample.reshape(B_l * T_l, D_MODEL)
    for l in range(DEPTH):
        y, _ = layer(y, l, False)
    y_sample = y.reshape(B_l, T_l, D_MODEL)
    return (y_prompt, y_sample, new_nat_k, new_nat_v, new_swa_k, new_swa_v, new_state)
```

```python
import functools
import math

import jax
import jax.numpy as jnp
import numpy as np
from jax import lax
from jax.experimental import pallas as pl
from jax.experimental.pallas import tpu as pltpu

F32 = jnp.float32
BF16 = jnp.bfloat16

D_MODEL = 1024
DEPTH = 4
GRID_W = 64
HEAD_DIM = 64
ATTN_SCALE = HEAD_DIM ** -0.5
LOG2E = math.log2(math.e)
Q_SCALE = ATTN_SCALE * LOG2E
NEG_INF = -1e30
LN_EPS = 1e-5
ALPHA = (2 * DEPTH) ** 0.25
ROPE_BASE = 10000.0
RNN_WIDTH = 512
RNN_BLOCKS = 8
CONV_W = 4
RGLRU_C = 8.0
NAT_HEADS = 8
NAT_KR = 8
NAT_KC = 16
SWA_HEADS = 8
SWA_KV_HEADS = 2
SWA_WINDOW = 128
SWA_BLOCK = 128
SWA_SPAN = SWA_BLOCK + 2 * SWA_WINDOW
FNET_GROUPS = 4
FNET_WIDTH = 512
FNET_GC = FNET_WIDTH // FNET_GROUPS
N_BRANCH = 4
N_EXPERTS = 16
N_EXPERT_GROUPS = 4
EXPERTS_PER_GROUP = N_EXPERTS // N_EXPERT_GROUPS
D_EXPERT = 512
W512 = 512
GATE_OFF = 3840

LANES = 128
SUBLANES = 8
VMEM_LIMIT = 56 * 1024 * 1024
TOKEN_TM = 1024
SUB_TM = 512


def _cparams(*sem):
    return pltpu.CompilerParams(dimension_semantics=sem, vmem_limit_bytes=VMEM_LIMIT)


def _const_spec(shape):
    nd = len(shape)
    return pl.BlockSpec(shape, lambda *_: (0,) * nd, pipeline_mode=pl.Buffered(1))


def _layer_spec(shape, l):
    nd = len(shape)
    return pl.BlockSpec((1,) + tuple(shape[1:]), lambda *_: (l,) + (0,) * (nd - 1), pipeline_mode=pl.Buffered(1))


def _mod_spec(l, per_seq, tm, t):
    return pl.BlockSpec((1, 1, SUBLANES, D_MODEL),
                        lambda i, *_: (l, 1 + (i * tm) // t if per_seq else 0, 0, 0))


def _ln(x):
    mu = jnp.mean(x, -1, keepdims=True)
    xc = x - mu
    var = jnp.mean(xc * xc, -1, keepdims=True)
    return xc * lax.rsqrt(var + LN_EPS)


def _dot(a, b):
    return jnp.dot(a, b, preferred_element_type=F32)


def _dot_nt(a, b):
    return lax.dot_general(a, b, (((1,), (1,)), ((), ())), preferred_element_type=F32)


ADA_ROWS = 2 * SUBLANES
ADA_TN = D_MODEL


def _ada_kernel(c_ref, w_ref, b_ref, o_ref):
    cv = c_ref[...]
    s = (cv * jax.nn.sigmoid(cv)).astype(BF16)
    o_ref[0] = _dot(s, w_ref[0].astype(BF16)) + b_ref[0]


def _ada_call(cv, w_ada, b_ada):
    n = w_ada.shape[-1]
    return pl.pallas_call(
        _ada_kernel,
        out_shape=jax.ShapeDtypeStruct((DEPTH, ADA_ROWS, n), F32),
        grid=(DEPTH, n // ADA_TN),
        in_specs=[
            pl.BlockSpec((ADA_ROWS, D_MODEL), lambda l, j: (0, 0)),
            pl.BlockSpec((1, D_MODEL, ADA_TN), lambda l, j: (l, 0, j)),
            pl.BlockSpec((1, 1, ADA_TN), lambda l, j: (l, 0, j)),
        ],
        out_specs=pl.BlockSpec((1, ADA_ROWS, ADA_TN), lambda l, j: (l, 0, j)),
        compiler_params=_cparams("parallel", "parallel"),
        name="ada",
    )(cv, w_ada, b_ada.reshape(DEPTH, 1, n))


def _rope(u, cos, sin):
    lane = lax.broadcasted_iota(jnp.int32, cos.shape, 1)
    first = (lane & 31) < 16
    outs = []
    for j in range(u.shape[1] // LANES):
        s = u[:, j * LANES:(j + 1) * LANES]
        partner = jnp.where(first, pltpu.roll(s, LANES - 16, 1), pltpu.roll(s, 16, 1))
        outs.append(s * cos + partner * sin)
    return outs[0] if len(outs) == 1 else jnp.concatenate(outs, -1)


def _pre_kernel(*refs, plan, rope, n1, nsub, n_alias, seq_t):
    x_ref, mod_ref, wm_ref, wt_ref = refs[:4]
    refs = refs[4:]
    if rope:
        cos_ref, sin_ref = refs[:2]
        refs = refs[2:]
    refs = refs[n_alias:]
    outs = refs[:len(plan)]
    m = mod_ref[0, 0]
    sub = x_ref.shape[0] // nsub
    hs = [(_ln(x_ref[k * sub:(k + 1) * sub, :]) * (1.0 + m[1:2]) + m[0:1]).astype(BF16) for k in range(nsub)]
    for (src, off, width, _, scale, do_rope, fourier, slot), o_ref in zip(plan, outs):
        w_ref = wt_ref if src else wm_ref
        for k in range(nsub):
            rows = slice(k * sub, (k + 1) * sub)
            u = _dot(hs[k], w_ref[0, :, off:off + width])
            if do_rope:
                u = _rope(u, cos_ref[rows, :], sin_ref[rows, :])
            if scale != 1.0:
                u = u * scale
            if fourier and n1 > 1:
                stage = refs[len(plan)]
                for c in range(width // LANES):
                    stage[k, c] = u[:, c * LANES:(c + 1) * LANES]
                r = sub // n1
                for t1 in range(n1):
                    for c in range(width // LANES):
                        lo = t1 * width + c * LANES
                        o_ref[0, k * r:(k + 1) * r, lo:lo + LANES] = (
                            stage[k, c, pl.ds(t1, r, stride=n1), :].astype(o_ref.dtype))
            elif fourier or slot is not None:
                if sub <= seq_t:
                    pos = slice((k * sub) % seq_t, (k * sub) % seq_t + sub)
                    pieces = [((k * sub) // seq_t, pos, u)]
                else:
                    pieces = [(k * (sub // seq_t) + j, slice(None), u[j * seq_t:(j + 1) * seq_t])
                              for j in range(sub // seq_t)]
                for b_, pos, val in pieces:
                    if fourier:
                        o_ref[b_, pos, :] = val.astype(o_ref.dtype)
                    else:
                        o_ref[b_, 0, pos, :] = val.astype(o_ref.dtype)
            else:
                o_ref[rows, :] = u.astype(o_ref.dtype)


def _pre_call(x, mod, per_seq, w_main, n_main, w_tail, l, plan, B, T, TM, rope_tabs, caches=None):
    N = x.shape[0]
    tpb = max(1, T // TM)
    spt = max(1, TM // T)
    seq_t = min(T, TM)
    nsub = max(1, TM // SUB_TM)
    n1 = T // FN_N2
    rope = rope_tabs is not None
    n_tail = w_tail.shape[-1]
    in_specs = [
        pl.BlockSpec((TM, D_MODEL), lambda i: (i, 0)),
        _mod_spec(l, per_seq, TM, T),
        pl.BlockSpec((1, D_MODEL, n_main), lambda i: (l, 0, 0), pipeline_mode=pl.Buffered(1)),
        pl.BlockSpec((1, D_MODEL, n_tail), lambda i: (l, 0, 0), pipeline_mode=pl.Buffered(1)),
    ]
    args = [x, mod, w_main, w_tail]
    if rope:
        in_specs += [pl.BlockSpec((TM, LANES), lambda i: (i % tpb, 0))] * 2
        args += list(rope_tabs)
    out_shape, out_specs, scratch, aliases = [], [], [], {}
    for k, p in enumerate(plan):
        if p[6]:
            out_shape.append(jax.ShapeDtypeStruct((B, FN_N2, n1 * p[2]), p[3]))
            if n1 > 1:
                out_specs.append(pl.BlockSpec((1, TM // n1, n1 * p[2]), lambda i: (i // tpb, i % tpb, 0)))
                scratch.append(pltpu.VMEM((nsub, p[2] // LANES, TM // nsub, LANES), F32))
            else:
                out_specs.append(pl.BlockSpec((spt, seq_t, p[2]), lambda i: (i // tpb, i % tpb, 0)))
        elif p[7] is not None:
            out_shape.append(jax.ShapeDtypeStruct((B, DEPTH, T, p[2]), p[3]))
            out_specs.append(pl.BlockSpec((spt, 1, seq_t, p[2]), lambda i: (i // tpb, l, i % tpb, 0)))
            if caches is not None:
                aliases[len(args)] = k
                in_specs.append(pl.BlockSpec(memory_space=pl.ANY))
                args.append(caches[p[7]])
        else:
            out_shape.append(jax.ShapeDtypeStruct((N, p[2]), p[3]))
            out_specs.append(pl.BlockSpec((TM, p[2]), lambda i: (i, 0)))
    return pl.pallas_call(
        functools.partial(_pre_kernel, plan=plan, rope=rope, n1=n1, nsub=nsub, n_alias=len(aliases), seq_t=seq_t),
        out_shape=out_shape,
        grid=(N // TM,),
        in_specs=in_specs,
        out_specs=out_specs,
        scratch_shapes=scratch,
        input_output_aliases=aliases,
        compiler_params=_cparams("parallel"),
        name="pre",
    )(*args)


RG_CW = 256
RG_TCH = 256
RG_SEGS = SUBLANES


def _rglru_kernel(xa_ref, ga_ref, cw_ref, cb_ref, wa_ref, wx_ref, ba_ref, bx_ref, lam_ref, h0_ref,
                  y_ref, hfin_ref, xpad, a_f, u_f, a_b, u_b, carry_f, carry_b, *, T):
    cw_ = RG_CW
    zeros8 = jnp.zeros((SUBLANES, cw_), F32)
    for k in range(cw_ // LANES):
        xpad[k, 0:SUBLANES, :] = zeros8[:, :LANES]
        xpad[k, T + SUBLANES:T + 2 * SUBLANES, :] = zeros8[:, :LANES]
        xpad[k, SUBLANES:T + SUBLANES, :] = xa_ref[0, :, k * LANES:(k + 1) * LANES]
    cw = cw_ref[0]
    cb = cb_ref[0]
    lam = lam_ref[0]
    sp = jnp.maximum(-lam, 0.0) + jnp.log1p(jnp.exp(-jnp.abs(lam)))
    sp4 = (0.5 * RGLRU_C) * sp
    nsp4_log2e = sp4 * (-LOG2E)
    ba = ba_ref[0]
    bx = bx_ref[0]
    h0 = h0_ref[0, 0]

    def chunk(c, carry):
        base = pl.multiple_of(c * RG_TCH, RG_TCH)
        xc = cb
        for i in range(CONV_W):
            xc = xc + cw[i:i + 1] * get(xpad, pl.ds(base + (SUBLANES - 1 + i), RG_TCH, stride=1))
        xcb = xc.astype(BF16)
        half_xc = 0.5 * xc
        for d, (a_s, u_s) in enumerate(((a_f, u_f), (a_b, u_b))):
            r2 = jnp.tanh(_dot(xcb, wa_ref[0, d]) + ba[d:d + 1]) + 1.0
            i2 = jnp.tanh(_dot(xcb, wx_ref[0, d]) + bx[d:d + 1]) + 1.0
            a = jnp.exp2(r2 * nsp4_log2e[d:d + 1])
            v = jnp.tanh(r2 * sp4[d:d + 1]) * (1.0 + a * a)
            gain = jnp.where(v > 0.0, v * lax.rsqrt(v), 0.0)
            put(a_s, pl.ds(store_row(c), RG_TCH), a)
            put(u_s, pl.ds(store_row(c), RG_TCH), gain * (i2 * half_xc))
        return carry

    def get(ref, rows):
        return jnp.concatenate([ref[k, rows, :] for k in range(cw_ // LANES)], -1)

    def put(ref, rows, val):
        for k in range(cw_ // LANES):
            ref[k, rows, :] = val[:, k * LANES:(k + 1) * LANES]

    seg_len = T // RG_SEGS
    segmented = seg_len % RG_TCH == 0
    seg_pitch = seg_len + SUBLANES
    per_seg = seg_len // RG_TCH if segmented else 1

    def store_row(c):
        base = c * RG_TCH
        return pl.multiple_of(base + SUBLANES * (c // per_seg), SUBLANES) if segmented else pl.multiple_of(base, RG_TCH)

    lax.fori_loop(0, T // RG_TCH, chunk, 0)

    row = lax.broadcasted_iota(jnp.int32, (SUBLANES, cw_), 0)

    if segmented:
        def seg_step(j, carry):
            h_f, p_f, h_b, p_b = carry
            jb = seg_len - 1 - j
            rows_f = pl.ds(j, RG_SEGS, stride=seg_pitch)
            rows_b = pl.ds(jb, RG_SEGS, stride=seg_pitch)
            a = get(a_f, rows_f)
            h_f = a * h_f + get(u_f, rows_f)
            p_f = p_f * a
            put(u_f, rows_f, h_f)
            put(a_f, rows_f, p_f)
            a = get(a_b, rows_b)
            h_b = a * h_b + get(u_b, rows_b)
            p_b = p_b * a
            put(u_b, rows_b, h_b)
            put(a_b, rows_b, p_b)
            return h_f, p_f, h_b, p_b

        zero = jnp.zeros((RG_SEGS, cw_), F32)
        one = jnp.ones((RG_SEGS, cw_), F32)
        h_f, p_f, h_b, p_b = lax.fori_loop(0, seg_len, seg_step, (zero, one, zero, one), unroll=2)
        cf = [h0[0:1, :]]
        for s_ in range(RG_SEGS):
            cf.append(h_f[s_:s_ + 1, :] + p_f[s_:s_ + 1, :] * cf[s_])
        cb_in = [None] * RG_SEGS + [h0[1:2, :]]
        for s_ in range(RG_SEGS - 1, -1, -1):
            cb_in[s_] = h_b[s_:s_ + 1, :] + p_b[s_:s_ + 1, :] * cb_in[s_ + 1]
        hfin_ref[0] = jnp.where(row == 0, cf[RG_SEGS], jnp.where(row == 1, cb_in[0], 0.0))
        carry_f[...] = jnp.concatenate(cf[:RG_SEGS], 0)
        carry_b[...] = jnp.concatenate(cb_in[1:], 0)

        def emit_seg(c, carry):
            base = pl.multiple_of(c * RG_TCH, RG_TCH)
            s_ = c // per_seg
            rows = pl.ds(store_row(c), RG_TCH)
            h = (get(u_f, rows) + get(a_f, rows) * carry_f[pl.ds(s_, 1), :]
                 + get(u_b, rows) + get(a_b, rows) * carry_b[pl.ds(s_, 1), :])
            y_ref[0, pl.ds(base, RG_TCH), :] = (
                h * jax.nn.gelu(ga_ref[0, pl.ds(base, RG_TCH), :])).astype(y_ref.dtype)
            return carry

        lax.fori_loop(0, T // RG_TCH, emit_seg, 0)
        return

    def block_scan(a, u, reverse):
        for dd in (1, 2, 4):
            sh = SUBLANES - dd if reverse else dd
            a_n = pltpu.roll(a, sh, 0)
            u_n = pltpu.roll(u, sh, 0)
            ok = (row < SUBLANES - dd) if reverse else (row >= dd)
            u = jnp.where(ok, u + a * u_n, u)
            a = jnp.where(ok, a * a_n, a)
        return a, u

    nblk = T // SUBLANES

    def scan(i, carry):
        c_f, c_b = carry
        lo_f = pl.multiple_of(i * SUBLANES, SUBLANES)
        lo_b = pl.multiple_of((nblk - 1 - i) * SUBLANES, SUBLANES)
        a, u = block_scan(get(a_f, pl.ds(lo_f, SUBLANES)), get(u_f, pl.ds(lo_f, SUBLANES)), False)
        h_f = u + a * c_f
        put(u_f, pl.ds(lo_f, SUBLANES), h_f)
        a, u = block_scan(get(a_b, pl.ds(lo_b, SUBLANES)), get(u_b, pl.ds(lo_b, SUBLANES)), True)
        h_b = u + a * c_b
        put(u_b, pl.ds(lo_b, SUBLANES), h_b)
        return (jnp.broadcast_to(h_f[SUBLANES - 1:SUBLANES, :], (SUBLANES, cw_)),
                jnp.broadcast_to(h_b[0:1, :], (SUBLANES, cw_)))

    c_f, c_b = lax.fori_loop(
        0, nblk, scan,
        (jnp.broadcast_to(h0[0:1, :], (SUBLANES, cw_)), jnp.broadcast_to(h0[1:2, :], (SUBLANES, cw_))),
        unroll=2)
    hfin_ref[0] = jnp.where(row == 0, c_f, jnp.where(row == 1, c_b, 0.0))

    def emit(c, carry):
        base = pl.multiple_of(c * RG_TCH, RG_TCH)
        h = get(u_f, pl.ds(base, RG_TCH)) + get(u_b, pl.ds(base, RG_TCH))
        y_ref[0, pl.ds(base, RG_TCH), :] = (h * jax.nn.gelu(ga_ref[0, pl.ds(base, RG_TCH), :])).astype(y_ref.dtype)
        return carry

    lax.fori_loop(0, T // RG_TCH, emit, 0)


def _rglru_call(xa, ga, cw, cb, wa, wx, ba, bx, lam, h0, l, l_state):
    B, T, _ = xa.shape
    nj = RNN_WIDTH // RG_CW
    seq = pl.BlockSpec((1, T, RG_CW), lambda b, j: (b, 0, j))
    vec8 = pl.BlockSpec((1, SUBLANES, RG_CW), lambda b, j: (l, 0, j))
    wsp = pl.BlockSpec((1, 2, RG_CW, RG_CW), lambda b, j: (l, 0, j, j))
    st = pl.BlockSpec((1, SUBLANES, RG_CW), lambda b, j: (b, 0, j))
    st_in = pl.BlockSpec((1, 1, SUBLANES, RG_CW), lambda b, j: (b, l_state, 0, j))
    return pl.pallas_call(
        functools.partial(_rglru_kernel, T=T),
        out_shape=[jax.ShapeDtypeStruct((B, T, RNN_WIDTH), BF16),
                   jax.ShapeDtypeStruct((B, SUBLANES, RNN_WIDTH), F32)],
        grid=(B, nj),
        in_specs=[seq, seq, vec8, pl.BlockSpec((1, 1, RG_CW), lambda b, j: (l, 0, j)),
                  wsp, wsp, vec8, vec8, vec8, st_in],
        out_specs=[seq, st],
        scratch_shapes=[pltpu.VMEM((RG_CW // LANES, T + 2 * SUBLANES, LANES), F32)]
        + [pltpu.VMEM((RG_CW // LANES, T + RG_SEGS * SUBLANES, LANES), F32)] * 4
        + [pltpu.VMEM((RG_SEGS, RG_CW), F32)] * 2,
        compiler_params=_cparams("parallel", "parallel"),
        name="rglru",
    )(xa, ga, cw, cb, wa, wx, ba, bx, lam, h0)


def _attend_slabs(jobs):
    M = jobs[0][0].shape[0]
    lo = lax.broadcasted_iota(jnp.int32, (M, LANES), 1) < HEAD_DIM
    scores = []
    for q2, srcs, _ in jobs:
        zero = jnp.zeros_like(q2)
        qs = jnp.concatenate([jnp.where(lo, q2, zero), jnp.where(lo, zero, q2)], 0)
        ss = []
        for k, _, bias in srcs:
            s = _dot_nt(qs, k)
            ss.append(s if bias is None else s + bias)
        scores.append(ss)
    maxima = []
    for (_, _, sink_col), ss in zip(jobs, scores):
        m = jnp.max(ss[0], -1, keepdims=True)
        for s in ss[1:]:
            m = jnp.maximum(m, jnp.max(s, -1, keepdims=True))
        maxima.append(m if sink_col is None else jnp.maximum(m, sink_col))
    outs = []
    for (_, srcs, sink_col), ss, m in zip(jobs, scores, maxima):
        den = None
        o = None
        for s, (_, v, _) in zip(ss, srcs):
            e = jnp.exp2(s - m)
            d_ = jnp.sum(e, -1, keepdims=True)
            o_ = _dot(e.astype(BF16), v)
            den = d_ if den is None else den + d_
            o = o_ if o is None else o + o_
        if sink_col is not None:
            den = den + jnp.exp2(sink_col - m)
        o = o / den
        outs.append(jnp.where(lo, o[:M], o[M:]))
    return outs


def _sink_col(sink_ref, l, j, M):
    r = lax.broadcasted_iota(jnp.int32, (2 * M, 1), 0)
    return jnp.where(r < M, sink_ref[l, 2 * j], sink_ref[l, 2 * j + 1]) * LOG2E


def _slab(j):
    return slice(j * LANES, (j + 1) * LANES)


def _attn_ctx_kernel(sink_ref, nq_ref, nk_ref, nv_ref, sq_ref, skd_ref, svd_ref, yb_ref, yc_ref, *, l):
    M = nq_ref.shape[1]
    nslab = W512 // LANES
    jobs = []
    for j in range(nslab):
        k2 = nk_ref[0, 0, :, _slab(j)].astype(BF16)
        v2 = nv_ref[0, 0, :, _slab(j)].astype(BF16)
        jobs.append((nq_ref[0, :, _slab(j)], [(k2, v2, None)], None))
    for j in range(nslab):
        g = j // 2
        jobs.append((sq_ref[0, :, _slab(j)], [(skd_ref[0, :, _slab(g)], svd_ref[0, :, _slab(g)], None)],
                     _sink_col(sink_ref, l, j, M)))
    outs = _attend_slabs(jobs)
    for j in range(nslab):
        yb_ref[0, :, _slab(j)] = outs[j].astype(yb_ref.dtype)
        yc_ref[0, :, _slab(j)] = outs[nslab + j].astype(yc_ref.dtype)


def _attn_ctx_call(sink, nq, nk, nv, sq, skd, svd, l):
    B, T, _ = nq.shape
    s512 = pl.BlockSpec((1, T, W512), lambda b: (b, 0, 0))
    s256 = pl.BlockSpec((1, T, 2 * LANES), lambda b: (b, 0, 0))
    cache = pl.BlockSpec((1, 1, T, W512), lambda b: (b, l, 0, 0))
    return pl.pallas_call(
        functools.partial(_attn_ctx_kernel, l=l),
        out_shape=[jax.ShapeDtypeStruct((B, T, W512), BF16)] * 2,
        grid=(B,),
        in_specs=[pl.BlockSpec(memory_space=pltpu.SMEM), s512, cache, cache, s512, s256, s256],
        out_specs=[s512, s512],
        compiler_params=_cparams("parallel"),
        name="attn_ctx",
    )(sink, nq, nk, nv, sq, skd, svd)


NAT_NLOC = NAT_KR * GRID_W


def _nat_lat_kernel(q_ref, k_ref, v_ref, ck_ref, cv_ref, bias_ref, y_ref, *, rows):
    r = pl.program_id(1)
    rstart = jnp.clip(r - NAT_KR // 2, 0, rows - NAT_KR)
    d = r - rstart
    kbase = pl.multiple_of(rstart * GRID_W, GRID_W)
    jobs = []
    for j in range(W512 // LANES):
        k2 = k_ref[0, pl.ds(kbase, NAT_NLOC), _slab(j)]
        v2 = v_ref[0, pl.ds(kbase, NAT_NLOC), _slab(j)]
        jobs.append((q_ref[0, :, _slab(j)],
                     [(k2, v2, bias_ref[0, j, d]), (ck_ref[0, 0, :, _slab(j)], cv_ref[0, 0, :, _slab(j)], None)], None))
    for j, o in enumerate(_attend_slabs(jobs)):
        y_ref[0, :, _slab(j)] = o.astype(y_ref.dtype)


def _nat_lat_call(q, k, v, ck, cv, bias, l):
    B, T, _ = q.shape
    rows = T // GRID_W
    P = ck.shape[2]
    qs = pl.BlockSpec((1, GRID_W, W512), lambda b, r: (b, r, 0))
    full = pl.BlockSpec((1, T, W512), lambda b, r: (b, 0, 0))
    cs = pl.BlockSpec((1, 1, P, W512), lambda b, r: (b, l, 0, 0))
    return pl.pallas_call(
        functools.partial(_nat_lat_kernel, rows=rows),
        out_shape=jax.ShapeDtypeStruct((B, T, W512), BF16),
        grid=(B, rows),
        in_specs=[qs, full, full, cs, cs, _layer_spec(bias.shape, l)],
        out_specs=qs,
        compiler_params=_cparams("parallel", "arbitrary"),
        name="nat_lat",
    )(q, k, v, ck, cv, bias)


def _swa_band_masks():
    i = np.arange(2 * SWA_BLOCK)[:, None] % SWA_BLOCK
    j = np.arange(SWA_SPAN)[None, :]
    tabs = []
    for span_start in (0, -SWA_WINDOW, -2 * SWA_WINDOW):
        dist = span_start + j - i
        tabs.append(np.where(np.abs(dist) <= SWA_WINDOW, 0.0, NEG_INF))
    return jnp.asarray(np.stack(tabs), F32)


def _swa_lat_kernel(sink_ref, q_ref, kd_ref, vd_ref, ckd_ref, cvd_ref, band_ref, y_ref, *, T, l):
    blk = pl.program_id(1)
    M = SWA_BLOCK
    start = blk * SWA_BLOCK
    ks = pl.multiple_of(jnp.clip(start - SWA_WINDOW, 0, T - SWA_SPAN), SWA_BLOCK)
    bias = band_ref[jnp.where(blk == 0, 0, jnp.where(blk == T // SWA_BLOCK - 1, 2, 1))]
    jobs = []
    for j in range(W512 // LANES):
        g = j // 2
        jobs.append((q_ref[0, :, _slab(j)],
                     [(kd_ref[0, pl.ds(ks, SWA_SPAN), _slab(g)], vd_ref[0, pl.ds(ks, SWA_SPAN), _slab(g)], bias),
                      (ckd_ref[0, 0, :, _slab(g)], cvd_ref[0, 0, :, _slab(g)], None)],
                     _sink_col(sink_ref, l, j, M)))
    for j, o in enumerate(_attend_slabs(jobs)):
        y_ref[0, :, _slab(j)] = o.astype(y_ref.dtype)


def _swa_lat_call(sink, q, kd, vd, ckd, cvd, l):
    B, T, _ = q.shape
    P = ckd.shape[2]
    qs = pl.BlockSpec((1, SWA_BLOCK, W512), lambda b, i: (b, i, 0))
    full = pl.BlockSpec((1, T, 2 * LANES), lambda b, i: (b, 0, 0))
    cs = pl.BlockSpec((1, 1, P, 2 * LANES), lambda b, i: (b, l, 0, 0))
    assert T // SWA_BLOCK >= 3
    band = _swa_band_masks()
    return pl.pallas_call(
        functools.partial(_swa_lat_kernel, T=T, l=l),
        out_shape=jax.ShapeDtypeStruct((B, T, W512), BF16),
        grid=(B, T // SWA_BLOCK),
        in_specs=[pl.BlockSpec(memory_space=pltpu.SMEM), qs, full, full, cs, cs, _const_spec(band.shape)],
        out_specs=qs,
        compiler_params=_cparams("parallel", "arbitrary"),
        name="swa_lat",
    )(sink, q, kd, vd, ckd, cvd, band)


FN_N2 = 256
FN_GP = 2


def _fourier_kernel(x_ref, cc_ref, m_ref, y_ref, *scratch, n1):
    gw = FN_GP * FNET_GC
    cc = cc_ref[...]
    for gp in range(FNET_GROUPS // FN_GP):
        for t1 in range(n1):
            ws = []
            for g in range(FN_GP):
                lo = t1 * FNET_WIDTH + (gp * FN_GP + g) * FNET_GC
                ws.append(_dot(x_ref[0, :, lo:lo + FNET_GC], cc))
            wr = jnp.concatenate([w[:, :FNET_GC] for w in ws], -1)
            wi = jnp.concatenate([w[:, FNET_GC:] for w in ws], -1)
            v = jnp.concatenate([wr, wi], 0).astype(BF16)
            b = _dot(m_ref[t1], v)
            if n1 == 1:
                y_ref[0, :, gp * gw:(gp + 1) * gw] = b.astype(y_ref.dtype)
            else:
                scratch[0][t1] = b
        if n1 > 1:
            _dft16_real(scratch[0], y_ref, gp * gw, gw)


FN_RCH = 16


def _dft16_real(bs, y_ref, col0, gw):
    tw = {m: (math.cos(2 * math.pi * m / 16), math.sin(2 * math.pi * m / 16)) for m in (1, 2, 3, 6, 9)}

    def body(rc, carry):
        r0 = pl.multiple_of(rc * FN_RCH, FN_RCH)
        re = [bs[t1, pl.ds(r0, FN_RCH), :] for t1 in range(16)]
        im = [bs[t1, pl.ds(FN_N2 + r0, FN_RCH), :] for t1 in range(16)]
        h_re = [[None] * 4 for _ in range(4)]
        h_im = [[None] * 4 for _ in range(4)]
        for b in range(4):
            s02r, s02i = re[b] + re[8 + b], im[b] + im[8 + b]
            d02r, d02i = re[b] - re[8 + b], im[b] - im[8 + b]
            s13r, s13i = re[4 + b] + re[12 + b], im[4 + b] + im[12 + b]
            d13r, d13i = re[4 + b] - re[12 + b], im[4 + b] - im[12 + b]
            g = [(s02r + s13r, s02i + s13i), (d02r + d13i, d02i - d13r),
                 (s02r - s13r, s02i - s13i), (d02r - d13i, d02i + d13r)]
            for c in range(4):
                gr, gi = g[c]
                m = b * c
                if m == 0:
                    hr, hi = gr, gi
                elif m == 4:
                    hr, hi = gi, -gr
                else:
                    cs, sn = tw[m]
                    hr = cs * gr + sn * gi
                    hi = (cs * gi - sn * gr) if b % 2 else None
                h_re[b][c], h_im[b][c] = hr, hi
        for c in range(4):
            p = h_re[0][c] + h_re[2][c]
            q = h_re[0][c] - h_re[2][c]
            r = h_re[1][c] + h_re[3][c]
            s = h_im[1][c] - h_im[3][c]
            for d, val in enumerate((p + r, q + s, p - r, q - s)):
                k1 = c + 4 * d
                y_ref[0, pl.ds(k1 * FN_N2 + r0, FN_RCH), col0:col0 + gw] = val.astype(y_ref.dtype)
        return carry

    lax.fori_loop(0, FN_N2 // FN_RCH, body, 0)


def _dft_consts(T):
    n1 = T // FN_N2
    j = np.arange(FNET_GC)
    ang = 2 * np.pi * np.outer(j, j) / FNET_GC
    sc = 1.0 / math.sqrt(FNET_GC)
    cc = np.concatenate([np.cos(ang) * sc, -np.sin(ang) * sc], 1)
    k2 = np.arange(FN_N2)[:, None]
    t2 = np.arange(FN_N2)[None, :]
    st = 1.0 / math.sqrt(T)
    mats = []
    for t1 in range(n1):
        th = 2 * np.pi * ((k2 * (n1 * t2 + t1)) % T) / T
        c, s = np.cos(th) * st, np.sin(th) * st
        top = np.concatenate([c, s], 1)
        mats.append(top if n1 == 1 else np.concatenate([top, np.concatenate([-s, c], 1)], 0))
    return n1, jnp.asarray(cc, BF16), jnp.asarray(np.stack(mats), BF16)


def _fourier_call(xv, T):
    B = xv.shape[0]
    n1, cc, mats = _dft_consts(T)
    assert n1 in (1, 16)
    gw = FN_GP * FNET_GC
    scratch = [pltpu.VMEM((n1, 2 * FN_N2, gw), F32)] if n1 > 1 else []
    return pl.pallas_call(
        functools.partial(_fourier_kernel, n1=n1),
        out_shape=jax.ShapeDtypeStruct((B, T, FNET_WIDTH), BF16),
        grid=(B,),
        in_specs=[pl.BlockSpec((1, FN_N2, n1 * FNET_WIDTH), lambda b: (b, 0, 0)),
                  _const_spec(cc.shape), _const_spec(mats.shape)],
        out_specs=pl.BlockSpec((1, T, FNET_WIDTH), lambda b: (b, 0, 0)),
        scratch_shapes=scratch,
        compiler_params=_cparams("parallel"),
        name="fourier",
    )(xv, cc, mats)


def _route(scores, biased):
    one, zero = jnp.float32(1.0), jnp.float32(0.0)
    in2, gscore = [], []
    for g in range(N_EXPERT_GROUPS):
        vs = [biased[g * EXPERTS_PER_GROUP + j:g * EXPERTS_PER_GROUP + j + 1] for j in range(EXPERTS_PER_GROUP)]
        gs = None
        for j in range(EXPERTS_PER_GROUP):
            rank = None
            for i in range(EXPERTS_PER_GROUP):
                if i == j:
                    continue
                beats = (vs[i] >= vs[j]) if i < j else (vs[i] > vs[j])
                t = jnp.where(beats, one, zero)
                rank = t if rank is None else rank + t
            keep = jnp.where(rank < 2.0, one, zero)
            in2.append(keep)
            t = keep * vs[j]
            gs = t if gs is None else gs + t
        gscore.append(gs)
    rows, sel = [], []
    for g in range(N_EXPERT_GROUPS):
        lost = None
        for i in range(N_EXPERT_GROUPS):
            if i == g:
                continue
            beats = (gscore[i] >= gscore[g]) if i < g else (gscore[i] > gscore[g])
            t = jnp.where(beats, one, zero)
            lost = t if lost is None else lost + t
        gsel = jnp.where(lost < 1.0, one, zero)
        for j in range(EXPERTS_PER_GROUP):
            e = g * EXPERTS_PER_GROUP + j
            sel.append(gsel * in2[e])
            rows.append(sel[-1] * scores[e:e + 1])
    tot = rows[0]
    for rr in rows[1:]:
        tot = tot + rr
    return jnp.concatenate(rows, 0) / tot, jnp.concatenate(sel, 0)


HP_ROWS = D_MODEL // 2 // LANES
Y_ROWS = D_MODEL // LANES


def _pack_halves(v):
    w = v.shape[1] // 2
    lo = lax.bitcast_convert_type(v[:, :w].astype(F32), jnp.uint32)
    hi = lax.bitcast_convert_type(v[:, w:].astype(F32), jnp.uint32)
    return (lo >> 16) | hi


def _unpack_halves(p):
    lo = lax.bitcast_convert_type(p << 16, F32).astype(BF16)
    hi = lax.bitcast_convert_type(p & jnp.uint32(0xFFFF0000), F32).astype(BF16)
    return lo, hi


def _merge_kernel(x_ref, mod_ref, ya_ref, yb_ref, yc_ref, yd_ref, wg_ref, wb_ref, wo_ref,
                  lng_ref, lnb_ref, wr_ref, rb_ref, x1_ref, hp_ref, rt_ref, *, nsub, l):
    m = mod_ref[0, 0]
    sub = x_ref.shape[0] // nsub
    y_refs = (ya_ref, yb_ref, yc_ref, yd_ref)
    st = [dict() for _ in range(nsub)]

    def rows(k):
        return slice(k * sub, (k + 1) * sub)

    def norm_in(k):
        st[k]['x'] = x_ref[rows(k), :]
        st[k]['h'] = (_ln(st[k]['x']) * (1.0 + m[1:2]) + m[0:1]).astype(BF16)
        st[k]['merged'] = None

    def branch(k, b):
        gate = 0.5 * jnp.tanh(_dot(st[k]['h'], wg_ref[0, :, b * D_MODEL:(b + 1) * D_MODEL])) + 0.5
        term = gate * _dot(y_refs[b][rows(k), :], wb_ref[0, b])
        st[k]['merged'] = term if st[k]['merged'] is None else st[k]['merged'] + term

    def project(k):
        st[k]['out'] = _dot(st[k]['merged'].astype(BF16), wo_ref[0])

    def norm_out(k):
        x1 = _ln(ALPHA * st[k]['x'] + m[2:3] * st[k]['out']) * lng_ref[l, 0:1, :] + lnb_ref[l, 0:1, :]
        x1_ref[rows(k), :] = x1
        st[k]['h2'] = (_ln(x1) * (1.0 + m[4:5]) + m[3:4]).astype(BF16)

    def pack(k):
        packed = _pack_halves(st[k]['h2'])
        for c in range(HP_ROWS):
            hp_ref[pl.ds(k * sub * HP_ROWS + c, sub, stride=HP_ROWS), :] = packed[:, c * LANES:(c + 1) * LANES]

    def route(k):
        logits = _dot_nt(wr_ref[...], st[k]['h2'])
        e = jnp.exp(logits - jnp.max(logits, 0, keepdims=True))
        scores = e / jnp.sum(e, 0, keepdims=True)
        comb, sel = _route(scores, scores + rb_ref[...])
        rt_ref[:, rows(k)] = jnp.concatenate([comb, sel], 0)

    tail = (norm_out, pack, route)
    for k in range(nsub):
        norm_in(k)
    for k in range(nsub):
        for b in range(N_BRANCH):
            branch(k, b)
            if k > 0 and b < len(tail):
                tail[b](k - 1)
        project(k)
    for stage in tail:
        stage(nsub - 1)


def _merge_call(x, mod, per_seq, ys, wg, wb, wo, l, lng, lnb, wr_t, rb, T, TM):
    N = x.shape[0]
    tok = lambda w: pl.BlockSpec((TM, w), lambda i: (i, 0))
    return pl.pallas_call(
        functools.partial(_merge_kernel, nsub=max(1, TM // SUB_TM), l=l),
        out_shape=[jax.ShapeDtypeStruct((N, D_MODEL), F32), jax.ShapeDtypeStruct((N * HP_ROWS, LANES), jnp.uint32),
                   jax.ShapeDtypeStruct((2 * N_EXPERTS, N), F32)],
        grid=(N // TM,),
        in_specs=[tok(D_MODEL), _mod_spec(l, per_seq, TM, T),
                  tok(W512), tok(W512), tok(W512), tok(W512),
                  _layer_spec(wg.shape, l), _layer_spec(wb.shape, l), _layer_spec(wo.shape, l),
                  _const_spec(lng.shape), _const_spec(lnb.shape), _const_spec(wr_t.shape), _const_spec(rb.shape)],
        out_specs=[tok(D_MODEL), pl.BlockSpec((TM * HP_ROWS, LANES), lambda i: (i, 0)),
                   pl.BlockSpec((2 * N_EXPERTS, TM), lambda i: (0, i))],
        compiler_params=_cparams("parallel"),
        name="merge",
    )(x, mod, *ys, wg, wb, wo, lng, lnb, wr_t, rb)


MOE_TM = 2048
MOE_RB = 256
MOE_KC = 4
MOE_CH = MOE_TM // MOE_KC
MOE_SLOTS = 2 * MOE_TM + N_EXPERTS * SUBLANES + MOE_RB
MOE_UNROLL = 8
MOE_EPS = 2
MOE_ESTEPS = N_EXPERTS // MOE_EPS


def _route_kernel(rt_ref, u_ref, pos_ref, wts_ref, seg_ref):
    comb = rt_ref[0:N_EXPERTS, :]
    sel = rt_ref[N_EXPERTS:2 * N_EXPERTS, :]
    nb = MOE_TM // LANES
    stacked = jnp.concatenate([sel[:, b * LANES:(b + 1) * LANES] for b in range(nb)], 0)
    within = _dot(stacked.astype(BF16), u_ref[...])
    tot = jnp.sum(stacked, -1, keepdims=True)
    base = jnp.zeros((N_EXPERTS, 1), F32)
    bases = []
    for b in range(nb):
        bases.append(base)
        base = base + tot[b * N_EXPERTS:(b + 1) * N_EXPERTS]
    cnt = base
    padded = jnp.floor((cnt + (SUBLANES - 1.0)) * (1.0 / SUBLANES)) * SUBLANES
    rowi = lax.broadcasted_iota(jnp.int32, (N_EXPERTS, 1), 0)
    off = jnp.zeros((N_EXPERTS, 1), F32)
    for e in range(N_EXPERTS - 1):
        off = off + jnp.where(rowi > e, padded[e:e + 1, :], 0.0)
    pos_rows, wts_rows = [], []
    for b in range(nb):
        slot = within[b * N_EXPERTS:(b + 1) * N_EXPERTS] + (bases[b] + off)
        seen = jnp.zeros((1, LANES), F32)
        acc = [jnp.zeros((1, LANES), F32) for _ in range(4)]
        for e in range(N_EXPERTS):
            s_e = sel[e:e + 1, b * LANES:(b + 1) * LANES]
            c_e = comb[e:e + 1, b * LANES:(b + 1) * LANES]
            first = jnp.where(seen == 0.0, s_e, 0.0)
            second = jnp.where(seen == 1.0, s_e, 0.0)
            acc[0] = acc[0] + first * slot[e:e + 1]
            acc[1] = acc[1] + second * slot[e:e + 1]
            acc[2] = acc[2] + first * c_e
            acc[3] = acc[3] + second * c_e
            seen = seen + s_e
        pos_rows += acc[0:2]
        wts_rows += acc[2:4]
    pos_ref[...] = (jnp.concatenate(pos_rows, 0) * float(HP_ROWS)).astype(jnp.int32).reshape(2 * MOE_TM)
    wts_ref[...] = jnp.concatenate(wts_rows, 0).reshape(2 * MOE_TM)
    lane = lax.broadcasted_iota(jnp.int32, (N_EXPERTS, LANES), 1)
    diag = lane == lax.broadcasted_iota(jnp.int32, (N_EXPERTS, LANES), 0)
    off_row = jnp.sum(jnp.where(diag, off, 0.0), 0, keepdims=True)
    cnt_row = jnp.sum(jnp.where(diag, cnt, 0.0), 0, keepdims=True)
    seg_ref[0] = jnp.concatenate([off_row, cnt_row, jnp.zeros((SUBLANES - 2, LANES), F32)], 0).astype(jnp.int32)


def _route_call(rt):
    N = rt.shape[1]
    nt = N // MOE_TM
    u = jnp.asarray(np.triu(np.ones((LANES, LANES), np.float32), 1), BF16)
    return pl.pallas_call(
        _route_kernel,
        out_shape=[jax.ShapeDtypeStruct((nt * 2 * MOE_TM,), jnp.int32),
                   jax.ShapeDtypeStruct((nt * 2 * MOE_TM,), F32),
                   jax.ShapeDtypeStruct((nt, SUBLANES, LANES), jnp.int32)],
        grid=(nt,),
        in_specs=[pl.BlockSpec((2 * N_EXPERTS, MOE_TM), lambda i: (0, i)), _const_spec(u.shape)],
        out_specs=[pl.BlockSpec((2 * MOE_TM,), lambda i: (i,)),
                   pl.BlockSpec((2 * MOE_TM,), lambda i: (i,)),
                   pl.BlockSpec((1, SUBLANES, LANES), lambda i: (i, 0, 0))],
        compiler_params=_cparams("parallel"),
        name="route",
    )(rt, u)


def _table_index(it):
    per_block = LANES // MOE_UNROLL
    return (it // per_block) * (2 * LANES) + (it % per_block) * MOE_UNROLL


def _moe_kernel(pos_ref, wts_ref, seg_ref, hp_ref, w1_ref, w3_ref, w2_ref, x1_ref, mod_ref, lng_ref, lnb_ref,
                o_ref, xs, ys, oc, *, l):
    s = pl.program_id(1)
    half = D_MODEL // 2

    @pl.when(s == 0)
    def _dispatch():
        xs[...] = jnp.zeros_like(xs)

        def body(it, carry):
            t0 = pl.multiple_of(it * (MOE_UNROLL * HP_ROWS), MOE_UNROLL * HP_ROWS)
            i0 = _table_index(it)
            for j in range(MOE_UNROLL):
                row = hp_ref[pl.ds(t0 + HP_ROWS * j, HP_ROWS), :]
                xs[pl.ds(pl.multiple_of(pos_ref[i0 + j], HP_ROWS), HP_ROWS), :] = row
                xs[pl.ds(pl.multiple_of(pos_ref[i0 + LANES + j], HP_ROWS), HP_ROWS), :] = row
            return carry

        lax.fori_loop(0, MOE_TM // MOE_UNROLL, body, 0)

    def expert(ee):
        e = s * MOE_EPS + ee
        off = seg_ref[0, 0, e]
        cnt = seg_ref[0, 1, e]
        nblk = (cnt + (MOE_RB // 2 - 1)) // MOE_RB

        def ffn_rows(r0, nrows):
            parts = [_unpack_halves(xs[pl.ds(pl.multiple_of(r0 * HP_ROWS, SUBLANES) + c, nrows, stride=HP_ROWS), :])
                     for c in range(HP_ROWS)]
            lo = jnp.concatenate([p[0] for p in parts], -1)
            hi = jnp.concatenate([p[1] for p in parts], -1)
            a = _dot(lo, w1_ref[0, ee, :half, :]) + _dot(hi, w1_ref[0, ee, half:, :])
            g = _dot(lo, w3_ref[0, ee, :half, :]) + _dot(hi, w3_ref[0, ee, half:, :])
            act = (a * jax.nn.sigmoid(a) * g).astype(BF16)
            y = _dot(act, w2_ref[0, ee])
            y0 = pl.multiple_of(r0 * Y_ROWS, SUBLANES)
            for c in range(Y_ROWS):
                ys[pl.ds(y0 + c, nrows, stride=Y_ROWS), :] = y[:, c * LANES:(c + 1) * LANES]

        def blk(i, carry):
            ffn_rows(pl.multiple_of(off + i * MOE_RB, SUBLANES), MOE_RB)
            return carry

        lax.fori_loop(0, nblk, blk, 0)

        @pl.when(cnt > nblk * MOE_RB)
        def _tail():
            ffn_rows(pl.multiple_of(off + nblk * MOE_RB, SUBLANES), MOE_RB // 2)

    @pl.when(s < MOE_ESTEPS)
    def _experts():
        for ee in range(MOE_EPS):
            expert(ee)

    @pl.when(s >= MOE_ESTEPS)
    def _combine():
        it0 = (s - MOE_ESTEPS) * (MOE_CH // MOE_UNROLL)

        def body(it, carry):
            t0 = pl.multiple_of(it * (MOE_UNROLL * Y_ROWS), MOE_UNROLL * Y_ROWS)
            i0 = _table_index(it0 + it)
            for j in range(MOE_UNROLL):
                p0 = pl.multiple_of(pos_ref[i0 + j] * (Y_ROWS // HP_ROWS), Y_ROWS)
                p1 = pl.multiple_of(pos_ref[i0 + LANES + j] * (Y_ROWS // HP_ROWS), Y_ROWS)
                oc[pl.ds(t0 + Y_ROWS * j, Y_ROWS), :] = (wts_ref[i0 + j] * ys[pl.ds(p0, Y_ROWS), :]
                                                         + wts_ref[i0 + LANES + j] * ys[pl.ds(p1, Y_ROWS), :])
            return carry

        lax.fori_loop(0, MOE_CH // MOE_UNROLL, body, 0)
        moe = jnp.concatenate([oc[pl.ds(c, MOE_CH, stride=Y_ROWS), :] for c in range(Y_ROWS)], -1)
        m = mod_ref[0, 0]
        o_ref[...] = _ln(ALPHA * x1_ref[...] + m[5:6] * moe) * lng_ref[l, 1:2, :] + lnb_ref[l, 1:2, :]


def _moe_call(pos, wts, seg, hp, w1, w3, w2, l, x1, mod, per_seq, lng, lnb, T):
    N = x1.shape[0]
    nt = N // MOE_TM
    flat = pl.BlockSpec((2 * MOE_TM,), lambda i, s: (i,), memory_space=pltpu.SMEM)
    wspec = lambda shp: pl.BlockSpec((1, MOE_EPS) + shp, lambda i, s: (l, jnp.minimum(s, MOE_ESTEPS - 1), 0, 0))
    chunk = pl.BlockSpec((MOE_CH, D_MODEL),
                         lambda i, s: (i * MOE_KC + jnp.clip(s - MOE_ESTEPS, 0, MOE_KC - 1), 0))
    return pl.pallas_call(
        functools.partial(_moe_kernel, l=l),
        out_shape=jax.ShapeDtypeStruct((N, D_MODEL), F32),
        grid=(nt, MOE_ESTEPS + MOE_KC),
        in_specs=[flat, flat,
                  pl.BlockSpec((1, SUBLANES, LANES), lambda i, s: (i, 0, 0), memory_space=pltpu.SMEM),
                  pl.BlockSpec((MOE_TM * HP_ROWS, LANES), lambda i, s: (i, 0), pipeline_mode=pl.Buffered(1)),
                  wspec((D_MODEL, D_EXPERT)), wspec((D_MODEL, D_EXPERT)), wspec((D_EXPERT, D_MODEL)),
                  chunk, _mod_spec(l, per_seq, MOE_TM, T),
                  _const_spec(lng.shape), _const_spec(lnb.shape)],
        out_specs=chunk,
        scratch_shapes=[pltpu.VMEM((MOE_SLOTS * HP_ROWS, LANES), jnp.uint32),
                        pltpu.VMEM((MOE_SLOTS * Y_ROWS, LANES), F32),
                        pltpu.VMEM((MOE_CH * Y_ROWS, LANES), F32)],
        compiler_params=_cparams("parallel", "arbitrary"),
        name="moe",
    )(pos, wts, seg, hp, w1, w3, w2, x1, mod, lng, lnb)


def _rope_tables(T):
    t = jnp.arange(T)
    nf = HEAD_DIM // 4
    inv = ROPE_BASE ** (-jnp.arange(nf, dtype=F32) / nf)
    ar = (t // GRID_W).astype(F32)[:, None] * inv
    ac = (t % GRID_W).astype(F32)[:, None] * inv
    cos = jnp.concatenate([jnp.cos(ar), jnp.cos(ar), jnp.cos(ac), jnp.cos(ac)], -1)
    sin = jnp.concatenate([-jnp.sin(ar), jnp.sin(ar), -jnp.sin(ac), jnp.sin(ac)], -1)
    return jnp.tile(cos, (1, LANES // HEAD_DIM)), jnp.tile(sin, (1, LANES // HEAD_DIM))


def _nat_bias_table(rpb):
    q = np.arange(GRID_W)
    kc = np.arange(GRID_W)
    cstart = np.clip(q - NAT_KC // 2, 0, GRID_W - NAT_KC)
    ok = (kc[None, :] >= cstart[:, None]) & (kc[None, :] < cstart[:, None] + NAT_KC)
    cidx = np.clip(kc[None, :] - q[:, None] + NAT_KC - 1, 0, 2 * NAT_KC - 2)
    onehot = (np.arange(2 * NAT_KC - 1)[:, None, None] == cidx[None]).astype(np.float32)
    t = jnp.einsum('lhrc,cqk->lhqrk', rpb.astype(F32), jnp.asarray(onehot), precision=lax.Precision.HIGHEST)
    t = jnp.where(jnp.asarray(ok)[None, None, :, None, :], t * LOG2E, NEG_INF)
    nl = rpb.shape[0]
    per_d = [t[:, :, :, NAT_KR - 1 - d:2 * NAT_KR - 1 - d, :].reshape(nl, NAT_HEADS // 2, 2 * GRID_W, NAT_KR * GRID_W)
             for d in range(NAT_KR)]
    return jnp.stack(per_d, 2)


def _dup_heads(a):
    a = jnp.broadcast_to(a[..., :, None, :], a.shape[:-1] + (2, a.shape[-1]))
    return a.reshape(a.shape[:-3] + (-1,))


def _block_diag(w):
    eye = jnp.eye(RNN_BLOCKS, dtype=w.dtype)
    return jnp.einsum('dnio,nm->dnimo', w, eye).reshape(2, RNN_WIDTH, RNN_WIDTH)


def kernel(x_prompt, x_sample, c, cache_nat_k, cache_nat_v, cache_swa_k, cache_swa_v, state_rglru, c_ctx,
           w_ada, b_ada, w_in, rg_conv_w, rg_conv_b, rg_wa, rg_ba, rg_wx, rg_bx, rg_lambda, nat_rpb,
           swa_sink, w_branch, w_out, ln_g, ln_b, w_router, router_bias, w1, w3, w2):
    B_c, T_c, _ = x_prompt.shape
    B_l, T_l, _ = x_sample.shape
    P = cache_nat_k.shape[2]

    cv = jnp.concatenate([c_ctx[None], c, jnp.zeros((ADA_ROWS - 1 - B_l, D_MODEL), F32)], 0)
    mod_all = _ada_call(cv, w_ada, b_ada).reshape(DEPTH, ADA_ROWS, 6, D_MODEL)
    mod_all = jnp.pad(mod_all, ((0, 0), (0, 0), (0, SUBLANES - 6), (0, 0)))

    kv_w = SWA_KV_HEADS * HEAD_DIM
    dup_w = 2 * kv_w
    xa0, ga0, nq0, nk0, nv0, sq0, sk0 = (i * W512 for i in range(7))
    sv0 = sk0 + kv_w
    xf0 = sv0 + kv_w
    assert xf0 + W512 == GATE_OFF
    w_in_b = w_in.astype(BF16)
    head = lambda c0, g: w_in_b[:, :, c0 + g * HEAD_DIM:c0 + (g + 1) * HEAD_DIM]
    w_tail_ctx = jnp.concatenate([head(c0, g) for c0 in (sk0, sv0) for g in (0, 0, 1, 1)], -1)
    w_tail_lat = jnp.concatenate([w_in_b[:, :, xf0:GATE_OFF], w_tail_ctx], -1)
    w_gate = (0.5 * w_in[:, :, GATE_OFF:]).astype(BF16)
    w_branch_b = w_branch.astype(BF16)
    w_out_b = w_out.astype(BF16)
    w1_b, w3_b, w2_b = w1.astype(BF16), w3.astype(BF16), w2.astype(BF16)
    wr_t = w_router.T.astype(BF16)
    wa_bd = jnp.stack([_block_diag(0.5 * rg_wa[l]) for l in range(DEPTH)]).astype(BF16)
    wx_bd = jnp.stack([_block_diag(0.5 * rg_wx[l]) for l in range(DEPTH)]).astype(BF16)

    def out(src, off, width, dtype, scale=1.0, rope=False, fourier=False, cache=None):
        return (src, off, width, dtype, scale, rope, fourier, cache)

    plan_ctx = (out(0, xa0, W512, F32), out(0, ga0, W512, F32), out(0, nq0, W512, BF16, Q_SCALE),
                out(0, nk0, W512, F32, cache=0), out(0, nv0, W512, F32, cache=1), out(0, sq0, W512, BF16, Q_SCALE),
                out(0, sk0, kv_w, F32, cache=2), out(0, sv0, kv_w, F32, cache=3),
                out(0, xf0, W512, BF16, fourier=True),
                out(1, 0, dup_w, BF16), out(1, dup_w, dup_w, BF16))
    plan_lat = (out(0, xa0, W512, F32), out(0, ga0, W512, F32), out(0, nq0, W512, BF16, Q_SCALE),
                out(0, nk0, W512, BF16), out(0, nv0, W512, BF16), out(0, sq0, W512, BF16, Q_SCALE, rope=True),
                out(1, 0, W512, BF16, fourier=True),
                out(1, W512, dup_w, BF16, rope=True), out(1, W512 + dup_w, dup_w, BF16))

    rope_tabs = _rope_tables(T_l)
    ck_nat = cache_nat_k.reshape(B_l, DEPTH, P, W512).astype(BF16)
    cv_nat = cache_nat_v.reshape(B_l, DEPTH, P, W512).astype(BF16)
    ckd_swa = _dup_heads(cache_swa_k).astype(BF16)
    cvd_swa = _dup_heads(cache_swa_v).astype(BF16)
    state8 = jnp.pad(state_rglru, ((0, 0), (0, 0), (0, SUBLANES - 2), (0, 0)))
    zero_state = jnp.zeros((B_c, 1, SUBLANES, RNN_WIDTH), F32)
    pad8 = lambda a: jnp.pad(a, ((0, 0), (0, SUBLANES - a.shape[1]), (0, 0)))
    rg_cw8, rg_cb3, rg_lam8 = pad8(rg_conv_w), rg_conv_b[:, None, :], pad8(rg_lambda)
    rg_ba8, rg_bx8 = pad8(0.5 * rg_ba), pad8(0.5 * rg_bx)
    nat_bias = _nat_bias_table(nat_rpb)

    def layer(x, l, ctx_pass, caches=None):
        B, T = (B_c, T_c) if ctx_pass else (B_l, T_l)
        whole = (B * T) % TOKEN_TM == 0 and (T % TOKEN_TM == 0 or TOKEN_TM % T == 0)
        TM = TOKEN_TM if whole else min(T, TOKEN_TM)
        sub_tm = TM // max(1, TM // SUB_TM)
        per_seq = not ctx_pass
        if ctx_pass:
            xa, ga, nq, nk, nv, sq, sk, sv, xf, skd, svd = _pre_call(
                x, mod_all, per_seq, w_in_b, GATE_OFF, w_tail_ctx, l, plan_ctx, B, T, TM, None, caches)
        else:
            xa, ga, nq, nk, nv, sq, xf, skd, svd = _pre_call(
                x, mod_all, per_seq, w_in_b, sk0, w_tail_lat, l, plan_lat, B, T, TM, rope_tabs)
        r3 = lambda a: a.reshape(B, T, a.shape[-1])
        ya, hfin = _rglru_call(r3(xa), r3(ga), rg_cw8, rg_cb3, wa_bd, wx_bd, rg_ba8, rg_bx8, rg_lam8,
                               zero_state if ctx_pass else state8, l, 0 if ctx_pass else l)
        if ctx_pass:
            yb, yc = _attn_ctx_call(swa_sink, r3(nq), nk, nv, r3(sq), r3(skd), r3(svd), l)
        else:
            yb = _nat_lat_call(r3(nq), r3(nk), r3(nv), ck_nat, cv_nat, nat_bias, l)
            yc = _swa_lat_call(swa_sink, r3(sq), r3(skd), r3(svd), ckd_swa, cvd_swa, l)
        yd = _fourier_call(xf, T)
        f2 = lambda a: a.reshape(B * T, a.shape[-1])
        x1, hp, rt = _merge_call(
            x, mod_all, per_seq, (f2(ya), f2(yb), f2(yc), f2(yd)), w_gate, w_branch_b, w_out_b, l,
            ln_g, ln_b, wr_t, jnp.broadcast_to(router_bias[:, None], (N_EXPERTS, sub_tm)), T, TM)
        pos, wts, seg = _route_call(rt)
        x2 = _moe_call(pos, wts, seg, hp, w1_b, w3_b, w2_b, l, x1, mod_all, per_seq, ln_g, ln_b, T)
        new = ((nk, nv, sk, sv), hfin[:, :2]) if ctx_pass else None
        return x2, new

    y = x_prompt.reshape(B_c * T_c, D_MODEL)
    caches = tuple(jnp.zeros((B_c, DEPTH, T_c, w), F32)
                   for w in (W512, W512, SWA_KV_HEADS * HEAD_DIM, SWA_KV_HEADS * HEAD_DIM))
    states = []
    for l in range(DEPTH):
        y, (caches, st) = layer(y, l, True, caches)
        states.append(st)
    y_prompt = y.reshape(B_c, T_c, D_MODEL)
    new_nat_k = caches[0].reshape(B_c, DEPTH, T_c, NAT_HEADS, HEAD_DIM)
    new_nat_v = caches[1].reshape(B_c, DEPTH, T_c, NAT_HEADS, HEAD_DIM)
    new_swa_k = caches[2].reshape(B_c, DEPTH, T_c, SWA_KV_HEADS, HEAD_DIM)
    new_swa_v = caches[3].reshape(B_c, DEPTH, T_c, SWA_KV_HEADS, HEAD_DIM)
    new_state = jnp.stack(states, 1)

    y = x_sample.reshape(B_l * T_l, D_MODEL)
    for l in range(DEPTH):
        y, _ = layer(y, l, False)
    y_sample = y.reshape(B_l, T_l, D_MODEL)
    return (y_prompt, y_sample, new_nat_k, new_nat_v, new_swa_k, new_swa_v, new_state)
```

```python
import functools
import math

import jax
import jax.numpy as jnp
import numpy as np
from jax import lax
from jax.experimental import pallas as pl
from jax.experimental.pallas import tpu as pltpu

F32 = jnp.float32
BF16 = jnp.bfloat16

D_MODEL = 1024
DEPTH = 4
GRID_W = 64
HEAD_DIM = 64
ATTN_SCALE = HEAD_DIM ** -0.5
LOG2E = math.log2(math.e)
Q_SCALE = ATTN_SCALE * LOG2E
NEG_INF = -1e30
LN_EPS = 1e-5
ALPHA = (2 * DEPTH) ** 0.25
ROPE_BASE = 10000.0
RNN_WIDTH = 512
RNN_BLOCKS = 8
CONV_W = 4
RGLRU_C = 8.0
NAT_HEADS = 8
NAT_KR = 8
NAT_KC = 16
SWA_HEADS = 8
SWA_KV_HEADS = 2
SWA_WINDOW = 128
SWA_BLOCK = 128
SWA_SPAN = SWA_BLOCK + 2 * SWA_WINDOW
FNET_GROUPS = 4
FNET_WIDTH = 512
FNET_GC = FNET_WIDTH // FNET_GROUPS
N_BRANCH = 4
N_EXPERTS = 16
N_EXPERT_GROUPS = 4
EXPERTS_PER_GROUP = N_EXPERTS // N_EXPERT_GROUPS
D_EXPERT = 512
W512 = 512
GATE_OFF = 3840

LANES = 128
SUBLANES = 8
VMEM_LIMIT = 56 * 1024 * 1024
TOKEN_TM = 1024
SUB_TM = 512


def _cparams(*sem):
    return pltpu.CompilerParams(dimension_semantics=sem, vmem_limit_bytes=VMEM_LIMIT)


def _const_spec(shape):
    nd = len(shape)
    return pl.BlockSpec(shape, lambda *_: (0,) * nd, pipeline_mode=pl.Buffered(1))


def _layer_spec(shape, l):
    nd = len(shape)
    return pl.BlockSpec((1,) + tuple(shape[1:]), lambda *_: (l,) + (0,) * (nd - 1), pipeline_mode=pl.Buffered(1))


def _mod_spec(l, per_seq, tm, t):
    return pl.BlockSpec((1, 1, SUBLANES, D_MODEL),
                        lambda i, *_: (l, 1 + (i * tm) // t if per_seq else 0, 0, 0))


def _ln(x):
    mu = jnp.mean(x, -1, keepdims=True)
    xc = x - mu
    var = jnp.mean(xc * xc, -1, keepdims=True)
    return xc * lax.rsqrt(var + LN_EPS)


def _dot(a, b):
    return jnp.dot(a, b, preferred_element_type=F32)


def _dot_nt(a, b):
    return lax.dot_general(a, b, (((1,), (1,)), ((), ())), preferred_element_type=F32)


ADA_ROWS = 2 * SUBLANES
ADA_TN = D_MODEL


def _ada_kernel(c_ref, w_ref, b_ref, o_ref):
    cv = c_ref[...]
    s = (cv * jax.nn.sigmoid(cv)).astype(BF16)
    o_ref[0] = _dot(s, w_ref[0].astype(BF16)) + b_ref[0]


def _ada_call(cv, w_ada, b_ada):
    n = w_ada.shape[-1]
    return pl.pallas_call(
        _ada_kernel,
        out_shape=jax.ShapeDtypeStruct((DEPTH, ADA_ROWS, n), F32),
        grid=(DEPTH, n // ADA_TN),
        in_specs=[
            pl.BlockSpec((ADA_ROWS, D_MODEL), lambda l, j: (0, 0)),
            pl.BlockSpec((1, D_MODEL, ADA_TN), lambda l, j: (l, 0, j)),
            pl.BlockSpec((1, 1, ADA_TN), lambda l, j: (l, 0, j)),
        ],
        out_specs=pl.BlockSpec((1, ADA_ROWS, ADA_TN), lambda l, j: (l, 0, j)),
        compiler_params=_cparams("parallel", "parallel"),
        name="ada",
    )(cv, w_ada, b_ada.reshape(DEPTH, 1, n))


def _rope(u, cos, sin):
    lane = lax.broadcasted_iota(jnp.int32, cos.shape, 1)
    first = (lane & 31) < 16
    outs = []
    for j in range(u.shape[1] // LANES):
        s = u[:, j * LANES:(j + 1) * LANES]
        partner = jnp.where(first, pltpu.roll(s, LANES - 16, 1), pltpu.roll(s, 16, 1))
        outs.append(s * cos + partner * sin)
    return outs[0] if len(outs) == 1 else jnp.concatenate(outs, -1)


def _pre_kernel(*refs, plan, rope, n1, nsub, n_alias, seq_t):
    x_ref, mod_ref, wm_ref, wt_ref = refs[:4]
    refs = refs[4:]
    if rope:
        cos_ref, sin_ref = refs[:2]
        refs = refs[2:]
    refs = refs[n_alias:]
    outs = refs[:len(plan)]
    m = mod_ref[0, 0]
    sub = x_ref.shape[0] // nsub
    hs = [(_ln(x_ref[k * sub:(k + 1) * sub, :]) * (1.0 + m[1:2]) + m[0:1]).astype(BF16) for k in range(nsub)]
    for (src, off, width, _, scale, do_rope, fourier, slot), o_ref in zip(plan, outs):
        w_ref = wt_ref if src else wm_ref
        for k in range(nsub):
            rows = slice(k * sub, (k + 1) * sub)
            u = _dot(hs[k], w_ref[0, :, off:off + width])
            if do_rope:
                u = _rope(u, cos_ref[rows, :], sin_ref[rows, :])
            if scale != 1.0:
                u = u * scale
            if fourier and n1 > 1:
                stage = refs[len(plan)]
                for c in range(width // LANES):
                    stage[k, c] = u[:, c * LANES:(c + 1) * LANES]
                r = sub // n1
                for t1 in range(n1):
                    for c in range(width // LANES):
                        lo = t1 * width + c * LANES
                        o_ref[0, k * r:(k + 1) * r, lo:lo + LANES] = (
                            stage[k, c, pl.ds(t1, r, stride=n1), :].astype(o_ref.dtype))
            elif fourier or slot is not None:
                if sub <= seq_t:
                    pos = slice((k * sub) % seq_t, (k * sub) % seq_t + sub)
                    pieces = [((k * sub) // seq_t, pos, u)]
                else:
                    pieces = [(k * (sub // seq_t) + j, slice(None), u[j * seq_t:(j + 1) * seq_t])
                              for j in range(sub // seq_t)]
                for b_, pos, val in pieces:
                    if fourier:
                        o_ref[b_, pos, :] = val.astype(o_ref.dtype)
                    else:
                        o_ref[b_, 0, pos, :] = val.astype(o_ref.dtype)
            else:
                o_ref[rows, :] = u.astype(o_ref.dtype)


def _pre_call(x, mod, per_seq, w_main, n_main, w_tail, l, plan, B, T, TM, rope_tabs, caches=None):
    N = x.shape[0]
    tpb = max(1, T // TM)
    spt = max(1, TM // T)
    seq_t = min(T, TM)
    nsub = max(1, TM // SUB_TM)
    n1 = T // FN_N2
    rope = rope_tabs is not None
    n_tail = w_tail.shape[-1]
    in_specs = [
        pl.BlockSpec((TM, D_MODEL), lambda i: (i, 0)),
        _mod_spec(l, per_seq, TM, T),
        pl.BlockSpec((1, D_MODEL, n_main), lambda i: (l, 0, 0), pipeline_mode=pl.Buffered(1)),
        pl.BlockSpec((1, D_MODEL, n_tail), lambda i: (l, 0, 0), pipeline_mode=pl.Buffered(1)),
    ]
    args = [x, mod, w_main, w_tail]
    if rope:
        in_specs += [pl.BlockSpec((TM, LANES), lambda i: (i % tpb, 0))] * 2
        args += list(rope_tabs)
    out_shape, out_specs, scratch, aliases = [], [], [], {}
    for k, p in enumerate(plan):
        if p[6]:
            out_shape.append(jax.ShapeDtypeStruct((B, FN_N2, n1 * p[2]), p[3]))
            if n1 > 1:
                out_specs.append(pl.BlockSpec((1, TM // n1, n1 * p[2]), lambda i: (i // tpb, i % tpb, 0)))
                scratch.append(pltpu.VMEM((nsub, p[2] // LANES, TM // nsub, LANES), F32))
            else:
                out_specs.append(pl.BlockSpec((spt, seq_t, p[2]), lambda i: (i // tpb, i % tpb, 0)))
        elif p[7] is not None:
            out_shape.append(jax.ShapeDtypeStruct((B, DEPTH, T, p[2]), p[3]))
            out_specs.append(pl.BlockSpec((spt, 1, seq_t, p[2]), lambda i: (i // tpb, l, i % tpb, 0)))
            if caches is not None:
                aliases[len(args)] = k
                in_specs.append(pl.BlockSpec(memory_space=pl.ANY))
                args.append(caches[p[7]])
        else:
            out_shape.append(jax.ShapeDtypeStruct((N, p[2]), p[3]))
            out_specs.append(pl.BlockSpec((TM, p[2]), lambda i: (i, 0)))
    return pl.pallas_call(
        functools.partial(_pre_kernel, plan=plan, rope=rope, n1=n1, nsub=nsub, n_alias=len(aliases), seq_t=seq_t),
        out_shape=out_shape,
        grid=(N // TM,),
        in_specs=in_specs,
        out_specs=out_specs,
        scratch_shapes=scratch,
        input_output_aliases=aliases,
        compiler_params=_cparams("parallel"),
        name="pre",
    )(*args)


RG_CW = 256
RG_TCH = 256
RG_SEGS = SUBLANES


def _rglru_kernel(xa_ref, ga_ref, cw_ref, cb_ref, wa_ref, wx_ref, ba_ref, bx_ref, lam_ref, h0_ref,
                  y_ref, hfin_ref, xpad, a_f, u_f, a_b, u_b, carry_f, carry_b, *, T):
    cw_ = xa_ref.shape[-1]
    zeros8 = jnp.zeros((SUBLANES, cw_), F32)
    for k in range(cw_ // LANES):
        xpad[k, 0:SUBLANES, :] = zeros8[:, :LANES]
        xpad[k, T + SUBLANES:T + 2 * SUBLANES, :] = zeros8[:, :LANES]
        xpad[k, SUBLANES:T + SUBLANES, :] = xa_ref[0, :, k * LANES:(k + 1) * LANES]
    cw = cw_ref[0]
    cb = cb_ref[0]
    lam = lam_ref[0]
    sp = jnp.maximum(-lam, 0.0) + jnp.log1p(jnp.exp(-jnp.abs(lam)))
    sp4 = (0.5 * RGLRU_C) * sp
    nsp4_log2e = sp4 * (-LOG2E)
    ba = ba_ref[0]
    bx = bx_ref[0]
    h0 = h0_ref[0, 0]

    def chunk(c, carry):
        base = pl.multiple_of(c * RG_TCH, RG_TCH)
        xc = cb
        for i in range(CONV_W):
            xc = xc + cw[i:i + 1] * get(xpad, pl.ds(base + (SUBLANES - 1 + i), RG_TCH, stride=1))
        xcb = xc.astype(BF16)
        half_xc = 0.5 * xc
        for d, (a_s, u_s) in enumerate(((a_f, u_f), (a_b, u_b))):
            r2 = jnp.tanh(_dot(xcb, wa_ref[0, d]) + ba[d:d + 1]) + 1.0
            i2 = jnp.tanh(_dot(xcb, wx_ref[0, d]) + bx[d:d + 1]) + 1.0
            a = jnp.exp2(r2 * nsp4_log2e[d:d + 1])
            v = jnp.tanh(r2 * sp4[d:d + 1]) * (1.0 + a * a)
            gain = jnp.where(v > 0.0, v * lax.rsqrt(v), 0.0)
            put(a_s, pl.ds(store_row(c), RG_TCH), a)
            put(u_s, pl.ds(store_row(c), RG_TCH), gain * (i2 * half_xc))
        return carry

    def get(ref, rows):
        return jnp.concatenate([ref[k, rows, :] for k in range(cw_ // LANES)], -1)

    def put(ref, rows, val):
        for k in range(cw_ // LANES):
            ref[k, rows, :] = val[:, k * LANES:(k + 1) * LANES]

    seg_len = T // RG_SEGS
    segmented = seg_len % RG_TCH == 0
    seg_pitch = seg_len + SUBLANES
    per_seg = seg_len // RG_TCH if segmented else 1

    def store_row(c):
        base = c * RG_TCH
        return pl.multiple_of(base + SUBLANES * (c // per_seg), SUBLANES) if segmented else pl.multiple_of(base, RG_TCH)

    lax.fori_loop(0, T // RG_TCH, chunk, 0)

    row = lax.broadcasted_iota(jnp.int32, (SUBLANES, cw_), 0)

    if segmented:
        def seg_step(j, carry):
            h_f, p_f, h_b, p_b = carry
            jb = seg_len - 1 - j
            rows_f = pl.ds(j, RG_SEGS, stride=seg_pitch)
            rows_b = pl.ds(jb, RG_SEGS, stride=seg_pitch)
            a = get(a_f, rows_f)
            h_f = a * h_f + get(u_f, rows_f)
            p_f = p_f * a
            put(u_f, rows_f, h_f)
            put(a_f, rows_f, p_f)
            a = get(a_b, rows_b)
            h_b = a * h_b + get(u_b, rows_b)
            p_b = p_b * a
            put(u_b, rows_b, h_b)
            put(a_b, rows_b, p_b)
            return h_f, p_f, h_b, p_b

        zero = jnp.zeros((RG_SEGS, cw_), F32)
        one = jnp.ones((RG_SEGS, cw_), F32)
        h_f, p_f, h_b, p_b = lax.fori_loop(0, seg_len, seg_step, (zero, one, zero, one), unroll=2)
        cf = [h0[0:1, :]]
        for s_ in range(RG_SEGS):
            cf.append(h_f[s_:s_ + 1, :] + p_f[s_:s_ + 1, :] * cf[s_])
        cb_in = [None] * RG_SEGS + [h0[1:2, :]]
        for s_ in range(RG_SEGS - 1, -1, -1):
            cb_in[s_] = h_b[s_:s_ + 1, :] + p_b[s_:s_ + 1, :] * cb_in[s_ + 1]
        hfin_ref[0] = jnp.where(row == 0, cf[RG_SEGS], jnp.where(row == 1, cb_in[0], 0.0))
        carry_f[...] = jnp.concatenate(cf[:RG_SEGS], 0)
        carry_b[...] = jnp.concatenate(cb_in[1:], 0)

        def emit_seg(c, carry):
            base = pl.multiple_of(c * RG_TCH, RG_TCH)
            s_ = c // per_seg
            rows = pl.ds(store_row(c), RG_TCH)
            h = (get(u_f, rows) + get(a_f, rows) * carry_f[pl.ds(s_, 1), :]
                 + get(u_b, rows) + get(a_b, rows) * carry_b[pl.ds(s_, 1), :])
            y_ref[0, pl.ds(base, RG_TCH), :] = (
                h * jax.nn.gelu(ga_ref[0, pl.ds(base, RG_TCH), :])).astype(y_ref.dtype)
            return carry

        lax.fori_loop(0, T // RG_TCH, emit_seg, 0)
        return

    def block_scan(a, u, reverse):
        for dd in (1, 2, 4):
            sh = SUBLANES - dd if reverse else dd
            a_n = pltpu.roll(a, sh, 0)
            u_n = pltpu.roll(u, sh, 0)
            ok = (row < SUBLANES - dd) if reverse else (row >= dd)
            u = jnp.where(ok, u + a * u_n, u)
            a = jnp.where(ok, a * a_n, a)
        return a, u

    nblk = T // SUBLANES

    def scan(i, carry):
        c_f, c_b = carry
        lo_f = pl.multiple_of(i * SUBLANES, SUBLANES)
        lo_b = pl.multiple_of((nblk - 1 - i) * SUBLANES, SUBLANES)
        a, u = block_scan(get(a_f, pl.ds(lo_f, SUBLANES)), get(u_f, pl.ds(lo_f, SUBLANES)), False)
        h_f = u + a * c_f
        put(u_f, pl.ds(lo_f, SUBLANES), h_f)
        a, u = block_scan(get(a_b, pl.ds(lo_b, SUBLANES)), get(u_b, pl.ds(lo_b, SUBLANES)), True)
        h_b = u + a * c_b
        put(u_b, pl.ds(lo_b, SUBLANES), h_b)
        return (jnp.broadcast_to(h_f[SUBLANES - 1:SUBLANES, :], (SUBLANES, cw_)),
                jnp.broadcast_to(h_b[0:1, :], (SUBLANES, cw_)))

    c_f, c_b = lax.fori_loop(
        0, nblk, scan,
        (jnp.broadcast_to(h0[0:1, :], (SUBLANES, cw_)), jnp.broadcast_to(h0[1:2, :], (SUBLANES, cw_))),
        unroll=2)
    hfin_ref[0] = jnp.where(row == 0, c_f, jnp.where(row == 1, c_b, 0.0))

    def emit(c, carry):
        base = pl.multiple_of(c * RG_TCH, RG_TCH)
        h = get(u_f, pl.ds(base, RG_TCH)) + get(u_b, pl.ds(base, RG_TCH))
        y_ref[0, pl.ds(base, RG_TCH), :] = (h * jax.nn.gelu(ga_ref[0, pl.ds(base, RG_TCH), :])).astype(y_ref.dtype)
        return carry

    lax.fori_loop(0, T // RG_TCH, emit, 0)


def _rglru_call(xa, ga, cw, cb, wa, wx, ba, bx, lam, h0, l, l_state):
    B, T, _ = xa.shape
    cw_ = RNN_WIDTH if T * RNN_WIDTH <= RG_TCH * RG_SEGS * RG_CW else RG_CW
    nj = RNN_WIDTH // cw_
    seq = pl.BlockSpec((1, T, cw_), lambda b, j: (b, 0, j))
    vec8 = pl.BlockSpec((1, SUBLANES, cw_), lambda b, j: (l, 0, j))
    wsp = pl.BlockSpec((1, 2, cw_, cw_), lambda b, j: (l, 0, j, j))
    st = pl.BlockSpec((1, SUBLANES, cw_), lambda b, j: (b, 0, j))
    st_in = pl.BlockSpec((1, 1, SUBLANES, cw_), lambda b, j: (b, l_state, 0, j))
    return pl.pallas_call(
        functools.partial(_rglru_kernel, T=T),
        out_shape=[jax.ShapeDtypeStruct((B, T, RNN_WIDTH), BF16),
                   jax.ShapeDtypeStruct((B, SUBLANES, RNN_WIDTH), F32)],
        grid=(B, nj),
        in_specs=[seq, seq, vec8, pl.BlockSpec((1, 1, cw_), lambda b, j: (l, 0, j)),
                  wsp, wsp, vec8, vec8, vec8, st_in],
        out_specs=[seq, st],
        scratch_shapes=[pltpu.VMEM((cw_ // LANES, T + 2 * SUBLANES, LANES), F32)]
        + [pltpu.VMEM((cw_ // LANES, T + RG_SEGS * SUBLANES, LANES), F32)] * 4
        + [pltpu.VMEM((RG_SEGS, cw_), F32)] * 2,
        compiler_params=_cparams("parallel", "parallel"),
        name="rglru",
    )(xa, ga, cw, cb, wa, wx, ba, bx, lam, h0)


def _attend_slabs(jobs):
    M = jobs[0][0].shape[0]
    lo = lax.broadcasted_iota(jnp.int32, (M, LANES), 1) < HEAD_DIM
    scores = []
    for q2, srcs, _ in jobs:
        zero = jnp.zeros_like(q2)
        qs = jnp.concatenate([jnp.where(lo, q2, zero), jnp.where(lo, zero, q2)], 0)
        ss = []
        for k, _, bias in srcs:
            s = _dot_nt(qs, k)
            ss.append(s if bias is None else s + bias)
        scores.append(ss)
    maxima = []
    for (_, _, sink_col), ss in zip(jobs, scores):
        m = jnp.max(ss[0], -1, keepdims=True)
        for s in ss[1:]:
            m = jnp.maximum(m, jnp.max(s, -1, keepdims=True))
        maxima.append(m if sink_col is None else jnp.maximum(m, sink_col))
    outs = []
    for (_, srcs, sink_col), ss, m in zip(jobs, scores, maxima):
        den = None
        o = None
        for s, (_, v, _) in zip(ss, srcs):
            e = jnp.exp2(s - m)
            d_ = jnp.sum(e, -1, keepdims=True)
            o_ = _dot(e.astype(BF16), v)
            den = d_ if den is None else den + d_
            o = o_ if o is None else o + o_
        if sink_col is not None:
            den = den + jnp.exp2(sink_col - m)
        o = o / den
        outs.append(jnp.where(lo, o[:M], o[M:]))
    return outs


def _sink_col(sink_ref, l, j, M):
    r = lax.broadcasted_iota(jnp.int32, (2 * M, 1), 0)
    return jnp.where(r < M, sink_ref[l, 2 * j], sink_ref[l, 2 * j + 1]) * LOG2E


def _slab(j):
    return slice(j * LANES, (j + 1) * LANES)


def _attn_ctx_kernel(sink_ref, nq_ref, nk_ref, nv_ref, sq_ref, skd_ref, svd_ref, yb_ref, yc_ref, *, l):
    M = nq_ref.shape[1]
    nslab = W512 // LANES
    jobs = []
    for j in range(nslab):
        k2 = nk_ref[0, 0, :, _slab(j)].astype(BF16)
        v2 = nv_ref[0, 0, :, _slab(j)].astype(BF16)
        jobs.append((nq_ref[0, :, _slab(j)], [(k2, v2, None)], None))
    for j in range(nslab):
        g = j // 2
        jobs.append((sq_ref[0, :, _slab(j)], [(skd_ref[0, :, _slab(g)], svd_ref[0, :, _slab(g)], None)],
                     _sink_col(sink_ref, l, j, M)))
    outs = _attend_slabs(jobs)
    for j in range(nslab):
        yb_ref[0, :, _slab(j)] = outs[j].astype(yb_ref.dtype)
        yc_ref[0, :, _slab(j)] = outs[nslab + j].astype(yc_ref.dtype)


def _attn_ctx_call(sink, nq, nk, nv, sq, skd, svd, l):
    B, T, _ = nq.shape
    s512 = pl.BlockSpec((1, T, W512), lambda b: (b, 0, 0))
    s256 = pl.BlockSpec((1, T, 2 * LANES), lambda b: (b, 0, 0))
    cache = pl.BlockSpec((1, 1, T, W512), lambda b: (b, l, 0, 0))
    return pl.pallas_call(
        functools.partial(_attn_ctx_kernel, l=l),
        out_shape=[jax.ShapeDtypeStruct((B, T, W512), BF16)] * 2,
        grid=(B,),
        in_specs=[pl.BlockSpec(memory_space=pltpu.SMEM), s512, cache, cache, s512, s256, s256],
        out_specs=[s512, s512],
        compiler_params=_cparams("parallel"),
        name="attn_ctx",
    )(sink, nq, nk, nv, sq, skd, svd)


NAT_NLOC = NAT_KR * GRID_W


def _nat_lat_kernel(q_ref, k_ref, v_ref, ck_ref, cv_ref, bias_ref, y_ref, *, rows):
    r = pl.program_id(1)
    rstart = jnp.clip(r - NAT_KR // 2, 0, rows - NAT_KR)
    d = r - rstart
    kbase = pl.multiple_of(rstart * GRID_W, GRID_W)
    jobs = []
    for j in range(W512 // LANES):
        k2 = k_ref[0, pl.ds(kbase, NAT_NLOC), _slab(j)]
        v2 = v_ref[0, pl.ds(kbase, NAT_NLOC), _slab(j)]
        jobs.append((q_ref[0, :, _slab(j)],
                     [(k2, v2, bias_ref[0, j, d]), (ck_ref[0, 0, :, _slab(j)], cv_ref[0, 0, :, _slab(j)], None)], None))
    for j, o in enumerate(_attend_slabs(jobs)):
        y_ref[0, :, _slab(j)] = o.astype(y_ref.dtype)


def _nat_lat_call(q, k, v, ck, cv, bias, l):
    B, T, _ = q.shape
    rows = T // GRID_W
    P = ck.shape[2]
    qs = pl.BlockSpec((1, GRID_W, W512), lambda b, r: (b, r, 0))
    full = pl.BlockSpec((1, T, W512), lambda b, r: (b, 0, 0))
    cs = pl.BlockSpec((1, 1, P, W512), lambda b, r: (b, l, 0, 0))
    return pl.pallas_call(
        functools.partial(_nat_lat_kernel, rows=rows),
        out_shape=jax.ShapeDtypeStruct((B, T, W512), BF16),
        grid=(B, rows),
        in_specs=[qs, full, full, cs, cs, _layer_spec(bias.shape, l)],
        out_specs=qs,
        compiler_params=_cparams("parallel", "arbitrary"),
        name="nat_lat",
    )(q, k, v, ck, cv, bias)


def _swa_band_masks():
    i = np.arange(2 * SWA_BLOCK)[:, None] % SWA_BLOCK
    j = np.arange(SWA_SPAN)[None, :]
    tabs = []
    for span_start in (0, -SWA_WINDOW, -2 * SWA_WINDOW):
        dist = span_start + j - i
        tabs.append(np.where(np.abs(dist) <= SWA_WINDOW, 0.0, NEG_INF))
    return jnp.asarray(np.stack(tabs), F32)


def _swa_lat_kernel(sink_ref, q_ref, kd_ref, vd_ref, ckd_ref, cvd_ref, band_ref, y_ref, *, T, l):
    blk = pl.program_id(1)
    M = SWA_BLOCK
    start = blk * SWA_BLOCK
    ks = pl.multiple_of(jnp.clip(start - SWA_WINDOW, 0, T - SWA_SPAN), SWA_BLOCK)
    bias = band_ref[jnp.where(blk == 0, 0, jnp.where(blk == T // SWA_BLOCK - 1, 2, 1))]
    jobs = []
    for j in range(W512 // LANES):
        g = j // 2
        jobs.append((q_ref[0, :, _slab(j)],
                     [(kd_ref[0, pl.ds(ks, SWA_SPAN), _slab(g)], vd_ref[0, pl.ds(ks, SWA_SPAN), _slab(g)], bias),
                      (ckd_ref[0, 0, :, _slab(g)], cvd_ref[0, 0, :, _slab(g)], None)],
                     _sink_col(sink_ref, l, j, M)))
    for j, o in enumerate(_attend_slabs(jobs)):
        y_ref[0, :, _slab(j)] = o.astype(y_ref.dtype)


def _swa_lat_call(sink, q, kd, vd, ckd, cvd, l):
    B, T, _ = q.shape
    P = ckd.shape[2]
    qs = pl.BlockSpec((1, SWA_BLOCK, W512), lambda b, i: (b, i, 0))
    full = pl.BlockSpec((1, T, 2 * LANES), lambda b, i: (b, 0, 0))
    cs = pl.BlockSpec((1, 1, P, 2 * LANES), lambda b, i: (b, l, 0, 0))
    assert T // SWA_BLOCK >= 3
    band = _swa_band_masks()
    return pl.pallas_call(
        functools.partial(_swa_lat_kernel, T=T, l=l),
        out_shape=jax.ShapeDtypeStruct((B, T, W512), BF16),
        grid=(B, T // SWA_BLOCK),
        in_specs=[pl.BlockSpec(memory_space=pltpu.SMEM), qs, full, full, cs, cs, _const_spec(band.shape)],
        out_specs=qs,
        compiler_params=_cparams("parallel", "arbitrary"),
        name="swa_lat",
    )(sink, q, kd, vd, ckd, cvd, band)


FN_N2 = 256
FN_GP = 2


def _fourier_kernel(x_ref, cc_ref, m_ref, y_ref, *scratch, n1):
    gw = FN_GP * FNET_GC
    cc = cc_ref[...]
    for gp in range(FNET_GROUPS // FN_GP):
        for t1 in range(n1):
            ws = []
            for g in range(FN_GP):
                lo = t1 * FNET_WIDTH + (gp * FN_GP + g) * FNET_GC
                ws.append(_dot(x_ref[0, :, lo:lo + FNET_GC], cc))
            wr = jnp.concatenate([w[:, :FNET_GC] for w in ws], -1)
            wi = jnp.concatenate([w[:, FNET_GC:] for w in ws], -1)
            v = jnp.concatenate([wr, wi], 0).astype(BF16)
            b = _dot(m_ref[t1], v)
            if n1 == 1:
                y_ref[0, :, gp * gw:(gp + 1) * gw] = b.astype(y_ref.dtype)
            else:
                scratch[0][t1] = b
        if n1 > 1:
            _dft16_real(scratch[0], y_ref, gp * gw, gw)


FN_RCH = 16


def _dft16_real(bs, y_ref, col0, gw):
    tw = {m: (math.cos(2 * math.pi * m / 16), math.sin(2 * math.pi * m / 16)) for m in (1, 2, 3, 6, 9)}

    def body(rc, carry):
        r0 = pl.multiple_of(rc * FN_RCH, FN_RCH)
        re = [bs[t1, pl.ds(r0, FN_RCH), :] for t1 in range(16)]
        im = [bs[t1, pl.ds(FN_N2 + r0, FN_RCH), :] for t1 in range(16)]
        h_re = [[None] * 4 for _ in range(4)]
        h_im = [[None] * 4 for _ in range(4)]
        for b in range(4):
            s02r, s02i = re[b] + re[8 + b], im[b] + im[8 + b]
            d02r, d02i = re[b] - re[8 + b], im[b] - im[8 + b]
            s13r, s13i = re[4 + b] + re[12 + b], im[4 + b] + im[12 + b]
            d13r, d13i = re[4 + b] - re[12 + b], im[4 + b] - im[12 + b]
            g = [(s02r + s13r, s02i + s13i), (d02r + d13i, d02i - d13r),
                 (s02r - s13r, s02i - s13i), (d02r - d13i, d02i + d13r)]
            for c in range(4):
                gr, gi = g[c]
                m = b * c
                if m == 0:
                    hr, hi = gr, gi
                elif m == 4:
                    hr, hi = gi, -gr
                else:
                    cs, sn = tw[m]
                    hr = cs * gr + sn * gi
                    hi = (cs * gi - sn * gr) if b % 2 else None
                h_re[b][c], h_im[b][c] = hr, hi
        for c in range(4):
            p = h_re[0][c] + h_re[2][c]
            q = h_re[0][c] - h_re[2][c]
            r = h_re[1][c] + h_re[3][c]
            s = h_im[1][c] - h_im[3][c]
            for d, val in enumerate((p + r, q + s, p - r, q - s)):
                k1 = c + 4 * d
                y_ref[0, pl.ds(k1 * FN_N2 + r0, FN_RCH), col0:col0 + gw] = val.astype(y_ref.dtype)
        return carry

    lax.fori_loop(0, FN_N2 // FN_RCH, body, 0)


def _dft_consts(T):
    n1 = T // FN_N2
    j = np.arange(FNET_GC)
    ang = 2 * np.pi * np.outer(j, j) / FNET_GC
    sc = 1.0 / math.sqrt(FNET_GC)
    cc = np.concatenate([np.cos(ang) * sc, -np.sin(ang) * sc], 1)
    k2 = np.arange(FN_N2)[:, None]
    t2 = np.arange(FN_N2)[None, :]
    st = 1.0 / math.sqrt(T)
    mats = []
    for t1 in range(n1):
        th = 2 * np.pi * ((k2 * (n1 * t2 + t1)) % T) / T
        c, s = np.cos(th) * st, np.sin(th) * st
        top = np.concatenate([c, s], 1)
        mats.append(top if n1 == 1 else np.concatenate([top, np.concatenate([-s, c], 1)], 0))
    return n1, jnp.asarray(cc, BF16), jnp.asarray(np.stack(mats), BF16)


def _fourier_call(xv, T):
    B = xv.shape[0]
    n1, cc, mats = _dft_consts(T)
    assert n1 in (1, 16)
    gw = FN_GP * FNET_GC
    scratch = [pltpu.VMEM((n1, 2 * FN_N2, gw), F32)] if n1 > 1 else []
    return pl.pallas_call(
        functools.partial(_fourier_kernel, n1=n1),
        out_shape=jax.ShapeDtypeStruct((B, T, FNET_WIDTH), BF16),
        grid=(B,),
        in_specs=[pl.BlockSpec((1, FN_N2, n1 * FNET_WIDTH), lambda b: (b, 0, 0)),
                  _const_spec(cc.shape), _const_spec(mats.shape)],
        out_specs=pl.BlockSpec((1, T, FNET_WIDTH), lambda b: (b, 0, 0)),
        scratch_shapes=scratch,
        compiler_params=_cparams("parallel"),
        name="fourier",
    )(xv, cc, mats)


def _route(scores, biased):
    one, zero = jnp.float32(1.0), jnp.float32(0.0)
    in2, gscore = [], []
    for g in range(N_EXPERT_GROUPS):
        vs = [biased[g * EXPERTS_PER_GROUP + j:g * EXPERTS_PER_GROUP + j + 1] for j in range(EXPERTS_PER_GROUP)]
        gs = None
        for j in range(EXPERTS_PER_GROUP):
            rank = None
            for i in range(EXPERTS_PER_GROUP):
                if i == j:
                    continue
                beats = (vs[i] >= vs[j]) if i < j else (vs[i] > vs[j])
                t = jnp.where(beats, one, zero)
                rank = t if rank is None else rank + t
            keep = jnp.where(rank < 2.0, one, zero)
            in2.append(keep)
            t = keep * vs[j]
            gs = t if gs is None else gs + t
        gscore.append(gs)
    rows, sel = [], []
    for g in range(N_EXPERT_GROUPS):
        lost = None
        for i in range(N_EXPERT_GROUPS):
            if i == g:
                continue
            beats = (gscore[i] >= gscore[g]) if i < g else (gscore[i] > gscore[g])
            t = jnp.where(beats, one, zero)
            lost = t if lost is None else lost + t
        gsel = jnp.where(lost < 1.0, one, zero)
        for j in range(EXPERTS_PER_GROUP):
            e = g * EXPERTS_PER_GROUP + j
            sel.append(gsel * in2[e])
            rows.append(sel[-1] * scores[e:e + 1])
    tot = rows[0]
    for rr in rows[1:]:
        tot = tot + rr
    return jnp.concatenate(rows, 0) / tot, jnp.concatenate(sel, 0)


HP_ROWS = D_MODEL // 2 // LANES
Y_ROWS = D_MODEL // LANES


def _pack_halves(v):
    w = v.shape[1] // 2
    lo = lax.bitcast_convert_type(v[:, :w].astype(F32), jnp.uint32)
    hi = lax.bitcast_convert_type(v[:, w:].astype(F32), jnp.uint32)
    return (lo >> 16) | hi


def _unpack_halves(p):
    lo = lax.bitcast_convert_type(p << 16, F32).astype(BF16)
    hi = lax.bitcast_convert_type(p & jnp.uint32(0xFFFF0000), F32).astype(BF16)
    return lo, hi


def _merge_kernel(x_ref, mod_ref, ya_ref, yb_ref, yc_ref, yd_ref, wg_ref, wb_ref, wo_ref,
                  lng_ref, lnb_ref, wr_ref, rb_ref, x1_ref, hp_ref, rt_ref, *, nsub, l):
    m = mod_ref[0, 0]
    sub = x_ref.shape[0] // nsub
    y_refs = (ya_ref, yb_ref, yc_ref, yd_ref)
    st = [dict() for _ in range(nsub)]

    def rows(k):
        return slice(k * sub, (k + 1) * sub)

    def norm_in(k):
        st[k]['x'] = x_ref[rows(k), :]
        st[k]['h'] = (_ln(st[k]['x']) * (1.0 + m[1:2]) + m[0:1]).astype(BF16)
        st[k]['merged'] = None

    def branch(k, b):
        gate = 0.5 * jnp.tanh(_dot(st[k]['h'], wg_ref[0, :, b * D_MODEL:(b + 1) * D_MODEL])) + 0.5
        term = gate * _dot(y_refs[b][rows(k), :], wb_ref[0, b])
        st[k]['merged'] = term if st[k]['merged'] is None else st[k]['merged'] + term

    def project(k):
        st[k]['out'] = _dot(st[k]['merged'].astype(BF16), wo_ref[0])

    def norm_out(k):
        x1 = _ln(ALPHA * st[k]['x'] + m[2:3] * st[k]['out']) * lng_ref[l, 0:1, :] + lnb_ref[l, 0:1, :]
        x1_ref[rows(k), :] = x1
        st[k]['h2'] = (_ln(x1) * (1.0 + m[4:5]) + m[3:4]).astype(BF16)

    def pack(k):
        packed = _pack_halves(st[k]['h2'])
        for c in range(HP_ROWS):
            hp_ref[pl.ds(k * sub * HP_ROWS + c, sub, stride=HP_ROWS), :] = packed[:, c * LANES:(c + 1) * LANES]

    def route(k):
        logits = _dot_nt(wr_ref[...], st[k]['h2'])
        e = jnp.exp(logits - jnp.max(logits, 0, keepdims=True))
        scores = e / jnp.sum(e, 0, keepdims=True)
        comb, sel = _route(scores, scores + rb_ref[...])
        rt_ref[:, rows(k)] = jnp.concatenate([comb, sel], 0)

    tail = (norm_out, pack, route)
    for k in range(nsub):
        norm_in(k)
    for k in range(nsub):
        for b in range(N_BRANCH):
            branch(k, b)
            if k > 0 and b < len(tail):
                tail[b](k - 1)
        project(k)
    for stage in tail:
        stage(nsub - 1)


def _merge_call(x, mod, per_seq, ys, wg, wb, wo, l, lng, lnb, wr_t, rb, T, TM):
    N = x.shape[0]
    tok = lambda w: pl.BlockSpec((TM, w), lambda i: (i, 0))
    return pl.pallas_call(
        functools.partial(_merge_kernel, nsub=max(1, TM // SUB_TM), l=l),
        out_shape=[jax.ShapeDtypeStruct((N, D_MODEL), F32), jax.ShapeDtypeStruct((N * HP_ROWS, LANES), jnp.uint32),
                   jax.ShapeDtypeStruct((2 * N_EXPERTS, N), F32)],
        grid=(N // TM,),
        in_specs=[tok(D_MODEL), _mod_spec(l, per_seq, TM, T),
                  tok(W512), tok(W512), tok(W512), tok(W512),
                  _layer_spec(wg.shape, l), _layer_spec(wb.shape, l), _layer_spec(wo.shape, l),
                  _const_spec(lng.shape), _const_spec(lnb.shape), _const_spec(wr_t.shape), _const_spec(rb.shape)],
        out_specs=[tok(D_MODEL), pl.BlockSpec((TM * HP_ROWS, LANES), lambda i: (i, 0)),
                   pl.BlockSpec((2 * N_EXPERTS, TM), lambda i: (0, i))],
        compiler_params=_cparams("parallel"),
        name="merge",
    )(x, mod, *ys, wg, wb, wo, lng, lnb, wr_t, rb)


MOE_TM = 2048
MOE_RB = 256
MOE_KC = 4
MOE_CH = MOE_TM // MOE_KC
MOE_SLOTS = 2 * MOE_TM + N_EXPERTS * SUBLANES + MOE_RB
MOE_UNROLL = 8
MOE_EPS = 2
MOE_ESTEPS = N_EXPERTS // MOE_EPS


def _route_kernel(rt_ref, u_ref, pos_ref, wts_ref, seg_ref):
    comb = rt_ref[0:N_EXPERTS, :]
    sel = rt_ref[N_EXPERTS:2 * N_EXPERTS, :]
    nb = MOE_TM // LANES
    stacked = jnp.concatenate([sel[:, b * LANES:(b + 1) * LANES] for b in range(nb)], 0)
    within = _dot(stacked.astype(BF16), u_ref[...])
    tot = jnp.sum(stacked, -1, keepdims=True)
    base = jnp.zeros((N_EXPERTS, 1), F32)
    bases = []
    for b in range(nb):
        bases.append(base)
        base = base + tot[b * N_EXPERTS:(b + 1) * N_EXPERTS]
    cnt = base
    padded = jnp.floor((cnt + (SUBLANES - 1.0)) * (1.0 / SUBLANES)) * SUBLANES
    rowi = lax.broadcasted_iota(jnp.int32, (N_EXPERTS, 1), 0)
    off = jnp.zeros((N_EXPERTS, 1), F32)
    for e in range(N_EXPERTS - 1):
        off = off + jnp.where(rowi > e, padded[e:e + 1, :], 0.0)
    pos_rows, wts_rows = [], []
    for b in range(nb):
        slot = within[b * N_EXPERTS:(b + 1) * N_EXPERTS] + (bases[b] + off)
        seen = jnp.zeros((1, LANES), F32)
        acc = [jnp.zeros((1, LANES), F32) for _ in range(4)]
        for e in range(N_EXPERTS):
            s_e = sel[e:e + 1, b * LANES:(b + 1) * LANES]
            c_e = comb[e:e + 1, b * LANES:(b + 1) * LANES]
            first = jnp.where(seen == 0.0, s_e, 0.0)
            second = jnp.where(seen == 1.0, s_e, 0.0)
            acc[0] = acc[0] + first * slot[e:e + 1]
            acc[1] = acc[1] + second * slot[e:e + 1]
            acc[2] = acc[2] + first * c_e
            acc[3] = acc[3] + second * c_e
            seen = seen + s_e
        pos_rows += acc[0:2]
        wts_rows += acc[2:4]
    pos_ref[...] = (jnp.concatenate(pos_rows, 0) * float(HP_ROWS)).astype(jnp.int32).reshape(2 * MOE_TM)
    wts_ref[...] = jnp.concatenate(wts_rows, 0).reshape(2 * MOE_TM)
    lane = lax.broadcasted_iota(jnp.int32, (N_EXPERTS, LANES), 1)
    diag = lane == lax.broadcasted_iota(jnp.int32, (N_EXPERTS, LANES), 0)
    off_row = jnp.sum(jnp.where(diag, off, 0.0), 0, keepdims=True)
    cnt_row = jnp.sum(jnp.where(diag, cnt, 0.0), 0, keepdims=True)
    seg_ref[0] = jnp.concatenate([off_row, cnt_row, jnp.zeros((SUBLANES - 2, LANES), F32)], 0).astype(jnp.int32)


def _route_call(rt):
    N = rt.shape[1]
    nt = N // MOE_TM
    u = jnp.asarray(np.triu(np.ones((LANES, LANES), np.float32), 1), BF16)
    return pl.pallas_call(
        _route_kernel,
        out_shape=[jax.ShapeDtypeStruct((nt * 2 * MOE_TM,), jnp.int32),
                   jax.ShapeDtypeStruct((nt * 2 * MOE_TM,), F32),
                   jax.ShapeDtypeStruct((nt, SUBLANES, LANES), jnp.int32)],
        grid=(nt,),
        in_specs=[pl.BlockSpec((2 * N_EXPERTS, MOE_TM), lambda i: (0, i)), _const_spec(u.shape)],
        out_specs=[pl.BlockSpec((2 * MOE_TM,), lambda i: (i,)),
                   pl.BlockSpec((2 * MOE_TM,), lambda i: (i,)),
                   pl.BlockSpec((1, SUBLANES, LANES), lambda i: (i, 0, 0))],
        compiler_params=_cparams("parallel"),
        name="route",
    )(rt, u)


def _table_index(it):
    per_block = LANES // MOE_UNROLL
    return (it // per_block) * (2 * LANES) + (it % per_block) * MOE_UNROLL


def _moe_kernel(pos_ref, wts_ref, seg_ref, hp_ref, w1_ref, w3_ref, w2_ref, x1_ref, mod_ref, lng_ref, lnb_ref,
                o_ref, xs, ys, oc, *, l):
    s = pl.program_id(1)
    half = D_MODEL // 2

    @pl.when(s == 0)
    def _dispatch():
        xs[...] = jnp.zeros_like(xs)

        def body(it, carry):
            t0 = pl.multiple_of(it * (MOE_UNROLL * HP_ROWS), MOE_UNROLL * HP_ROWS)
            i0 = _table_index(it)
            for j in range(MOE_UNROLL):
                row = hp_ref[pl.ds(t0 + HP_ROWS * j, HP_ROWS), :]
                xs[pl.ds(pl.multiple_of(pos_ref[i0 + j], HP_ROWS), HP_ROWS), :] = row
                xs[pl.ds(pl.multiple_of(pos_ref[i0 + LANES + j], HP_ROWS), HP_ROWS), :] = row
            return carry

        lax.fori_loop(0, MOE_TM // MOE_UNROLL, body, 0)

    def expert(ee):
        e = s * MOE_EPS + ee
        off = seg_ref[0, 0, e]
        cnt = seg_ref[0, 1, e]
        nblk = (cnt + (MOE_RB // 2 - 1)) // MOE_RB

        def ffn_rows(r0, nrows):
            parts = [_unpack_halves(xs[pl.ds(pl.multiple_of(r0 * HP_ROWS, SUBLANES) + c, nrows, stride=HP_ROWS), :])
                     for c in range(HP_ROWS)]
            lo = jnp.concatenate([p[0] for p in parts], -1)
            hi = jnp.concatenate([p[1] for p in parts], -1)
            a = _dot(lo, w1_ref[0, ee, :half, :]) + _dot(hi, w1_ref[0, ee, half:, :])
            g = _dot(lo, w3_ref[0, ee, :half, :]) + _dot(hi, w3_ref[0, ee, half:, :])
            act = (a * jax.nn.sigmoid(a) * g).astype(BF16)
            y = _dot(act, w2_ref[0, ee])
            y0 = pl.multiple_of(r0 * Y_ROWS, SUBLANES)
            for c in range(Y_ROWS):
                ys[pl.ds(y0 + c, nrows, stride=Y_ROWS), :] = y[:, c * LANES:(c + 1) * LANES]

        def blk(i, carry):
            ffn_rows(pl.multiple_of(off + i * MOE_RB, SUBLANES), MOE_RB)
            return carry

        lax.fori_loop(0, nblk, blk, 0)

        @pl.when(cnt > nblk * MOE_RB)
        def _tail():
            ffn_rows(pl.multiple_of(off + nblk * MOE_RB, SUBLANES), MOE_RB // 2)

    @pl.when(s < MOE_ESTEPS)
    def _experts():
        for ee in range(MOE_EPS):
            expert(ee)

    @pl.when(s >= MOE_ESTEPS)
    def _combine():
        it0 = (s - MOE_ESTEPS) * (MOE_CH // MOE_UNROLL)

        def body(it, carry):
            t0 = pl.multiple_of(it * (MOE_UNROLL * Y_ROWS), MOE_UNROLL * Y_ROWS)
            i0 = _table_index(it0 + it)
            for j in range(MOE_UNROLL):
                p0 = pl.multiple_of(pos_ref[i0 + j] * (Y_ROWS // HP_ROWS), Y_ROWS)
                p1 = pl.multiple_of(pos_ref[i0 + LANES + j] * (Y_ROWS // HP_ROWS), Y_ROWS)
                oc[pl.ds(t0 + Y_ROWS * j, Y_ROWS), :] = (wts_ref[i0 + j] * ys[pl.ds(p0, Y_ROWS), :]
                                                         + wts_ref[i0 + LANES + j] * ys[pl.ds(p1, Y_ROWS), :])
            return carry

        lax.fori_loop(0, MOE_CH // MOE_UNROLL, body, 0)
        moe = jnp.concatenate([oc[pl.ds(c, MOE_CH, stride=Y_ROWS), :] for c in range(Y_ROWS)], -1)
        m = mod_ref[0, 0]
        o_ref[...] = _ln(ALPHA * x1_ref[...] + m[5:6] * moe) * lng_ref[l, 1:2, :] + lnb_ref[l, 1:2, :]


def _moe_call(pos, wts, seg, hp, w1, w3, w2, l, x1, mod, per_seq, lng, lnb, T):
    N = x1.shape[0]
    nt = N // MOE_TM
    flat = pl.BlockSpec((2 * MOE_TM,), lambda i, s: (i,), memory_space=pltpu.SMEM)
    wspec = lambda shp: pl.BlockSpec((1, MOE_EPS) + shp, lambda i, s: (l, jnp.minimum(s, MOE_ESTEPS - 1), 0, 0))
    chunk = pl.BlockSpec((MOE_CH, D_MODEL),
                         lambda i, s: (i * MOE_KC + jnp.clip(s - MOE_ESTEPS, 0, MOE_KC - 1), 0))
    return pl.pallas_call(
        functools.partial(_moe_kernel, l=l),
        out_shape=jax.ShapeDtypeStruct((N, D_MODEL), F32),
        grid=(nt, MOE_ESTEPS + MOE_KC),
        in_specs=[flat, flat,
                  pl.BlockSpec((1, SUBLANES, LANES), lambda i, s: (i, 0, 0), memory_space=pltpu.SMEM),
                  pl.BlockSpec((MOE_TM * HP_ROWS, LANES), lambda i, s: (i, 0), pipeline_mode=pl.Buffered(1)),
                  wspec((D_MODEL, D_EXPERT)), wspec((D_MODEL, D_EXPERT)), wspec((D_EXPERT, D_MODEL)),
                  chunk, _mod_spec(l, per_seq, MOE_TM, T),
                  _const_spec(lng.shape), _const_spec(lnb.shape)],
        out_specs=chunk,
        scratch_shapes=[pltpu.VMEM((MOE_SLOTS * HP_ROWS, LANES), jnp.uint32),
                        pltpu.VMEM((MOE_SLOTS * Y_ROWS, LANES), F32),
                        pltpu.VMEM((MOE_CH * Y_ROWS, LANES), F32)],
        compiler_params=_cparams("parallel", "arbitrary"),
        name="moe",
    )(pos, wts, seg, hp, w1, w3, w2, x1, mod, lng, lnb)


def _rope_tables(T):
    t = jnp.arange(T)
    nf = HEAD_DIM // 4
    inv = ROPE_BASE ** (-jnp.arange(nf, dtype=F32) / nf)
    ar = (t // GRID_W).astype(F32)[:, None] * inv
    ac = (t % GRID_W).astype(F32)[:, None] * inv
    cos = jnp.concatenate([jnp.cos(ar), jnp.cos(ar), jnp.cos(ac), jnp.cos(ac)], -1)
    sin = jnp.concatenate([-jnp.sin(ar), jnp.sin(ar), -jnp.sin(ac), jnp.sin(ac)], -1)
    return jnp.tile(cos, (1, LANES // HEAD_DIM)), jnp.tile(sin, (1, LANES // HEAD_DIM))


def _nat_bias_table(rpb):
    q = np.arange(GRID_W)
    kc = np.arange(GRID_W)
    cstart = np.clip(q - NAT_KC // 2, 0, GRID_W - NAT_KC)
    ok = (kc[None, :] >= cstart[:, None]) & (kc[None, :] < cstart[:, None] + NAT_KC)
    cidx = np.clip(kc[None, :] - q[:, None] + NAT_KC - 1, 0, 2 * NAT_KC - 2)
    onehot = (np.arange(2 * NAT_KC - 1)[:, None, None] == cidx[None]).astype(np.float32)
    t = jnp.einsum('lhrc,cqk->lhqrk', rpb.astype(F32), jnp.asarray(onehot), precision=lax.Precision.HIGHEST)
    t = jnp.where(jnp.asarray(ok)[None, None, :, None, :], t * LOG2E, NEG_INF)
    nl = rpb.shape[0]
    per_d = [t[:, :, :, NAT_KR - 1 - d:2 * NAT_KR - 1 - d, :].reshape(nl, NAT_HEADS // 2, 2 * GRID_W, NAT_KR * GRID_W)
             for d in range(NAT_KR)]
    return jnp.stack(per_d, 2)


def _dup_heads(a):
    a = jnp.broadcast_to(a[..., :, None, :], a.shape[:-1] + (2, a.shape[-1]))
    return a.reshape(a.shape[:-3] + (-1,))


def _block_diag(w):
    eye = jnp.eye(RNN_BLOCKS, dtype=w.dtype)
    return jnp.einsum('dnio,nm->dnimo', w, eye).reshape(2, RNN_WIDTH, RNN_WIDTH)


def kernel(x_prompt, x_sample, c, cache_nat_k, cache_nat_v, cache_swa_k, cache_swa_v, state_rglru, c_ctx,
           w_ada, b_ada, w_in, rg_conv_w, rg_conv_b, rg_wa, rg_ba, rg_wx, rg_bx, rg_lambda, nat_rpb,
           swa_sink, w_branch, w_out, ln_g, ln_b, w_router, router_bias, w1, w3, w2):
    B_c, T_c, _ = x_prompt.shape
    B_l, T_l, _ = x_sample.shape
    P = cache_nat_k.shape[2]

    cv = jnp.concatenate([c_ctx[None], c, jnp.zeros((ADA_ROWS - 1 - B_l, D_MODEL), F32)], 0)
    mod_all = _ada_call(cv, w_ada, b_ada).reshape(DEPTH, ADA_ROWS, 6, D_MODEL)
    mod_all = jnp.pad(mod_all, ((0, 0), (0, 0), (0, SUBLANES - 6), (0, 0)))

    kv_w = SWA_KV_HEADS * HEAD_DIM
    dup_w = 2 * kv_w
    xa0, ga0, nq0, nk0, nv0, sq0, sk0 = (i * W512 for i in range(7))
    sv0 = sk0 + kv_w
    xf0 = sv0 + kv_w
    assert xf0 + W512 == GATE_OFF
    w_in_b = w_in.astype(BF16)
    head = lambda c0, g: w_in_b[:, :, c0 + g * HEAD_DIM:c0 + (g + 1) * HEAD_DIM]
    w_tail_ctx = jnp.concatenate([head(c0, g) for c0 in (sk0, sv0) for g in (0, 0, 1, 1)], -1)
    w_tail_lat = jnp.concatenate([w_in_b[:, :, xf0:GATE_OFF], w_tail_ctx], -1)
    w_gate = (0.5 * w_in[:, :, GATE_OFF:]).astype(BF16)
    w_branch_b = w_branch.astype(BF16)
    w_out_b = w_out.astype(BF16)
    w1_b, w3_b, w2_b = w1.astype(BF16), w3.astype(BF16), w2.astype(BF16)
    wr_t = w_router.T.astype(BF16)
    wa_bd = jnp.stack([_block_diag(0.5 * rg_wa[l]) for l in range(DEPTH)]).astype(BF16)
    wx_bd = jnp.stack([_block_diag(0.5 * rg_wx[l]) for l in range(DEPTH)]).astype(BF16)

    def out(src, off, width, dtype, scale=1.0, rope=False, fourier=False, cache=None):
        return (src, off, width, dtype, scale, rope, fourier, cache)

    plan_ctx = (out(0, xa0, W512, F32), out(0, ga0, W512, F32), out(0, nq0, W512, BF16, Q_SCALE),
                out(0, nk0, W512, F32, cache=0), out(0, nv0, W512, F32, cache=1), out(0, sq0, W512, BF16, Q_SCALE),
                out(0, sk0, kv_w, F32, cache=2), out(0, sv0, kv_w, F32, cache=3),
                out(0, xf0, W512, BF16, fourier=True),
                out(1, 0, dup_w, BF16), out(1, dup_w, dup_w, BF16))
    plan_lat = (out(0, xa0, W512, F32), out(0, ga0, W512, F32), out(0, nq0, W512, BF16, Q_SCALE),
                out(0, nk0, W512, BF16), out(0, nv0, W512, BF16), out(0, sq0, W512, BF16, Q_SCALE, rope=True),
                out(1, 0, W512, BF16, fourier=True),
                out(1, W512, dup_w, BF16, rope=True), out(1, W512 + dup_w, dup_w, BF16))

    rope_tabs = _rope_tables(T_l)
    ck_nat = cache_nat_k.reshape(B_l, DEPTH, P, W512).astype(BF16)
    cv_nat = cache_nat_v.reshape(B_l, DEPTH, P, W512).astype(BF16)
    ckd_swa = _dup_heads(cache_swa_k).astype(BF16)
    cvd_swa = _dup_heads(cache_swa_v).astype(BF16)
    state8 = jnp.pad(state_rglru, ((0, 0), (0, 0), (0, SUBLANES - 2), (0, 0)))
    zero_state = jnp.zeros((B_c, 1, SUBLANES, RNN_WIDTH), F32)
    pad8 = lambda a: jnp.pad(a, ((0, 0), (0, SUBLANES - a.shape[1]), (0, 0)))
    rg_cw8, rg_cb3, rg_lam8 = pad8(rg_conv_w), rg_conv_b[:, None, :], pad8(rg_lambda)
    rg_ba8, rg_bx8 = pad8(0.5 * rg_ba), pad8(0.5 * rg_bx)
    nat_bias = _nat_bias_table(nat_rpb)

    def layer(x, l, ctx_pass, caches=None):
        B, T = (B_c, T_c) if ctx_pass else (B_l, T_l)
        whole = (B * T) % TOKEN_TM == 0 and (T % TOKEN_TM == 0 or TOKEN_TM % T == 0)
        TM = TOKEN_TM if whole else min(T, TOKEN_TM)
        sub_tm = TM // max(1, TM // SUB_TM)
        per_seq = not ctx_pass
        if ctx_pass:
            xa, ga, nq, nk, nv, sq, sk, sv, xf, skd, svd = _pre_call(
                x, mod_all, per_seq, w_in_b, GATE_OFF, w_tail_ctx, l, plan_ctx, B, T, TM, None, caches)
        else:
            xa, ga, nq, nk, nv, sq, xf, skd, svd = _pre_call(
                x, mod_all, per_seq, w_in_b, sk0, w_tail_lat, l, plan_lat, B, T, TM, rope_tabs)
        r3 = lambda a: a.reshape(B, T, a.shape[-1])
        ya, hfin = _rglru_call(r3(xa), r3(ga), rg_cw8, rg_cb3, wa_bd, wx_bd, rg_ba8, rg_bx8, rg_lam8,
                               zero_state if ctx_pass else state8, l, 0 if ctx_pass else l)
        if ctx_pass:
            yb, yc = _attn_ctx_call(swa_sink, r3(nq), nk, nv, r3(sq), r3(skd), r3(svd), l)
        else:
            yb = _nat_lat_call(r3(nq), r3(nk), r3(nv), ck_nat, cv_nat, nat_bias, l)
            yc = _swa_lat_call(swa_sink, r3(sq), r3(skd), r3(svd), ckd_swa, cvd_swa, l)
        yd = _fourier_call(xf, T)
        f2 = lambda a: a.reshape(B * T, a.shape[-1])
        x1, hp, rt = _merge_call(
            x, mod_all, per_seq, (f2(ya), f2(yb), f2(yc), f2(yd)), w_gate, w_branch_b, w_out_b, l,
            ln_g, ln_b, wr_t, jnp.broadcast_to(router_bias[:, None], (N_EXPERTS, sub_tm)), T, TM)
        pos, wts, seg = _route_call(rt)
        x2 = _moe_call(pos, wts, seg, hp, w1_b, w3_b, w2_b, l, x1, mod_all, per_seq, ln_g, ln_b, T)
        new = ((nk, nv, sk, sv), hfin[:, :2]) if ctx_pass else None
        return x2, new

    y = x_prompt.reshape(B_c * T_c, D_MODEL)
    caches = tuple(jnp.zeros((B_c, DEPTH, T_c, w), F32)
                   for w in (W512, W512, SWA_KV_HEADS * HEAD_DIM, SWA_KV_HEADS * HEAD_DIM))
    states = []
    for l in range(DEPTH):
        y, (caches, st) = layer(y, l, True, caches)
        states.append(st)
    y_prompt = y.reshape(B_c, T_c, D_MODEL)
    new_nat_k = caches[0].reshape(B_c, DEPTH, T_c, NAT_HEADS, HEAD_DIM)
    new_nat_v = caches[1].reshape(B_c, DEPTH, T_c, NAT_HEADS, HEAD_DIM)
    new_swa_k = caches[2].reshape(B_c, DEPTH, T_c, SWA_KV_HEADS, HEAD_DIM)
    new_swa_v = caches[3].reshape(B_c, DEPTH, T_c, SWA_KV_HEADS, HEAD_DIM)
    new_state = jnp.stack(states, 1)

    y = x_sample.reshape(B_l * T_l, D_MODEL)
    for l in range(DEPTH):
        y, _ = layer(y, l, False)
    y_sample = y.reshape(B_l, T_l, D_MODEL)
    return (y_prompt, y_sample, new_nat_k, new_nat_v, new_swa_k, new_swa_v, new_state)
```

```python
import functools
import math

import jax
import jax.numpy as jnp
import numpy as np
from jax import lax
from jax.experimental import pallas as pl
from jax.experimental.pallas import tpu as pltpu

F32 = jnp.float32
BF16 = jnp.bfloat16

D_MODEL = 1024
DEPTH = 4
GRID_W = 64
HEAD_DIM = 64
ATTN_SCALE = HEAD_DIM ** -0.5
LOG2E = math.log2(math.e)
Q_SCALE = ATTN_SCALE * LOG2E
NEG_INF = -1e30
LN_EPS = 1e-5
ALPHA = (2 * DEPTH) ** 0.25
ROPE_BASE = 10000.0
RNN_WIDTH = 512
RNN_BLOCKS = 8
CONV_W = 4
RGLRU_C = 8.0
NAT_HEADS = 8
NAT_KR = 8
NAT_KC = 16
SWA_HEADS = 8
SWA_KV_HEADS = 2
SWA_WINDOW = 128
SWA_BLOCK = 128
SWA_SPAN = SWA_BLOCK + 2 * SWA_WINDOW
FNET_GROUPS = 4
FNET_WIDTH = 512
FNET_GC = FNET_WIDTH // FNET_GROUPS
N_BRANCH = 4
N_EXPERTS = 16
N_EXPERT_GROUPS = 4
EXPERTS_PER_GROUP = N_EXPERTS // N_EXPERT_GROUPS
D_EXPERT = 512
W512 = 512
GATE_OFF = 3840

LANES = 128
SUBLANES = 8
VMEM_LIMIT = 56 * 1024 * 1024
TOKEN_TM = 1024
SUB_TM = 512


def _cparams(*sem):
    return pltpu.CompilerParams(dimension_semantics=sem, vmem_limit_bytes=VMEM_LIMIT)


def _const_spec(shape):
    nd = len(shape)
    return pl.BlockSpec(shape, lambda *_: (0,) * nd, pipeline_mode=pl.Buffered(1))


def _layer_spec(shape, l):
    nd = len(shape)
    return pl.BlockSpec((1,) + tuple(shape[1:]), lambda *_: (l,) + (0,) * (nd - 1), pipeline_mode=pl.Buffered(1))


def _mod_spec(l, per_seq, tm, t):
    return pl.BlockSpec((1, 1, SUBLANES, D_MODEL),
                        lambda i, *_: (l, 1 + (i * tm) // t if per_seq else 0, 0, 0))


def _ln(x):
    mu = jnp.mean(x, -1, keepdims=True)
    xc = x - mu
    var = jnp.mean(xc * xc, -1, keepdims=True)
    return xc * lax.rsqrt(var + LN_EPS)


def _dot(a, b):
    return jnp.dot(a, b, preferred_element_type=F32)


def _dot_nt(a, b):
    return lax.dot_general(a, b, (((1,), (1,)), ((), ())), preferred_element_type=F32)


ADA_ROWS = 2 * SUBLANES
ADA_TN = D_MODEL


def _ada_kernel(c_ref, w_ref, b_ref, o_ref):
    cv = c_ref[...]
    s = (cv * jax.nn.sigmoid(cv)).astype(BF16)
    o_ref[0] = _dot(s, w_ref[0].astype(BF16)) + b_ref[0]


def _ada_call(cv, w_ada, b_ada):
    n = w_ada.shape[-1]
    return pl.pallas_call(
        _ada_kernel,
        out_shape=jax.ShapeDtypeStruct((DEPTH, ADA_ROWS, n), F32),
        grid=(DEPTH, n // ADA_TN),
        in_specs=[
            pl.BlockSpec((ADA_ROWS, D_MODEL), lambda l, j: (0, 0)),
            pl.BlockSpec((1, D_MODEL, ADA_TN), lambda l, j: (l, 0, j)),
            pl.BlockSpec((1, 1, ADA_TN), lambda l, j: (l, 0, j)),
        ],
        out_specs=pl.BlockSpec((1, ADA_ROWS, ADA_TN), lambda l, j: (l, 0, j)),
        compiler_params=_cparams("parallel", "parallel"),
        name="ada",
    )(cv, w_ada, b_ada.reshape(DEPTH, 1, n))


def _rope(u, cos, sin):
    lane = lax.broadcasted_iota(jnp.int32, cos.shape, 1)
    first = (lane & 31) < 16
    outs = []
    for j in range(u.shape[1] // LANES):
        s = u[:, j * LANES:(j + 1) * LANES]
        partner = jnp.where(first, pltpu.roll(s, LANES - 16, 1), pltpu.roll(s, 16, 1))
        outs.append(s * cos + partner * sin)
    return outs[0] if len(outs) == 1 else jnp.concatenate(outs, -1)


def _pre_kernel(*refs, plan, rope, n1, nsub, n_alias, seq_t):
    x_ref, mod_ref, wm_ref, wt_ref = refs[:4]
    refs = refs[4:]
    if rope:
        cos_ref, sin_ref = refs[:2]
        refs = refs[2:]
    refs = refs[n_alias:]
    outs = refs[:len(plan)]
    m = mod_ref[0, 0]
    sub = x_ref.shape[0] // nsub
    hs = [(_ln(x_ref[k * sub:(k + 1) * sub, :]) * (1.0 + m[1:2]) + m[0:1]).astype(BF16) for k in range(nsub)]
    for (src, off, width, _, scale, do_rope, fourier, slot), o_ref in zip(plan, outs):
        w_ref = wt_ref if src else wm_ref
        for k in range(nsub):
            rows = slice(k * sub, (k + 1) * sub)
            u = _dot(hs[k], w_ref[0, :, off:off + width])
            if do_rope:
                u = _rope(u, cos_ref[rows, :], sin_ref[rows, :])
            if scale != 1.0:
                u = u * scale
            if fourier and n1 > 1:
                stage = refs[len(plan)]
                for c in range(width // LANES):
                    stage[k, c] = u[:, c * LANES:(c + 1) * LANES]
                r = sub // n1
                for t1 in range(n1):
                    for c in range(width // LANES):
                        lo = t1 * width + c * LANES
                        o_ref[0, k * r:(k + 1) * r, lo:lo + LANES] = (
                            stage[k, c, pl.ds(t1, r, stride=n1), :].astype(o_ref.dtype))
            elif fourier or slot is not None:
                if sub <= seq_t:
                    pos = slice((k * sub) % seq_t, (k * sub) % seq_t + sub)
                    pieces = [((k * sub) // seq_t, pos, u)]
                else:
                    pieces = [(k * (sub // seq_t) + j, slice(None), u[j * seq_t:(j + 1) * seq_t])
                              for j in range(sub // seq_t)]
                for b_, pos, val in pieces:
                    if fourier:
                        o_ref[b_, pos, :] = val.astype(o_ref.dtype)
                    else:
                        o_ref[b_, 0, pos, :] = val.astype(o_ref.dtype)
            else:
                o_ref[rows, :] = u.astype(o_ref.dtype)


def _pre_call(x, mod, per_seq, w_main, n_main, w_tail, l, plan, B, T, TM, rope_tabs, caches=None):
    N = x.shape[0]
    tpb = max(1, T // TM)
    spt = max(1, TM // T)
    seq_t = min(T, TM)
    nsub = max(1, TM // SUB_TM)
    n1 = T // FN_N2
    rope = rope_tabs is not None
    n_tail = w_tail.shape[-1]
    in_specs = [
        pl.BlockSpec((TM, D_MODEL), lambda i: (i, 0)),
        _mod_spec(l, per_seq, TM, T),
        pl.BlockSpec((1, D_MODEL, n_main), lambda i: (l, 0, 0), pipeline_mode=pl.Buffered(1)),
        pl.BlockSpec((1, D_MODEL, n_tail), lambda i: (l, 0, 0), pipeline_mode=pl.Buffered(1)),
    ]
    args = [x, mod, w_main, w_tail]
    if rope:
        in_specs += [pl.BlockSpec((TM, LANES), lambda i: (i % tpb, 0))] * 2
        args += list(rope_tabs)
    out_shape, out_specs, scratch, aliases = [], [], [], {}
    for k, p in enumerate(plan):
        if p[6]:
            out_shape.append(jax.ShapeDtypeStruct((B, FN_N2, n1 * p[2]), p[3]))
            if n1 > 1:
                out_specs.append(pl.BlockSpec((1, TM // n1, n1 * p[2]), lambda i: (i // tpb, i % tpb, 0)))
                scratch.append(pltpu.VMEM((nsub, p[2] // LANES, TM // nsub, LANES), F32))
            else:
                out_specs.append(pl.BlockSpec((spt, seq_t, p[2]), lambda i: (i // tpb, i % tpb, 0)))
        elif p[7] is not None:
            out_shape.append(jax.ShapeDtypeStruct((B, DEPTH, T, p[2]), p[3]))
            out_specs.append(pl.BlockSpec((spt, 1, seq_t, p[2]), lambda i: (i // tpb, l, i % tpb, 0)))
            if caches is not None:
                aliases[len(args)] = k
                in_specs.append(pl.BlockSpec(memory_space=pl.ANY))
                args.append(caches[p[7]])
        else:
            out_shape.append(jax.ShapeDtypeStruct((N, p[2]), p[3]))
            out_specs.append(pl.BlockSpec((TM, p[2]), lambda i: (i, 0)))
    return pl.pallas_call(
        functools.partial(_pre_kernel, plan=plan, rope=rope, n1=n1, nsub=nsub, n_alias=len(aliases), seq_t=seq_t),
        out_shape=out_shape,
        grid=(N // TM,),
        in_specs=in_specs,
        out_specs=out_specs,
        scratch_shapes=scratch,
        input_output_aliases=aliases,
        compiler_params=_cparams("parallel"),
        name="pre",
    )(*args)


RG_CW = 256
RG_TCH = 256
RG_SEGS = SUBLANES


def _rglru_kernel(xa_ref, ga_ref, cw_ref, cb_ref, wa_ref, wx_ref, ba_ref, bx_ref, lam_ref, h0_ref,
                  y_ref, hfin_ref, xpad, a_f, u_f, a_b, u_b, carry_f, carry_b, *, T):
    cw_ = xa_ref.shape[-1]
    zeros8 = jnp.zeros((SUBLANES, cw_), F32)
    for k in range(cw_ // LANES):
        xpad[k, 0:SUBLANES, :] = zeros8[:, :LANES]
        xpad[k, T + SUBLANES:T + 2 * SUBLANES, :] = zeros8[:, :LANES]
        xpad[k, SUBLANES:T + SUBLANES, :] = xa_ref[0, :, k * LANES:(k + 1) * LANES]
    cw = cw_ref[0]
    cb = cb_ref[0]
    lam = lam_ref[0]
    sp = jnp.maximum(-lam, 0.0) + jnp.log1p(jnp.exp(-jnp.abs(lam)))
    sp4 = (0.5 * RGLRU_C) * sp
    nsp4_log2e = sp4 * (-LOG2E)
    ba = ba_ref[0]
    bx = bx_ref[0]
    h0 = h0_ref[0, 0]

    def chunk(c, carry):
        base = pl.multiple_of(c * RG_TCH, RG_TCH)
        xc = cb
        for i in range(CONV_W):
            xc = xc + cw[i:i + 1] * get(xpad, pl.ds(base + (SUBLANES - 1 + i), RG_TCH, stride=1))
        xcb = xc.astype(BF16)
        half_xc = 0.5 * xc
        for d, (a_s, u_s) in enumerate(((a_f, u_f), (a_b, u_b))):
            r2 = jnp.tanh(_dot(xcb, wa_ref[0, d]) + ba[d:d + 1]) + 1.0
            i2 = jnp.tanh(_dot(xcb, wx_ref[0, d]) + bx[d:d + 1]) + 1.0
            a = jnp.exp2(r2 * nsp4_log2e[d:d + 1])
            v = jnp.tanh(r2 * sp4[d:d + 1]) * (1.0 + a * a)
            gain = jnp.where(v > 0.0, v * lax.rsqrt(v), 0.0)
            put(a_s, pl.ds(store_row(c), RG_TCH), a)
            put(u_s, pl.ds(store_row(c), RG_TCH), gain * (i2 * half_xc))
        return carry

    def get(ref, rows):
        return jnp.concatenate([ref[k, rows, :] for k in range(cw_ // LANES)], -1)

    def put(ref, rows, val):
        for k in range(cw_ // LANES):
            ref[k, rows, :] = val[:, k * LANES:(k + 1) * LANES]

    seg_len = T // RG_SEGS
    segmented = seg_len % RG_TCH == 0
    seg_pitch = seg_len + SUBLANES
    per_seg = seg_len // RG_TCH if segmented else 1

    def store_row(c):
        base = c * RG_TCH
        return pl.multiple_of(base + SUBLANES * (c // per_seg), SUBLANES) if segmented else pl.multiple_of(base, RG_TCH)

    lax.fori_loop(0, T // RG_TCH, chunk, 0)

    row = lax.broadcasted_iota(jnp.int32, (SUBLANES, cw_), 0)

    if segmented:
        def seg_step(j, carry):
            h_f, p_f, h_b, p_b = carry
            jb = seg_len - 1 - j
            rows_f = pl.ds(j, RG_SEGS, stride=seg_pitch)
            rows_b = pl.ds(jb, RG_SEGS, stride=seg_pitch)
            a = get(a_f, rows_f)
            h_f = a * h_f + get(u_f, rows_f)
            p_f = p_f * a
            put(u_f, rows_f, h_f)
            put(a_f, rows_f, p_f)
            a = get(a_b, rows_b)
            h_b = a * h_b + get(u_b, rows_b)
            p_b = p_b * a
            put(u_b, rows_b, h_b)
            put(a_b, rows_b, p_b)
            return h_f, p_f, h_b, p_b

        zero = jnp.zeros((RG_SEGS, cw_), F32)
        one = jnp.ones((RG_SEGS, cw_), F32)
        h_f, p_f, h_b, p_b = lax.fori_loop(0, seg_len, seg_step, (zero, one, zero, one), unroll=2)
        cf = [h0[0:1, :]]
        for s_ in range(RG_SEGS):
            cf.append(h_f[s_:s_ + 1, :] + p_f[s_:s_ + 1, :] * cf[s_])
        cb_in = [None] * RG_SEGS + [h0[1:2, :]]
        for s_ in range(RG_SEGS - 1, -1, -1):
            cb_in[s_] = h_b[s_:s_ + 1, :] + p_b[s_:s_ + 1, :] * cb_in[s_ + 1]
        hfin_ref[0] = jnp.where(row == 0, cf[RG_SEGS], jnp.where(row == 1, cb_in[0], 0.0))
        carry_f[...] = jnp.concatenate(cf[:RG_SEGS], 0)
        carry_b[...] = jnp.concatenate(cb_in[1:], 0)

        def emit_seg(c, carry):
            base = pl.multiple_of(c * RG_TCH, RG_TCH)
            s_ = c // per_seg
            rows = pl.ds(store_row(c), RG_TCH)
            h = (get(u_f, rows) + get(a_f, rows) * carry_f[pl.ds(s_, 1), :]
                 + get(u_b, rows) + get(a_b, rows) * carry_b[pl.ds(s_, 1), :])
            y_ref[0, pl.ds(base, RG_TCH), :] = (
                h * jax.nn.gelu(ga_ref[0, pl.ds(base, RG_TCH), :])).astype(y_ref.dtype)
            return carry

        lax.fori_loop(0, T // RG_TCH, emit_seg, 0)
        return

    def block_scan(a, u, reverse):
        for dd in (1, 2, 4):
            sh = SUBLANES - dd if reverse else dd
            a_n = pltpu.roll(a, sh, 0)
            u_n = pltpu.roll(u, sh, 0)
            ok = (row < SUBLANES - dd) if reverse else (row >= dd)
            u = jnp.where(ok, u + a * u_n, u)
            a = jnp.where(ok, a * a_n, a)
        return a, u

    nblk = T // SUBLANES

    def scan(i, carry):
        c_f, c_b = carry
        lo_f = pl.multiple_of(i * SUBLANES, SUBLANES)
        lo_b = pl.multiple_of((nblk - 1 - i) * SUBLANES, SUBLANES)
        a, u = block_scan(get(a_f, pl.ds(lo_f, SUBLANES)), get(u_f, pl.ds(lo_f, SUBLANES)), False)
        h_f = u + a * c_f
        put(u_f, pl.ds(lo_f, SUBLANES), h_f)
        a, u = block_scan(get(a_b, pl.ds(lo_b, SUBLANES)), get(u_b, pl.ds(lo_b, SUBLANES)), True)
        h_b = u + a * c_b
        put(u_b, pl.ds(lo_b, SUBLANES), h_b)
        return (jnp.broadcast_to(h_f[SUBLANES - 1:SUBLANES, :], (SUBLANES, cw_)),
                jnp.broadcast_to(h_b[0:1, :], (SUBLANES, cw_)))

    c_f, c_b = lax.fori_loop(
        0, nblk, scan,
        (jnp.broadcast_to(h0[0:1, :], (SUBLANES, cw_)), jnp.broadcast_to(h0[1:2, :], (SUBLANES, cw_))),
        unroll=2)
    hfin_ref[0] = jnp.where(row == 0, c_f, jnp.where(row == 1, c_b, 0.0))

    def emit(c, carry):
        base = pl.multiple_of(c * RG_TCH, RG_TCH)
        h = get(u_f, pl.ds(base, RG_TCH)) + get(u_b, pl.ds(base, RG_TCH))
        y_ref[0, pl.ds(base, RG_TCH), :] = (h * jax.nn.gelu(ga_ref[0, pl.ds(base, RG_TCH), :])).astype(y_ref.dtype)
        return carry

    lax.fori_loop(0, T // RG_TCH, emit, 0)


def _rglru_call(xa, ga, cw, cb, wa, wx, ba, bx, lam, h0, l, l_state):
    B, T, _ = xa.shape
    cw_ = RNN_WIDTH if T * RNN_WIDTH <= RG_TCH * RG_SEGS * RG_CW else RG_CW
    nj = RNN_WIDTH // cw_
    seq = pl.BlockSpec((1, T, cw_), lambda b, j: (b, 0, j))
    vec8 = pl.BlockSpec((1, SUBLANES, cw_), lambda b, j: (l, 0, j))
    wsp = pl.BlockSpec((1, 2, cw_, cw_), lambda b, j: (l, 0, j, j))
    st = pl.BlockSpec((1, SUBLANES, cw_), lambda b, j: (b, 0, j))
    st_in = pl.BlockSpec((1, 1, SUBLANES, cw_), lambda b, j: (b, l_state, 0, j))
    return pl.pallas_call(
        functools.partial(_rglru_kernel, T=T),
        out_shape=[jax.ShapeDtypeStruct((B, T, RNN_WIDTH), BF16),
                   jax.ShapeDtypeStruct((B, SUBLANES, RNN_WIDTH), F32)],
        grid=(B, nj),
        in_specs=[seq, seq, vec8, pl.BlockSpec((1, 1, cw_), lambda b, j: (l, 0, j)),
                  wsp, wsp, vec8, vec8, vec8, st_in],
        out_specs=[seq, st],
        scratch_shapes=[pltpu.VMEM((cw_ // LANES, T + 2 * SUBLANES, LANES), F32)]
        + [pltpu.VMEM((cw_ // LANES, T + RG_SEGS * SUBLANES, LANES), F32)] * 4
        + [pltpu.VMEM((RG_SEGS, cw_), F32)] * 2,
        compiler_params=_cparams("parallel", "parallel"),
        name="rglru",
    )(xa, ga, cw, cb, wa, wx, ba, bx, lam, h0)


def _attend_slabs(jobs):
    M = jobs[0][0].shape[0]
    lo = lax.broadcasted_iota(jnp.int32, (M, LANES), 1) < HEAD_DIM
    scores = []
    for q2, srcs, _ in jobs:
        zero = jnp.zeros_like(q2)
        qs = jnp.concatenate([jnp.where(lo, q2, zero), jnp.where(lo, zero, q2)], 0)
        ss = []
        for k, _, bias in srcs:
            s = _dot_nt(qs, k)
            ss.append(s if bias is None else s + bias)
        scores.append(ss)
    maxima = []
    for (_, _, sink_col), ss in zip(jobs, scores):
        m = jnp.max(ss[0], -1, keepdims=True)
        for s in ss[1:]:
            m = jnp.maximum(m, jnp.max(s, -1, keepdims=True))
        maxima.append(m if sink_col is None else jnp.maximum(m, sink_col))
    outs = []
    for (_, srcs, sink_col), ss, m in zip(jobs, scores, maxima):
        den = None
        o = None
        for s, (_, v, _) in zip(ss, srcs):
            e = jnp.exp2(s - m)
            d_ = jnp.sum(e, -1, keepdims=True)
            o_ = _dot(e.astype(BF16), v)
            den = d_ if den is None else den + d_
            o = o_ if o is None else o + o_
        if sink_col is not None:
            den = den + jnp.exp2(sink_col - m)
        o = o / den
        outs.append(jnp.where(lo, o[:M], o[M:]))
    return outs


def _sink_col(sink_ref, l, j, M):
    r = lax.broadcasted_iota(jnp.int32, (2 * M, 1), 0)
    return jnp.where(r < M, sink_ref[l, 2 * j], sink_ref[l, 2 * j + 1]) * LOG2E


def _slab(j):
    return slice(j * LANES, (j + 1) * LANES)


def _attn_ctx_kernel(sink_ref, nq_ref, nk_ref, nv_ref, sq_ref, skd_ref, svd_ref, yb_ref, yc_ref, *, l):
    M = nq_ref.shape[1]
    nslab = W512 // LANES
    jobs = []
    for j in range(nslab):
        k2 = nk_ref[0, 0, :, _slab(j)].astype(BF16)
        v2 = nv_ref[0, 0, :, _slab(j)].astype(BF16)
        jobs.append((nq_ref[0, :, _slab(j)], [(k2, v2, None)], None))
    for j in range(nslab):
        g = j // 2
        jobs.append((sq_ref[0, :, _slab(j)], [(skd_ref[0, :, _slab(g)], svd_ref[0, :, _slab(g)], None)],
                     _sink_col(sink_ref, l, j, M)))
    outs = _attend_slabs(jobs)
    for j in range(nslab):
        yb_ref[0, :, _slab(j)] = outs[j].astype(yb_ref.dtype)
        yc_ref[0, :, _slab(j)] = outs[nslab + j].astype(yc_ref.dtype)


def _attn_ctx_call(sink, nq, nk, nv, sq, skd, svd, l):
    B, T, _ = nq.shape
    s512 = pl.BlockSpec((1, T, W512), lambda b: (b, 0, 0))
    s256 = pl.BlockSpec((1, T, 2 * LANES), lambda b: (b, 0, 0))
    cache = pl.BlockSpec((1, 1, T, W512), lambda b: (b, l, 0, 0))
    return pl.pallas_call(
        functools.partial(_attn_ctx_kernel, l=l),
        out_shape=[jax.ShapeDtypeStruct((B, T, W512), BF16)] * 2,
        grid=(B,),
        in_specs=[pl.BlockSpec(memory_space=pltpu.SMEM), s512, cache, cache, s512, s256, s256],
        out_specs=[s512, s512],
        compiler_params=_cparams("parallel"),
        name="attn_ctx",
    )(sink, nq, nk, nv, sq, skd, svd)


NAT_NLOC = NAT_KR * GRID_W


def _nat_lat_kernel(q_ref, k_ref, v_ref, ck_ref, cv_ref, bias_ref, y_ref, *, rows):
    r = pl.program_id(1)
    rstart = jnp.clip(r - NAT_KR // 2, 0, rows - NAT_KR)
    d = r - rstart
    kbase = pl.multiple_of(rstart * GRID_W, GRID_W)
    jobs = []
    for j in range(W512 // LANES):
        k2 = k_ref[0, pl.ds(kbase, NAT_NLOC), _slab(j)]
        v2 = v_ref[0, pl.ds(kbase, NAT_NLOC), _slab(j)]
        jobs.append((q_ref[0, :, _slab(j)],
                     [(k2, v2, bias_ref[0, j, d]), (ck_ref[0, 0, :, _slab(j)], cv_ref[0, 0, :, _slab(j)], None)], None))
    for j, o in enumerate(_attend_slabs(jobs)):
        y_ref[0, :, _slab(j)] = o.astype(y_ref.dtype)


def _nat_lat_call(q, k, v, ck, cv, bias, l):
    B, T, _ = q.shape
    rows = T // GRID_W
    P = ck.shape[2]
    qs = pl.BlockSpec((1, GRID_W, W512), lambda b, r: (b, r, 0))
    full = pl.BlockSpec((1, T, W512), lambda b, r: (b, 0, 0))
    cs = pl.BlockSpec((1, 1, P, W512), lambda b, r: (b, l, 0, 0))
    return pl.pallas_call(
        functools.partial(_nat_lat_kernel, rows=rows),
        out_shape=jax.ShapeDtypeStruct((B, T, W512), BF16),
        grid=(B, rows),
        in_specs=[qs, full, full, cs, cs, _layer_spec(bias.shape, l)],
        out_specs=qs,
        compiler_params=_cparams("parallel", "arbitrary"),
        name="nat_lat",
    )(q, k, v, ck, cv, bias)


def _swa_band_masks():
    i = np.arange(2 * SWA_BLOCK)[:, None] % SWA_BLOCK
    j = np.arange(SWA_SPAN)[None, :]
    tabs = []
    for span_start in (0, -SWA_WINDOW, -2 * SWA_WINDOW):
        dist = span_start + j - i
        tabs.append(np.where(np.abs(dist) <= SWA_WINDOW, 0.0, NEG_INF))
    return jnp.asarray(np.stack(tabs), F32)


def _swa_lat_kernel(sink_ref, q_ref, kd_ref, vd_ref, ckd_ref, cvd_ref, band_ref, y_ref, *, T, l):
    blk = pl.program_id(1)
    M = SWA_BLOCK
    start = blk * SWA_BLOCK
    ks = pl.multiple_of(jnp.clip(start - SWA_WINDOW, 0, T - SWA_SPAN), SWA_BLOCK)
    bias = band_ref[jnp.where(blk == 0, 0, jnp.where(blk == T // SWA_BLOCK - 1, 2, 1))]
    jobs = []
    for j in range(W512 // LANES):
        g = j // 2
        jobs.append((q_ref[0, :, _slab(j)],
                     [(kd_ref[0, pl.ds(ks, SWA_SPAN), _slab(g)], vd_ref[0, pl.ds(ks, SWA_SPAN), _slab(g)], bias),
                      (ckd_ref[0, 0, :, _slab(g)], cvd_ref[0, 0, :, _slab(g)], None)],
                     _sink_col(sink_ref, l, j, M)))
    for j, o in enumerate(_attend_slabs(jobs)):
        y_ref[0, :, _slab(j)] = o.astype(y_ref.dtype)


def _swa_lat_call(sink, q, kd, vd, ckd, cvd, l):
    B, T, _ = q.shape
    P = ckd.shape[2]
    qs = pl.BlockSpec((1, SWA_BLOCK, W512), lambda b, i: (b, i, 0))
    full = pl.BlockSpec((1, T, 2 * LANES), lambda b, i: (b, 0, 0))
    cs = pl.BlockSpec((1, 1, P, 2 * LANES), lambda b, i: (b, l, 0, 0))
    assert T // SWA_BLOCK >= 3
    band = _swa_band_masks()
    return pl.pallas_call(
        functools.partial(_swa_lat_kernel, T=T, l=l),
        out_shape=jax.ShapeDtypeStruct((B, T, W512), BF16),
        grid=(B, T // SWA_BLOCK),
        in_specs=[pl.BlockSpec(memory_space=pltpu.SMEM), qs, full, full, cs, cs, _const_spec(band.shape)],
        out_specs=qs,
        compiler_params=_cparams("parallel", "arbitrary"),
        name="swa_lat",
    )(sink, q, kd, vd, ckd, cvd, band)


FN_N2 = 256
FN_GP = 2


def _fourier_kernel(x_ref, cc_ref, m_ref, y_ref, *scratch, n1):
    gw = FN_GP * FNET_GC
    cc = cc_ref[...]
    for gp in range(FNET_GROUPS // FN_GP):
        for t1 in range(n1):
            ws = []
            for g in range(FN_GP):
                lo = t1 * FNET_WIDTH + (gp * FN_GP + g) * FNET_GC
                ws.append(_dot(x_ref[0, :, lo:lo + FNET_GC], cc))
            wr = jnp.concatenate([w[:, :FNET_GC] for w in ws], -1)
            wi = jnp.concatenate([w[:, FNET_GC:] for w in ws], -1)
            v = jnp.concatenate([wr, wi], 0).astype(BF16)
            b = _dot(m_ref[t1], v)
            if n1 == 1:
                y_ref[0, :, gp * gw:(gp + 1) * gw] = b.astype(y_ref.dtype)
            else:
                scratch[0][t1] = b
        if n1 > 1:
            _dft16_real(scratch[0], y_ref, gp * gw, gw)


FN_RCH = 16


def _dft16_real(bs, y_ref, col0, gw):
    tw = {m: (math.cos(2 * math.pi * m / 16), math.sin(2 * math.pi * m / 16)) for m in (1, 2, 3, 6, 9)}

    def body(rc, carry):
        r0 = pl.multiple_of(rc * FN_RCH, FN_RCH)
        re = [bs[t1, pl.ds(r0, FN_RCH), :] for t1 in range(16)]
        im = [bs[t1, pl.ds(FN_N2 + r0, FN_RCH), :] for t1 in range(16)]
        h_re = [[None] * 4 for _ in range(4)]
        h_im = [[None] * 4 for _ in range(4)]
        for b in range(4):
            s02r, s02i = re[b] + re[8 + b], im[b] + im[8 + b]
            d02r, d02i = re[b] - re[8 + b], im[b] - im[8 + b]
            s13r, s13i = re[4 + b] + re[12 + b], im[4 + b] + im[12 + b]
            d13r, d13i = re[4 + b] - re[12 + b], im[4 + b] - im[12 + b]
            g = [(s02r + s13r, s02i + s13i), (d02r + d13i, d02i - d13r),
                 (s02r - s13r, s02i - s13i), (d02r - d13i, d02i + d13r)]
            for c in range(4):
                gr, gi = g[c]
                m = b * c
                if m == 0:
                    hr, hi = gr, gi
                elif m == 4:
                    hr, hi = gi, -gr
                else:
                    cs, sn = tw[m]
                    hr = cs * gr + sn * gi
                    hi = (cs * gi - sn * gr) if b % 2 else None
                h_re[b][c], h_im[b][c] = hr, hi
        for c in range(4):
            p = h_re[0][c] + h_re[2][c]
            q = h_re[0][c] - h_re[2][c]
            r = h_re[1][c] + h_re[3][c]
            s = h_im[1][c] - h_im[3][c]
            for d, val in enumerate((p + r, q + s, p - r, q - s)):
                k1 = c + 4 * d
                y_ref[0, pl.ds(k1 * FN_N2 + r0, FN_RCH), col0:col0 + gw] = val.astype(y_ref.dtype)
        return carry

    lax.fori_loop(0, FN_N2 // FN_RCH, body, 0)


def _dft_consts(T):
    n1 = T // FN_N2
    j = np.arange(FNET_GC)
    ang = 2 * np.pi * np.outer(j, j) / FNET_GC
    sc = 1.0 / math.sqrt(FNET_GC)
    cc = np.concatenate([np.cos(ang) * sc, -np.sin(ang) * sc], 1)
    k2 = np.arange(FN_N2)[:, None]
    t2 = np.arange(FN_N2)[None, :]
    st = 1.0 / math.sqrt(T)
    mats = []
    for t1 in range(n1):
        th = 2 * np.pi * ((k2 * (n1 * t2 + t1)) % T) / T
        c, s = np.cos(th) * st, np.sin(th) * st
        top = np.concatenate([c, s], 1)
        mats.append(top if n1 == 1 else np.concatenate([top, np.concatenate([-s, c], 1)], 0))
    return n1, jnp.asarray(cc, BF16), jnp.asarray(np.stack(mats), BF16)


def _fourier_call(xv, T):
    B = xv.shape[0]
    n1, cc, mats = _dft_consts(T)
    assert n1 in (1, 16)
    gw = FN_GP * FNET_GC
    scratch = [pltpu.VMEM((n1, 2 * FN_N2, gw), F32)] if n1 > 1 else []
    return pl.pallas_call(
        functools.partial(_fourier_kernel, n1=n1),
        out_shape=jax.ShapeDtypeStruct((B, T, FNET_WIDTH), BF16),
        grid=(B,),
        in_specs=[pl.BlockSpec((1, FN_N2, n1 * FNET_WIDTH), lambda b: (b, 0, 0)),
                  _const_spec(cc.shape), _const_spec(mats.shape)],
        out_specs=pl.BlockSpec((1, T, FNET_WIDTH), lambda b: (b, 0, 0)),
        scratch_shapes=scratch,
        compiler_params=_cparams("parallel"),
        name="fourier",
    )(xv, cc, mats)


def _route(scores, biased):
    one, zero = jnp.float32(1.0), jnp.float32(0.0)
    in2, gscore = [], []
    for g in range(N_EXPERT_GROUPS):
        vs = [biased[g * EXPERTS_PER_GROUP + j:g * EXPERTS_PER_GROUP + j + 1] for j in range(EXPERTS_PER_GROUP)]
        gs = None
        for j in range(EXPERTS_PER_GROUP):
            rank = None
            for i in range(EXPERTS_PER_GROUP):
                if i == j:
                    continue
                beats = (vs[i] >= vs[j]) if i < j else (vs[i] > vs[j])
                t = jnp.where(beats, one, zero)
                rank = t if rank is None else rank + t
            keep = jnp.where(rank < 2.0, one, zero)
            in2.append(keep)
            t = keep * vs[j]
            gs = t if gs is None else gs + t
        gscore.append(gs)
    rows, sel = [], []
    for g in range(N_EXPERT_GROUPS):
        lost = None
        for i in range(N_EXPERT_GROUPS):
            if i == g:
                continue
            beats = (gscore[i] >= gscore[g]) if i < g else (gscore[i] > gscore[g])
            t = jnp.where(beats, one, zero)
            lost = t if lost is None else lost + t
        gsel = jnp.where(lost < 1.0, one, zero)
        for j in range(EXPERTS_PER_GROUP):
            e = g * EXPERTS_PER_GROUP + j
            sel.append(gsel * in2[e])
            rows.append(sel[-1] * scores[e:e + 1])
    tot = rows[0]
    for rr in rows[1:]:
        tot = tot + rr
    return jnp.concatenate(rows, 0) / tot, jnp.concatenate(sel, 0)


HP_ROWS = D_MODEL // 2 // LANES
Y_ROWS = D_MODEL // LANES


def _pack_halves(v):
    w = v.shape[1] // 2
    lo = lax.bitcast_convert_type(v[:, :w].astype(F32), jnp.uint32)
    hi = lax.bitcast_convert_type(v[:, w:].astype(F32), jnp.uint32)
    return (lo >> 16) | hi


def _unpack_halves(p):
    lo = lax.bitcast_convert_type(p << 16, F32).astype(BF16)
    hi = lax.bitcast_convert_type(p & jnp.uint32(0xFFFF0000), F32).astype(BF16)
    return lo, hi


def _merge_kernel(x_ref, mod_ref, ya_ref, yb_ref, yc_ref, yd_ref, wg_ref, wb_ref, wo_ref,
                  lng_ref, lnb_ref, wr_ref, rb_ref, x1_ref, hp_ref, rt_ref, *, nsub, l):
    m = mod_ref[0, 0]
    sub = x_ref.shape[0] // nsub
    y_refs = (ya_ref, yb_ref, yc_ref, yd_ref)
    st = [dict() for _ in range(nsub)]

    def rows(k):
        return slice(k * sub, (k + 1) * sub)

    def norm_in(k):
        st[k]['x'] = x_ref[rows(k), :]
        st[k]['h'] = (_ln(st[k]['x']) * (1.0 + m[1:2]) + m[0:1]).astype(BF16)
        st[k]['merged'] = None

    def branch(k, b):
        gate = 0.5 * jnp.tanh(_dot(st[k]['h'], wg_ref[0, :, b * D_MODEL:(b + 1) * D_MODEL])) + 0.5
        term = gate * _dot(y_refs[b][rows(k), :], wb_ref[0, b])
        st[k]['merged'] = term if st[k]['merged'] is None else st[k]['merged'] + term

    def project(k):
        st[k]['out'] = _dot(st[k]['merged'].astype(BF16), wo_ref[0])

    def norm_out(k):
        x1 = _ln(ALPHA * st[k]['x'] + m[2:3] * st[k]['out']) * lng_ref[l, 0:1, :] + lnb_ref[l, 0:1, :]
        x1_ref[rows(k), :] = x1
        st[k]['h2'] = (_ln(x1) * (1.0 + m[4:5]) + m[3:4]).astype(BF16)

    def pack(k):
        packed = _pack_halves(st[k]['h2'])
        for c in range(HP_ROWS):
            hp_ref[pl.ds(k * sub * HP_ROWS + c, sub, stride=HP_ROWS), :] = packed[:, c * LANES:(c + 1) * LANES]

    def route(k):
        logits = _dot_nt(wr_ref[...], st[k]['h2'])
        e = jnp.exp(logits - jnp.max(logits, 0, keepdims=True))
        scores = e / jnp.sum(e, 0, keepdims=True)
        comb, sel = _route(scores, scores + rb_ref[...])
        rt_ref[:, rows(k)] = jnp.concatenate([comb, sel], 0)

    tail = (norm_out, pack, route)
    for k in range(nsub):
        norm_in(k)
    for k in range(nsub):
        for b in range(N_BRANCH):
            branch(k, b)
            if k > 0 and b < len(tail):
                tail[b](k - 1)
        project(k)
    for stage in tail:
        stage(nsub - 1)


def _merge_call(x, mod, per_seq, ys, wg, wb, wo, l, lng, lnb, wr_t, rb, T, TM):
    N = x.shape[0]
    tok = lambda w: pl.BlockSpec((TM, w), lambda i: (i, 0))
    return pl.pallas_call(
        functools.partial(_merge_kernel, nsub=max(1, TM // SUB_TM), l=l),
        out_shape=[jax.ShapeDtypeStruct((N, D_MODEL), F32), jax.ShapeDtypeStruct((N * HP_ROWS, LANES), jnp.uint32),
                   jax.ShapeDtypeStruct((2 * N_EXPERTS, N), F32)],
        grid=(N // TM,),
        in_specs=[tok(D_MODEL), _mod_spec(l, per_seq, TM, T),
                  tok(W512), tok(W512), tok(W512), tok(W512),
                  _layer_spec(wg.shape, l), _layer_spec(wb.shape, l), _layer_spec(wo.shape, l),
                  _const_spec(lng.shape), _const_spec(lnb.shape), _const_spec(wr_t.shape), _const_spec(rb.shape)],
        out_specs=[tok(D_MODEL), pl.BlockSpec((TM * HP_ROWS, LANES), lambda i: (i, 0)),
                   pl.BlockSpec((2 * N_EXPERTS, TM), lambda i: (0, i))],
        compiler_params=_cparams("parallel"),
        name="merge",
    )(x, mod, *ys, wg, wb, wo, lng, lnb, wr_t, rb)


MOE_TM = 2048
MOE_RB = 256
MOE_KC = 4
MOE_CH = MOE_TM // MOE_KC
MOE_SLOTS = 2 * MOE_TM + N_EXPERTS * SUBLANES + MOE_RB
MOE_UNROLL = 16
MOE_EPS = 2
MOE_ESTEPS = N_EXPERTS // MOE_EPS


def _route_kernel(rt_ref, u_ref, pos_ref, wts_ref, seg_ref):
    comb = rt_ref[0:N_EXPERTS, :]
    sel = rt_ref[N_EXPERTS:2 * N_EXPERTS, :]
    nb = MOE_TM // LANES
    stacked = jnp.concatenate([sel[:, b * LANES:(b + 1) * LANES] for b in range(nb)], 0)
    within = _dot(stacked.astype(BF16), u_ref[...])
    tot = jnp.sum(stacked, -1, keepdims=True)
    base = jnp.zeros((N_EXPERTS, 1), F32)
    bases = []
    for b in range(nb):
        bases.append(base)
        base = base + tot[b * N_EXPERTS:(b + 1) * N_EXPERTS]
    cnt = base
    padded = jnp.floor((cnt + (SUBLANES - 1.0)) * (1.0 / SUBLANES)) * SUBLANES
    rowi = lax.broadcasted_iota(jnp.int32, (N_EXPERTS, 1), 0)
    off = jnp.zeros((N_EXPERTS, 1), F32)
    for e in range(N_EXPERTS - 1):
        off = off + jnp.where(rowi > e, padded[e:e + 1, :], 0.0)
    pos_rows, wts_rows = [], []
    for b in range(nb):
        slot = within[b * N_EXPERTS:(b + 1) * N_EXPERTS] + (bases[b] + off)
        seen = jnp.zeros((1, LANES), F32)
        acc = [jnp.zeros((1, LANES), F32) for _ in range(4)]
        for e in range(N_EXPERTS):
            s_e = sel[e:e + 1, b * LANES:(b + 1) * LANES]
            c_e = comb[e:e + 1, b * LANES:(b + 1) * LANES]
            first = jnp.where(seen == 0.0, s_e, 0.0)
            second = jnp.where(seen == 1.0, s_e, 0.0)
            acc[0] = acc[0] + first * slot[e:e + 1]
            acc[1] = acc[1] + second * slot[e:e + 1]
            acc[2] = acc[2] + first * c_e
            acc[3] = acc[3] + second * c_e
            seen = seen + s_e
        pos_rows += acc[0:2]
        wts_rows += acc[2:4]
    pos_ref[...] = (jnp.concatenate(pos_rows, 0) * float(HP_ROWS)).astype(jnp.int32).reshape(2 * MOE_TM)
    wts_ref[...] = jnp.concatenate(wts_rows, 0).reshape(2 * MOE_TM)
    lane = lax.broadcasted_iota(jnp.int32, (N_EXPERTS, LANES), 1)
    diag = lane == lax.broadcasted_iota(jnp.int32, (N_EXPERTS, LANES), 0)
    off_row = jnp.sum(jnp.where(diag, off, 0.0), 0, keepdims=True)
    cnt_row = jnp.sum(jnp.where(diag, cnt, 0.0), 0, keepdims=True)
    seg_ref[0] = jnp.concatenate([off_row, cnt_row, jnp.zeros((SUBLANES - 2, LANES), F32)], 0).astype(jnp.int32)


def _route_call(rt):
    N = rt.shape[1]
    nt = N // MOE_TM
    u = jnp.asarray(np.triu(np.ones((LANES, LANES), np.float32), 1), BF16)
    return pl.pallas_call(
        _route_kernel,
        out_shape=[jax.ShapeDtypeStruct((nt * 2 * MOE_TM,), jnp.int32),
                   jax.ShapeDtypeStruct((nt * 2 * MOE_TM,), F32),
                   jax.ShapeDtypeStruct((nt, SUBLANES, LANES), jnp.int32)],
        grid=(nt,),
        in_specs=[pl.BlockSpec((2 * N_EXPERTS, MOE_TM), lambda i: (0, i)), _const_spec(u.shape)],
        out_specs=[pl.BlockSpec((2 * MOE_TM,), lambda i: (i,)),
                   pl.BlockSpec((2 * MOE_TM,), lambda i: (i,)),
                   pl.BlockSpec((1, SUBLANES, LANES), lambda i: (i, 0, 0))],
        compiler_params=_cparams("parallel"),
        name="route",
    )(rt, u)


def _table_index(it):
    per_block = LANES // MOE_UNROLL
    return (it // per_block) * (2 * LANES) + (it % per_block) * MOE_UNROLL


def _moe_kernel(pos_ref, wts_ref, seg_ref, hp_ref, w1_ref, w3_ref, w2_ref, x1_ref, mod_ref, lng_ref, lnb_ref,
                o_ref, xs, ys, oc, *, l):
    s = pl.program_id(1)
    half = D_MODEL // 2

    @pl.when(s == 0)
    def _dispatch():
        xs[...] = jnp.zeros_like(xs)

        def body(it, carry):
            t0 = pl.multiple_of(it * (MOE_UNROLL * HP_ROWS), MOE_UNROLL * HP_ROWS)
            i0 = _table_index(it)
            for j in range(MOE_UNROLL):
                row = hp_ref[pl.ds(t0 + HP_ROWS * j, HP_ROWS), :]
                xs[pl.ds(pl.multiple_of(pos_ref[i0 + j], HP_ROWS), HP_ROWS), :] = row
                xs[pl.ds(pl.multiple_of(pos_ref[i0 + LANES + j], HP_ROWS), HP_ROWS), :] = row
            return carry

        lax.fori_loop(0, MOE_TM // MOE_UNROLL, body, 0)

    def expert(ee):
        e = s * MOE_EPS + ee
        off = seg_ref[0, 0, e]
        cnt = seg_ref[0, 1, e]
        nblk = (cnt + (MOE_RB // 2 - 1)) // MOE_RB

        def ffn_rows(r0, nrows):
            parts = [_unpack_halves(xs[pl.ds(pl.multiple_of(r0 * HP_ROWS, SUBLANES) + c, nrows, stride=HP_ROWS), :])
                     for c in range(HP_ROWS)]
            lo = jnp.concatenate([p[0] for p in parts], -1)
            hi = jnp.concatenate([p[1] for p in parts], -1)
            a = _dot(lo, w1_ref[0, ee, :half, :]) + _dot(hi, w1_ref[0, ee, half:, :])
            g = _dot(lo, w3_ref[0, ee, :half, :]) + _dot(hi, w3_ref[0, ee, half:, :])
            act = (a * jax.nn.sigmoid(a) * g).astype(BF16)
            y = _dot(act, w2_ref[0, ee])
            y0 = pl.multiple_of(r0 * Y_ROWS, SUBLANES)
            for c in range(Y_ROWS):
                ys[pl.ds(y0 + c, nrows, stride=Y_ROWS), :] = y[:, c * LANES:(c + 1) * LANES]

        def blk(i, carry):
            ffn_rows(pl.multiple_of(off + i * MOE_RB, SUBLANES), MOE_RB)
            return carry

        lax.fori_loop(0, nblk, blk, 0)

        @pl.when(cnt > nblk * MOE_RB)
        def _tail():
            ffn_rows(pl.multiple_of(off + nblk * MOE_RB, SUBLANES), MOE_RB // 2)

    @pl.when(s < MOE_ESTEPS)
    def _experts():
        for ee in range(MOE_EPS):
            expert(ee)

    @pl.when(s >= MOE_ESTEPS)
    def _combine():
        it0 = (s - MOE_ESTEPS) * (MOE_CH // MOE_UNROLL)

        def body(it, carry):
            t0 = pl.multiple_of(it * (MOE_UNROLL * Y_ROWS), MOE_UNROLL * Y_ROWS)
            i0 = _table_index(it0 + it)
            for j in range(MOE_UNROLL):
                p0 = pl.multiple_of(pos_ref[i0 + j] * (Y_ROWS // HP_ROWS), Y_ROWS)
                p1 = pl.multiple_of(pos_ref[i0 + LANES + j] * (Y_ROWS // HP_ROWS), Y_ROWS)
                oc[pl.ds(t0 + Y_ROWS * j, Y_ROWS), :] = (wts_ref[i0 + j] * ys[pl.ds(p0, Y_ROWS), :]
                                                         + wts_ref[i0 + LANES + j] * ys[pl.ds(p1, Y_ROWS), :])
            return carry

        lax.fori_loop(0, MOE_CH // MOE_UNROLL, body, 0)
        moe = jnp.concatenate([oc[pl.ds(c, MOE_CH, stride=Y_ROWS), :] for c in range(Y_ROWS)], -1)
        m = mod_ref[0, 0]
        o_ref[...] = _ln(ALPHA * x1_ref[...] + m[5:6] * moe) * lng_ref[l, 1:2, :] + lnb_ref[l, 1:2, :]


def _moe_call(pos, wts, seg, hp, w1, w3, w2, l, x1, mod, per_seq, lng, lnb, T):
    N = x1.shape[0]
    nt = N // MOE_TM
    flat = pl.BlockSpec((2 * MOE_TM,), lambda i, s: (i,), memory_space=pltpu.SMEM)
    wspec = lambda shp: pl.BlockSpec((1, MOE_EPS) + shp, lambda i, s: (l, jnp.minimum(s, MOE_ESTEPS - 1), 0, 0))
    chunk = pl.BlockSpec((MOE_CH, D_MODEL),
                         lambda i, s: (i * MOE_KC + jnp.clip(s - MOE_ESTEPS, 0, MOE_KC - 1), 0))
    return pl.pallas_call(
        functools.partial(_moe_kernel, l=l),
        out_shape=jax.ShapeDtypeStruct((N, D_MODEL), F32),
        grid=(nt, MOE_ESTEPS + MOE_KC),
        in_specs=[flat, flat,
                  pl.BlockSpec((1, SUBLANES, LANES), lambda i, s: (i, 0, 0), memory_space=pltpu.SMEM),
                  pl.BlockSpec((MOE_TM * HP_ROWS, LANES), lambda i, s: (i, 0), pipeline_mode=pl.Buffered(1)),
                  wspec((D_MODEL, D_EXPERT)), wspec((D_MODEL, D_EXPERT)), wspec((D_EXPERT, D_MODEL)),
                  chunk, _mod_spec(l, per_seq, MOE_TM, T),
                  _const_spec(lng.shape), _const_spec(lnb.shape)],
        out_specs=chunk,
        scratch_shapes=[pltpu.VMEM((MOE_SLOTS * HP_ROWS, LANES), jnp.uint32),
                        pltpu.VMEM((MOE_SLOTS * Y_ROWS, LANES), F32),
                        pltpu.VMEM((MOE_CH * Y_ROWS, LANES), F32)],
        compiler_params=_cparams("parallel", "arbitrary"),
        name="moe",
    )(pos, wts, seg, hp, w1, w3, w2, x1, mod, lng, lnb)


def _rope_tables(T):
    t = jnp.arange(T)
    nf = HEAD_DIM // 4
    inv = ROPE_BASE ** (-jnp.arange(nf, dtype=F32) / nf)
    ar = (t // GRID_W).astype(F32)[:, None] * inv
    ac = (t % GRID_W).astype(F32)[:, None] * inv
    cos = jnp.concatenate([jnp.cos(ar), jnp.cos(ar), jnp.cos(ac), jnp.cos(ac)], -1)
    sin = jnp.concatenate([-jnp.sin(ar), jnp.sin(ar), -jnp.sin(ac), jnp.sin(ac)], -1)
    return jnp.tile(cos, (1, LANES // HEAD_DIM)), jnp.tile(sin, (1, LANES // HEAD_DIM))


def _nat_bias_table(rpb):
    q = np.arange(GRID_W)
    kc = np.arange(GRID_W)
    cstart = np.clip(q - NAT_KC // 2, 0, GRID_W - NAT_KC)
    ok = (kc[None, :] >= cstart[:, None]) & (kc[None, :] < cstart[:, None] + NAT_KC)
    cidx = np.clip(kc[None, :] - q[:, None] + NAT_KC - 1, 0, 2 * NAT_KC - 2)
    onehot = (np.arange(2 * NAT_KC - 1)[:, None, None] == cidx[None]).astype(np.float32)
    t = jnp.einsum('lhrc,cqk->lhqrk', rpb.astype(F32), jnp.asarray(onehot), precision=lax.Precision.HIGHEST)
    t = jnp.where(jnp.asarray(ok)[None, None, :, None, :], t * LOG2E, NEG_INF)
    nl = rpb.shape[0]
    per_d = [t[:, :, :, NAT_KR - 1 - d:2 * NAT_KR - 1 - d, :].reshape(nl, NAT_HEADS // 2, 2 * GRID_W, NAT_KR * GRID_W)
             for d in range(NAT_KR)]
    return jnp.stack(per_d, 2)


def _dup_heads(a):
    a = jnp.broadcast_to(a[..., :, None, :], a.shape[:-1] + (2, a.shape[-1]))
    return a.reshape(a.shape[:-3] + (-1,))


def _block_diag(w):
    eye = jnp.eye(RNN_BLOCKS, dtype=w.dtype)
    return jnp.einsum('dnio,nm->dnimo', w, eye).reshape(2, RNN_WIDTH, RNN_WIDTH)


def kernel(x_prompt, x_sample, c, cache_nat_k, cache_nat_v, cache_swa_k, cache_swa_v, state_rglru, c_ctx,
           w_ada, b_ada, w_in, rg_conv_w, rg_conv_b, rg_wa, rg_ba, rg_wx, rg_bx, rg_lambda, nat_rpb,
           swa_sink, w_branch, w_out, ln_g, ln_b, w_router, router_bias, w1, w3, w2):
    B_c, T_c, _ = x_prompt.shape
    B_l, T_l, _ = x_sample.shape
    P = cache_nat_k.shape[2]

    cv = jnp.concatenate([c_ctx[None], c, jnp.zeros((ADA_ROWS - 1 - B_l, D_MODEL), F32)], 0)
    mod_all = _ada_call(cv, w_ada, b_ada).reshape(DEPTH, ADA_ROWS, 6, D_MODEL)
    mod_all = jnp.pad(mod_all, ((0, 0), (0, 0), (0, SUBLANES - 6), (0, 0)))

    kv_w = SWA_KV_HEADS * HEAD_DIM
    dup_w = 2 * kv_w
    xa0, ga0, nq0, nk0, nv0, sq0, sk0 = (i * W512 for i in range(7))
    sv0 = sk0 + kv_w
    xf0 = sv0 + kv_w
    assert xf0 + W512 == GATE_OFF
    w_in_b = w_in.astype(BF16)
    head = lambda c0, g: w_in_b[:, :, c0 + g * HEAD_DIM:c0 + (g + 1) * HEAD_DIM]
    w_tail_ctx = jnp.concatenate([head(c0, g) for c0 in (sk0, sv0) for g in (0, 0, 1, 1)], -1)
    w_tail_lat = jnp.concatenate([w_in_b[:, :, xf0:GATE_OFF], w_tail_ctx], -1)
    w_gate = (0.5 * w_in[:, :, GATE_OFF:]).astype(BF16)
    w_branch_b = w_branch.astype(BF16)
    w_out_b = w_out.astype(BF16)
    w1_b, w3_b, w2_b = w1.astype(BF16), w3.astype(BF16), w2.astype(BF16)
    wr_t = w_router.T.astype(BF16)
    wa_bd = jnp.stack([_block_diag(0.5 * rg_wa[l]) for l in range(DEPTH)]).astype(BF16)
    wx_bd = jnp.stack([_block_diag(0.5 * rg_wx[l]) for l in range(DEPTH)]).astype(BF16)

    def out(src, off, width, dtype, scale=1.0, rope=False, fourier=False, cache=None):
        return (src, off, width, dtype, scale, rope, fourier, cache)

    plan_ctx = (out(0, xa0, W512, F32), out(0, ga0, W512, F32), out(0, nq0, W512, BF16, Q_SCALE),
                out(0, nk0, W512, F32, cache=0), out(0, nv0, W512, F32, cache=1), out(0, sq0, W512, BF16, Q_SCALE),
                out(0, sk0, kv_w, F32, cache=2), out(0, sv0, kv_w, F32, cache=3),
                out(0, xf0, W512, BF16, fourier=True),
                out(1, 0, dup_w, BF16), out(1, dup_w, dup_w, BF16))
    plan_lat = (out(0, xa0, W512, F32), out(0, ga0, W512, F32), out(0, nq0, W512, BF16, Q_SCALE),
                out(0, nk0, W512, BF16), out(0, nv0, W512, BF16), out(0, sq0, W512, BF16, Q_SCALE, rope=True),
                out(1, 0, W512, BF16, fourier=True),
                out(1, W512, dup_w, BF16, rope=True), out(1, W512 + dup_w, dup_w, BF16))

    rope_tabs = _rope_tables(T_l)
    ck_nat = cache_nat_k.reshape(B_l, DEPTH, P, W512).astype(BF16)
    cv_nat = cache_nat_v.reshape(B_l, DEPTH, P, W512).astype(BF16)
    ckd_swa = _dup_heads(cache_swa_k).astype(BF16)
    cvd_swa = _dup_heads(cache_swa_v).astype(BF16)
    state8 = jnp.pad(state_rglru, ((0, 0), (0, 0), (0, SUBLANES - 2), (0, 0)))
    zero_state = jnp.zeros((B_c, 1, SUBLANES, RNN_WIDTH), F32)
    pad8 = lambda a: jnp.pad(a, ((0, 0), (0, SUBLANES - a.shape[1]), (0, 0)))
    rg_cw8, rg_cb3, rg_lam8 = pad8(rg_conv_w), rg_conv_b[:, None, :], pad8(rg_lambda)
    rg_ba8, rg_bx8 = pad8(0.5 * rg_ba), pad8(0.5 * rg_bx)
    nat_bias = _nat_bias_table(nat_rpb)

    def layer(x, l, ctx_pass, caches=None):
        B, T = (B_c, T_c) if ctx_pass else (B_l, T_l)
        whole = (B * T) % TOKEN_TM == 0 and (T % TOKEN_TM == 0 or TOKEN_TM % T == 0)
        TM = TOKEN_TM if whole else min(T, TOKEN_TM)
        sub_tm = TM // max(1, TM // SUB_TM)
        per_seq = not ctx_pass
        if ctx_pass:
            xa, ga, nq, nk, nv, sq, sk, sv, xf, skd, svd = _pre_call(
                x, mod_all, per_seq, w_in_b, GATE_OFF, w_tail_ctx, l, plan_ctx, B, T, TM, None, caches)
        else:
            xa, ga, nq, nk, nv, sq, xf, skd, svd = _pre_call(
                x, mod_all, per_seq, w_in_b, sk0, w_tail_lat, l, plan_lat, B, T, TM, rope_tabs)
        r3 = lambda a: a.reshape(B, T, a.shape[-1])
        ya, hfin = _rglru_call(r3(xa), r3(ga), rg_cw8, rg_cb3, wa_bd, wx_bd, rg_ba8, rg_bx8, rg_lam8,
                               zero_state if ctx_pass else state8, l, 0 if ctx_pass else l)
        if ctx_pass:
            yb, yc = _attn_ctx_call(swa_sink, r3(nq), nk, nv, r3(sq), r3(skd), r3(svd), l)
        else:
            yb = _nat_lat_call(r3(nq), r3(nk), r3(nv), ck_nat, cv_nat, nat_bias, l)
            yc = _swa_lat_call(swa_sink, r3(sq), r3(skd), r3(svd), ckd_swa, cvd_swa, l)
        yd = _fourier_call(xf, T)
        f2 = lambda a: a.reshape(B * T, a.shape[-1])
        x1, hp, rt = _merge_call(
            x, mod_all, per_seq, (f2(ya), f2(yb), f2(yc), f2(yd)), w_gate, w_branch_b, w_out_b, l,
            ln_g, ln_b, wr_t, jnp.broadcast_to(router_bias[:, None], (N_EXPERTS, sub_tm)), T, TM)
        pos, wts, seg = _route_call(rt)
        x2 = _moe_call(pos, wts, seg, hp, w1_b, w3_b, w2_b, l, x1, mod_all, per_seq, ln_g, ln_b, T)
        new = ((nk, nv, sk, sv), hfin[:, :2]) if ctx_pass else None
        return x2, new

    y = x_prompt.reshape(B_c * T_c, D_MODEL)
    caches = tuple(jnp.zeros((B_c, DEPTH, T_c, w), F32)
                   for w in (W512, W512, SWA_KV_HEADS * HEAD_DIM, SWA_KV_HEADS * HEAD_DIM))
    states = []
    for l in range(DEPTH):
        y, (caches, st) = layer(y, l, True, caches)
        states.append(st)
    y_prompt = y.reshape(B_c, T_c, D_MODEL)
    new_nat_k = caches[0].reshape(B_c, DEPTH, T_c, NAT_HEADS, HEAD_DIM)
    new_nat_v = caches[1].reshape(B_c, DEPTH, T_c, NAT_HEADS, HEAD_DIM)
    new_swa_k = caches[2].reshape(B_c, DEPTH, T_c, SWA_KV_HEADS, HEAD_DIM)
    new_swa_v = caches[3].reshape(B_c, DEPTH, T_c, SWA_KV_HEADS, HEAD_DIM)
    new_state = jnp.stack(states, 1)

    y = x_sample.reshape(B_l * T_l, D_MODEL)
    for l in range(DEPTH):
        y, _ = layer(y, l, False)
    y_sample = y.reshape(B_l, T_l, D_MODEL)
    return (y_prompt, y_sample, new_nat_k, new_nat_v, new_swa_k, new_swa_v, new_state)
```
